```python
import math
import jax, jax.numpy as jnp
from jax import lax
import numpy as np

D_MODEL = 2048
BATCH = 8
SEQ = 8192
DEPTH = 1

HEAD_DIM = 128
FOX_HEADS = D_MODEL // (2 * HEAD_DIM)
GDN_HEADS = D_MODEL // (2 * HEAD_DIM)
FOX_W = FOX_HEADS * HEAD_DIM
GDN_W = GDN_HEADS * HEAD_DIM
MIX_W = FOX_W + GDN_W
CONV_W = 4
GDN_CHUNK = 64
Q_BLOCK = 128
D_FF = 256 * ((8 * D_MODEL // 3 + 255) // 256)
N_MOD = 9
MACARON_W = 0.5
EPS = 1e-6

_SIZES = [FOX_W, FOX_W, FOX_W, FOX_HEADS,
          3 * GDN_W, GDN_HEADS, GDN_HEADS, GDN_W]
IN_W = sum(_SIZES)
SPLIT_IDX = tuple(int(v) for v in np.cumsum(_SIZES)[:-1])

kernel_name = "hybrid_fox_gdn_macaron_adaln"


def rmsnorm(x, g):
    xf = x.astype(jnp.float32)
    y = xf * lax.rsqrt(jnp.mean(xf * xf, axis=-1, keepdims=True) + EPS)
    return y * g.astype(jnp.float32)


def ada_in(x, g, shift, scale):
    y = rmsnorm(x, g) * (1.0 + scale[:, None, :].astype(jnp.float32)) + shift[:, None, :].astype(jnp.float32)
    return y.astype(x.dtype)


def swiglu(h, wg, wu, wd):
    return (jax.nn.silu(h @ wg) * (h @ wu)) @ wd


def l2norm(t):
    return t * lax.rsqrt(jnp.sum(t * t, axis=-1, keepdims=True) + EPS)


def causal_dwconv(x, w):
    k = w.shape[0]
    return lax.conv_general_dilated(
        x, w[:, None, :].astype(x.dtype), window_strides=(1,), padding=[(k - 1, 0)],
        dimension_numbers=("NWC", "WIO", "NWC"), feature_group_count=x.shape[-1])


def forgetting_attention(q, k, v, logf):
    b, h, s, d = q.shape
    nb = s // Q_BLOCK
    scale = 1.0 / math.sqrt(d)
    cum = jnp.cumsum(logf, axis=-1)
    qb = jnp.moveaxis(q.reshape(b, h, nb, Q_BLOCK, d), 2, 0)
    cb = jnp.moveaxis(cum.reshape(b, h, nb, Q_BLOCK), 2, 0)
    kpos = jnp.arange(s)

    def one_block(args):
        i, q_i, c_i = args
        logits = jnp.einsum("bhqd,bhkd->bhqk", q_i, k) * scale + c_i[..., None] - cum[:, :, None, :]
        qpos = i * Q_BLOCK + jnp.arange(Q_BLOCK)
        mask = kpos[None, :] <= qpos[:, None]
        p = jax.nn.softmax(jnp.where(mask, logits, -jnp.inf), axis=-1)
        return jnp.einsum("bhqk,bhkd->bhqd", p, v)

    o = lax.map(one_block, (jnp.arange(nb), qb, cb))
    return jnp.moveaxis(o, 0, 2).reshape(b, h, s, d)


def gated_delta_rule(q, k, v, g, beta):
    b, h, s, dk = q.shape
    dv = v.shape[-1]
    c = GDN_CHUNK
    n = s // c
    q = q * (dk ** -0.5)
    kb = k * beta[..., None]
    vb = v * beta[..., None]
    resh = lambda t: t.reshape(b, h, n, c, *t.shape[3:])
    q, k, kb, vb, g = resh(q), resh(k), resh(kb), resh(vb), resh(g)
    g = jnp.cumsum(g, axis=-1)
    incl = jnp.tril(jnp.ones((c, c), dtype=bool))
    strict = jnp.tril(jnp.ones((c, c), dtype=bool), -1)
    decay = jnp.exp(jnp.where(incl, g[..., :, None] - g[..., None, :], -jnp.inf))
    lower = jnp.where(strict, jnp.einsum("bhnid,bhnjd->bhnij", kb, k) * decay, 0.0)
    eye = jnp.eye(c, dtype=q.dtype)
    t_inv = lax.linalg.triangular_solve(eye + lower, jnp.broadcast_to(eye, lower.shape),
                                        left_side=True, lower=True, unit_diagonal=True)
    u = t_inv @ vb
    w = t_inv @ (kb * jnp.exp(g)[..., None])
    a_intra = jnp.where(incl, jnp.einsum("bhnid,bhnjd->bhnij", q, k) * decay, 0.0)

    def step(state, inp):
        q_i, k_i, u_i, w_i, g_i, a_i = inp
        v_new = u_i - w_i @ state
        o = (q_i * jnp.exp(g_i)[..., None]) @ state + a_i @ v_new
        g_last = g_i[..., -1]
        state = state * jnp.exp(g_last)[..., None, None] + jnp.einsum(
            "bhcd,bhce->bhde", k_i * jnp.exp(g_last[..., None] - g_i)[..., None], v_new)
        return state, o

    mv = lambda t: jnp.moveaxis(t, 2, 0)
    state0 = jnp.zeros((b, h, dk, dv), dtype=q.dtype)
    _, o = lax.scan(step, state0, (mv(q), mv(k), mv(u), mv(w), mv(g), mv(a_intra)))
    return jnp.moveaxis(o, 0, 2).reshape(b, h, s, dv)


def hybrid_mixer(h, w_in, w_out, fox_f_bias, fox_out_norm, gdn_conv, gdn_A_log, gdn_dt_bias, gdn_out_norm):
    bsz, s, _ = h.shape
    proj = (h @ w_in).astype(jnp.float32)
    q_f, k_f, v_f, f_f, qkv_g, a_g, b_g, z_g = jnp.split(proj, SPLIT_IDX, axis=-1)
    heads = lambda t, nh: t.reshape(bsz, s, nh, HEAD_DIM).transpose(0, 2, 1, 3)

    logf = jax.nn.log_sigmoid(f_f + fox_f_bias.astype(jnp.float32)).transpose(0, 2, 1)
    o_f = forgetting_attention(heads(q_f, FOX_HEADS), heads(k_f, FOX_HEADS), heads(v_f, FOX_HEADS), logf)
    o_f = rmsnorm(o_f.transpose(0, 2, 1, 3), fox_out_norm).reshape(bsz, s, FOX_W)

    qkv_g = jax.nn.silu(causal_dwconv(qkv_g, gdn_conv.astype(jnp.float32)))
    q_g, k_g, v_g = jnp.split(qkv_g, 3, axis=-1)
    q_g = l2norm(heads(q_g, GDN_HEADS))
    k_g = l2norm(heads(k_g, GDN_HEADS))
    v_g = heads(v_g, GDN_HEADS)
    g_log = (-jnp.exp(gdn_A_log.astype(jnp.float32))
             * jax.nn.softplus(a_g + gdn_dt_bias.astype(jnp.float32))).transpose(0, 2, 1)
    beta = jax.nn.sigmoid(b_g).transpose(0, 2, 1)
    o_g = gated_delta_rule(q_g, k_g, v_g, g_log, beta).transpose(0, 2, 1, 3)
    z = z_g.reshape(bsz, s, GDN_HEADS, HEAD_DIM)
    o_g = (rmsnorm(o_g, gdn_out_norm) * jax.nn.silu(z)).reshape(bsz, s, GDN_W)

    o = jnp.concatenate([o_f, o_g], axis=-1).astype(h.dtype)
    return o @ w_out


def _fwd_setup_inputs(seed: int = 0) -> dict:
    key = jax.random.key(seed)
    ks = jax.random.split(key, 20)
    nrm = lambda k, shape, s: jax.random.normal(k, shape, jnp.float32) * s
    x = nrm(ks[0], (BATCH, SEQ, D_MODEL), 1.0)
    c = nrm(ks[1], (BATCH, D_MODEL), 1.0)
    ada_w = nrm(ks[2], (DEPTH, D_MODEL, N_MOD * D_MODEL), 0.5 * D_MODEL ** -0.5)
    ada_b = nrm(ks[3], (DEPTH, N_MOD * D_MODEL), 0.1)
    norm_g = 1.0 + nrm(ks[4], (DEPTH, 3, D_MODEL), 0.1)
    ffn_w_gate = nrm(ks[5], (DEPTH, 2, D_MODEL, D_FF), D_MODEL ** -0.5)
    ffn_w_up = nrm(ks[6], (DEPTH, 2, D_MODEL, D_FF), D_MODEL ** -0.5)
    ffn_w_down = nrm(ks[7], (DEPTH, 2, D_FF, D_MODEL), D_FF ** -0.5)
    w_in = nrm(ks[8], (DEPTH, D_MODEL, IN_W), D_MODEL ** -0.5)
    w_out = nrm(ks[9], (DEPTH, MIX_W, D_MODEL), MIX_W ** -0.5)
    fox_f_bias = 3.0 + nrm(ks[10], (DEPTH, FOX_HEADS), 0.5)
    fox_out_norm = 1.0 + nrm(ks[11], (DEPTH, HEAD_DIM), 0.1)
    gdn_conv = nrm(ks[12], (DEPTH, CONV_W, 3 * GDN_W), CONV_W ** -0.5)
    gdn_A_log = jnp.log(jax.random.uniform(ks[13], (DEPTH, GDN_HEADS), jnp.float32, 1.0, 16.0))
    dt = jnp.exp(jax.random.uniform(ks[14], (DEPTH, GDN_HEADS), jnp.float32,
                                    math.log(1e-3), math.log(1e-1)))
    gdn_dt_bias = dt + jnp.log(-jnp.expm1(-dt))
    gdn_out_norm = 1.0 + nrm(ks[15], (DEPTH, HEAD_DIM), 0.1)
    final_norm = 1.0 + nrm(ks[16], (D_MODEL,), 0.1)
    return {"x": x, "c": c, "ada_w": ada_w, "ada_b": ada_b, "norm_g": norm_g,
            "ffn_w_gate": ffn_w_gate, "ffn_w_up": ffn_w_up, "ffn_w_down": ffn_w_down,
            "w_in": w_in, "w_out": w_out, "fox_f_bias": fox_f_bias, "fox_out_norm": fox_out_norm,
            "gdn_conv": gdn_conv, "gdn_A_log": gdn_A_log, "gdn_dt_bias": gdn_dt_bias,
            "gdn_out_norm": gdn_out_norm, "final_norm": final_norm}


def _fwd_reference(x, c, ada_w, ada_b, norm_g, ffn_w_gate, ffn_w_up, ffn_w_down, w_in, w_out,
              fox_f_bias, fox_out_norm, gdn_conv, gdn_A_log, gdn_dt_bias, gdn_out_norm, final_norm):
    cond = jax.nn.silu(c)
    for l in range(DEPTH):
        mod = cond @ ada_w[l] + ada_b[l]
        sh1, sc1, gt1, sh2, sc2, gt2, sh3, sc3, gt3 = jnp.split(mod, N_MOD, axis=-1)
        h = ada_in(x, norm_g[l, 0], sh1, sc1)
        x = x + MACARON_W * gt1[:, None, :] * swiglu(h, ffn_w_gate[l, 0], ffn_w_up[l, 0], ffn_w_down[l, 0])
        h = ada_in(x, norm_g[l, 1], sh2, sc2)
        x = x + gt2[:, None, :] * hybrid_mixer(h, w_in[l], w_out[l], fox_f_bias[l], fox_out_norm[l],
                                               gdn_conv[l], gdn_A_log[l], gdn_dt_bias[l], gdn_out_norm[l])
        h = ada_in(x, norm_g[l, 2], sh3, sc3)
        x = x + MACARON_W * gt3[:, None, :] * swiglu(h, ffn_w_gate[l, 1], ffn_w_up[l, 1], ffn_w_down[l, 1])
    return rmsnorm(x, final_norm).astype(x.dtype)


import jax as _jax
import jax.numpy as _jnp

TWIN_FORMAT = 'train_step'
FWD_PARAMS = ['x', 'c', 'ada_w', 'ada_b', 'norm_g', 'ffn_w_gate', 'ffn_w_up', 'ffn_w_down', 'w_in', 'w_out', 'fox_f_bias', 'fox_out_norm', 'gdn_conv', 'gdn_A_log', 'gdn_dt_bias', 'gdn_out_norm', 'final_norm']
TWIN_WEIGHTS = ['ada_w', 'ada_b', 'norm_g', 'ffn_w_gate', 'ffn_w_up', 'ffn_w_down', 'w_in', 'w_out', 'fox_f_bias', 'fox_out_norm', 'gdn_conv', 'gdn_A_log', 'gdn_dt_bias', 'gdn_out_norm', 'final_norm']
TWIN_DIFF_INPUT = 'x'
TWIN_INPUTS = ['x', 'c', 'ada_w', 'ada_b', 'norm_g', 'ffn_w_gate', 'ffn_w_up', 'ffn_w_down', 'w_in', 'w_out', 'fox_f_bias', 'fox_out_norm', 'gdn_conv', 'gdn_A_log', 'gdn_dt_bias', 'gdn_out_norm', 'final_norm', 'loss_target', 'm_ada_w', 'm_ada_b', 'm_norm_g', 'm_ffn_w_gate', 'm_ffn_w_up', 'm_ffn_w_down', 'm_w_in', 'm_w_out', 'm_fox_f_bias', 'm_fox_out_norm', 'm_gdn_conv', 'm_gdn_A_log', 'm_gdn_dt_bias', 'm_gdn_out_norm', 'm_final_norm', 'v_ada_w', 'v_ada_b', 'v_norm_g', 'v_ffn_w_gate', 'v_ffn_w_up', 'v_ffn_w_down', 'v_w_in', 'v_w_out', 'v_fox_f_bias', 'v_fox_out_norm', 'v_gdn_conv', 'v_gdn_A_log', 'v_gdn_dt_bias', 'v_gdn_out_norm', 'v_final_norm']
TWIN_OUTPUTS = ['loss', 'grad_x', 'grad_ada_w', 'grad_ada_b', 'grad_norm_g', 'grad_ffn_w_gate', 'grad_ffn_w_up', 'grad_ffn_w_down', 'grad_w_in', 'grad_w_out', 'grad_fox_f_bias', 'grad_fox_out_norm', 'grad_gdn_conv', 'grad_gdn_A_log', 'grad_gdn_dt_bias', 'grad_gdn_out_norm', 'grad_final_norm', 'delta_ada_w', 'delta_ada_b', 'delta_norm_g', 'delta_ffn_w_gate', 'delta_ffn_w_up', 'delta_ffn_w_down', 'delta_w_in', 'delta_w_out', 'delta_fox_f_bias', 'delta_fox_out_norm', 'delta_gdn_conv', 'delta_gdn_A_log', 'delta_gdn_dt_bias', 'delta_gdn_out_norm', 'delta_final_norm', 'new_m_ada_w', 'new_m_ada_b', 'new_m_norm_g', 'new_m_ffn_w_gate', 'new_m_ffn_w_up', 'new_m_ffn_w_down', 'new_m_w_in', 'new_m_w_out', 'new_m_fox_f_bias', 'new_m_fox_out_norm', 'new_m_gdn_conv', 'new_m_gdn_A_log', 'new_m_gdn_dt_bias', 'new_m_gdn_out_norm', 'new_m_final_norm', 'new_v_ada_w', 'new_v_ada_b', 'new_v_norm_g', 'new_v_ffn_w_gate', 'new_v_ffn_w_up', 'new_v_ffn_w_down', 'new_v_w_in', 'new_v_w_out', 'new_v_fox_f_bias', 'new_v_fox_out_norm', 'new_v_gdn_conv', 'new_v_gdn_A_log', 'new_v_gdn_dt_bias', 'new_v_gdn_out_norm', 'new_v_final_norm']
TWIN_LEAF_KINDS = {'loss': 'loss', 'grad_x': 'grad_x', 'grad_ada_w': 'grad_w', 'grad_ada_b': 'grad_w', 'grad_norm_g': 'grad_w', 'grad_ffn_w_gate': 'grad_w', 'grad_ffn_w_up': 'grad_w', 'grad_ffn_w_down': 'grad_w', 'grad_w_in': 'grad_w', 'grad_w_out': 'grad_w', 'grad_fox_f_bias': 'grad_w', 'grad_fox_out_norm': 'grad_w', 'grad_gdn_conv': 'grad_w', 'grad_gdn_A_log': 'grad_w', 'grad_gdn_dt_bias': 'grad_w', 'grad_gdn_out_norm': 'grad_w', 'grad_final_norm': 'grad_w', 'delta_ada_w': 'delta_w', 'delta_ada_b': 'delta_w', 'delta_norm_g': 'delta_w', 'delta_ffn_w_gate': 'delta_w', 'delta_ffn_w_up': 'delta_w', 'delta_ffn_w_down': 'delta_w', 'delta_w_in': 'delta_w', 'delta_w_out': 'delta_w', 'delta_fox_f_bias': 'delta_w', 'delta_fox_out_norm': 'delta_w', 'delta_gdn_conv': 'delta_w', 'delta_gdn_A_log': 'delta_w', 'delta_gdn_dt_bias': 'delta_w', 'delta_gdn_out_norm': 'delta_w', 'delta_final_norm': 'delta_w', 'new_m_ada_w': 'new_m', 'new_m_ada_b': 'new_m', 'new_m_norm_g': 'new_m', 'new_m_ffn_w_gate': 'new_m', 'new_m_ffn_w_up': 'new_m', 'new_m_ffn_w_down': 'new_m', 'new_m_w_in': 'new_m', 'new_m_w_out': 'new_m', 'new_m_fox_f_bias': 'new_m', 'new_m_fox_out_norm': 'new_m', 'new_m_gdn_conv': 'new_m', 'new_m_gdn_A_log': 'new_m', 'new_m_gdn_dt_bias': 'new_m', 'new_m_gdn_out_norm': 'new_m', 'new_m_final_norm': 'new_m', 'new_v_ada_w': 'new_v', 'new_v_ada_b': 'new_v', 'new_v_norm_g': 'new_v', 'new_v_ffn_w_gate': 'new_v', 'new_v_ffn_w_up': 'new_v', 'new_v_ffn_w_down': 'new_v', 'new_v_w_in': 'new_v', 'new_v_w_out': 'new_v', 'new_v_fox_f_bias': 'new_v', 'new_v_fox_out_norm': 'new_v', 'new_v_gdn_conv': 'new_v', 'new_v_gdn_A_log': 'new_v', 'new_v_gdn_dt_bias': 'new_v', 'new_v_gdn_out_norm': 'new_v', 'new_v_final_norm': 'new_v'}


def _forward(args):
    return _fwd_reference(*[args[k] for k in FWD_PARAMS])


def _output_shape():
    def fwd():
        inp = _fwd_setup_inputs(0)
        return _fwd_reference(*[inp[k] for k in FWD_PARAMS])
    out = _jax.eval_shape(fwd)
    return out.shape, out.dtype

N_MICROBATCH = 1
ADAM_LR = 0.001
ADAM_B1 = 0.9
ADAM_B2 = 0.999
ADAM_EPS = 1e-08
ADAM_WD = 0.01
ADAM_STEP = 10
PER_EXAMPLE_BATCH_AXIS = {'x': 0, 'c': 0, 'loss_target': 0}
SHARED_INPUTS = []
_WEIGHT_DTYPES = {'ada_w': _jnp.float32, 'ada_b': _jnp.float32, 'norm_g': _jnp.float32, 'ffn_w_gate': _jnp.float32, 'ffn_w_up': _jnp.float32, 'ffn_w_down': _jnp.float32, 'w_in': _jnp.float32, 'w_out': _jnp.float32, 'fox_f_bias': _jnp.float32, 'fox_out_norm': _jnp.float32, 'gdn_conv': _jnp.float32, 'gdn_A_log': _jnp.float32, 'gdn_dt_bias': _jnp.float32, 'gdn_out_norm': _jnp.float32, 'final_norm': _jnp.float32}
MOMENT_SCALE = {'ada_w': 9.838970e-02, 'ada_b': 2.280943e-01, 'norm_g': 3.058498e-02, 'ffn_w_gate': 9.295050e-03, 'ffn_w_up': 9.203250e-03, 'ffn_w_down': 1.528719e-02, 'w_in': 4.134547e-02, 'w_out': 7.610859e-02, 'fox_f_bias': 2.167285e-01, 'fox_out_norm': 3.057534e-01, 'gdn_conv': 2.347673e-02, 'gdn_A_log': 1.446096e-01, 'gdn_dt_bias': 1.383370e-01, 'gdn_out_norm': 1.263280e-01, 'final_norm': 3.220356e+01}


def _to_microbatches(a, axis):
    t = _jnp.moveaxis(a, axis, 0)
    t = t.reshape((N_MICROBATCH, t.shape[0] // N_MICROBATCH) + t.shape[1:])
    return _jnp.moveaxis(t, 1, axis + 1)


def setup_inputs(seed: int = 0) -> dict:
    inp = _fwd_setup_inputs(seed)
    key = _jax.random.fold_in(_jax.random.key(seed), 7919)
    shape, _ = _output_shape()
    out = dict(inp)
    out["loss_target"] = _jax.random.normal(_jax.random.fold_in(key, 0), shape, _jnp.float32)
    for i, name in enumerate(TWIN_WEIGHTS):
        w = inp[name].astype(_jnp.float32)
        if MOMENT_SCALE is None:
            s = _jnp.sqrt(_jnp.mean(_jnp.square(w)) + 1e-30)
        else:
            s = MOMENT_SCALE[name]
        km, kv = _jax.random.split(_jax.random.fold_in(key, i + 1))
        out[name] = w
        out["m_" + name] = s * _jax.random.normal(km, w.shape, _jnp.float32)
        out["v_" + name] = (s * s) * _jax.random.uniform(kv, w.shape, _jnp.float32, 0.5, 1.5)
    if N_MICROBATCH > 1:
        for name, axis in PER_EXAMPLE_BATCH_AXIS.items():
            out[name] = _to_microbatches(out[name], axis)
    return {'x': out['x'], 'c': out['c'], 'ada_w': out['ada_w'], 'ada_b': out['ada_b'], 'norm_g': out['norm_g'], 'ffn_w_gate': out['ffn_w_gate'], 'ffn_w_up': out['ffn_w_up'], 'ffn_w_down': out['ffn_w_down'], 'w_in': out['w_in'], 'w_out': out['w_out'], 'fox_f_bias': out['fox_f_bias'], 'fox_out_norm': out['fox_out_norm'], 'gdn_conv': out['gdn_conv'], 'gdn_A_log': out['gdn_A_log'], 'gdn_dt_bias': out['gdn_dt_bias'], 'gdn_out_norm': out['gdn_out_norm'], 'final_norm': out['final_norm'], 'loss_target': out['loss_target'], 'm_ada_w': out['m_ada_w'], 'm_ada_b': out['m_ada_b'], 'm_norm_g': out['m_norm_g'], 'm_ffn_w_gate': out['m_ffn_w_gate'], 'm_ffn_w_up': out['m_ffn_w_up'], 'm_ffn_w_down': out['m_ffn_w_down'], 'm_w_in': out['m_w_in'], 'm_w_out': out['m_w_out'], 'm_fox_f_bias': out['m_fox_f_bias'], 'm_fox_out_norm': out['m_fox_out_norm'], 'm_gdn_conv': out['m_gdn_conv'], 'm_gdn_A_log': out['m_gdn_A_log'], 'm_gdn_dt_bias': out['m_gdn_dt_bias'], 'm_gdn_out_norm': out['m_gdn_out_norm'], 'm_final_norm': out['m_final_norm'], 'v_ada_w': out['v_ada_w'], 'v_ada_b': out['v_ada_b'], 'v_norm_g': out['v_norm_g'], 'v_ffn_w_gate': out['v_ffn_w_gate'], 'v_ffn_w_up': out['v_ffn_w_up'], 'v_ffn_w_down': out['v_ffn_w_down'], 'v_w_in': out['v_w_in'], 'v_w_out': out['v_w_out'], 'v_fox_f_bias': out['v_fox_f_bias'], 'v_fox_out_norm': out['v_fox_out_norm'], 'v_gdn_conv': out['v_gdn_conv'], 'v_gdn_A_log': out['v_gdn_A_log'], 'v_gdn_dt_bias': out['v_gdn_dt_bias'], 'v_gdn_out_norm': out['v_gdn_out_norm'], 'v_final_norm': out['v_final_norm']}


def _loss(weights, diff, rest, loss_target):
    with _jax.named_scope("forward"):
        args = {**rest, TWIN_DIFF_INPUT: diff, **{k: w.astype(_WEIGHT_DTYPES[k]) for k, w in weights.items()}}
        y = _forward(args)
    with _jax.named_scope("loss_head"):
        err = _jnp.square(y.astype(_jnp.float32) - loss_target)
        return 0.5 * _jnp.sum(_jnp.mean(err, axis=-1)) if err.ndim else 0.5 * err


def _adamw(w, g, m, v):
    m = ADAM_B1 * m + (1.0 - ADAM_B1) * g
    v = ADAM_B2 * v + (1.0 - ADAM_B2) * _jnp.square(g)
    m_hat = m / (1.0 - ADAM_B1 ** ADAM_STEP)
    v_hat = v / (1.0 - ADAM_B2 ** ADAM_STEP)
    delta = -ADAM_LR * (m_hat / (_jnp.sqrt(v_hat) + ADAM_EPS) + ADAM_WD * w)
    return delta, m, v


def reference(x, c, ada_w, ada_b, norm_g, ffn_w_gate, ffn_w_up, ffn_w_down, w_in, w_out, fox_f_bias, fox_out_norm, gdn_conv, gdn_A_log, gdn_dt_bias, gdn_out_norm, final_norm, loss_target, m_ada_w, m_ada_b, m_norm_g, m_ffn_w_gate, m_ffn_w_up, m_ffn_w_down, m_w_in, m_w_out, m_fox_f_bias, m_fox_out_norm, m_gdn_conv, m_gdn_A_log, m_gdn_dt_bias, m_gdn_out_norm, m_final_norm, v_ada_w, v_ada_b, v_norm_g, v_ffn_w_gate, v_ffn_w_up, v_ffn_w_down, v_w_in, v_w_out, v_fox_f_bias, v_fox_out_norm, v_gdn_conv, v_gdn_A_log, v_gdn_dt_bias, v_gdn_out_norm, v_final_norm):
    given = dict(x=x, c=c, ada_w=ada_w, ada_b=ada_b, norm_g=norm_g, ffn_w_gate=ffn_w_gate, ffn_w_up=ffn_w_up, ffn_w_down=ffn_w_down, w_in=w_in, w_out=w_out, fox_f_bias=fox_f_bias, fox_out_norm=fox_out_norm, gdn_conv=gdn_conv, gdn_A_log=gdn_A_log, gdn_dt_bias=gdn_dt_bias, gdn_out_norm=gdn_out_norm, final_norm=final_norm, loss_target=loss_target, m_ada_w=m_ada_w, m_ada_b=m_ada_b, m_norm_g=m_norm_g, m_ffn_w_gate=m_ffn_w_gate, m_ffn_w_up=m_ffn_w_up, m_ffn_w_down=m_ffn_w_down, m_w_in=m_w_in, m_w_out=m_w_out, m_fox_f_bias=m_fox_f_bias, m_fox_out_norm=m_fox_out_norm, m_gdn_conv=m_gdn_conv, m_gdn_A_log=m_gdn_A_log, m_gdn_dt_bias=m_gdn_dt_bias, m_gdn_out_norm=m_gdn_out_norm, m_final_norm=m_final_norm, v_ada_w=v_ada_w, v_ada_b=v_ada_b, v_norm_g=v_norm_g, v_ffn_w_gate=v_ffn_w_gate, v_ffn_w_up=v_ffn_w_up, v_ffn_w_down=v_ffn_w_down, v_w_in=v_w_in, v_w_out=v_w_out, v_fox_f_bias=v_fox_f_bias, v_fox_out_norm=v_fox_out_norm, v_gdn_conv=v_gdn_conv, v_gdn_A_log=v_gdn_A_log, v_gdn_dt_bias=v_gdn_dt_bias, v_gdn_out_norm=v_gdn_out_norm, v_final_norm=v_final_norm)
    weights = {n: given[n] for n in TWIN_WEIGHTS}
    shared = {n: given[n] for n in SHARED_INPUTS}
    per_example = {n: given[n] for n in ['x', 'c']}
    grad_fn = _jax.value_and_grad(_loss, argnums=(0, 1))

    def one_microbatch(ex, loss_target):
        ex = dict(ex)
        diff = ex.pop(TWIN_DIFF_INPUT)
        return grad_fn(weights, diff, {**shared, **ex}, loss_target)

    if N_MICROBATCH == 1:
        loss, (grad_w, grad_x) = one_microbatch(per_example, given["loss_target"])
    else:
        def body(carry, xs):
            loss_sum, grad_sum = carry
            l_k, (gw_k, gx_k) = one_microbatch(xs[0], xs[1])
            with _jax.named_scope("update"):
                return (loss_sum + l_k, _jax.tree.map(_jnp.add, grad_sum, gw_k)), gx_k

        init = (_jnp.zeros((), _jnp.float32), _jax.tree.map(_jnp.zeros_like, weights))
        (loss, grad_w), grad_x = _jax.lax.scan(body, init, (per_example, given["loss_target"]))
    with _jax.named_scope("update"):
        delta_w, new_m, new_v = {}, {}, {}
        for n in TWIN_WEIGHTS:
            delta_w[n], new_m[n], new_v[n] = _adamw(weights[n], grad_w[n], given["m_" + n], given["v_" + n])
    return (loss, grad_x, *[grad_w[n] for n in TWIN_WEIGHTS], *[delta_w[n] for n in TWIN_WEIGHTS],
            *[new_m[n] for n in TWIN_WEIGHTS], *[new_v[n] for n in TWIN_WEIGHTS])
```

```python
import functools
import math

import jax
import jax.numpy as jnp
from jax import lax
from jax.experimental import pallas as pl
from jax.experimental.pallas import tpu as pltpu

F32 = jnp.float32
BF16 = jnp.bfloat16

N_DEV = 8
MESH_AXES = ("x", "y", "c")
LANES = 128
HEAD_DIM = 128
GDN_CHUNK = 64
CONV_W = 4
N_MOD = 9
MACARON_W = 0.5
EPS = 1e-6
NEG = -1e30
VMEM_LIMIT_BYTES = 56 * 2 ** 20
ADAM_BLOCK_BYTES = 4 * 2 ** 20

ADAM_LR = 0.001
ADAM_B1 = 0.9
ADAM_B2 = 0.999
ADAM_EPS = 1e-08
ADAM_WD = 0.01
ADAM_STEP = 10

MESH_ID = pl.DeviceIdType.MESH
ANY = pl.BlockSpec(memory_space=pl.ANY)
VMEM = pl.BlockSpec(memory_space=pltpu.VMEM)


def _pcall(body, **kw):
    return pl.pallas_call(body, **kw)


def _params(*semantics):
    return pltpu.CompilerParams(dimension_semantics=semantics, vmem_limit_bytes=VMEM_LIMIT_BYTES)


def _tile(n, pref):
    t = 1 << (max(1, min(n, pref)).bit_length() - 1)
    while n % t:
        t //= 2
    return t if t % 8 == 0 else n


def _sigmoid(x):
    return 1.0 / (1.0 + jnp.exp(-x))


def _dot(a, b, dims):
    return lax.dot_general(a.astype(BF16), b.astype(BF16), (dims, ((), ())), preferred_element_type=F32)


NN = ((1,), (0,))
NT = ((1,), (1,))
TN = ((0,), (0,))


def _split3(x):
    hi = x.astype(BF16)
    r1 = x - hi.astype(F32)
    mid = r1.astype(BF16)
    lo = (r1 - mid.astype(F32)).astype(BF16)
    return hi, mid, lo


def _dot_exact_lhs(m_bf16, x, dims=NN):
    hi, mid, lo = _split3(x)
    d = lambda p: lax.dot_general(m_bf16, p, (dims, ((), ())), preferred_element_type=F32)
    return d(hi) + (d(mid) + d(lo))


def _dot_hp(a, b, dims):
    ah = a.astype(BF16)
    al = (a - ah.astype(F32)).astype(BF16)
    bh = b.astype(BF16)
    bl = (b - bh.astype(F32)).astype(BF16)
    d = lambda p, q: lax.dot_general(p, q, (dims, ((), ())), preferred_element_type=F32)
    return d(ah, bh) + (d(ah, bl) + d(al, bh))


def _mesh_pos():
    return lax.axis_index("x"), lax.axis_index("y"), lax.axis_index("c")


def _peer(pos, mask):
    x, y, c = pos
    return (1 - x if mask & 4 else x, 1 - y if mask & 2 else y, 1 - c if mask & 1 else c)


def _linear(pos):
    return 4 * pos[0] + 2 * pos[1] + pos[2]


def _exchange_body(n, scatter):
    def body(*refs):
        ins, outs = refs[:n], refs[n:2 * n]
        send_sems, recv_sems, local_sems = refs[2 * n:]
        pos = _mesh_pos()
        me = _linear(pos)
        local = []
        for i in range(n):
            src = ins[i].at[me] if scatter else ins[i]
            cp = pltpu.make_async_copy(src, outs[i].at[me], local_sems.at[i])
            cp.start()
            local.append(cp)
        sends = []
        for mask in range(1, N_DEV):
            peer = _peer(pos, mask)
            for i in range(n):
                src = ins[i].at[_linear(peer)] if scatter else ins[i]
                cp = pltpu.make_async_remote_copy(
                    src_ref=src, dst_ref=outs[i].at[me],
                    send_sem=send_sems.at[i, mask - 1], recv_sem=recv_sems.at[i, mask - 1],
                    device_id=peer, device_id_type=MESH_ID)
                cp.start()
                sends.append(cp)
        for mask in range(1, N_DEV):
            peer = _peer(pos, mask)
            for i in range(n):
                src = ins[i].at[me] if scatter else ins[i]
                pltpu.make_async_remote_copy(
                    src_ref=src, dst_ref=outs[i].at[_linear(peer)],
                    send_sem=send_sems.at[i, mask - 1], recv_sem=recv_sems.at[i, mask - 1],
                    device_id=peer, device_id_type=MESH_ID).wait_recv()
        for cp in sends:
            cp.wait_send()
        for cp in local:
            cp.wait()

    return body


def _exchange(arrays, *, scatter, in_vmem, name):
    n = len(arrays)
    shapes = [a.shape if scatter else (N_DEV,) + a.shape for a in arrays]
    spec = VMEM if in_vmem else ANY
    outs = _pcall(
        _exchange_body(n, scatter), name=name,
        out_shape=[jax.ShapeDtypeStruct(s, a.dtype) for s, a in zip(shapes, arrays)],
        in_specs=[spec] * n, out_specs=[spec] * n,
        scratch_shapes=[pltpu.SemaphoreType.DMA((n, N_DEV - 1)), pltpu.SemaphoreType.DMA((n, N_DEV - 1)),
                        pltpu.SemaphoreType.DMA((n,))],
    )(*arrays)
    return list(outs)


def _gather_row(v, name):
    return _exchange([v], scatter=False, in_vmem=True, name=name)[0].reshape(N_DEV, v.shape[1])


def _ada_fwd(c_all, w, b):
    d, n = w.shape
    tn = _tile(n, 256)

    def body(c_ref, w_ref, b_ref, o_ref):
        cv = c_ref[...]
        cond = cv * _sigmoid(cv)
        o_ref[...] = _dot_hp(cond, w_ref[...], NN) + b_ref[...]

    return _pcall(
        body, name="ada_fwd", grid=(n // tn,),
        in_specs=[pl.BlockSpec((N_DEV, d), lambda j: (0, 0)), pl.BlockSpec((d, tn), lambda j: (0, j)),
                  pl.BlockSpec((1, tn), lambda j: (0, j))],
        out_specs=pl.BlockSpec((N_DEV, tn), lambda j: (0, j)),
        out_shape=jax.ShapeDtypeStruct((N_DEV, n), F32), compiler_params=_params("parallel"),
    )(c_all, w, b)


def _ada_bwd(ct_pad, dmod_pad):
    d = ct_pad.shape[0]
    n = dmod_pad.shape[1]
    tn = _tile(n, 256)

    def body(c_ref, g_ref, o_ref):
        cv = c_ref[...]
        cond = cv * _sigmoid(cv)
        o_ref[...] = _dot_hp(cond, g_ref[...], NN)

    return _pcall(
        body, name="ada_bwd", grid=(n // tn,),
        in_specs=[pl.BlockSpec((d, LANES), lambda j: (0, 0)), pl.BlockSpec((LANES, tn), lambda j: (0, j))],
        out_specs=pl.BlockSpec((d, tn), lambda j: (0, j)),
        out_shape=jax.ShapeDtypeStruct((d, n), F32), compiler_params=_params("parallel"),
    )(ct_pad, dmod_pad)


def _norm_mod(x, g, sc, sh, name):
    s, d = x.shape
    ts = _tile(s, 512)

    def body(x_ref, g_ref, sc_ref, sh_ref, h_ref):
        xv = x_ref[...]
        r = lax.rsqrt(jnp.mean(xv * xv, axis=-1, keepdims=True) + EPS)
        h_ref[...] = (xv * r * g_ref[...] * (1.0 + sc_ref[...]) + sh_ref[...]).astype(BF16)

    row = pl.BlockSpec((1, d), lambda i: (0, 0))
    return _pcall(
        body, name=name, grid=(s // ts,),
        in_specs=[pl.BlockSpec((ts, d), lambda i: (i, 0)), row, row, row],
        out_specs=pl.BlockSpec((ts, d), lambda i: (i, 0)),
        out_shape=jax.ShapeDtypeStruct((s, d), BF16), compiler_params=_params("parallel"),
    )(x, g, sc, sh)


def _norm_mod_bwd(x, dh, dx_out, g, sc, name):
    s, d = x.shape
    ts = _tile(s, 512)

    def body(x_ref, dh_ref, dxo_ref, g_ref, sc_ref, dx_ref, dsh_ref, dsc_ref, dg_ref):
        @pl.when(pl.program_id(0) == 0)
        def _():
            dsh_ref[...] = jnp.zeros_like(dsh_ref)
            dsc_ref[...] = jnp.zeros_like(dsc_ref)
            dg_ref[...] = jnp.zeros_like(dg_ref)

        xv = x_ref[...]
        dh_v = dh_ref[...]
        gv = g_ref[...]
        one_sc = 1.0 + sc_ref[...]
        r = lax.rsqrt(jnp.mean(xv * xv, axis=-1, keepdims=True) + EPS)
        xn = xv * r
        dxn = dh_v * (gv * one_sc)
        dx_ref[...] = dxo_ref[...] + r * (dxn - xn * jnp.mean(dxn * xn, axis=-1, keepdims=True))
        t = dh_v * xn
        dsh_ref[...] += jnp.sum(dh_v, axis=0, keepdims=True)
        dsc_ref[...] += jnp.sum(t * gv, axis=0, keepdims=True)
        dg_ref[...] += jnp.sum(t * one_sc, axis=0, keepdims=True)

    blk = pl.BlockSpec((ts, d), lambda i: (i, 0))
    row = pl.BlockSpec((1, d), lambda i: (0, 0))
    return _pcall(
        body, name=name, grid=(s // ts,),
        in_specs=[blk, blk, blk, row, row], out_specs=[blk, row, row, row],
        out_shape=[jax.ShapeDtypeStruct((s, d), F32)] + [jax.ShapeDtypeStruct((1, d), F32)] * 3,
        compiler_params=_params("arbitrary"),
    )(x, dh, dx_out, g, sc)


def _gate_bwd(dx, f, gt, k, name):
    s, d = dx.shape
    ts = _tile(s, 512)

    def body(dx_ref, f_ref, gt_ref, df_ref, dgt_ref):
        @pl.when(pl.program_id(0) == 0)
        def _():
            dgt_ref[...] = jnp.zeros_like(dgt_ref)

        dxv = dx_ref[...]
        df_ref[...] = ((k * gt_ref[...]) * dxv).astype(BF16)
        dgt_ref[...] += k * jnp.sum(f_ref[...] * dxv, axis=0, keepdims=True)

    blk = pl.BlockSpec((ts, d), lambda i: (i, 0))
    row = pl.BlockSpec((1, d), lambda i: (0, 0))
    return _pcall(
        body, name=name, grid=(s // ts,),
        in_specs=[blk, blk, row], out_specs=[blk, row],
        out_shape=[jax.ShapeDtypeStruct((s, d), BF16), jax.ShapeDtypeStruct((1, d), F32)],
        compiler_params=_params("arbitrary"),
    )(dx, f, gt)


def _ffn_up(h, wg, wu, layer, name):
    s, d = h.shape
    fs = wg.shape[-1]
    tm = _tile(s, 512)

    def body(h_ref, wg_ref, wu_ref, a_ref, b_ref, s_ref):
        hv = h_ref[...]
        a = jnp.dot(hv, wg_ref[...], preferred_element_type=F32)
        b = jnp.dot(hv, wu_ref[...], preferred_element_type=F32)
        a_ref[...] = a
        b_ref[...] = b
        s_ref[...] = (a * _sigmoid(a) * b).astype(BF16)

    wspec = pl.BlockSpec((None, None, d, fs), lambda j, m: (j, layer, 0, 0))
    ospec = pl.BlockSpec((None, tm, fs), lambda j, m: (j, m, 0))
    return _pcall(
        body, name=name, grid=(N_DEV, s // tm),
        in_specs=[pl.BlockSpec((tm, d), lambda j, m: (m, 0)), wspec, wspec],
        out_specs=[ospec, ospec, ospec],
        out_shape=[jax.ShapeDtypeStruct((N_DEV, s, fs), F32)] * 2 + [jax.ShapeDtypeStruct((N_DEV, s, fs), BF16)],
        compiler_params=_params("parallel", "parallel"),
    )(h, wg, wu)


def _ffn_down(sv, wd, layer, x_in, gt, name):
    _, s, fs = sv.shape
    d = wd.shape[-1]
    tm = _tile(s, 512)

    def body(s_ref, wd_ref, x_ref, gt_ref, f_ref, xo_ref, acc):
        j = pl.program_id(1)

        @pl.when(j == 0)
        def _():
            acc[...] = jnp.zeros_like(acc)

        acc[...] += jnp.dot(s_ref[...], wd_ref[...], preferred_element_type=F32)

        @pl.when(j == N_DEV - 1)
        def _():
            fv = acc[...]
            f_ref[...] = fv
            xo_ref[...] = x_ref[...] + (MACARON_W * gt_ref[...]) * fv

    blk = pl.BlockSpec((tm, d), lambda m, j: (m, 0))
    return _pcall(
        body, name=name, grid=(s // tm, N_DEV),
        in_specs=[pl.BlockSpec((None, tm, fs), lambda m, j: (j, m, 0)),
                  pl.BlockSpec((None, None, fs, d), lambda m, j: (j, layer, 0, 0)),
                  blk, pl.BlockSpec((1, d), lambda m, j: (0, 0))],
        out_specs=[blk, blk],
        out_shape=[jax.ShapeDtypeStruct((s, d), F32)] * 2,
        scratch_shapes=[pltpu.VMEM((tm, d), F32)],
        compiler_params=_params("parallel", "arbitrary"),
    )(sv, wd, x_in, gt)


def _ffn_bwd_act(df, wd, layer, a, b, name):
    s, d = df.shape
    fs = a.shape[-1]
    tm = _tile(s, 512)

    def body(df_ref, wd_ref, a_ref, b_ref, da_ref, db_ref):
        ds = lax.dot_general(df_ref[...], wd_ref[...], (NT, ((), ())), preferred_element_type=F32)
        av = a_ref[...]
        sg = _sigmoid(av)
        da_ref[...] = (ds * b_ref[...] * (sg * (1.0 + av * (1.0 - sg)))).astype(BF16)
        db_ref[...] = (ds * (av * sg)).astype(BF16)

    hid = pl.BlockSpec((None, tm, fs), lambda j, m: (j, m, 0))
    return _pcall(
        body, name=name, grid=(N_DEV, s // tm),
        in_specs=[pl.BlockSpec((tm, d), lambda j, m: (m, 0)),
                  pl.BlockSpec((None, None, fs, d), lambda j, m: (j, layer, 0, 0)), hid, hid],
        out_specs=[hid, hid],
        out_shape=[jax.ShapeDtypeStruct((N_DEV, s, fs), BF16)] * 2,
        compiler_params=_params("parallel", "parallel"),
    )(df, wd, a, b)


def _ffn_bwd_wd(sv, df, name):
    _, s, fs = sv.shape
    d = df.shape[1]
    tk = _tile(s, 512)

    def body(s_ref, df_ref, o_ref):
        @pl.when(pl.program_id(1) == 0)
        def _():
            o_ref[...] = jnp.zeros_like(o_ref)

        o_ref[...] += lax.dot_general(s_ref[...], df_ref[...], (TN, ((), ())), preferred_element_type=F32)

    return _pcall(
        body, name=name, grid=(N_DEV, s // tk),
        in_specs=[pl.BlockSpec((None, tk, fs), lambda j, k: (j, k, 0)), pl.BlockSpec((tk, d), lambda j, k: (k, 0))],
        out_specs=pl.BlockSpec((None, fs, d), lambda j, k: (j, 0, 0)),
        out_shape=jax.ShapeDtypeStruct((N_DEV, fs, d), F32),
        compiler_params=_params("parallel", "arbitrary"),
    )(sv, df)


def _ffn_bwd_h(da, db, wg, wu, layer, name):
    _, s, fs = da.shape
    d = wg.shape[-2]
    tm = _tile(s, 1024)

    def body(da_ref, db_ref, wg_ref, wu_ref, o_ref, acc):
        j = pl.program_id(1)

        @pl.when(j == 0)
        def _():
            acc[...] = jnp.zeros_like(acc)

        acc[...] += (lax.dot_general(da_ref[...], wg_ref[...], (NT, ((), ())), preferred_element_type=F32)
                     + lax.dot_general(db_ref[...], wu_ref[...], (NT, ((), ())), preferred_element_type=F32))

        @pl.when(j == N_DEV - 1)
        def _():
            o_ref[...] = acc[...]

    hid = pl.BlockSpec((None, tm, fs), lambda m, j: (j, m, 0))
    wspec = pl.BlockSpec((None, None, d, fs), lambda m, j: (j, layer, 0, 0))
    return _pcall(
        body, name=name, grid=(s // tm, N_DEV),
        in_specs=[hid, hid, wspec, wspec],
        out_specs=pl.BlockSpec((tm, d), lambda m, j: (m, 0)),
        out_shape=jax.ShapeDtypeStruct((s, d), F32),
        scratch_shapes=[pltpu.VMEM((tm, d), F32)],
        compiler_params=_params("parallel", "arbitrary"),
    )(da, db, wg, wu)


def _ffn_bwd_wgu(h, da, db, name):
    s, d = h.shape
    fs = da.shape[-1]
    tk = _tile(s, 512)

    def body(h_ref, da_ref, db_ref, og_ref, ou_ref):
        @pl.when(pl.program_id(1) == 0)
        def _():
            og_ref[...] = jnp.zeros_like(og_ref)
            ou_ref[...] = jnp.zeros_like(ou_ref)

        hv = h_ref[...]
        og_ref[...] += lax.dot_general(hv, da_ref[...], (TN, ((), ())), preferred_element_type=F32)
        ou_ref[...] += lax.dot_general(hv, db_ref[...], (TN, ((), ())), preferred_element_type=F32)

    hid = pl.BlockSpec((None, tk, fs), lambda j, k: (j, k, 0))
    ospec = pl.BlockSpec((None, d, fs), lambda j, k: (j, 0, 0))
    return _pcall(
        body, name=name, grid=(N_DEV, s // tk),
        in_specs=[pl.BlockSpec((tk, d), lambda j, k: (k, 0)), hid, hid],
        out_specs=[ospec, ospec],
        out_shape=[jax.ShapeDtypeStruct((N_DEV, d, fs), F32)] * 2,
        compiler_params=_params("parallel", "arbitrary"),
    )(h, da, db)


def _mm(a, b, *, ta=False, tb=False, out_dtype=F32, name, tm=1024, tn=1024, tk=2048, residual=None):
    m, kdim = (a.shape[1], a.shape[0]) if ta else a.shape
    n = b.shape[0] if tb else b.shape[1]
    tm, tn, tk = _tile(m, tm), _tile(n, tn), _tile(kdim, tk)
    nk = kdim // tk
    dims = ((0,) if ta else (1,), (1,) if tb else (0,))

    def body(*refs):
        a_ref, b_ref = refs[:2]
        acc = refs[-1]
        kk = pl.program_id(2)

        @pl.when(kk == 0)
        def _():
            acc[...] = jnp.zeros_like(acc)

        acc[...] += lax.dot_general(a_ref[...].astype(BF16), b_ref[...].astype(BF16), (dims, ((), ())),
                                    preferred_element_type=F32)

        @pl.when(kk == nk - 1)
        def _():
            if residual is None:
                refs[2][...] = acc[...].astype(out_dtype)
            else:
                res_ref, gate_ref, y_ref, xo_ref = refs[2:6]
                yv = acc[...]
                y_ref[...] = yv
                xo_ref[...] = res_ref[...] + gate_ref[...] * yv

    a_spec = pl.BlockSpec((tk, tm), lambda i, j, k: (k, i)) if ta else pl.BlockSpec((tm, tk), lambda i, j, k: (i, k))
    b_spec = pl.BlockSpec((tn, tk), lambda i, j, k: (j, k)) if tb else pl.BlockSpec((tk, tn), lambda i, j, k: (k, j))
    o_spec = pl.BlockSpec((tm, tn), lambda i, j, k: (i, j))
    if residual is None:
        in_specs, out_specs = [a_spec, b_spec], o_spec
        out_shape = jax.ShapeDtypeStruct((m, n), out_dtype)
        args = (a, b)
    else:
        in_specs = [a_spec, b_spec, o_spec, pl.BlockSpec((1, tn), lambda i, j, k: (0, j))]
        out_specs = [o_spec, o_spec]
        out_shape = [jax.ShapeDtypeStruct((m, n), F32)] * 2
        args = (a, b) + tuple(residual)
    return _pcall(
        body, name=name, grid=(m // tm, n // tn, nk), in_specs=in_specs, out_specs=out_specs, out_shape=out_shape,
        scratch_shapes=[pltpu.VMEM((tm, tn), F32)],
        compiler_params=_params("parallel", "parallel", "arbitrary"),
    )(*args)


def _log_sigmoid(z):
    return jnp.minimum(z, 0.0) - jnp.log(1.0 + jnp.exp(-jnp.abs(z)))


def _fox_gate(proj, small_blk, bias_lane):
    s = proj.shape[0]
    ts = _tile(s, 1024)
    nsub = ts // LANES

    def body(z_ref, b_ref, cum_ref, carry):
        @pl.when(pl.program_id(0) == 0)
        def _():
            carry[...] = jnp.zeros_like(carry)

        ii = lax.broadcasted_iota(jnp.int32, (LANES, LANES), 0)
        jj = lax.broadcasted_iota(jnp.int32, (LANES, LANES), 1)
        tri = (ii >= jj).astype(BF16)
        logf = _log_sigmoid(z_ref[...] + b_ref[...])
        cv = carry[...]
        for sb in range(nsub):
            blk = logf[sb * LANES:(sb + 1) * LANES, :]
            cum_ref[sb * LANES:(sb + 1) * LANES, :] = _dot_exact_lhs(tri, blk) + cv
            cv = cv + jnp.sum(blk, axis=0, keepdims=True)
        carry[...] = cv

    return _pcall(
        body, name="fox_gate", grid=(s // ts,),
        in_specs=[pl.BlockSpec((ts, LANES), lambda i: (i, small_blk)), pl.BlockSpec((1, LANES), lambda i: (0, 0))],
        out_specs=pl.BlockSpec((ts, LANES), lambda i: (i, 0)),
        out_shape=jax.ShapeDtypeStruct((s, LANES), F32),
        scratch_shapes=[pltpu.VMEM((1, LANES), F32)],
        compiler_params=_params("arbitrary"),
    )(proj, bias_lane)


def _fox_gate_bwd(dcum_q, dcum_k, proj, small_blk, bias_lane):
    s = proj.shape[0]
    ts = _tile(s, 1024)
    nsub = ts // LANES
    nb = s // ts

    def body(dcq_ref, dc_ref, z_ref, b_ref, dz_ref, db_ref, carry):
        @pl.when(pl.program_id(0) == 0)
        def _():
            carry[...] = jnp.zeros_like(carry)
            db_ref[...] = jnp.zeros_like(db_ref)

        ii = lax.broadcasted_iota(jnp.int32, (LANES, LANES), 0)
        jj = lax.broadcasted_iota(jnp.int32, (LANES, LANES), 1)
        triu = (jj >= ii).astype(BF16)
        dc = dcq_ref[...] + dc_ref[...]
        zb = z_ref[...] + b_ref[...]
        cv = carry[...]
        dbv = jnp.zeros((1, LANES), F32)
        for sb in reversed(range(nsub)):
            rows = slice(sb * LANES, (sb + 1) * LANES)
            blk = dc[rows, :]
            dlogf = _dot_exact_lhs(triu, blk) + cv
            cv = cv + jnp.sum(blk, axis=0, keepdims=True)
            dz = dlogf * _sigmoid(-zb[rows, :])
            dz_ref[rows, :] = dz
            dbv = dbv + jnp.sum(dz, axis=0, keepdims=True)
        carry[...] = cv
        db_ref[...] += dbv

    row = pl.BlockSpec((1, LANES), lambda i: (0, 0))
    return _pcall(
        body, name="fox_gate_bwd", grid=(nb,),
        in_specs=[pl.BlockSpec((ts, LANES), lambda i: (nb - 1 - i, 0)),
                  pl.BlockSpec((ts, LANES), lambda i: (nb - 1 - i, 0)),
                  pl.BlockSpec((ts, LANES), lambda i: (nb - 1 - i, small_blk)), row],
        out_specs=[pl.BlockSpec((ts, LANES), lambda i: (nb - 1 - i, 0)), row],
        out_shape=[jax.ShapeDtypeStruct((s, LANES), F32), jax.ShapeDtypeStruct((1, LANES), F32)],
        scratch_shapes=[pltpu.VMEM((1, LANES), F32)],
        compiler_params=_params("arbitrary"),
    )(dcum_q, dcum_k, proj, bias_lane)


def _fox_scores(q, k, cq_col, ck_row, q0, k0, scale):
    sc = lax.dot_general(q.astype(BF16), k.astype(BF16), (NT, ((), ())), preferred_element_type=F32) * scale
    sc = sc + (cq_col - ck_row)
    row = q0 + lax.broadcasted_iota(jnp.int32, sc.shape, 0)
    col = k0 + lax.broadcasted_iota(jnp.int32, sc.shape, 1)
    return jnp.where(col <= row, sc, NEG)


def _fox_fwd(proj, cum_col, cum_row, w_norm, heads):
    s = proj.shape[0]
    t = _tile(s, 512)
    nq = s // t
    scale = 1.0 / math.sqrt(HEAD_DIM)

    def body(q_ref, k_ref, v_ref, cq_ref, ck_ref, w_ref, o_ref, lse_ref, on_ref, m_s, l_s, acc_s):
        iq, ik = pl.program_id(1), pl.program_id(2)

        @pl.when(ik == 0)
        def _():
            m_s[...] = jnp.full_like(m_s, NEG)
            l_s[...] = jnp.zeros_like(l_s)
            acc_s[...] = jnp.zeros_like(acc_s)

        @pl.when(ik <= iq)
        def _():
            sc = _fox_scores(q_ref[...], k_ref[...], cq_ref[:, 0:1], ck_ref[...], iq * t, ik * t, scale)
            m_prev = m_s[...]
            m_new = jnp.maximum(m_prev, jnp.max(sc, axis=1, keepdims=True))
            alpha = jnp.exp(m_prev - m_new)
            p = jnp.exp(sc - m_new)
            l_s[...] = alpha * l_s[...] + jnp.sum(p, axis=1, keepdims=True)
            acc_s[...] = alpha * acc_s[...] + _dot(p, v_ref[...], NN)
            m_s[...] = m_new

        @pl.when(ik == iq)
        def _():
            o = acc_s[...] / l_s[...]
            o_ref[...] = o
            lse_ref[...] = jnp.broadcast_to(m_s[...] + jnp.log(l_s[...]), lse_ref.shape)
            r = lax.rsqrt(jnp.mean(o * o, axis=1, keepdims=True) + EPS)
            on_ref[...] = (o * r * w_ref[...]).astype(BF16)

    kmap = lambda off: (lambda h, iq, ik: (jnp.minimum(ik, iq), off + h))
    oblk = pl.BlockSpec((t, HEAD_DIM), lambda h, iq, ik: (iq, h))
    return _pcall(
        body, name="fox_fwd", grid=(heads, nq, nq),
        in_specs=[pl.BlockSpec((t, HEAD_DIM), lambda h, iq, ik: (iq, h)),
                  pl.BlockSpec((t, HEAD_DIM), kmap(heads)), pl.BlockSpec((t, HEAD_DIM), kmap(2 * heads)),
                  pl.BlockSpec((None, t, LANES), lambda h, iq, ik: (h, iq, 0)),
                  pl.BlockSpec((None, 1, t), lambda h, iq, ik: (h, 0, jnp.minimum(ik, iq))),
                  pl.BlockSpec((1, HEAD_DIM), lambda h, iq, ik: (0, 0))],
        out_specs=[oblk, pl.BlockSpec((None, t, LANES), lambda h, iq, ik: (h, iq, 0)), oblk],
        out_shape=[jax.ShapeDtypeStruct((s, heads * HEAD_DIM), F32), jax.ShapeDtypeStruct((heads, s, LANES), F32),
                   jax.ShapeDtypeStruct((s, heads * HEAD_DIM), BF16)],
        scratch_shapes=[pltpu.VMEM((t, 1), F32), pltpu.VMEM((t, 1), F32), pltpu.VMEM((t, HEAD_DIM), F32)],
        compiler_params=_params("parallel", "parallel", "arbitrary"),
    )(proj, proj, proj, cum_col, cum_row, w_norm)


def _fox_prep_bwd(do_cat, o_raw, w_norm, heads):
    s = o_raw.shape[0]
    ts = _tile(s, 1024)

    def body(g_ref, o_ref, w_ref, do_ref, delta_ref, dw_ref):
        @pl.when((pl.program_id(0) == 0) & (pl.program_id(1) == 0))
        def _():
            dw_ref[...] = jnp.zeros_like(dw_ref)

        o = o_ref[...]
        g = g_ref[...]
        r = lax.rsqrt(jnp.mean(o * o, axis=1, keepdims=True) + EPS)
        wg = g * w_ref[...]
        do = r * wg - o * (r * r * r) * jnp.mean(wg * o, axis=1, keepdims=True)
        do_ref[...] = do.astype(BF16)
        delta_ref[...] = jnp.broadcast_to(jnp.sum(do * o, axis=1, keepdims=True), delta_ref.shape)
        dw_ref[...] += jnp.sum(g * o * r, axis=0, keepdims=True)

    blk = pl.BlockSpec((ts, HEAD_DIM), lambda h, i: (i, h))
    row = pl.BlockSpec((1, HEAD_DIM), lambda h, i: (0, 0))
    return _pcall(
        body, name="fox_prep_bwd", grid=(heads, s // ts),
        in_specs=[blk, blk, row],
        out_specs=[blk, pl.BlockSpec((None, ts, LANES), lambda h, i: (h, i, 0)), row],
        out_shape=[jax.ShapeDtypeStruct((s, heads * HEAD_DIM), BF16), jax.ShapeDtypeStruct((heads, s, LANES), F32),
                   jax.ShapeDtypeStruct((1, HEAD_DIM), F32)],
        compiler_params=_params("arbitrary", "arbitrary"),
    )(do_cat, o_raw, w_norm)


def _fox_dq(proj, do, cum_col, cum_row, lse, delta, heads):
    s = proj.shape[0]
    t = _tile(s, 512)
    nq = s // t
    scale = 1.0 / math.sqrt(HEAD_DIM)

    def body(q_ref, k_ref, v_ref, do_ref, cq_ref, ck_ref, lse_ref, dl_ref, dq_ref, dc_ref, acc, dc_acc):
        iq, ik = pl.program_id(1), pl.program_id(2)

        @pl.when(ik == 0)
        def _():
            acc[...] = jnp.zeros_like(acc)
            dc_acc[...] = jnp.zeros_like(dc_acc)

        @pl.when(ik <= iq)
        def _():
            kv = k_ref[...]
            sc = _fox_scores(q_ref[...], kv, cq_ref[:, 0:1], ck_ref[...], iq * t, ik * t, scale)
            p = jnp.exp(sc - lse_ref[:, 0:1])
            dp = _dot(do_ref[...], v_ref[...], NT)
            ds = p * (dp - dl_ref[:, 0:1])
            acc[...] += _dot(ds, kv, NN)
            dc_acc[...] += jnp.sum(ds, axis=1, keepdims=True)

        @pl.when(ik == iq)
        def _():
            dq_ref[...] = (acc[...] * scale).astype(BF16)
            dc_ref[...] = jnp.broadcast_to(dc_acc[...], dc_ref.shape)

    kmap = lambda off: (lambda h, iq, ik: (jnp.minimum(ik, iq), off + h))
    qblk = pl.BlockSpec((t, HEAD_DIM), lambda h, iq, ik: (iq, h))
    col = pl.BlockSpec((None, t, LANES), lambda h, iq, ik: (h, iq, 0))
    return _pcall(
        body, name="fox_dq", grid=(heads, nq, nq),
        in_specs=[qblk, pl.BlockSpec((t, HEAD_DIM), kmap(heads)), pl.BlockSpec((t, HEAD_DIM), kmap(2 * heads)),
                  qblk, col, pl.BlockSpec((None, 1, t), lambda h, iq, ik: (h, 0, jnp.minimum(ik, iq))), col, col],
        out_specs=[qblk, col],
        out_shape=[jax.ShapeDtypeStruct((s, heads * HEAD_DIM), BF16), jax.ShapeDtypeStruct((heads, s, LANES), F32)],
        scratch_shapes=[pltpu.VMEM((t, HEAD_DIM), F32), pltpu.VMEM((t, 1), F32)],
        compiler_params=_params("parallel", "parallel", "arbitrary"),
    )(proj, proj, proj, do, cum_col, cum_row, lse, delta)


def _fox_dkv(proj, do, cum_col, cum_row, lse_row, delta_row, heads):
    s = proj.shape[0]
    t = _tile(s, 512)
    nk = s // t
    scale = 1.0 / math.sqrt(HEAD_DIM)

    def body(q_ref, k_ref, v_ref, do_ref, cqr_ref, ckc_ref, lse_ref, dl_ref, dk_ref, dv_ref, dc_ref, dk_acc, dv_acc,
             dc_acc):
        ik, iq = pl.program_id(1), pl.program_id(2)

        @pl.when(iq == 0)
        def _():
            dk_acc[...] = jnp.zeros_like(dk_acc)
            dv_acc[...] = jnp.zeros_like(dv_acc)
            dc_acc[...] = jnp.zeros_like(dc_acc)

        @pl.when(iq >= ik)
        def _():
            qv = q_ref[...]
            dov = do_ref[...]
            st = lax.dot_general(k_ref[...].astype(BF16), qv.astype(BF16), (NT, ((), ())),
                                 preferred_element_type=F32) * scale
            st = st + (cqr_ref[...] - ckc_ref[:, 0:1])
            kpos = ik * t + lax.broadcasted_iota(jnp.int32, st.shape, 0)
            qpos = iq * t + lax.broadcasted_iota(jnp.int32, st.shape, 1)
            st = jnp.where(kpos <= qpos, st, NEG)
            pt = jnp.exp(st - lse_ref[...])
            dv_acc[...] += _dot(pt, dov, NN)
            dpt = _dot(v_ref[...], dov, NT)
            dst = pt * (dpt - dl_ref[...])
            dk_acc[...] += _dot(dst, qv, NN)
            dc_acc[...] += jnp.sum(dst, axis=1, keepdims=True)

        @pl.when(iq == nk - 1)
        def _():
            dk_ref[...] = (dk_acc[...] * scale).astype(BF16)
            dv_ref[...] = dv_acc[...].astype(BF16)
            dc_ref[...] = jnp.broadcast_to(-dc_acc[...], dc_ref.shape)

    qi = lambda ik, iq: jnp.maximum(iq, ik)
    qblk = pl.BlockSpec((t, HEAD_DIM), lambda h, ik, iq: (qi(ik, iq), h))
    qrow = pl.BlockSpec((None, 1, t), lambda h, ik, iq: (h, 0, qi(ik, iq)))
    kblk = lambda off: pl.BlockSpec((t, HEAD_DIM), lambda h, ik, iq: (ik, off + h))
    kout = pl.BlockSpec((t, HEAD_DIM), lambda h, ik, iq: (ik, h))
    kcol = pl.BlockSpec((None, t, LANES), lambda h, ik, iq: (h, ik, 0))
    return _pcall(
        body, name="fox_dkv", grid=(heads, nk, nk),
        in_specs=[qblk, kblk(heads), kblk(2 * heads), qblk, qrow, kcol, qrow, qrow],
        out_specs=[kout, kout, kcol],
        out_shape=[jax.ShapeDtypeStruct((s, heads * HEAD_DIM), BF16)] * 2
        + [jax.ShapeDtypeStruct((heads, s, LANES), F32)],
        scratch_shapes=[pltpu.VMEM((t, HEAD_DIM), F32), pltpu.VMEM((t, HEAD_DIM), F32), pltpu.VMEM((t, 1), F32)],
        compiler_params=_params("parallel", "parallel", "arbitrary"),
    )(proj, proj, proj, do, cum_row, cum_col, lse_row, delta_row)


def _shift_rows(xv, halo, j, forward):
    n = xv.shape[0]
    rid = lax.broadcasted_iota(jnp.int32, (8, xv.shape[1]), 0)
    if forward:
        xs = pltpu.roll(xv, n - j, 0)
        hs = pltpu.roll(halo, 8 - j, 0)
        edge = jnp.where(rid >= 8 - j, hs, xs[n - 8:, :])
        return jnp.concatenate([xs[:n - 8, :], edge], axis=0)
    xs = pltpu.roll(xv, j, 0)
    hs = pltpu.roll(halo, j, 0)
    edge = jnp.where(rid < j, hs, xs[:8, :])
    return jnp.concatenate([edge, xs[8:, :]], axis=0)


def _conv_silu(xv, halo, w):
    xc = w[CONV_W - 1:CONV_W, :] * xv
    for j in range(1, CONV_W):
        xc = xc + w[CONV_W - 1 - j:CONV_W - j, :] * _shift_rows(xv, halo, j, False)
    return xc, xc * _sigmoid(xc)


def _gdn_pre(proj, conv_w, heads):
    s = proj.shape[0]
    cw = 3 * heads * HEAD_DIM
    ts = _tile(s, 256)
    tb = ts // 8

    def body(x_ref, halo_ref, w_ref, q_ref, k_ref, v_ref):
        halo = jnp.where(pl.program_id(0) == 0, 0.0, halo_ref[...])
        _, y = _conv_silu(x_ref[...], halo, w_ref[...])
        for h in range(heads):
            for part, ref in enumerate((q_ref, k_ref, v_ref)):
                c0 = (part * heads + h) * HEAD_DIM
                blk = y[:, c0:c0 + HEAD_DIM]
                if part < 2:
                    blk = blk * lax.rsqrt(jnp.sum(blk * blk, axis=1, keepdims=True) + EPS)
                ref[h] = blk

    out = pl.BlockSpec((heads, ts, HEAD_DIM), lambda i: (0, i, 0))
    return _pcall(
        body, name="gdn_pre", grid=(s // ts,),
        in_specs=[pl.BlockSpec((ts, cw), lambda i: (i, 1)),
                  pl.BlockSpec((8, cw), lambda i: (jnp.maximum(i * tb - 1, 0), 1)),
                  pl.BlockSpec((CONV_W, cw), lambda i: (0, 0))],
        out_specs=[out, out, out],
        out_shape=[jax.ShapeDtypeStruct((heads, s, HEAD_DIM), F32)] * 3,
        compiler_params=_params("parallel"),
    )(proj, proj, conv_w)


def _gdn_pre_bwd_act(proj, conv_w, dq, dk, dv, heads):
    s = proj.shape[0]
    cw = 3 * heads * HEAD_DIM
    ts = _tile(s, 256)
    tb = ts // 8

    def body(x_ref, halo_ref, w_ref, dq_ref, dk_ref, dv_ref, dxc_ref, dw_ref):
        @pl.when(pl.program_id(0) == 0)
        def _():
            dw_ref[...] = jnp.zeros_like(dw_ref)

        xv = x_ref[...]
        halo = jnp.where(pl.program_id(0) == 0, 0.0, halo_ref[...])
        xc, y = _conv_silu(xv, halo, w_ref[...])
        sg = _sigmoid(xc)
        dsilu = sg * (1.0 + xc * (1.0 - sg))
        for h in range(heads):
            for part, ref in enumerate((dq_ref, dk_ref, dv_ref)):
                c0 = (part * heads + h) * HEAD_DIM
                g = ref[h]
                if part < 2:
                    blk = y[:, c0:c0 + HEAD_DIM]
                    r = lax.rsqrt(jnp.sum(blk * blk, axis=1, keepdims=True) + EPS)
                    g = r * g - blk * (r * r * r) * jnp.sum(g * blk, axis=1, keepdims=True)
                dxc_ref[:, c0:c0 + HEAD_DIM] = g * dsilu[:, c0:c0 + HEAD_DIM]
        dxc = dxc_ref[...]
        rows = [jnp.sum(dxc * (xv if j == 0 else _shift_rows(xv, halo, j, False)), axis=0, keepdims=True)
                for j in range(CONV_W)]
        dw_ref[...] += jnp.concatenate([rows[CONV_W - 1 - k] for k in range(CONV_W)]
                                       + [jnp.zeros((8 - CONV_W, cw), F32)], axis=0)

    hblk = pl.BlockSpec((heads, ts, HEAD_DIM), lambda i: (0, i, 0))
    return _pcall(
        body, name="gdn_pre_bwd_act", grid=(s // ts,),
        in_specs=[pl.BlockSpec((ts, cw), lambda i: (i, 1)),
                  pl.BlockSpec((8, cw), lambda i: (jnp.maximum(i * tb - 1, 0), 1)),
                  pl.BlockSpec((CONV_W, cw), lambda i: (0, 0)), hblk, hblk, hblk],
        out_specs=[pl.BlockSpec((ts, cw), lambda i: (i, 0)), pl.BlockSpec((8, cw), lambda i: (0, 0))],
        out_shape=[jax.ShapeDtypeStruct((s, cw), F32), jax.ShapeDtypeStruct((8, cw), F32)],
        compiler_params=_params("arbitrary"),
    )(proj, proj, conv_w, dq, dk, dv)


def _gdn_pre_bwd_conv(dxc, conv_w):
    s, cw = dxc.shape
    ts = _tile(s, 256)
    tb = ts // 8
    last = s // 8 - 1

    def body(g_ref, halo_ref, w_ref, dx_ref):
        gv = g_ref[...]
        w = w_ref[...]
        halo = jnp.where(pl.program_id(0) == s // ts - 1, 0.0, halo_ref[...])
        dx = w[CONV_W - 1:CONV_W, :] * gv
        for j in range(1, CONV_W):
            dx = dx + w[CONV_W - 1 - j:CONV_W - j, :] * _shift_rows(gv, halo, j, True)
        dx_ref[...] = dx.astype(BF16)

    return _pcall(
        body, name="gdn_pre_bwd_conv", grid=(s // ts,),
        in_specs=[pl.BlockSpec((ts, cw), lambda i: (i, 0)),
                  pl.BlockSpec((8, cw), lambda i: (jnp.minimum((i + 1) * tb, last), 0)),
                  pl.BlockSpec((CONV_W, cw), lambda i: (0, 0))],
        out_specs=pl.BlockSpec((ts, cw), lambda i: (i, 0)),
        out_shape=jax.ShapeDtypeStruct((s, cw), BF16),
        compiler_params=_params("parallel"),
    )(dxc, dxc, conv_w)


def _bdot(a, b, ca, cb):
    return lax.dot_general(a.astype(BF16), b.astype(BF16), (((ca,), (cb,)), ((0,), (0,))),
                           preferred_element_type=F32)


def _bdot_hp(a, b, ca, cb):
    ah = a.astype(BF16)
    al = (a - ah.astype(F32)).astype(BF16)
    bh = b.astype(BF16)
    bl = (b - bh.astype(F32)).astype(BF16)
    d = lambda p, q: lax.dot_general(p, q, (((ca,), (cb,)), ((0,), (0,))), preferred_element_type=F32)
    return d(ah, bh) + (d(ah, bl) + d(al, bh))


def _gdn_gates(small, a_lane, dt_lane, heads):
    lane = lax.broadcasted_iota(jnp.int32, small.shape, 1)
    za = small + dt_lane
    g_all = -jnp.exp(a_lane) * (jnp.maximum(za, 0.0) + jnp.log(1.0 + jnp.exp(-jnp.abs(za))))
    b_all = _sigmoid(small)
    pick = lambda v, l: jnp.sum(jnp.where(lane == l, v, 0.0), axis=1, keepdims=True)
    g = jnp.stack([pick(g_all, heads + h) for h in range(heads)], axis=0)
    beta = jnp.stack([pick(b_all, 2 * heads + h) for h in range(heads)], axis=0)
    return g, beta


def _chunk_masks(c):
    ii = lax.broadcasted_iota(jnp.int32, (1, c, c), 1)
    jj = lax.broadcasted_iota(jnp.int32, (1, c, c), 2)
    return ii >= jj, ii > jj, ii == jj


def _col_to_row(col, eye):
    return jnp.sum(jnp.where(eye, col, 0.0), axis=1, keepdims=True)


def _row_to_col(row, eye):
    return jnp.sum(jnp.where(eye, row, 0.0), axis=2, keepdims=True)


def _gdn_chunk(q, k, v, g, beta, state):
    c = q.shape[1]
    incl, strict, eye = _chunk_masks(c)
    g_row = _col_to_row(g, eye)
    gc_col = jnp.sum(jnp.where(incl, g_row, 0.0), axis=2, keepdims=True)
    gc_row = _col_to_row(gc_col, eye)
    gam = jnp.where(incl, jnp.exp(jnp.where(incl, gc_col - gc_row, NEG)), 0.0)
    egc = jnp.exp(gc_col)
    kb = k * beta
    vb = v * beta
    kbe = kb * egc
    low = jnp.where(strict, _bdot(kb, k, 2, 2), 0.0) * gam
    p = -low
    tinv = jnp.where(eye, 1.0, 0.0) + p
    width = 2
    while width < c:
        p = _bdot_hp(p, p, 2, 1)
        tinv = tinv + _bdot_hp(tinv, p, 2, 1)
        width *= 2
    u = _bdot(tinv, vb, 2, 1)
    w = _bdot(tinv, kbe, 2, 1)
    att = jnp.where(incl, _bdot(q, k, 2, 2), 0.0) * gam
    vn = u - _bdot(w, state, 2, 1)
    qe = q * egc
    o = _bdot(qe, state, 2, 1) + _bdot(att, vn, 2, 1)
    gl = jnp.sum(g, axis=1, keepdims=True)
    edec = jnp.exp(gl - gc_col)
    kdec = k * edec
    egl = jnp.exp(gl)
    new_state = state * egl + _bdot(kdec, vn, 1, 1)
    return dict(incl=incl, strict=strict, eye=eye, gam=gam, egc=egc, kb=kb, vb=vb, kbe=kbe, low=low, tinv=tinv, w=w,
                att=att, vn=vn, qe=qe, o=o, edec=edec, kdec=kdec, egl=egl, new_state=new_state)


def _gdn_load(q_ref, k_ref, v_ref, small_ref, a_ref, dt_ref, rows, heads):
    q = q_ref[:, rows, :] * (HEAD_DIM ** -0.5)
    g, beta = _gdn_gates(small_ref[rows, :], a_ref[...], dt_ref[...], heads)
    return q, k_ref[:, rows, :], v_ref[:, rows, :], g, beta


def _gdn_fwd(q, k, v, proj, z_blk, small_blk, a_lane, dt_lane, w_norm):
    heads, s, _ = q.shape
    c = min(GDN_CHUNK, s)
    r = _tile(s, 512)
    npb = r // c
    gw = heads * HEAD_DIM

    def body(q_ref, k_ref, v_ref, z_ref, small_ref, a_ref, dt_ref, w_ref, o_ref, st_ref, state):
        @pl.when(pl.program_id(0) == 0)
        def _():
            state[...] = jnp.zeros_like(state)

        def chunk(cb, carry):
            rows = pl.ds(pl.multiple_of(cb * c, c), c)
            qv, kv, vv, g, beta = _gdn_load(q_ref, k_ref, v_ref, small_ref, a_ref, dt_ref, rows, heads)
            st = state[...]
            st_ref[:, cb] = st
            res = _gdn_chunk(qv, kv, vv, g, beta, st)
            state[...] = res["new_state"]
            o = res["o"]
            rn = lax.rsqrt(jnp.mean(o * o, axis=2, keepdims=True) + EPS)
            zv = z_ref[rows, :]
            for h in range(heads):
                zh = zv[:, h * HEAD_DIM:(h + 1) * HEAD_DIM]
                o_ref[rows, h * HEAD_DIM:(h + 1) * HEAD_DIM] = (
                    o[h] * rn[h] * w_ref[...] * (zh * _sigmoid(zh))).astype(BF16)
            return carry

        lax.fori_loop(0, npb, chunk, 0)

    hblk = pl.BlockSpec((heads, r, HEAD_DIM), lambda i: (0, i, 0))
    row = pl.BlockSpec((1, LANES), lambda i: (0, 0))
    return _pcall(
        body, name="gdn_fwd", grid=(s // r,),
        in_specs=[hblk, hblk, hblk, pl.BlockSpec((r, gw), lambda i: (i, z_blk)),
                  pl.BlockSpec((r, LANES), lambda i: (i, small_blk)), row, row, row],
        out_specs=[pl.BlockSpec((r, gw), lambda i: (i, 0)),
                   pl.BlockSpec((heads, npb, HEAD_DIM, HEAD_DIM), lambda i: (0, i, 0, 0))],
        out_shape=[jax.ShapeDtypeStruct((s, gw), BF16),
                   jax.ShapeDtypeStruct((heads, s // c, HEAD_DIM, HEAD_DIM), F32)],
        scratch_shapes=[pltpu.VMEM((heads, HEAD_DIM, HEAD_DIM), F32)],
        compiler_params=_params("arbitrary"),
    )(q, k, v, proj, proj, a_lane, dt_lane, w_norm)


def _gdn_bwd(q, k, v, proj, z_blk, small_blk, a_lane, dt_lane, w_norm, states, do_cat, do_blk):
    heads, s, _ = q.shape
    c = min(GDN_CHUNK, s)
    r = _tile(s, 512)
    npb = r // c
    nb = s // r
    gw = heads * HEAD_DIM

    def body(q_ref, k_ref, v_ref, z_ref, small_ref, a_ref, dt_ref, w_ref, st_ref, do_ref,
             dq_ref, dk_ref, dv_ref, dz_ref, dsm_ref, da_ref, ddt_ref, dw_ref, dstate):
        @pl.when(pl.program_id(0) == 0)
        def _():
            dstate[...] = jnp.zeros_like(dstate)
            da_ref[...] = jnp.zeros_like(da_ref)
            ddt_ref[...] = jnp.zeros_like(ddt_ref)
            dw_ref[...] = jnp.zeros_like(dw_ref)

        def chunk(it, carry):
            cb = npb - 1 - it
            rows = pl.ds(pl.multiple_of(cb * c, c), c)
            qv, kv, vv, g, beta = _gdn_load(q_ref, k_ref, v_ref, small_ref, a_ref, dt_ref, rows, heads)
            st = st_ref[:, cb]
            f = _gdn_chunk(qv, kv, vv, g, beta, st)
            incl, strict, eye = f["incl"], f["strict"], f["eye"]
            o = f["o"]
            wv = w_ref[...]
            zv = z_ref[rows, :]
            dov = do_ref[rows, :]
            rn = lax.rsqrt(jnp.mean(o * o, axis=2, keepdims=True) + EPS)
            do_l, dw_acc = [], jnp.zeros((1, HEAD_DIM), F32)
            for h in range(heads):
                sl = slice(h * HEAD_DIM, (h + 1) * HEAD_DIM)
                zh, gh = zv[:, sl], dov[:, sl]
                sg = _sigmoid(zh)
                on = o[h] * rn[h]
                dz_ref[rows, sl] = (gh * (on * wv) * (sg * (1.0 + zh * (1.0 - sg)))).astype(BF16)
                gn = gh * (zh * sg)
                dw_acc = dw_acc + jnp.sum(gn * on, axis=0, keepdims=True)
                wg = gn * wv
                do_l.append(rn[h] * wg - o[h] * (rn[h] * rn[h] * rn[h]) * jnp.mean(wg * o[h], axis=1, keepdims=True))
            dw_ref[...] += dw_acc
            do = jnp.stack(do_l, axis=0)
            ds_out = dstate[...]
            dvn = _bdot(f["att"], do, 1, 1) + _bdot(f["kdec"], ds_out, 2, 1)
            datt = jnp.where(incl, _bdot(do, f["vn"], 2, 2), 0.0)
            dqe = _bdot(do, st, 2, 2)
            dstate[...] = _bdot(f["qe"], do, 1, 1) + f["egl"] * ds_out - _bdot(f["w"], dvn, 1, 1)
            dw = -_bdot(dvn, st, 2, 2)
            dkdec = _bdot(f["vn"], ds_out, 2, 2)
            t_kdec = jnp.sum(dkdec * f["kdec"], axis=2, keepdims=True)
            dgl = (jnp.sum(jnp.sum(st * ds_out, axis=2, keepdims=True), axis=1, keepdims=True) * f["egl"]
                   + jnp.sum(t_kdec, axis=1, keepdims=True))
            dgc = jnp.sum(dqe * f["qe"], axis=2, keepdims=True) - t_kdec
            dq = dqe * f["egc"]
            dk = dkdec * f["edec"]
            dtinv = _bdot(dvn, f["vb"], 2, 2) + _bdot(dw, f["kbe"], 2, 2)
            dvb = _bdot(f["tinv"], dvn, 1, 1)
            dkbe = _bdot(f["tinv"], dw, 1, 1)
            dkb = dkbe * f["egc"]
            dgc = dgc + jnp.sum(dkbe * f["kbe"], axis=2, keepdims=True)
            dlow = jnp.where(strict, -_bdot_hp(_bdot_hp(f["tinv"], dtinv, 1, 1), f["tinv"], 2, 2), 0.0)
            ml = dlow * f["gam"]
            dkb = dkb + _bdot(ml, kv, 2, 1)
            dk = dk + _bdot(ml, f["kb"], 1, 1)
            ma = datt * f["gam"]
            dq = dq + _bdot(ma, kv, 2, 1)
            dk = dk + _bdot(ma, qv, 1, 1)
            e = dlow * f["low"] + datt * f["att"]
            dgc = dgc + jnp.sum(e, axis=2, keepdims=True) - _row_to_col(jnp.sum(e, axis=1, keepdims=True), eye)
            dk = dk + beta * dkb
            dbeta = jnp.sum(dkb * kv, axis=2, keepdims=True) + jnp.sum(dvb * vv, axis=2, keepdims=True)
            dgc_row = _col_to_row(dgc, eye)
            dg = jnp.sum(jnp.where(incl, 0.0, dgc_row) + jnp.where(eye, dgc_row, 0.0), axis=2, keepdims=True) + dgl
            dq_ref[:, rows, :] = dq * (HEAD_DIM ** -0.5)
            dk_ref[:, rows, :] = dk
            dv_ref[:, rows, :] = beta * dvb
            small = small_ref[rows, :]
            lane = lax.broadcasted_iota(jnp.int32, small.shape, 1)
            dg_l = jnp.zeros(small.shape, F32)
            db_l = jnp.zeros(small.shape, F32)
            for h in range(heads):
                dg_l = dg_l + jnp.where(lane == heads + h, dg[h], 0.0)
                db_l = db_l + jnp.where(lane == 2 * heads + h, dbeta[h], 0.0)
            za = small + dt_ref[...]
            nexp = -jnp.exp(a_ref[...])
            softplus = jnp.maximum(za, 0.0) + jnp.log(1.0 + jnp.exp(-jnp.abs(za)))
            da_logit = dg_l * nexp * _sigmoid(za)
            sb = _sigmoid(small)
            dsm_ref[rows, :] = da_logit + db_l * sb * (1.0 - sb)
            ddt_ref[...] += jnp.sum(da_logit, axis=0, keepdims=True)
            da_ref[...] += jnp.sum(dg_l * nexp * softplus, axis=0, keepdims=True)
            return carry

        lax.fori_loop(0, npb, chunk, 0)

    rev = lambda i: nb - 1 - i
    hblk = pl.BlockSpec((heads, r, HEAD_DIM), lambda i: (0, rev(i), 0))
    row = pl.BlockSpec((1, LANES), lambda i: (0, 0))
    wide = lambda blk: pl.BlockSpec((r, gw), lambda i: (rev(i), blk))
    return _pcall(
        body, name="gdn_bwd", grid=(nb,),
        in_specs=[hblk, hblk, hblk, wide(z_blk), pl.BlockSpec((r, LANES), lambda i: (rev(i), small_blk)),
                  row, row, row, pl.BlockSpec((heads, npb, HEAD_DIM, HEAD_DIM), lambda i: (0, rev(i), 0, 0)),
                  wide(do_blk)],
        out_specs=[hblk, hblk, hblk, wide(0), pl.BlockSpec((r, LANES), lambda i: (rev(i), 0)), row, row, row],
        out_shape=[jax.ShapeDtypeStruct((heads, s, HEAD_DIM), F32)] * 3
        + [jax.ShapeDtypeStruct((s, gw), BF16), jax.ShapeDtypeStruct((s, LANES), F32)]
        + [jax.ShapeDtypeStruct((1, LANES), F32)] * 3,
        scratch_shapes=[pltpu.VMEM((heads, HEAD_DIM, HEAD_DIM), F32)],
        compiler_params=_params("arbitrary"),
    )(q, k, v, proj, proj, a_lane, dt_lane, w_norm, states, do_cat)


def _final(x, target, gf):
    s, d = x.shape
    ts = _tile(s, 512)

    def body(x_ref, t_ref, g_ref, loss_ref, dx_ref, dg_ref):
        @pl.when(pl.program_id(0) == 0)
        def _():
            loss_ref[...] = jnp.zeros_like(loss_ref)
            dg_ref[...] = jnp.zeros_like(dg_ref)

        xv = x_ref[...]
        gv = g_ref[...]
        r = lax.rsqrt(jnp.mean(xv * xv, axis=-1, keepdims=True) + EPS)
        xn = xv * r
        err = xn * gv - t_ref[...]
        per_tok = jnp.mean(err * err, axis=-1, keepdims=True)
        loss_ref[...] += 0.5 * jnp.sum(per_tok, axis=0, keepdims=True)
        dy = err * (1.0 / d)
        dg_ref[...] += jnp.sum(dy * xn, axis=0, keepdims=True)
        dxn = dy * gv
        dx_ref[...] = r * (dxn - xn * jnp.mean(dxn * xn, axis=-1, keepdims=True))

    blk = pl.BlockSpec((ts, d), lambda i: (i, 0))
    row = pl.BlockSpec((1, d), lambda i: (0, 0))
    return _pcall(
        body, name="final_loss", grid=(s // ts,),
        in_specs=[blk, blk, row], out_specs=[pl.BlockSpec((1, LANES), lambda i: (0, 0)), blk, row],
        out_shape=[jax.ShapeDtypeStruct((1, LANES), F32), jax.ShapeDtypeStruct((s, d), F32),
                   jax.ShapeDtypeStruct((1, d), F32)],
        compiler_params=_params("arbitrary"),
    )(x, target, gf)


def _adamw(parts, w, m, v, name):
    npart, rows, cols = parts.shape
    tr = _tile(rows, max(8, ADAM_BLOCK_BYTES // (4 * npart * cols)))
    c1 = 1.0 - ADAM_B1 ** ADAM_STEP
    c2 = 1.0 - ADAM_B2 ** ADAM_STEP

    def body(p_ref, w_ref, m_ref, v_ref, g_ref, d_ref, mo_ref, vo_ref):
        g = p_ref[0]
        for i in range(1, npart):
            g = g + p_ref[i]
        mn = ADAM_B1 * m_ref[...] + (1.0 - ADAM_B1) * g
        vn = ADAM_B2 * v_ref[...] + (1.0 - ADAM_B2) * (g * g)
        g_ref[...] = g
        mo_ref[...] = mn
        vo_ref[...] = vn
        d_ref[...] = -ADAM_LR * ((mn / c1) / (jnp.sqrt(vn / c2) + ADAM_EPS) + ADAM_WD * w_ref[...])

    blk = pl.BlockSpec((tr, cols), lambda i: (i, 0))
    return _pcall(
        body, name=name, grid=(rows // tr,),
        in_specs=[pl.BlockSpec((npart, tr, cols), lambda i: (0, i, 0)), blk, blk, blk],
        out_specs=[blk] * 4, out_shape=[jax.ShapeDtypeStruct((rows, cols), F32)] * 4,
        compiler_params=_params("parallel"),
    )(parts, w, m, v)


def _adamw_nd(parts, w, m, v, name):
    shp = w.shape
    flat = lambda a: a.reshape((-1, shp[-1]))
    outs = _adamw(parts.reshape((parts.shape[0], -1, shp[-1])), flat(w), flat(m), flat(v), name)
    return [o.reshape(shp) for o in outs]


def _pad_lanes(v, n=LANES, at=0):
    return jnp.pad(v, ((0, 0), (at, n - at - v.shape[1])))


def _my_cols(a, me, width):
    return lax.dynamic_slice_in_dim(a, me * width, width, axis=a.ndim - 1)


def kernel(x, c, ada_w, ada_b, norm_g, ffn_w_gate, ffn_w_up, ffn_w_down, w_in, w_out, fox_f_bias, fox_out_norm, gdn_conv, gdn_A_log, gdn_dt_bias, gdn_out_norm, final_norm, loss_target, m_ada_w, m_ada_b, m_norm_g, m_ffn_w_gate, m_ffn_w_up, m_ffn_w_down, m_w_in, m_w_out, m_fox_f_bias, m_fox_out_norm, m_gdn_conv, m_gdn_A_log, m_gdn_dt_bias, m_gdn_out_norm, m_final_norm, v_ada_w, v_ada_b, v_norm_g, v_ffn_w_gate, v_ffn_w_up, v_ffn_w_down, v_w_in, v_w_out, v_fox_f_bias, v_fox_out_norm, v_gdn_conv, v_gdn_A_log, v_gdn_dt_bias, v_gdn_out_norm, v_final_norm):
    me = _linear(_mesh_pos())
    x0 = x[0]
    s, d = x0.shape
    heads = d // (2 * HEAD_DIM)
    fw = heads * HEAD_DIM
    ng = norm_g.shape[-1]
    ncv = gdn_conv.shape[-1]
    nada = ada_w.shape[-1]
    in_w = w_in.shape[-1] * N_DEV
    in_pad = -(-in_w // 512) * 512

    pack = jnp.concatenate([c, norm_g[0].reshape(1, 3 * ng), gdn_conv[0].reshape(1, CONV_W * ncv)], axis=1)
    pack_all = _gather_row(pack, "gather_small_params")
    c_all = pack_all[:, :d]
    g_all = pack_all[:, d:d + 3 * ng].reshape(N_DEV, 3, ng).transpose(1, 0, 2).reshape(3, d)
    conv_all = pack_all[:, d + 3 * ng:].reshape(N_DEV, CONV_W, ncv).transpose(1, 0, 2).reshape(CONV_W, 3 * fw)

    mod_blk = _ada_fwd(c_all, ada_w[0], _my_cols(ada_b, me, nada))
    mod_all = _exchange([mod_blk], scatter=False, in_vmem=True, name="gather_mod")[0]
    mod = lax.dynamic_slice_in_dim(mod_all, me, 1, axis=1).reshape(N_MOD, d)
    sh1, sc1, gt1, sh2, sc2, gt2, sh3, sc3, gt3 = [mod[i:i + 1] for i in range(N_MOD)]

    wg_all, wu_all, wd_all, win_g, wout_g = _exchange(
        [ffn_w_gate[0].astype(BF16), ffn_w_up[0].astype(BF16), ffn_w_down[0].astype(BF16),
         w_in[0].astype(BF16), w_out[0].astype(BF16)], scatter=False, in_vmem=False, name="gather_weights")
    win_full = win_g.transpose(1, 0, 2).reshape(d, in_w)
    o_f, o_qkv, o_a, o_z = 3 * fw, 3 * fw + heads, 6 * fw + heads, 6 * fw + 3 * heads
    win_al = jnp.concatenate(
        [win_full[:, :o_f], win_full[:, o_qkv:o_a], win_full[:, o_z:], win_full[:, o_f:o_qkv],
         win_full[:, o_a:o_z], jnp.zeros((d, in_pad - in_w), BF16)], axis=1)
    wout_full = wout_g.reshape(d, d)
    small_blk = 7 * heads

    bias_lane = _pad_lanes(fox_f_bias)
    a_lane = _pad_lanes(gdn_A_log, at=heads)
    dt_lane = _pad_lanes(gdn_dt_bias, at=heads)

    h1 = _norm_mod(x0, g_all[0:1], sc1, sh1, "norm_mod_1")
    a1, b1, s1 = _ffn_up(h1, wg_all, wu_all, 0, "ffn1_up")
    f1, x1 = _ffn_down(s1, wd_all, 0, x0, gt1, "ffn1_down")

    h2 = _norm_mod(x1, g_all[1:2], sc2, sh2, "norm_mod_2")
    proj = _mm(h2, win_al, name="in_proj", tn=1536)
    cum = _fox_gate(proj, small_blk, bias_lane)
    cum_t = cum[:, :heads].T
    cum_row = cum_t[:, None, :]
    cum_col = jnp.broadcast_to(cum_t[:, :, None], (heads, s, LANES))
    o_raw, lse, o_fox = _fox_fwd(proj, cum_col, cum_row, fox_out_norm, heads)
    qg, kg, vg = _gdn_pre(proj, conv_all, heads)
    o_gdn, states = _gdn_fwd(qg, kg, vg, proj, 6, small_blk, a_lane, dt_lane, gdn_out_norm)
    o_cat = jnp.concatenate([o_fox, o_gdn], axis=1)
    mix, x2 = _mm(o_cat, wout_full, name="out_proj", residual=(x1, gt2))

    h3 = _norm_mod(x2, g_all[2:3], sc3, sh3, "norm_mod_3")
    a3, b3, s3 = _ffn_up(h3, wg_all, wu_all, 1, "ffn2_up")
    f3, x3 = _ffn_down(s3, wd_all, 1, x2, gt3, "ffn2_down")

    loss_row, dx3, d_final = _final(x3, loss_target[0], final_norm.reshape(1, d))
    loss = lax.psum(loss_row[0, 0], MESH_AXES)

    df3, dgt3 = _gate_bwd(dx3, f3, gt3, MACARON_W, "ffn2_gate_bwd")
    da3, db3 = _ffn_bwd_act(df3, wd_all, 1, a3, b3, "ffn2_bwd_act")
    dwd2 = _ffn_bwd_wd(s3, df3, "ffn2_bwd_wd")
    dh3 = _ffn_bwd_h(da3, db3, wg_all, wu_all, 1, "ffn2_bwd_h")
    dwg2, dwu2 = _ffn_bwd_wgu(h3, da3, db3, "ffn2_bwd_wgu")
    dx2, dsh3, dsc3, dg3 = _norm_mod_bwd(x2, dh3, dx3, g_all[2:3], sc3, "norm_mod_3_bwd")

    dmix, dgt2 = _gate_bwd(dx2, mix, gt2, 1.0, "mix_gate_bwd")
    do_cat = _mm(dmix, wout_full, tb=True, name="out_proj_bwd_x")
    dwout = _mm(o_cat, dmix, ta=True, name="out_proj_bwd_w", tk=512)
    do_fox, delta, d_foxw = _fox_prep_bwd(do_cat, o_raw, fox_out_norm, heads)
    dq_f, dcum_q = _fox_dq(proj, do_fox, cum_col, cum_row, lse, delta, heads)
    lse_row = lse[:, :, 0][:, None, :]
    delta_row = delta[:, :, 0][:, None, :]
    dk_f, dv_f, dcum_k = _fox_dkv(proj, do_fox, cum_col, cum_row, lse_row, delta_row, heads)
    head_lanes = lambda t: jnp.pad(t[:, :, 0].T, ((0, 0), (0, LANES - heads)))
    dsm_fox, d_fbias = _fox_gate_bwd(head_lanes(dcum_q), head_lanes(dcum_k), proj, small_blk, bias_lane)
    dqg, dkg, dvg, dz, dsm_gdn, d_alog, d_dt, d_gdnw = _gdn_bwd(
        qg, kg, vg, proj, 6, small_blk, a_lane, dt_lane, gdn_out_norm, states, do_cat, 1)
    dxc, d_conv = _gdn_pre_bwd_act(proj, conv_all, dqg, dkg, dvg, heads)
    dqkv = _gdn_pre_bwd_conv(dxc, conv_all)
    dsmall = (dsm_fox + dsm_gdn).astype(BF16)
    dproj = jnp.concatenate([dq_f, dk_f, dv_f, dqkv, dz, dsmall, jnp.zeros((s, in_pad - 7 * fw - LANES), BF16)], axis=1)
    dh2 = _mm(dproj, win_al, tb=True, name="in_proj_bwd_x", tk=1536)
    dwin_al = _mm(h2, dproj, ta=True, name="in_proj_bwd_w", tn=1536, tk=512)
    dx1, dsh2, dsc2, dg2 = _norm_mod_bwd(x1, dh2, dx2, g_all[1:2], sc2, "norm_mod_2_bwd")

    df1, dgt1 = _gate_bwd(dx1, f1, gt1, MACARON_W, "ffn1_gate_bwd")
    da1, db1 = _ffn_bwd_act(df1, wd_all, 0, a1, b1, "ffn1_bwd_act")
    dwd1 = _ffn_bwd_wd(s1, df1, "ffn1_bwd_wd")
    dh1 = _ffn_bwd_h(da1, db1, wg_all, wu_all, 0, "ffn1_bwd_h")
    dwg1, dwu1 = _ffn_bwd_wgu(h1, da1, db1, "ffn1_bwd_wgu")
    grad_x, dsh1, dsc1, dg1 = _norm_mod_bwd(x0, dh1, dx1, g_all[0:1], sc1, "norm_mod_1_bwd")

    dmod = jnp.concatenate([dsh1, dsc1, dgt1, dsh2, dsc2, dgt2, dsh3, dsc3, dgt3], axis=1)
    dmod_all = _gather_row(dmod, "gather_dmod")
    ct_pad = jnp.pad(c_all.T, ((0, 0), (0, LANES - N_DEV)))
    dmod_mine = jnp.pad(_my_cols(dmod_all, me, nada), ((0, LANES - N_DEV), (0, 0)))
    g_ada_w = _ada_bwd(ct_pad, dmod_mine)

    g_small_cols = [d_fbias, d_foxw, d_alog[:, heads:], d_dt[:, heads:], d_gdnw]
    small_part = jnp.concatenate(
        [_pad_lanes(v[:, :LANES]) for v in g_small_cols]
        + [d_final, dg1, dg2, dg3] + [d_conv[k:k + 1] for k in range(CONV_W)], axis=1)
    small_all = _gather_row(small_part, "gather_small_grads")
    off = 5 * LANES
    w_small = jnp.concatenate(
        [_pad_lanes(fox_f_bias), fox_out_norm, _pad_lanes(gdn_A_log), _pad_lanes(gdn_dt_bias), gdn_out_norm,
         final_norm.reshape(1, d)], axis=1)
    m_small = jnp.concatenate(
        [_pad_lanes(m_fox_f_bias), m_fox_out_norm, _pad_lanes(m_gdn_A_log), _pad_lanes(m_gdn_dt_bias),
         m_gdn_out_norm, m_final_norm.reshape(1, d)], axis=1)
    v_small = jnp.concatenate(
        [_pad_lanes(v_fox_f_bias), v_fox_out_norm, _pad_lanes(v_gdn_A_log), _pad_lanes(v_gdn_dt_bias),
         v_gdn_out_norm, v_final_norm.reshape(1, d)], axis=1)
    rep = _adamw(small_all[:, None, :off + d], w_small, m_small, v_small, "adamw_replicated")
    ab = _adamw(dmod_all[:, None, :], ada_b, m_ada_b, v_ada_b, "adamw_ada_b")
    g_ng = small_all[:, off + d:off + 4 * d].reshape(N_DEV, 3, d)
    ngs = _adamw(_my_cols(g_ng, me, ng), norm_g[0], m_norm_g[0], v_norm_g[0], "adamw_norm_g")
    g_cv = small_all[:, off + 4 * d:].reshape(N_DEV, CONV_W, 3 * fw)
    cvs = _adamw(_my_cols(g_cv, me, ncv), gdn_conv[0], m_gdn_conv[0], v_gdn_conv[0], "adamw_gdn_conv")

    dwin_full = jnp.concatenate(
        [dwin_al[:, :o_f], dwin_al[:, 7 * fw:7 * fw + heads], dwin_al[:, o_f:o_f + 3 * fw],
         dwin_al[:, 7 * fw + heads:7 * fw + 3 * heads], dwin_al[:, 6 * fw:7 * fw]], axis=1)
    dwin_parts = dwin_full.reshape(d, N_DEV, in_w // N_DEV).transpose(1, 0, 2)
    r_wg1, r_wu1, r_wd1, r_wg2, r_wu2, r_wd2, r_win, r_wout = _exchange(
        [dwg1, dwu1, dwd1, dwg2, dwu2, dwd2, dwin_parts, dwout.reshape(N_DEV, d // N_DEV, d)],
        scatter=True, in_vmem=False, name="exchange_weight_grads")

    def adam2(r0, r1, w, m, v, name):
        outs0 = _adamw(r0, w[0, 0], m[0, 0], v[0, 0], name + "_0")
        outs1 = _adamw(r1, w[0, 1], m[0, 1], v[0, 1], name + "_1")
        return [jnp.stack([p, q])[None] for p, q in zip(outs0, outs1)]

    wgs = adam2(r_wg1, r_wg2, ffn_w_gate, m_ffn_w_gate, v_ffn_w_gate, "adamw_w_gate")
    wus = adam2(r_wu1, r_wu2, ffn_w_up, m_ffn_w_up, v_ffn_w_up, "adamw_w_up")
    wds = adam2(r_wd1, r_wd2, ffn_w_down, m_ffn_w_down, v_ffn_w_down, "adamw_w_down")
    wis = [o[None] for o in _adamw(r_win, w_in[0], m_w_in[0], v_w_in[0], "adamw_w_in")]
    wos = [o[None] for o in _adamw(r_wout, w_out[0], m_w_out[0], v_w_out[0], "adamw_w_out")]
    adas = [o[None] for o in _adamw(g_ada_w[None], ada_w[0], m_ada_w[0], v_ada_w[0], "adamw_ada_w")]
    ngs = [o[None] for o in ngs]
    cvs = [o[None] for o in cvs]

    def rep_piece(i, lo, width):
        return rep[i][:, lo:lo + width]

    nh = fox_f_bias.shape[1]
    outs = []
    for i in range(4):
        outs.append([adas[i], ab[i], ngs[i], wgs[i], wus[i], wds[i], wis[i], wos[i],
                     rep_piece(i, 0, nh), rep_piece(i, LANES, HEAD_DIM), cvs[i], rep_piece(i, 2 * LANES, nh),
                     rep_piece(i, 3 * LANES, nh), rep_piece(i, 4 * LANES, HEAD_DIM), rep_piece(i, off, d).reshape(d)])
    return (loss, grad_x[None], *outs[0], *outs[1], *outs[2], *outs[3])
```

```python
import math

import numpy as np
import jax
import jax.numpy as jnp
from jax import lax
from jax.experimental import pallas as pl
from jax.experimental.pallas import tpu as pltpu

F32 = jnp.float32
BF16 = jnp.bfloat16

N_DEV = 8
MESH_AXES = ("x", "y", "c")
LANES = 128
HEAD_DIM = 128
GDN_CHUNK = 64
CONV_W = 4
N_MOD = 9
MACARON_W = 0.5
EPS = 1e-6
NEG = -1e30
VMEM_LIMIT_BYTES = 56 * 2 ** 20
ADAM_BLOCK_BYTES = 4 * 2 ** 20

ADAM_LR = 0.001
ADAM_B1 = 0.9
ADAM_B2 = 0.999
ADAM_EPS = 1e-08
ADAM_WD = 0.01
ADAM_STEP = 10

MESH_ID = pl.DeviceIdType.MESH
ANY = pl.BlockSpec(memory_space=pl.ANY)
VMEM = pl.BlockSpec(memory_space=pltpu.VMEM)


def _pcall(body, **kw):
    return pl.pallas_call(body, **kw)


def _params(*semantics):
    return pltpu.CompilerParams(dimension_semantics=semantics, vmem_limit_bytes=VMEM_LIMIT_BYTES)


def _tile(n, pref):
    t = 1 << (max(1, min(n, pref)).bit_length() - 1)
    while n % t:
        t //= 2
    return t if t % 8 == 0 else n


def _sigmoid(x):
    return 1.0 / (1.0 + jnp.exp(-x))


def _dot(a, b, dims):
    return lax.dot_general(a.astype(BF16), b.astype(BF16), (dims, ((), ())), preferred_element_type=F32)


NN = ((1,), (0,))
NT = ((1,), (1,))
TN = ((0,), (0,))


def _split3(x):
    hi = x.astype(BF16)
    r1 = x - hi.astype(F32)
    mid = r1.astype(BF16)
    lo = (r1 - mid.astype(F32)).astype(BF16)
    return hi, mid, lo


def _dot_exact_lhs(m_bf16, x, dims=NN):
    hi, mid, lo = _split3(x)
    d = lambda p: lax.dot_general(m_bf16, p, (dims, ((), ())), preferred_element_type=F32)
    return d(hi) + (d(mid) + d(lo))


def _dot_hp(a, b, dims):
    ah = a.astype(BF16)
    al = (a - ah.astype(F32)).astype(BF16)
    bh = b.astype(BF16)
    bl = (b - bh.astype(F32)).astype(BF16)
    d = lambda p, q: lax.dot_general(p, q, (dims, ((), ())), preferred_element_type=F32)
    return d(ah, bh) + (d(ah, bl) + d(al, bh))


def _mesh_pos():
    return lax.axis_index("x"), lax.axis_index("y"), lax.axis_index("c")


def _peer(pos, mask):
    x, y, c = pos
    return (1 - x if mask & 4 else x, 1 - y if mask & 2 else y, 1 - c if mask & 1 else c)


def _linear(pos):
    return 4 * pos[0] + 2 * pos[1] + pos[2]


def _exchange_body(n, scatter):
    def body(*refs):
        ins, outs = refs[:n], refs[n:2 * n]
        send_sems, recv_sems, local_sems = refs[2 * n:]
        pos = _mesh_pos()
        me = _linear(pos)
        local = []
        for i in range(n):
            src = ins[i].at[me] if scatter else ins[i]
            cp = pltpu.make_async_copy(src, outs[i].at[me], local_sems.at[i])
            cp.start()
            local.append(cp)
        sends = []
        for mask in range(1, N_DEV):
            peer = _peer(pos, mask)
            for i in range(n):
                src = ins[i].at[_linear(peer)] if scatter else ins[i]
                cp = pltpu.make_async_remote_copy(
                    src_ref=src, dst_ref=outs[i].at[me],
                    send_sem=send_sems.at[i, mask - 1], recv_sem=recv_sems.at[i, mask - 1],
                    device_id=peer, device_id_type=MESH_ID)
                cp.start()
                sends.append(cp)
        for mask in range(1, N_DEV):
            peer = _peer(pos, mask)
            for i in range(n):
                src = ins[i].at[me] if scatter else ins[i]
                pltpu.make_async_remote_copy(
                    src_ref=src, dst_ref=outs[i].at[_linear(peer)],
                    send_sem=send_sems.at[i, mask - 1], recv_sem=recv_sems.at[i, mask - 1],
                    device_id=peer, device_id_type=MESH_ID).wait_recv()
        for cp in sends:
            cp.wait_send()
        for cp in local:
            cp.wait()

    return body


def _exchange(arrays, *, scatter, in_vmem, name):
    n = len(arrays)
    shapes = [a.shape if scatter else (N_DEV,) + a.shape for a in arrays]
    spec = VMEM if in_vmem else ANY
    outs = _pcall(
        _exchange_body(n, scatter), name=name,
        out_shape=[jax.ShapeDtypeStruct(s, a.dtype) for s, a in zip(shapes, arrays)],
        in_specs=[spec] * n, out_specs=[spec] * n,
        scratch_shapes=[pltpu.SemaphoreType.DMA((n, N_DEV - 1)), pltpu.SemaphoreType.DMA((n, N_DEV - 1)),
                        pltpu.SemaphoreType.DMA((n,))],
    )(*arrays)
    return list(outs)


def _gather_row(v, name):
    return _exchange([v], scatter=False, in_vmem=True, name=name)[0].reshape(N_DEV, v.shape[1])


def _ada_fwd(c_all, w, b):
    d, n = w.shape
    tn = _tile(n, 256)

    def body(c_ref, w_ref, b_ref, o_ref):
        cv = c_ref[...]
        cond = cv * _sigmoid(cv)
        o_ref[...] = _dot_hp(cond, w_ref[...], NN) + b_ref[...]

    return _pcall(
        body, name="ada_fwd", grid=(n // tn,),
        in_specs=[pl.BlockSpec((N_DEV, d), lambda j: (0, 0)), pl.BlockSpec((d, tn), lambda j: (0, j)),
                  pl.BlockSpec((1, tn), lambda j: (0, j))],
        out_specs=pl.BlockSpec((N_DEV, tn), lambda j: (0, j)),
        out_shape=jax.ShapeDtypeStruct((N_DEV, n), F32), compiler_params=_params("parallel"),
    )(c_all, w, b)


def _ada_bwd(ct_pad, dmod_pad):
    d = ct_pad.shape[0]
    n = dmod_pad.shape[1]
    tn = _tile(n, 256)

    def body(c_ref, g_ref, o_ref):
        cv = c_ref[...]
        cond = cv * _sigmoid(cv)
        o_ref[...] = _dot_hp(cond, g_ref[...], NN)

    return _pcall(
        body, name="ada_bwd", grid=(n // tn,),
        in_specs=[pl.BlockSpec((d, LANES), lambda j: (0, 0)), pl.BlockSpec((LANES, tn), lambda j: (0, j))],
        out_specs=pl.BlockSpec((d, tn), lambda j: (0, j)),
        out_shape=jax.ShapeDtypeStruct((d, n), F32), compiler_params=_params("parallel"),
    )(ct_pad, dmod_pad)


def _norm_mod(x, g, sc, sh, name):
    s, d = x.shape
    ts = _tile(s, 512)

    def body(x_ref, g_ref, sc_ref, sh_ref, h_ref):
        xv = x_ref[...]
        r = lax.rsqrt(jnp.mean(xv * xv, axis=-1, keepdims=True) + EPS)
        h_ref[...] = (xv * r * g_ref[...] * (1.0 + sc_ref[...]) + sh_ref[...]).astype(BF16)

    row = pl.BlockSpec((1, d), lambda i: (0, 0))
    return _pcall(
        body, name=name, grid=(s // ts,),
        in_specs=[pl.BlockSpec((ts, d), lambda i: (i, 0)), row, row, row],
        out_specs=pl.BlockSpec((ts, d), lambda i: (i, 0)),
        out_shape=jax.ShapeDtypeStruct((s, d), BF16), compiler_params=_params("parallel"),
    )(x, g, sc, sh)


def _norm_mod_bwd(x, dh, dx_out, g, sc, name):
    s, d = x.shape
    ts = _tile(s, 512)

    def body(x_ref, dh_ref, dxo_ref, g_ref, sc_ref, dx_ref, dsh_ref, dsc_ref, dg_ref):
        @pl.when(pl.program_id(0) == 0)
        def _():
            dsh_ref[...] = jnp.zeros_like(dsh_ref)
            dsc_ref[...] = jnp.zeros_like(dsc_ref)
            dg_ref[...] = jnp.zeros_like(dg_ref)

        xv = x_ref[...]
        dh_v = dh_ref[...]
        gv = g_ref[...]
        one_sc = 1.0 + sc_ref[...]
        r = lax.rsqrt(jnp.mean(xv * xv, axis=-1, keepdims=True) + EPS)
        xn = xv * r
        dxn = dh_v * (gv * one_sc)
        dx_ref[...] = dxo_ref[...] + r * (dxn - xn * jnp.mean(dxn * xn, axis=-1, keepdims=True))
        t = dh_v * xn
        dsh_ref[...] += jnp.sum(dh_v, axis=0, keepdims=True)
        dsc_ref[...] += jnp.sum(t * gv, axis=0, keepdims=True)
        dg_ref[...] += jnp.sum(t * one_sc, axis=0, keepdims=True)

    blk = pl.BlockSpec((ts, d), lambda i: (i, 0))
    row = pl.BlockSpec((1, d), lambda i: (0, 0))
    return _pcall(
        body, name=name, grid=(s // ts,),
        in_specs=[blk, blk, blk, row, row], out_specs=[blk, row, row, row],
        out_shape=[jax.ShapeDtypeStruct((s, d), F32)] + [jax.ShapeDtypeStruct((1, d), F32)] * 3,
        compiler_params=_params("arbitrary"),
    )(x, dh, dx_out, g, sc)


def _gate_bwd(dx, f, gt, k, name):
    s, d = dx.shape
    ts = _tile(s, 512)

    def body(dx_ref, f_ref, gt_ref, df_ref, dgt_ref):
        @pl.when(pl.program_id(0) == 0)
        def _():
            dgt_ref[...] = jnp.zeros_like(dgt_ref)

        dxv = dx_ref[...]
        df_ref[...] = ((k * gt_ref[...]) * dxv).astype(BF16)
        dgt_ref[...] += k * jnp.sum(f_ref[...] * dxv, axis=0, keepdims=True)

    blk = pl.BlockSpec((ts, d), lambda i: (i, 0))
    row = pl.BlockSpec((1, d), lambda i: (0, 0))
    return _pcall(
        body, name=name, grid=(s // ts,),
        in_specs=[blk, blk, row], out_specs=[blk, row],
        out_shape=[jax.ShapeDtypeStruct((s, d), BF16), jax.ShapeDtypeStruct((1, d), F32)],
        compiler_params=_params("arbitrary"),
    )(dx, f, gt)


def _ffn_up(h, wg, wu, layer, name):
    s, d = h.shape
    fs = wg.shape[-1]
    tm = _tile(s, 512)

    def body(h_ref, wg_ref, wu_ref, a_ref, b_ref, s_ref):
        hv = h_ref[...]
        a = jnp.dot(hv, wg_ref[...], preferred_element_type=F32)
        b = jnp.dot(hv, wu_ref[...], preferred_element_type=F32)
        a_ref[...] = a
        b_ref[...] = b
        s_ref[...] = (a * _sigmoid(a) * b).astype(BF16)

    wspec = pl.BlockSpec((None, None, d, fs), lambda j, m: (j, layer, 0, 0))
    ospec = pl.BlockSpec((None, tm, fs), lambda j, m: (j, m, 0))
    return _pcall(
        body, name=name, grid=(N_DEV, s // tm),
        in_specs=[pl.BlockSpec((tm, d), lambda j, m: (m, 0)), wspec, wspec],
        out_specs=[ospec, ospec, ospec],
        out_shape=[jax.ShapeDtypeStruct((N_DEV, s, fs), F32)] * 2 + [jax.ShapeDtypeStruct((N_DEV, s, fs), BF16)],
        compiler_params=_params("parallel", "parallel"),
    )(h, wg, wu)


def _ffn_down(sv, wd, layer, x_in, gt, name):
    _, s, fs = sv.shape
    d = wd.shape[-1]
    tm = _tile(s, 512)

    def body(s_ref, wd_ref, x_ref, gt_ref, f_ref, xo_ref, acc):
        j = pl.program_id(1)

        @pl.when(j == 0)
        def _():
            acc[...] = jnp.zeros_like(acc)

        acc[...] += jnp.dot(s_ref[...], wd_ref[...], preferred_element_type=F32)

        @pl.when(j == N_DEV - 1)
        def _():
            fv = acc[...]
            f_ref[...] = fv
            xo_ref[...] = x_ref[...] + (MACARON_W * gt_ref[...]) * fv

    blk = pl.BlockSpec((tm, d), lambda m, j: (m, 0))
    return _pcall(
        body, name=name, grid=(s // tm, N_DEV),
        in_specs=[pl.BlockSpec((None, tm, fs), lambda m, j: (j, m, 0)),
                  pl.BlockSpec((None, None, fs, d), lambda m, j: (j, layer, 0, 0)),
                  blk, pl.BlockSpec((1, d), lambda m, j: (0, 0))],
        out_specs=[blk, blk],
        out_shape=[jax.ShapeDtypeStruct((s, d), F32)] * 2,
        scratch_shapes=[pltpu.VMEM((tm, d), F32)],
        compiler_params=_params("parallel", "arbitrary"),
    )(sv, wd, x_in, gt)


def _ffn_bwd_act(df, wd, layer, a, b, name):
    s, d = df.shape
    fs = a.shape[-1]
    tm = _tile(s, 512)

    def body(df_ref, wd_ref, a_ref, b_ref, da_ref, db_ref):
        ds = lax.dot_general(df_ref[...], wd_ref[...], (NT, ((), ())), preferred_element_type=F32)
        av = a_ref[...]
        sg = _sigmoid(av)
        da_ref[...] = (ds * b_ref[...] * (sg * (1.0 + av * (1.0 - sg)))).astype(BF16)
        db_ref[...] = (ds * (av * sg)).astype(BF16)

    hid = pl.BlockSpec((None, tm, fs), lambda j, m: (j, m, 0))
    return _pcall(
        body, name=name, grid=(N_DEV, s // tm),
        in_specs=[pl.BlockSpec((tm, d), lambda j, m: (m, 0)),
                  pl.BlockSpec((None, None, fs, d), lambda j, m: (j, layer, 0, 0)), hid, hid],
        out_specs=[hid, hid],
        out_shape=[jax.ShapeDtypeStruct((N_DEV, s, fs), BF16)] * 2,
        compiler_params=_params("parallel", "parallel"),
    )(df, wd, a, b)


def _ffn_bwd_wd(sv, df, name):
    _, s, fs = sv.shape
    d = df.shape[1]
    tk = _tile(s, 512)
    nk = s // tk

    def body(s_ref, df_ref, o_ref, acc):
        @pl.when(pl.program_id(1) == 0)
        def _():
            acc[...] = jnp.zeros_like(acc)

        acc[...] += lax.dot_general(s_ref[...], df_ref[...], (TN, ((), ())), preferred_element_type=F32)

        @pl.when(pl.program_id(1) == nk - 1)
        def _():
            o_ref[...] = acc[...].astype(BF16)

    return _pcall(
        body, name=name, grid=(N_DEV, nk),
        in_specs=[pl.BlockSpec((None, tk, fs), lambda j, k: (j, k, 0)), pl.BlockSpec((tk, d), lambda j, k: (k, 0))],
        out_specs=pl.BlockSpec((None, fs, d), lambda j, k: (j, 0, 0)),
        out_shape=jax.ShapeDtypeStruct((N_DEV, fs, d), BF16),
        scratch_shapes=[pltpu.VMEM((fs, d), F32)],
        compiler_params=_params("parallel", "arbitrary"),
    )(sv, df)


def _ffn_bwd_h(da, db, wg, wu, layer, name):
    _, s, fs = da.shape
    d = wg.shape[-2]
    tm = _tile(s, 1024)

    def body(da_ref, db_ref, wg_ref, wu_ref, o_ref, acc):
        j = pl.program_id(1)

        @pl.when(j == 0)
        def _():
            acc[...] = jnp.zeros_like(acc)

        acc[...] += (lax.dot_general(da_ref[...], wg_ref[...], (NT, ((), ())), preferred_element_type=F32)
                     + lax.dot_general(db_ref[...], wu_ref[...], (NT, ((), ())), preferred_element_type=F32))

        @pl.when(j == N_DEV - 1)
        def _():
            o_ref[...] = acc[...]

    hid = pl.BlockSpec((None, tm, fs), lambda m, j: (j, m, 0))
    wspec = pl.BlockSpec((None, None, d, fs), lambda m, j: (j, layer, 0, 0))
    return _pcall(
        body, name=name, grid=(s // tm, N_DEV),
        in_specs=[hid, hid, wspec, wspec],
        out_specs=pl.BlockSpec((tm, d), lambda m, j: (m, 0)),
        out_shape=jax.ShapeDtypeStruct((s, d), F32),
        scratch_shapes=[pltpu.VMEM((tm, d), F32)],
        compiler_params=_params("parallel", "arbitrary"),
    )(da, db, wg, wu)


def _ffn_bwd_wgu(h, da, db, name):
    s, d = h.shape
    fs = da.shape[-1]
    tk = _tile(s, 512)
    nk = s // tk

    def body(h_ref, da_ref, db_ref, og_ref, ou_ref, accg, accu):
        @pl.when(pl.program_id(1) == 0)
        def _():
            accg[...] = jnp.zeros_like(accg)
            accu[...] = jnp.zeros_like(accu)

        hv = h_ref[...]
        accg[...] += lax.dot_general(hv, da_ref[...], (TN, ((), ())), preferred_element_type=F32)
        accu[...] += lax.dot_general(hv, db_ref[...], (TN, ((), ())), preferred_element_type=F32)

        @pl.when(pl.program_id(1) == nk - 1)
        def _():
            og_ref[...] = accg[...].astype(BF16)
            ou_ref[...] = accu[...].astype(BF16)

    hid = pl.BlockSpec((None, tk, fs), lambda j, k: (j, k, 0))
    ospec = pl.BlockSpec((None, d, fs), lambda j, k: (j, 0, 0))
    return _pcall(
        body, name=name, grid=(N_DEV, nk),
        in_specs=[pl.BlockSpec((tk, d), lambda j, k: (k, 0)), hid, hid],
        out_specs=[ospec, ospec],
        out_shape=[jax.ShapeDtypeStruct((N_DEV, d, fs), BF16)] * 2,
        scratch_shapes=[pltpu.VMEM((d, fs), F32), pltpu.VMEM((d, fs), F32)],
        compiler_params=_params("parallel", "arbitrary"),
    )(h, da, db)


def _mm(a, b, *, ta=False, tb=False, out_dtype=F32, name, tm=1024, tn=1024, tk=2048, residual=None):
    m, kdim = (a.shape[1], a.shape[0]) if ta else a.shape
    n = b.shape[0] if tb else b.shape[1]
    tm, tn, tk = _tile(m, tm), _tile(n, tn), _tile(kdim, tk)
    nk = kdim // tk
    dims = ((0,) if ta else (1,), (1,) if tb else (0,))

    def body(*refs):
        a_ref, b_ref = refs[:2]
        acc = refs[-1]
        kk = pl.program_id(2)

        @pl.when(kk == 0)
        def _():
            acc[...] = jnp.zeros_like(acc)

        acc[...] += lax.dot_general(a_ref[...].astype(BF16), b_ref[...].astype(BF16), (dims, ((), ())),
                                    preferred_element_type=F32)

        @pl.when(kk == nk - 1)
        def _():
            if residual is None:
                refs[2][...] = acc[...].astype(out_dtype)
            else:
                res_ref, gate_ref, y_ref, xo_ref = refs[2:6]
                yv = acc[...]
                y_ref[...] = yv
                xo_ref[...] = res_ref[...] + gate_ref[...] * yv

    a_spec = pl.BlockSpec((tk, tm), lambda i, j, k: (k, i)) if ta else pl.BlockSpec((tm, tk), lambda i, j, k: (i, k))
    b_spec = pl.BlockSpec((tn, tk), lambda i, j, k: (j, k)) if tb else pl.BlockSpec((tk, tn), lambda i, j, k: (k, j))
    o_spec = pl.BlockSpec((tm, tn), lambda i, j, k: (i, j))
    if residual is None:
        in_specs, out_specs = [a_spec, b_spec], o_spec
        out_shape = jax.ShapeDtypeStruct((m, n), out_dtype)
        args = (a, b)
    else:
        in_specs = [a_spec, b_spec, o_spec, pl.BlockSpec((1, tn), lambda i, j, k: (0, j))]
        out_specs = [o_spec, o_spec]
        out_shape = [jax.ShapeDtypeStruct((m, n), F32)] * 2
        args = (a, b) + tuple(residual)
    return _pcall(
        body, name=name, grid=(m // tm, n // tn, nk), in_specs=in_specs, out_specs=out_specs, out_shape=out_shape,
        scratch_shapes=[pltpu.VMEM((tm, tn), F32)],
        compiler_params=_params("parallel", "parallel", "arbitrary"),
    )(*args)


def _log_sigmoid(z):
    return jnp.minimum(z, 0.0) - jnp.log(1.0 + jnp.exp(-jnp.abs(z)))


def _fox_gate(proj, small_blk, bias_lane):
    s = proj.shape[0]
    ts = _tile(s, 1024)
    nsub = ts // LANES

    def body(z_ref, b_ref, cum_ref, carry):
        @pl.when(pl.program_id(0) == 0)
        def _():
            carry[...] = jnp.zeros_like(carry)

        ii = lax.broadcasted_iota(jnp.int32, (LANES, LANES), 0)
        jj = lax.broadcasted_iota(jnp.int32, (LANES, LANES), 1)
        tri = (ii >= jj).astype(BF16)
        logf = _log_sigmoid(z_ref[...] + b_ref[...])
        cv = carry[...]
        for sb in range(nsub):
            blk = logf[sb * LANES:(sb + 1) * LANES, :]
            cum_ref[sb * LANES:(sb + 1) * LANES, :] = _dot_exact_lhs(tri, blk) + cv
            cv = cv + jnp.sum(blk, axis=0, keepdims=True)
        carry[...] = cv

    return _pcall(
        body, name="fox_gate", grid=(s // ts,),
        in_specs=[pl.BlockSpec((ts, LANES), lambda i: (i, small_blk)), pl.BlockSpec((1, LANES), lambda i: (0, 0))],
        out_specs=pl.BlockSpec((ts, LANES), lambda i: (i, 0)),
        out_shape=jax.ShapeDtypeStruct((s, LANES), F32),
        scratch_shapes=[pltpu.VMEM((1, LANES), F32)],
        compiler_params=_params("arbitrary"),
    )(proj, bias_lane)


def _fox_gate_bwd(dcum_q, dcum_k, proj, small_blk, bias_lane):
    s = proj.shape[0]
    ts = _tile(s, 1024)
    nsub = ts // LANES
    nb = s // ts

    def body(dcq_ref, dc_ref, z_ref, b_ref, dz_ref, db_ref, carry):
        @pl.when(pl.program_id(0) == 0)
        def _():
            carry[...] = jnp.zeros_like(carry)
            db_ref[...] = jnp.zeros_like(db_ref)

        ii = lax.broadcasted_iota(jnp.int32, (LANES, LANES), 0)
        jj = lax.broadcasted_iota(jnp.int32, (LANES, LANES), 1)
        triu = (jj >= ii).astype(BF16)
        dc = dcq_ref[...] + dc_ref[...]
        zb = z_ref[...] + b_ref[...]
        cv = carry[...]
        dbv = jnp.zeros((1, LANES), F32)
        for sb in reversed(range(nsub)):
            rows = slice(sb * LANES, (sb + 1) * LANES)
            blk = dc[rows, :]
            dlogf = _dot_exact_lhs(triu, blk) + cv
            cv = cv + jnp.sum(blk, axis=0, keepdims=True)
            dz = dlogf * _sigmoid(-zb[rows, :])
            dz_ref[rows, :] = dz
            dbv = dbv + jnp.sum(dz, axis=0, keepdims=True)
        carry[...] = cv
        db_ref[...] += dbv

    row = pl.BlockSpec((1, LANES), lambda i: (0, 0))
    return _pcall(
        body, name="fox_gate_bwd", grid=(nb,),
        in_specs=[pl.BlockSpec((ts, LANES), lambda i: (nb - 1 - i, 0)),
                  pl.BlockSpec((ts, LANES), lambda i: (nb - 1 - i, 0)),
                  pl.BlockSpec((ts, LANES), lambda i: (nb - 1 - i, small_blk)), row],
        out_specs=[pl.BlockSpec((ts, LANES), lambda i: (nb - 1 - i, 0)), row],
        out_shape=[jax.ShapeDtypeStruct((s, LANES), F32), jax.ShapeDtypeStruct((1, LANES), F32)],
        scratch_shapes=[pltpu.VMEM((1, LANES), F32)],
        compiler_params=_params("arbitrary"),
    )(dcum_q, dcum_k, proj, bias_lane)


def _tri_tables(n, by_key):
    if by_key:
        pairs = [(i, j) for j in range(n) for i in range(j, n)]
    else:
        pairs = [(i, j) for i in range(n) for j in range(i + 1)]
    return (jnp.asarray(np.array([p[0] for p in pairs], np.int32)),
            jnp.asarray(np.array([p[1] for p in pairs], np.int32)))


def _as_row(col):
    t = col.shape[0]
    eye = lax.broadcasted_iota(jnp.int32, (t, t), 0) == lax.broadcasted_iota(jnp.int32, (t, t), 1)
    return jnp.sum(jnp.where(eye, col, 0.0), axis=0, keepdims=True)


def _fox_scores(a, b, bias_col, bias_row, scale, diagonal, rows_are_keys=False):
    sc = lax.dot_general(a.astype(BF16), b.astype(BF16), (NT, ((), ())), preferred_element_type=F32) * scale
    sc = sc + (bias_col + bias_row)
    if not diagonal:
        return sc
    row = lax.broadcasted_iota(jnp.int32, sc.shape, 0)
    col = lax.broadcasted_iota(jnp.int32, sc.shape, 1)
    return jnp.where(row <= col if rows_are_keys else col <= row, sc, NEG)


def _fox_fwd(proj, cum_col, cum_row, w_norm, heads):
    s = proj.shape[0]
    t = _tile(s, 512)
    qi, ki = _tri_tables(s // t, False)
    scale = 1.0 / math.sqrt(HEAD_DIM)

    def body(qi_ref, ki_ref, q_ref, k_ref, v_ref, cq_ref, ck_ref, w_ref, o_ref, lse_ref, lser_ref, on_ref, m_s, acc_s):
        iq, ik = qi_ref[pl.program_id(1)], ki_ref[pl.program_id(1)]

        @pl.when(ik == 0)
        def _():
            m_s[...] = jnp.full_like(m_s, NEG)
            acc_s[...] = jnp.zeros_like(acc_s)

        def step(diagonal):
            sc = _fox_scores(q_ref[...], k_ref[...], cq_ref[:, 0:1], -ck_ref[...], scale, diagonal)
            m_prev = m_s[...]
            m_new = jnp.maximum(m_prev, jnp.max(sc, axis=1, keepdims=True))
            p = jnp.exp(sc - m_new).astype(BF16)
            v_ones = jnp.concatenate([v_ref[...].astype(BF16), jnp.ones((t, LANES), BF16)], axis=1)
            acc_s[...] = jnp.exp(m_prev - m_new) * acc_s[...] + jnp.dot(p, v_ones, preferred_element_type=F32)
            m_s[...] = m_new

        @pl.when(ik < iq)
        def _():
            step(False)

        @pl.when(ik == iq)
        def _():
            step(True)
            acc = acc_s[...]
            o = acc[:, :HEAD_DIM] / acc[:, HEAD_DIM:]
            lse = m_s[...] + jnp.log(acc[:, HEAD_DIM:])
            o_ref[...] = o
            lse_ref[...] = lse
            lser_ref[...] = _as_row(lse[:, 0:1])
            r = lax.rsqrt(jnp.mean(o * o, axis=1, keepdims=True) + EPS)
            on_ref[...] = (o * r * w_ref[...]).astype(BF16)

    qblk = pl.BlockSpec((t, HEAD_DIM), lambda h, p, qi, ki: (qi[p], h))
    kblk = lambda off: pl.BlockSpec((t, HEAD_DIM), lambda h, p, qi, ki: (ki[p], off + h))
    qcol = pl.BlockSpec((None, t, LANES), lambda h, p, qi, ki: (h, qi[p], 0))
    grid_spec = pltpu.PrefetchScalarGridSpec(
        num_scalar_prefetch=2, grid=(heads, qi.shape[0]),
        in_specs=[qblk, kblk(heads), kblk(2 * heads), qcol,
                  pl.BlockSpec((None, 1, t), lambda h, p, qi, ki: (h, 0, ki[p])),
                  pl.BlockSpec((1, HEAD_DIM), lambda h, p, qi, ki: (0, 0))],
        out_specs=[qblk, qcol, pl.BlockSpec((None, 1, t), lambda h, p, qi, ki: (h, 0, qi[p])), qblk],
        scratch_shapes=[pltpu.VMEM((t, 1), F32), pltpu.VMEM((t, 2 * HEAD_DIM), F32)])
    return _pcall(
        body, name="fox_fwd", grid_spec=grid_spec,
        out_shape=[jax.ShapeDtypeStruct((s, heads * HEAD_DIM), F32), jax.ShapeDtypeStruct((heads, s, LANES), F32),
                   jax.ShapeDtypeStruct((heads, 1, s), F32), jax.ShapeDtypeStruct((s, heads * HEAD_DIM), BF16)],
        compiler_params=_params("parallel", "arbitrary"),
    )(qi, ki, proj, proj, proj, cum_col, cum_row, w_norm)


def _fox_prep_bwd(do_cat, o_raw, w_norm, heads):
    s = o_raw.shape[0]
    ts = _tile(s, 512)

    def body(g_ref, o_ref, w_ref, do_ref, delta_ref, deltar_ref, dw_ref):
        @pl.when((pl.program_id(0) == 0) & (pl.program_id(1) == 0))
        def _():
            dw_ref[...] = jnp.zeros_like(dw_ref)

        o = o_ref[...]
        g = g_ref[...]
        r = lax.rsqrt(jnp.mean(o * o, axis=1, keepdims=True) + EPS)
        wg = g * w_ref[...]
        do = r * wg - o * (r * r * r) * jnp.mean(wg * o, axis=1, keepdims=True)
        do_ref[...] = do.astype(BF16)
        delta = jnp.sum(do * o, axis=1, keepdims=True)
        delta_ref[...] = jnp.broadcast_to(delta, delta_ref.shape)
        deltar_ref[...] = _as_row(delta)
        dw_ref[...] += jnp.sum(g * o * r, axis=0, keepdims=True)

    blk = pl.BlockSpec((ts, HEAD_DIM), lambda h, i: (i, h))
    row = pl.BlockSpec((1, HEAD_DIM), lambda h, i: (0, 0))
    return _pcall(
        body, name="fox_prep_bwd", grid=(heads, s // ts),
        in_specs=[blk, blk, row],
        out_specs=[blk, pl.BlockSpec((None, ts, LANES), lambda h, i: (h, i, 0)),
                   pl.BlockSpec((None, 1, ts), lambda h, i: (h, 0, i)), row],
        out_shape=[jax.ShapeDtypeStruct((s, heads * HEAD_DIM), BF16), jax.ShapeDtypeStruct((heads, s, LANES), F32),
                   jax.ShapeDtypeStruct((heads, 1, s), F32), jax.ShapeDtypeStruct((1, HEAD_DIM), F32)],
        compiler_params=_params("arbitrary", "arbitrary"),
    )(do_cat, o_raw, w_norm)


def _fox_dq(proj, do, cum_col, cum_row, lse, delta, heads):
    s = proj.shape[0]
    t = _tile(s, 512)
    qi, ki = _tri_tables(s // t, False)
    scale = 1.0 / math.sqrt(HEAD_DIM)

    def body(qi_ref, ki_ref, q_ref, k_ref, v_ref, do_ref, cq_ref, ck_ref, lse_ref, dl_ref, dq_ref, dc_ref, acc, dc_acc):
        iq, ik = qi_ref[pl.program_id(1)], ki_ref[pl.program_id(1)]

        @pl.when(ik == 0)
        def _():
            acc[...] = jnp.zeros_like(acc)
            dc_acc[...] = jnp.zeros_like(dc_acc)

        def step(diagonal):
            kv = k_ref[...]
            sc = _fox_scores(q_ref[...], kv, cq_ref[:, 0:1] - lse_ref[:, 0:1], -ck_ref[...], scale, diagonal)
            p = jnp.exp(sc)
            dp = _dot(do_ref[...], v_ref[...], NT)
            ds = p * (dp - dl_ref[:, 0:1])
            acc[...] += _dot(ds, kv, NN)
            dc_acc[...] += jnp.sum(ds, axis=1, keepdims=True)

        @pl.when(ik < iq)
        def _():
            step(False)

        @pl.when(ik == iq)
        def _():
            step(True)
            dq_ref[...] = (acc[...] * scale).astype(BF16)
            dc_ref[...] = _as_row(dc_acc[...])

    qblk = pl.BlockSpec((t, HEAD_DIM), lambda h, p, qi, ki: (qi[p], h))
    kblk = lambda off: pl.BlockSpec((t, HEAD_DIM), lambda h, p, qi, ki: (ki[p], off + h))
    qcol = pl.BlockSpec((None, t, LANES), lambda h, p, qi, ki: (h, qi[p], 0))
    grid_spec = pltpu.PrefetchScalarGridSpec(
        num_scalar_prefetch=2, grid=(heads, qi.shape[0]),
        in_specs=[qblk, kblk(heads), kblk(2 * heads), qblk, qcol,
                  pl.BlockSpec((None, 1, t), lambda h, p, qi, ki: (h, 0, ki[p])), qcol, qcol],
        out_specs=[qblk, pl.BlockSpec((None, 1, t), lambda h, p, qi, ki: (h, 0, qi[p]))],
        scratch_shapes=[pltpu.VMEM((t, HEAD_DIM), F32), pltpu.VMEM((t, 1), F32)])
    return _pcall(
        body, name="fox_dq", grid_spec=grid_spec,
        out_shape=[jax.ShapeDtypeStruct((s, heads * HEAD_DIM), BF16), jax.ShapeDtypeStruct((heads, 1, s), F32)],
        compiler_params=_params("parallel", "arbitrary"),
    )(qi, ki, proj, proj, proj, do, cum_col, cum_row, lse, delta)


def _fox_dkv(proj, do, cum_col, cum_row, lse_row, delta_row, heads):
    s = proj.shape[0]
    t = _tile(s, 512)
    nk = s // t
    qi, ki = _tri_tables(nk, True)
    scale = 1.0 / math.sqrt(HEAD_DIM)

    def body(qi_ref, ki_ref, q_ref, k_ref, v_ref, do_ref, cqr_ref, ckc_ref, lse_ref, dl_ref, dk_ref, dv_ref, dc_ref,
             dk_acc, dv_acc, dc_acc):
        iq, ik = qi_ref[pl.program_id(1)], ki_ref[pl.program_id(1)]

        def step(diagonal):
            qv = q_ref[...]
            dov = do_ref[...]
            st = _fox_scores(k_ref[...], qv, -ckc_ref[:, 0:1], cqr_ref[...] - lse_ref[...], scale, diagonal, True)
            pt = jnp.exp(st)
            dv_acc[...] += _dot(pt, dov, NN)
            dpt = _dot(v_ref[...], dov, NT)
            dst = pt * (dpt - dl_ref[...])
            dk_acc[...] += _dot(dst, qv, NN)
            dc_acc[...] += jnp.sum(dst, axis=1, keepdims=True)

        @pl.when(iq == ik)
        def _():
            dk_acc[...] = jnp.zeros_like(dk_acc)
            dv_acc[...] = jnp.zeros_like(dv_acc)
            dc_acc[...] = jnp.zeros_like(dc_acc)
            step(True)

        @pl.when(iq > ik)
        def _():
            step(False)

        @pl.when(iq == nk - 1)
        def _():
            dk_ref[...] = (dk_acc[...] * scale).astype(BF16)
            dv_ref[...] = dv_acc[...].astype(BF16)
            dc_ref[...] = _as_row(-dc_acc[...])

    qblk = pl.BlockSpec((t, HEAD_DIM), lambda h, p, qi, ki: (qi[p], h))
    qrow = pl.BlockSpec((None, 1, t), lambda h, p, qi, ki: (h, 0, qi[p]))
    kblk = lambda off: pl.BlockSpec((t, HEAD_DIM), lambda h, p, qi, ki: (ki[p], off + h))
    kout = pl.BlockSpec((t, HEAD_DIM), lambda h, p, qi, ki: (ki[p], h))
    grid_spec = pltpu.PrefetchScalarGridSpec(
        num_scalar_prefetch=2, grid=(heads, qi.shape[0]),
        in_specs=[qblk, kblk(heads), kblk(2 * heads), qblk, qrow,
                  pl.BlockSpec((None, t, LANES), lambda h, p, qi, ki: (h, ki[p], 0)), qrow, qrow],
        out_specs=[kout, kout, pl.BlockSpec((None, 1, t), lambda h, p, qi, ki: (h, 0, ki[p]))],
        scratch_shapes=[pltpu.VMEM((t, HEAD_DIM), F32), pltpu.VMEM((t, HEAD_DIM), F32), pltpu.VMEM((t, 1), F32)])
    return _pcall(
        body, name="fox_dkv", grid_spec=grid_spec,
        out_shape=[jax.ShapeDtypeStruct((s, heads * HEAD_DIM), BF16)] * 2 + [jax.ShapeDtypeStruct((heads, 1, s), F32)],
        compiler_params=_params("parallel", "arbitrary"),
    )(qi, ki, proj, proj, proj, do, cum_row, cum_col, lse_row, delta_row)


def _shift_rows(xv, halo, j, forward):
    n = xv.shape[0]
    rid = lax.broadcasted_iota(jnp.int32, (8, xv.shape[1]), 0)
    if forward:
        xs = pltpu.roll(xv, n - j, 0)
        hs = pltpu.roll(halo, 8 - j, 0)
        edge = jnp.where(rid >= 8 - j, hs, xs[n - 8:, :])
        return jnp.concatenate([xs[:n - 8, :], edge], axis=0)
    xs = pltpu.roll(xv, j, 0)
    hs = pltpu.roll(halo, j, 0)
    edge = jnp.where(rid < j, hs, xs[:8, :])
    return jnp.concatenate([edge, xs[8:, :]], axis=0)


def _conv_silu(xv, halo, w):
    xc = w[CONV_W - 1:CONV_W, :] * xv
    for j in range(1, CONV_W):
        xc = xc + w[CONV_W - 1 - j:CONV_W - j, :] * _shift_rows(xv, halo, j, False)
    return xc, xc * _sigmoid(xc)


def _gdn_pre(proj, conv_w, heads):
    s = proj.shape[0]
    cw = 3 * heads * HEAD_DIM
    ts = _tile(s, 256)
    tb = ts // 8

    def body(x_ref, halo_ref, w_ref, q_ref, k_ref, v_ref):
        halo = jnp.where(pl.program_id(0) == 0, 0.0, halo_ref[...])
        _, y = _conv_silu(x_ref[...], halo, w_ref[...])
        for h in range(heads):
            for part, ref in enumerate((q_ref, k_ref, v_ref)):
                c0 = (part * heads + h) * HEAD_DIM
                blk = y[:, c0:c0 + HEAD_DIM]
                if part < 2:
                    blk = blk * lax.rsqrt(jnp.sum(blk * blk, axis=1, keepdims=True) + EPS)
                ref[h] = blk

    out = pl.BlockSpec((heads, ts, HEAD_DIM), lambda i: (0, i, 0))
    return _pcall(
        body, name="gdn_pre", grid=(s // ts,),
        in_specs=[pl.BlockSpec((ts, cw), lambda i: (i, 1)),
                  pl.BlockSpec((8, cw), lambda i: (jnp.maximum(i * tb - 1, 0), 1)),
                  pl.BlockSpec((CONV_W, cw), lambda i: (0, 0))],
        out_specs=[out, out, out],
        out_shape=[jax.ShapeDtypeStruct((heads, s, HEAD_DIM), F32)] * 3,
        compiler_params=_params("parallel"),
    )(proj, proj, conv_w)


def _gdn_pre_bwd_act(proj, conv_w, dq, dk, dv, heads):
    s = proj.shape[0]
    cw = 3 * heads * HEAD_DIM
    ts = _tile(s, 256)
    tb = ts // 8

    def body(x_ref, halo_ref, w_ref, dq_ref, dk_ref, dv_ref, dxc_ref, dw_ref):
        @pl.when(pl.program_id(0) == 0)
        def _():
            dw_ref[...] = jnp.zeros_like(dw_ref)

        xv = x_ref[...]
        halo = jnp.where(pl.program_id(0) == 0, 0.0, halo_ref[...])
        xc, y = _conv_silu(xv, halo, w_ref[...])
        sg = _sigmoid(xc)
        dsilu = sg * (1.0 + xc * (1.0 - sg))
        for h in range(heads):
            for part, ref in enumerate((dq_ref, dk_ref, dv_ref)):
                c0 = (part * heads + h) * HEAD_DIM
                g = ref[h]
                if part < 2:
                    blk = y[:, c0:c0 + HEAD_DIM]
                    r = lax.rsqrt(jnp.sum(blk * blk, axis=1, keepdims=True) + EPS)
                    g = r * g - blk * (r * r * r) * jnp.sum(g * blk, axis=1, keepdims=True)
                dxc_ref[:, c0:c0 + HEAD_DIM] = g * dsilu[:, c0:c0 + HEAD_DIM]
        dxc = dxc_ref[...]
        rows = [jnp.sum(dxc * (xv if j == 0 else _shift_rows(xv, halo, j, False)), axis=0, keepdims=True)
                for j in range(CONV_W)]
        dw_ref[...] += jnp.concatenate([rows[CONV_W - 1 - k] for k in range(CONV_W)]
                                       + [jnp.zeros((8 - CONV_W, cw), F32)], axis=0)

    hblk = pl.BlockSpec((heads, ts, HEAD_DIM), lambda i: (0, i, 0))
    return _pcall(
        body, name="gdn_pre_bwd_act", grid=(s // ts,),
        in_specs=[pl.BlockSpec((ts, cw), lambda i: (i, 1)),
                  pl.BlockSpec((8, cw), lambda i: (jnp.maximum(i * tb - 1, 0), 1)),
                  pl.BlockSpec((CONV_W, cw), lambda i: (0, 0)), hblk, hblk, hblk],
        out_specs=[pl.BlockSpec((ts, cw), lambda i: (i, 0)), pl.BlockSpec((8, cw), lambda i: (0, 0))],
        out_shape=[jax.ShapeDtypeStruct((s, cw), F32), jax.ShapeDtypeStruct((8, cw), F32)],
        compiler_params=_params("arbitrary"),
    )(proj, proj, conv_w, dq, dk, dv)


def _gdn_pre_bwd_conv(dxc, conv_w):
    s, cw = dxc.shape
    ts = _tile(s, 256)
    tb = ts // 8
    last = s // 8 - 1

    def body(g_ref, halo_ref, w_ref, dx_ref):
        gv = g_ref[...]
        w = w_ref[...]
        halo = jnp.where(pl.program_id(0) == s // ts - 1, 0.0, halo_ref[...])
        dx = w[CONV_W - 1:CONV_W, :] * gv
        for j in range(1, CONV_W):
            dx = dx + w[CONV_W - 1 - j:CONV_W - j, :] * _shift_rows(gv, halo, j, True)
        dx_ref[...] = dx.astype(BF16)

    return _pcall(
        body, name="gdn_pre_bwd_conv", grid=(s // ts,),
        in_specs=[pl.BlockSpec((ts, cw), lambda i: (i, 0)),
                  pl.BlockSpec((8, cw), lambda i: (jnp.minimum((i + 1) * tb, last), 0)),
                  pl.BlockSpec((CONV_W, cw), lambda i: (0, 0))],
        out_specs=pl.BlockSpec((ts, cw), lambda i: (i, 0)),
        out_shape=jax.ShapeDtypeStruct((s, cw), BF16),
        compiler_params=_params("parallel"),
    )(dxc, dxc, conv_w)


def _bdot(a, b, ca, cb):
    return lax.dot_general(a.astype(BF16), b.astype(BF16), (((ca,), (cb,)), ((0,), (0,))),
                           preferred_element_type=F32)


def _bdot_hp(a, b, ca, cb):
    ah = a.astype(BF16)
    al = (a - ah.astype(F32)).astype(BF16)
    bh = b.astype(BF16)
    bl = (b - bh.astype(F32)).astype(BF16)
    d = lambda p, q: lax.dot_general(p, q, (((ca,), (cb,)), ((0,), (0,))), preferred_element_type=F32)
    return d(ah, bh) + (d(ah, bl) + d(al, bh))


def _gdn_gates(small, a_lane, dt_lane, heads):
    lane = lax.broadcasted_iota(jnp.int32, small.shape, 1)
    za = small + dt_lane
    g_all = -jnp.exp(a_lane) * (jnp.maximum(za, 0.0) + jnp.log(1.0 + jnp.exp(-jnp.abs(za))))
    b_all = _sigmoid(small)
    pick = lambda v, l: jnp.sum(jnp.where(lane == l, v, 0.0), axis=1, keepdims=True)
    g = jnp.stack([pick(g_all, heads + h) for h in range(heads)], axis=0)
    beta = jnp.stack([pick(b_all, 2 * heads + h) for h in range(heads)], axis=0)
    return g, beta


def _chunk_masks(c):
    ii = lax.broadcasted_iota(jnp.int32, (1, c, c), 1)
    jj = lax.broadcasted_iota(jnp.int32, (1, c, c), 2)
    return ii >= jj, ii > jj, ii == jj


def _col_to_row(col, eye):
    return jnp.sum(jnp.where(eye, col, 0.0), axis=1, keepdims=True)


def _row_to_col(row, eye):
    return jnp.sum(jnp.where(eye, row, 0.0), axis=2, keepdims=True)


def _gdn_chunk(q, k, v, g, beta, state):
    c = q.shape[1]
    incl, strict, eye = _chunk_masks(c)
    g_row = _col_to_row(g, eye)
    gc_col = jnp.sum(jnp.where(incl, g_row, 0.0), axis=2, keepdims=True)
    gc_row = _col_to_row(gc_col, eye)
    gam = jnp.where(incl, jnp.exp(jnp.where(incl, gc_col - gc_row, NEG)), 0.0)
    egc = jnp.exp(gc_col)
    kb = k * beta
    vb = v * beta
    kbe = kb * egc
    low = jnp.where(strict, _bdot(kb, k, 2, 2), 0.0) * gam
    p = -low
    tinv = jnp.where(eye, 1.0, 0.0) + p
    width = 2
    while width < c:
        p = _bdot_hp(p, p, 2, 1)
        tinv = tinv + _bdot_hp(tinv, p, 2, 1)
        width *= 2
    u = _bdot(tinv, vb, 2, 1)
    w = _bdot(tinv, kbe, 2, 1)
    att = jnp.where(incl, _bdot(q, k, 2, 2), 0.0) * gam
    vn = u - _bdot(w, state, 2, 1)
    qe = q * egc
    o = _bdot(qe, state, 2, 1) + _bdot(att, vn, 2, 1)
    gl = jnp.sum(g, axis=1, keepdims=True)
    edec = jnp.exp(gl - gc_col)
    kdec = k * edec
    egl = jnp.exp(gl)
    new_state = state * egl + _bdot(kdec, vn, 1, 1)
    return dict(incl=incl, strict=strict, eye=eye, gam=gam, egc=egc, kb=kb, vb=vb, kbe=kbe, low=low, tinv=tinv, w=w,
                att=att, vn=vn, qe=qe, o=o, edec=edec, kdec=kdec, egl=egl, new_state=new_state)


def _gdn_load(q_ref, k_ref, v_ref, small_ref, a_ref, dt_ref, rows, heads):
    q = q_ref[:, rows, :] * (HEAD_DIM ** -0.5)
    g, beta = _gdn_gates(small_ref[rows, :], a_ref[...], dt_ref[...], heads)
    return q, k_ref[:, rows, :], v_ref[:, rows, :], g, beta


def _gdn_fwd(q, k, v, proj, z_blk, small_blk, a_lane, dt_lane, w_norm):
    heads, s, _ = q.shape
    c = min(GDN_CHUNK, s)
    r = _tile(s, 512)
    npb = r // c
    gw = heads * HEAD_DIM

    def body(q_ref, k_ref, v_ref, z_ref, small_ref, a_ref, dt_ref, w_ref, o_ref, st_ref, state):
        @pl.when(pl.program_id(0) == 0)
        def _():
            state[...] = jnp.zeros_like(state)

        def chunk(cb, carry):
            rows = pl.ds(pl.multiple_of(cb * c, c), c)
            qv, kv, vv, g, beta = _gdn_load(q_ref, k_ref, v_ref, small_ref, a_ref, dt_ref, rows, heads)
            st = state[...]
            st_ref[:, cb] = st
            res = _gdn_chunk(qv, kv, vv, g, beta, st)
            state[...] = res["new_state"]
            o = res["o"]
            rn = lax.rsqrt(jnp.mean(o * o, axis=2, keepdims=True) + EPS)
            zv = z_ref[rows, :]
            for h in range(heads):
                zh = zv[:, h * HEAD_DIM:(h + 1) * HEAD_DIM]
                o_ref[rows, h * HEAD_DIM:(h + 1) * HEAD_DIM] = (
                    o[h] * rn[h] * w_ref[...] * (zh * _sigmoid(zh))).astype(BF16)
            return carry

        lax.fori_loop(0, npb, chunk, 0)

    hblk = pl.BlockSpec((heads, r, HEAD_DIM), lambda i: (0, i, 0))
    row = pl.BlockSpec((1, LANES), lambda i: (0, 0))
    return _pcall(
        body, name="gdn_fwd", grid=(s // r,),
        in_specs=[hblk, hblk, hblk, pl.BlockSpec((r, gw), lambda i: (i, z_blk)),
                  pl.BlockSpec((r, LANES), lambda i: (i, small_blk)), row, row, row],
        out_specs=[pl.BlockSpec((r, gw), lambda i: (i, 0)),
                   pl.BlockSpec((heads, npb, HEAD_DIM, HEAD_DIM), lambda i: (0, i, 0, 0))],
        out_shape=[jax.ShapeDtypeStruct((s, gw), BF16),
                   jax.ShapeDtypeStruct((heads, s // c, HEAD_DIM, HEAD_DIM), F32)],
        scratch_shapes=[pltpu.VMEM((heads, HEAD_DIM, HEAD_DIM), F32)],
        compiler_params=_params("arbitrary"),
    )(q, k, v, proj, proj, a_lane, dt_lane, w_norm)


def _gdn_bwd(q, k, v, proj, z_blk, small_blk, a_lane, dt_lane, w_norm, states, do_cat, do_blk):
    heads, s, _ = q.shape
    c = min(GDN_CHUNK, s)
    r = _tile(s, 512)
    npb = r // c
    nb = s // r
    gw = heads * HEAD_DIM

    def body(q_ref, k_ref, v_ref, z_ref, small_ref, a_ref, dt_ref, w_ref, st_ref, do_ref,
             dq_ref, dk_ref, dv_ref, dz_ref, dsm_ref, da_ref, ddt_ref, dw_ref, dstate):
        @pl.when(pl.program_id(0) == 0)
        def _():
            dstate[...] = jnp.zeros_like(dstate)
            da_ref[...] = jnp.zeros_like(da_ref)
            ddt_ref[...] = jnp.zeros_like(ddt_ref)
            dw_ref[...] = jnp.zeros_like(dw_ref)

        def chunk(it, carry):
            cb = npb - 1 - it
            rows = pl.ds(pl.multiple_of(cb * c, c), c)
            qv, kv, vv, g, beta = _gdn_load(q_ref, k_ref, v_ref, small_ref, a_ref, dt_ref, rows, heads)
            st = st_ref[:, cb]
            f = _gdn_chunk(qv, kv, vv, g, beta, st)
            incl, strict, eye = f["incl"], f["strict"], f["eye"]
            o = f["o"]
            wv = w_ref[...]
            zv = z_ref[rows, :]
            dov = do_ref[rows, :]
            rn = lax.rsqrt(jnp.mean(o * o, axis=2, keepdims=True) + EPS)
            do_l, dw_acc = [], jnp.zeros((1, HEAD_DIM), F32)
            for h in range(heads):
                sl = slice(h * HEAD_DIM, (h + 1) * HEAD_DIM)
                zh, gh = zv[:, sl], dov[:, sl]
                sg = _sigmoid(zh)
                on = o[h] * rn[h]
                dz_ref[rows, sl] = (gh * (on * wv) * (sg * (1.0 + zh * (1.0 - sg)))).astype(BF16)
                gn = gh * (zh * sg)
                dw_acc = dw_acc + jnp.sum(gn * on, axis=0, keepdims=True)
                wg = gn * wv
                do_l.append(rn[h] * wg - o[h] * (rn[h] * rn[h] * rn[h]) * jnp.mean(wg * o[h], axis=1, keepdims=True))
            dw_ref[...] += dw_acc
            do = jnp.stack(do_l, axis=0)
            ds_out = dstate[...]
            dvn = _bdot(f["att"], do, 1, 1) + _bdot(f["kdec"], ds_out, 2, 1)
            datt = jnp.where(incl, _bdot(do, f["vn"], 2, 2), 0.0)
            dqe = _bdot(do, st, 2, 2)
            dstate[...] = _bdot(f["qe"], do, 1, 1) + f["egl"] * ds_out - _bdot(f["w"], dvn, 1, 1)
            dw = -_bdot(dvn, st, 2, 2)
            dkdec = _bdot(f["vn"], ds_out, 2, 2)
            t_kdec = jnp.sum(dkdec * f["kdec"], axis=2, keepdims=True)
            dgl = (jnp.sum(jnp.sum(st * ds_out, axis=2, keepdims=True), axis=1, keepdims=True) * f["egl"]
                   + jnp.sum(t_kdec, axis=1, keepdims=True))
            dgc = jnp.sum(dqe * f["qe"], axis=2, keepdims=True) - t_kdec
            dq = dqe * f["egc"]
            dk = dkdec * f["edec"]
            dtinv = _bdot(dvn, f["vb"], 2, 2) + _bdot(dw, f["kbe"], 2, 2)
            dvb = _bdot(f["tinv"], dvn, 1, 1)
            dkbe = _bdot(f["tinv"], dw, 1, 1)
            dkb = dkbe * f["egc"]
            dgc = dgc + jnp.sum(dkbe * f["kbe"], axis=2, keepdims=True)
            dlow = jnp.where(strict, -_bdot_hp(_bdot_hp(f["tinv"], dtinv, 1, 1), f["tinv"], 2, 2), 0.0)
            ml = dlow * f["gam"]
            dkb = dkb + _bdot(ml, kv, 2, 1)
            dk = dk + _bdot(ml, f["kb"], 1, 1)
            ma = datt * f["gam"]
            dq = dq + _bdot(ma, kv, 2, 1)
            dk = dk + _bdot(ma, qv, 1, 1)
            e = dlow * f["low"] + datt * f["att"]
            dgc = dgc + jnp.sum(e, axis=2, keepdims=True) - _row_to_col(jnp.sum(e, axis=1, keepdims=True), eye)
            dk = dk + beta * dkb
            dbeta = jnp.sum(dkb * kv, axis=2, keepdims=True) + jnp.sum(dvb * vv, axis=2, keepdims=True)
            dgc_row = _col_to_row(dgc, eye)
            dg = jnp.sum(jnp.where(incl, 0.0, dgc_row) + jnp.where(eye, dgc_row, 0.0), axis=2, keepdims=True) + dgl
            dq_ref[:, rows, :] = dq * (HEAD_DIM ** -0.5)
            dk_ref[:, rows, :] = dk
            dv_ref[:, rows, :] = beta * dvb
            small = small_ref[rows, :]
            lane = lax.broadcasted_iota(jnp.int32, small.shape, 1)
            dg_l = jnp.zeros(small.shape, F32)
            db_l = jnp.zeros(small.shape, F32)
            for h in range(heads):
                dg_l = dg_l + jnp.where(lane == heads + h, dg[h], 0.0)
                db_l = db_l + jnp.where(lane == 2 * heads + h, dbeta[h], 0.0)
            za = small + dt_ref[...]
            nexp = -jnp.exp(a_ref[...])
            softplus = jnp.maximum(za, 0.0) + jnp.log(1.0 + jnp.exp(-jnp.abs(za)))
            da_logit = dg_l * nexp * _sigmoid(za)
            sb = _sigmoid(small)
            dsm_ref[rows, :] = da_logit + db_l * sb * (1.0 - sb)
            ddt_ref[...] += jnp.sum(da_logit, axis=0, keepdims=True)
            da_ref[...] += jnp.sum(dg_l * nexp * softplus, axis=0, keepdims=True)
            return carry

        lax.fori_loop(0, npb, chunk, 0)

    rev = lambda i: nb - 1 - i
    hblk = pl.BlockSpec((heads, r, HEAD_DIM), lambda i: (0, rev(i), 0))
    row = pl.BlockSpec((1, LANES), lambda i: (0, 0))
    wide = lambda blk: pl.BlockSpec((r, gw), lambda i: (rev(i), blk))
    return _pcall(
        body, name="gdn_bwd", grid=(nb,),
        in_specs=[hblk, hblk, hblk, wide(z_blk), pl.BlockSpec((r, LANES), lambda i: (rev(i), small_blk)),
                  row, row, row, pl.BlockSpec((heads, npb, HEAD_DIM, HEAD_DIM), lambda i: (0, rev(i), 0, 0)),
                  wide(do_blk)],
        out_specs=[hblk, hblk, hblk, wide(0), pl.BlockSpec((r, LANES), lambda i: (rev(i), 0)), row, row, row],
        out_shape=[jax.ShapeDtypeStruct((heads, s, HEAD_DIM), F32)] * 3
        + [jax.ShapeDtypeStruct((s, gw), BF16), jax.ShapeDtypeStruct((s, LANES), F32)]
        + [jax.ShapeDtypeStruct((1, LANES), F32)] * 3,
        scratch_shapes=[pltpu.VMEM((heads, HEAD_DIM, HEAD_DIM), F32)],
        compiler_params=_params("arbitrary"),
    )(q, k, v, proj, proj, a_lane, dt_lane, w_norm, states, do_cat)


def _final(x, target, gf):
    s, d = x.shape
    ts = _tile(s, 512)

    def body(x_ref, t_ref, g_ref, loss_ref, dx_ref, dg_ref):
        @pl.when(pl.program_id(0) == 0)
        def _():
            loss_ref[...] = jnp.zeros_like(loss_ref)
            dg_ref[...] = jnp.zeros_like(dg_ref)

        xv = x_ref[...]
        gv = g_ref[...]
        r = lax.rsqrt(jnp.mean(xv * xv, axis=-1, keepdims=True) + EPS)
        xn = xv * r
        err = xn * gv - t_ref[...]
        per_tok = jnp.mean(err * err, axis=-1, keepdims=True)
        loss_ref[...] += 0.5 * jnp.sum(per_tok, axis=0, keepdims=True)
        dy = err * (1.0 / d)
        dg_ref[...] += jnp.sum(dy * xn, axis=0, keepdims=True)
        dxn = dy * gv
        dx_ref[...] = r * (dxn - xn * jnp.mean(dxn * xn, axis=-1, keepdims=True))

    blk = pl.BlockSpec((ts, d), lambda i: (i, 0))
    row = pl.BlockSpec((1, d), lambda i: (0, 0))
    return _pcall(
        body, name="final_loss", grid=(s // ts,),
        in_specs=[blk, blk, row], out_specs=[pl.BlockSpec((1, LANES), lambda i: (0, 0)), blk, row],
        out_shape=[jax.ShapeDtypeStruct((1, LANES), F32), jax.ShapeDtypeStruct((s, d), F32),
                   jax.ShapeDtypeStruct((1, d), F32)],
        compiler_params=_params("arbitrary"),
    )(x, target, gf)


def _adamw(parts, w, m, v, name):
    npart, rows, cols = parts.shape
    tr = _tile(rows, max(8, ADAM_BLOCK_BYTES // (4 * npart * cols)))
    c1 = 1.0 - ADAM_B1 ** ADAM_STEP
    c2 = 1.0 - ADAM_B2 ** ADAM_STEP

    def body(p_ref, w_ref, m_ref, v_ref, g_ref, d_ref, mo_ref, vo_ref):
        g = p_ref[0].astype(F32)
        for i in range(1, npart):
            g = g + p_ref[i].astype(F32)
        mn = ADAM_B1 * m_ref[...] + (1.0 - ADAM_B1) * g
        vn = ADAM_B2 * v_ref[...] + (1.0 - ADAM_B2) * (g * g)
        g_ref[...] = g
        mo_ref[...] = mn
        vo_ref[...] = vn
        d_ref[...] = -ADAM_LR * ((mn / c1) / (jnp.sqrt(vn / c2) + ADAM_EPS) + ADAM_WD * w_ref[...])

    blk = pl.BlockSpec((tr, cols), lambda i: (i, 0))
    return _pcall(
        body, name=name, grid=(rows // tr,),
        in_specs=[pl.BlockSpec((npart, tr, cols), lambda i: (0, i, 0)), blk, blk, blk],
        out_specs=[blk] * 4, out_shape=[jax.ShapeDtypeStruct((rows, cols), F32)] * 4,
        compiler_params=_params("parallel"),
    )(parts, w, m, v)


def _pad_lanes(v, n=LANES, at=0):
    return jnp.pad(v, ((0, 0), (at, n - at - v.shape[1])))


def _my_cols(a, me, width):
    return lax.dynamic_slice_in_dim(a, me * width, width, axis=a.ndim - 1)


def kernel(x, c, ada_w, ada_b, norm_g, ffn_w_gate, ffn_w_up, ffn_w_down, w_in, w_out, fox_f_bias, fox_out_norm, gdn_conv, gdn_A_log, gdn_dt_bias, gdn_out_norm, final_norm, loss_target, m_ada_w, m_ada_b, m_norm_g, m_ffn_w_gate, m_ffn_w_up, m_ffn_w_down, m_w_in, m_w_out, m_fox_f_bias, m_fox_out_norm, m_gdn_conv, m_gdn_A_log, m_gdn_dt_bias, m_gdn_out_norm, m_final_norm, v_ada_w, v_ada_b, v_norm_g, v_ffn_w_gate, v_ffn_w_up, v_ffn_w_down, v_w_in, v_w_out, v_fox_f_bias, v_fox_out_norm, v_gdn_conv, v_gdn_A_log, v_gdn_dt_bias, v_gdn_out_norm, v_final_norm):
    me = _linear(_mesh_pos())
    x0 = x[0]
    s, d = x0.shape
    heads = d // (2 * HEAD_DIM)
    fw = heads * HEAD_DIM
    ng = norm_g.shape[-1]
    ncv = gdn_conv.shape[-1]
    nada = ada_w.shape[-1]
    in_w = w_in.shape[-1] * N_DEV
    in_pad = -(-in_w // 512) * 512

    pack = jnp.concatenate([c, norm_g[0].reshape(1, 3 * ng), gdn_conv[0].reshape(1, CONV_W * ncv)], axis=1)
    pack_all = _gather_row(pack, "gather_small_params")
    c_all = pack_all[:, :d]
    g_all = pack_all[:, d:d + 3 * ng].reshape(N_DEV, 3, ng).transpose(1, 0, 2).reshape(3, d)
    conv_all = pack_all[:, d + 3 * ng:].reshape(N_DEV, CONV_W, ncv).transpose(1, 0, 2).reshape(CONV_W, 3 * fw)

    mod_blk = _ada_fwd(c_all, ada_w[0], _my_cols(ada_b, me, nada))
    mod_all = _exchange([mod_blk], scatter=False, in_vmem=True, name="gather_mod")[0]
    mod = lax.dynamic_slice_in_dim(mod_all, me, 1, axis=1).reshape(N_MOD, d)
    sh1, sc1, gt1, sh2, sc2, gt2, sh3, sc3, gt3 = [mod[i:i + 1] for i in range(N_MOD)]

    wg_all, wu_all, wd_all, win_g, wout_g = _exchange(
        [ffn_w_gate[0].astype(BF16), ffn_w_up[0].astype(BF16), ffn_w_down[0].astype(BF16),
         w_in[0].astype(BF16), w_out[0].astype(BF16)], scatter=False, in_vmem=False, name="gather_weights")
    win_full = win_g.transpose(1, 0, 2).reshape(d, in_w)
    o_f, o_qkv, o_a, o_z = 3 * fw, 3 * fw + heads, 6 * fw + heads, 6 * fw + 3 * heads
    win_al = jnp.concatenate(
        [win_full[:, :o_f], win_full[:, o_qkv:o_a], win_full[:, o_z:], win_full[:, o_f:o_qkv],
         win_full[:, o_a:o_z], jnp.zeros((d, in_pad - in_w), BF16)], axis=1)
    wout_full = wout_g.reshape(d, d)
    small_blk = 7 * heads

    bias_lane = _pad_lanes(fox_f_bias)
    a_lane = _pad_lanes(gdn_A_log, at=heads)
    dt_lane = _pad_lanes(gdn_dt_bias, at=heads)

    h1 = _norm_mod(x0, g_all[0:1], sc1, sh1, "norm_mod_1")
    a1, b1, s1 = _ffn_up(h1, wg_all, wu_all, 0, "ffn1_up")
    f1, x1 = _ffn_down(s1, wd_all, 0, x0, gt1, "ffn1_down")

    h2 = _norm_mod(x1, g_all[1:2], sc2, sh2, "norm_mod_2")
    proj = _mm(h2, win_al, name="in_proj", tn=1536)
    cum = _fox_gate(proj, small_blk, bias_lane)
    cum_t = cum[:, :heads].T
    cum_row = cum_t[:, None, :]
    cum_col = jnp.broadcast_to(cum_t[:, :, None], (heads, s, LANES))
    o_raw, lse, lse_row, o_fox = _fox_fwd(proj, cum_col, cum_row, fox_out_norm, heads)
    qg, kg, vg = _gdn_pre(proj, conv_all, heads)
    o_gdn, states = _gdn_fwd(qg, kg, vg, proj, 6, small_blk, a_lane, dt_lane, gdn_out_norm)
    o_cat = jnp.concatenate([o_fox, o_gdn], axis=1)
    mix, x2 = _mm(o_cat, wout_full, name="out_proj", residual=(x1, gt2))

    h3 = _norm_mod(x2, g_all[2:3], sc3, sh3, "norm_mod_3")
    a3, b3, s3 = _ffn_up(h3, wg_all, wu_all, 1, "ffn2_up")
    f3, x3 = _ffn_down(s3, wd_all, 1, x2, gt3, "ffn2_down")

    loss_row, dx3, d_final = _final(x3, loss_target[0], final_norm.reshape(1, d))
    loss = lax.psum(loss_row[0, 0], MESH_AXES)

    df3, dgt3 = _gate_bwd(dx3, f3, gt3, MACARON_W, "ffn2_gate_bwd")
    da3, db3 = _ffn_bwd_act(df3, wd_all, 1, a3, b3, "ffn2_bwd_act")
    dwd2 = _ffn_bwd_wd(s3, df3, "ffn2_bwd_wd")
    dh3 = _ffn_bwd_h(da3, db3, wg_all, wu_all, 1, "ffn2_bwd_h")
    dwg2, dwu2 = _ffn_bwd_wgu(h3, da3, db3, "ffn2_bwd_wgu")
    dx2, dsh3, dsc3, dg3 = _norm_mod_bwd(x2, dh3, dx3, g_all[2:3], sc3, "norm_mod_3_bwd")

    dmix, dgt2 = _gate_bwd(dx2, mix, gt2, 1.0, "mix_gate_bwd")
    do_cat = _mm(dmix, wout_full, tb=True, name="out_proj_bwd_x")
    dwout = _mm(o_cat, dmix, ta=True, out_dtype=BF16, name="out_proj_bwd_w", tk=512)
    do_fox, delta, delta_row, d_foxw = _fox_prep_bwd(do_cat, o_raw, fox_out_norm, heads)
    dq_f, dcum_q = _fox_dq(proj, do_fox, cum_col, cum_row, lse, delta, heads)
    dk_f, dv_f, dcum_k = _fox_dkv(proj, do_fox, cum_col, cum_row, lse_row, delta_row, heads)
    head_lanes = lambda t: jnp.pad(t[:, 0, :].T, ((0, 0), (0, LANES - heads)))
    dsm_fox, d_fbias = _fox_gate_bwd(head_lanes(dcum_q), head_lanes(dcum_k), proj, small_blk, bias_lane)
    dqg, dkg, dvg, dz, dsm_gdn, d_alog, d_dt, d_gdnw = _gdn_bwd(
        qg, kg, vg, proj, 6, small_blk, a_lane, dt_lane, gdn_out_norm, states, do_cat, 1)
    dxc, d_conv = _gdn_pre_bwd_act(proj, conv_all, dqg, dkg, dvg, heads)
    dqkv = _gdn_pre_bwd_conv(dxc, conv_all)
    dsmall = (dsm_fox + dsm_gdn).astype(BF16)
    dproj = jnp.concatenate([dq_f, dk_f, dv_f, dqkv, dz, dsmall, jnp.zeros((s, in_pad - 7 * fw - LANES), BF16)], axis=1)
    dh2 = _mm(dproj, win_al, tb=True, name="in_proj_bwd_x", tk=1536)
    dwin_al = _mm(h2, dproj, ta=True, out_dtype=BF16, name="in_proj_bwd_w", tn=1536, tk=512)
    dx1, dsh2, dsc2, dg2 = _norm_mod_bwd(x1, dh2, dx2, g_all[1:2], sc2, "norm_mod_2_bwd")

    df1, dgt1 = _gate_bwd(dx1, f1, gt1, MACARON_W, "ffn1_gate_bwd")
    da1, db1 = _ffn_bwd_act(df1, wd_all, 0, a1, b1, "ffn1_bwd_act")
    dwd1 = _ffn_bwd_wd(s1, df1, "ffn1_bwd_wd")
    dh1 = _ffn_bwd_h(da1, db1, wg_all, wu_all, 0, "ffn1_bwd_h")
    dwg1, dwu1 = _ffn_bwd_wgu(h1, da1, db1, "ffn1_bwd_wgu")
    grad_x, dsh1, dsc1, dg1 = _norm_mod_bwd(x0, dh1, dx1, g_all[0:1], sc1, "norm_mod_1_bwd")

    dmod = jnp.concatenate([dsh1, dsc1, dgt1, dsh2, dsc2, dgt2, dsh3, dsc3, dgt3], axis=1)
    dmod_all = _gather_row(dmod, "gather_dmod")
    ct_pad = jnp.pad(c_all.T, ((0, 0), (0, LANES - N_DEV)))
    dmod_mine = jnp.pad(_my_cols(dmod_all, me, nada), ((0, LANES - N_DEV), (0, 0)))
    g_ada_w = _ada_bwd(ct_pad, dmod_mine)

    g_small_cols = [d_fbias, d_foxw, d_alog[:, heads:], d_dt[:, heads:], d_gdnw]
    small_part = jnp.concatenate(
        [_pad_lanes(v[:, :LANES]) for v in g_small_cols]
        + [d_final, dg1, dg2, dg3] + [d_conv[k:k + 1] for k in range(CONV_W)], axis=1)
    small_all = _gather_row(small_part, "gather_small_grads")
    off = 5 * LANES
    w_small = jnp.concatenate(
        [_pad_lanes(fox_f_bias), fox_out_norm, _pad_lanes(gdn_A_log), _pad_lanes(gdn_dt_bias), gdn_out_norm,
         final_norm.reshape(1, d)], axis=1)
    m_small = jnp.concatenate(
        [_pad_lanes(m_fox_f_bias), m_fox_out_norm, _pad_lanes(m_gdn_A_log), _pad_lanes(m_gdn_dt_bias),
         m_gdn_out_norm, m_final_norm.reshape(1, d)], axis=1)
    v_small = jnp.concatenate(
        [_pad_lanes(v_fox_f_bias), v_fox_out_norm, _pad_lanes(v_gdn_A_log), _pad_lanes(v_gdn_dt_bias),
         v_gdn_out_norm, v_final_norm.reshape(1, d)], axis=1)
    rep = _adamw(small_all[:, None, :off + d], w_small, m_small, v_small, "adamw_replicated")
    ab = _adamw(dmod_all[:, None, :], ada_b, m_ada_b, v_ada_b, "adamw_ada_b")
    g_ng = small_all[:, off + d:off + 4 * d].reshape(N_DEV, 3, d)
    ngs = _adamw(_my_cols(g_ng, me, ng), norm_g[0], m_norm_g[0], v_norm_g[0], "adamw_norm_g")
    g_cv = small_all[:, off + 4 * d:].reshape(N_DEV, CONV_W, 3 * fw)
    cvs = _adamw(_my_cols(g_cv, me, ncv), gdn_conv[0], m_gdn_conv[0], v_gdn_conv[0], "adamw_gdn_conv")

    dwin_full = jnp.concatenate(
        [dwin_al[:, :o_f], dwin_al[:, 7 * fw:7 * fw + heads], dwin_al[:, o_f:o_f + 3 * fw],
         dwin_al[:, 7 * fw + heads:7 * fw + 3 * heads], dwin_al[:, 6 * fw:7 * fw]], axis=1)
    dwin_parts = dwin_full.reshape(d, N_DEV, in_w // N_DEV).transpose(1, 0, 2)
    r_wg1, r_wu1, r_wd1, r_wg2, r_wu2, r_wd2, r_win, r_wout = _exchange(
        [dwg1, dwu1, dwd1, dwg2, dwu2, dwd2, dwin_parts, dwout.reshape(N_DEV, d // N_DEV, d)],
        scatter=True, in_vmem=False, name="exchange_weight_grads")

    def adam2(r0, r1, w, m, v, name):
        outs0 = _adamw(r0, w[0, 0], m[0, 0], v[0, 0], name + "_0")
        outs1 = _adamw(r1, w[0, 1], m[0, 1], v[0, 1], name + "_1")
        return [jnp.stack([p, q])[None] for p, q in zip(outs0, outs1)]

    wgs = adam2(r_wg1, r_wg2, ffn_w_gate, m_ffn_w_gate, v_ffn_w_gate, "adamw_w_gate")
    wus = adam2(r_wu1, r_wu2, ffn_w_up, m_ffn_w_up, v_ffn_w_up, "adamw_w_up")
    wds = adam2(r_wd1, r_wd2, ffn_w_down, m_ffn_w_down, v_ffn_w_down, "adamw_w_down")
    wis = [o[None] for o in _adamw(r_win, w_in[0], m_w_in[0], v_w_in[0], "adamw_w_in")]
    wos = [o[None] for o in _adamw(r_wout, w_out[0], m_w_out[0], v_w_out[0], "adamw_w_out")]
    adas = [o[None] for o in _adamw(g_ada_w[None], ada_w[0], m_ada_w[0], v_ada_w[0], "adamw_ada_w")]
    ngs = [o[None] for o in ngs]
    cvs = [o[None] for o in cvs]

    def rep_piece(i, lo, width):
        return rep[i][:, lo:lo + width]

    nh = fox_f_bias.shape[1]
    outs = []
    for i in range(4):
        outs.append([adas[i], ab[i], ngs[i], wgs[i], wus[i], wds[i], wis[i], wos[i],
                     rep_piece(i, 0, nh), rep_piece(i, LANES, HEAD_DIM), cvs[i], rep_piece(i, 2 * LANES, nh),
                     rep_piece(i, 3 * LANES, nh), rep_piece(i, 4 * LANES, HEAD_DIM), rep_piece(i, off, d).reshape(d)])
    return (loss, grad_x[None], *outs[0], *outs[1], *outs[2], *outs[3])
```

```python
import math

import numpy as np
import jax
import jax.numpy as jnp
from jax import lax
from jax.experimental import pallas as pl
from jax.experimental.pallas import tpu as pltpu

F32 = jnp.float32
BF16 = jnp.bfloat16

N_DEV = 8
MESH_AXES = ("x", "y", "c")
LANES = 128
HEAD_DIM = 128
GDN_CHUNK = 64
CONV_W = 4
N_MOD = 9
MACARON_W = 0.5
EPS = 1e-6
NEG = -1e30
VMEM_LIMIT_BYTES = 56 * 2 ** 20
ADAM_BLOCK_BYTES = 4 * 2 ** 20

ADAM_LR = 0.001
ADAM_B1 = 0.9
ADAM_B2 = 0.999
ADAM_EPS = 1e-08
ADAM_WD = 0.01
ADAM_STEP = 10

MESH_ID = pl.DeviceIdType.MESH
ANY = pl.BlockSpec(memory_space=pl.ANY)
VMEM = pl.BlockSpec(memory_space=pltpu.VMEM)


def _pcall(body, **kw):
    return pl.pallas_call(body, **kw)


def _params(*semantics):
    return pltpu.CompilerParams(dimension_semantics=semantics, vmem_limit_bytes=VMEM_LIMIT_BYTES)


def _tile(n, pref):
    t = 1 << (max(1, min(n, pref)).bit_length() - 1)
    while n % t:
        t //= 2
    return t if t % 8 == 0 else n


def _sigmoid(x):
    return 1.0 / (1.0 + jnp.exp(-x))


def _dot(a, b, dims):
    return lax.dot_general(a.astype(BF16), b.astype(BF16), (dims, ((), ())), preferred_element_type=F32)


NN = ((1,), (0,))
NT = ((1,), (1,))
TN = ((0,), (0,))


def _split3(x):
    hi = x.astype(BF16)
    r1 = x - hi.astype(F32)
    mid = r1.astype(BF16)
    lo = (r1 - mid.astype(F32)).astype(BF16)
    return hi, mid, lo


def _dot_exact_lhs(m_bf16, x, dims=NN):
    hi, mid, lo = _split3(x)
    d = lambda p: lax.dot_general(m_bf16, p, (dims, ((), ())), preferred_element_type=F32)
    return d(hi) + (d(mid) + d(lo))


def _dot_hp(a, b, dims):
    ah = a.astype(BF16)
    al = (a - ah.astype(F32)).astype(BF16)
    bh = b.astype(BF16)
    bl = (b - bh.astype(F32)).astype(BF16)
    d = lambda p, q: lax.dot_general(p, q, (dims, ((), ())), preferred_element_type=F32)
    return d(ah, bh) + (d(ah, bl) + d(al, bh))


def _mesh_pos():
    return lax.axis_index("x"), lax.axis_index("y"), lax.axis_index("c")


def _peer(pos, mask):
    x, y, c = pos
    return (1 - x if mask & 4 else x, 1 - y if mask & 2 else y, 1 - c if mask & 1 else c)


def _linear(pos):
    return 4 * pos[0] + 2 * pos[1] + pos[2]


def _exchange_copies(ins, outs, sems, scatter, with_receives=True):
    send_sems, recv_sems, local_sems = sems
    pos = _mesh_pos()
    me = _linear(pos)
    local, sends, recvs = [], [], []
    for i in range(len(ins)):
        src = ins[i].at[me] if scatter else ins[i]
        local.append(pltpu.make_async_copy(src, outs[i].at[me], local_sems.at[i]))
    for mask in range(1, N_DEV):
        peer = _peer(pos, mask)
        for i in range(len(ins)):
            sem = dict(send_sem=send_sems.at[i, mask - 1], recv_sem=recv_sems.at[i, mask - 1],
                       device_id=peer, device_id_type=MESH_ID)
            sends.append(pltpu.make_async_remote_copy(
                src_ref=ins[i].at[_linear(peer)] if scatter else ins[i], dst_ref=outs[i].at[me], **sem))
            if with_receives:
                recvs.append(pltpu.make_async_remote_copy(
                    src_ref=ins[i].at[me] if scatter else ins[i], dst_ref=outs[i].at[_linear(peer)], **sem))
    return local, sends, recvs


def _exchange_start(ins, outs, sems, scatter):
    local, sends, _ = _exchange_copies(ins, outs, sems, scatter, with_receives=False)
    for cp in local + sends:
        cp.start()


def _exchange_wait(ins, outs, sems, scatter):
    local, sends, recvs = _exchange_copies(ins, outs, sems, scatter)
    for cp in recvs:
        cp.wait_recv()
    for cp in sends:
        cp.wait_send()
    for cp in local:
        cp.wait()


def _exchange_sems(n):
    return [pltpu.SemaphoreType.DMA((n, N_DEV - 1)), pltpu.SemaphoreType.DMA((n, N_DEV - 1)),
            pltpu.SemaphoreType.DMA((n,))]


def _exchange_shapes(arrays, scatter):
    return [jax.ShapeDtypeStruct(a.shape if scatter else (N_DEV,) + a.shape, a.dtype) for a in arrays]


def _exchange(arrays, *, scatter, in_vmem, name):
    n = len(arrays)

    def body(*refs):
        ins, outs, sems = refs[:n], refs[n:2 * n], refs[2 * n:]
        _exchange_start(ins, outs, sems, scatter)
        _exchange_wait(ins, outs, sems, scatter)

    spec = VMEM if in_vmem else ANY
    outs = _pcall(
        body, name=name, out_shape=_exchange_shapes(arrays, scatter),
        in_specs=[spec] * n, out_specs=[spec] * n, scratch_shapes=_exchange_sems(n),
    )(*arrays)
    return list(outs)


def _hosted(body, *, name, grid, in_specs, out_specs, out_shape, args, scratch_shapes=(), prefetch=(), carry=None):
    n_in, n_out, n_scr, n_pre = len(in_specs), len(out_shape), len(scratch_shapes), len(prefetch)
    arrays, scatter = carry if carry is not None else ([], False)
    n = len(arrays)

    def wrapped(*refs):
        pre, r = refs[:n_pre], refs[n_pre:]
        host_in, comm_in = r[:n_in], r[n_in:n_in + n]
        r = r[n_in + n:]
        host_out, comm_out = r[:n_out], r[n_out:n_out + n]
        r = r[n_out + n:]
        host_scr, sems = r[:n_scr], r[n_scr:]
        if n:
            first = pl.program_id(0) == 0
            last = pl.program_id(0) == grid[0] - 1
            for ax in range(1, len(grid)):
                first = first & (pl.program_id(ax) == 0)
                last = last & (pl.program_id(ax) == grid[ax] - 1)

            @pl.when(first)
            def _():
                _exchange_start(comm_in, comm_out, sems, scatter)

        body(*pre, *host_in, *host_out, *host_scr)
        if n:
            @pl.when(last)
            def _():
                _exchange_wait(comm_in, comm_out, sems, scatter)

    grid_spec = pltpu.PrefetchScalarGridSpec(
        num_scalar_prefetch=n_pre, grid=grid, in_specs=list(in_specs) + [ANY] * n,
        out_specs=list(out_specs) + [ANY] * n,
        scratch_shapes=list(scratch_shapes) + (_exchange_sems(n) if n else []))
    outs = _pcall(
        wrapped, name=name, grid_spec=grid_spec, out_shape=list(out_shape) + _exchange_shapes(arrays, scatter),
        compiler_params=_params(*(["arbitrary"] * len(grid))),
    )(*prefetch, *args, *arrays)
    return list(outs[:n_out]), list(outs[n_out:])


def _gather_row(v, name):
    return _exchange([v], scatter=False, in_vmem=True, name=name)[0].reshape(N_DEV, v.shape[1])


def _ada_fwd(c_all, w, b):
    d, n = w.shape
    tn = _tile(n, 256)

    def body(c_ref, w_ref, b_ref, o_ref):
        cv = c_ref[...]
        cond = cv * _sigmoid(cv)
        o_ref[...] = _dot_hp(cond, w_ref[...], NN) + b_ref[...]

    return _pcall(
        body, name="ada_fwd", grid=(n // tn,),
        in_specs=[pl.BlockSpec((N_DEV, d), lambda j: (0, 0)), pl.BlockSpec((d, tn), lambda j: (0, j)),
                  pl.BlockSpec((1, tn), lambda j: (0, j))],
        out_specs=pl.BlockSpec((N_DEV, tn), lambda j: (0, j)),
        out_shape=jax.ShapeDtypeStruct((N_DEV, n), F32), compiler_params=_params("parallel"),
    )(c_all, w, b)


def _ada_bwd(ct_pad, dmod_pad):
    d = ct_pad.shape[0]
    n = dmod_pad.shape[1]
    tn = _tile(n, 256)

    def body(c_ref, g_ref, o_ref):
        cv = c_ref[...]
        cond = cv * _sigmoid(cv)
        o_ref[...] = _dot_hp(cond, g_ref[...], NN)

    return _pcall(
        body, name="ada_bwd", grid=(n // tn,),
        in_specs=[pl.BlockSpec((d, LANES), lambda j: (0, 0)), pl.BlockSpec((LANES, tn), lambda j: (0, j))],
        out_specs=pl.BlockSpec((d, tn), lambda j: (0, j)),
        out_shape=jax.ShapeDtypeStruct((d, n), F32), compiler_params=_params("parallel"),
    )(ct_pad, dmod_pad)


def _norm_mod(x, g, sc, sh, name):
    s, d = x.shape
    ts = _tile(s, 512)

    def body(x_ref, g_ref, sc_ref, sh_ref, h_ref):
        xv = x_ref[...]
        r = lax.rsqrt(jnp.mean(xv * xv, axis=-1, keepdims=True) + EPS)
        h_ref[...] = (xv * r * g_ref[...] * (1.0 + sc_ref[...]) + sh_ref[...]).astype(BF16)

    row = pl.BlockSpec((1, d), lambda i: (0, 0))
    return _pcall(
        body, name=name, grid=(s // ts,),
        in_specs=[pl.BlockSpec((ts, d), lambda i: (i, 0)), row, row, row],
        out_specs=pl.BlockSpec((ts, d), lambda i: (i, 0)),
        out_shape=jax.ShapeDtypeStruct((s, d), BF16), compiler_params=_params("parallel"),
    )(x, g, sc, sh)


def _norm_mod_bwd(x, dh, dx_out, g, sc, name):
    s, d = x.shape
    ts = _tile(s, 512)

    def body(x_ref, dh_ref, dxo_ref, g_ref, sc_ref, dx_ref, dsh_ref, dsc_ref, dg_ref):
        @pl.when(pl.program_id(0) == 0)
        def _():
            dsh_ref[...] = jnp.zeros_like(dsh_ref)
            dsc_ref[...] = jnp.zeros_like(dsc_ref)
            dg_ref[...] = jnp.zeros_like(dg_ref)

        xv = x_ref[...]
        dh_v = dh_ref[...]
        gv = g_ref[...]
        one_sc = 1.0 + sc_ref[...]
        r = lax.rsqrt(jnp.mean(xv * xv, axis=-1, keepdims=True) + EPS)
        xn = xv * r
        dxn = dh_v * (gv * one_sc)
        dx_ref[...] = dxo_ref[...] + r * (dxn - xn * jnp.mean(dxn * xn, axis=-1, keepdims=True))
        t = dh_v * xn
        dsh_ref[...] += jnp.sum(dh_v, axis=0, keepdims=True)
        dsc_ref[...] += jnp.sum(t * gv, axis=0, keepdims=True)
        dg_ref[...] += jnp.sum(t * one_sc, axis=0, keepdims=True)

    blk = pl.BlockSpec((ts, d), lambda i: (i, 0))
    row = pl.BlockSpec((1, d), lambda i: (0, 0))
    return _pcall(
        body, name=name, grid=(s // ts,),
        in_specs=[blk, blk, blk, row, row], out_specs=[blk, row, row, row],
        out_shape=[jax.ShapeDtypeStruct((s, d), F32)] + [jax.ShapeDtypeStruct((1, d), F32)] * 3,
        compiler_params=_params("arbitrary"),
    )(x, dh, dx_out, g, sc)


def _gate_bwd(dx, f, gt, k, name):
    s, d = dx.shape
    ts = _tile(s, 512)

    def body(dx_ref, f_ref, gt_ref, df_ref, dgt_ref):
        @pl.when(pl.program_id(0) == 0)
        def _():
            dgt_ref[...] = jnp.zeros_like(dgt_ref)

        dxv = dx_ref[...]
        df_ref[...] = ((k * gt_ref[...]) * dxv).astype(BF16)
        dgt_ref[...] += k * jnp.sum(f_ref[...] * dxv, axis=0, keepdims=True)

    blk = pl.BlockSpec((ts, d), lambda i: (i, 0))
    row = pl.BlockSpec((1, d), lambda i: (0, 0))
    return _pcall(
        body, name=name, grid=(s // ts,),
        in_specs=[blk, blk, row], out_specs=[blk, row],
        out_shape=[jax.ShapeDtypeStruct((s, d), BF16), jax.ShapeDtypeStruct((1, d), F32)],
        compiler_params=_params("arbitrary"),
    )(dx, f, gt)


def _ffn_up(h, wg, wu, layer, name, carry=None):
    s, d = h.shape
    fs = wg.shape[-1]
    tm = _tile(s, 512)

    def body(h_ref, wg_ref, wu_ref, a_ref, b_ref, s_ref):
        hv = h_ref[...]
        a = jnp.dot(hv, wg_ref[...], preferred_element_type=F32)
        b = jnp.dot(hv, wu_ref[...], preferred_element_type=F32)
        a_ref[...] = a
        b_ref[...] = b
        s_ref[...] = (a * _sigmoid(a) * b).astype(BF16)

    wspec = pl.BlockSpec((None, None, d, fs), lambda j, m: (j, layer, 0, 0))
    ospec = pl.BlockSpec((None, tm, fs), lambda j, m: (j, m, 0))
    return _hosted(
        body, name=name, grid=(N_DEV, s // tm),
        in_specs=[pl.BlockSpec((tm, d), lambda j, m: (m, 0)), wspec, wspec],
        out_specs=[ospec, ospec, ospec],
        out_shape=[jax.ShapeDtypeStruct((N_DEV, s, fs), F32)] * 2 + [jax.ShapeDtypeStruct((N_DEV, s, fs), BF16)],
        args=(h, wg, wu), carry=carry)


def _ffn_down(sv, wd, layer, x_in, gt, name, carry=None):
    _, s, fs = sv.shape
    d = wd.shape[-1]
    tm = _tile(s, 512)

    def body(s_ref, wd_ref, x_ref, gt_ref, f_ref, xo_ref, acc):
        j = pl.program_id(1)

        @pl.when(j == 0)
        def _():
            acc[...] = jnp.zeros_like(acc)

        acc[...] += jnp.dot(s_ref[...], wd_ref[...], preferred_element_type=F32)

        @pl.when(j == N_DEV - 1)
        def _():
            fv = acc[...]
            f_ref[...] = fv
            xo_ref[...] = x_ref[...] + (MACARON_W * gt_ref[...]) * fv

    blk = pl.BlockSpec((tm, d), lambda m, j: (m, 0))
    return _hosted(
        body, name=name, grid=(s // tm, N_DEV),
        in_specs=[pl.BlockSpec((None, tm, fs), lambda m, j: (j, m, 0)),
                  pl.BlockSpec((None, None, fs, d), lambda m, j: (j, layer, 0, 0)),
                  blk, pl.BlockSpec((1, d), lambda m, j: (0, 0))],
        out_specs=[blk, blk],
        out_shape=[jax.ShapeDtypeStruct((s, d), F32)] * 2,
        scratch_shapes=[pltpu.VMEM((tm, d), F32)],
        args=(sv, wd, x_in, gt), carry=carry)


def _ffn_bwd_act(df, wd, layer, a, b, name, carry=None):
    s, d = df.shape
    fs = a.shape[-1]
    tm = _tile(s, 512)

    def body(df_ref, wd_ref, a_ref, b_ref, da_ref, db_ref):
        ds = lax.dot_general(df_ref[...], wd_ref[...], (NT, ((), ())), preferred_element_type=F32)
        av = a_ref[...]
        sg = _sigmoid(av)
        da_ref[...] = (ds * b_ref[...] * (sg * (1.0 + av * (1.0 - sg)))).astype(BF16)
        db_ref[...] = (ds * (av * sg)).astype(BF16)

    hid = pl.BlockSpec((None, tm, fs), lambda j, m: (j, m, 0))
    return _hosted(
        body, name=name, grid=(N_DEV, s // tm),
        in_specs=[pl.BlockSpec((tm, d), lambda j, m: (m, 0)),
                  pl.BlockSpec((None, None, fs, d), lambda j, m: (j, layer, 0, 0)), hid, hid],
        out_specs=[hid, hid],
        out_shape=[jax.ShapeDtypeStruct((N_DEV, s, fs), BF16)] * 2,
        args=(df, wd, a, b), carry=carry)


def _ffn_bwd_wd(sv, df, name, carry=None):
    _, s, fs = sv.shape
    d = df.shape[1]
    tk = _tile(s, 512)
    nk = s // tk

    def body(s_ref, df_ref, o_ref, acc):
        @pl.when(pl.program_id(1) == 0)
        def _():
            acc[...] = jnp.zeros_like(acc)

        acc[...] += lax.dot_general(s_ref[...], df_ref[...], (TN, ((), ())), preferred_element_type=F32)

        @pl.when(pl.program_id(1) == nk - 1)
        def _():
            o_ref[...] = acc[...].astype(BF16)

    return _hosted(
        body, name=name, grid=(N_DEV, nk),
        in_specs=[pl.BlockSpec((None, tk, fs), lambda j, k: (j, k, 0)), pl.BlockSpec((tk, d), lambda j, k: (k, 0))],
        out_specs=[pl.BlockSpec((None, fs, d), lambda j, k: (j, 0, 0))],
        out_shape=[jax.ShapeDtypeStruct((N_DEV, fs, d), BF16)],
        scratch_shapes=[pltpu.VMEM((fs, d), F32)],
        args=(sv, df), carry=carry)


def _ffn_bwd_h(da, db, wg, wu, layer, name, carry=None):
    _, s, fs = da.shape
    d = wg.shape[-2]
    tm = _tile(s, 1024)

    def body(da_ref, db_ref, wg_ref, wu_ref, o_ref, acc):
        j = pl.program_id(1)

        @pl.when(j == 0)
        def _():
            acc[...] = jnp.zeros_like(acc)

        acc[...] += (lax.dot_general(da_ref[...], wg_ref[...], (NT, ((), ())), preferred_element_type=F32)
                     + lax.dot_general(db_ref[...], wu_ref[...], (NT, ((), ())), preferred_element_type=F32))

        @pl.when(j == N_DEV - 1)
        def _():
            o_ref[...] = acc[...]

    hid = pl.BlockSpec((None, tm, fs), lambda m, j: (j, m, 0))
    wspec = pl.BlockSpec((None, None, d, fs), lambda m, j: (j, layer, 0, 0))
    return _hosted(
        body, name=name, grid=(s // tm, N_DEV),
        in_specs=[hid, hid, wspec, wspec],
        out_specs=[pl.BlockSpec((tm, d), lambda m, j: (m, 0))],
        out_shape=[jax.ShapeDtypeStruct((s, d), F32)],
        scratch_shapes=[pltpu.VMEM((tm, d), F32)],
        args=(da, db, wg, wu), carry=carry)


def _ffn_bwd_wgu(h, da, db, name):
    s, d = h.shape
    fs = da.shape[-1]
    tk = _tile(s, 512)
    nk = s // tk

    def body(h_ref, da_ref, db_ref, og_ref, ou_ref, accg, accu):
        @pl.when(pl.program_id(1) == 0)
        def _():
            accg[...] = jnp.zeros_like(accg)
            accu[...] = jnp.zeros_like(accu)

        hv = h_ref[...]
        accg[...] += lax.dot_general(hv, da_ref[...], (TN, ((), ())), preferred_element_type=F32)
        accu[...] += lax.dot_general(hv, db_ref[...], (TN, ((), ())), preferred_element_type=F32)

        @pl.when(pl.program_id(1) == nk - 1)
        def _():
            og_ref[...] = accg[...].astype(BF16)
            ou_ref[...] = accu[...].astype(BF16)

    hid = pl.BlockSpec((None, tk, fs), lambda j, k: (j, k, 0))
    ospec = pl.BlockSpec((None, d, fs), lambda j, k: (j, 0, 0))
    return _pcall(
        body, name=name, grid=(N_DEV, nk),
        in_specs=[pl.BlockSpec((tk, d), lambda j, k: (k, 0)), hid, hid],
        out_specs=[ospec, ospec],
        out_shape=[jax.ShapeDtypeStruct((N_DEV, d, fs), BF16)] * 2,
        scratch_shapes=[pltpu.VMEM((d, fs), F32), pltpu.VMEM((d, fs), F32)],
        compiler_params=_params("parallel", "arbitrary"),
    )(h, da, db)


def _mm(a, b, *, ta=False, tb=False, out_dtype=F32, name, tm=1024, tn=1024, tk=2048, residual=None):
    m, kdim = (a.shape[1], a.shape[0]) if ta else a.shape
    n = b.shape[0] if tb else b.shape[1]
    tm, tn, tk = _tile(m, tm), _tile(n, tn), _tile(kdim, tk)
    nk = kdim // tk
    dims = ((0,) if ta else (1,), (1,) if tb else (0,))

    def body(*refs):
        a_ref, b_ref = refs[:2]
        acc = refs[-1]
        kk = pl.program_id(2)

        @pl.when(kk == 0)
        def _():
            acc[...] = jnp.zeros_like(acc)

        acc[...] += lax.dot_general(a_ref[...].astype(BF16), b_ref[...].astype(BF16), (dims, ((), ())),
                                    preferred_element_type=F32)

        @pl.when(kk == nk - 1)
        def _():
            if residual is None:
                refs[2][...] = acc[...].astype(out_dtype)
            else:
                res_ref, gate_ref, y_ref, xo_ref = refs[2:6]
                yv = acc[...]
                y_ref[...] = yv
                xo_ref[...] = res_ref[...] + gate_ref[...] * yv

    a_spec = pl.BlockSpec((tk, tm), lambda i, j, k: (k, i)) if ta else pl.BlockSpec((tm, tk), lambda i, j, k: (i, k))
    b_spec = pl.BlockSpec((tn, tk), lambda i, j, k: (j, k)) if tb else pl.BlockSpec((tk, tn), lambda i, j, k: (k, j))
    o_spec = pl.BlockSpec((tm, tn), lambda i, j, k: (i, j))
    if residual is None:
        in_specs, out_specs = [a_spec, b_spec], o_spec
        out_shape = jax.ShapeDtypeStruct((m, n), out_dtype)
        args = (a, b)
    else:
        in_specs = [a_spec, b_spec, o_spec, pl.BlockSpec((1, tn), lambda i, j, k: (0, j))]
        out_specs = [o_spec, o_spec]
        out_shape = [jax.ShapeDtypeStruct((m, n), F32)] * 2
        args = (a, b) + tuple(residual)
    return _pcall(
        body, name=name, grid=(m // tm, n // tn, nk), in_specs=in_specs, out_specs=out_specs, out_shape=out_shape,
        scratch_shapes=[pltpu.VMEM((tm, tn), F32)],
        compiler_params=_params("parallel", "parallel", "arbitrary"),
    )(*args)


def _log_sigmoid(z):
    return jnp.minimum(z, 0.0) - jnp.log(1.0 + jnp.exp(-jnp.abs(z)))


def _fox_gate(proj, small_blk, bias_lane):
    s = proj.shape[0]
    ts = _tile(s, 1024)
    nsub = ts // LANES

    def body(z_ref, b_ref, cum_ref, carry):
        @pl.when(pl.program_id(0) == 0)
        def _():
            carry[...] = jnp.zeros_like(carry)

        ii = lax.broadcasted_iota(jnp.int32, (LANES, LANES), 0)
        jj = lax.broadcasted_iota(jnp.int32, (LANES, LANES), 1)
        tri = (ii >= jj).astype(BF16)
        logf = _log_sigmoid(z_ref[...] + b_ref[...])
        cv = carry[...]
        for sb in range(nsub):
            blk = logf[sb * LANES:(sb + 1) * LANES, :]
            cum_ref[sb * LANES:(sb + 1) * LANES, :] = _dot_exact_lhs(tri, blk) + cv
            cv = cv + jnp.sum(blk, axis=0, keepdims=True)
        carry[...] = cv

    return _pcall(
        body, name="fox_gate", grid=(s // ts,),
        in_specs=[pl.BlockSpec((ts, LANES), lambda i: (i, small_blk)), pl.BlockSpec((1, LANES), lambda i: (0, 0))],
        out_specs=pl.BlockSpec((ts, LANES), lambda i: (i, 0)),
        out_shape=jax.ShapeDtypeStruct((s, LANES), F32),
        scratch_shapes=[pltpu.VMEM((1, LANES), F32)],
        compiler_params=_params("arbitrary"),
    )(proj, bias_lane)


def _fox_gate_bwd(dcum_q, dcum_k, proj, small_blk, bias_lane):
    s = proj.shape[0]
    ts = _tile(s, 1024)
    nsub = ts // LANES
    nb = s // ts

    def body(dcq_ref, dc_ref, z_ref, b_ref, dz_ref, db_ref, carry):
        @pl.when(pl.program_id(0) == 0)
        def _():
            carry[...] = jnp.zeros_like(carry)
            db_ref[...] = jnp.zeros_like(db_ref)

        ii = lax.broadcasted_iota(jnp.int32, (LANES, LANES), 0)
        jj = lax.broadcasted_iota(jnp.int32, (LANES, LANES), 1)
        triu = (jj >= ii).astype(BF16)
        dc = dcq_ref[...] + dc_ref[...]
        zb = z_ref[...] + b_ref[...]
        cv = carry[...]
        dbv = jnp.zeros((1, LANES), F32)
        for sb in reversed(range(nsub)):
            rows = slice(sb * LANES, (sb + 1) * LANES)
            blk = dc[rows, :]
            dlogf = _dot_exact_lhs(triu, blk) + cv
            cv = cv + jnp.sum(blk, axis=0, keepdims=True)
            dz = dlogf * _sigmoid(-zb[rows, :])
            dz_ref[rows, :] = dz
            dbv = dbv + jnp.sum(dz, axis=0, keepdims=True)
        carry[...] = cv
        db_ref[...] += dbv

    row = pl.BlockSpec((1, LANES), lambda i: (0, 0))
    return _pcall(
        body, name="fox_gate_bwd", grid=(nb,),
        in_specs=[pl.BlockSpec((ts, LANES), lambda i: (nb - 1 - i, 0)),
                  pl.BlockSpec((ts, LANES), lambda i: (nb - 1 - i, 0)),
                  pl.BlockSpec((ts, LANES), lambda i: (nb - 1 - i, small_blk)), row],
        out_specs=[pl.BlockSpec((ts, LANES), lambda i: (nb - 1 - i, 0)), row],
        out_shape=[jax.ShapeDtypeStruct((s, LANES), F32), jax.ShapeDtypeStruct((1, LANES), F32)],
        scratch_shapes=[pltpu.VMEM((1, LANES), F32)],
        compiler_params=_params("arbitrary"),
    )(dcum_q, dcum_k, proj, bias_lane)


def _tri_tables(n, by_key):
    if by_key:
        pairs = [(i, j) for j in range(n) for i in range(j, n)]
    else:
        pairs = [(i, j) for i in range(n) for j in range(i + 1)]
    return (jnp.asarray(np.array([p[0] for p in pairs], np.int32)),
            jnp.asarray(np.array([p[1] for p in pairs], np.int32)))


def _as_row(col):
    t = col.shape[0]
    eye = lax.broadcasted_iota(jnp.int32, (t, t), 0) == lax.broadcasted_iota(jnp.int32, (t, t), 1)
    return jnp.sum(jnp.where(eye, col, 0.0), axis=0, keepdims=True)


def _fox_scores(a, b, bias_col, bias_row, scale, diagonal, rows_are_keys=False):
    sc = lax.dot_general(a.astype(BF16), b.astype(BF16), (NT, ((), ())), preferred_element_type=F32) * scale
    sc = sc + (bias_col + bias_row)
    if not diagonal:
        return sc
    row = lax.broadcasted_iota(jnp.int32, sc.shape, 0)
    col = lax.broadcasted_iota(jnp.int32, sc.shape, 1)
    return jnp.where(row <= col if rows_are_keys else col <= row, sc, NEG)


def _fox_fwd(proj, cum_col, cum_row, w_norm, heads, carry=None):
    s = proj.shape[0]
    t = _tile(s, 512)
    qi, ki = _tri_tables(s // t, False)
    scale = 1.0 / math.sqrt(HEAD_DIM)

    def body(qi_ref, ki_ref, q_ref, k_ref, v_ref, cq_ref, ck_ref, w_ref, o_ref, lse_ref, lser_ref, on_ref, m_s, acc_s):
        iq, ik = qi_ref[pl.program_id(1)], ki_ref[pl.program_id(1)]

        @pl.when(ik == 0)
        def _():
            m_s[...] = jnp.full_like(m_s, NEG)
            acc_s[...] = jnp.zeros_like(acc_s)

        def step(diagonal):
            sc = _fox_scores(q_ref[...], k_ref[...], cq_ref[:, 0:1], -ck_ref[...], scale, diagonal)
            m_prev = m_s[...]
            m_new = jnp.maximum(m_prev, jnp.max(sc, axis=1, keepdims=True))
            p = jnp.exp(sc - m_new).astype(BF16)
            v_ones = jnp.concatenate([v_ref[...].astype(BF16), jnp.ones((t, LANES), BF16)], axis=1)
            acc_s[...] = jnp.exp(m_prev - m_new) * acc_s[...] + jnp.dot(p, v_ones, preferred_element_type=F32)
            m_s[...] = m_new

        @pl.when(ik < iq)
        def _():
            step(False)

        @pl.when(ik == iq)
        def _():
            step(True)
            acc = acc_s[...]
            o = acc[:, :HEAD_DIM] / acc[:, HEAD_DIM:]
            lse = m_s[...] + jnp.log(acc[:, HEAD_DIM:])
            o_ref[...] = o
            lse_ref[...] = lse
            lser_ref[...] = _as_row(lse[:, 0:1])
            r = lax.rsqrt(jnp.mean(o * o, axis=1, keepdims=True) + EPS)
            on_ref[...] = (o * r * w_ref[...]).astype(BF16)

    qblk = pl.BlockSpec((t, HEAD_DIM), lambda h, p, qi, ki: (qi[p], h))
    kblk = lambda off: pl.BlockSpec((t, HEAD_DIM), lambda h, p, qi, ki: (ki[p], off + h))
    qcol = pl.BlockSpec((None, t, LANES), lambda h, p, qi, ki: (h, qi[p], 0))
    return _hosted(
        body, name="fox_fwd", grid=(heads, int(qi.shape[0])), prefetch=(qi, ki),
        in_specs=[qblk, kblk(heads), kblk(2 * heads), qcol,
                  pl.BlockSpec((None, 1, t), lambda h, p, qi, ki: (h, 0, ki[p])),
                  pl.BlockSpec((1, HEAD_DIM), lambda h, p, qi, ki: (0, 0))],
        out_specs=[qblk, qcol, pl.BlockSpec((None, 1, t), lambda h, p, qi, ki: (h, 0, qi[p])), qblk],
        scratch_shapes=[pltpu.VMEM((t, 1), F32), pltpu.VMEM((t, 2 * HEAD_DIM), F32)],
        out_shape=[jax.ShapeDtypeStruct((s, heads * HEAD_DIM), F32), jax.ShapeDtypeStruct((heads, s, LANES), F32),
                   jax.ShapeDtypeStruct((heads, 1, s), F32), jax.ShapeDtypeStruct((s, heads * HEAD_DIM), BF16)],
        args=(proj, proj, proj, cum_col, cum_row, w_norm), carry=carry)


def _fox_prep_bwd(do_cat, o_raw, w_norm, heads):
    s = o_raw.shape[0]
    ts = _tile(s, 512)

    def body(g_ref, o_ref, w_ref, do_ref, delta_ref, deltar_ref, dw_ref):
        @pl.when((pl.program_id(0) == 0) & (pl.program_id(1) == 0))
        def _():
            dw_ref[...] = jnp.zeros_like(dw_ref)

        o = o_ref[...]
        g = g_ref[...]
        r = lax.rsqrt(jnp.mean(o * o, axis=1, keepdims=True) + EPS)
        wg = g * w_ref[...]
        do = r * wg - o * (r * r * r) * jnp.mean(wg * o, axis=1, keepdims=True)
        do_ref[...] = do.astype(BF16)
        delta = jnp.sum(do * o, axis=1, keepdims=True)
        delta_ref[...] = jnp.broadcast_to(delta, delta_ref.shape)
        deltar_ref[...] = _as_row(delta)
        dw_ref[...] += jnp.sum(g * o * r, axis=0, keepdims=True)

    blk = pl.BlockSpec((ts, HEAD_DIM), lambda h, i: (i, h))
    row = pl.BlockSpec((1, HEAD_DIM), lambda h, i: (0, 0))
    return _pcall(
        body, name="fox_prep_bwd", grid=(heads, s // ts),
        in_specs=[blk, blk, row],
        out_specs=[blk, pl.BlockSpec((None, ts, LANES), lambda h, i: (h, i, 0)),
                   pl.BlockSpec((None, 1, ts), lambda h, i: (h, 0, i)), row],
        out_shape=[jax.ShapeDtypeStruct((s, heads * HEAD_DIM), BF16), jax.ShapeDtypeStruct((heads, s, LANES), F32),
                   jax.ShapeDtypeStruct((heads, 1, s), F32), jax.ShapeDtypeStruct((1, HEAD_DIM), F32)],
        compiler_params=_params("arbitrary", "arbitrary"),
    )(do_cat, o_raw, w_norm)


def _fox_dq(proj, do, cum_col, cum_row, lse, delta, heads, carry=None):
    s = proj.shape[0]
    t = _tile(s, 512)
    qi, ki = _tri_tables(s // t, False)
    scale = 1.0 / math.sqrt(HEAD_DIM)

    def body(qi_ref, ki_ref, q_ref, k_ref, v_ref, do_ref, cq_ref, ck_ref, lse_ref, dl_ref, dq_ref, dc_ref, acc, dc_acc):
        iq, ik = qi_ref[pl.program_id(1)], ki_ref[pl.program_id(1)]

        @pl.when(ik == 0)
        def _():
            acc[...] = jnp.zeros_like(acc)
            dc_acc[...] = jnp.zeros_like(dc_acc)

        def step(diagonal):
            kv = k_ref[...]
            sc = _fox_scores(q_ref[...], kv, cq_ref[:, 0:1] - lse_ref[:, 0:1], -ck_ref[...], scale, diagonal)
            p = jnp.exp(sc)
            dp = _dot(do_ref[...], v_ref[...], NT)
            ds = p * (dp - dl_ref[:, 0:1])
            acc[...] += _dot(ds, kv, NN)
            dc_acc[...] += jnp.sum(ds, axis=1, keepdims=True)

        @pl.when(ik < iq)
        def _():
            step(False)

        @pl.when(ik == iq)
        def _():
            step(True)
            dq_ref[...] = (acc[...] * scale).astype(BF16)
            dc_ref[...] = _as_row(dc_acc[...])

    qblk = pl.BlockSpec((t, HEAD_DIM), lambda h, p, qi, ki: (qi[p], h))
    kblk = lambda off: pl.BlockSpec((t, HEAD_DIM), lambda h, p, qi, ki: (ki[p], off + h))
    qcol = pl.BlockSpec((None, t, LANES), lambda h, p, qi, ki: (h, qi[p], 0))
    return _hosted(
        body, name="fox_dq", grid=(heads, int(qi.shape[0])), prefetch=(qi, ki),
        in_specs=[qblk, kblk(heads), kblk(2 * heads), qblk, qcol,
                  pl.BlockSpec((None, 1, t), lambda h, p, qi, ki: (h, 0, ki[p])), qcol, qcol],
        out_specs=[qblk, pl.BlockSpec((None, 1, t), lambda h, p, qi, ki: (h, 0, qi[p]))],
        scratch_shapes=[pltpu.VMEM((t, HEAD_DIM), F32), pltpu.VMEM((t, 1), F32)],
        out_shape=[jax.ShapeDtypeStruct((s, heads * HEAD_DIM), BF16), jax.ShapeDtypeStruct((heads, 1, s), F32)],
        args=(proj, proj, proj, do, cum_col, cum_row, lse, delta), carry=carry)


def _fox_dkv(proj, do, cum_col, cum_row, lse_row, delta_row, heads, carry=None):
    s = proj.shape[0]
    t = _tile(s, 512)
    nk = s // t
    qi, ki = _tri_tables(nk, True)
    scale = 1.0 / math.sqrt(HEAD_DIM)

    def body(qi_ref, ki_ref, q_ref, k_ref, v_ref, do_ref, cqr_ref, ckc_ref, lse_ref, dl_ref, dk_ref, dv_ref, dc_ref,
             dk_acc, dv_acc, dc_acc):
        iq, ik = qi_ref[pl.program_id(1)], ki_ref[pl.program_id(1)]

        def step(diagonal):
            qv = q_ref[...]
            dov = do_ref[...]
            st = _fox_scores(k_ref[...], qv, -ckc_ref[:, 0:1], cqr_ref[...] - lse_ref[...], scale, diagonal, True)
            pt = jnp.exp(st)
            dv_acc[...] += _dot(pt, dov, NN)
            dpt = _dot(v_ref[...], dov, NT)
            dst = pt * (dpt - dl_ref[...])
            dk_acc[...] += _dot(dst, qv, NN)
            dc_acc[...] += jnp.sum(dst, axis=1, keepdims=True)

        @pl.when(iq == ik)
        def _():
            dk_acc[...] = jnp.zeros_like(dk_acc)
            dv_acc[...] = jnp.zeros_like(dv_acc)
            dc_acc[...] = jnp.zeros_like(dc_acc)
            step(True)

        @pl.when(iq > ik)
        def _():
            step(False)

        @pl.when(iq == nk - 1)
        def _():
            dk_ref[...] = (dk_acc[...] * scale).astype(BF16)
            dv_ref[...] = dv_acc[...].astype(BF16)
            dc_ref[...] = _as_row(-dc_acc[...])

    qblk = pl.BlockSpec((t, HEAD_DIM), lambda h, p, qi, ki: (qi[p], h))
    qrow = pl.BlockSpec((None, 1, t), lambda h, p, qi, ki: (h, 0, qi[p]))
    kblk = lambda off: pl.BlockSpec((t, HEAD_DIM), lambda h, p, qi, ki: (ki[p], off + h))
    kout = pl.BlockSpec((t, HEAD_DIM), lambda h, p, qi, ki: (ki[p], h))
    return _hosted(
        body, name="fox_dkv", grid=(heads, int(qi.shape[0])), prefetch=(qi, ki),
        in_specs=[qblk, kblk(heads), kblk(2 * heads), qblk, qrow,
                  pl.BlockSpec((None, t, LANES), lambda h, p, qi, ki: (h, ki[p], 0)), qrow, qrow],
        out_specs=[kout, kout, pl.BlockSpec((None, 1, t), lambda h, p, qi, ki: (h, 0, ki[p]))],
        scratch_shapes=[pltpu.VMEM((t, HEAD_DIM), F32), pltpu.VMEM((t, HEAD_DIM), F32), pltpu.VMEM((t, 1), F32)],
        out_shape=[jax.ShapeDtypeStruct((s, heads * HEAD_DIM), BF16)] * 2 + [jax.ShapeDtypeStruct((heads, 1, s), F32)],
        args=(proj, proj, proj, do, cum_row, cum_col, lse_row, delta_row), carry=carry)


def _shift_rows(xv, halo, j, forward):
    n = xv.shape[0]
    rid = lax.broadcasted_iota(jnp.int32, (8, xv.shape[1]), 0)
    if forward:
        xs = pltpu.roll(xv, n - j, 0)
        hs = pltpu.roll(halo, 8 - j, 0)
        edge = jnp.where(rid >= 8 - j, hs, xs[n - 8:, :])
        return jnp.concatenate([xs[:n - 8, :], edge], axis=0)
    xs = pltpu.roll(xv, j, 0)
    hs = pltpu.roll(halo, j, 0)
    edge = jnp.where(rid < j, hs, xs[:8, :])
    return jnp.concatenate([edge, xs[8:, :]], axis=0)


def _conv_silu(xv, halo, w):
    xc = w[CONV_W - 1:CONV_W, :] * xv
    for j in range(1, CONV_W):
        xc = xc + w[CONV_W - 1 - j:CONV_W - j, :] * _shift_rows(xv, halo, j, False)
    return xc, xc * _sigmoid(xc)


def _gdn_pre(proj, conv_w, heads):
    s = proj.shape[0]
    cw = 3 * heads * HEAD_DIM
    ts = _tile(s, 256)
    tb = ts // 8

    def body(x_ref, halo_ref, w_ref, q_ref, k_ref, v_ref):
        halo = jnp.where(pl.program_id(0) == 0, 0.0, halo_ref[...])
        _, y = _conv_silu(x_ref[...], halo, w_ref[...])
        for h in range(heads):
            for part, ref in enumerate((q_ref, k_ref, v_ref)):
                c0 = (part * heads + h) * HEAD_DIM
                blk = y[:, c0:c0 + HEAD_DIM]
                if part < 2:
                    blk = blk * lax.rsqrt(jnp.sum(blk * blk, axis=1, keepdims=True) + EPS)
                ref[h] = blk

    out = pl.BlockSpec((heads, ts, HEAD_DIM), lambda i: (0, i, 0))
    return _pcall(
        body, name="gdn_pre", grid=(s // ts,),
        in_specs=[pl.BlockSpec((ts, cw), lambda i: (i, 1)),
                  pl.BlockSpec((8, cw), lambda i: (jnp.maximum(i * tb - 1, 0), 1)),
                  pl.BlockSpec((CONV_W, cw), lambda i: (0, 0))],
        out_specs=[out, out, out],
        out_shape=[jax.ShapeDtypeStruct((heads, s, HEAD_DIM), F32)] * 3,
        compiler_params=_params("parallel"),
    )(proj, proj, conv_w)


def _gdn_pre_bwd_act(proj, conv_w, dq, dk, dv, heads):
    s = proj.shape[0]
    cw = 3 * heads * HEAD_DIM
    ts = _tile(s, 256)
    tb = ts // 8

    def body(x_ref, halo_ref, w_ref, dq_ref, dk_ref, dv_ref, dxc_ref, dw_ref):
        @pl.when(pl.program_id(0) == 0)
        def _():
            dw_ref[...] = jnp.zeros_like(dw_ref)

        xv = x_ref[...]
        halo = jnp.where(pl.program_id(0) == 0, 0.0, halo_ref[...])
        xc, y = _conv_silu(xv, halo, w_ref[...])
        sg = _sigmoid(xc)
        dsilu = sg * (1.0 + xc * (1.0 - sg))
        for h in range(heads):
            for part, ref in enumerate((dq_ref, dk_ref, dv_ref)):
                c0 = (part * heads + h) * HEAD_DIM
                g = ref[h]
                if part < 2:
                    blk = y[:, c0:c0 + HEAD_DIM]
                    r = lax.rsqrt(jnp.sum(blk * blk, axis=1, keepdims=True) + EPS)
                    g = r * g - blk * (r * r * r) * jnp.sum(g * blk, axis=1, keepdims=True)
                dxc_ref[:, c0:c0 + HEAD_DIM] = g * dsilu[:, c0:c0 + HEAD_DIM]
        dxc = dxc_ref[...]
        rows = [jnp.sum(dxc * (xv if j == 0 else _shift_rows(xv, halo, j, False)), axis=0, keepdims=True)
                for j in range(CONV_W)]
        dw_ref[...] += jnp.concatenate([rows[CONV_W - 1 - k] for k in range(CONV_W)]
                                       + [jnp.zeros((8 - CONV_W, cw), F32)], axis=0)

    hblk = pl.BlockSpec((heads, ts, HEAD_DIM), lambda i: (0, i, 0))
    return _pcall(
        body, name="gdn_pre_bwd_act", grid=(s // ts,),
        in_specs=[pl.BlockSpec((ts, cw), lambda i: (i, 1)),
                  pl.BlockSpec((8, cw), lambda i: (jnp.maximum(i * tb - 1, 0), 1)),
                  pl.BlockSpec((CONV_W, cw), lambda i: (0, 0)), hblk, hblk, hblk],
        out_specs=[pl.BlockSpec((ts, cw), lambda i: (i, 0)), pl.BlockSpec((8, cw), lambda i: (0, 0))],
        out_shape=[jax.ShapeDtypeStruct((s, cw), F32), jax.ShapeDtypeStruct((8, cw), F32)],
        compiler_params=_params("arbitrary"),
    )(proj, proj, conv_w, dq, dk, dv)


def _gdn_pre_bwd_conv(dxc, conv_w):
    s, cw = dxc.shape
    ts = _tile(s, 256)
    tb = ts // 8
    last = s // 8 - 1

    def body(g_ref, halo_ref, w_ref, dx_ref):
        gv = g_ref[...]
        w = w_ref[...]
        halo = jnp.where(pl.program_id(0) == s // ts - 1, 0.0, halo_ref[...])
        dx = w[CONV_W - 1:CONV_W, :] * gv
        for j in range(1, CONV_W):
            dx = dx + w[CONV_W - 1 - j:CONV_W - j, :] * _shift_rows(gv, halo, j, True)
        dx_ref[...] = dx.astype(BF16)

    return _pcall(
        body, name="gdn_pre_bwd_conv", grid=(s // ts,),
        in_specs=[pl.BlockSpec((ts, cw), lambda i: (i, 0)),
                  pl.BlockSpec((8, cw), lambda i: (jnp.minimum((i + 1) * tb, last), 0)),
                  pl.BlockSpec((CONV_W, cw), lambda i: (0, 0))],
        out_specs=pl.BlockSpec((ts, cw), lambda i: (i, 0)),
        out_shape=jax.ShapeDtypeStruct((s, cw), BF16),
        compiler_params=_params("parallel"),
    )(dxc, dxc, conv_w)


def _bdot(a, b, ca, cb):
    return lax.dot_general(a.astype(BF16), b.astype(BF16), (((ca,), (cb,)), ((0,), (0,))),
                           preferred_element_type=F32)


def _bdot_hp(a, b, ca, cb):
    ah = a.astype(BF16)
    al = (a - ah.astype(F32)).astype(BF16)
    bh = b.astype(BF16)
    bl = (b - bh.astype(F32)).astype(BF16)
    d = lambda p, q: lax.dot_general(p, q, (((ca,), (cb,)), ((0,), (0,))), preferred_element_type=F32)
    return d(ah, bh) + (d(ah, bl) + d(al, bh))


def _gdn_gates(small, a_lane, dt_lane, heads):
    lane = lax.broadcasted_iota(jnp.int32, small.shape, 1)
    za = small + dt_lane
    g_all = -jnp.exp(a_lane) * (jnp.maximum(za, 0.0) + jnp.log(1.0 + jnp.exp(-jnp.abs(za))))
    b_all = _sigmoid(small)
    pick = lambda v, l: jnp.sum(jnp.where(lane == l, v, 0.0), axis=1, keepdims=True)
    g = jnp.stack([pick(g_all, heads + h) for h in range(heads)], axis=0)
    beta = jnp.stack([pick(b_all, 2 * heads + h) for h in range(heads)], axis=0)
    return g, beta


def _chunk_masks(c):
    ii = lax.broadcasted_iota(jnp.int32, (1, c, c), 1)
    jj = lax.broadcasted_iota(jnp.int32, (1, c, c), 2)
    return ii >= jj, ii > jj, ii == jj


def _col_to_row(col, eye):
    return jnp.sum(jnp.where(eye, col, 0.0), axis=1, keepdims=True)


def _row_to_col(row, eye):
    return jnp.sum(jnp.where(eye, row, 0.0), axis=2, keepdims=True)


def _gdn_chunk(q, k, v, g, beta, state):
    c = q.shape[1]
    incl, strict, eye = _chunk_masks(c)
    g_row = _col_to_row(g, eye)
    gc_col = jnp.sum(jnp.where(incl, g_row, 0.0), axis=2, keepdims=True)
    gc_row = _col_to_row(gc_col, eye)
    gam = jnp.where(incl, jnp.exp(jnp.where(incl, gc_col - gc_row, NEG)), 0.0)
    egc = jnp.exp(gc_col)
    kb = k * beta
    vb = v * beta
    kbe = kb * egc
    low = jnp.where(strict, _bdot(kb, k, 2, 2), 0.0) * gam
    p = -low
    tinv = jnp.where(eye, 1.0, 0.0) + p
    width = 2
    while width < c:
        p = _bdot_hp(p, p, 2, 1)
        tinv = tinv + _bdot_hp(tinv, p, 2, 1)
        width *= 2
    u = _bdot(tinv, vb, 2, 1)
    w = _bdot(tinv, kbe, 2, 1)
    att = jnp.where(incl, _bdot(q, k, 2, 2), 0.0) * gam
    vn = u - _bdot(w, state, 2, 1)
    qe = q * egc
    o = _bdot(qe, state, 2, 1) + _bdot(att, vn, 2, 1)
    gl = jnp.sum(g, axis=1, keepdims=True)
    edec = jnp.exp(gl - gc_col)
    kdec = k * edec
    egl = jnp.exp(gl)
    new_state = state * egl + _bdot(kdec, vn, 1, 1)
    return dict(incl=incl, strict=strict, eye=eye, gam=gam, egc=egc, kb=kb, vb=vb, kbe=kbe, low=low, tinv=tinv, w=w,
                att=att, vn=vn, qe=qe, o=o, edec=edec, kdec=kdec, egl=egl, new_state=new_state)


def _gdn_load(q_ref, k_ref, v_ref, small_ref, a_ref, dt_ref, rows, heads):
    q = q_ref[:, rows, :] * (HEAD_DIM ** -0.5)
    g, beta = _gdn_gates(small_ref[rows, :], a_ref[...], dt_ref[...], heads)
    return q, k_ref[:, rows, :], v_ref[:, rows, :], g, beta


def _gdn_fwd(q, k, v, proj, z_blk, small_blk, a_lane, dt_lane, w_norm):
    heads, s, _ = q.shape
    c = min(GDN_CHUNK, s)
    r = _tile(s, 512)
    npb = r // c
    gw = heads * HEAD_DIM

    def body(q_ref, k_ref, v_ref, z_ref, small_ref, a_ref, dt_ref, w_ref, o_ref, st_ref, state):
        @pl.when(pl.program_id(0) == 0)
        def _():
            state[...] = jnp.zeros_like(state)

        def chunk(cb, carry):
            rows = pl.ds(pl.multiple_of(cb * c, c), c)
            qv, kv, vv, g, beta = _gdn_load(q_ref, k_ref, v_ref, small_ref, a_ref, dt_ref, rows, heads)
            st = state[...]
            st_ref[:, cb] = st
            res = _gdn_chunk(qv, kv, vv, g, beta, st)
            state[...] = res["new_state"]
            o = res["o"]
            rn = lax.rsqrt(jnp.mean(o * o, axis=2, keepdims=True) + EPS)
            zv = z_ref[rows, :]
            for h in range(heads):
                zh = zv[:, h * HEAD_DIM:(h + 1) * HEAD_DIM]
                o_ref[rows, h * HEAD_DIM:(h + 1) * HEAD_DIM] = (
                    o[h] * rn[h] * w_ref[...] * (zh * _sigmoid(zh))).astype(BF16)
            return carry

        lax.fori_loop(0, npb, chunk, 0)

    hblk = pl.BlockSpec((heads, r, HEAD_DIM), lambda i: (0, i, 0))
    row = pl.BlockSpec((1, LANES), lambda i: (0, 0))
    return _pcall(
        body, name="gdn_fwd", grid=(s // r,),
        in_specs=[hblk, hblk, hblk, pl.BlockSpec((r, gw), lambda i: (i, z_blk)),
                  pl.BlockSpec((r, LANES), lambda i: (i, small_blk)), row, row, row],
        out_specs=[pl.BlockSpec((r, gw), lambda i: (i, 0)),
                   pl.BlockSpec((heads, npb, HEAD_DIM, HEAD_DIM), lambda i: (0, i, 0, 0))],
        out_shape=[jax.ShapeDtypeStruct((s, gw), BF16),
                   jax.ShapeDtypeStruct((heads, s // c, HEAD_DIM, HEAD_DIM), F32)],
        scratch_shapes=[pltpu.VMEM((heads, HEAD_DIM, HEAD_DIM), F32)],
        compiler_params=_params("arbitrary"),
    )(q, k, v, proj, proj, a_lane, dt_lane, w_norm)


def _gdn_bwd(q, k, v, proj, z_blk, small_blk, a_lane, dt_lane, w_norm, states, do_cat, do_blk):
    heads, s, _ = q.shape
    c = min(GDN_CHUNK, s)
    r = _tile(s, 512)
    npb = r // c
    nb = s // r
    gw = heads * HEAD_DIM

    def body(q_ref, k_ref, v_ref, z_ref, small_ref, a_ref, dt_ref, w_ref, st_ref, do_ref,
             dq_ref, dk_ref, dv_ref, dz_ref, dsm_ref, da_ref, ddt_ref, dw_ref, dstate):
        @pl.when(pl.program_id(0) == 0)
        def _():
            dstate[...] = jnp.zeros_like(dstate)
            da_ref[...] = jnp.zeros_like(da_ref)
            ddt_ref[...] = jnp.zeros_like(ddt_ref)
            dw_ref[...] = jnp.zeros_like(dw_ref)

        def chunk(it, carry):
            cb = npb - 1 - it
            rows = pl.ds(pl.multiple_of(cb * c, c), c)
            qv, kv, vv, g, beta = _gdn_load(q_ref, k_ref, v_ref, small_ref, a_ref, dt_ref, rows, heads)
            st = st_ref[:, cb]
            f = _gdn_chunk(qv, kv, vv, g, beta, st)
            incl, strict, eye = f["incl"], f["strict"], f["eye"]
            o = f["o"]
            wv = w_ref[...]
            zv = z_ref[rows, :]
            dov = do_ref[rows, :]
            rn = lax.rsqrt(jnp.mean(o * o, axis=2, keepdims=True) + EPS)
            do_l, dw_acc = [], jnp.zeros((1, HEAD_DIM), F32)
            for h in range(heads):
                sl = slice(h * HEAD_DIM, (h + 1) * HEAD_DIM)
                zh, gh = zv[:, sl], dov[:, sl]
                sg = _sigmoid(zh)
                on = o[h] * rn[h]
                dz_ref[rows, sl] = (gh * (on * wv) * (sg * (1.0 + zh * (1.0 - sg)))).astype(BF16)
                gn = gh * (zh * sg)
                dw_acc = dw_acc + jnp.sum(gn * on, axis=0, keepdims=True)
                wg = gn * wv
                do_l.append(rn[h] * wg - o[h] * (rn[h] * rn[h] * rn[h]) * jnp.mean(wg * o[h], axis=1, keepdims=True))
            dw_ref[...] += dw_acc
            do = jnp.stack(do_l, axis=0)
            ds_out = dstate[...]
            dvn = _bdot(f["att"], do, 1, 1) + _bdot(f["kdec"], ds_out, 2, 1)
            datt = jnp.where(incl, _bdot(do, f["vn"], 2, 2), 0.0)
            dqe = _bdot(do, st, 2, 2)
            dstate[...] = _bdot(f["qe"], do, 1, 1) + f["egl"] * ds_out - _bdot(f["w"], dvn, 1, 1)
            dw = -_bdot(dvn, st, 2, 2)
            dkdec = _bdot(f["vn"], ds_out, 2, 2)
            t_kdec = jnp.sum(dkdec * f["kdec"], axis=2, keepdims=True)
            dgl = (jnp.sum(jnp.sum(st * ds_out, axis=2, keepdims=True), axis=1, keepdims=True) * f["egl"]
                   + jnp.sum(t_kdec, axis=1, keepdims=True))
            dgc = jnp.sum(dqe * f["qe"], axis=2, keepdims=True) - t_kdec
            dq = dqe * f["egc"]
            dk = dkdec * f["edec"]
            dtinv = _bdot(dvn, f["vb"], 2, 2) + _bdot(dw, f["kbe"], 2, 2)
            dvb = _bdot(f["tinv"], dvn, 1, 1)
            dkbe = _bdot(f["tinv"], dw, 1, 1)
            dkb = dkbe * f["egc"]
            dgc = dgc + jnp.sum(dkbe * f["kbe"], axis=2, keepdims=True)
            dlow = jnp.where(strict, -_bdot_hp(_bdot_hp(f["tinv"], dtinv, 1, 1), f["tinv"], 2, 2), 0.0)
            ml = dlow * f["gam"]
            dkb = dkb + _bdot(ml, kv, 2, 1)
            dk = dk + _bdot(ml, f["kb"], 1, 1)
            ma = datt * f["gam"]
            dq = dq + _bdot(ma, kv, 2, 1)
            dk = dk + _bdot(ma, qv, 1, 1)
            e = dlow * f["low"] + datt * f["att"]
            dgc = dgc + jnp.sum(e, axis=2, keepdims=True) - _row_to_col(jnp.sum(e, axis=1, keepdims=True), eye)
            dk = dk + beta * dkb
            dbeta = jnp.sum(dkb * kv, axis=2, keepdims=True) + jnp.sum(dvb * vv, axis=2, keepdims=True)
            dgc_row = _col_to_row(dgc, eye)
            dg = jnp.sum(jnp.where(incl, 0.0, dgc_row) + jnp.where(eye, dgc_row, 0.0), axis=2, keepdims=True) + dgl
            dq_ref[:, rows, :] = dq * (HEAD_DIM ** -0.5)
            dk_ref[:, rows, :] = dk
            dv_ref[:, rows, :] = beta * dvb
            small = small_ref[rows, :]
            lane = lax.broadcasted_iota(jnp.int32, small.shape, 1)
            dg_l = jnp.zeros(small.shape, F32)
            db_l = jnp.zeros(small.shape, F32)
            for h in range(heads):
                dg_l = dg_l + jnp.where(lane == heads + h, dg[h], 0.0)
                db_l = db_l + jnp.where(lane == 2 * heads + h, dbeta[h], 0.0)
            za = small + dt_ref[...]
            nexp = -jnp.exp(a_ref[...])
            softplus = jnp.maximum(za, 0.0) + jnp.log(1.0 + jnp.exp(-jnp.abs(za)))
            da_logit = dg_l * nexp * _sigmoid(za)
            sb = _sigmoid(small)
            dsm_ref[rows, :] = da_logit + db_l * sb * (1.0 - sb)
            ddt_ref[...] += jnp.sum(da_logit, axis=0, keepdims=True)
            da_ref[...] += jnp.sum(dg_l * nexp * softplus, axis=0, keepdims=True)
            return carry

        lax.fori_loop(0, npb, chunk, 0)

    rev = lambda i: nb - 1 - i
    hblk = pl.BlockSpec((heads, r, HEAD_DIM), lambda i: (0, rev(i), 0))
    row = pl.BlockSpec((1, LANES), lambda i: (0, 0))
    wide = lambda blk: pl.BlockSpec((r, gw), lambda i: (rev(i), blk))
    return _pcall(
        body, name="gdn_bwd", grid=(nb,),
        in_specs=[hblk, hblk, hblk, wide(z_blk), pl.BlockSpec((r, LANES), lambda i: (rev(i), small_blk)),
                  row, row, row, pl.BlockSpec((heads, npb, HEAD_DIM, HEAD_DIM), lambda i: (0, rev(i), 0, 0)),
                  wide(do_blk)],
        out_specs=[hblk, hblk, hblk, wide(0), pl.BlockSpec((r, LANES), lambda i: (rev(i), 0)), row, row, row],
        out_shape=[jax.ShapeDtypeStruct((heads, s, HEAD_DIM), F32)] * 3
        + [jax.ShapeDtypeStruct((s, gw), BF16), jax.ShapeDtypeStruct((s, LANES), F32)]
        + [jax.ShapeDtypeStruct((1, LANES), F32)] * 3,
        scratch_shapes=[pltpu.VMEM((heads, HEAD_DIM, HEAD_DIM), F32)],
        compiler_params=_params("arbitrary"),
    )(q, k, v, proj, proj, a_lane, dt_lane, w_norm, states, do_cat)


def _final(x, target, gf):
    s, d = x.shape
    ts = _tile(s, 512)

    def body(x_ref, t_ref, g_ref, loss_ref, dx_ref, dg_ref):
        @pl.when(pl.program_id(0) == 0)
        def _():
            loss_ref[...] = jnp.zeros_like(loss_ref)
            dg_ref[...] = jnp.zeros_like(dg_ref)

        xv = x_ref[...]
        gv = g_ref[...]
        r = lax.rsqrt(jnp.mean(xv * xv, axis=-1, keepdims=True) + EPS)
        xn = xv * r
        err = xn * gv - t_ref[...]
        per_tok = jnp.mean(err * err, axis=-1, keepdims=True)
        loss_ref[...] += 0.5 * jnp.sum(per_tok, axis=0, keepdims=True)
        dy = err * (1.0 / d)
        dg_ref[...] += jnp.sum(dy * xn, axis=0, keepdims=True)
        dxn = dy * gv
        dx_ref[...] = r * (dxn - xn * jnp.mean(dxn * xn, axis=-1, keepdims=True))

    blk = pl.BlockSpec((ts, d), lambda i: (i, 0))
    row = pl.BlockSpec((1, d), lambda i: (0, 0))
    return _pcall(
        body, name="final_loss", grid=(s // ts,),
        in_specs=[blk, blk, row], out_specs=[pl.BlockSpec((1, LANES), lambda i: (0, 0)), blk, row],
        out_shape=[jax.ShapeDtypeStruct((1, LANES), F32), jax.ShapeDtypeStruct((s, d), F32),
                   jax.ShapeDtypeStruct((1, d), F32)],
        compiler_params=_params("arbitrary"),
    )(x, target, gf)


def _adamw(parts, w, m, v, name):
    npart, rows, cols = parts.shape
    tr = _tile(rows, max(8, ADAM_BLOCK_BYTES // (4 * npart * cols)))
    c1 = 1.0 - ADAM_B1 ** ADAM_STEP
    c2 = 1.0 - ADAM_B2 ** ADAM_STEP

    def body(p_ref, w_ref, m_ref, v_ref, g_ref, d_ref, mo_ref, vo_ref):
        g = p_ref[0].astype(F32)
        for i in range(1, npart):
            g = g + p_ref[i].astype(F32)
        mn = ADAM_B1 * m_ref[...] + (1.0 - ADAM_B1) * g
        vn = ADAM_B2 * v_ref[...] + (1.0 - ADAM_B2) * (g * g)
        g_ref[...] = g
        mo_ref[...] = mn
        vo_ref[...] = vn
        d_ref[...] = -ADAM_LR * ((mn / c1) / (jnp.sqrt(vn / c2) + ADAM_EPS) + ADAM_WD * w_ref[...])

    blk = pl.BlockSpec((tr, cols), lambda i: (i, 0))
    return _pcall(
        body, name=name, grid=(rows // tr,),
        in_specs=[pl.BlockSpec((npart, tr, cols), lambda i: (0, i, 0)), blk, blk, blk],
        out_specs=[blk] * 4, out_shape=[jax.ShapeDtypeStruct((rows, cols), F32)] * 4,
        compiler_params=_params("parallel"),
    )(parts, w, m, v)


def _pad_lanes(v, n=LANES, at=0):
    return jnp.pad(v, ((0, 0), (at, n - at - v.shape[1])))


def _my_cols(a, me, width):
    return lax.dynamic_slice_in_dim(a, me * width, width, axis=a.ndim - 1)


def kernel(x, c, ada_w, ada_b, norm_g, ffn_w_gate, ffn_w_up, ffn_w_down, w_in, w_out, fox_f_bias, fox_out_norm, gdn_conv, gdn_A_log, gdn_dt_bias, gdn_out_norm, final_norm, loss_target, m_ada_w, m_ada_b, m_norm_g, m_ffn_w_gate, m_ffn_w_up, m_ffn_w_down, m_w_in, m_w_out, m_fox_f_bias, m_fox_out_norm, m_gdn_conv, m_gdn_A_log, m_gdn_dt_bias, m_gdn_out_norm, m_final_norm, v_ada_w, v_ada_b, v_norm_g, v_ffn_w_gate, v_ffn_w_up, v_ffn_w_down, v_w_in, v_w_out, v_fox_f_bias, v_fox_out_norm, v_gdn_conv, v_gdn_A_log, v_gdn_dt_bias, v_gdn_out_norm, v_final_norm):
    me = _linear(_mesh_pos())
    x0 = x[0]
    s, d = x0.shape
    heads = d // (2 * HEAD_DIM)
    fw = heads * HEAD_DIM
    ng = norm_g.shape[-1]
    ncv = gdn_conv.shape[-1]
    nada = ada_w.shape[-1]
    in_w = w_in.shape[-1] * N_DEV
    in_pad = -(-in_w // 512) * 512

    pack = jnp.concatenate([c, norm_g[0].reshape(1, 3 * ng), gdn_conv[0].reshape(1, CONV_W * ncv)], axis=1)
    pack_all = _gather_row(pack, "gather_small_params")
    c_all = pack_all[:, :d]
    g_all = pack_all[:, d:d + 3 * ng].reshape(N_DEV, 3, ng).transpose(1, 0, 2).reshape(3, d)
    conv_all = pack_all[:, d + 3 * ng:].reshape(N_DEV, CONV_W, ncv).transpose(1, 0, 2).reshape(CONV_W, 3 * fw)

    mod_blk = _ada_fwd(c_all, ada_w[0], _my_cols(ada_b, me, nada))
    mod_all = _exchange([mod_blk], scatter=False, in_vmem=True, name="gather_mod")[0]
    mod = lax.dynamic_slice_in_dim(mod_all, me, 1, axis=1).reshape(N_MOD, d)
    sh1, sc1, gt1, sh2, sc2, gt2, sh3, sc3, gt3 = [mod[i:i + 1] for i in range(N_MOD)]

    wg_sh, wu_sh, wd_sh = [w[0].astype(BF16) for w in (ffn_w_gate, ffn_w_up, ffn_w_down)]
    layer = lambda w, i: w[i:i + 1]
    wg0, wu0 = _exchange([layer(wg_sh, 0), layer(wu_sh, 0)], scatter=False, in_vmem=False,
                         name="gather_ffn1_up_weights")
    small_blk = 7 * heads

    bias_lane = _pad_lanes(fox_f_bias)
    a_lane = _pad_lanes(gdn_A_log, at=heads)
    dt_lane = _pad_lanes(gdn_dt_bias, at=heads)

    h1 = _norm_mod(x0, g_all[0:1], sc1, sh1, "norm_mod_1")
    (a1, b1, s1), (wd0, wout_g) = _ffn_up(h1, wg0, wu0, 0, "ffn1_up",
                                          carry=([layer(wd_sh, 0), w_out[0].astype(BF16)], False))
    (f1, x1), (win_g,) = _ffn_down(s1, wd0, 0, x0, gt1, "ffn1_down", carry=([w_in[0].astype(BF16)], False))
    win_full = win_g.transpose(1, 0, 2).reshape(d, in_w)
    o_f, o_qkv, o_a, o_z = 3 * fw, 3 * fw + heads, 6 * fw + heads, 6 * fw + 3 * heads
    win_al = jnp.concatenate(
        [win_full[:, :o_f], win_full[:, o_qkv:o_a], win_full[:, o_z:], win_full[:, o_f:o_qkv],
         win_full[:, o_a:o_z], jnp.zeros((d, in_pad - in_w), BF16)], axis=1)
    wout_full = wout_g.reshape(d, d)

    h2 = _norm_mod(x1, g_all[1:2], sc2, sh2, "norm_mod_2")
    proj = _mm(h2, win_al, name="in_proj", tn=1536)
    cum = _fox_gate(proj, small_blk, bias_lane)
    cum_t = cum[:, :heads].T
    cum_row = cum_t[:, None, :]
    cum_col = jnp.broadcast_to(cum_t[:, :, None], (heads, s, LANES))
    (o_raw, lse, lse_row, o_fox), (wg1, wu1, wd1) = _fox_fwd(
        proj, cum_col, cum_row, fox_out_norm, heads,
        carry=([layer(wg_sh, 1), layer(wu_sh, 1), layer(wd_sh, 1)], False))
    qg, kg, vg = _gdn_pre(proj, conv_all, heads)
    o_gdn, states = _gdn_fwd(qg, kg, vg, proj, 6, small_blk, a_lane, dt_lane, gdn_out_norm)
    o_cat = jnp.concatenate([o_fox, o_gdn], axis=1)
    mix, x2 = _mm(o_cat, wout_full, name="out_proj", residual=(x1, gt2))

    h3 = _norm_mod(x2, g_all[2:3], sc3, sh3, "norm_mod_3")
    (a3, b3, s3), _ = _ffn_up(h3, wg1, wu1, 0, "ffn2_up")
    (f3, x3), _ = _ffn_down(s3, wd1, 0, x2, gt3, "ffn2_down")

    loss_row, dx3, d_final = _final(x3, loss_target[0], final_norm.reshape(1, d))
    loss = lax.psum(loss_row[0, 0], MESH_AXES)

    df3, dgt3 = _gate_bwd(dx3, f3, gt3, MACARON_W, "ffn2_gate_bwd")
    (da3, db3), _ = _ffn_bwd_act(df3, wd1, 0, a3, b3, "ffn2_bwd_act")
    (dwd2,), _ = _ffn_bwd_wd(s3, df3, "ffn2_bwd_wd")
    (dh3,), (r_wd2,) = _ffn_bwd_h(da3, db3, wg1, wu1, 0, "ffn2_bwd_h", carry=([dwd2], True))
    dwg2, dwu2 = _ffn_bwd_wgu(h3, da3, db3, "ffn2_bwd_wgu")
    dx2, dsh3, dsc3, dg3 = _norm_mod_bwd(x2, dh3, dx3, g_all[2:3], sc3, "norm_mod_3_bwd")

    dmix, dgt2 = _gate_bwd(dx2, mix, gt2, 1.0, "mix_gate_bwd")
    do_cat = _mm(dmix, wout_full, tb=True, name="out_proj_bwd_x")
    dwout = _mm(o_cat, dmix, ta=True, out_dtype=BF16, name="out_proj_bwd_w", tk=512)
    do_fox, delta, delta_row, d_foxw = _fox_prep_bwd(do_cat, o_raw, fox_out_norm, heads)
    (dq_f, dcum_q), (r_wg2, r_wu2) = _fox_dq(proj, do_fox, cum_col, cum_row, lse, delta, heads,
                                             carry=([dwg2, dwu2], True))
    (dk_f, dv_f, dcum_k), (r_wout,) = _fox_dkv(proj, do_fox, cum_col, cum_row, lse_row, delta_row, heads,
                                               carry=([dwout.reshape(N_DEV, d // N_DEV, d)], True))
    head_lanes = lambda t: jnp.pad(t[:, 0, :].T, ((0, 0), (0, LANES - heads)))
    dsm_fox, d_fbias = _fox_gate_bwd(head_lanes(dcum_q), head_lanes(dcum_k), proj, small_blk, bias_lane)
    dqg, dkg, dvg, dz, dsm_gdn, d_alog, d_dt, d_gdnw = _gdn_bwd(
        qg, kg, vg, proj, 6, small_blk, a_lane, dt_lane, gdn_out_norm, states, do_cat, 1)
    dxc, d_conv = _gdn_pre_bwd_act(proj, conv_all, dqg, dkg, dvg, heads)
    dqkv = _gdn_pre_bwd_conv(dxc, conv_all)
    dsmall = (dsm_fox + dsm_gdn).astype(BF16)
    dproj = jnp.concatenate([dq_f, dk_f, dv_f, dqkv, dz, dsmall, jnp.zeros((s, in_pad - 7 * fw - LANES), BF16)], axis=1)
    dh2 = _mm(dproj, win_al, tb=True, name="in_proj_bwd_x", tk=1536)
    dwin_al = _mm(h2, dproj, ta=True, out_dtype=BF16, name="in_proj_bwd_w", tn=1536, tk=512)
    dwin_full = jnp.concatenate(
        [dwin_al[:, :o_f], dwin_al[:, 7 * fw:7 * fw + heads], dwin_al[:, o_f:o_f + 3 * fw],
         dwin_al[:, 7 * fw + heads:7 * fw + 3 * heads], dwin_al[:, 6 * fw:7 * fw]], axis=1)
    dwin_parts = dwin_full.reshape(d, N_DEV, in_w // N_DEV).transpose(1, 0, 2)
    dx1, dsh2, dsc2, dg2 = _norm_mod_bwd(x1, dh2, dx2, g_all[1:2], sc2, "norm_mod_2_bwd")

    df1, dgt1 = _gate_bwd(dx1, f1, gt1, MACARON_W, "ffn1_gate_bwd")
    (da1, db1), (r_win,) = _ffn_bwd_act(df1, wd0, 0, a1, b1, "ffn1_bwd_act", carry=([dwin_parts], True))
    dwg1, dwu1 = _ffn_bwd_wgu(h1, da1, db1, "ffn1_bwd_wgu")
    (dwd1,), (r_wg1,) = _ffn_bwd_wd(s1, df1, "ffn1_bwd_wd", carry=([dwg1], True))
    (dh1,), (r_wu1, r_wd1) = _ffn_bwd_h(da1, db1, wg0, wu0, 0, "ffn1_bwd_h", carry=([dwu1, dwd1], True))
    grad_x, dsh1, dsc1, dg1 = _norm_mod_bwd(x0, dh1, dx1, g_all[0:1], sc1, "norm_mod_1_bwd")

    dmod = jnp.concatenate([dsh1, dsc1, dgt1, dsh2, dsc2, dgt2, dsh3, dsc3, dgt3], axis=1)
    dmod_all = _gather_row(dmod, "gather_dmod")
    ct_pad = jnp.pad(c_all.T, ((0, 0), (0, LANES - N_DEV)))
    dmod_mine = jnp.pad(_my_cols(dmod_all, me, nada), ((0, LANES - N_DEV), (0, 0)))
    g_ada_w = _ada_bwd(ct_pad, dmod_mine)

    g_small_cols = [d_fbias, d_foxw, d_alog[:, heads:], d_dt[:, heads:], d_gdnw]
    small_part = jnp.concatenate(
        [_pad_lanes(v[:, :LANES]) for v in g_small_cols]
        + [d_final, dg1, dg2, dg3] + [d_conv[k:k + 1] for k in range(CONV_W)], axis=1)
    small_all = _gather_row(small_part, "gather_small_grads")
    off = 5 * LANES
    w_small = jnp.concatenate(
        [_pad_lanes(fox_f_bias), fox_out_norm, _pad_lanes(gdn_A_log), _pad_lanes(gdn_dt_bias), gdn_out_norm,
         final_norm.reshape(1, d)], axis=1)
    m_small = jnp.concatenate(
        [_pad_lanes(m_fox_f_bias), m_fox_out_norm, _pad_lanes(m_gdn_A_log), _pad_lanes(m_gdn_dt_bias),
         m_gdn_out_norm, m_final_norm.reshape(1, d)], axis=1)
    v_small = jnp.concatenate(
        [_pad_lanes(v_fox_f_bias), v_fox_out_norm, _pad_lanes(v_gdn_A_log), _pad_lanes(v_gdn_dt_bias),
         v_gdn_out_norm, v_final_norm.reshape(1, d)], axis=1)
    rep = _adamw(small_all[:, None, :off + d], w_small, m_small, v_small, "adamw_replicated")
    ab = _adamw(dmod_all[:, None, :], ada_b, m_ada_b, v_ada_b, "adamw_ada_b")
    g_ng = small_all[:, off + d:off + 4 * d].reshape(N_DEV, 3, d)
    ngs = _adamw(_my_cols(g_ng, me, ng), norm_g[0], m_norm_g[0], v_norm_g[0], "adamw_norm_g")
    g_cv = small_all[:, off + 4 * d:].reshape(N_DEV, CONV_W, 3 * fw)
    cvs = _adamw(_my_cols(g_cv, me, ncv), gdn_conv[0], m_gdn_conv[0], v_gdn_conv[0], "adamw_gdn_conv")

    def adam2(r0, r1, w, m, v, name):
        outs0 = _adamw(r0, w[0, 0], m[0, 0], v[0, 0], name + "_0")
        outs1 = _adamw(r1, w[0, 1], m[0, 1], v[0, 1], name + "_1")
        return [jnp.stack([p, q])[None] for p, q in zip(outs0, outs1)]

    wgs = adam2(r_wg1, r_wg2, ffn_w_gate, m_ffn_w_gate, v_ffn_w_gate, "adamw_w_gate")
    wus = adam2(r_wu1, r_wu2, ffn_w_up, m_ffn_w_up, v_ffn_w_up, "adamw_w_up")
    wds = adam2(r_wd1, r_wd2, ffn_w_down, m_ffn_w_down, v_ffn_w_down, "adamw_w_down")
    wis = [o[None] for o in _adamw(r_win, w_in[0], m_w_in[0], v_w_in[0], "adamw_w_in")]
    wos = [o[None] for o in _adamw(r_wout, w_out[0], m_w_out[0], v_w_out[0], "adamw_w_out")]
    adas = [o[None] for o in _adamw(g_ada_w[None], ada_w[0], m_ada_w[0], v_ada_w[0], "adamw_ada_w")]
    ngs = [o[None] for o in ngs]
    cvs = [o[None] for o in cvs]

    def rep_piece(i, lo, width):
        return rep[i][:, lo:lo + width]

    nh = fox_f_bias.shape[1]
    outs = []
    for i in range(4):
        outs.append([adas[i], ab[i], ngs[i], wgs[i], wus[i], wds[i], wis[i], wos[i],
                     rep_piece(i, 0, nh), rep_piece(i, LANES, HEAD_DIM), cvs[i], rep_piece(i, 2 * LANES, nh),
                     rep_piece(i, 3 * LANES, nh), rep_piece(i, 4 * LANES, HEAD_DIM), rep_piece(i, off, d).reshape(d)])
    return (loss, grad_x[None], *outs[0], *outs[1], *outs[2], *outs[3])
```

```python
import math

import numpy as np
import jax
import jax.numpy as jnp
from jax import lax
from jax.experimental import pallas as pl
from jax.experimental.pallas import tpu as pltpu

F32 = jnp.float32
BF16 = jnp.bfloat16

N_DEV = 8
MESH_AXES = ("x", "y", "c")
LANES = 128
HEAD_DIM = 128
GDN_CHUNK = 64
CONV_W = 4
N_MOD = 9
MACARON_W = 0.5
EPS = 1e-6
NEG = -1e30
VMEM_LIMIT_BYTES = 56 * 2 ** 20
ADAM_BLOCK_BYTES = 4 * 2 ** 20
FOX_HEADS_PER_STEP = 4

ADAM_LR = 0.001
ADAM_B1 = 0.9
ADAM_B2 = 0.999
ADAM_EPS = 1e-08
ADAM_WD = 0.01
ADAM_STEP = 10

MESH_ID = pl.DeviceIdType.MESH
ANY = pl.BlockSpec(memory_space=pl.ANY)
VMEM = pl.BlockSpec(memory_space=pltpu.VMEM)


def _pcall(body, **kw):
    return pl.pallas_call(body, **kw)


def _params(*semantics):
    return pltpu.CompilerParams(dimension_semantics=semantics, vmem_limit_bytes=VMEM_LIMIT_BYTES)


def _tile(n, pref):
    if n % pref == 0 and pref % 8 == 0:
        return pref
    t = 1 << (max(1, min(n, pref)).bit_length() - 1)
    while n % t:
        t //= 2
    return t if t % 8 == 0 else n


def _sigmoid(x):
    return 1.0 / (1.0 + jnp.exp(-x))


def _dot(a, b, dims):
    return lax.dot_general(a.astype(BF16), b.astype(BF16), (dims, ((), ())), preferred_element_type=F32)


NN = ((1,), (0,))
NT = ((1,), (1,))
TN = ((0,), (0,))


def _split3(x):
    hi = x.astype(BF16)
    r1 = x - hi.astype(F32)
    mid = r1.astype(BF16)
    lo = (r1 - mid.astype(F32)).astype(BF16)
    return hi, mid, lo


def _dot_exact_lhs(m_bf16, x, dims=NN):
    hi, mid, lo = _split3(x)
    d = lambda p: lax.dot_general(m_bf16, p, (dims, ((), ())), preferred_element_type=F32)
    return d(hi) + (d(mid) + d(lo))


def _dot_hp(a, b, dims):
    ah = a.astype(BF16)
    al = (a - ah.astype(F32)).astype(BF16)
    bh = b.astype(BF16)
    bl = (b - bh.astype(F32)).astype(BF16)
    d = lambda p, q: lax.dot_general(p, q, (dims, ((), ())), preferred_element_type=F32)
    return d(ah, bh) + (d(ah, bl) + d(al, bh))


def _mesh_pos():
    return lax.axis_index("x"), lax.axis_index("y"), lax.axis_index("c")


def _peer(pos, mask):
    x, y, c = pos
    return (1 - x if mask & 4 else x, 1 - y if mask & 2 else y, 1 - c if mask & 1 else c)


def _linear(pos):
    return 4 * pos[0] + 2 * pos[1] + pos[2]


def _exchange_copies(ins, outs, sems, scatter, with_receives=True):
    send_sems, recv_sems, local_sems = sems
    pos = _mesh_pos()
    me = _linear(pos)
    local, sends, recvs = [], [], []
    for i in range(len(ins)):
        src = ins[i].at[me] if scatter else ins[i]
        local.append(pltpu.make_async_copy(src, outs[i].at[me], local_sems.at[i]))
    for mask in range(1, N_DEV):
        peer = _peer(pos, mask)
        for i in range(len(ins)):
            sem = dict(send_sem=send_sems.at[i, mask - 1], recv_sem=recv_sems.at[i, mask - 1],
                       device_id=peer, device_id_type=MESH_ID)
            sends.append(pltpu.make_async_remote_copy(
                src_ref=ins[i].at[_linear(peer)] if scatter else ins[i], dst_ref=outs[i].at[me], **sem))
            if with_receives:
                recvs.append(pltpu.make_async_remote_copy(
                    src_ref=ins[i].at[me] if scatter else ins[i], dst_ref=outs[i].at[_linear(peer)], **sem))
    return local, sends, recvs


def _exchange_start(ins, outs, sems, scatter):
    local, sends, _ = _exchange_copies(ins, outs, sems, scatter, with_receives=False)
    for cp in local + sends:
        cp.start()


def _exchange_wait(ins, outs, sems, scatter):
    local, sends, recvs = _exchange_copies(ins, outs, sems, scatter)
    for cp in recvs:
        cp.wait_recv()
    for cp in sends:
        cp.wait_send()
    for cp in local:
        cp.wait()


def _exchange_sems(n):
    return [pltpu.SemaphoreType.DMA((n, N_DEV - 1)), pltpu.SemaphoreType.DMA((n, N_DEV - 1)),
            pltpu.SemaphoreType.DMA((n,))]


def _exchange_shapes(arrays, scatter):
    return [jax.ShapeDtypeStruct(a.shape if scatter else (N_DEV,) + a.shape, a.dtype) for a in arrays]


def _exchange(arrays, *, scatter, in_vmem, name):
    n = len(arrays)

    def body(*refs):
        ins, outs, sems = refs[:n], refs[n:2 * n], refs[2 * n:]
        _exchange_start(ins, outs, sems, scatter)
        _exchange_wait(ins, outs, sems, scatter)

    spec = VMEM if in_vmem else ANY
    outs = _pcall(
        body, name=name, out_shape=_exchange_shapes(arrays, scatter),
        in_specs=[spec] * n, out_specs=[spec] * n, scratch_shapes=_exchange_sems(n),
    )(*arrays)
    return list(outs)


def _hosted(body, *, name, grid, in_specs, out_specs, out_shape, args, scratch_shapes=(), prefetch=(), carry=None):
    n_in, n_out, n_scr, n_pre = len(in_specs), len(out_shape), len(scratch_shapes), len(prefetch)
    arrays, scatter = carry if carry is not None else ([], False)
    n = len(arrays)

    def wrapped(*refs):
        pre, r = refs[:n_pre], refs[n_pre:]
        host_in, comm_in = r[:n_in], r[n_in:n_in + n]
        r = r[n_in + n:]
        host_out, comm_out = r[:n_out], r[n_out:n_out + n]
        r = r[n_out + n:]
        host_scr, sems = r[:n_scr], r[n_scr:]
        if n:
            first = pl.program_id(0) == 0
            last = pl.program_id(0) == grid[0] - 1
            for ax in range(1, len(grid)):
                first = first & (pl.program_id(ax) == 0)
                last = last & (pl.program_id(ax) == grid[ax] - 1)

            @pl.when(first)
            def _():
                _exchange_start(comm_in, comm_out, sems, scatter)

        body(*pre, *host_in, *host_out, *host_scr)
        if n:
            @pl.when(last)
            def _():
                _exchange_wait(comm_in, comm_out, sems, scatter)

    grid_spec = pltpu.PrefetchScalarGridSpec(
        num_scalar_prefetch=n_pre, grid=grid, in_specs=list(in_specs) + [ANY] * n,
        out_specs=list(out_specs) + [ANY] * n,
        scratch_shapes=list(scratch_shapes) + (_exchange_sems(n) if n else []))
    outs = _pcall(
        wrapped, name=name, grid_spec=grid_spec, out_shape=list(out_shape) + _exchange_shapes(arrays, scatter),
        compiler_params=_params(*(["arbitrary"] * len(grid))),
    )(*prefetch, *args, *arrays)
    return list(outs[:n_out]), list(outs[n_out:])


def _gather_row(v, name):
    return _exchange([v], scatter=False, in_vmem=True, name=name)[0].reshape(N_DEV, v.shape[1])


def _ada_fwd(c_all, w, b):
    d, n = w.shape
    tn = _tile(n, 256)

    def body(c_ref, w_ref, b_ref, o_ref):
        cv = c_ref[...]
        cond = cv * _sigmoid(cv)
        o_ref[...] = _dot_hp(cond, w_ref[...], NN) + b_ref[...]

    return _pcall(
        body, name="ada_fwd", grid=(n // tn,),
        in_specs=[pl.BlockSpec((N_DEV, d), lambda j: (0, 0)), pl.BlockSpec((d, tn), lambda j: (0, j)),
                  pl.BlockSpec((1, tn), lambda j: (0, j))],
        out_specs=pl.BlockSpec((N_DEV, tn), lambda j: (0, j)),
        out_shape=jax.ShapeDtypeStruct((N_DEV, n), F32), compiler_params=_params("parallel"),
    )(c_all, w, b)


def _ada_bwd(ct_pad, dmod_pad):
    d = ct_pad.shape[0]
    n = dmod_pad.shape[1]
    tn = _tile(n, 256)

    def body(c_ref, g_ref, o_ref):
        cv = c_ref[...]
        cond = cv * _sigmoid(cv)
        o_ref[...] = _dot_hp(cond, g_ref[...], NN)

    return _pcall(
        body, name="ada_bwd", grid=(n // tn,),
        in_specs=[pl.BlockSpec((d, LANES), lambda j: (0, 0)), pl.BlockSpec((LANES, tn), lambda j: (0, j))],
        out_specs=pl.BlockSpec((d, tn), lambda j: (0, j)),
        out_shape=jax.ShapeDtypeStruct((d, n), F32), compiler_params=_params("parallel"),
    )(ct_pad, dmod_pad)


def _norm_mod(x, g, sc, sh, name):
    s, d = x.shape
    ts = _tile(s, 512)

    def body(x_ref, g_ref, sc_ref, sh_ref, h_ref):
        xv = x_ref[...]
        r = lax.rsqrt(jnp.mean(xv * xv, axis=-1, keepdims=True) + EPS)
        h_ref[...] = (xv * r * g_ref[...] * (1.0 + sc_ref[...]) + sh_ref[...]).astype(BF16)

    row = pl.BlockSpec((1, d), lambda i: (0, 0))
    return _pcall(
        body, name=name, grid=(s // ts,),
        in_specs=[pl.BlockSpec((ts, d), lambda i: (i, 0)), row, row, row],
        out_specs=pl.BlockSpec((ts, d), lambda i: (i, 0)),
        out_shape=jax.ShapeDtypeStruct((s, d), BF16), compiler_params=_params("parallel"),
    )(x, g, sc, sh)


def _norm_mod_bwd(x, dh, dx_out, g, sc, name):
    s, d = x.shape
    ts = _tile(s, 512)

    def body(x_ref, dh_ref, dxo_ref, g_ref, sc_ref, dx_ref, dsh_ref, dsc_ref, dg_ref):
        @pl.when(pl.program_id(0) == 0)
        def _():
            dsh_ref[...] = jnp.zeros_like(dsh_ref)
            dsc_ref[...] = jnp.zeros_like(dsc_ref)
            dg_ref[...] = jnp.zeros_like(dg_ref)

        xv = x_ref[...]
        dh_v = dh_ref[...]
        gv = g_ref[...]
        one_sc = 1.0 + sc_ref[...]
        r = lax.rsqrt(jnp.mean(xv * xv, axis=-1, keepdims=True) + EPS)
        xn = xv * r
        dxn = dh_v * (gv * one_sc)
        dx_ref[...] = dxo_ref[...] + r * (dxn - xn * jnp.mean(dxn * xn, axis=-1, keepdims=True))
        t = dh_v * xn
        dsh_ref[...] += jnp.sum(dh_v, axis=0, keepdims=True)
        dsc_ref[...] += jnp.sum(t * gv, axis=0, keepdims=True)
        dg_ref[...] += jnp.sum(t * one_sc, axis=0, keepdims=True)

    blk = pl.BlockSpec((ts, d), lambda i: (i, 0))
    row = pl.BlockSpec((1, d), lambda i: (0, 0))
    return _pcall(
        body, name=name, grid=(s // ts,),
        in_specs=[blk, blk, blk, row, row], out_specs=[blk, row, row, row],
        out_shape=[jax.ShapeDtypeStruct((s, d), F32)] + [jax.ShapeDtypeStruct((1, d), F32)] * 3,
        compiler_params=_params("arbitrary"),
    )(x, dh, dx_out, g, sc)


def _gate_bwd(dx, f, gt, k, name):
    s, d = dx.shape
    ts = _tile(s, 512)

    def body(dx_ref, f_ref, gt_ref, df_ref, dgt_ref):
        @pl.when(pl.program_id(0) == 0)
        def _():
            dgt_ref[...] = jnp.zeros_like(dgt_ref)

        dxv = dx_ref[...]
        df_ref[...] = ((k * gt_ref[...]) * dxv).astype(BF16)
        dgt_ref[...] += k * jnp.sum(f_ref[...] * dxv, axis=0, keepdims=True)

    blk = pl.BlockSpec((ts, d), lambda i: (i, 0))
    row = pl.BlockSpec((1, d), lambda i: (0, 0))
    return _pcall(
        body, name=name, grid=(s // ts,),
        in_specs=[blk, blk, row], out_specs=[blk, row],
        out_shape=[jax.ShapeDtypeStruct((s, d), BF16), jax.ShapeDtypeStruct((1, d), F32)],
        compiler_params=_params("arbitrary"),
    )(dx, f, gt)


def _ffn_up(h, wg, wu, layer, name, carry=None):
    s, d = h.shape
    fs = wg.shape[-1]
    tm = _tile(s, 512)

    def body(h_ref, wg_ref, wu_ref, a_ref, b_ref, s_ref):
        hv = h_ref[...]
        a = jnp.dot(hv, wg_ref[...], preferred_element_type=F32)
        b = jnp.dot(hv, wu_ref[...], preferred_element_type=F32)
        a_ref[...] = a
        b_ref[...] = b
        s_ref[...] = (a * _sigmoid(a) * b).astype(BF16)

    wspec = pl.BlockSpec((None, None, d, fs), lambda j, m: (j, layer, 0, 0))
    ospec = pl.BlockSpec((None, tm, fs), lambda j, m: (j, m, 0))
    return _hosted(
        body, name=name, grid=(N_DEV, s // tm),
        in_specs=[pl.BlockSpec((tm, d), lambda j, m: (m, 0)), wspec, wspec],
        out_specs=[ospec, ospec, ospec],
        out_shape=[jax.ShapeDtypeStruct((N_DEV, s, fs), F32)] * 2 + [jax.ShapeDtypeStruct((N_DEV, s, fs), BF16)],
        args=(h, wg, wu), carry=carry)


def _ffn_down(sv, wd, layer, x_in, gt, name, carry=None):
    _, s, fs = sv.shape
    d = wd.shape[-1]
    tm = _tile(s, 512)

    def body(s_ref, wd_ref, x_ref, gt_ref, f_ref, xo_ref, acc):
        j = pl.program_id(1)

        @pl.when(j == 0)
        def _():
            acc[...] = jnp.zeros_like(acc)

        acc[...] += jnp.dot(s_ref[...], wd_ref[...], preferred_element_type=F32)

        @pl.when(j == N_DEV - 1)
        def _():
            fv = acc[...]
            f_ref[...] = fv
            xo_ref[...] = x_ref[...] + (MACARON_W * gt_ref[...]) * fv

    blk = pl.BlockSpec((tm, d), lambda m, j: (m, 0))
    return _hosted(
        body, name=name, grid=(s // tm, N_DEV),
        in_specs=[pl.BlockSpec((None, tm, fs), lambda m, j: (j, m, 0)),
                  pl.BlockSpec((None, None, fs, d), lambda m, j: (j, layer, 0, 0)),
                  blk, pl.BlockSpec((1, d), lambda m, j: (0, 0))],
        out_specs=[blk, blk],
        out_shape=[jax.ShapeDtypeStruct((s, d), F32)] * 2,
        scratch_shapes=[pltpu.VMEM((tm, d), F32)],
        args=(sv, wd, x_in, gt), carry=carry)


def _ffn_bwd_act(df, wd, layer, a, b, name, carry=None):
    s, d = df.shape
    fs = a.shape[-1]
    tm = _tile(s, 512)

    def body(df_ref, wd_ref, a_ref, b_ref, da_ref, db_ref):
        ds = lax.dot_general(df_ref[...], wd_ref[...], (NT, ((), ())), preferred_element_type=F32)
        av = a_ref[...]
        sg = _sigmoid(av)
        da_ref[...] = (ds * b_ref[...] * (sg * (1.0 + av * (1.0 - sg)))).astype(BF16)
        db_ref[...] = (ds * (av * sg)).astype(BF16)

    hid = pl.BlockSpec((None, tm, fs), lambda j, m: (j, m, 0))
    return _hosted(
        body, name=name, grid=(N_DEV, s // tm),
        in_specs=[pl.BlockSpec((tm, d), lambda j, m: (m, 0)),
                  pl.BlockSpec((None, None, fs, d), lambda j, m: (j, layer, 0, 0)), hid, hid],
        out_specs=[hid, hid],
        out_shape=[jax.ShapeDtypeStruct((N_DEV, s, fs), BF16)] * 2,
        args=(df, wd, a, b), carry=carry)


def _ffn_bwd_wd(sv, df, name, carry=None):
    _, s, fs = sv.shape
    d = df.shape[1]
    tk = _tile(s, 512)
    nk = s // tk

    def body(s_ref, df_ref, o_ref, acc):
        @pl.when(pl.program_id(1) == 0)
        def _():
            acc[...] = jnp.zeros_like(acc)

        acc[...] += lax.dot_general(s_ref[...], df_ref[...], (TN, ((), ())), preferred_element_type=F32)

        @pl.when(pl.program_id(1) == nk - 1)
        def _():
            o_ref[...] = acc[...].astype(BF16)

    return _hosted(
        body, name=name, grid=(N_DEV, nk),
        in_specs=[pl.BlockSpec((None, tk, fs), lambda j, k: (j, k, 0)), pl.BlockSpec((tk, d), lambda j, k: (k, 0))],
        out_specs=[pl.BlockSpec((None, fs, d), lambda j, k: (j, 0, 0))],
        out_shape=[jax.ShapeDtypeStruct((N_DEV, fs, d), BF16)],
        scratch_shapes=[pltpu.VMEM((fs, d), F32)],
        args=(sv, df), carry=carry)


def _ffn_bwd_h(da, db, wg, wu, layer, name, carry=None):
    _, s, fs = da.shape
    d = wg.shape[-2]
    tm = _tile(s, 1024)

    def body(da_ref, db_ref, wg_ref, wu_ref, o_ref, acc):
        j = pl.program_id(1)

        @pl.when(j == 0)
        def _():
            acc[...] = jnp.zeros_like(acc)

        acc[...] += (lax.dot_general(da_ref[...], wg_ref[...], (NT, ((), ())), preferred_element_type=F32)
                     + lax.dot_general(db_ref[...], wu_ref[...], (NT, ((), ())), preferred_element_type=F32))

        @pl.when(j == N_DEV - 1)
        def _():
            o_ref[...] = acc[...]

    hid = pl.BlockSpec((None, tm, fs), lambda m, j: (j, m, 0))
    wspec = pl.BlockSpec((None, None, d, fs), lambda m, j: (j, layer, 0, 0))
    return _hosted(
        body, name=name, grid=(s // tm, N_DEV),
        in_specs=[hid, hid, wspec, wspec],
        out_specs=[pl.BlockSpec((tm, d), lambda m, j: (m, 0))],
        out_shape=[jax.ShapeDtypeStruct((s, d), F32)],
        scratch_shapes=[pltpu.VMEM((tm, d), F32)],
        args=(da, db, wg, wu), carry=carry)


def _ffn_bwd_wgu(h, da, db, name):
    s, d = h.shape
    fs = da.shape[-1]
    tk = _tile(s, 512)
    nk = s // tk

    def body(h_ref, da_ref, db_ref, og_ref, ou_ref, accg, accu):
        @pl.when(pl.program_id(1) == 0)
        def _():
            accg[...] = jnp.zeros_like(accg)
            accu[...] = jnp.zeros_like(accu)

        hv = h_ref[...]
        accg[...] += lax.dot_general(hv, da_ref[...], (TN, ((), ())), preferred_element_type=F32)
        accu[...] += lax.dot_general(hv, db_ref[...], (TN, ((), ())), preferred_element_type=F32)

        @pl.when(pl.program_id(1) == nk - 1)
        def _():
            og_ref[...] = accg[...].astype(BF16)
            ou_ref[...] = accu[...].astype(BF16)

    hid = pl.BlockSpec((None, tk, fs), lambda j, k: (j, k, 0))
    ospec = pl.BlockSpec((None, d, fs), lambda j, k: (j, 0, 0))
    return _pcall(
        body, name=name, grid=(N_DEV, nk),
        in_specs=[pl.BlockSpec((tk, d), lambda j, k: (k, 0)), hid, hid],
        out_specs=[ospec, ospec],
        out_shape=[jax.ShapeDtypeStruct((N_DEV, d, fs), BF16)] * 2,
        scratch_shapes=[pltpu.VMEM((d, fs), F32), pltpu.VMEM((d, fs), F32)],
        compiler_params=_params("parallel", "arbitrary"),
    )(h, da, db)


def _mm(a, b, *, ta=False, tb=False, out_dtype=F32, name, tm=1024, tn=1024, tk=2048, residual=None):
    m, kdim = (a.shape[1], a.shape[0]) if ta else a.shape
    n = b.shape[0] if tb else b.shape[1]
    tm, tn, tk = _tile(m, tm), _tile(n, tn), _tile(kdim, tk)
    nk = kdim // tk
    dims = ((0,) if ta else (1,), (1,) if tb else (0,))

    def body(*refs):
        a_ref, b_ref = refs[:2]
        acc = refs[-1]
        kk = pl.program_id(2)

        @pl.when(kk == 0)
        def _():
            acc[...] = jnp.zeros_like(acc)

        acc[...] += lax.dot_general(a_ref[...].astype(BF16), b_ref[...].astype(BF16), (dims, ((), ())),
                                    preferred_element_type=F32)

        @pl.when(kk == nk - 1)
        def _():
            if residual is None:
                refs[2][...] = acc[...].astype(out_dtype)
            else:
                res_ref, gate_ref, y_ref, xo_ref = refs[2:6]
                yv = acc[...]
                y_ref[...] = yv
                xo_ref[...] = res_ref[...] + gate_ref[...] * yv

    a_spec = pl.BlockSpec((tk, tm), lambda i, j, k: (k, i)) if ta else pl.BlockSpec((tm, tk), lambda i, j, k: (i, k))
    b_spec = pl.BlockSpec((tn, tk), lambda i, j, k: (j, k)) if tb else pl.BlockSpec((tk, tn), lambda i, j, k: (k, j))
    o_spec = pl.BlockSpec((tm, tn), lambda i, j, k: (i, j))
    if residual is None:
        in_specs, out_specs = [a_spec, b_spec], o_spec
        out_shape = jax.ShapeDtypeStruct((m, n), out_dtype)
        args = (a, b)
    else:
        in_specs = [a_spec, b_spec, o_spec, pl.BlockSpec((1, tn), lambda i, j, k: (0, j))]
        out_specs = [o_spec, o_spec]
        out_shape = [jax.ShapeDtypeStruct((m, n), F32)] * 2
        args = (a, b) + tuple(residual)
    return _pcall(
        body, name=name, grid=(m // tm, n // tn, nk), in_specs=in_specs, out_specs=out_specs, out_shape=out_shape,
        scratch_shapes=[pltpu.VMEM((tm, tn), F32)],
        compiler_params=_params("parallel", "parallel", "arbitrary"),
    )(*args)


def _log_sigmoid(z):
    return jnp.minimum(z, 0.0) - jnp.log(1.0 + jnp.exp(-jnp.abs(z)))


def _fox_gate(proj, small_blk, bias_lane):
    s = proj.shape[0]
    ts = _tile(s, 1024)
    nsub = ts // LANES

    def body(z_ref, b_ref, cum_ref, carry):
        @pl.when(pl.program_id(0) == 0)
        def _():
            carry[...] = jnp.zeros_like(carry)

        ii = lax.broadcasted_iota(jnp.int32, (LANES, LANES), 0)
        jj = lax.broadcasted_iota(jnp.int32, (LANES, LANES), 1)
        tri = (ii >= jj).astype(BF16)
        logf = _log_sigmoid(z_ref[...] + b_ref[...])
        cv = carry[...]
        for sb in range(nsub):
            blk = logf[sb * LANES:(sb + 1) * LANES, :]
            cum_ref[sb * LANES:(sb + 1) * LANES, :] = _dot_exact_lhs(tri, blk) + cv
            cv = cv + jnp.sum(blk, axis=0, keepdims=True)
        carry[...] = cv

    return _pcall(
        body, name="fox_gate", grid=(s // ts,),
        in_specs=[pl.BlockSpec((ts, LANES), lambda i: (i, small_blk)), pl.BlockSpec((1, LANES), lambda i: (0, 0))],
        out_specs=pl.BlockSpec((ts, LANES), lambda i: (i, 0)),
        out_shape=jax.ShapeDtypeStruct((s, LANES), F32),
        scratch_shapes=[pltpu.VMEM((1, LANES), F32)],
        compiler_params=_params("arbitrary"),
    )(proj, bias_lane)


def _fox_gate_bwd(dcum_q, dcum_k, proj, small_blk, bias_lane):
    s = proj.shape[0]
    ts = _tile(s, 1024)
    nsub = ts // LANES
    nb = s // ts

    def body(dcq_ref, dc_ref, z_ref, b_ref, dz_ref, db_ref, carry):
        @pl.when(pl.program_id(0) == 0)
        def _():
            carry[...] = jnp.zeros_like(carry)
            db_ref[...] = jnp.zeros_like(db_ref)

        ii = lax.broadcasted_iota(jnp.int32, (LANES, LANES), 0)
        jj = lax.broadcasted_iota(jnp.int32, (LANES, LANES), 1)
        triu = (jj >= ii).astype(BF16)
        dc = dcq_ref[...] + dc_ref[...]
        zb = z_ref[...] + b_ref[...]
        cv = carry[...]
        dbv = jnp.zeros((1, LANES), F32)
        for sb in reversed(range(nsub)):
            rows = slice(sb * LANES, (sb + 1) * LANES)
            blk = dc[rows, :]
            dlogf = _dot_exact_lhs(triu, blk) + cv
            cv = cv + jnp.sum(blk, axis=0, keepdims=True)
            dz = dlogf * _sigmoid(-zb[rows, :])
            dz_ref[rows, :] = dz
            dbv = dbv + jnp.sum(dz, axis=0, keepdims=True)
        carry[...] = cv
        db_ref[...] += dbv

    row = pl.BlockSpec((1, LANES), lambda i: (0, 0))
    return _pcall(
        body, name="fox_gate_bwd", grid=(nb,),
        in_specs=[pl.BlockSpec((ts, LANES), lambda i: (nb - 1 - i, 0)),
                  pl.BlockSpec((ts, LANES), lambda i: (nb - 1 - i, 0)),
                  pl.BlockSpec((ts, LANES), lambda i: (nb - 1 - i, small_blk)), row],
        out_specs=[pl.BlockSpec((ts, LANES), lambda i: (nb - 1 - i, 0)), row],
        out_shape=[jax.ShapeDtypeStruct((s, LANES), F32), jax.ShapeDtypeStruct((1, LANES), F32)],
        scratch_shapes=[pltpu.VMEM((1, LANES), F32)],
        compiler_params=_params("arbitrary"),
    )(dcum_q, dcum_k, proj, bias_lane)


def _tri_tables(n, by_key):
    if by_key:
        pairs = [(i, j) for j in range(n) for i in range(j, n)]
    else:
        pairs = [(i, j) for i in range(n) for j in range(i + 1)]
    return (jnp.asarray(np.array([p[0] for p in pairs], np.int32)),
            jnp.asarray(np.array([p[1] for p in pairs], np.int32)))


def _fox_group(heads):
    return FOX_HEADS_PER_STEP if heads % FOX_HEADS_PER_STEP == 0 else 1


def _as_row(col):
    t = col.shape[0]
    eye = lax.broadcasted_iota(jnp.int32, (t, t), 0) == lax.broadcasted_iota(jnp.int32, (t, t), 1)
    return jnp.sum(jnp.where(eye, col, 0.0), axis=0, keepdims=True)


def _fox_scores(a, b, bias_col, bias_row, scale, diagonal, rows_are_keys=False):
    sc = lax.dot_general(a.astype(BF16), b.astype(BF16), (NT, ((), ())), preferred_element_type=F32) * scale
    sc = sc + (bias_col + bias_row)
    if not diagonal:
        return sc
    row = lax.broadcasted_iota(jnp.int32, sc.shape, 0)
    col = lax.broadcasted_iota(jnp.int32, sc.shape, 1)
    return jnp.where(row <= col if rows_are_keys else col <= row, sc, NEG)


def _fox_fwd(proj, cum_col, cum_row, w_norm, heads, carry=None):
    s = proj.shape[0]
    t = _tile(s, 512)
    grp = _fox_group(heads)
    qi, ki = _tri_tables(s // t, False)
    scale = 1.0 / math.sqrt(HEAD_DIM)

    def body(qi_ref, ki_ref, q_ref, k_ref, v_ref, cq_ref, ck_ref, w_ref, o_ref, lse_ref, lser_ref, on_ref, m_s, acc_s):
        iq, ik = qi_ref[pl.program_id(1)], ki_ref[pl.program_id(1)]

        @pl.when(ik == 0)
        def _():
            m_s[...] = jnp.full_like(m_s, NEG)
            acc_s[...] = jnp.zeros_like(acc_s)

        def step(diagonal):
            for g in range(grp):
                sl = slice(g * HEAD_DIM, (g + 1) * HEAD_DIM)
                sc = _fox_scores(q_ref[:, sl], k_ref[:, sl], cq_ref[g, :, 0:1], -ck_ref[g], scale, diagonal)
                m_prev = m_s[g]
                m_new = jnp.maximum(m_prev, jnp.max(sc, axis=1, keepdims=True))
                p = jnp.exp(sc - m_new).astype(BF16)
                v_ones = jnp.concatenate([v_ref[:, sl].astype(BF16), jnp.ones((t, LANES), BF16)], axis=1)
                acc_s[g] = jnp.exp(m_prev - m_new) * acc_s[g] + jnp.dot(p, v_ones, preferred_element_type=F32)
                m_s[g] = m_new

        @pl.when(ik < iq)
        def _():
            step(False)

        @pl.when(ik == iq)
        def _():
            step(True)
            for g in range(grp):
                sl = slice(g * HEAD_DIM, (g + 1) * HEAD_DIM)
                acc = acc_s[g]
                o = acc[:, :HEAD_DIM] / acc[:, HEAD_DIM:]
                lse = m_s[g] + jnp.log(acc[:, HEAD_DIM:])
                o_ref[:, sl] = o
                lse_ref[g] = lse
                lser_ref[g] = _as_row(lse[:, 0:1])
                r = lax.rsqrt(jnp.mean(o * o, axis=1, keepdims=True) + EPS)
                on_ref[:, sl] = (o * r * w_ref[...]).astype(BF16)

    ng = heads // grp
    qblk = pl.BlockSpec((t, grp * HEAD_DIM), lambda h, p, qi, ki: (qi[p], h))
    kblk = lambda off: pl.BlockSpec((t, grp * HEAD_DIM), lambda h, p, qi, ki: (ki[p], off + h))
    qcol = pl.BlockSpec((grp, t, LANES), lambda h, p, qi, ki: (h, qi[p], 0))
    return _hosted(
        body, name="fox_fwd", grid=(ng, int(qi.shape[0])), prefetch=(qi, ki),
        in_specs=[qblk, kblk(ng), kblk(2 * ng), qcol,
                  pl.BlockSpec((grp, 1, t), lambda h, p, qi, ki: (h, 0, ki[p])),
                  pl.BlockSpec((1, HEAD_DIM), lambda h, p, qi, ki: (0, 0))],
        out_specs=[qblk, qcol, pl.BlockSpec((grp, 1, t), lambda h, p, qi, ki: (h, 0, qi[p])), qblk],
        scratch_shapes=[pltpu.VMEM((grp, t, 1), F32), pltpu.VMEM((grp, t, 2 * HEAD_DIM), F32)],
        out_shape=[jax.ShapeDtypeStruct((s, heads * HEAD_DIM), F32), jax.ShapeDtypeStruct((heads, s, LANES), F32),
                   jax.ShapeDtypeStruct((heads, 1, s), F32), jax.ShapeDtypeStruct((s, heads * HEAD_DIM), BF16)],
        args=(proj, proj, proj, cum_col, cum_row, w_norm), carry=carry)


def _fox_prep_bwd(do_cat, o_raw, w_norm, heads):
    s = o_raw.shape[0]
    ts = _tile(s, 512)

    def body(g_ref, o_ref, w_ref, do_ref, delta_ref, deltar_ref, dw_ref):
        @pl.when((pl.program_id(0) == 0) & (pl.program_id(1) == 0))
        def _():
            dw_ref[...] = jnp.zeros_like(dw_ref)

        o = o_ref[...]
        g = g_ref[...]
        r = lax.rsqrt(jnp.mean(o * o, axis=1, keepdims=True) + EPS)
        wg = g * w_ref[...]
        do = r * wg - o * (r * r * r) * jnp.mean(wg * o, axis=1, keepdims=True)
        do_ref[...] = do.astype(BF16)
        delta = jnp.sum(do * o, axis=1, keepdims=True)
        delta_ref[...] = jnp.broadcast_to(delta, delta_ref.shape)
        deltar_ref[...] = _as_row(delta)
        dw_ref[...] += jnp.sum(g * o * r, axis=0, keepdims=True)

    blk = pl.BlockSpec((ts, HEAD_DIM), lambda h, i: (i, h))
    row = pl.BlockSpec((1, HEAD_DIM), lambda h, i: (0, 0))
    return _pcall(
        body, name="fox_prep_bwd", grid=(heads, s // ts),
        in_specs=[blk, blk, row],
        out_specs=[blk, pl.BlockSpec((None, ts, LANES), lambda h, i: (h, i, 0)),
                   pl.BlockSpec((None, 1, ts), lambda h, i: (h, 0, i)), row],
        out_shape=[jax.ShapeDtypeStruct((s, heads * HEAD_DIM), BF16), jax.ShapeDtypeStruct((heads, s, LANES), F32),
                   jax.ShapeDtypeStruct((heads, 1, s), F32), jax.ShapeDtypeStruct((1, HEAD_DIM), F32)],
        compiler_params=_params("arbitrary", "arbitrary"),
    )(do_cat, o_raw, w_norm)


def _fox_dq(proj, do, cum_col, cum_row, lse, delta, heads, carry=None):
    s = proj.shape[0]
    t = _tile(s, 512)
    grp = _fox_group(heads)
    qi, ki = _tri_tables(s // t, False)
    scale = 1.0 / math.sqrt(HEAD_DIM)

    def body(qi_ref, ki_ref, q_ref, k_ref, v_ref, do_ref, cq_ref, ck_ref, lse_ref, dl_ref, dq_ref, dc_ref, acc, dc_acc):
        iq, ik = qi_ref[pl.program_id(1)], ki_ref[pl.program_id(1)]

        @pl.when(ik == 0)
        def _():
            acc[...] = jnp.zeros_like(acc)
            dc_acc[...] = jnp.zeros_like(dc_acc)

        def step(diagonal):
            for g in range(grp):
                sl = slice(g * HEAD_DIM, (g + 1) * HEAD_DIM)
                kv = k_ref[:, sl]
                sc = _fox_scores(q_ref[:, sl], kv, cq_ref[g, :, 0:1] - lse_ref[g, :, 0:1], -ck_ref[g], scale, diagonal)
                p = jnp.exp(sc)
                dp = _dot(do_ref[:, sl], v_ref[:, sl], NT)
                ds = p * (dp - dl_ref[g, :, 0:1])
                acc[g] += _dot(ds, kv, NN)
                dc_acc[g] += jnp.sum(ds, axis=1, keepdims=True)

        @pl.when(ik < iq)
        def _():
            step(False)

        @pl.when(ik == iq)
        def _():
            step(True)
            for g in range(grp):
                dq_ref[:, g * HEAD_DIM:(g + 1) * HEAD_DIM] = (acc[g] * scale).astype(BF16)
                dc_ref[g] = _as_row(dc_acc[g])

    ng = heads // grp
    qblk = pl.BlockSpec((t, grp * HEAD_DIM), lambda h, p, qi, ki: (qi[p], h))
    kblk = lambda off: pl.BlockSpec((t, grp * HEAD_DIM), lambda h, p, qi, ki: (ki[p], off + h))
    qcol = pl.BlockSpec((grp, t, LANES), lambda h, p, qi, ki: (h, qi[p], 0))
    return _hosted(
        body, name="fox_dq", grid=(ng, int(qi.shape[0])), prefetch=(qi, ki),
        in_specs=[qblk, kblk(ng), kblk(2 * ng), qblk, qcol,
                  pl.BlockSpec((grp, 1, t), lambda h, p, qi, ki: (h, 0, ki[p])), qcol, qcol],
        out_specs=[qblk, pl.BlockSpec((grp, 1, t), lambda h, p, qi, ki: (h, 0, qi[p]))],
        scratch_shapes=[pltpu.VMEM((grp, t, HEAD_DIM), F32), pltpu.VMEM((grp, t, 1), F32)],
        out_shape=[jax.ShapeDtypeStruct((s, heads * HEAD_DIM), BF16), jax.ShapeDtypeStruct((heads, 1, s), F32)],
        args=(proj, proj, proj, do, cum_col, cum_row, lse, delta), carry=carry)


def _fox_dkv(proj, do, cum_col, cum_row, lse_row, delta_row, heads, carry=None):
    s = proj.shape[0]
    t = _tile(s, 512)
    nk = s // t
    grp = _fox_group(heads)
    qi, ki = _tri_tables(nk, True)
    scale = 1.0 / math.sqrt(HEAD_DIM)

    def body(qi_ref, ki_ref, q_ref, k_ref, v_ref, do_ref, cqr_ref, ckc_ref, lse_ref, dl_ref, dk_ref, dv_ref, dc_ref,
             dk_acc, dv_acc, dc_acc):
        iq, ik = qi_ref[pl.program_id(1)], ki_ref[pl.program_id(1)]

        def step(diagonal):
            for g in range(grp):
                sl = slice(g * HEAD_DIM, (g + 1) * HEAD_DIM)
                qv = q_ref[:, sl]
                dov = do_ref[:, sl]
                st = _fox_scores(k_ref[:, sl], qv, -ckc_ref[g, :, 0:1], cqr_ref[g] - lse_ref[g], scale, diagonal, True)
                pt = jnp.exp(st)
                dv_acc[g] += _dot(pt, dov, NN)
                dpt = _dot(v_ref[:, sl], dov, NT)
                dst = pt * (dpt - dl_ref[g])
                dk_acc[g] += _dot(dst, qv, NN)
                dc_acc[g] += jnp.sum(dst, axis=1, keepdims=True)

        @pl.when(iq == ik)
        def _():
            dk_acc[...] = jnp.zeros_like(dk_acc)
            dv_acc[...] = jnp.zeros_like(dv_acc)
            dc_acc[...] = jnp.zeros_like(dc_acc)
            step(True)

        @pl.when(iq > ik)
        def _():
            step(False)

        @pl.when(iq == nk - 1)
        def _():
            for g in range(grp):
                sl = slice(g * HEAD_DIM, (g + 1) * HEAD_DIM)
                dk_ref[:, sl] = (dk_acc[g] * scale).astype(BF16)
                dv_ref[:, sl] = dv_acc[g].astype(BF16)
                dc_ref[g] = _as_row(-dc_acc[g])

    ng = heads // grp
    qblk = pl.BlockSpec((t, grp * HEAD_DIM), lambda h, p, qi, ki: (qi[p], h))
    qrow = pl.BlockSpec((grp, 1, t), lambda h, p, qi, ki: (h, 0, qi[p]))
    kblk = lambda off: pl.BlockSpec((t, grp * HEAD_DIM), lambda h, p, qi, ki: (ki[p], off + h))
    kout = pl.BlockSpec((t, grp * HEAD_DIM), lambda h, p, qi, ki: (ki[p], h))
    return _hosted(
        body, name="fox_dkv", grid=(ng, int(qi.shape[0])), prefetch=(qi, ki),
        in_specs=[qblk, kblk(ng), kblk(2 * ng), qblk, qrow,
                  pl.BlockSpec((grp, t, LANES), lambda h, p, qi, ki: (h, ki[p], 0)), qrow, qrow],
        out_specs=[kout, kout, pl.BlockSpec((grp, 1, t), lambda h, p, qi, ki: (h, 0, ki[p]))],
        scratch_shapes=[pltpu.VMEM((grp, t, HEAD_DIM), F32), pltpu.VMEM((grp, t, HEAD_DIM), F32),
                        pltpu.VMEM((grp, t, 1), F32)],
        out_shape=[jax.ShapeDtypeStruct((s, heads * HEAD_DIM), BF16)] * 2 + [jax.ShapeDtypeStruct((heads, 1, s), F32)],
        args=(proj, proj, proj, do, cum_row, cum_col, lse_row, delta_row), carry=carry)


def _shift_rows(xv, halo, j, forward):
    n = xv.shape[0]
    rid = lax.broadcasted_iota(jnp.int32, (8, xv.shape[1]), 0)
    if forward:
        xs = pltpu.roll(xv, n - j, 0)
        hs = pltpu.roll(halo, 8 - j, 0)
        edge = jnp.where(rid >= 8 - j, hs, xs[n - 8:, :])
        return jnp.concatenate([xs[:n - 8, :], edge], axis=0)
    xs = pltpu.roll(xv, j, 0)
    hs = pltpu.roll(halo, j, 0)
    edge = jnp.where(rid < j, hs, xs[:8, :])
    return jnp.concatenate([edge, xs[8:, :]], axis=0)


def _conv_silu(xv, halo, w):
    xc = w[CONV_W - 1:CONV_W, :] * xv
    for j in range(1, CONV_W):
        xc = xc + w[CONV_W - 1 - j:CONV_W - j, :] * _shift_rows(xv, halo, j, False)
    return xc, xc * _sigmoid(xc)


def _gdn_pre(proj, conv_w, heads):
    s = proj.shape[0]
    cw = 3 * heads * HEAD_DIM
    ts = _tile(s, 256)
    tb = ts // 8

    def body(x_ref, halo_ref, w_ref, q_ref, k_ref, v_ref):
        halo = jnp.where(pl.program_id(0) == 0, 0.0, halo_ref[...])
        _, y = _conv_silu(x_ref[...], halo, w_ref[...])
        for h in range(heads):
            for part, ref in enumerate((q_ref, k_ref, v_ref)):
                c0 = (part * heads + h) * HEAD_DIM
                blk = y[:, c0:c0 + HEAD_DIM]
                if part < 2:
                    blk = blk * lax.rsqrt(jnp.sum(blk * blk, axis=1, keepdims=True) + EPS)
                ref[h] = blk

    out = pl.BlockSpec((heads, ts, HEAD_DIM), lambda i: (0, i, 0))
    return _pcall(
        body, name="gdn_pre", grid=(s // ts,),
        in_specs=[pl.BlockSpec((ts, cw), lambda i: (i, 1)),
                  pl.BlockSpec((8, cw), lambda i: (jnp.maximum(i * tb - 1, 0), 1)),
                  pl.BlockSpec((CONV_W, cw), lambda i: (0, 0))],
        out_specs=[out, out, out],
        out_shape=[jax.ShapeDtypeStruct((heads, s, HEAD_DIM), F32)] * 3,
        compiler_params=_params("parallel"),
    )(proj, proj, conv_w)


def _gdn_pre_bwd_act(proj, conv_w, dq, dk, dv, heads):
    s = proj.shape[0]
    cw = 3 * heads * HEAD_DIM
    ts = _tile(s, 256)
    tb = ts // 8

    def body(x_ref, halo_ref, w_ref, dq_ref, dk_ref, dv_ref, dxc_ref, dw_ref):
        @pl.when(pl.program_id(0) == 0)
        def _():
            dw_ref[...] = jnp.zeros_like(dw_ref)

        xv = x_ref[...]
        halo = jnp.where(pl.program_id(0) == 0, 0.0, halo_ref[...])
        xc, y = _conv_silu(xv, halo, w_ref[...])
        sg = _sigmoid(xc)
        dsilu = sg * (1.0 + xc * (1.0 - sg))
        for h in range(heads):
            for part, ref in enumerate((dq_ref, dk_ref, dv_ref)):
                c0 = (part * heads + h) * HEAD_DIM
                g = ref[h]
                if part < 2:
                    blk = y[:, c0:c0 + HEAD_DIM]
                    r = lax.rsqrt(jnp.sum(blk * blk, axis=1, keepdims=True) + EPS)
                    g = r * g - blk * (r * r * r) * jnp.sum(g * blk, axis=1, keepdims=True)
                dxc_ref[:, c0:c0 + HEAD_DIM] = g * dsilu[:, c0:c0 + HEAD_DIM]
        dxc = dxc_ref[...]
        rows = [jnp.sum(dxc * (xv if j == 0 else _shift_rows(xv, halo, j, False)), axis=0, keepdims=True)
                for j in range(CONV_W)]
        dw_ref[...] += jnp.concatenate([rows[CONV_W - 1 - k] for k in range(CONV_W)]
                                       + [jnp.zeros((8 - CONV_W, cw), F32)], axis=0)

    hblk = pl.BlockSpec((heads, ts, HEAD_DIM), lambda i: (0, i, 0))
    return _pcall(
        body, name="gdn_pre_bwd_act", grid=(s // ts,),
        in_specs=[pl.BlockSpec((ts, cw), lambda i: (i, 1)),
                  pl.BlockSpec((8, cw), lambda i: (jnp.maximum(i * tb - 1, 0), 1)),
                  pl.BlockSpec((CONV_W, cw), lambda i: (0, 0)), hblk, hblk, hblk],
        out_specs=[pl.BlockSpec((ts, cw), lambda i: (i, 0)), pl.BlockSpec((8, cw), lambda i: (0, 0))],
        out_shape=[jax.ShapeDtypeStruct((s, cw), F32), jax.ShapeDtypeStruct((8, cw), F32)],
        compiler_params=_params("arbitrary"),
    )(proj, proj, conv_w, dq, dk, dv)


def _gdn_pre_bwd_conv(dxc, conv_w):
    s, cw = dxc.shape
    ts = _tile(s, 256)
    tb = ts // 8
    last = s // 8 - 1

    def body(g_ref, halo_ref, w_ref, dx_ref):
        gv = g_ref[...]
        w = w_ref[...]
        halo = jnp.where(pl.program_id(0) == s // ts - 1, 0.0, halo_ref[...])
        dx = w[CONV_W - 1:CONV_W, :] * gv
        for j in range(1, CONV_W):
            dx = dx + w[CONV_W - 1 - j:CONV_W - j, :] * _shift_rows(gv, halo, j, True)
        dx_ref[...] = dx.astype(BF16)

    return _pcall(
        body, name="gdn_pre_bwd_conv", grid=(s // ts,),
        in_specs=[pl.BlockSpec((ts, cw), lambda i: (i, 0)),
                  pl.BlockSpec((8, cw), lambda i: (jnp.minimum((i + 1) * tb, last), 0)),
                  pl.BlockSpec((CONV_W, cw), lambda i: (0, 0))],
        out_specs=pl.BlockSpec((ts, cw), lambda i: (i, 0)),
        out_shape=jax.ShapeDtypeStruct((s, cw), BF16),
        compiler_params=_params("parallel"),
    )(dxc, dxc, conv_w)


def _bdot(a, b, ca, cb):
    return lax.dot_general(a.astype(BF16), b.astype(BF16), (((ca,), (cb,)), ((0,), (0,))),
                           preferred_element_type=F32)


def _bdot_hp(a, b, ca, cb):
    ah = a.astype(BF16)
    al = (a - ah.astype(F32)).astype(BF16)
    bh = b.astype(BF16)
    bl = (b - bh.astype(F32)).astype(BF16)
    d = lambda p, q: lax.dot_general(p, q, (((ca,), (cb,)), ((0,), (0,))), preferred_element_type=F32)
    return d(ah, bh) + (d(ah, bl) + d(al, bh))


def _gdn_gates(small, a_lane, dt_lane, heads):
    lane = lax.broadcasted_iota(jnp.int32, small.shape, 1)
    za = small + dt_lane
    g_all = -jnp.exp(a_lane) * (jnp.maximum(za, 0.0) + jnp.log(1.0 + jnp.exp(-jnp.abs(za))))
    b_all = _sigmoid(small)
    pick = lambda v, l: jnp.sum(jnp.where(lane == l, v, 0.0), axis=1, keepdims=True)
    g = jnp.stack([pick(g_all, heads + h) for h in range(heads)], axis=0)
    beta = jnp.stack([pick(b_all, 2 * heads + h) for h in range(heads)], axis=0)
    return g, beta


def _chunk_masks(c):
    ii = lax.broadcasted_iota(jnp.int32, (1, c, c), 1)
    jj = lax.broadcasted_iota(jnp.int32, (1, c, c), 2)
    return ii >= jj, ii > jj, ii == jj


def _col_to_row(col, eye):
    return jnp.sum(jnp.where(eye, col, 0.0), axis=1, keepdims=True)


def _row_to_col(row, eye):
    return jnp.sum(jnp.where(eye, row, 0.0), axis=2, keepdims=True)


def _gdn_chunk(q, k, v, g, beta, state):
    c = q.shape[1]
    incl, strict, eye = _chunk_masks(c)
    g_row = _col_to_row(g, eye)
    gc_col = jnp.sum(jnp.where(incl, g_row, 0.0), axis=2, keepdims=True)
    gc_row = _col_to_row(gc_col, eye)
    gam = jnp.where(incl, jnp.exp(jnp.where(incl, gc_col - gc_row, NEG)), 0.0)
    egc = jnp.exp(gc_col)
    kb = k * beta
    vb = v * beta
    kbe = kb * egc
    low = jnp.where(strict, _bdot(kb, k, 2, 2), 0.0) * gam
    p = -low
    tinv = jnp.where(eye, 1.0, 0.0) + p
    width = 2
    while width < c:
        p = _bdot_hp(p, p, 2, 1)
        tinv = tinv + _bdot_hp(tinv, p, 2, 1)
        width *= 2
    u = _bdot(tinv, vb, 2, 1)
    w = _bdot(tinv, kbe, 2, 1)
    att = jnp.where(incl, _bdot(q, k, 2, 2), 0.0) * gam
    vn = u - _bdot(w, state, 2, 1)
    qe = q * egc
    o = _bdot(qe, state, 2, 1) + _bdot(att, vn, 2, 1)
    gl = jnp.sum(g, axis=1, keepdims=True)
    edec = jnp.exp(gl - gc_col)
    kdec = k * edec
    egl = jnp.exp(gl)
    new_state = state * egl + _bdot(kdec, vn, 1, 1)
    return dict(incl=incl, strict=strict, eye=eye, gam=gam, egc=egc, kb=kb, vb=vb, kbe=kbe, low=low, tinv=tinv, w=w,
                att=att, vn=vn, qe=qe, o=o, edec=edec, kdec=kdec, egl=egl, new_state=new_state)


def _gdn_load(q_ref, k_ref, v_ref, small_ref, a_ref, dt_ref, rows, heads):
    q = q_ref[:, rows, :] * (HEAD_DIM ** -0.5)
    g, beta = _gdn_gates(small_ref[rows, :], a_ref[...], dt_ref[...], heads)
    return q, k_ref[:, rows, :], v_ref[:, rows, :], g, beta


def _gdn_fwd(q, k, v, proj, z_blk, small_blk, a_lane, dt_lane, w_norm):
    heads, s, _ = q.shape
    c = min(GDN_CHUNK, s)
    r = _tile(s, 512)
    npb = r // c
    gw = heads * HEAD_DIM

    def body(q_ref, k_ref, v_ref, z_ref, small_ref, a_ref, dt_ref, w_ref, o_ref, st_ref, state):
        @pl.when(pl.program_id(0) == 0)
        def _():
            state[...] = jnp.zeros_like(state)

        def chunk(cb, carry):
            rows = pl.ds(pl.multiple_of(cb * c, c), c)
            qv, kv, vv, g, beta = _gdn_load(q_ref, k_ref, v_ref, small_ref, a_ref, dt_ref, rows, heads)
            st = state[...]
            st_ref[:, cb] = st
            res = _gdn_chunk(qv, kv, vv, g, beta, st)
            state[...] = res["new_state"]
            o = res["o"]
            rn = lax.rsqrt(jnp.mean(o * o, axis=2, keepdims=True) + EPS)
            zv = z_ref[rows, :]
            for h in range(heads):
                zh = zv[:, h * HEAD_DIM:(h + 1) * HEAD_DIM]
                o_ref[rows, h * HEAD_DIM:(h + 1) * HEAD_DIM] = (
                    o[h] * rn[h] * w_ref[...] * (zh * _sigmoid(zh))).astype(BF16)
            return carry

        lax.fori_loop(0, npb, chunk, 0)

    hblk = pl.BlockSpec((heads, r, HEAD_DIM), lambda i: (0, i, 0))
    row = pl.BlockSpec((1, LANES), lambda i: (0, 0))
    return _pcall(
        body, name="gdn_fwd", grid=(s // r,),
        in_specs=[hblk, hblk, hblk, pl.BlockSpec((r, gw), lambda i: (i, z_blk)),
                  pl.BlockSpec((r, LANES), lambda i: (i, small_blk)), row, row, row],
        out_specs=[pl.BlockSpec((r, gw), lambda i: (i, 0)),
                   pl.BlockSpec((heads, npb, HEAD_DIM, HEAD_DIM), lambda i: (0, i, 0, 0))],
        out_shape=[jax.ShapeDtypeStruct((s, gw), BF16),
                   jax.ShapeDtypeStruct((heads, s // c, HEAD_DIM, HEAD_DIM), F32)],
        scratch_shapes=[pltpu.VMEM((heads, HEAD_DIM, HEAD_DIM), F32)],
        compiler_params=_params("arbitrary"),
    )(q, k, v, proj, proj, a_lane, dt_lane, w_norm)


def _gdn_bwd(q, k, v, proj, z_blk, small_blk, a_lane, dt_lane, w_norm, states, do_cat, do_blk):
    heads, s, _ = q.shape
    c = min(GDN_CHUNK, s)
    r = _tile(s, 512)
    npb = r // c
    nb = s // r
    gw = heads * HEAD_DIM

    def body(q_ref, k_ref, v_ref, z_ref, small_ref, a_ref, dt_ref, w_ref, st_ref, do_ref,
             dq_ref, dk_ref, dv_ref, dz_ref, dsm_ref, da_ref, ddt_ref, dw_ref, dstate):
        @pl.when(pl.program_id(0) == 0)
        def _():
            dstate[...] = jnp.zeros_like(dstate)
            da_ref[...] = jnp.zeros_like(da_ref)
            ddt_ref[...] = jnp.zeros_like(ddt_ref)
            dw_ref[...] = jnp.zeros_like(dw_ref)

        def chunk(it, carry):
            cb = npb - 1 - it
            rows = pl.ds(pl.multiple_of(cb * c, c), c)
            qv, kv, vv, g, beta = _gdn_load(q_ref, k_ref, v_ref, small_ref, a_ref, dt_ref, rows, heads)
            st = st_ref[:, cb]
            f = _gdn_chunk(qv, kv, vv, g, beta, st)
            incl, strict, eye = f["incl"], f["strict"], f["eye"]
            o = f["o"]
            wv = w_ref[...]
            zv = z_ref[rows, :]
            dov = do_ref[rows, :]
            rn = lax.rsqrt(jnp.mean(o * o, axis=2, keepdims=True) + EPS)
            do_l, dw_acc = [], jnp.zeros((1, HEAD_DIM), F32)
            for h in range(heads):
                sl = slice(h * HEAD_DIM, (h + 1) * HEAD_DIM)
                zh, gh = zv[:, sl], dov[:, sl]
                sg = _sigmoid(zh)
                on = o[h] * rn[h]
                dz_ref[rows, sl] = (gh * (on * wv) * (sg * (1.0 + zh * (1.0 - sg)))).astype(BF16)
                gn = gh * (zh * sg)
                dw_acc = dw_acc + jnp.sum(gn * on, axis=0, keepdims=True)
                wg = gn * wv
                do_l.append(rn[h] * wg - o[h] * (rn[h] * rn[h] * rn[h]) * jnp.mean(wg * o[h], axis=1, keepdims=True))
            dw_ref[...] += dw_acc
            do = jnp.stack(do_l, axis=0)
            ds_out = dstate[...]
            dvn = _bdot(f["att"], do, 1, 1) + _bdot(f["kdec"], ds_out, 2, 1)
            datt = jnp.where(incl, _bdot(do, f["vn"], 2, 2), 0.0)
            dqe = _bdot(do, st, 2, 2)
            dstate[...] = _bdot(f["qe"], do, 1, 1) + f["egl"] * ds_out - _bdot(f["w"], dvn, 1, 1)
            dw = -_bdot(dvn, st, 2, 2)
            dkdec = _bdot(f["vn"], ds_out, 2, 2)
            t_kdec = jnp.sum(dkdec * f["kdec"], axis=2, keepdims=True)
            dgl = (jnp.sum(jnp.sum(st * ds_out, axis=2, keepdims=True), axis=1, keepdims=True) * f["egl"]
                   + jnp.sum(t_kdec, axis=1, keepdims=True))
            dgc = jnp.sum(dqe * f["qe"], axis=2, keepdims=True) - t_kdec
            dq = dqe * f["egc"]
            dk = dkdec * f["edec"]
            dtinv = _bdot(dvn, f["vb"], 2, 2) + _bdot(dw, f["kbe"], 2, 2)
            dvb = _bdot(f["tinv"], dvn, 1, 1)
            dkbe = _bdot(f["tinv"], dw, 1, 1)
            dkb = dkbe * f["egc"]
            dgc = dgc + jnp.sum(dkbe * f["kbe"], axis=2, keepdims=True)
            dlow = jnp.where(strict, -_bdot_hp(_bdot_hp(f["tinv"], dtinv, 1, 1), f["tinv"], 2, 2), 0.0)
            ml = dlow * f["gam"]
            dkb = dkb + _bdot(ml, kv, 2, 1)
            dk = dk + _bdot(ml, f["kb"], 1, 1)
            ma = datt * f["gam"]
            dq = dq + _bdot(ma, kv, 2, 1)
            dk = dk + _bdot(ma, qv, 1, 1)
            e = dlow * f["low"] + datt * f["att"]
            dgc = dgc + jnp.sum(e, axis=2, keepdims=True) - _row_to_col(jnp.sum(e, axis=1, keepdims=True), eye)
            dk = dk + beta * dkb
            dbeta = jnp.sum(dkb * kv, axis=2, keepdims=True) + jnp.sum(dvb * vv, axis=2, keepdims=True)
            dgc_row = _col_to_row(dgc, eye)
            dg = jnp.sum(jnp.where(incl, 0.0, dgc_row) + jnp.where(eye, dgc_row, 0.0), axis=2, keepdims=True) + dgl
            dq_ref[:, rows, :] = dq * (HEAD_DIM ** -0.5)
            dk_ref[:, rows, :] = dk
            dv_ref[:, rows, :] = beta * dvb
            small = small_ref[rows, :]
            lane = lax.broadcasted_iota(jnp.int32, small.shape, 1)
            dg_l = jnp.zeros(small.shape, F32)
            db_l = jnp.zeros(small.shape, F32)
            for h in range(heads):
                dg_l = dg_l + jnp.where(lane == heads + h, dg[h], 0.0)
                db_l = db_l + jnp.where(lane == 2 * heads + h, dbeta[h], 0.0)
            za = small + dt_ref[...]
            nexp = -jnp.exp(a_ref[...])
            softplus = jnp.maximum(za, 0.0) + jnp.log(1.0 + jnp.exp(-jnp.abs(za)))
            da_logit = dg_l * nexp * _sigmoid(za)
            sb = _sigmoid(small)
            dsm_ref[rows, :] = da_logit + db_l * sb * (1.0 - sb)
            ddt_ref[...] += jnp.sum(da_logit, axis=0, keepdims=True)
            da_ref[...] += jnp.sum(dg_l * nexp * softplus, axis=0, keepdims=True)
            return carry

        lax.fori_loop(0, npb, chunk, 0)

    rev = lambda i: nb - 1 - i
    hblk = pl.BlockSpec((heads, r, HEAD_DIM), lambda i: (0, rev(i), 0))
    row = pl.BlockSpec((1, LANES), lambda i: (0, 0))
    wide = lambda blk: pl.BlockSpec((r, gw), lambda i: (rev(i), blk))
    return _pcall(
        body, name="gdn_bwd", grid=(nb,),
        in_specs=[hblk, hblk, hblk, wide(z_blk), pl.BlockSpec((r, LANES), lambda i: (rev(i), small_blk)),
                  row, row, row, pl.BlockSpec((heads, npb, HEAD_DIM, HEAD_DIM), lambda i: (0, rev(i), 0, 0)),
                  wide(do_blk)],
        out_specs=[hblk, hblk, hblk, wide(0), pl.BlockSpec((r, LANES), lambda i: (rev(i), 0)), row, row, row],
        out_shape=[jax.ShapeDtypeStruct((heads, s, HEAD_DIM), F32)] * 3
        + [jax.ShapeDtypeStruct((s, gw), BF16), jax.ShapeDtypeStruct((s, LANES), F32)]
        + [jax.ShapeDtypeStruct((1, LANES), F32)] * 3,
        scratch_shapes=[pltpu.VMEM((heads, HEAD_DIM, HEAD_DIM), F32)],
        compiler_params=_params("arbitrary"),
    )(q, k, v, proj, proj, a_lane, dt_lane, w_norm, states, do_cat)


def _final(x, target, gf):
    s, d = x.shape
    ts = _tile(s, 512)

    def body(x_ref, t_ref, g_ref, loss_ref, dx_ref, dg_ref):
        @pl.when(pl.program_id(0) == 0)
        def _():
            loss_ref[...] = jnp.zeros_like(loss_ref)
            dg_ref[...] = jnp.zeros_like(dg_ref)

        xv = x_ref[...]
        gv = g_ref[...]
        r = lax.rsqrt(jnp.mean(xv * xv, axis=-1, keepdims=True) + EPS)
        xn = xv * r
        err = xn * gv - t_ref[...]
        per_tok = jnp.mean(err * err, axis=-1, keepdims=True)
        loss_ref[...] += 0.5 * jnp.sum(per_tok, axis=0, keepdims=True)
        dy = err * (1.0 / d)
        dg_ref[...] += jnp.sum(dy * xn, axis=0, keepdims=True)
        dxn = dy * gv
        dx_ref[...] = r * (dxn - xn * jnp.mean(dxn * xn, axis=-1, keepdims=True))

    blk = pl.BlockSpec((ts, d), lambda i: (i, 0))
    row = pl.BlockSpec((1, d), lambda i: (0, 0))
    return _pcall(
        body, name="final_loss", grid=(s // ts,),
        in_specs=[blk, blk, row], out_specs=[pl.BlockSpec((1, LANES), lambda i: (0, 0)), blk, row],
        out_shape=[jax.ShapeDtypeStruct((1, LANES), F32), jax.ShapeDtypeStruct((s, d), F32),
                   jax.ShapeDtypeStruct((1, d), F32)],
        compiler_params=_params("arbitrary"),
    )(x, target, gf)


def _adamw(parts, w, m, v, name):
    npart, rows, cols = parts.shape
    tr = _tile(rows, max(8, ADAM_BLOCK_BYTES // (4 * npart * cols)))
    c1 = 1.0 - ADAM_B1 ** ADAM_STEP
    c2 = 1.0 - ADAM_B2 ** ADAM_STEP

    def body(p_ref, w_ref, m_ref, v_ref, g_ref, d_ref, mo_ref, vo_ref):
        g = p_ref[0].astype(F32)
        for i in range(1, npart):
            g = g + p_ref[i].astype(F32)
        mn = ADAM_B1 * m_ref[...] + (1.0 - ADAM_B1) * g
        vn = ADAM_B2 * v_ref[...] + (1.0 - ADAM_B2) * (g * g)
        g_ref[...] = g
        mo_ref[...] = mn
        vo_ref[...] = vn
        d_ref[...] = -ADAM_LR * ((mn / c1) / (jnp.sqrt(vn / c2) + ADAM_EPS) + ADAM_WD * w_ref[...])

    blk = pl.BlockSpec((tr, cols), lambda i: (i, 0))
    return _pcall(
        body, name=name, grid=(rows // tr,),
        in_specs=[pl.BlockSpec((npart, tr, cols), lambda i: (0, i, 0)), blk, blk, blk],
        out_specs=[blk] * 4, out_shape=[jax.ShapeDtypeStruct((rows, cols), F32)] * 4,
        compiler_params=_params("parallel"),
    )(parts, w, m, v)


def _pad_lanes(v, n=LANES, at=0):
    return jnp.pad(v, ((0, 0), (at, n - at - v.shape[1])))


def _my_cols(a, me, width):
    return lax.dynamic_slice_in_dim(a, me * width, width, axis=a.ndim - 1)


def kernel(x, c, ada_w, ada_b, norm_g, ffn_w_gate, ffn_w_up, ffn_w_down, w_in, w_out, fox_f_bias, fox_out_norm, gdn_conv, gdn_A_log, gdn_dt_bias, gdn_out_norm, final_norm, loss_target, m_ada_w, m_ada_b, m_norm_g, m_ffn_w_gate, m_ffn_w_up, m_ffn_w_down, m_w_in, m_w_out, m_fox_f_bias, m_fox_out_norm, m_gdn_conv, m_gdn_A_log, m_gdn_dt_bias, m_gdn_out_norm, m_final_norm, v_ada_w, v_ada_b, v_norm_g, v_ffn_w_gate, v_ffn_w_up, v_ffn_w_down, v_w_in, v_w_out, v_fox_f_bias, v_fox_out_norm, v_gdn_conv, v_gdn_A_log, v_gdn_dt_bias, v_gdn_out_norm, v_final_norm):
    me = _linear(_mesh_pos())
    x0 = x[0]
    s, d = x0.shape
    heads = d // (2 * HEAD_DIM)
    fw = heads * HEAD_DIM
    ng = norm_g.shape[-1]
    ncv = gdn_conv.shape[-1]
    nada = ada_w.shape[-1]
    in_w = w_in.shape[-1] * N_DEV
    in_pad = -(-in_w // 512) * 512

    pack = jnp.concatenate([c, norm_g[0].reshape(1, 3 * ng), gdn_conv[0].reshape(1, CONV_W * ncv)], axis=1)
    pack_all = _gather_row(pack, "gather_small_params")
    c_all = pack_all[:, :d]
    g_all = pack_all[:, d:d + 3 * ng].reshape(N_DEV, 3, ng).transpose(1, 0, 2).reshape(3, d)
    conv_all = pack_all[:, d + 3 * ng:].reshape(N_DEV, CONV_W, ncv).transpose(1, 0, 2).reshape(CONV_W, 3 * fw)

    mod_blk = _ada_fwd(c_all, ada_w[0], _my_cols(ada_b, me, nada))
    mod_all = _exchange([mod_blk], scatter=False, in_vmem=True, name="gather_mod")[0]
    mod = lax.dynamic_slice_in_dim(mod_all, me, 1, axis=1).reshape(N_MOD, d)
    sh1, sc1, gt1, sh2, sc2, gt2, sh3, sc3, gt3 = [mod[i:i + 1] for i in range(N_MOD)]

    wg_sh, wu_sh, wd_sh = [w[0].astype(BF16) for w in (ffn_w_gate, ffn_w_up, ffn_w_down)]
    layer = lambda w, i: w[i:i + 1]
    wg0, wu0 = _exchange([layer(wg_sh, 0), layer(wu_sh, 0)], scatter=False, in_vmem=False,
                         name="gather_ffn1_up_weights")
    small_blk = 7 * heads

    bias_lane = _pad_lanes(fox_f_bias)
    a_lane = _pad_lanes(gdn_A_log, at=heads)
    dt_lane = _pad_lanes(gdn_dt_bias, at=heads)

    h1 = _norm_mod(x0, g_all[0:1], sc1, sh1, "norm_mod_1")
    (a1, b1, s1), (wd0, wout_g) = _ffn_up(h1, wg0, wu0, 0, "ffn1_up",
                                          carry=([layer(wd_sh, 0), w_out[0].astype(BF16)], False))
    (f1, x1), (win_g,) = _ffn_down(s1, wd0, 0, x0, gt1, "ffn1_down", carry=([w_in[0].astype(BF16)], False))
    win_full = win_g.transpose(1, 0, 2).reshape(d, in_w)
    o_f, o_qkv, o_a, o_z = 3 * fw, 3 * fw + heads, 6 * fw + heads, 6 * fw + 3 * heads
    win_al = jnp.concatenate(
        [win_full[:, :o_f], win_full[:, o_qkv:o_a], win_full[:, o_z:], win_full[:, o_f:o_qkv],
         win_full[:, o_a:o_z], jnp.zeros((d, in_pad - in_w), BF16)], axis=1)
    wout_full = wout_g.reshape(d, d)

    h2 = _norm_mod(x1, g_all[1:2], sc2, sh2, "norm_mod_2")
    proj = _mm(h2, win_al, name="in_proj", tn=1536)
    cum = _fox_gate(proj, small_blk, bias_lane)
    cum_t = cum[:, :heads].T
    cum_row = cum_t[:, None, :]
    cum_col = jnp.broadcast_to(cum_t[:, :, None], (heads, s, LANES))
    (o_raw, lse, lse_row, o_fox), (wg1, wu1, wd1) = _fox_fwd(
        proj, cum_col, cum_row, fox_out_norm, heads,
        carry=([layer(wg_sh, 1), layer(wu_sh, 1), layer(wd_sh, 1)], False))
    qg, kg, vg = _gdn_pre(proj, conv_all, heads)
    o_gdn, states = _gdn_fwd(qg, kg, vg, proj, 6, small_blk, a_lane, dt_lane, gdn_out_norm)
    o_cat = jnp.concatenate([o_fox, o_gdn], axis=1)
    mix, x2 = _mm(o_cat, wout_full, name="out_proj", residual=(x1, gt2))

    h3 = _norm_mod(x2, g_all[2:3], sc3, sh3, "norm_mod_3")
    (a3, b3, s3), _ = _ffn_up(h3, wg1, wu1, 0, "ffn2_up")
    (f3, x3), _ = _ffn_down(s3, wd1, 0, x2, gt3, "ffn2_down")

    loss_row, dx3, d_final = _final(x3, loss_target[0], final_norm.reshape(1, d))
    loss = lax.psum(loss_row[0, 0], MESH_AXES)

    df3, dgt3 = _gate_bwd(dx3, f3, gt3, MACARON_W, "ffn2_gate_bwd")
    (da3, db3), _ = _ffn_bwd_act(df3, wd1, 0, a3, b3, "ffn2_bwd_act")
    (dwd2,), _ = _ffn_bwd_wd(s3, df3, "ffn2_bwd_wd")
    (dh3,), (r_wd2,) = _ffn_bwd_h(da3, db3, wg1, wu1, 0, "ffn2_bwd_h", carry=([dwd2], True))
    dwg2, dwu2 = _ffn_bwd_wgu(h3, da3, db3, "ffn2_bwd_wgu")
    dx2, dsh3, dsc3, dg3 = _norm_mod_bwd(x2, dh3, dx3, g_all[2:3], sc3, "norm_mod_3_bwd")

    dmix, dgt2 = _gate_bwd(dx2, mix, gt2, 1.0, "mix_gate_bwd")
    do_cat = _mm(dmix, wout_full, tb=True, name="out_proj_bwd_x")
    dwout = _mm(o_cat, dmix, ta=True, out_dtype=BF16, name="out_proj_bwd_w", tk=512)
    do_fox, delta, delta_row, d_foxw = _fox_prep_bwd(do_cat, o_raw, fox_out_norm, heads)
    (dq_f, dcum_q), (r_wg2, r_wu2) = _fox_dq(proj, do_fox, cum_col, cum_row, lse, delta, heads,
                                             carry=([dwg2, dwu2], True))
    (dk_f, dv_f, dcum_k), (r_wout,) = _fox_dkv(proj, do_fox, cum_col, cum_row, lse_row, delta_row, heads,
                                               carry=([dwout.reshape(N_DEV, d // N_DEV, d)], True))
    head_lanes = lambda t: jnp.pad(t[:, 0, :].T, ((0, 0), (0, LANES - heads)))
    dsm_fox, d_fbias = _fox_gate_bwd(head_lanes(dcum_q), head_lanes(dcum_k), proj, small_blk, bias_lane)
    dqg, dkg, dvg, dz, dsm_gdn, d_alog, d_dt, d_gdnw = _gdn_bwd(
        qg, kg, vg, proj, 6, small_blk, a_lane, dt_lane, gdn_out_norm, states, do_cat, 1)
    dxc, d_conv = _gdn_pre_bwd_act(proj, conv_all, dqg, dkg, dvg, heads)
    dqkv = _gdn_pre_bwd_conv(dxc, conv_all)
    dsmall = (dsm_fox + dsm_gdn).astype(BF16)
    dproj = jnp.concatenate([dq_f, dk_f, dv_f, dqkv, dz, dsmall, jnp.zeros((s, in_pad - 7 * fw - LANES), BF16)], axis=1)
    dh2 = _mm(dproj, win_al, tb=True, name="in_proj_bwd_x", tk=1536)
    dwin_al = _mm(h2, dproj, ta=True, out_dtype=BF16, name="in_proj_bwd_w", tm=2048, tn=1536, tk=1024)
    dwin_full = jnp.concatenate(
        [dwin_al[:, :o_f], dwin_al[:, 7 * fw:7 * fw + heads], dwin_al[:, o_f:o_f + 3 * fw],
         dwin_al[:, 7 * fw + heads:7 * fw + 3 * heads], dwin_al[:, 6 * fw:7 * fw]], axis=1)
    dwin_parts = dwin_full.reshape(d, N_DEV, in_w // N_DEV).transpose(1, 0, 2)
    dx1, dsh2, dsc2, dg2 = _norm_mod_bwd(x1, dh2, dx2, g_all[1:2], sc2, "norm_mod_2_bwd")

    df1, dgt1 = _gate_bwd(dx1, f1, gt1, MACARON_W, "ffn1_gate_bwd")
    (da1, db1), (r_win,) = _ffn_bwd_act(df1, wd0, 0, a1, b1, "ffn1_bwd_act", carry=([dwin_parts], True))
    dwg1, dwu1 = _ffn_bwd_wgu(h1, da1, db1, "ffn1_bwd_wgu")
    (dwd1,), (r_wg1,) = _ffn_bwd_wd(s1, df1, "ffn1_bwd_wd", carry=([dwg1], True))
    (dh1,), (r_wu1, r_wd1) = _ffn_bwd_h(da1, db1, wg0, wu0, 0, "ffn1_bwd_h", carry=([dwu1, dwd1], True))
    grad_x, dsh1, dsc1, dg1 = _norm_mod_bwd(x0, dh1, dx1, g_all[0:1], sc1, "norm_mod_1_bwd")

    dmod = jnp.concatenate([dsh1, dsc1, dgt1, dsh2, dsc2, dgt2, dsh3, dsc3, dgt3], axis=1)
    dmod_all = _gather_row(dmod, "gather_dmod")
    ct_pad = jnp.pad(c_all.T, ((0, 0), (0, LANES - N_DEV)))
    dmod_mine = jnp.pad(_my_cols(dmod_all, me, nada), ((0, LANES - N_DEV), (0, 0)))
    g_ada_w = _ada_bwd(ct_pad, dmod_mine)

    g_small_cols = [d_fbias, d_foxw, d_alog[:, heads:], d_dt[:, heads:], d_gdnw]
    small_part = jnp.concatenate(
        [_pad_lanes(v[:, :LANES]) for v in g_small_cols]
        + [d_final, dg1, dg2, dg3] + [d_conv[k:k + 1] for k in range(CONV_W)], axis=1)
    small_all = _gather_row(small_part, "gather_small_grads")
    off = 5 * LANES
    w_small = jnp.concatenate(
        [_pad_lanes(fox_f_bias), fox_out_norm, _pad_lanes(gdn_A_log), _pad_lanes(gdn_dt_bias), gdn_out_norm,
         final_norm.reshape(1, d)], axis=1)
    m_small = jnp.concatenate(
        [_pad_lanes(m_fox_f_bias), m_fox_out_norm, _pad_lanes(m_gdn_A_log), _pad_lanes(m_gdn_dt_bias),
         m_gdn_out_norm, m_final_norm.reshape(1, d)], axis=1)
    v_small = jnp.concatenate(
        [_pad_lanes(v_fox_f_bias), v_fox_out_norm, _pad_lanes(v_gdn_A_log), _pad_lanes(v_gdn_dt_bias),
         v_gdn_out_norm, v_final_norm.reshape(1, d)], axis=1)
    rep = _adamw(small_all[:, None, :off + d], w_small, m_small, v_small, "adamw_replicated")
    ab = _adamw(dmod_all[:, None, :], ada_b, m_ada_b, v_ada_b, "adamw_ada_b")
    g_ng = small_all[:, off + d:off + 4 * d].reshape(N_DEV, 3, d)
    ngs = _adamw(_my_cols(g_ng, me, ng), norm_g[0], m_norm_g[0], v_norm_g[0], "adamw_norm_g")
    g_cv = small_all[:, off + 4 * d:].reshape(N_DEV, CONV_W, 3 * fw)
    cvs = _adamw(_my_cols(g_cv, me, ncv), gdn_conv[0], m_gdn_conv[0], v_gdn_conv[0], "adamw_gdn_conv")

    def adam2(r0, r1, w, m, v, name):
        outs0 = _adamw(r0, w[0, 0], m[0, 0], v[0, 0], name + "_0")
        outs1 = _adamw(r1, w[0, 1], m[0, 1], v[0, 1], name + "_1")
        return [jnp.stack([p, q])[None] for p, q in zip(outs0, outs1)]

    wgs = adam2(r_wg1, r_wg2, ffn_w_gate, m_ffn_w_gate, v_ffn_w_gate, "adamw_w_gate")
    wus = adam2(r_wu1, r_wu2, ffn_w_up, m_ffn_w_up, v_ffn_w_up, "adamw_w_up")
    wds = adam2(r_wd1, r_wd2, ffn_w_down, m_ffn_w_down, v_ffn_w_down, "adamw_w_down")
    wis = [o[None] for o in _adamw(r_win, w_in[0], m_w_in[0], v_w_in[0], "adamw_w_in")]
    wos = [o[None] for o in _adamw(r_wout, w_out[0], m_w_out[0], v_w_out[0], "adamw_w_out")]
    adas = [o[None] for o in _adamw(g_ada_w[None], ada_w[0], m_ada_w[0], v_ada_w[0], "adamw_ada_w")]
    ngs = [o[None] for o in ngs]
    cvs = [o[None] for o in cvs]

    def rep_piece(i, lo, width):
        return rep[i][:, lo:lo + width]

    nh = fox_f_bias.shape[1]
    outs = []
    for i in range(4):
        outs.append([adas[i], ab[i], ngs[i], wgs[i], wus[i], wds[i], wis[i], wos[i],
                     rep_piece(i, 0, nh), rep_piece(i, LANES, HEAD_DIM), cvs[i], rep_piece(i, 2 * LANES, nh),
                     rep_piece(i, 3 * LANES, nh), rep_piece(i, 4 * LANES, HEAD_DIM), rep_piece(i, off, d).reshape(d)])
    return (loss, grad_x[None], *outs[0], *outs[1], *outs[2], *outs[3])
```

```python
import math

import numpy as np
import jax
import jax.numpy as jnp
from jax import lax
from jax.experimental import pallas as pl
from jax.experimental.pallas import tpu as pltpu

F32 = jnp.float32
BF16 = jnp.bfloat16

N_DEV = 8
MESH_AXES = ("x", "y", "c")
LANES = 128
HEAD_DIM = 128
GDN_CHUNK = 64
CONV_W = 4
N_MOD = 9
MACARON_W = 0.5
EPS = 1e-6
NEG = -1e30
VMEM_LIMIT_BYTES = 56 * 2 ** 20
ADAM_BLOCK_BYTES = 4 * 2 ** 20
FOX_HEADS_PER_STEP = 8

ADAM_LR = 0.001
ADAM_B1 = 0.9
ADAM_B2 = 0.999
ADAM_EPS = 1e-08
ADAM_WD = 0.01
ADAM_STEP = 10

MESH_ID = pl.DeviceIdType.MESH
ANY = pl.BlockSpec(memory_space=pl.ANY)
VMEM = pl.BlockSpec(memory_space=pltpu.VMEM)


def _pcall(body, **kw):
    return pl.pallas_call(body, **kw)


def _params(*semantics):
    return pltpu.CompilerParams(dimension_semantics=semantics, vmem_limit_bytes=VMEM_LIMIT_BYTES)


def _tile(n, pref):
    if n % pref == 0 and pref % 8 == 0:
        return pref
    t = 1 << (max(1, min(n, pref)).bit_length() - 1)
    while n % t:
        t //= 2
    return t if t % 8 == 0 else n


def _sigmoid(x):
    return 1.0 / (1.0 + jnp.exp(-x))


def _dot(a, b, dims):
    return lax.dot_general(a.astype(BF16), b.astype(BF16), (dims, ((), ())), preferred_element_type=F32)


NN = ((1,), (0,))
NT = ((1,), (1,))
TN = ((0,), (0,))


def _split3(x):
    hi = x.astype(BF16)
    r1 = x - hi.astype(F32)
    mid = r1.astype(BF16)
    lo = (r1 - mid.astype(F32)).astype(BF16)
    return hi, mid, lo


def _dot_exact_lhs(m_bf16, x, dims=NN):
    hi, mid, lo = _split3(x)
    d = lambda p: lax.dot_general(m_bf16, p, (dims, ((), ())), preferred_element_type=F32)
    return d(hi) + (d(mid) + d(lo))


def _dot_hp(a, b, dims):
    ah = a.astype(BF16)
    al = (a - ah.astype(F32)).astype(BF16)
    bh = b.astype(BF16)
    bl = (b - bh.astype(F32)).astype(BF16)
    d = lambda p, q: lax.dot_general(p, q, (dims, ((), ())), preferred_element_type=F32)
    return d(ah, bh) + (d(ah, bl) + d(al, bh))


def _mesh_pos():
    return lax.axis_index("x"), lax.axis_index("y"), lax.axis_index("c")


def _peer(pos, mask):
    x, y, c = pos
    return (1 - x if mask & 4 else x, 1 - y if mask & 2 else y, 1 - c if mask & 1 else c)


def _linear(pos):
    return 4 * pos[0] + 2 * pos[1] + pos[2]


def _exchange_copies(ins, outs, sems, scatter, with_receives=True):
    send_sems, recv_sems, local_sems = sems
    pos = _mesh_pos()
    me = _linear(pos)
    local, sends, recvs = [], [], []
    for i in range(len(ins)):
        src = ins[i].at[me] if scatter else ins[i]
        local.append(pltpu.make_async_copy(src, outs[i].at[me], local_sems.at[i]))
    for mask in range(1, N_DEV):
        peer = _peer(pos, mask)
        for i in range(len(ins)):
            sem = dict(send_sem=send_sems.at[i, mask - 1], recv_sem=recv_sems.at[i, mask - 1],
                       device_id=peer, device_id_type=MESH_ID)
            sends.append(pltpu.make_async_remote_copy(
                src_ref=ins[i].at[_linear(peer)] if scatter else ins[i], dst_ref=outs[i].at[me], **sem))
            if with_receives:
                recvs.append(pltpu.make_async_remote_copy(
                    src_ref=ins[i].at[me] if scatter else ins[i], dst_ref=outs[i].at[_linear(peer)], **sem))
    return local, sends, recvs


def _exchange_start(ins, outs, sems, scatter):
    local, sends, _ = _exchange_copies(ins, outs, sems, scatter, with_receives=False)
    for cp in local + sends:
        cp.start()


def _exchange_wait(ins, outs, sems, scatter):
    local, sends, recvs = _exchange_copies(ins, outs, sems, scatter)
    for cp in recvs:
        cp.wait_recv()
    for cp in sends:
        cp.wait_send()
    for cp in local:
        cp.wait()


def _exchange_sems(n):
    return [pltpu.SemaphoreType.DMA((n, N_DEV - 1)), pltpu.SemaphoreType.DMA((n, N_DEV - 1)),
            pltpu.SemaphoreType.DMA((n,))]


def _exchange_shapes(arrays, scatter):
    return [jax.ShapeDtypeStruct(a.shape if scatter else (N_DEV,) + a.shape, a.dtype) for a in arrays]


def _exchange(arrays, *, scatter, in_vmem, name):
    n = len(arrays)

    def body(*refs):
        ins, outs, sems = refs[:n], refs[n:2 * n], refs[2 * n:]
        _exchange_start(ins, outs, sems, scatter)
        _exchange_wait(ins, outs, sems, scatter)

    spec = VMEM if in_vmem else ANY
    outs = _pcall(
        body, name=name, out_shape=_exchange_shapes(arrays, scatter),
        in_specs=[spec] * n, out_specs=[spec] * n, scratch_shapes=_exchange_sems(n),
    )(*arrays)
    return list(outs)


def _gather_via_sibling(arrays, name):
    n = len(arrays)

    def body(*refs):
        ins, outs = refs[:n], refs[n:2 * n]
        send_sems, recv_sems, local_sems = refs[2 * n:]
        x, y, c = _mesh_pos()
        me, sibling = (x, y, c), (x, y, 1 - c)
        chips = [(1 - x, y), (x, 1 - y), (1 - x, 1 - y)]

        def copy(i, k, block, to, from_input=False):
            return pltpu.make_async_remote_copy(
                src_ref=ins[i] if from_input else outs[i].at[_linear(block)], dst_ref=outs[i].at[_linear(block)],
                send_sem=send_sems.at[i, k], recv_sem=recv_sems.at[i, k], device_id=to, device_id_type=MESH_ID)

        mine = [pltpu.make_async_copy(ins[i], outs[i].at[_linear(me)], local_sems.at[i]) for i in range(n)]
        first = []
        for i in range(n):
            first.append(copy(i, 0, me, sibling, True))
            first += [copy(i, 1 + j, me, (*chip, c), True) for j, chip in enumerate(chips)]
        for cp in mine + first:
            cp.start()
        passed = []
        for j, chip in enumerate(chips):
            for i in range(n):
                copy(i, 1 + j, (*chip, c), me).wait_recv()
                cp = copy(i, 4 + j, (*chip, c), sibling)
                cp.start()
                passed.append(cp)
        for i in range(n):
            copy(i, 0, sibling, me).wait_recv()
            for j, chip in enumerate(chips):
                copy(i, 4 + j, (*chip, 1 - c), me).wait_recv()
        for cp in first + passed:
            cp.wait_send()
        for cp in mine:
            cp.wait()

    outs = _pcall(
        body, name=name, out_shape=_exchange_shapes(arrays, False), in_specs=[ANY] * n, out_specs=[ANY] * n,
        scratch_shapes=_exchange_sems(n),
    )(*arrays)
    return list(outs)


def _hosted(body, *, name, grid, in_specs, out_specs, out_shape, args, scratch_shapes=(), prefetch=(), carry=None):
    n_in, n_out, n_scr, n_pre = len(in_specs), len(out_shape), len(scratch_shapes), len(prefetch)
    arrays, scatter = carry if carry is not None else ([], False)
    n = len(arrays)

    def wrapped(*refs):
        pre, r = refs[:n_pre], refs[n_pre:]
        host_in, comm_in = r[:n_in], r[n_in:n_in + n]
        r = r[n_in + n:]
        host_out, comm_out = r[:n_out], r[n_out:n_out + n]
        r = r[n_out + n:]
        host_scr, sems = r[:n_scr], r[n_scr:]
        if n:
            first = pl.program_id(0) == 0
            last = pl.program_id(0) == grid[0] - 1
            for ax in range(1, len(grid)):
                first = first & (pl.program_id(ax) == 0)
                last = last & (pl.program_id(ax) == grid[ax] - 1)

            @pl.when(first)
            def _():
                _exchange_start(comm_in, comm_out, sems, scatter)

        body(*pre, *host_in, *host_out, *host_scr)
        if n:
            @pl.when(last)
            def _():
                _exchange_wait(comm_in, comm_out, sems, scatter)

    grid_spec = pltpu.PrefetchScalarGridSpec(
        num_scalar_prefetch=n_pre, grid=grid, in_specs=list(in_specs) + [ANY] * n,
        out_specs=list(out_specs) + [ANY] * n,
        scratch_shapes=list(scratch_shapes) + (_exchange_sems(n) if n else []))
    outs = _pcall(
        wrapped, name=name, grid_spec=grid_spec, out_shape=list(out_shape) + _exchange_shapes(arrays, scatter),
        compiler_params=_params(*(["arbitrary"] * len(grid))),
    )(*prefetch, *args, *arrays)
    return list(outs[:n_out]), list(outs[n_out:])


def _gather_row(v, name):
    return _exchange([v], scatter=False, in_vmem=True, name=name)[0].reshape(N_DEV, v.shape[1])


def _ada_fwd(c_all, w, b):
    d, n = w.shape
    tn = _tile(n, 256)

    def body(c_ref, w_ref, b_ref, o_ref):
        cv = c_ref[...]
        cond = cv * _sigmoid(cv)
        o_ref[...] = _dot_hp(cond, w_ref[...], NN) + b_ref[...]

    return _pcall(
        body, name="ada_fwd", grid=(n // tn,),
        in_specs=[pl.BlockSpec((N_DEV, d), lambda j: (0, 0)), pl.BlockSpec((d, tn), lambda j: (0, j)),
                  pl.BlockSpec((1, tn), lambda j: (0, j))],
        out_specs=pl.BlockSpec((N_DEV, tn), lambda j: (0, j)),
        out_shape=jax.ShapeDtypeStruct((N_DEV, n), F32), compiler_params=_params("parallel"),
    )(c_all, w, b)


def _ada_bwd(ct_pad, dmod_pad):
    d = ct_pad.shape[0]
    n = dmod_pad.shape[1]
    tn = _tile(n, 256)

    def body(c_ref, g_ref, o_ref):
        cv = c_ref[...]
        cond = cv * _sigmoid(cv)
        o_ref[...] = _dot_hp(cond, g_ref[...], NN)

    return _pcall(
        body, name="ada_bwd", grid=(n // tn,),
        in_specs=[pl.BlockSpec((d, LANES), lambda j: (0, 0)), pl.BlockSpec((LANES, tn), lambda j: (0, j))],
        out_specs=pl.BlockSpec((d, tn), lambda j: (0, j)),
        out_shape=jax.ShapeDtypeStruct((d, n), F32), compiler_params=_params("parallel"),
    )(ct_pad, dmod_pad)


def _norm_mod(x, g, sc, sh, name):
    s, d = x.shape
    ts = _tile(s, 512)

    def body(x_ref, g_ref, sc_ref, sh_ref, h_ref):
        xv = x_ref[...]
        r = lax.rsqrt(jnp.mean(xv * xv, axis=-1, keepdims=True) + EPS)
        h_ref[...] = (xv * r * g_ref[...] * (1.0 + sc_ref[...]) + sh_ref[...]).astype(BF16)

    row = pl.BlockSpec((1, d), lambda i: (0, 0))
    return _pcall(
        body, name=name, grid=(s // ts,),
        in_specs=[pl.BlockSpec((ts, d), lambda i: (i, 0)), row, row, row],
        out_specs=pl.BlockSpec((ts, d), lambda i: (i, 0)),
        out_shape=jax.ShapeDtypeStruct((s, d), BF16), compiler_params=_params("parallel"),
    )(x, g, sc, sh)


def _norm_mod_bwd(x, dh, dx_out, g, sc, name):
    s, d = x.shape
    ts = _tile(s, 512)

    def body(x_ref, dh_ref, dxo_ref, g_ref, sc_ref, dx_ref, dsh_ref, dsc_ref, dg_ref):
        @pl.when(pl.program_id(0) == 0)
        def _():
            dsh_ref[...] = jnp.zeros_like(dsh_ref)
            dsc_ref[...] = jnp.zeros_like(dsc_ref)
            dg_ref[...] = jnp.zeros_like(dg_ref)

        xv = x_ref[...]
        dh_v = dh_ref[...]
        gv = g_ref[...]
        one_sc = 1.0 + sc_ref[...]
        r = lax.rsqrt(jnp.mean(xv * xv, axis=-1, keepdims=True) + EPS)
        xn = xv * r
        dxn = dh_v * (gv * one_sc)
        dx_ref[...] = dxo_ref[...] + r * (dxn - xn * jnp.mean(dxn * xn, axis=-1, keepdims=True))
        t = dh_v * xn
        dsh_ref[...] += jnp.sum(dh_v, axis=0, keepdims=True)
        dsc_ref[...] += jnp.sum(t * gv, axis=0, keepdims=True)
        dg_ref[...] += jnp.sum(t * one_sc, axis=0, keepdims=True)

    blk = pl.BlockSpec((ts, d), lambda i: (i, 0))
    row = pl.BlockSpec((1, d), lambda i: (0, 0))
    return _pcall(
        body, name=name, grid=(s // ts,),
        in_specs=[blk, blk, blk, row, row], out_specs=[blk, row, row, row],
        out_shape=[jax.ShapeDtypeStruct((s, d), F32)] + [jax.ShapeDtypeStruct((1, d), F32)] * 3,
        compiler_params=_params("arbitrary"),
    )(x, dh, dx_out, g, sc)


def _gate_bwd(dx, f, gt, k, name):
    s, d = dx.shape
    ts = _tile(s, 512)

    def body(dx_ref, f_ref, gt_ref, df_ref, dgt_ref):
        @pl.when(pl.program_id(0) == 0)
        def _():
            dgt_ref[...] = jnp.zeros_like(dgt_ref)

        dxv = dx_ref[...]
        df_ref[...] = ((k * gt_ref[...]) * dxv).astype(BF16)
        dgt_ref[...] += k * jnp.sum(f_ref[...] * dxv, axis=0, keepdims=True)

    blk = pl.BlockSpec((ts, d), lambda i: (i, 0))
    row = pl.BlockSpec((1, d), lambda i: (0, 0))
    return _pcall(
        body, name=name, grid=(s // ts,),
        in_specs=[blk, blk, row], out_specs=[blk, row],
        out_shape=[jax.ShapeDtypeStruct((s, d), BF16), jax.ShapeDtypeStruct((1, d), F32)],
        compiler_params=_params("arbitrary"),
    )(dx, f, gt)


def _ffn_up(h, wg, wu, layer, name, carry=None):
    s, d = h.shape
    fs = wg.shape[-1]
    tm = _tile(s, 1024)

    def body(h_ref, wg_ref, wu_ref, a_ref, b_ref, s_ref):
        hv = h_ref[...]
        a = jnp.dot(hv, wg_ref[...], preferred_element_type=F32)
        b = jnp.dot(hv, wu_ref[...], preferred_element_type=F32)
        a_ref[...] = a.astype(BF16)
        b_ref[...] = b.astype(BF16)
        s_ref[...] = (a * _sigmoid(a) * b).astype(BF16)

    wspec = pl.BlockSpec((None, None, d, fs), lambda j, m: (j, layer, 0, 0))
    ospec = pl.BlockSpec((None, tm, fs), lambda j, m: (j, m, 0))
    return _hosted(
        body, name=name, grid=(N_DEV, s // tm),
        in_specs=[pl.BlockSpec((tm, d), lambda j, m: (m, 0)), wspec, wspec],
        out_specs=[ospec, ospec, ospec],
        out_shape=[jax.ShapeDtypeStruct((N_DEV, s, fs), BF16)] * 3,
        args=(h, wg, wu), carry=carry)


def _ffn_down(sv, wd, layer, x_in, gt, name, carry=None):
    _, s, fs = sv.shape
    d = wd.shape[-1]
    tm = _tile(s, 512)

    def body(s_ref, wd_ref, x_ref, gt_ref, f_ref, xo_ref, acc):
        j = pl.program_id(1)

        @pl.when(j == 0)
        def _():
            acc[...] = jnp.zeros_like(acc)

        acc[...] += jnp.dot(s_ref[...], wd_ref[...], preferred_element_type=F32)

        @pl.when(j == N_DEV - 1)
        def _():
            fv = acc[...]
            f_ref[...] = fv
            xo_ref[...] = x_ref[...] + (MACARON_W * gt_ref[...]) * fv

    blk = pl.BlockSpec((tm, d), lambda m, j: (m, 0))
    return _hosted(
        body, name=name, grid=(s // tm, N_DEV),
        in_specs=[pl.BlockSpec((None, tm, fs), lambda m, j: (j, m, 0)),
                  pl.BlockSpec((None, None, fs, d), lambda m, j: (j, layer, 0, 0)),
                  blk, pl.BlockSpec((1, d), lambda m, j: (0, 0))],
        out_specs=[blk, blk],
        out_shape=[jax.ShapeDtypeStruct((s, d), F32)] * 2,
        scratch_shapes=[pltpu.VMEM((tm, d), F32)],
        args=(sv, wd, x_in, gt), carry=carry)


def _ffn_bwd_act(df, wd, layer, a, b, name, carry=None):
    s, d = df.shape
    fs = a.shape[-1]
    tm = _tile(s, 1024)

    def body(df_ref, wd_ref, a_ref, b_ref, da_ref, db_ref):
        ds = lax.dot_general(df_ref[...], wd_ref[...], (NT, ((), ())), preferred_element_type=F32)
        av = a_ref[...].astype(F32)
        sg = _sigmoid(av)
        da_ref[...] = (ds * b_ref[...].astype(F32) * (sg * (1.0 + av * (1.0 - sg)))).astype(BF16)
        db_ref[...] = (ds * (av * sg)).astype(BF16)

    hid = pl.BlockSpec((None, tm, fs), lambda j, m: (j, m, 0))
    return _hosted(
        body, name=name, grid=(N_DEV, s // tm),
        in_specs=[pl.BlockSpec((tm, d), lambda j, m: (m, 0)),
                  pl.BlockSpec((None, None, fs, d), lambda j, m: (j, layer, 0, 0)), hid, hid],
        out_specs=[hid, hid],
        out_shape=[jax.ShapeDtypeStruct((N_DEV, s, fs), BF16)] * 2,
        args=(df, wd, a, b), carry=carry)


def _ffn_bwd_wd(sv, df, name, carry=None):
    _, s, fs = sv.shape
    d = df.shape[1]
    tk = _tile(s, 512)
    nk = s // tk

    def body(s_ref, df_ref, o_ref, acc):
        @pl.when(pl.program_id(1) == 0)
        def _():
            acc[...] = jnp.zeros_like(acc)

        acc[...] += lax.dot_general(s_ref[...], df_ref[...], (TN, ((), ())), preferred_element_type=F32)

        @pl.when(pl.program_id(1) == nk - 1)
        def _():
            o_ref[...] = acc[...].astype(BF16)

    return _hosted(
        body, name=name, grid=(N_DEV, nk),
        in_specs=[pl.BlockSpec((None, tk, fs), lambda j, k: (j, k, 0)), pl.BlockSpec((tk, d), lambda j, k: (k, 0))],
        out_specs=[pl.BlockSpec((None, fs, d), lambda j, k: (j, 0, 0))],
        out_shape=[jax.ShapeDtypeStruct((N_DEV, fs, d), BF16)],
        scratch_shapes=[pltpu.VMEM((fs, d), F32)],
        args=(sv, df), carry=carry)


def _ffn_bwd_h(da, db, wg, wu, layer, name, carry=None):
    _, s, fs = da.shape
    d = wg.shape[-2]
    tm = _tile(s, 1024)

    def body(da_ref, db_ref, wg_ref, wu_ref, o_ref, acc):
        j = pl.program_id(1)

        @pl.when(j == 0)
        def _():
            acc[...] = jnp.zeros_like(acc)

        acc[...] += (lax.dot_general(da_ref[...], wg_ref[...], (NT, ((), ())), preferred_element_type=F32)
                     + lax.dot_general(db_ref[...], wu_ref[...], (NT, ((), ())), preferred_element_type=F32))

        @pl.when(j == N_DEV - 1)
        def _():
            o_ref[...] = acc[...]

    hid = pl.BlockSpec((None, tm, fs), lambda m, j: (j, m, 0))
    wspec = pl.BlockSpec((None, None, d, fs), lambda m, j: (j, layer, 0, 0))
    return _hosted(
        body, name=name, grid=(s // tm, N_DEV),
        in_specs=[hid, hid, wspec, wspec],
        out_specs=[pl.BlockSpec((tm, d), lambda m, j: (m, 0))],
        out_shape=[jax.ShapeDtypeStruct((s, d), F32)],
        scratch_shapes=[pltpu.VMEM((tm, d), F32)],
        args=(da, db, wg, wu), carry=carry)


def _ffn_bwd_wgu(h, da, db, name):
    s, d = h.shape
    fs = da.shape[-1]
    tk = _tile(s, 512)
    nk = s // tk

    def body(h_ref, da_ref, db_ref, og_ref, ou_ref, accg, accu):
        @pl.when(pl.program_id(1) == 0)
        def _():
            accg[...] = jnp.zeros_like(accg)
            accu[...] = jnp.zeros_like(accu)

        hv = h_ref[...]
        accg[...] += lax.dot_general(hv, da_ref[...], (TN, ((), ())), preferred_element_type=F32)
        accu[...] += lax.dot_general(hv, db_ref[...], (TN, ((), ())), preferred_element_type=F32)

        @pl.when(pl.program_id(1) == nk - 1)
        def _():
            og_ref[...] = accg[...].astype(BF16)
            ou_ref[...] = accu[...].astype(BF16)

    hid = pl.BlockSpec((None, tk, fs), lambda j, k: (j, k, 0))
    ospec = pl.BlockSpec((None, d, fs), lambda j, k: (j, 0, 0))
    return _pcall(
        body, name=name, grid=(N_DEV, nk),
        in_specs=[pl.BlockSpec((tk, d), lambda j, k: (k, 0)), hid, hid],
        out_specs=[ospec, ospec],
        out_shape=[jax.ShapeDtypeStruct((N_DEV, d, fs), BF16)] * 2,
        scratch_shapes=[pltpu.VMEM((d, fs), F32), pltpu.VMEM((d, fs), F32)],
        compiler_params=_params("parallel", "arbitrary"),
    )(h, da, db)


def _mm(a, b, *, ta=False, tb=False, out_dtype=F32, name, tm=1024, tn=1024, tk=2048, residual=None):
    m, kdim = (a.shape[1], a.shape[0]) if ta else a.shape
    n = b.shape[0] if tb else b.shape[1]
    tm, tn, tk = _tile(m, tm), _tile(n, tn), _tile(kdim, tk)
    nk = kdim // tk
    dims = ((0,) if ta else (1,), (1,) if tb else (0,))

    def body(*refs):
        a_ref, b_ref = refs[:2]
        acc = refs[-1]
        kk = pl.program_id(2)

        @pl.when(kk == 0)
        def _():
            acc[...] = jnp.zeros_like(acc)

        acc[...] += lax.dot_general(a_ref[...].astype(BF16), b_ref[...].astype(BF16), (dims, ((), ())),
                                    preferred_element_type=F32)

        @pl.when(kk == nk - 1)
        def _():
            if residual is None:
                refs[2][...] = acc[...].astype(out_dtype)
            else:
                res_ref, gate_ref, y_ref, xo_ref = refs[2:6]
                yv = acc[...]
                y_ref[...] = yv
                xo_ref[...] = res_ref[...] + gate_ref[...] * yv

    a_spec = pl.BlockSpec((tk, tm), lambda i, j, k: (k, i)) if ta else pl.BlockSpec((tm, tk), lambda i, j, k: (i, k))
    b_spec = pl.BlockSpec((tn, tk), lambda i, j, k: (j, k)) if tb else pl.BlockSpec((tk, tn), lambda i, j, k: (k, j))
    o_spec = pl.BlockSpec((tm, tn), lambda i, j, k: (i, j))
    if residual is None:
        in_specs, out_specs = [a_spec, b_spec], o_spec
        out_shape = jax.ShapeDtypeStruct((m, n), out_dtype)
        args = (a, b)
    else:
        in_specs = [a_spec, b_spec, o_spec, pl.BlockSpec((1, tn), lambda i, j, k: (0, j))]
        out_specs = [o_spec, o_spec]
        out_shape = [jax.ShapeDtypeStruct((m, n), F32)] * 2
        args = (a, b) + tuple(residual)
    return _pcall(
        body, name=name, grid=(m // tm, n // tn, nk), in_specs=in_specs, out_specs=out_specs, out_shape=out_shape,
        scratch_shapes=[pltpu.VMEM((tm, tn), F32)],
        compiler_params=_params("parallel", "parallel", "arbitrary"),
    )(*args)


def _log_sigmoid(z):
    return jnp.minimum(z, 0.0) - jnp.log(1.0 + jnp.exp(-jnp.abs(z)))


def _fox_gate(proj, small_blk, bias_lane):
    s = proj.shape[0]
    ts = _tile(s, 1024)
    nsub = ts // LANES

    def body(z_ref, b_ref, cum_ref, carry):
        @pl.when(pl.program_id(0) == 0)
        def _():
            carry[...] = jnp.zeros_like(carry)

        ii = lax.broadcasted_iota(jnp.int32, (LANES, LANES), 0)
        jj = lax.broadcasted_iota(jnp.int32, (LANES, LANES), 1)
        tri = (ii >= jj).astype(BF16)
        logf = _log_sigmoid(z_ref[...] + b_ref[...])
        cv = carry[...]
        for sb in range(nsub):
            blk = logf[sb * LANES:(sb + 1) * LANES, :]
            cum_ref[sb * LANES:(sb + 1) * LANES, :] = _dot_exact_lhs(tri, blk) + cv
            cv = cv + jnp.sum(blk, axis=0, keepdims=True)
        carry[...] = cv

    return _pcall(
        body, name="fox_gate", grid=(s // ts,),
        in_specs=[pl.BlockSpec((ts, LANES), lambda i: (i, small_blk)), pl.BlockSpec((1, LANES), lambda i: (0, 0))],
        out_specs=pl.BlockSpec((ts, LANES), lambda i: (i, 0)),
        out_shape=jax.ShapeDtypeStruct((s, LANES), F32),
        scratch_shapes=[pltpu.VMEM((1, LANES), F32)],
        compiler_params=_params("arbitrary"),
    )(proj, bias_lane)


def _fox_gate_bwd(dcum_q, dcum_k, proj, small_blk, bias_lane):
    s = proj.shape[0]
    ts = _tile(s, 1024)
    nsub = ts // LANES
    nb = s // ts

    def body(dcq_ref, dc_ref, z_ref, b_ref, dz_ref, db_ref, carry):
        @pl.when(pl.program_id(0) == 0)
        def _():
            carry[...] = jnp.zeros_like(carry)
            db_ref[...] = jnp.zeros_like(db_ref)

        ii = lax.broadcasted_iota(jnp.int32, (LANES, LANES), 0)
        jj = lax.broadcasted_iota(jnp.int32, (LANES, LANES), 1)
        triu = (jj >= ii).astype(BF16)
        dc = dcq_ref[...] + dc_ref[...]
        zb = z_ref[...] + b_ref[...]
        cv = carry[...]
        dbv = jnp.zeros((1, LANES), F32)
        for sb in reversed(range(nsub)):
            rows = slice(sb * LANES, (sb + 1) * LANES)
            blk = dc[rows, :]
            dlogf = _dot_exact_lhs(triu, blk) + cv
            cv = cv + jnp.sum(blk, axis=0, keepdims=True)
            dz = dlogf * _sigmoid(-zb[rows, :])
            dz_ref[rows, :] = dz
            dbv = dbv + jnp.sum(dz, axis=0, keepdims=True)
        carry[...] = cv
        db_ref[...] += dbv

    row = pl.BlockSpec((1, LANES), lambda i: (0, 0))
    return _pcall(
        body, name="fox_gate_bwd", grid=(nb,),
        in_specs=[pl.BlockSpec((ts, LANES), lambda i: (nb - 1 - i, 0)),
                  pl.BlockSpec((ts, LANES), lambda i: (nb - 1 - i, 0)),
                  pl.BlockSpec((ts, LANES), lambda i: (nb - 1 - i, small_blk)), row],
        out_specs=[pl.BlockSpec((ts, LANES), lambda i: (nb - 1 - i, 0)), row],
        out_shape=[jax.ShapeDtypeStruct((s, LANES), F32), jax.ShapeDtypeStruct((1, LANES), F32)],
        scratch_shapes=[pltpu.VMEM((1, LANES), F32)],
        compiler_params=_params("arbitrary"),
    )(dcum_q, dcum_k, proj, bias_lane)


def _tri_tables(n, by_key):
    if by_key:
        pairs = [(i, j) for j in range(n) for i in range(j, n)]
    else:
        pairs = [(i, j) for i in range(n) for j in range(i + 1)]
    return (jnp.asarray(np.array([p[0] for p in pairs], np.int32)),
            jnp.asarray(np.array([p[1] for p in pairs], np.int32)))


def _fox_group(heads):
    return FOX_HEADS_PER_STEP if heads % FOX_HEADS_PER_STEP == 0 else 1


def _as_row(col):
    t = col.shape[0]
    eye = lax.broadcasted_iota(jnp.int32, (t, t), 0) == lax.broadcasted_iota(jnp.int32, (t, t), 1)
    return jnp.sum(jnp.where(eye, col, 0.0), axis=0, keepdims=True)


def _fox_scores(a, b, bias_col, bias_row, scale, diagonal, rows_are_keys=False):
    sc = lax.dot_general(a.astype(BF16), b.astype(BF16), (NT, ((), ())), preferred_element_type=F32) * scale
    sc = sc + (bias_col + bias_row)
    if not diagonal:
        return sc
    row = lax.broadcasted_iota(jnp.int32, sc.shape, 0)
    col = lax.broadcasted_iota(jnp.int32, sc.shape, 1)
    return jnp.where(row <= col if rows_are_keys else col <= row, sc, NEG)


def _fox_fwd(proj, cum_col, cum_row, w_norm, heads, carry=None):
    s = proj.shape[0]
    t = _tile(s, 512)
    grp = _fox_group(heads)
    qi, ki = _tri_tables(s // t, False)
    scale = 1.0 / math.sqrt(HEAD_DIM)

    def body(qi_ref, ki_ref, q_ref, k_ref, v_ref, cq_ref, ck_ref, w_ref, o_ref, lse_ref, lser_ref, on_ref, m_s, acc_s):
        iq, ik = qi_ref[pl.program_id(1)], ki_ref[pl.program_id(1)]

        @pl.when(ik == 0)
        def _():
            m_s[...] = jnp.full_like(m_s, NEG)
            acc_s[...] = jnp.zeros_like(acc_s)

        def step(diagonal):
            for g in range(grp):
                sl = slice(g * HEAD_DIM, (g + 1) * HEAD_DIM)
                sc = _fox_scores(q_ref[:, sl], k_ref[:, sl], cq_ref[g, :, 0:1], -ck_ref[g], scale, diagonal)
                m_prev = m_s[g]
                m_new = jnp.maximum(m_prev, jnp.max(sc, axis=1, keepdims=True))
                p = jnp.exp(sc - m_new).astype(BF16)
                v_ones = jnp.concatenate([v_ref[:, sl].astype(BF16), jnp.ones((t, LANES), BF16)], axis=1)
                acc_s[g] = jnp.exp(m_prev - m_new) * acc_s[g] + jnp.dot(p, v_ones, preferred_element_type=F32)
                m_s[g] = m_new

        @pl.when(ik < iq)
        def _():
            step(False)

        @pl.when(ik == iq)
        def _():
            step(True)
            for g in range(grp):
                sl = slice(g * HEAD_DIM, (g + 1) * HEAD_DIM)
                acc = acc_s[g]
                o = acc[:, :HEAD_DIM] / acc[:, HEAD_DIM:]
                lse = m_s[g] + jnp.log(acc[:, HEAD_DIM:])
                o_ref[:, sl] = o
                lse_ref[g] = lse
                lser_ref[g] = _as_row(lse[:, 0:1])
                r = lax.rsqrt(jnp.mean(o * o, axis=1, keepdims=True) + EPS)
                on_ref[:, sl] = (o * r * w_ref[...]).astype(BF16)

    ng = heads // grp
    qblk = pl.BlockSpec((t, grp * HEAD_DIM), lambda h, p, qi, ki: (qi[p], h))
    kblk = lambda off: pl.BlockSpec((t, grp * HEAD_DIM), lambda h, p, qi, ki: (ki[p], off + h))
    qcol = pl.BlockSpec((grp, t, LANES), lambda h, p, qi, ki: (h, qi[p], 0))
    return _hosted(
        body, name="fox_fwd", grid=(ng, int(qi.shape[0])), prefetch=(qi, ki),
        in_specs=[qblk, kblk(ng), kblk(2 * ng), qcol,
                  pl.BlockSpec((grp, 1, t), lambda h, p, qi, ki: (h, 0, ki[p])),
                  pl.BlockSpec((1, HEAD_DIM), lambda h, p, qi, ki: (0, 0))],
        out_specs=[qblk, qcol, pl.BlockSpec((grp, 1, t), lambda h, p, qi, ki: (h, 0, qi[p])), qblk],
        scratch_shapes=[pltpu.VMEM((grp, t, 1), F32), pltpu.VMEM((grp, t, 2 * HEAD_DIM), F32)],
        out_shape=[jax.ShapeDtypeStruct((s, heads * HEAD_DIM), F32), jax.ShapeDtypeStruct((heads, s, LANES), F32),
                   jax.ShapeDtypeStruct((heads, 1, s), F32), jax.ShapeDtypeStruct((s, heads * HEAD_DIM), BF16)],
        args=(proj, proj, proj, cum_col, cum_row, w_norm), carry=carry)


def _fox_prep_bwd(do_cat, o_raw, w_norm, heads):
    s = o_raw.shape[0]
    ts = _tile(s, 512)

    def body(g_ref, o_ref, w_ref, do_ref, delta_ref, deltar_ref, dw_ref):
        @pl.when((pl.program_id(0) == 0) & (pl.program_id(1) == 0))
        def _():
            dw_ref[...] = jnp.zeros_like(dw_ref)

        o = o_ref[...]
        g = g_ref[...]
        r = lax.rsqrt(jnp.mean(o * o, axis=1, keepdims=True) + EPS)
        wg = g * w_ref[...]
        do = r * wg - o * (r * r * r) * jnp.mean(wg * o, axis=1, keepdims=True)
        do_ref[...] = do.astype(BF16)
        delta = jnp.sum(do * o, axis=1, keepdims=True)
        delta_ref[...] = jnp.broadcast_to(delta, delta_ref.shape)
        deltar_ref[...] = _as_row(delta)
        dw_ref[...] += jnp.sum(g * o * r, axis=0, keepdims=True)

    blk = pl.BlockSpec((ts, HEAD_DIM), lambda h, i: (i, h))
    row = pl.BlockSpec((1, HEAD_DIM), lambda h, i: (0, 0))
    return _pcall(
        body, name="fox_prep_bwd", grid=(heads, s // ts),
        in_specs=[blk, blk, row],
        out_specs=[blk, pl.BlockSpec((None, ts, LANES), lambda h, i: (h, i, 0)),
                   pl.BlockSpec((None, 1, ts), lambda h, i: (h, 0, i)), row],
        out_shape=[jax.ShapeDtypeStruct((s, heads * HEAD_DIM), BF16), jax.ShapeDtypeStruct((heads, s, LANES), F32),
                   jax.ShapeDtypeStruct((heads, 1, s), F32), jax.ShapeDtypeStruct((1, HEAD_DIM), F32)],
        compiler_params=_params("arbitrary", "arbitrary"),
    )(do_cat, o_raw, w_norm)


def _fox_dq(proj, do, cum_col, cum_row, lse, delta, heads, carry=None):
    s = proj.shape[0]
    t = _tile(s, 512)
    grp = _fox_group(heads)
    qi, ki = _tri_tables(s // t, False)
    scale = 1.0 / math.sqrt(HEAD_DIM)

    def body(qi_ref, ki_ref, q_ref, k_ref, v_ref, do_ref, cq_ref, ck_ref, lse_ref, dl_ref, dq_ref, dc_ref, acc, dc_acc):
        iq, ik = qi_ref[pl.program_id(1)], ki_ref[pl.program_id(1)]

        @pl.when(ik == 0)
        def _():
            acc[...] = jnp.zeros_like(acc)
            dc_acc[...] = jnp.zeros_like(dc_acc)

        def step(diagonal):
            for g in range(grp):
                sl = slice(g * HEAD_DIM, (g + 1) * HEAD_DIM)
                kv = k_ref[:, sl]
                sc = _fox_scores(q_ref[:, sl], kv, cq_ref[g, :, 0:1] - lse_ref[g, :, 0:1], -ck_ref[g], scale, diagonal)
                p = jnp.exp(sc)
                dp = _dot(do_ref[:, sl], v_ref[:, sl], NT)
                ds = p * (dp - dl_ref[g, :, 0:1])
                acc[g] += _dot(ds, kv, NN)
                dc_acc[g] += jnp.sum(ds, axis=1, keepdims=True)

        @pl.when(ik < iq)
        def _():
            step(False)

        @pl.when(ik == iq)
        def _():
            step(True)
            for g in range(grp):
                dq_ref[:, g * HEAD_DIM:(g + 1) * HEAD_DIM] = (acc[g] * scale).astype(BF16)
                dc_ref[g] = _as_row(dc_acc[g])

    ng = heads // grp
    qblk = pl.BlockSpec((t, grp * HEAD_DIM), lambda h, p, qi, ki: (qi[p], h))
    kblk = lambda off: pl.BlockSpec((t, grp * HEAD_DIM), lambda h, p, qi, ki: (ki[p], off + h))
    qcol = pl.BlockSpec((grp, t, LANES), lambda h, p, qi, ki: (h, qi[p], 0))
    return _hosted(
        body, name="fox_dq", grid=(ng, int(qi.shape[0])), prefetch=(qi, ki),
        in_specs=[qblk, kblk(ng), kblk(2 * ng), qblk, qcol,
                  pl.BlockSpec((grp, 1, t), lambda h, p, qi, ki: (h, 0, ki[p])), qcol, qcol],
        out_specs=[qblk, pl.BlockSpec((grp, 1, t), lambda h, p, qi, ki: (h, 0, qi[p]))],
        scratch_shapes=[pltpu.VMEM((grp, t, HEAD_DIM), F32), pltpu.VMEM((grp, t, 1), F32)],
        out_shape=[jax.ShapeDtypeStruct((s, heads * HEAD_DIM), BF16), jax.ShapeDtypeStruct((heads, 1, s), F32)],
        args=(proj, proj, proj, do, cum_col, cum_row, lse, delta), carry=carry)


def _fox_dkv(proj, do, cum_col, cum_row, lse_row, delta_row, heads, carry=None):
    s = proj.shape[0]
    t = _tile(s, 512)
    nk = s // t
    grp = _fox_group(heads)
    qi, ki = _tri_tables(nk, True)
    scale = 1.0 / math.sqrt(HEAD_DIM)

    def body(qi_ref, ki_ref, q_ref, k_ref, v_ref, do_ref, cqr_ref, ckc_ref, lse_ref, dl_ref, dk_ref, dv_ref, dc_ref,
             dk_acc, dv_acc, dc_acc):
        iq, ik = qi_ref[pl.program_id(1)], ki_ref[pl.program_id(1)]

        def step(diagonal):
            for g in range(grp):
                sl = slice(g * HEAD_DIM, (g + 1) * HEAD_DIM)
                qv = q_ref[:, sl]
                dov = do_ref[:, sl]
                st = _fox_scores(k_ref[:, sl], qv, -ckc_ref[g, :, 0:1], cqr_ref[g] - lse_ref[g], scale, diagonal, True)
                pt = jnp.exp(st)
                dv_acc[g] += _dot(pt, dov, NN)
                dpt = _dot(v_ref[:, sl], dov, NT)
                dst = pt * (dpt - dl_ref[g])
                dk_acc[g] += _dot(dst, qv, NN)
                dc_acc[g] += jnp.sum(dst, axis=1, keepdims=True)

        @pl.when(iq == ik)
        def _():
            dk_acc[...] = jnp.zeros_like(dk_acc)
            dv_acc[...] = jnp.zeros_like(dv_acc)
            dc_acc[...] = jnp.zeros_like(dc_acc)
            step(True)

        @pl.when(iq > ik)
        def _():
            step(False)

        @pl.when(iq == nk - 1)
        def _():
            for g in range(grp):
                sl = slice(g * HEAD_DIM, (g + 1) * HEAD_DIM)
                dk_ref[:, sl] = (dk_acc[g] * scale).astype(BF16)
                dv_ref[:, sl] = dv_acc[g].astype(BF16)
                dc_ref[g] = _as_row(-dc_acc[g])

    ng = heads // grp
    qblk = pl.BlockSpec((t, grp * HEAD_DIM), lambda h, p, qi, ki: (qi[p], h))
    qrow = pl.BlockSpec((grp, 1, t), lambda h, p, qi, ki: (h, 0, qi[p]))
    kblk = lambda off: pl.BlockSpec((t, grp * HEAD_DIM), lambda h, p, qi, ki: (ki[p], off + h))
    kout = pl.BlockSpec((t, grp * HEAD_DIM), lambda h, p, qi, ki: (ki[p], h))
    return _hosted(
        body, name="fox_dkv", grid=(ng, int(qi.shape[0])), prefetch=(qi, ki),
        in_specs=[qblk, kblk(ng), kblk(2 * ng), qblk, qrow,
                  pl.BlockSpec((grp, t, LANES), lambda h, p, qi, ki: (h, ki[p], 0)), qrow, qrow],
        out_specs=[kout, kout, pl.BlockSpec((grp, 1, t), lambda h, p, qi, ki: (h, 0, ki[p]))],
        scratch_shapes=[pltpu.VMEM((grp, t, HEAD_DIM), F32), pltpu.VMEM((grp, t, HEAD_DIM), F32),
                        pltpu.VMEM((grp, t, 1), F32)],
        out_shape=[jax.ShapeDtypeStruct((s, heads * HEAD_DIM), BF16)] * 2 + [jax.ShapeDtypeStruct((heads, 1, s), F32)],
        args=(proj, proj, proj, do, cum_row, cum_col, lse_row, delta_row), carry=carry)


def _shift_rows(xv, halo, j, forward):
    n = xv.shape[0]
    rid = lax.broadcasted_iota(jnp.int32, (8, xv.shape[1]), 0)
    if forward:
        xs = pltpu.roll(xv, n - j, 0)
        hs = pltpu.roll(halo, 8 - j, 0)
        edge = jnp.where(rid >= 8 - j, hs, xs[n - 8:, :])
        return jnp.concatenate([xs[:n - 8, :], edge], axis=0)
    xs = pltpu.roll(xv, j, 0)
    hs = pltpu.roll(halo, j, 0)
    edge = jnp.where(rid < j, hs, xs[:8, :])
    return jnp.concatenate([edge, xs[8:, :]], axis=0)


def _conv_silu(xv, halo, w):
    xc = w[CONV_W - 1:CONV_W, :] * xv
    for j in range(1, CONV_W):
        xc = xc + w[CONV_W - 1 - j:CONV_W - j, :] * _shift_rows(xv, halo, j, False)
    return xc, xc * _sigmoid(xc)


def _gdn_pre(proj, conv_w, heads):
    s = proj.shape[0]
    cw = 3 * heads * HEAD_DIM
    ts = _tile(s, 256)
    tb = ts // 8

    def body(x_ref, halo_ref, w_ref, q_ref, k_ref, v_ref):
        halo = jnp.where(pl.program_id(0) == 0, 0.0, halo_ref[...])
        _, y = _conv_silu(x_ref[...], halo, w_ref[...])
        for h in range(heads):
            for part, ref in enumerate((q_ref, k_ref, v_ref)):
                c0 = (part * heads + h) * HEAD_DIM
                blk = y[:, c0:c0 + HEAD_DIM]
                if part < 2:
                    blk = blk * lax.rsqrt(jnp.sum(blk * blk, axis=1, keepdims=True) + EPS)
                ref[h] = blk

    out = pl.BlockSpec((heads, ts, HEAD_DIM), lambda i: (0, i, 0))
    return _pcall(
        body, name="gdn_pre", grid=(s // ts,),
        in_specs=[pl.BlockSpec((ts, cw), lambda i: (i, 1)),
                  pl.BlockSpec((8, cw), lambda i: (jnp.maximum(i * tb - 1, 0), 1)),
                  pl.BlockSpec((CONV_W, cw), lambda i: (0, 0))],
        out_specs=[out, out, out],
        out_shape=[jax.ShapeDtypeStruct((heads, s, HEAD_DIM), F32)] * 3,
        compiler_params=_params("parallel"),
    )(proj, proj, conv_w)


def _gdn_pre_bwd_act(proj, conv_w, dq, dk, dv, heads):
    s = proj.shape[0]
    cw = 3 * heads * HEAD_DIM
    ts = _tile(s, 256)
    tb = ts // 8

    def body(x_ref, halo_ref, w_ref, dq_ref, dk_ref, dv_ref, dxc_ref, dw_ref):
        @pl.when(pl.program_id(0) == 0)
        def _():
            dw_ref[...] = jnp.zeros_like(dw_ref)

        xv = x_ref[...]
        halo = jnp.where(pl.program_id(0) == 0, 0.0, halo_ref[...])
        xc, y = _conv_silu(xv, halo, w_ref[...])
        sg = _sigmoid(xc)
        dsilu = sg * (1.0 + xc * (1.0 - sg))
        for h in range(heads):
            for part, ref in enumerate((dq_ref, dk_ref, dv_ref)):
                c0 = (part * heads + h) * HEAD_DIM
                g = ref[h]
                if part < 2:
                    blk = y[:, c0:c0 + HEAD_DIM]
                    r = lax.rsqrt(jnp.sum(blk * blk, axis=1, keepdims=True) + EPS)
                    g = r * g - blk * (r * r * r) * jnp.sum(g * blk, axis=1, keepdims=True)
                dxc_ref[:, c0:c0 + HEAD_DIM] = g * dsilu[:, c0:c0 + HEAD_DIM]
        dxc = dxc_ref[...]
        rows = [jnp.sum(dxc * (xv if j == 0 else _shift_rows(xv, halo, j, False)), axis=0, keepdims=True)
                for j in range(CONV_W)]
        dw_ref[...] += jnp.concatenate([rows[CONV_W - 1 - k] for k in range(CONV_W)]
                                       + [jnp.zeros((8 - CONV_W, cw), F32)], axis=0)

    hblk = pl.BlockSpec((heads, ts, HEAD_DIM), lambda i: (0, i, 0))
    return _pcall(
        body, name="gdn_pre_bwd_act", grid=(s // ts,),
        in_specs=[pl.BlockSpec((ts, cw), lambda i: (i, 1)),
                  pl.BlockSpec((8, cw), lambda i: (jnp.maximum(i * tb - 1, 0), 1)),
                  pl.BlockSpec((CONV_W, cw), lambda i: (0, 0)), hblk, hblk, hblk],
        out_specs=[pl.BlockSpec((ts, cw), lambda i: (i, 0)), pl.BlockSpec((8, cw), lambda i: (0, 0))],
        out_shape=[jax.ShapeDtypeStruct((s, cw), F32), jax.ShapeDtypeStruct((8, cw), F32)],
        compiler_params=_params("arbitrary"),
    )(proj, proj, conv_w, dq, dk, dv)


def _gdn_pre_bwd_conv(dxc, conv_w):
    s, cw = dxc.shape
    ts = _tile(s, 256)
    tb = ts // 8
    last = s // 8 - 1

    def body(g_ref, halo_ref, w_ref, dx_ref):
        gv = g_ref[...]
        w = w_ref[...]
        halo = jnp.where(pl.program_id(0) == s // ts - 1, 0.0, halo_ref[...])
        dx = w[CONV_W - 1:CONV_W, :] * gv
        for j in range(1, CONV_W):
            dx = dx + w[CONV_W - 1 - j:CONV_W - j, :] * _shift_rows(gv, halo, j, True)
        dx_ref[...] = dx.astype(BF16)

    return _pcall(
        body, name="gdn_pre_bwd_conv", grid=(s // ts,),
        in_specs=[pl.BlockSpec((ts, cw), lambda i: (i, 0)),
                  pl.BlockSpec((8, cw), lambda i: (jnp.minimum((i + 1) * tb, last), 0)),
                  pl.BlockSpec((CONV_W, cw), lambda i: (0, 0))],
        out_specs=pl.BlockSpec((ts, cw), lambda i: (i, 0)),
        out_shape=jax.ShapeDtypeStruct((s, cw), BF16),
        compiler_params=_params("parallel"),
    )(dxc, dxc, conv_w)


def _bdot(a, b, ca, cb):
    return lax.dot_general(a.astype(BF16), b.astype(BF16), (((ca,), (cb,)), ((0,), (0,))),
                           preferred_element_type=F32)


def _bdot_hp(a, b, ca, cb):
    ah = a.astype(BF16)
    al = (a - ah.astype(F32)).astype(BF16)
    bh = b.astype(BF16)
    bl = (b - bh.astype(F32)).astype(BF16)
    d = lambda p, q: lax.dot_general(p, q, (((ca,), (cb,)), ((0,), (0,))), preferred_element_type=F32)
    return d(ah, bh) + (d(ah, bl) + d(al, bh))


def _gdn_gates(small, a_lane, dt_lane, heads):
    lane = lax.broadcasted_iota(jnp.int32, small.shape, 1)
    za = small + dt_lane
    g_all = -jnp.exp(a_lane) * (jnp.maximum(za, 0.0) + jnp.log(1.0 + jnp.exp(-jnp.abs(za))))
    b_all = _sigmoid(small)
    pick = lambda v, l: jnp.sum(jnp.where(lane == l, v, 0.0), axis=1, keepdims=True)
    g = jnp.stack([pick(g_all, heads + h) for h in range(heads)], axis=0)
    beta = jnp.stack([pick(b_all, 2 * heads + h) for h in range(heads)], axis=0)
    return g, beta


def _chunk_masks(c):
    ii = lax.broadcasted_iota(jnp.int32, (1, c, c), 1)
    jj = lax.broadcasted_iota(jnp.int32, (1, c, c), 2)
    return ii >= jj, ii > jj, ii == jj


def _col_to_row(col, eye):
    return jnp.sum(jnp.where(eye, col, 0.0), axis=1, keepdims=True)


def _row_to_col(row, eye):
    return jnp.sum(jnp.where(eye, row, 0.0), axis=2, keepdims=True)


def _gdn_chunk(q, k, v, g, beta, state):
    c = q.shape[1]
    incl, strict, eye = _chunk_masks(c)
    g_row = _col_to_row(g, eye)
    gc_col = jnp.sum(jnp.where(incl, g_row, 0.0), axis=2, keepdims=True)
    gc_row = _col_to_row(gc_col, eye)
    gam = jnp.where(incl, jnp.exp(jnp.where(incl, gc_col - gc_row, NEG)), 0.0)
    egc = jnp.exp(gc_col)
    kb = k * beta
    vb = v * beta
    kbe = kb * egc
    low = jnp.where(strict, _bdot(kb, k, 2, 2), 0.0) * gam
    p = -low
    tinv = jnp.where(eye, 1.0, 0.0) + p
    width = 2
    while width < c:
        p = _bdot_hp(p, p, 2, 1)
        tinv = tinv + _bdot_hp(tinv, p, 2, 1)
        width *= 2
    u = _bdot(tinv, vb, 2, 1)
    w = _bdot(tinv, kbe, 2, 1)
    att = jnp.where(incl, _bdot(q, k, 2, 2), 0.0) * gam
    vn = u - _bdot(w, state, 2, 1)
    qe = q * egc
    o = _bdot(qe, state, 2, 1) + _bdot(att, vn, 2, 1)
    gl = jnp.sum(g, axis=1, keepdims=True)
    edec = jnp.exp(gl - gc_col)
    kdec = k * edec
    egl = jnp.exp(gl)
    new_state = state * egl + _bdot(kdec, vn, 1, 1)
    return dict(incl=incl, strict=strict, eye=eye, gam=gam, egc=egc, kb=kb, vb=vb, kbe=kbe, low=low, tinv=tinv, w=w,
                att=att, vn=vn, qe=qe, o=o, edec=edec, kdec=kdec, egl=egl, new_state=new_state)


def _gdn_load(q_ref, k_ref, v_ref, small_ref, a_ref, dt_ref, rows, heads):
    q = q_ref[:, rows, :] * (HEAD_DIM ** -0.5)
    g, beta = _gdn_gates(small_ref[rows, :], a_ref[...], dt_ref[...], heads)
    return q, k_ref[:, rows, :], v_ref[:, rows, :], g, beta


def _gdn_fwd(q, k, v, proj, z_blk, small_blk, a_lane, dt_lane, w_norm):
    heads, s, _ = q.shape
    c = min(GDN_CHUNK, s)
    r = _tile(s, 512)
    npb = r // c
    gw = heads * HEAD_DIM

    def body(q_ref, k_ref, v_ref, z_ref, small_ref, a_ref, dt_ref, w_ref, o_ref, st_ref, state):
        @pl.when(pl.program_id(0) == 0)
        def _():
            state[...] = jnp.zeros_like(state)

        def chunk(cb, carry):
            rows = pl.ds(pl.multiple_of(cb * c, c), c)
            qv, kv, vv, g, beta = _gdn_load(q_ref, k_ref, v_ref, small_ref, a_ref, dt_ref, rows, heads)
            st = state[...]
            st_ref[:, cb] = st
            res = _gdn_chunk(qv, kv, vv, g, beta, st)
            state[...] = res["new_state"]
            o = res["o"]
            rn = lax.rsqrt(jnp.mean(o * o, axis=2, keepdims=True) + EPS)
            zv = z_ref[rows, :]
            for h in range(heads):
                zh = zv[:, h * HEAD_DIM:(h + 1) * HEAD_DIM]
                o_ref[rows, h * HEAD_DIM:(h + 1) * HEAD_DIM] = (
                    o[h] * rn[h] * w_ref[...] * (zh * _sigmoid(zh))).astype(BF16)
            return carry

        lax.fori_loop(0, npb, chunk, 0)

    hblk = pl.BlockSpec((heads, r, HEAD_DIM), lambda i: (0, i, 0))
    row = pl.BlockSpec((1, LANES), lambda i: (0, 0))
    return _pcall(
        body, name="gdn_fwd", grid=(s // r,),
        in_specs=[hblk, hblk, hblk, pl.BlockSpec((r, gw), lambda i: (i, z_blk)),
                  pl.BlockSpec((r, LANES), lambda i: (i, small_blk)), row, row, row],
        out_specs=[pl.BlockSpec((r, gw), lambda i: (i, 0)),
                   pl.BlockSpec((heads, npb, HEAD_DIM, HEAD_DIM), lambda i: (0, i, 0, 0))],
        out_shape=[jax.ShapeDtypeStruct((s, gw), BF16),
                   jax.ShapeDtypeStruct((heads, s // c, HEAD_DIM, HEAD_DIM), F32)],
        scratch_shapes=[pltpu.VMEM((heads, HEAD_DIM, HEAD_DIM), F32)],
        compiler_params=_params("arbitrary"),
    )(q, k, v, proj, proj, a_lane, dt_lane, w_norm)


def _gdn_bwd(q, k, v, proj, z_blk, small_blk, a_lane, dt_lane, w_norm, states, do_cat, do_blk):
    heads, s, _ = q.shape
    c = min(GDN_CHUNK, s)
    r = _tile(s, 512)
    npb = r // c
    nb = s // r
    gw = heads * HEAD_DIM

    def body(q_ref, k_ref, v_ref, z_ref, small_ref, a_ref, dt_ref, w_ref, st_ref, do_ref,
             dq_ref, dk_ref, dv_ref, dz_ref, dsm_ref, da_ref, ddt_ref, dw_ref, dstate):
        @pl.when(pl.program_id(0) == 0)
        def _():
            dstate[...] = jnp.zeros_like(dstate)
            da_ref[...] = jnp.zeros_like(da_ref)
            ddt_ref[...] = jnp.zeros_like(ddt_ref)
            dw_ref[...] = jnp.zeros_like(dw_ref)

        def chunk(it, carry):
            cb = npb - 1 - it
            rows = pl.ds(pl.multiple_of(cb * c, c), c)
            qv, kv, vv, g, beta = _gdn_load(q_ref, k_ref, v_ref, small_ref, a_ref, dt_ref, rows, heads)
            st = st_ref[:, cb]
            f = _gdn_chunk(qv, kv, vv, g, beta, st)
            incl, strict, eye = f["incl"], f["strict"], f["eye"]
            o = f["o"]
            wv = w_ref[...]
            zv = z_ref[rows, :]
            dov = do_ref[rows, :]
            rn = lax.rsqrt(jnp.mean(o * o, axis=2, keepdims=True) + EPS)
            do_l, dw_acc = [], jnp.zeros((1, HEAD_DIM), F32)
            for h in range(heads):
                sl = slice(h * HEAD_DIM, (h + 1) * HEAD_DIM)
                zh, gh = zv[:, sl], dov[:, sl]
                sg = _sigmoid(zh)
                on = o[h] * rn[h]
                dz_ref[rows, sl] = (gh * (on * wv) * (sg * (1.0 + zh * (1.0 - sg)))).astype(BF16)
                gn = gh * (zh * sg)
                dw_acc = dw_acc + jnp.sum(gn * on, axis=0, keepdims=True)
                wg = gn * wv
                do_l.append(rn[h] * wg - o[h] * (rn[h] * rn[h] * rn[h]) * jnp.mean(wg * o[h], axis=1, keepdims=True))
            dw_ref[...] += dw_acc
            do = jnp.stack(do_l, axis=0)
            ds_out = dstate[...]
            dvn = _bdot(f["att"], do, 1, 1) + _bdot(f["kdec"], ds_out, 2, 1)
            datt = jnp.where(incl, _bdot(do, f["vn"], 2, 2), 0.0)
            dqe = _bdot(do, st, 2, 2)
            dstate[...] = _bdot(f["qe"], do, 1, 1) + f["egl"] * ds_out - _bdot(f["w"], dvn, 1, 1)
            dw = -_bdot(dvn, st, 2, 2)
            dkdec = _bdot(f["vn"], ds_out, 2, 2)
            t_kdec = jnp.sum(dkdec * f["kdec"], axis=2, keepdims=True)
            dgl = (jnp.sum(jnp.sum(st * ds_out, axis=2, keepdims=True), axis=1, keepdims=True) * f["egl"]
                   + jnp.sum(t_kdec, axis=1, keepdims=True))
            dgc = jnp.sum(dqe * f["qe"], axis=2, keepdims=True) - t_kdec
            dq = dqe * f["egc"]
            dk = dkdec * f["edec"]
            dtinv = _bdot(dvn, f["vb"], 2, 2) + _bdot(dw, f["kbe"], 2, 2)
            dvb = _bdot(f["tinv"], dvn, 1, 1)
            dkbe = _bdot(f["tinv"], dw, 1, 1)
            dkb = dkbe * f["egc"]
            dgc = dgc + jnp.sum(dkbe * f["kbe"], axis=2, keepdims=True)
            dlow = jnp.where(strict, -_bdot_hp(_bdot_hp(f["tinv"], dtinv, 1, 1), f["tinv"], 2, 2), 0.0)
            ml = dlow * f["gam"]
            dkb = dkb + _bdot(ml, kv, 2, 1)
            dk = dk + _bdot(ml, f["kb"], 1, 1)
            ma = datt * f["gam"]
            dq = dq + _bdot(ma, kv, 2, 1)
            dk = dk + _bdot(ma, qv, 1, 1)
            e = dlow * f["low"] + datt * f["att"]
            dgc = dgc + jnp.sum(e, axis=2, keepdims=True) - _row_to_col(jnp.sum(e, axis=1, keepdims=True), eye)
            dk = dk + beta * dkb
            dbeta = jnp.sum(dkb * kv, axis=2, keepdims=True) + jnp.sum(dvb * vv, axis=2, keepdims=True)
            dgc_row = _col_to_row(dgc, eye)
            dg = jnp.sum(jnp.where(incl, 0.0, dgc_row) + jnp.where(eye, dgc_row, 0.0), axis=2, keepdims=True) + dgl
            dq_ref[:, rows, :] = dq * (HEAD_DIM ** -0.5)
            dk_ref[:, rows, :] = dk
            dv_ref[:, rows, :] = beta * dvb
            small = small_ref[rows, :]
            lane = lax.broadcasted_iota(jnp.int32, small.shape, 1)
            dg_l = jnp.zeros(small.shape, F32)
            db_l = jnp.zeros(small.shape, F32)
            for h in range(heads):
                dg_l = dg_l + jnp.where(lane == heads + h, dg[h], 0.0)
                db_l = db_l + jnp.where(lane == 2 * heads + h, dbeta[h], 0.0)
            za = small + dt_ref[...]
            nexp = -jnp.exp(a_ref[...])
            softplus = jnp.maximum(za, 0.0) + jnp.log(1.0 + jnp.exp(-jnp.abs(za)))
            da_logit = dg_l * nexp * _sigmoid(za)
            sb = _sigmoid(small)
            dsm_ref[rows, :] = da_logit + db_l * sb * (1.0 - sb)
            ddt_ref[...] += jnp.sum(da_logit, axis=0, keepdims=True)
            da_ref[...] += jnp.sum(dg_l * nexp * softplus, axis=0, keepdims=True)
            return carry

        lax.fori_loop(0, npb, chunk, 0)

    rev = lambda i: nb - 1 - i
    hblk = pl.BlockSpec((heads, r, HEAD_DIM), lambda i: (0, rev(i), 0))
    row = pl.BlockSpec((1, LANES), lambda i: (0, 0))
    wide = lambda blk: pl.BlockSpec((r, gw), lambda i: (rev(i), blk))
    return _pcall(
        body, name="gdn_bwd", grid=(nb,),
        in_specs=[hblk, hblk, hblk, wide(z_blk), pl.BlockSpec((r, LANES), lambda i: (rev(i), small_blk)),
                  row, row, row, pl.BlockSpec((heads, npb, HEAD_DIM, HEAD_DIM), lambda i: (0, rev(i), 0, 0)),
                  wide(do_blk)],
        out_specs=[hblk, hblk, hblk, wide(0), pl.BlockSpec((r, LANES), lambda i: (rev(i), 0)), row, row, row],
        out_shape=[jax.ShapeDtypeStruct((heads, s, HEAD_DIM), F32)] * 3
        + [jax.ShapeDtypeStruct((s, gw), BF16), jax.ShapeDtypeStruct((s, LANES), F32)]
        + [jax.ShapeDtypeStruct((1, LANES), F32)] * 3,
        scratch_shapes=[pltpu.VMEM((heads, HEAD_DIM, HEAD_DIM), F32)],
        compiler_params=_params("arbitrary"),
    )(q, k, v, proj, proj, a_lane, dt_lane, w_norm, states, do_cat)


def _final(x, target, gf):
    s, d = x.shape
    ts = _tile(s, 512)

    def body(x_ref, t_ref, g_ref, loss_ref, dx_ref, dg_ref):
        @pl.when(pl.program_id(0) == 0)
        def _():
            loss_ref[...] = jnp.zeros_like(loss_ref)
            dg_ref[...] = jnp.zeros_like(dg_ref)

        xv = x_ref[...]
        gv = g_ref[...]
        r = lax.rsqrt(jnp.mean(xv * xv, axis=-1, keepdims=True) + EPS)
        xn = xv * r
        err = xn * gv - t_ref[...]
        per_tok = jnp.mean(err * err, axis=-1, keepdims=True)
        loss_ref[...] += 0.5 * jnp.sum(per_tok, axis=0, keepdims=True)
        dy = err * (1.0 / d)
        dg_ref[...] += jnp.sum(dy * xn, axis=0, keepdims=True)
        dxn = dy * gv
        dx_ref[...] = r * (dxn - xn * jnp.mean(dxn * xn, axis=-1, keepdims=True))

    blk = pl.BlockSpec((ts, d), lambda i: (i, 0))
    row = pl.BlockSpec((1, d), lambda i: (0, 0))
    return _pcall(
        body, name="final_loss", grid=(s // ts,),
        in_specs=[blk, blk, row], out_specs=[pl.BlockSpec((1, LANES), lambda i: (0, 0)), blk, row],
        out_shape=[jax.ShapeDtypeStruct((1, LANES), F32), jax.ShapeDtypeStruct((s, d), F32),
                   jax.ShapeDtypeStruct((1, d), F32)],
        compiler_params=_params("arbitrary"),
    )(x, target, gf)


def _adamw(parts, w, m, v, name):
    npart, rows, cols = parts.shape
    tr = _tile(rows, max(8, ADAM_BLOCK_BYTES // (4 * npart * cols)))
    c1 = 1.0 - ADAM_B1 ** ADAM_STEP
    c2 = 1.0 - ADAM_B2 ** ADAM_STEP

    def body(p_ref, w_ref, m_ref, v_ref, g_ref, d_ref, mo_ref, vo_ref):
        g = p_ref[0].astype(F32)
        for i in range(1, npart):
            g = g + p_ref[i].astype(F32)
        mn = ADAM_B1 * m_ref[...] + (1.0 - ADAM_B1) * g
        vn = ADAM_B2 * v_ref[...] + (1.0 - ADAM_B2) * (g * g)
        g_ref[...] = g
        mo_ref[...] = mn
        vo_ref[...] = vn
        d_ref[...] = -ADAM_LR * ((mn / c1) / (jnp.sqrt(vn / c2) + ADAM_EPS) + ADAM_WD * w_ref[...])

    blk = pl.BlockSpec((tr, cols), lambda i: (i, 0))
    return _pcall(
        body, name=name, grid=(rows // tr,),
        in_specs=[pl.BlockSpec((npart, tr, cols), lambda i: (0, i, 0)), blk, blk, blk],
        out_specs=[blk] * 4, out_shape=[jax.ShapeDtypeStruct((rows, cols), F32)] * 4,
        compiler_params=_params("parallel"),
    )(parts, w, m, v)


def _pad_lanes(v, n=LANES, at=0):
    return jnp.pad(v, ((0, 0), (at, n - at - v.shape[1])))


def _my_cols(a, me, width):
    return lax.dynamic_slice_in_dim(a, me * width, width, axis=a.ndim - 1)


def kernel(x, c, ada_w, ada_b, norm_g, ffn_w_gate, ffn_w_up, ffn_w_down, w_in, w_out, fox_f_bias, fox_out_norm, gdn_conv, gdn_A_log, gdn_dt_bias, gdn_out_norm, final_norm, loss_target, m_ada_w, m_ada_b, m_norm_g, m_ffn_w_gate, m_ffn_w_up, m_ffn_w_down, m_w_in, m_w_out, m_fox_f_bias, m_fox_out_norm, m_gdn_conv, m_gdn_A_log, m_gdn_dt_bias, m_gdn_out_norm, m_final_norm, v_ada_w, v_ada_b, v_norm_g, v_ffn_w_gate, v_ffn_w_up, v_ffn_w_down, v_w_in, v_w_out, v_fox_f_bias, v_fox_out_norm, v_gdn_conv, v_gdn_A_log, v_gdn_dt_bias, v_gdn_out_norm, v_final_norm):
    me = _linear(_mesh_pos())
    x0 = x[0]
    s, d = x0.shape
    heads = d // (2 * HEAD_DIM)
    fw = heads * HEAD_DIM
    ng = norm_g.shape[-1]
    ncv = gdn_conv.shape[-1]
    nada = ada_w.shape[-1]
    in_w = w_in.shape[-1] * N_DEV
    in_pad = -(-in_w // 512) * 512

    pack = jnp.concatenate([c, norm_g[0].reshape(1, 3 * ng), gdn_conv[0].reshape(1, CONV_W * ncv)], axis=1)
    pack_all = _gather_row(pack, "gather_small_params")
    c_all = pack_all[:, :d]
    g_all = pack_all[:, d:d + 3 * ng].reshape(N_DEV, 3, ng).transpose(1, 0, 2).reshape(3, d)
    conv_all = pack_all[:, d + 3 * ng:].reshape(N_DEV, CONV_W, ncv).transpose(1, 0, 2).reshape(CONV_W, 3 * fw)

    mod_blk = _ada_fwd(c_all, ada_w[0], _my_cols(ada_b, me, nada))
    mod_all = _exchange([mod_blk], scatter=False, in_vmem=True, name="gather_mod")[0]
    mod = lax.dynamic_slice_in_dim(mod_all, me, 1, axis=1).reshape(N_MOD, d)
    sh1, sc1, gt1, sh2, sc2, gt2, sh3, sc3, gt3 = [mod[i:i + 1] for i in range(N_MOD)]

    wg_sh, wu_sh, wd_sh = [w[0].astype(BF16) for w in (ffn_w_gate, ffn_w_up, ffn_w_down)]
    layer = lambda w, i: w[i:i + 1]
    wg0, wu0 = _gather_via_sibling([layer(wg_sh, 0), layer(wu_sh, 0)], "gather_ffn1_up_weights")
    small_blk = 7 * heads

    bias_lane = _pad_lanes(fox_f_bias)
    a_lane = _pad_lanes(gdn_A_log, at=heads)
    dt_lane = _pad_lanes(gdn_dt_bias, at=heads)

    h1 = _norm_mod(x0, g_all[0:1], sc1, sh1, "norm_mod_1")
    (a1, b1, s1), (wd0, wout_g) = _ffn_up(h1, wg0, wu0, 0, "ffn1_up",
                                          carry=([layer(wd_sh, 0), w_out[0].astype(BF16)], False))
    (f1, x1), (win_g,) = _ffn_down(s1, wd0, 0, x0, gt1, "ffn1_down", carry=([w_in[0].astype(BF16)], False))
    win_full = win_g.transpose(1, 0, 2).reshape(d, in_w)
    o_f, o_qkv, o_a, o_z = 3 * fw, 3 * fw + heads, 6 * fw + heads, 6 * fw + 3 * heads
    win_al = jnp.concatenate(
        [win_full[:, :o_f], win_full[:, o_qkv:o_a], win_full[:, o_z:], win_full[:, o_f:o_qkv],
         win_full[:, o_a:o_z], jnp.zeros((d, in_pad - in_w), BF16)], axis=1)
    wout_full = wout_g.reshape(d, d)

    h2 = _norm_mod(x1, g_all[1:2], sc2, sh2, "norm_mod_2")
    proj = _mm(h2, win_al, name="in_proj", tn=1536)
    cum = _fox_gate(proj, small_blk, bias_lane)
    cum_t = cum[:, :heads].T
    cum_row = cum_t[:, None, :]
    cum_col = jnp.broadcast_to(cum_t[:, :, None], (heads, s, LANES))
    (o_raw, lse, lse_row, o_fox), (wg1, wu1, wd1) = _fox_fwd(
        proj, cum_col, cum_row, fox_out_norm, heads,
        carry=([layer(wg_sh, 1), layer(wu_sh, 1), layer(wd_sh, 1)], False))
    qg, kg, vg = _gdn_pre(proj, conv_all, heads)
    o_gdn, states = _gdn_fwd(qg, kg, vg, proj, 6, small_blk, a_lane, dt_lane, gdn_out_norm)
    o_cat = jnp.concatenate([o_fox, o_gdn], axis=1)
    mix, x2 = _mm(o_cat, wout_full, name="out_proj", residual=(x1, gt2))

    h3 = _norm_mod(x2, g_all[2:3], sc3, sh3, "norm_mod_3")
    (a3, b3, s3), _ = _ffn_up(h3, wg1, wu1, 0, "ffn2_up")
    (f3, x3), _ = _ffn_down(s3, wd1, 0, x2, gt3, "ffn2_down")

    loss_row, dx3, d_final = _final(x3, loss_target[0], final_norm.reshape(1, d))
    loss = lax.psum(loss_row[0, 0], MESH_AXES)

    df3, dgt3 = _gate_bwd(dx3, f3, gt3, MACARON_W, "ffn2_gate_bwd")
    (da3, db3), _ = _ffn_bwd_act(df3, wd1, 0, a3, b3, "ffn2_bwd_act")
    (dwd2,), _ = _ffn_bwd_wd(s3, df3, "ffn2_bwd_wd")
    (dh3,), (r_wd2,) = _ffn_bwd_h(da3, db3, wg1, wu1, 0, "ffn2_bwd_h", carry=([dwd2], True))
    dwg2, dwu2 = _ffn_bwd_wgu(h3, da3, db3, "ffn2_bwd_wgu")
    dx2, dsh3, dsc3, dg3 = _norm_mod_bwd(x2, dh3, dx3, g_all[2:3], sc3, "norm_mod_3_bwd")

    dmix, dgt2 = _gate_bwd(dx2, mix, gt2, 1.0, "mix_gate_bwd")
    do_cat = _mm(dmix, wout_full, tb=True, name="out_proj_bwd_x")
    dwout = _mm(o_cat, dmix, ta=True, out_dtype=BF16, name="out_proj_bwd_w", tk=512)
    do_fox, delta, delta_row, d_foxw = _fox_prep_bwd(do_cat, o_raw, fox_out_norm, heads)
    (dq_f, dcum_q), (r_wg2, r_wu2) = _fox_dq(proj, do_fox, cum_col, cum_row, lse, delta, heads,
                                             carry=([dwg2, dwu2], True))
    (dk_f, dv_f, dcum_k), (r_wout,) = _fox_dkv(proj, do_fox, cum_col, cum_row, lse_row, delta_row, heads,
                                               carry=([dwout.reshape(N_DEV, d // N_DEV, d)], True))
    head_lanes = lambda t: jnp.pad(t[:, 0, :].T, ((0, 0), (0, LANES - heads)))
    dsm_fox, d_fbias = _fox_gate_bwd(head_lanes(dcum_q), head_lanes(dcum_k), proj, small_blk, bias_lane)
    dqg, dkg, dvg, dz, dsm_gdn, d_alog, d_dt, d_gdnw = _gdn_bwd(
        qg, kg, vg, proj, 6, small_blk, a_lane, dt_lane, gdn_out_norm, states, do_cat, 1)
    dxc, d_conv = _gdn_pre_bwd_act(proj, conv_all, dqg, dkg, dvg, heads)
    dqkv = _gdn_pre_bwd_conv(dxc, conv_all)
    dsmall = (dsm_fox + dsm_gdn).astype(BF16)
    dproj = jnp.concatenate([dq_f, dk_f, dv_f, dqkv, dz, dsmall, jnp.zeros((s, in_pad - 7 * fw - LANES), BF16)], axis=1)
    dh2 = _mm(dproj, win_al, tb=True, name="in_proj_bwd_x", tk=1536)
    dwin_al = _mm(h2, dproj, ta=True, out_dtype=BF16, name="in_proj_bwd_w", tm=2048, tn=1536, tk=1024)
    dwin_full = jnp.concatenate(
        [dwin_al[:, :o_f], dwin_al[:, 7 * fw:7 * fw + heads], dwin_al[:, o_f:o_f + 3 * fw],
         dwin_al[:, 7 * fw + heads:7 * fw + 3 * heads], dwin_al[:, 6 * fw:7 * fw]], axis=1)
    dwin_parts = dwin_full.reshape(d, N_DEV, in_w // N_DEV).transpose(1, 0, 2)
    dx1, dsh2, dsc2, dg2 = _norm_mod_bwd(x1, dh2, dx2, g_all[1:2], sc2, "norm_mod_2_bwd")

    df1, dgt1 = _gate_bwd(dx1, f1, gt1, MACARON_W, "ffn1_gate_bwd")
    (da1, db1), (r_win,) = _ffn_bwd_act(df1, wd0, 0, a1, b1, "ffn1_bwd_act", carry=([dwin_parts], True))
    dwg1, dwu1 = _ffn_bwd_wgu(h1, da1, db1, "ffn1_bwd_wgu")
    (dwd1,), (r_wg1,) = _ffn_bwd_wd(s1, df1, "ffn1_bwd_wd", carry=([dwg1], True))
    (dh1,), (r_wu1, r_wd1) = _ffn_bwd_h(da1, db1, wg0, wu0, 0, "ffn1_bwd_h", carry=([dwu1, dwd1], True))
    grad_x, dsh1, dsc1, dg1 = _norm_mod_bwd(x0, dh1, dx1, g_all[0:1], sc1, "norm_mod_1_bwd")

    dmod = jnp.concatenate([dsh1, dsc1, dgt1, dsh2, dsc2, dgt2, dsh3, dsc3, dgt3], axis=1)
    dmod_all = _gather_row(dmod, "gather_dmod")
    ct_pad = jnp.pad(c_all.T, ((0, 0), (0, LANES - N_DEV)))
    dmod_mine = jnp.pad(_my_cols(dmod_all, me, nada), ((0, LANES - N_DEV), (0, 0)))
    g_ada_w = _ada_bwd(ct_pad, dmod_mine)

    g_small_cols = [d_fbias, d_foxw, d_alog[:, heads:], d_dt[:, heads:], d_gdnw]
    small_part = jnp.concatenate(
        [_pad_lanes(v[:, :LANES]) for v in g_small_cols]
        + [d_final, dg1, dg2, dg3] + [d_conv[k:k + 1] for k in range(CONV_W)], axis=1)
    small_all = _gather_row(small_part, "gather_small_grads")
    off = 5 * LANES
    w_small = jnp.concatenate(
        [_pad_lanes(fox_f_bias), fox_out_norm, _pad_lanes(gdn_A_log), _pad_lanes(gdn_dt_bias), gdn_out_norm,
         final_norm.reshape(1, d)], axis=1)
    m_small = jnp.concatenate(
        [_pad_lanes(m_fox_f_bias), m_fox_out_norm, _pad_lanes(m_gdn_A_log), _pad_lanes(m_gdn_dt_bias),
         m_gdn_out_norm, m_final_norm.reshape(1, d)], axis=1)
    v_small = jnp.concatenate(
        [_pad_lanes(v_fox_f_bias), v_fox_out_norm, _pad_lanes(v_gdn_A_log), _pad_lanes(v_gdn_dt_bias),
         v_gdn_out_norm, v_final_norm.reshape(1, d)], axis=1)
    rep = _adamw(small_all[:, None, :off + d], w_small, m_small, v_small, "adamw_replicated")
    ab = _adamw(dmod_all[:, None, :], ada_b, m_ada_b, v_ada_b, "adamw_ada_b")
    g_ng = small_all[:, off + d:off + 4 * d].reshape(N_DEV, 3, d)
    ngs = _adamw(_my_cols(g_ng, me, ng), norm_g[0], m_norm_g[0], v_norm_g[0], "adamw_norm_g")
    g_cv = small_all[:, off + 4 * d:].reshape(N_DEV, CONV_W, 3 * fw)
    cvs = _adamw(_my_cols(g_cv, me, ncv), gdn_conv[0], m_gdn_conv[0], v_gdn_conv[0], "adamw_gdn_conv")

    def adam2(r0, r1, w, m, v, name):
        outs0 = _adamw(r0, w[0, 0], m[0, 0], v[0, 0], name + "_0")
        outs1 = _adamw(r1, w[0, 1], m[0, 1], v[0, 1], name + "_1")
        return [jnp.stack([p, q])[None] for p, q in zip(outs0, outs1)]

    wgs = adam2(r_wg1, r_wg2, ffn_w_gate, m_ffn_w_gate, v_ffn_w_gate, "adamw_w_gate")
    wus = adam2(r_wu1, r_wu2, ffn_w_up, m_ffn_w_up, v_ffn_w_up, "adamw_w_up")
    wds = adam2(r_wd1, r_wd2, ffn_w_down, m_ffn_w_down, v_ffn_w_down, "adamw_w_down")
    wis = [o[None] for o in _adamw(r_win, w_in[0], m_w_in[0], v_w_in[0], "adamw_w_in")]
    wos = [o[None] for o in _adamw(r_wout, w_out[0], m_w_out[0], v_w_out[0], "adamw_w_out")]
    adas = [o[None] for o in _adamw(g_ada_w[None], ada_w[0], m_ada_w[0], v_ada_w[0], "adamw_ada_w")]
    ngs = [o[None] for o in ngs]
    cvs = [o[None] for o in cvs]

    def rep_piece(i, lo, width):
        return rep[i][:, lo:lo + width]

    nh = fox_f_bias.shape[1]
    outs = []
    for i in range(4):
        outs.append([adas[i], ab[i], ngs[i], wgs[i], wus[i], wds[i], wis[i], wos[i],
                     rep_piece(i, 0, nh), rep_piece(i, LANES, HEAD_DIM), cvs[i], rep_piece(i, 2 * LANES, nh),
                     rep_piece(i, 3 * LANES, nh), rep_piece(i, 4 * LANES, HEAD_DIM), rep_piece(i, off, d).reshape(d)])
    return (loss, grad_x[None], *outs[0], *outs[1], *outs[2], *outs[3])
```

```python
import math

import numpy as np
import jax
import jax.numpy as jnp
from jax import lax
from jax.experimental import pallas as pl
from jax.experimental.pallas import tpu as pltpu

F32 = jnp.float32
BF16 = jnp.bfloat16

N_DEV = 8
MESH_AXES = ("x", "y", "c")
LANES = 128
HEAD_DIM = 128
GDN_CHUNK = 64
CONV_W = 4
N_MOD = 9
MACARON_W = 0.5
EPS = 1e-6
NEG = -1e30
VMEM_LIMIT_BYTES = 56 * 2 ** 20
ADAM_BLOCK_BYTES = 4 * 2 ** 20
FOX_HEADS_PER_STEP = 8

ADAM_LR = 0.001
ADAM_B1 = 0.9
ADAM_B2 = 0.999
ADAM_EPS = 1e-08
ADAM_WD = 0.01
ADAM_STEP = 10

MESH_ID = pl.DeviceIdType.MESH
ANY = pl.BlockSpec(memory_space=pl.ANY)
VMEM = pl.BlockSpec(memory_space=pltpu.VMEM)


def _pcall(body, **kw):
    return pl.pallas_call(body, **kw)


def _params(*semantics):
    return pltpu.CompilerParams(dimension_semantics=semantics, vmem_limit_bytes=VMEM_LIMIT_BYTES)


def _tile(n, pref):
    if n % pref == 0 and pref % 8 == 0:
        return pref
    t = 1 << (max(1, min(n, pref)).bit_length() - 1)
    while n % t:
        t //= 2
    return t if t % 8 == 0 else n


def _sigmoid(x):
    return 1.0 / (1.0 + jnp.exp(-x))


def _dot(a, b, dims):
    return lax.dot_general(a.astype(BF16), b.astype(BF16), (dims, ((), ())), preferred_element_type=F32)


NN = ((1,), (0,))
NT = ((1,), (1,))
TN = ((0,), (0,))


def _split3(x):
    hi = x.astype(BF16)
    r1 = x - hi.astype(F32)
    mid = r1.astype(BF16)
    lo = (r1 - mid.astype(F32)).astype(BF16)
    return hi, mid, lo


def _dot_exact_lhs(m_bf16, x, dims=NN):
    hi, mid, lo = _split3(x)
    d = lambda p: lax.dot_general(m_bf16, p, (dims, ((), ())), preferred_element_type=F32)
    return d(hi) + (d(mid) + d(lo))


def _dot_hp(a, b, dims):
    ah = a.astype(BF16)
    al = (a - ah.astype(F32)).astype(BF16)
    bh = b.astype(BF16)
    bl = (b - bh.astype(F32)).astype(BF16)
    d = lambda p, q: lax.dot_general(p, q, (dims, ((), ())), preferred_element_type=F32)
    return d(ah, bh) + (d(ah, bl) + d(al, bh))


def _mesh_pos():
    return lax.axis_index("x"), lax.axis_index("y"), lax.axis_index("c")


def _peer(pos, mask):
    x, y, c = pos
    return (1 - x if mask & 4 else x, 1 - y if mask & 2 else y, 1 - c if mask & 1 else c)


def _linear(pos):
    return 4 * pos[0] + 2 * pos[1] + pos[2]


def _exchange_copies(ins, outs, sems, scatter, with_receives=True):
    send_sems, recv_sems, local_sems = sems
    pos = _mesh_pos()
    me = _linear(pos)
    local, sends, recvs = [], [], []
    for i in range(len(ins)):
        src = ins[i].at[me] if scatter else ins[i]
        local.append(pltpu.make_async_copy(src, outs[i].at[me], local_sems.at[i]))
    for mask in range(1, N_DEV):
        peer = _peer(pos, mask)
        for i in range(len(ins)):
            sem = dict(send_sem=send_sems.at[i, mask - 1], recv_sem=recv_sems.at[i, mask - 1],
                       device_id=peer, device_id_type=MESH_ID)
            sends.append(pltpu.make_async_remote_copy(
                src_ref=ins[i].at[_linear(peer)] if scatter else ins[i], dst_ref=outs[i].at[me], **sem))
            if with_receives:
                recvs.append(pltpu.make_async_remote_copy(
                    src_ref=ins[i].at[me] if scatter else ins[i], dst_ref=outs[i].at[_linear(peer)], **sem))
    return local, sends, recvs


def _exchange_start(ins, outs, sems, scatter):
    local, sends, _ = _exchange_copies(ins, outs, sems, scatter, with_receives=False)
    for cp in local + sends:
        cp.start()


def _exchange_wait(ins, outs, sems, scatter):
    local, sends, recvs = _exchange_copies(ins, outs, sems, scatter)
    for cp in recvs:
        cp.wait_recv()
    for cp in sends:
        cp.wait_send()
    for cp in local:
        cp.wait()


def _exchange_sems(n):
    return [pltpu.SemaphoreType.DMA((n, N_DEV - 1)), pltpu.SemaphoreType.DMA((n, N_DEV - 1)),
            pltpu.SemaphoreType.DMA((n,))]


def _exchange_shapes(arrays, scatter):
    return [jax.ShapeDtypeStruct(a.shape if scatter else (N_DEV,) + a.shape, a.dtype) for a in arrays]


def _exchange(arrays, *, scatter, in_vmem, name):
    n = len(arrays)

    def body(*refs):
        ins, outs, sems = refs[:n], refs[n:2 * n], refs[2 * n:]
        _exchange_start(ins, outs, sems, scatter)
        _exchange_wait(ins, outs, sems, scatter)

    spec = VMEM if in_vmem else ANY
    outs = _pcall(
        body, name=name, out_shape=_exchange_shapes(arrays, scatter),
        in_specs=[spec] * n, out_specs=[spec] * n, scratch_shapes=_exchange_sems(n),
    )(*arrays)
    return list(outs)


def _gather_via_sibling(arrays, name):
    n = len(arrays)

    def body(*refs):
        ins, outs = refs[:n], refs[n:2 * n]
        send_sems, recv_sems, local_sems = refs[2 * n:]
        x, y, c = _mesh_pos()
        me, sibling = (x, y, c), (x, y, 1 - c)
        chips = [(1 - x, y), (x, 1 - y), (1 - x, 1 - y)]

        def copy(i, k, block, to, from_input=False):
            return pltpu.make_async_remote_copy(
                src_ref=ins[i] if from_input else outs[i].at[_linear(block)], dst_ref=outs[i].at[_linear(block)],
                send_sem=send_sems.at[i, k], recv_sem=recv_sems.at[i, k], device_id=to, device_id_type=MESH_ID)

        mine = [pltpu.make_async_copy(ins[i], outs[i].at[_linear(me)], local_sems.at[i]) for i in range(n)]
        first = []
        for i in range(n):
            first.append(copy(i, 0, me, sibling, True))
            first += [copy(i, 1 + j, me, (*chip, c), True) for j, chip in enumerate(chips)]
        for cp in mine + first:
            cp.start()
        passed = []
        for j, chip in enumerate(chips):
            for i in range(n):
                copy(i, 1 + j, (*chip, c), me).wait_recv()
                cp = copy(i, 4 + j, (*chip, c), sibling)
                cp.start()
                passed.append(cp)
        for i in range(n):
            copy(i, 0, sibling, me).wait_recv()
            for j, chip in enumerate(chips):
                copy(i, 4 + j, (*chip, 1 - c), me).wait_recv()
        for cp in first + passed:
            cp.wait_send()
        for cp in mine:
            cp.wait()

    outs = _pcall(
        body, name=name, out_shape=_exchange_shapes(arrays, False), in_specs=[ANY] * n, out_specs=[ANY] * n,
        scratch_shapes=_exchange_sems(n),
    )(*arrays)
    return list(outs)


def _hosted(body, *, name, grid, in_specs, out_specs, out_shape, args, scratch_shapes=(), prefetch=(), carry=None):
    n_in, n_out, n_scr, n_pre = len(in_specs), len(out_shape), len(scratch_shapes), len(prefetch)
    arrays, scatter = carry if carry is not None else ([], False)
    n = len(arrays)

    def wrapped(*refs):
        pre, r = refs[:n_pre], refs[n_pre:]
        host_in, comm_in = r[:n_in], r[n_in:n_in + n]
        r = r[n_in + n:]
        host_out, comm_out = r[:n_out], r[n_out:n_out + n]
        r = r[n_out + n:]
        host_scr, sems = r[:n_scr], r[n_scr:]
        if n:
            first = pl.program_id(0) == 0
            last = pl.program_id(0) == grid[0] - 1
            for ax in range(1, len(grid)):
                first = first & (pl.program_id(ax) == 0)
                last = last & (pl.program_id(ax) == grid[ax] - 1)

            @pl.when(first)
            def _():
                _exchange_start(comm_in, comm_out, sems, scatter)

        body(*pre, *host_in, *host_out, *host_scr)
        if n:
            @pl.when(last)
            def _():
                _exchange_wait(comm_in, comm_out, sems, scatter)

    grid_spec = pltpu.PrefetchScalarGridSpec(
        num_scalar_prefetch=n_pre, grid=grid, in_specs=list(in_specs) + [ANY] * n,
        out_specs=list(out_specs) + [ANY] * n,
        scratch_shapes=list(scratch_shapes) + (_exchange_sems(n) if n else []))
    outs = _pcall(
        wrapped, name=name, grid_spec=grid_spec, out_shape=list(out_shape) + _exchange_shapes(arrays, scatter),
        compiler_params=_params(*(["arbitrary"] * len(grid))),
    )(*prefetch, *args, *arrays)
    return list(outs[:n_out]), list(outs[n_out:])


def _gather_row(v, name):
    return _exchange([v], scatter=False, in_vmem=True, name=name)[0].reshape(N_DEV, v.shape[1])


def _ada_fwd(c_all, w, b):
    d, n = w.shape
    tn = _tile(n, 256)

    def body(c_ref, w_ref, b_ref, o_ref):
        cv = c_ref[...]
        cond = cv * _sigmoid(cv)
        o_ref[...] = _dot_hp(cond, w_ref[...], NN) + b_ref[...]

    return _pcall(
        body, name="ada_fwd", grid=(n // tn,),
        in_specs=[pl.BlockSpec((N_DEV, d), lambda j: (0, 0)), pl.BlockSpec((d, tn), lambda j: (0, j)),
                  pl.BlockSpec((1, tn), lambda j: (0, j))],
        out_specs=pl.BlockSpec((N_DEV, tn), lambda j: (0, j)),
        out_shape=jax.ShapeDtypeStruct((N_DEV, n), F32), compiler_params=_params("parallel"),
    )(c_all, w, b)


def _ada_bwd(ct_pad, dmod_pad):
    d = ct_pad.shape[0]
    n = dmod_pad.shape[1]
    tn = _tile(n, 256)

    def body(c_ref, g_ref, o_ref):
        cv = c_ref[...]
        cond = cv * _sigmoid(cv)
        o_ref[...] = _dot_hp(cond, g_ref[...], NN)

    return _pcall(
        body, name="ada_bwd", grid=(n // tn,),
        in_specs=[pl.BlockSpec((d, LANES), lambda j: (0, 0)), pl.BlockSpec((LANES, tn), lambda j: (0, j))],
        out_specs=pl.BlockSpec((d, tn), lambda j: (0, j)),
        out_shape=jax.ShapeDtypeStruct((d, n), F32), compiler_params=_params("parallel"),
    )(ct_pad, dmod_pad)


def _norm_mod(x, g, sc, sh, name):
    s, d = x.shape
    ts = _tile(s, 512)

    def body(x_ref, g_ref, sc_ref, sh_ref, h_ref):
        xv = x_ref[...]
        r = lax.rsqrt(jnp.mean(xv * xv, axis=-1, keepdims=True) + EPS)
        h_ref[...] = (xv * r * g_ref[...] * (1.0 + sc_ref[...]) + sh_ref[...]).astype(BF16)

    row = pl.BlockSpec((1, d), lambda i: (0, 0))
    return _pcall(
        body, name=name, grid=(s // ts,),
        in_specs=[pl.BlockSpec((ts, d), lambda i: (i, 0)), row, row, row],
        out_specs=pl.BlockSpec((ts, d), lambda i: (i, 0)),
        out_shape=jax.ShapeDtypeStruct((s, d), BF16), compiler_params=_params("parallel"),
    )(x, g, sc, sh)


def _norm_mod_bwd(x, dh, dx_out, g, sc, name):
    s, d = x.shape
    ts = _tile(s, 512)

    def body(x_ref, dh_ref, dxo_ref, g_ref, sc_ref, dx_ref, dsh_ref, dsc_ref, dg_ref):
        @pl.when(pl.program_id(0) == 0)
        def _():
            dsh_ref[...] = jnp.zeros_like(dsh_ref)
            dsc_ref[...] = jnp.zeros_like(dsc_ref)
            dg_ref[...] = jnp.zeros_like(dg_ref)

        xv = x_ref[...]
        dh_v = dh_ref[...]
        gv = g_ref[...]
        one_sc = 1.0 + sc_ref[...]
        r = lax.rsqrt(jnp.mean(xv * xv, axis=-1, keepdims=True) + EPS)
        xn = xv * r
        dxn = dh_v * (gv * one_sc)
        dx_ref[...] = dxo_ref[...] + r * (dxn - xn * jnp.mean(dxn * xn, axis=-1, keepdims=True))
        t = dh_v * xn
        dsh_ref[...] += jnp.sum(dh_v, axis=0, keepdims=True)
        dsc_ref[...] += jnp.sum(t * gv, axis=0, keepdims=True)
        dg_ref[...] += jnp.sum(t * one_sc, axis=0, keepdims=True)

    blk = pl.BlockSpec((ts, d), lambda i: (i, 0))
    row = pl.BlockSpec((1, d), lambda i: (0, 0))
    return _pcall(
        body, name=name, grid=(s // ts,),
        in_specs=[blk, blk, blk, row, row], out_specs=[blk, row, row, row],
        out_shape=[jax.ShapeDtypeStruct((s, d), F32)] + [jax.ShapeDtypeStruct((1, d), F32)] * 3,
        compiler_params=_params("arbitrary"),
    )(x, dh, dx_out, g, sc)


def _gate_bwd(dx, f, gt, k, name):
    s, d = dx.shape
    ts = _tile(s, 512)

    def body(dx_ref, f_ref, gt_ref, df_ref, dgt_ref):
        @pl.when(pl.program_id(0) == 0)
        def _():
            dgt_ref[...] = jnp.zeros_like(dgt_ref)

        dxv = dx_ref[...]
        df_ref[...] = ((k * gt_ref[...]) * dxv).astype(BF16)
        dgt_ref[...] += k * jnp.sum(f_ref[...] * dxv, axis=0, keepdims=True)

    blk = pl.BlockSpec((ts, d), lambda i: (i, 0))
    row = pl.BlockSpec((1, d), lambda i: (0, 0))
    return _pcall(
        body, name=name, grid=(s // ts,),
        in_specs=[blk, blk, row], out_specs=[blk, row],
        out_shape=[jax.ShapeDtypeStruct((s, d), BF16), jax.ShapeDtypeStruct((1, d), F32)],
        compiler_params=_params("arbitrary"),
    )(dx, f, gt)


def _ffn_up(h, wg, wu, layer, name, carry=None):
    s, d = h.shape
    fs = wg.shape[-1]
    tm = _tile(s, 1024)

    def body(h_ref, wg_ref, wu_ref, a_ref, b_ref, s_ref):
        hv = h_ref[...]
        a = jnp.dot(hv, wg_ref[...], preferred_element_type=F32)
        b = jnp.dot(hv, wu_ref[...], preferred_element_type=F32)
        a_ref[...] = a.astype(BF16)
        b_ref[...] = b.astype(BF16)
        s_ref[...] = (a * _sigmoid(a) * b).astype(BF16)

    wspec = pl.BlockSpec((None, None, d, fs), lambda j, m: (j, layer, 0, 0))
    ospec = pl.BlockSpec((None, tm, fs), lambda j, m: (j, m, 0))
    return _hosted(
        body, name=name, grid=(N_DEV, s // tm),
        in_specs=[pl.BlockSpec((tm, d), lambda j, m: (m, 0)), wspec, wspec],
        out_specs=[ospec, ospec, ospec],
        out_shape=[jax.ShapeDtypeStruct((N_DEV, s, fs), BF16)] * 3,
        args=(h, wg, wu), carry=carry)


def _ffn_down(sv, wd, layer, x_in, gt, name, carry=None):
    _, s, fs = sv.shape
    d = wd.shape[-1]
    tm = _tile(s, 512)

    def body(s_ref, wd_ref, x_ref, gt_ref, f_ref, xo_ref, acc):
        j = pl.program_id(1)

        @pl.when(j == 0)
        def _():
            acc[...] = jnp.zeros_like(acc)

        acc[...] += jnp.dot(s_ref[...], wd_ref[...], preferred_element_type=F32)

        @pl.when(j == N_DEV - 1)
        def _():
            fv = acc[...]
            f_ref[...] = fv
            xo_ref[...] = x_ref[...] + (MACARON_W * gt_ref[...]) * fv

    blk = pl.BlockSpec((tm, d), lambda m, j: (m, 0))
    return _hosted(
        body, name=name, grid=(s // tm, N_DEV),
        in_specs=[pl.BlockSpec((None, tm, fs), lambda m, j: (j, m, 0)),
                  pl.BlockSpec((None, None, fs, d), lambda m, j: (j, layer, 0, 0)),
                  blk, pl.BlockSpec((1, d), lambda m, j: (0, 0))],
        out_specs=[blk, blk],
        out_shape=[jax.ShapeDtypeStruct((s, d), F32)] * 2,
        scratch_shapes=[pltpu.VMEM((tm, d), F32)],
        args=(sv, wd, x_in, gt), carry=carry)


def _ffn_bwd_act(df, wd, layer, a, b, name, carry=None):
    s, d = df.shape
    fs = a.shape[-1]
    tm = _tile(s, 1024)

    def body(df_ref, wd_ref, a_ref, b_ref, da_ref, db_ref):
        ds = lax.dot_general(df_ref[...], wd_ref[...], (NT, ((), ())), preferred_element_type=F32)
        av = a_ref[...].astype(F32)
        sg = _sigmoid(av)
        da_ref[...] = (ds * b_ref[...].astype(F32) * (sg * (1.0 + av * (1.0 - sg)))).astype(BF16)
        db_ref[...] = (ds * (av * sg)).astype(BF16)

    hid = pl.BlockSpec((None, tm, fs), lambda j, m: (j, m, 0))
    return _hosted(
        body, name=name, grid=(N_DEV, s // tm),
        in_specs=[pl.BlockSpec((tm, d), lambda j, m: (m, 0)),
                  pl.BlockSpec((None, None, fs, d), lambda j, m: (j, layer, 0, 0)), hid, hid],
        out_specs=[hid, hid],
        out_shape=[jax.ShapeDtypeStruct((N_DEV, s, fs), BF16)] * 2,
        args=(df, wd, a, b), carry=carry)


def _ffn_bwd_wd(sv, df, name, carry=None):
    _, s, fs = sv.shape
    d = df.shape[1]
    tk = _tile(s, 1024)
    nk = s // tk

    def body(s_ref, df_ref, o_ref, acc):
        @pl.when(pl.program_id(1) == 0)
        def _():
            acc[...] = jnp.zeros_like(acc)

        acc[...] += lax.dot_general(s_ref[...], df_ref[...], (TN, ((), ())), preferred_element_type=F32)

        @pl.when(pl.program_id(1) == nk - 1)
        def _():
            o_ref[...] = acc[...].astype(BF16)

    return _hosted(
        body, name=name, grid=(N_DEV, nk),
        in_specs=[pl.BlockSpec((None, tk, fs), lambda j, k: (j, k, 0)), pl.BlockSpec((tk, d), lambda j, k: (k, 0))],
        out_specs=[pl.BlockSpec((None, fs, d), lambda j, k: (j, 0, 0))],
        out_shape=[jax.ShapeDtypeStruct((N_DEV, fs, d), BF16)],
        scratch_shapes=[pltpu.VMEM((fs, d), F32)],
        args=(sv, df), carry=carry)


def _ffn_bwd_h(da, db, wg, wu, layer, name, carry=None):
    _, s, fs = da.shape
    d = wg.shape[-2]
    tm = _tile(s, 1024)

    def body(da_ref, db_ref, wg_ref, wu_ref, o_ref, acc):
        j = pl.program_id(1)

        @pl.when(j == 0)
        def _():
            acc[...] = jnp.zeros_like(acc)

        acc[...] += (lax.dot_general(da_ref[...], wg_ref[...], (NT, ((), ())), preferred_element_type=F32)
                     + lax.dot_general(db_ref[...], wu_ref[...], (NT, ((), ())), preferred_element_type=F32))

        @pl.when(j == N_DEV - 1)
        def _():
            o_ref[...] = acc[...]

    hid = pl.BlockSpec((None, tm, fs), lambda m, j: (j, m, 0))
    wspec = pl.BlockSpec((None, None, d, fs), lambda m, j: (j, layer, 0, 0))
    return _hosted(
        body, name=name, grid=(s // tm, N_DEV),
        in_specs=[hid, hid, wspec, wspec],
        out_specs=[pl.BlockSpec((tm, d), lambda m, j: (m, 0))],
        out_shape=[jax.ShapeDtypeStruct((s, d), F32)],
        scratch_shapes=[pltpu.VMEM((tm, d), F32)],
        args=(da, db, wg, wu), carry=carry)


def _ffn_bwd_wgu(h, da, db, name):
    s, d = h.shape
    fs = da.shape[-1]
    tk = _tile(s, 1024)
    nk = s // tk

    def body(h_ref, da_ref, db_ref, og_ref, ou_ref, accg, accu):
        @pl.when(pl.program_id(1) == 0)
        def _():
            accg[...] = jnp.zeros_like(accg)
            accu[...] = jnp.zeros_like(accu)

        hv = h_ref[...]
        accg[...] += lax.dot_general(hv, da_ref[...], (TN, ((), ())), preferred_element_type=F32)
        accu[...] += lax.dot_general(hv, db_ref[...], (TN, ((), ())), preferred_element_type=F32)

        @pl.when(pl.program_id(1) == nk - 1)
        def _():
            og_ref[...] = accg[...].astype(BF16)
            ou_ref[...] = accu[...].astype(BF16)

    hid = pl.BlockSpec((None, tk, fs), lambda j, k: (j, k, 0))
    ospec = pl.BlockSpec((None, d, fs), lambda j, k: (j, 0, 0))
    return _pcall(
        body, name=name, grid=(N_DEV, nk),
        in_specs=[pl.BlockSpec((tk, d), lambda j, k: (k, 0)), hid, hid],
        out_specs=[ospec, ospec],
        out_shape=[jax.ShapeDtypeStruct((N_DEV, d, fs), BF16)] * 2,
        scratch_shapes=[pltpu.VMEM((d, fs), F32), pltpu.VMEM((d, fs), F32)],
        compiler_params=_params("parallel", "arbitrary"),
    )(h, da, db)


def _mm(a, b, *, ta=False, tb=False, out_dtype=F32, name, tm=1024, tn=1024, tk=2048, residual=None):
    m, kdim = (a.shape[1], a.shape[0]) if ta else a.shape
    n = b.shape[0] if tb else b.shape[1]
    tm, tn, tk = _tile(m, tm), _tile(n, tn), _tile(kdim, tk)
    nk = kdim // tk
    dims = ((0,) if ta else (1,), (1,) if tb else (0,))

    def body(*refs):
        a_ref, b_ref = refs[:2]
        acc = refs[-1]
        kk = pl.program_id(2)

        @pl.when(kk == 0)
        def _():
            acc[...] = jnp.zeros_like(acc)

        acc[...] += lax.dot_general(a_ref[...].astype(BF16), b_ref[...].astype(BF16), (dims, ((), ())),
                                    preferred_element_type=F32)

        @pl.when(kk == nk - 1)
        def _():
            if residual is None:
                refs[2][...] = acc[...].astype(out_dtype)
            else:
                res_ref, gate_ref, y_ref, xo_ref = refs[2:6]
                yv = acc[...]
                y_ref[...] = yv
                xo_ref[...] = res_ref[...] + gate_ref[...] * yv

    a_spec = pl.BlockSpec((tk, tm), lambda i, j, k: (k, i)) if ta else pl.BlockSpec((tm, tk), lambda i, j, k: (i, k))
    b_spec = pl.BlockSpec((tn, tk), lambda i, j, k: (j, k)) if tb else pl.BlockSpec((tk, tn), lambda i, j, k: (k, j))
    o_spec = pl.BlockSpec((tm, tn), lambda i, j, k: (i, j))
    if residual is None:
        in_specs, out_specs = [a_spec, b_spec], o_spec
        out_shape = jax.ShapeDtypeStruct((m, n), out_dtype)
        args = (a, b)
    else:
        in_specs = [a_spec, b_spec, o_spec, pl.BlockSpec((1, tn), lambda i, j, k: (0, j))]
        out_specs = [o_spec, o_spec]
        out_shape = [jax.ShapeDtypeStruct((m, n), F32)] * 2
        args = (a, b) + tuple(residual)
    return _pcall(
        body, name=name, grid=(m // tm, n // tn, nk), in_specs=in_specs, out_specs=out_specs, out_shape=out_shape,
        scratch_shapes=[pltpu.VMEM((tm, tn), F32)],
        compiler_params=_params("parallel", "parallel", "arbitrary"),
    )(*args)


def _log_sigmoid(z):
    return jnp.minimum(z, 0.0) - jnp.log(1.0 + jnp.exp(-jnp.abs(z)))


def _fox_gate(proj, small_blk, bias_lane):
    s = proj.shape[0]
    ts = _tile(s, 1024)
    nsub = ts // LANES

    def body(z_ref, b_ref, cum_ref, carry):
        @pl.when(pl.program_id(0) == 0)
        def _():
            carry[...] = jnp.zeros_like(carry)

        ii = lax.broadcasted_iota(jnp.int32, (LANES, LANES), 0)
        jj = lax.broadcasted_iota(jnp.int32, (LANES, LANES), 1)
        tri = (ii >= jj).astype(BF16)
        logf = _log_sigmoid(z_ref[...] + b_ref[...])
        cv = carry[...]
        for sb in range(nsub):
            blk = logf[sb * LANES:(sb + 1) * LANES, :]
            cum_ref[sb * LANES:(sb + 1) * LANES, :] = _dot_exact_lhs(tri, blk) + cv
            cv = cv + jnp.sum(blk, axis=0, keepdims=True)
        carry[...] = cv

    return _pcall(
        body, name="fox_gate", grid=(s // ts,),
        in_specs=[pl.BlockSpec((ts, LANES), lambda i: (i, small_blk)), pl.BlockSpec((1, LANES), lambda i: (0, 0))],
        out_specs=pl.BlockSpec((ts, LANES), lambda i: (i, 0)),
        out_shape=jax.ShapeDtypeStruct((s, LANES), F32),
        scratch_shapes=[pltpu.VMEM((1, LANES), F32)],
        compiler_params=_params("arbitrary"),
    )(proj, bias_lane)


def _fox_gate_bwd(dcum_q, dcum_k, proj, small_blk, bias_lane):
    s = proj.shape[0]
    ts = _tile(s, 1024)
    nsub = ts // LANES
    nb = s // ts

    def body(dcq_ref, dc_ref, z_ref, b_ref, dz_ref, db_ref, carry):
        @pl.when(pl.program_id(0) == 0)
        def _():
            carry[...] = jnp.zeros_like(carry)
            db_ref[...] = jnp.zeros_like(db_ref)

        ii = lax.broadcasted_iota(jnp.int32, (LANES, LANES), 0)
        jj = lax.broadcasted_iota(jnp.int32, (LANES, LANES), 1)
        triu = (jj >= ii).astype(BF16)
        dc = dcq_ref[...] + dc_ref[...]
        zb = z_ref[...] + b_ref[...]
        cv = carry[...]
        dbv = jnp.zeros((1, LANES), F32)
        for sb in reversed(range(nsub)):
            rows = slice(sb * LANES, (sb + 1) * LANES)
            blk = dc[rows, :]
            dlogf = _dot_exact_lhs(triu, blk) + cv
            cv = cv + jnp.sum(blk, axis=0, keepdims=True)
            dz = dlogf * _sigmoid(-zb[rows, :])
            dz_ref[rows, :] = dz
            dbv = dbv + jnp.sum(dz, axis=0, keepdims=True)
        carry[...] = cv
        db_ref[...] += dbv

    row = pl.BlockSpec((1, LANES), lambda i: (0, 0))
    return _pcall(
        body, name="fox_gate_bwd", grid=(nb,),
        in_specs=[pl.BlockSpec((ts, LANES), lambda i: (nb - 1 - i, 0)),
                  pl.BlockSpec((ts, LANES), lambda i: (nb - 1 - i, 0)),
                  pl.BlockSpec((ts, LANES), lambda i: (nb - 1 - i, small_blk)), row],
        out_specs=[pl.BlockSpec((ts, LANES), lambda i: (nb - 1 - i, 0)), row],
        out_shape=[jax.ShapeDtypeStruct((s, LANES), F32), jax.ShapeDtypeStruct((1, LANES), F32)],
        scratch_shapes=[pltpu.VMEM((1, LANES), F32)],
        compiler_params=_params("arbitrary"),
    )(dcum_q, dcum_k, proj, bias_lane)


def _tri_tables(n, by_key):
    if by_key:
        pairs = [(i, j) for j in range(n) for i in range(j, n)]
    else:
        pairs = [(i, j) for i in range(n) for j in range(i + 1)]
    return (jnp.asarray(np.array([p[0] for p in pairs], np.int32)),
            jnp.asarray(np.array([p[1] for p in pairs], np.int32)))


def _fox_group(heads):
    return FOX_HEADS_PER_STEP if heads % FOX_HEADS_PER_STEP == 0 else 1


def _as_row(col):
    t = col.shape[0]
    eye = lax.broadcasted_iota(jnp.int32, (t, t), 0) == lax.broadcasted_iota(jnp.int32, (t, t), 1)
    return jnp.sum(jnp.where(eye, col, 0.0), axis=0, keepdims=True)


def _fox_scores(a, b, bias_col, bias_row, scale, diagonal, rows_are_keys=False):
    sc = lax.dot_general(a.astype(BF16), b.astype(BF16), (NT, ((), ())), preferred_element_type=F32) * scale
    sc = sc + (bias_col + bias_row)
    if not diagonal:
        return sc
    row = lax.broadcasted_iota(jnp.int32, sc.shape, 0)
    col = lax.broadcasted_iota(jnp.int32, sc.shape, 1)
    return jnp.where(row <= col if rows_are_keys else col <= row, sc, NEG)


def _fox_fwd(proj, cum_col, cum_row, w_norm, heads, carry=None):
    s = proj.shape[0]
    t = _tile(s, 512)
    grp = _fox_group(heads)
    qi, ki = _tri_tables(s // t, False)
    scale = 1.0 / math.sqrt(HEAD_DIM)

    def body(qi_ref, ki_ref, q_ref, k_ref, v_ref, cq_ref, ck_ref, w_ref, o_ref, lse_ref, lser_ref, on_ref, m_s, acc_s):
        iq, ik = qi_ref[pl.program_id(1)], ki_ref[pl.program_id(1)]

        @pl.when(ik == 0)
        def _():
            m_s[...] = jnp.full_like(m_s, NEG)
            acc_s[...] = jnp.zeros_like(acc_s)

        def step(diagonal):
            for g in range(grp):
                sl = slice(g * HEAD_DIM, (g + 1) * HEAD_DIM)
                sc = _fox_scores(q_ref[:, sl], k_ref[:, sl], cq_ref[g, :, 0:1], -ck_ref[g], scale, diagonal)
                m_prev = m_s[g]
                m_new = jnp.maximum(m_prev, jnp.max(sc, axis=1, keepdims=True))
                p = jnp.exp(sc - m_new).astype(BF16)
                v_ones = jnp.concatenate([v_ref[:, sl].astype(BF16), jnp.ones((t, LANES), BF16)], axis=1)
                acc_s[g] = jnp.exp(m_prev - m_new) * acc_s[g] + jnp.dot(p, v_ones, preferred_element_type=F32)
                m_s[g] = m_new

        @pl.when(ik < iq)
        def _():
            step(False)

        @pl.when(ik == iq)
        def _():
            step(True)
            for g in range(grp):
                sl = slice(g * HEAD_DIM, (g + 1) * HEAD_DIM)
                acc = acc_s[g]
                o = acc[:, :HEAD_DIM] / acc[:, HEAD_DIM:]
                lse = m_s[g] + jnp.log(acc[:, HEAD_DIM:])
                o_ref[:, sl] = o
                lse_ref[g] = lse
                lser_ref[g] = _as_row(lse[:, 0:1])
                r = lax.rsqrt(jnp.mean(o * o, axis=1, keepdims=True) + EPS)
                on_ref[:, sl] = (o * r * w_ref[...]).astype(BF16)

    ng = heads // grp
    qblk = pl.BlockSpec((t, grp * HEAD_DIM), lambda h, p, qi, ki: (qi[p], h))
    kblk = lambda off: pl.BlockSpec((t, grp * HEAD_DIM), lambda h, p, qi, ki: (ki[p], off + h))
    qcol = pl.BlockSpec((grp, t, LANES), lambda h, p, qi, ki: (h, qi[p], 0))
    return _hosted(
        body, name="fox_fwd", grid=(ng, int(qi.shape[0])), prefetch=(qi, ki),
        in_specs=[qblk, kblk(ng), kblk(2 * ng), qcol,
                  pl.BlockSpec((grp, 1, t), lambda h, p, qi, ki: (h, 0, ki[p])),
                  pl.BlockSpec((1, HEAD_DIM), lambda h, p, qi, ki: (0, 0))],
        out_specs=[qblk, qcol, pl.BlockSpec((grp, 1, t), lambda h, p, qi, ki: (h, 0, qi[p])), qblk],
        scratch_shapes=[pltpu.VMEM((grp, t, 1), F32), pltpu.VMEM((grp, t, 2 * HEAD_DIM), F32)],
        out_shape=[jax.ShapeDtypeStruct((s, heads * HEAD_DIM), F32), jax.ShapeDtypeStruct((heads, s, LANES), F32),
                   jax.ShapeDtypeStruct((heads, 1, s), F32), jax.ShapeDtypeStruct((s, heads * HEAD_DIM), BF16)],
        args=(proj, proj, proj, cum_col, cum_row, w_norm), carry=carry)


def _fox_prep_bwd(do_cat, o_raw, w_norm, heads):
    s = o_raw.shape[0]
    ts = _tile(s, 512)

    def body(g_ref, o_ref, w_ref, do_ref, delta_ref, deltar_ref, dw_ref):
        @pl.when((pl.program_id(0) == 0) & (pl.program_id(1) == 0))
        def _():
            dw_ref[...] = jnp.zeros_like(dw_ref)

        o = o_ref[...]
        g = g_ref[...]
        r = lax.rsqrt(jnp.mean(o * o, axis=1, keepdims=True) + EPS)
        wg = g * w_ref[...]
        do = r * wg - o * (r * r * r) * jnp.mean(wg * o, axis=1, keepdims=True)
        do_ref[...] = do.astype(BF16)
        delta = jnp.sum(do * o, axis=1, keepdims=True)
        delta_ref[...] = jnp.broadcast_to(delta, delta_ref.shape)
        deltar_ref[...] = _as_row(delta)
        dw_ref[...] += jnp.sum(g * o * r, axis=0, keepdims=True)

    blk = pl.BlockSpec((ts, HEAD_DIM), lambda h, i: (i, h))
    row = pl.BlockSpec((1, HEAD_DIM), lambda h, i: (0, 0))
    return _pcall(
        body, name="fox_prep_bwd", grid=(heads, s // ts),
        in_specs=[blk, blk, row],
        out_specs=[blk, pl.BlockSpec((None, ts, LANES), lambda h, i: (h, i, 0)),
                   pl.BlockSpec((None, 1, ts), lambda h, i: (h, 0, i)), row],
        out_shape=[jax.ShapeDtypeStruct((s, heads * HEAD_DIM), BF16), jax.ShapeDtypeStruct((heads, s, LANES), F32),
                   jax.ShapeDtypeStruct((heads, 1, s), F32), jax.ShapeDtypeStruct((1, HEAD_DIM), F32)],
        compiler_params=_params("arbitrary", "arbitrary"),
    )(do_cat, o_raw, w_norm)


def _fox_dq(proj, do, cum_col, cum_row, lse, delta, heads, carry=None):
    s = proj.shape[0]
    t = _tile(s, 512)
    grp = _fox_group(heads)
    qi, ki = _tri_tables(s // t, False)
    scale = 1.0 / math.sqrt(HEAD_DIM)

    def body(qi_ref, ki_ref, q_ref, k_ref, v_ref, do_ref, cq_ref, ck_ref, lse_ref, dl_ref, dq_ref, dc_ref, acc, dc_acc):
        iq, ik = qi_ref[pl.program_id(1)], ki_ref[pl.program_id(1)]

        @pl.when(ik == 0)
        def _():
            acc[...] = jnp.zeros_like(acc)
            dc_acc[...] = jnp.zeros_like(dc_acc)

        def step(diagonal):
            for g in range(grp):
                sl = slice(g * HEAD_DIM, (g + 1) * HEAD_DIM)
                kv = k_ref[:, sl]
                sc = _fox_scores(q_ref[:, sl], kv, cq_ref[g, :, 0:1] - lse_ref[g, :, 0:1], -ck_ref[g], scale, diagonal)
                p = jnp.exp(sc)
                dp = _dot(do_ref[:, sl], v_ref[:, sl], NT)
                ds = p * (dp - dl_ref[g, :, 0:1])
                acc[g] += _dot(ds, kv, NN)
                dc_acc[g] += jnp.sum(ds, axis=1, keepdims=True)

        @pl.when(ik < iq)
        def _():
            step(False)

        @pl.when(ik == iq)
        def _():
            step(True)
            for g in range(grp):
                dq_ref[:, g * HEAD_DIM:(g + 1) * HEAD_DIM] = (acc[g] * scale).astype(BF16)
                dc_ref[g] = _as_row(dc_acc[g])

    ng = heads // grp
    qblk = pl.BlockSpec((t, grp * HEAD_DIM), lambda h, p, qi, ki: (qi[p], h))
    kblk = lambda off: pl.BlockSpec((t, grp * HEAD_DIM), lambda h, p, qi, ki: (ki[p], off + h))
    qcol = pl.BlockSpec((grp, t, LANES), lambda h, p, qi, ki: (h, qi[p], 0))
    return _hosted(
        body, name="fox_dq", grid=(ng, int(qi.shape[0])), prefetch=(qi, ki),
        in_specs=[qblk, kblk(ng), kblk(2 * ng), qblk, qcol,
                  pl.BlockSpec((grp, 1, t), lambda h, p, qi, ki: (h, 0, ki[p])), qcol, qcol],
        out_specs=[qblk, pl.BlockSpec((grp, 1, t), lambda h, p, qi, ki: (h, 0, qi[p]))],
        scratch_shapes=[pltpu.VMEM((grp, t, HEAD_DIM), F32), pltpu.VMEM((grp, t, 1), F32)],
        out_shape=[jax.ShapeDtypeStruct((s, heads * HEAD_DIM), BF16), jax.ShapeDtypeStruct((heads, 1, s), F32)],
        args=(proj, proj, proj, do, cum_col, cum_row, lse, delta), carry=carry)


def _fox_dkv(proj, do, cum_col, cum_row, lse_row, delta_row, heads, carry=None):
    s = proj.shape[0]
    t = _tile(s, 512)
    nk = s // t
    grp = _fox_group(heads)
    qi, ki = _tri_tables(nk, True)
    scale = 1.0 / math.sqrt(HEAD_DIM)

    def body(qi_ref, ki_ref, q_ref, k_ref, v_ref, do_ref, cqr_ref, ckc_ref, lse_ref, dl_ref, dk_ref, dv_ref, dc_ref,
             dk_acc, dv_acc, dc_acc):
        iq, ik = qi_ref[pl.program_id(1)], ki_ref[pl.program_id(1)]

        def step(diagonal):
            for g in range(grp):
                sl = slice(g * HEAD_DIM, (g + 1) * HEAD_DIM)
                qv = q_ref[:, sl]
                dov = do_ref[:, sl]
                st = _fox_scores(k_ref[:, sl], qv, -ckc_ref[g, :, 0:1], cqr_ref[g] - lse_ref[g], scale, diagonal, True)
                pt = jnp.exp(st)
                dv_acc[g] += _dot(pt, dov, NN)
                dpt = _dot(v_ref[:, sl], dov, NT)
                dst = pt * (dpt - dl_ref[g])
                dk_acc[g] += _dot(dst, qv, NN)
                dc_acc[g] += jnp.sum(dst, axis=1, keepdims=True)

        @pl.when(iq == ik)
        def _():
            dk_acc[...] = jnp.zeros_like(dk_acc)
            dv_acc[...] = jnp.zeros_like(dv_acc)
            dc_acc[...] = jnp.zeros_like(dc_acc)
            step(True)

        @pl.when(iq > ik)
        def _():
            step(False)

        @pl.when(iq == nk - 1)
        def _():
            for g in range(grp):
                sl = slice(g * HEAD_DIM, (g + 1) * HEAD_DIM)
                dk_ref[:, sl] = (dk_acc[g] * scale).astype(BF16)
                dv_ref[:, sl] = dv_acc[g].astype(BF16)
                dc_ref[g] = _as_row(-dc_acc[g])

    ng = heads // grp
    qblk = pl.BlockSpec((t, grp * HEAD_DIM), lambda h, p, qi, ki: (qi[p], h))
    qrow = pl.BlockSpec((grp, 1, t), lambda h, p, qi, ki: (h, 0, qi[p]))
    kblk = lambda off: pl.BlockSpec((t, grp * HEAD_DIM), lambda h, p, qi, ki: (ki[p], off + h))
    kout = pl.BlockSpec((t, grp * HEAD_DIM), lambda h, p, qi, ki: (ki[p], h))
    return _hosted(
        body, name="fox_dkv", grid=(ng, int(qi.shape[0])), prefetch=(qi, ki),
        in_specs=[qblk, kblk(ng), kblk(2 * ng), qblk, qrow,
                  pl.BlockSpec((grp, t, LANES), lambda h, p, qi, ki: (h, ki[p], 0)), qrow, qrow],
        out_specs=[kout, kout, pl.BlockSpec((grp, 1, t), lambda h, p, qi, ki: (h, 0, ki[p]))],
        scratch_shapes=[pltpu.VMEM((grp, t, HEAD_DIM), F32), pltpu.VMEM((grp, t, HEAD_DIM), F32),
                        pltpu.VMEM((grp, t, 1), F32)],
        out_shape=[jax.ShapeDtypeStruct((s, heads * HEAD_DIM), BF16)] * 2 + [jax.ShapeDtypeStruct((heads, 1, s), F32)],
        args=(proj, proj, proj, do, cum_row, cum_col, lse_row, delta_row), carry=carry)


def _shift_rows(xv, halo, j, forward):
    n = xv.shape[0]
    rid = lax.broadcasted_iota(jnp.int32, (8, xv.shape[1]), 0)
    if forward:
        xs = pltpu.roll(xv, n - j, 0)
        hs = pltpu.roll(halo, 8 - j, 0)
        edge = jnp.where(rid >= 8 - j, hs, xs[n - 8:, :])
        return jnp.concatenate([xs[:n - 8, :], edge], axis=0)
    xs = pltpu.roll(xv, j, 0)
    hs = pltpu.roll(halo, j, 0)
    edge = jnp.where(rid < j, hs, xs[:8, :])
    return jnp.concatenate([edge, xs[8:, :]], axis=0)


def _conv_silu(xv, halo, w):
    xc = w[CONV_W - 1:CONV_W, :] * xv
    for j in range(1, CONV_W):
        xc = xc + w[CONV_W - 1 - j:CONV_W - j, :] * _shift_rows(xv, halo, j, False)
    return xc, xc * _sigmoid(xc)


def _gdn_pre(proj, conv_w, heads):
    s = proj.shape[0]
    cw = 3 * heads * HEAD_DIM
    ts = _tile(s, 256)
    tb = ts // 8

    def body(x_ref, halo_ref, w_ref, q_ref, k_ref, v_ref):
        halo = jnp.where(pl.program_id(0) == 0, 0.0, halo_ref[...])
        _, y = _conv_silu(x_ref[...], halo, w_ref[...])
        for h in range(heads):
            for part, ref in enumerate((q_ref, k_ref, v_ref)):
                c0 = (part * heads + h) * HEAD_DIM
                blk = y[:, c0:c0 + HEAD_DIM]
                if part < 2:
                    blk = blk * lax.rsqrt(jnp.sum(blk * blk, axis=1, keepdims=True) + EPS)
                ref[h] = blk

    out = pl.BlockSpec((heads, ts, HEAD_DIM), lambda i: (0, i, 0))
    return _pcall(
        body, name="gdn_pre", grid=(s // ts,),
        in_specs=[pl.BlockSpec((ts, cw), lambda i: (i, 1)),
                  pl.BlockSpec((8, cw), lambda i: (jnp.maximum(i * tb - 1, 0), 1)),
                  pl.BlockSpec((CONV_W, cw), lambda i: (0, 0))],
        out_specs=[out, out, out],
        out_shape=[jax.ShapeDtypeStruct((heads, s, HEAD_DIM), F32)] * 3,
        compiler_params=_params("parallel"),
    )(proj, proj, conv_w)


def _gdn_pre_bwd_act(proj, conv_w, dq, dk, dv, heads):
    s = proj.shape[0]
    cw = 3 * heads * HEAD_DIM
    ts = _tile(s, 256)
    tb = ts // 8

    def body(x_ref, halo_ref, w_ref, dq_ref, dk_ref, dv_ref, dxc_ref, dw_ref):
        @pl.when(pl.program_id(0) == 0)
        def _():
            dw_ref[...] = jnp.zeros_like(dw_ref)

        xv = x_ref[...]
        halo = jnp.where(pl.program_id(0) == 0, 0.0, halo_ref[...])
        xc, y = _conv_silu(xv, halo, w_ref[...])
        sg = _sigmoid(xc)
        dsilu = sg * (1.0 + xc * (1.0 - sg))
        for h in range(heads):
            for part, ref in enumerate((dq_ref, dk_ref, dv_ref)):
                c0 = (part * heads + h) * HEAD_DIM
                g = ref[h]
                if part < 2:
                    blk = y[:, c0:c0 + HEAD_DIM]
                    r = lax.rsqrt(jnp.sum(blk * blk, axis=1, keepdims=True) + EPS)
                    g = r * g - blk * (r * r * r) * jnp.sum(g * blk, axis=1, keepdims=True)
                dxc_ref[:, c0:c0 + HEAD_DIM] = g * dsilu[:, c0:c0 + HEAD_DIM]
        dxc = dxc_ref[...]
        rows = [jnp.sum(dxc * (xv if j == 0 else _shift_rows(xv, halo, j, False)), axis=0, keepdims=True)
                for j in range(CONV_W)]
        dw_ref[...] += jnp.concatenate([rows[CONV_W - 1 - k] for k in range(CONV_W)]
                                       + [jnp.zeros((8 - CONV_W, cw), F32)], axis=0)

    hblk = pl.BlockSpec((heads, ts, HEAD_DIM), lambda i: (0, i, 0))
    return _pcall(
        body, name="gdn_pre_bwd_act", grid=(s // ts,),
        in_specs=[pl.BlockSpec((ts, cw), lambda i: (i, 1)),
                  pl.BlockSpec((8, cw), lambda i: (jnp.maximum(i * tb - 1, 0), 1)),
                  pl.BlockSpec((CONV_W, cw), lambda i: (0, 0)), hblk, hblk, hblk],
        out_specs=[pl.BlockSpec((ts, cw), lambda i: (i, 0)), pl.BlockSpec((8, cw), lambda i: (0, 0))],
        out_shape=[jax.ShapeDtypeStruct((s, cw), F32), jax.ShapeDtypeStruct((8, cw), F32)],
        compiler_params=_params("arbitrary"),
    )(proj, proj, conv_w, dq, dk, dv)


def _gdn_pre_bwd_conv(dxc, conv_w):
    s, cw = dxc.shape
    ts = _tile(s, 256)
    tb = ts // 8
    last = s // 8 - 1

    def body(g_ref, halo_ref, w_ref, dx_ref):
        gv = g_ref[...]
        w = w_ref[...]
        halo = jnp.where(pl.program_id(0) == s // ts - 1, 0.0, halo_ref[...])
        dx = w[CONV_W - 1:CONV_W, :] * gv
        for j in range(1, CONV_W):
            dx = dx + w[CONV_W - 1 - j:CONV_W - j, :] * _shift_rows(gv, halo, j, True)
        dx_ref[...] = dx.astype(BF16)

    return _pcall(
        body, name="gdn_pre_bwd_conv", grid=(s // ts,),
        in_specs=[pl.BlockSpec((ts, cw), lambda i: (i, 0)),
                  pl.BlockSpec((8, cw), lambda i: (jnp.minimum((i + 1) * tb, last), 0)),
                  pl.BlockSpec((CONV_W, cw), lambda i: (0, 0))],
        out_specs=pl.BlockSpec((ts, cw), lambda i: (i, 0)),
        out_shape=jax.ShapeDtypeStruct((s, cw), BF16),
        compiler_params=_params("parallel"),
    )(dxc, dxc, conv_w)


def _bdot(a, b, ca, cb):
    return lax.dot_general(a.astype(BF16), b.astype(BF16), (((ca,), (cb,)), ((0,), (0,))),
                           preferred_element_type=F32)


def _bdot_hp(a, b, ca, cb):
    ah = a.astype(BF16)
    al = (a - ah.astype(F32)).astype(BF16)
    bh = b.astype(BF16)
    bl = (b - bh.astype(F32)).astype(BF16)
    d = lambda p, q: lax.dot_general(p, q, (((ca,), (cb,)), ((0,), (0,))), preferred_element_type=F32)
    return d(ah, bh) + (d(ah, bl) + d(al, bh))


def _gdn_gates(small, a_lane, dt_lane, heads):
    lane = lax.broadcasted_iota(jnp.int32, small.shape, 1)
    za = small + dt_lane
    g_all = -jnp.exp(a_lane) * (jnp.maximum(za, 0.0) + jnp.log(1.0 + jnp.exp(-jnp.abs(za))))
    b_all = _sigmoid(small)
    pick = lambda v, l: jnp.sum(jnp.where(lane == l, v, 0.0), axis=1, keepdims=True)
    g = jnp.stack([pick(g_all, heads + h) for h in range(heads)], axis=0)
    beta = jnp.stack([pick(b_all, 2 * heads + h) for h in range(heads)], axis=0)
    return g, beta


def _chunk_masks(c):
    ii = lax.broadcasted_iota(jnp.int32, (1, c, c), 1)
    jj = lax.broadcasted_iota(jnp.int32, (1, c, c), 2)
    return ii >= jj, ii > jj, ii == jj


def _col_to_row(col, eye):
    return jnp.sum(jnp.where(eye, col, 0.0), axis=1, keepdims=True)


def _row_to_col(row, eye):
    return jnp.sum(jnp.where(eye, row, 0.0), axis=2, keepdims=True)


def _gdn_chunk(q, k, v, g, beta, state):
    c = q.shape[1]
    incl, strict, eye = _chunk_masks(c)
    g_row = _col_to_row(g, eye)
    gc_col = jnp.sum(jnp.where(incl, g_row, 0.0), axis=2, keepdims=True)
    gc_row = _col_to_row(gc_col, eye)
    gam = jnp.where(incl, jnp.exp(jnp.where(incl, gc_col - gc_row, NEG)), 0.0)
    egc = jnp.exp(gc_col)
    kb = k * beta
    vb = v * beta
    kbe = kb * egc
    low = jnp.where(strict, _bdot(kb, k, 2, 2), 0.0) * gam
    p = -low
    tinv = jnp.where(eye, 1.0, 0.0) + p
    width = 2
    while width < c:
        p = _bdot_hp(p, p, 2, 1)
        tinv = tinv + _bdot_hp(tinv, p, 2, 1)
        width *= 2
    u = _bdot(tinv, vb, 2, 1)
    w = _bdot(tinv, kbe, 2, 1)
    att = jnp.where(incl, _bdot(q, k, 2, 2), 0.0) * gam
    vn = u - _bdot(w, state, 2, 1)
    qe = q * egc
    o = _bdot(qe, state, 2, 1) + _bdot(att, vn, 2, 1)
    gl = jnp.sum(g, axis=1, keepdims=True)
    edec = jnp.exp(gl - gc_col)
    kdec = k * edec
    egl = jnp.exp(gl)
    new_state = state * egl + _bdot(kdec, vn, 1, 1)
    return dict(incl=incl, strict=strict, eye=eye, gam=gam, egc=egc, kb=kb, vb=vb, kbe=kbe, low=low, tinv=tinv, w=w,
                att=att, vn=vn, qe=qe, o=o, edec=edec, kdec=kdec, egl=egl, new_state=new_state)


def _gdn_load(q_ref, k_ref, v_ref, small_ref, a_ref, dt_ref, rows, heads):
    q = q_ref[:, rows, :] * (HEAD_DIM ** -0.5)
    g, beta = _gdn_gates(small_ref[rows, :], a_ref[...], dt_ref[...], heads)
    return q, k_ref[:, rows, :], v_ref[:, rows, :], g, beta


def _gdn_fwd(q, k, v, proj, z_blk, small_blk, a_lane, dt_lane, w_norm):
    heads, s, _ = q.shape
    c = min(GDN_CHUNK, s)
    r = _tile(s, 512)
    npb = r // c
    gw = heads * HEAD_DIM

    def body(q_ref, k_ref, v_ref, z_ref, small_ref, a_ref, dt_ref, w_ref, o_ref, st_ref, state):
        @pl.when(pl.program_id(0) == 0)
        def _():
            state[...] = jnp.zeros_like(state)

        def chunk(cb, carry):
            rows = pl.ds(pl.multiple_of(cb * c, c), c)
            qv, kv, vv, g, beta = _gdn_load(q_ref, k_ref, v_ref, small_ref, a_ref, dt_ref, rows, heads)
            st = state[...]
            st_ref[:, cb] = st
            res = _gdn_chunk(qv, kv, vv, g, beta, st)
            state[...] = res["new_state"]
            o = res["o"]
            rn = lax.rsqrt(jnp.mean(o * o, axis=2, keepdims=True) + EPS)
            zv = z_ref[rows, :]
            for h in range(heads):
                zh = zv[:, h * HEAD_DIM:(h + 1) * HEAD_DIM]
                o_ref[rows, h * HEAD_DIM:(h + 1) * HEAD_DIM] = (
                    o[h] * rn[h] * w_ref[...] * (zh * _sigmoid(zh))).astype(BF16)
            return carry

        lax.fori_loop(0, npb, chunk, 0)

    hblk = pl.BlockSpec((heads, r, HEAD_DIM), lambda i: (0, i, 0))
    row = pl.BlockSpec((1, LANES), lambda i: (0, 0))
    return _pcall(
        body, name="gdn_fwd", grid=(s // r,),
        in_specs=[hblk, hblk, hblk, pl.BlockSpec((r, gw), lambda i: (i, z_blk)),
                  pl.BlockSpec((r, LANES), lambda i: (i, small_blk)), row, row, row],
        out_specs=[pl.BlockSpec((r, gw), lambda i: (i, 0)),
                   pl.BlockSpec((heads, npb, HEAD_DIM, HEAD_DIM), lambda i: (0, i, 0, 0))],
        out_shape=[jax.ShapeDtypeStruct((s, gw), BF16),
                   jax.ShapeDtypeStruct((heads, s // c, HEAD_DIM, HEAD_DIM), F32)],
        scratch_shapes=[pltpu.VMEM((heads, HEAD_DIM, HEAD_DIM), F32)],
        compiler_params=_params("arbitrary"),
    )(q, k, v, proj, proj, a_lane, dt_lane, w_norm)


def _gdn_bwd(q, k, v, proj, z_blk, small_blk, a_lane, dt_lane, w_norm, states, do_cat, do_blk):
    heads, s, _ = q.shape
    c = min(GDN_CHUNK, s)
    r = _tile(s, 512)
    npb = r // c
    nb = s // r
    gw = heads * HEAD_DIM

    def body(q_ref, k_ref, v_ref, z_ref, small_ref, a_ref, dt_ref, w_ref, st_ref, do_ref,
             dq_ref, dk_ref, dv_ref, dz_ref, dsm_ref, da_ref, ddt_ref, dw_ref, dstate):
        @pl.when(pl.program_id(0) == 0)
        def _():
            dstate[...] = jnp.zeros_like(dstate)
            da_ref[...] = jnp.zeros_like(da_ref)
            ddt_ref[...] = jnp.zeros_like(ddt_ref)
            dw_ref[...] = jnp.zeros_like(dw_ref)

        def chunk(it, carry):
            cb = npb - 1 - it
            rows = pl.ds(pl.multiple_of(cb * c, c), c)
            qv, kv, vv, g, beta = _gdn_load(q_ref, k_ref, v_ref, small_ref, a_ref, dt_ref, rows, heads)
            st = st_ref[:, cb]
            f = _gdn_chunk(qv, kv, vv, g, beta, st)
            incl, strict, eye = f["incl"], f["strict"], f["eye"]
            o = f["o"]
            wv = w_ref[...]
            zv = z_ref[rows, :]
            dov = do_ref[rows, :]
            rn = lax.rsqrt(jnp.mean(o * o, axis=2, keepdims=True) + EPS)
            do_l, dw_acc = [], jnp.zeros((1, HEAD_DIM), F32)
            for h in range(heads):
                sl = slice(h * HEAD_DIM, (h + 1) * HEAD_DIM)
                zh, gh = zv[:, sl], dov[:, sl]
                sg = _sigmoid(zh)
                on = o[h] * rn[h]
                dz_ref[rows, sl] = (gh * (on * wv) * (sg * (1.0 + zh * (1.0 - sg)))).astype(BF16)
                gn = gh * (zh * sg)
                dw_acc = dw_acc + jnp.sum(gn * on, axis=0, keepdims=True)
                wg = gn * wv
                do_l.append(rn[h] * wg - o[h] * (rn[h] * rn[h] * rn[h]) * jnp.mean(wg * o[h], axis=1, keepdims=True))
            dw_ref[...] += dw_acc
            do = jnp.stack(do_l, axis=0)
            ds_out = dstate[...]
            dvn = _bdot(f["att"], do, 1, 1) + _bdot(f["kdec"], ds_out, 2, 1)
            datt = jnp.where(incl, _bdot(do, f["vn"], 2, 2), 0.0)
            dqe = _bdot(do, st, 2, 2)
            dstate[...] = _bdot(f["qe"], do, 1, 1) + f["egl"] * ds_out - _bdot(f["w"], dvn, 1, 1)
            dw = -_bdot(dvn, st, 2, 2)
            dkdec = _bdot(f["vn"], ds_out, 2, 2)
            t_kdec = jnp.sum(dkdec * f["kdec"], axis=2, keepdims=True)
            dgl = (jnp.sum(jnp.sum(st * ds_out, axis=2, keepdims=True), axis=1, keepdims=True) * f["egl"]
                   + jnp.sum(t_kdec, axis=1, keepdims=True))
            dgc = jnp.sum(dqe * f["qe"], axis=2, keepdims=True) - t_kdec
            dq = dqe * f["egc"]
            dk = dkdec * f["edec"]
            dtinv = _bdot(dvn, f["vb"], 2, 2) + _bdot(dw, f["kbe"], 2, 2)
            dvb = _bdot(f["tinv"], dvn, 1, 1)
            dkbe = _bdot(f["tinv"], dw, 1, 1)
            dkb = dkbe * f["egc"]
            dgc = dgc + jnp.sum(dkbe * f["kbe"], axis=2, keepdims=True)
            dlow = jnp.where(strict, -_bdot_hp(_bdot_hp(f["tinv"], dtinv, 1, 1), f["tinv"], 2, 2), 0.0)
            ml = dlow * f["gam"]
            dkb = dkb + _bdot(ml, kv, 2, 1)
            dk = dk + _bdot(ml, f["kb"], 1, 1)
            ma = datt * f["gam"]
            dq = dq + _bdot(ma, kv, 2, 1)
            dk = dk + _bdot(ma, qv, 1, 1)
            e = dlow * f["low"] + datt * f["att"]
            dgc = dgc + jnp.sum(e, axis=2, keepdims=True) - _row_to_col(jnp.sum(e, axis=1, keepdims=True), eye)
            dk = dk + beta * dkb
            dbeta = jnp.sum(dkb * kv, axis=2, keepdims=True) + jnp.sum(dvb * vv, axis=2, keepdims=True)
            dgc_row = _col_to_row(dgc, eye)
            dg = jnp.sum(jnp.where(incl, 0.0, dgc_row) + jnp.where(eye, dgc_row, 0.0), axis=2, keepdims=True) + dgl
            dq_ref[:, rows, :] = dq * (HEAD_DIM ** -0.5)
            dk_ref[:, rows, :] = dk
            dv_ref[:, rows, :] = beta * dvb
            small = small_ref[rows, :]
            lane = lax.broadcasted_iota(jnp.int32, small.shape, 1)
            dg_l = jnp.zeros(small.shape, F32)
            db_l = jnp.zeros(small.shape, F32)
            for h in range(heads):
                dg_l = dg_l + jnp.where(lane == heads + h, dg[h], 0.0)
                db_l = db_l + jnp.where(lane == 2 * heads + h, dbeta[h], 0.0)
            za = small + dt_ref[...]
            nexp = -jnp.exp(a_ref[...])
            softplus = jnp.maximum(za, 0.0) + jnp.log(1.0 + jnp.exp(-jnp.abs(za)))
            da_logit = dg_l * nexp * _sigmoid(za)
            sb = _sigmoid(small)
            dsm_ref[rows, :] = da_logit + db_l * sb * (1.0 - sb)
            ddt_ref[...] += jnp.sum(da_logit, axis=0, keepdims=True)
            da_ref[...] += jnp.sum(dg_l * nexp * softplus, axis=0, keepdims=True)
            return carry

        lax.fori_loop(0, npb, chunk, 0)

    rev = lambda i: nb - 1 - i
    hblk = pl.BlockSpec((heads, r, HEAD_DIM), lambda i: (0, rev(i), 0))
    row = pl.BlockSpec((1, LANES), lambda i: (0, 0))
    wide = lambda blk: pl.BlockSpec((r, gw), lambda i: (rev(i), blk))
    return _pcall(
        body, name="gdn_bwd", grid=(nb,),
        in_specs=[hblk, hblk, hblk, wide(z_blk), pl.BlockSpec((r, LANES), lambda i: (rev(i), small_blk)),
                  row, row, row, pl.BlockSpec((heads, npb, HEAD_DIM, HEAD_DIM), lambda i: (0, rev(i), 0, 0)),
                  wide(do_blk)],
        out_specs=[hblk, hblk, hblk, wide(0), pl.BlockSpec((r, LANES), lambda i: (rev(i), 0)), row, row, row],
        out_shape=[jax.ShapeDtypeStruct((heads, s, HEAD_DIM), F32)] * 3
        + [jax.ShapeDtypeStruct((s, gw), BF16), jax.ShapeDtypeStruct((s, LANES), F32)]
        + [jax.ShapeDtypeStruct((1, LANES), F32)] * 3,
        scratch_shapes=[pltpu.VMEM((heads, HEAD_DIM, HEAD_DIM), F32)],
        compiler_params=_params("arbitrary"),
    )(q, k, v, proj, proj, a_lane, dt_lane, w_norm, states, do_cat)


def _final(x, target, gf):
    s, d = x.shape
    ts = _tile(s, 512)

    def body(x_ref, t_ref, g_ref, loss_ref, dx_ref, dg_ref):
        @pl.when(pl.program_id(0) == 0)
        def _():
            loss_ref[...] = jnp.zeros_like(loss_ref)
            dg_ref[...] = jnp.zeros_like(dg_ref)

        xv = x_ref[...]
        gv = g_ref[...]
        r = lax.rsqrt(jnp.mean(xv * xv, axis=-1, keepdims=True) + EPS)
        xn = xv * r
        err = xn * gv - t_ref[...]
        per_tok = jnp.mean(err * err, axis=-1, keepdims=True)
        loss_ref[...] += 0.5 * jnp.sum(per_tok, axis=0, keepdims=True)
        dy = err * (1.0 / d)
        dg_ref[...] += jnp.sum(dy * xn, axis=0, keepdims=True)
        dxn = dy * gv
        dx_ref[...] = r * (dxn - xn * jnp.mean(dxn * xn, axis=-1, keepdims=True))

    blk = pl.BlockSpec((ts, d), lambda i: (i, 0))
    row = pl.BlockSpec((1, d), lambda i: (0, 0))
    return _pcall(
        body, name="final_loss", grid=(s // ts,),
        in_specs=[blk, blk, row], out_specs=[pl.BlockSpec((1, LANES), lambda i: (0, 0)), blk, row],
        out_shape=[jax.ShapeDtypeStruct((1, LANES), F32), jax.ShapeDtypeStruct((s, d), F32),
                   jax.ShapeDtypeStruct((1, d), F32)],
        compiler_params=_params("arbitrary"),
    )(x, target, gf)


def _adamw(parts, w, m, v, name):
    npart, rows, cols = parts.shape
    tr = _tile(rows, max(8, ADAM_BLOCK_BYTES // (4 * npart * cols)))
    c1 = 1.0 - ADAM_B1 ** ADAM_STEP
    c2 = 1.0 - ADAM_B2 ** ADAM_STEP

    def body(p_ref, w_ref, m_ref, v_ref, g_ref, d_ref, mo_ref, vo_ref):
        g = p_ref[0].astype(F32)
        for i in range(1, npart):
            g = g + p_ref[i].astype(F32)
        mn = ADAM_B1 * m_ref[...] + (1.0 - ADAM_B1) * g
        vn = ADAM_B2 * v_ref[...] + (1.0 - ADAM_B2) * (g * g)
        g_ref[...] = g
        mo_ref[...] = mn
        vo_ref[...] = vn
        d_ref[...] = -ADAM_LR * ((mn / c1) / (jnp.sqrt(vn / c2) + ADAM_EPS) + ADAM_WD * w_ref[...])

    blk = pl.BlockSpec((tr, cols), lambda i: (i, 0))
    return _pcall(
        body, name=name, grid=(rows // tr,),
        in_specs=[pl.BlockSpec((npart, tr, cols), lambda i: (0, i, 0)), blk, blk, blk],
        out_specs=[blk] * 4, out_shape=[jax.ShapeDtypeStruct((rows, cols), F32)] * 4,
        compiler_params=_params("parallel"),
    )(parts, w, m, v)


def _adamw_layers(parts0, parts1, w, m, v, name):
    npart, rows, cols = parts0.shape
    tr = _tile(rows, max(8, ADAM_BLOCK_BYTES // (4 * npart * cols)))
    nb = rows // tr
    c1 = 1.0 - ADAM_B1 ** ADAM_STEP
    c2 = 1.0 - ADAM_B2 ** ADAM_STEP

    def body(p0_ref, p1_ref, w_ref, m_ref, v_ref, g_ref, d_ref, mo_ref, vo_ref):
        def update(p_ref):
            g = p_ref[0].astype(F32)
            for i in range(1, npart):
                g = g + p_ref[i].astype(F32)
            mn = ADAM_B1 * m_ref[...] + (1.0 - ADAM_B1) * g
            vn = ADAM_B2 * v_ref[...] + (1.0 - ADAM_B2) * (g * g)
            g_ref[...] = g
            mo_ref[...] = mn
            vo_ref[...] = vn
            d_ref[...] = -ADAM_LR * ((mn / c1) / (jnp.sqrt(vn / c2) + ADAM_EPS) + ADAM_WD * w_ref[...])

        @pl.when(pl.program_id(0) == 0)
        def _():
            update(p0_ref)

        @pl.when(pl.program_id(0) == 1)
        def _():
            update(p1_ref)

    blk = pl.BlockSpec((None, None, tr, cols), lambda l, i: (0, l, i, 0))
    p0 = pl.BlockSpec((npart, tr, cols), lambda l, i: (0, jnp.where(l == 0, i, nb - 1), 0))
    p1 = pl.BlockSpec((npart, tr, cols), lambda l, i: (0, jnp.where(l == 0, 0, i), 0))
    return _pcall(
        body, name=name, grid=(2, nb), in_specs=[p0, p1, blk, blk, blk], out_specs=[blk] * 4,
        out_shape=[jax.ShapeDtypeStruct(w.shape, F32)] * 4, compiler_params=_params("arbitrary", "arbitrary"),
    )(parts0, parts1, w, m, v)


def _pad_lanes(v, n=LANES, at=0):
    return jnp.pad(v, ((0, 0), (at, n - at - v.shape[1])))


def _my_cols(a, me, width):
    return lax.dynamic_slice_in_dim(a, me * width, width, axis=a.ndim - 1)


def kernel(x, c, ada_w, ada_b, norm_g, ffn_w_gate, ffn_w_up, ffn_w_down, w_in, w_out, fox_f_bias, fox_out_norm, gdn_conv, gdn_A_log, gdn_dt_bias, gdn_out_norm, final_norm, loss_target, m_ada_w, m_ada_b, m_norm_g, m_ffn_w_gate, m_ffn_w_up, m_ffn_w_down, m_w_in, m_w_out, m_fox_f_bias, m_fox_out_norm, m_gdn_conv, m_gdn_A_log, m_gdn_dt_bias, m_gdn_out_norm, m_final_norm, v_ada_w, v_ada_b, v_norm_g, v_ffn_w_gate, v_ffn_w_up, v_ffn_w_down, v_w_in, v_w_out, v_fox_f_bias, v_fox_out_norm, v_gdn_conv, v_gdn_A_log, v_gdn_dt_bias, v_gdn_out_norm, v_final_norm):
    me = _linear(_mesh_pos())
    x0 = x[0]
    s, d = x0.shape
    heads = d // (2 * HEAD_DIM)
    fw = heads * HEAD_DIM
    ng = norm_g.shape[-1]
    ncv = gdn_conv.shape[-1]
    nada = ada_w.shape[-1]
    in_w = w_in.shape[-1] * N_DEV
    in_pad = -(-in_w // 512) * 512

    pack = jnp.concatenate([c, norm_g[0].reshape(1, 3 * ng), gdn_conv[0].reshape(1, CONV_W * ncv)], axis=1)
    pack_all = _gather_row(pack, "gather_small_params")
    c_all = pack_all[:, :d]
    g_all = pack_all[:, d:d + 3 * ng].reshape(N_DEV, 3, ng).transpose(1, 0, 2).reshape(3, d)
    conv_all = pack_all[:, d + 3 * ng:].reshape(N_DEV, CONV_W, ncv).transpose(1, 0, 2).reshape(CONV_W, 3 * fw)

    mod_blk = _ada_fwd(c_all, ada_w[0], _my_cols(ada_b, me, nada))
    mod_all = _exchange([mod_blk], scatter=False, in_vmem=True, name="gather_mod")[0]
    mod = lax.dynamic_slice_in_dim(mod_all, me, 1, axis=1).reshape(N_MOD, d)
    sh1, sc1, gt1, sh2, sc2, gt2, sh3, sc3, gt3 = [mod[i:i + 1] for i in range(N_MOD)]

    wg_sh, wu_sh, wd_sh = [w[0].astype(BF16) for w in (ffn_w_gate, ffn_w_up, ffn_w_down)]
    layer = lambda w, i: w[i:i + 1]
    wg0, wu0 = _gather_via_sibling([layer(wg_sh, 0), layer(wu_sh, 0)], "gather_ffn1_up_weights")
    small_blk = 7 * heads

    bias_lane = _pad_lanes(fox_f_bias)
    a_lane = _pad_lanes(gdn_A_log, at=heads)
    dt_lane = _pad_lanes(gdn_dt_bias, at=heads)

    h1 = _norm_mod(x0, g_all[0:1], sc1, sh1, "norm_mod_1")
    (a1, b1, s1), (wd0, wout_g) = _ffn_up(h1, wg0, wu0, 0, "ffn1_up",
                                          carry=([layer(wd_sh, 0), w_out[0].astype(BF16)], False))
    (f1, x1), (win_g,) = _ffn_down(s1, wd0, 0, x0, gt1, "ffn1_down", carry=([w_in[0].astype(BF16)], False))
    win_full = win_g.transpose(1, 0, 2).reshape(d, in_w)
    o_f, o_qkv, o_a, o_z = 3 * fw, 3 * fw + heads, 6 * fw + heads, 6 * fw + 3 * heads
    win_al = jnp.concatenate(
        [win_full[:, :o_f], win_full[:, o_qkv:o_a], win_full[:, o_z:], win_full[:, o_f:o_qkv],
         win_full[:, o_a:o_z], jnp.zeros((d, in_pad - in_w), BF16)], axis=1)
    wout_full = wout_g.reshape(d, d)

    h2 = _norm_mod(x1, g_all[1:2], sc2, sh2, "norm_mod_2")
    proj = _mm(h2, win_al, name="in_proj", tn=1536)
    cum = _fox_gate(proj, small_blk, bias_lane)
    cum_t = cum[:, :heads].T
    cum_row = cum_t[:, None, :]
    cum_col = jnp.broadcast_to(cum_t[:, :, None], (heads, s, LANES))
    (o_raw, lse, lse_row, o_fox), (wg1, wu1, wd1) = _fox_fwd(
        proj, cum_col, cum_row, fox_out_norm, heads,
        carry=([layer(wg_sh, 1), layer(wu_sh, 1), layer(wd_sh, 1)], False))
    qg, kg, vg = _gdn_pre(proj, conv_all, heads)
    o_gdn, states = _gdn_fwd(qg, kg, vg, proj, 6, small_blk, a_lane, dt_lane, gdn_out_norm)
    o_cat = jnp.concatenate([o_fox, o_gdn], axis=1)
    mix, x2 = _mm(o_cat, wout_full, name="out_proj", residual=(x1, gt2))

    h3 = _norm_mod(x2, g_all[2:3], sc3, sh3, "norm_mod_3")
    (a3, b3, s3), _ = _ffn_up(h3, wg1, wu1, 0, "ffn2_up")
    (f3, x3), _ = _ffn_down(s3, wd1, 0, x2, gt3, "ffn2_down")

    loss_row, dx3, d_final = _final(x3, loss_target[0], final_norm.reshape(1, d))
    loss = lax.psum(loss_row[0, 0], MESH_AXES)

    df3, dgt3 = _gate_bwd(dx3, f3, gt3, MACARON_W, "ffn2_gate_bwd")
    (da3, db3), _ = _ffn_bwd_act(df3, wd1, 0, a3, b3, "ffn2_bwd_act")
    (dwd2,), _ = _ffn_bwd_wd(s3, df3, "ffn2_bwd_wd")
    (dh3,), (r_wd2,) = _ffn_bwd_h(da3, db3, wg1, wu1, 0, "ffn2_bwd_h", carry=([dwd2], True))
    dwg2, dwu2 = _ffn_bwd_wgu(h3, da3, db3, "ffn2_bwd_wgu")
    dx2, dsh3, dsc3, dg3 = _norm_mod_bwd(x2, dh3, dx3, g_all[2:3], sc3, "norm_mod_3_bwd")

    dmix, dgt2 = _gate_bwd(dx2, mix, gt2, 1.0, "mix_gate_bwd")
    do_cat = _mm(dmix, wout_full, tb=True, name="out_proj_bwd_x")
    dwout = _mm(o_cat, dmix, ta=True, out_dtype=BF16, name="out_proj_bwd_w", tk=512)
    do_fox, delta, delta_row, d_foxw = _fox_prep_bwd(do_cat, o_raw, fox_out_norm, heads)
    (dq_f, dcum_q), (r_wg2, r_wu2) = _fox_dq(proj, do_fox, cum_col, cum_row, lse, delta, heads,
                                             carry=([dwg2, dwu2], True))
    (dk_f, dv_f, dcum_k), (r_wout,) = _fox_dkv(proj, do_fox, cum_col, cum_row, lse_row, delta_row, heads,
                                               carry=([dwout.reshape(N_DEV, d // N_DEV, d)], True))
    head_lanes = lambda t: jnp.pad(t[:, 0, :].T, ((0, 0), (0, LANES - heads)))
    dsm_fox, d_fbias = _fox_gate_bwd(head_lanes(dcum_q), head_lanes(dcum_k), proj, small_blk, bias_lane)
    dqg, dkg, dvg, dz, dsm_gdn, d_alog, d_dt, d_gdnw = _gdn_bwd(
        qg, kg, vg, proj, 6, small_blk, a_lane, dt_lane, gdn_out_norm, states, do_cat, 1)
    dxc, d_conv = _gdn_pre_bwd_act(proj, conv_all, dqg, dkg, dvg, heads)
    dqkv = _gdn_pre_bwd_conv(dxc, conv_all)
    dsmall = (dsm_fox + dsm_gdn).astype(BF16)
    dproj = jnp.concatenate([dq_f, dk_f, dv_f, dqkv, dz, dsmall, jnp.zeros((s, in_pad - 7 * fw - LANES), BF16)], axis=1)
    dh2 = _mm(dproj, win_al, tb=True, name="in_proj_bwd_x", tk=1536)
    dwin_al = _mm(h2, dproj, ta=True, out_dtype=BF16, name="in_proj_bwd_w", tm=2048, tn=1536, tk=1024)
    dwin_full = jnp.concatenate(
        [dwin_al[:, :o_f], dwin_al[:, 7 * fw:7 * fw + heads], dwin_al[:, o_f:o_f + 3 * fw],
         dwin_al[:, 7 * fw + heads:7 * fw + 3 * heads], dwin_al[:, 6 * fw:7 * fw]], axis=1)
    dwin_parts = dwin_full.reshape(d, N_DEV, in_w // N_DEV).transpose(1, 0, 2)
    dx1, dsh2, dsc2, dg2 = _norm_mod_bwd(x1, dh2, dx2, g_all[1:2], sc2, "norm_mod_2_bwd")

    df1, dgt1 = _gate_bwd(dx1, f1, gt1, MACARON_W, "ffn1_gate_bwd")
    (da1, db1), (r_win,) = _ffn_bwd_act(df1, wd0, 0, a1, b1, "ffn1_bwd_act", carry=([dwin_parts], True))
    dwg1, dwu1 = _ffn_bwd_wgu(h1, da1, db1, "ffn1_bwd_wgu")
    (dwd1,), (r_wg1,) = _ffn_bwd_wd(s1, df1, "ffn1_bwd_wd", carry=([dwg1], True))
    (dh1,), (r_wu1, r_wd1) = _ffn_bwd_h(da1, db1, wg0, wu0, 0, "ffn1_bwd_h", carry=([dwu1, dwd1], True))
    grad_x, dsh1, dsc1, dg1 = _norm_mod_bwd(x0, dh1, dx1, g_all[0:1], sc1, "norm_mod_1_bwd")

    dmod = jnp.concatenate([dsh1, dsc1, dgt1, dsh2, dsc2, dgt2, dsh3, dsc3, dgt3], axis=1)
    dmod_all = _gather_row(dmod, "gather_dmod")
    ct_pad = jnp.pad(c_all.T, ((0, 0), (0, LANES - N_DEV)))
    dmod_mine = jnp.pad(_my_cols(dmod_all, me, nada), ((0, LANES - N_DEV), (0, 0)))
    g_ada_w = _ada_bwd(ct_pad, dmod_mine)

    g_small_cols = [d_fbias, d_foxw, d_alog[:, heads:], d_dt[:, heads:], d_gdnw]
    small_part = jnp.concatenate(
        [_pad_lanes(v[:, :LANES]) for v in g_small_cols]
        + [d_final, dg1, dg2, dg3] + [d_conv[k:k + 1] for k in range(CONV_W)], axis=1)
    small_all = _gather_row(small_part, "gather_small_grads")
    off = 5 * LANES
    w_small = jnp.concatenate(
        [_pad_lanes(fox_f_bias), fox_out_norm, _pad_lanes(gdn_A_log), _pad_lanes(gdn_dt_bias), gdn_out_norm,
         final_norm.reshape(1, d)], axis=1)
    m_small = jnp.concatenate(
        [_pad_lanes(m_fox_f_bias), m_fox_out_norm, _pad_lanes(m_gdn_A_log), _pad_lanes(m_gdn_dt_bias),
         m_gdn_out_norm, m_final_norm.reshape(1, d)], axis=1)
    v_small = jnp.concatenate(
        [_pad_lanes(v_fox_f_bias), v_fox_out_norm, _pad_lanes(v_gdn_A_log), _pad_lanes(v_gdn_dt_bias),
         v_gdn_out_norm, v_final_norm.reshape(1, d)], axis=1)
    rep = _adamw(small_all[:, None, :off + d], w_small, m_small, v_small, "adamw_replicated")
    ab = _adamw(dmod_all[:, None, :], ada_b, m_ada_b, v_ada_b, "adamw_ada_b")
    g_ng = small_all[:, off + d:off + 4 * d].reshape(N_DEV, 3, d)
    ngs = _adamw(_my_cols(g_ng, me, ng), norm_g[0], m_norm_g[0], v_norm_g[0], "adamw_norm_g")
    g_cv = small_all[:, off + 4 * d:].reshape(N_DEV, CONV_W, 3 * fw)
    cvs = _adamw(_my_cols(g_cv, me, ncv), gdn_conv[0], m_gdn_conv[0], v_gdn_conv[0], "adamw_gdn_conv")

    wgs = _adamw_layers(r_wg1, r_wg2, ffn_w_gate, m_ffn_w_gate, v_ffn_w_gate, "adamw_w_gate")
    wus = _adamw_layers(r_wu1, r_wu2, ffn_w_up, m_ffn_w_up, v_ffn_w_up, "adamw_w_up")
    wds = _adamw_layers(r_wd1, r_wd2, ffn_w_down, m_ffn_w_down, v_ffn_w_down, "adamw_w_down")
    wis = [o[None] for o in _adamw(r_win, w_in[0], m_w_in[0], v_w_in[0], "adamw_w_in")]
    wos = [o[None] for o in _adamw(r_wout, w_out[0], m_w_out[0], v_w_out[0], "adamw_w_out")]
    adas = [o[None] for o in _adamw(g_ada_w[None], ada_w[0], m_ada_w[0], v_ada_w[0], "adamw_ada_w")]
    ngs = [o[None] for o in ngs]
    cvs = [o[None] for o in cvs]

    def rep_piece(i, lo, width):
        return rep[i][:, lo:lo + width]

    nh = fox_f_bias.shape[1]
    outs = []
    for i in range(4):
        outs.append([adas[i], ab[i], ngs[i], wgs[i], wus[i], wds[i], wis[i], wos[i],
                     rep_piece(i, 0, nh), rep_piece(i, LANES, HEAD_DIM), cvs[i], rep_piece(i, 2 * LANES, nh),
                     rep_piece(i, 3 * LANES, nh), rep_piece(i, 4 * LANES, HEAD_DIM), rep_piece(i, off, d).reshape(d)])
    return (loss, grad_x[None], *outs[0], *outs[1], *outs[2], *outs[3])
```

```python
import math

import numpy as np
import jax
import jax.numpy as jnp
from jax import lax
from jax.experimental import pallas as pl
from jax.experimental.pallas import tpu as pltpu

F32 = jnp.float32
BF16 = jnp.bfloat16

N_DEV = 8
MESH_AXES = ("x", "y", "c")
LANES = 128
HEAD_DIM = 128
GDN_CHUNK = 64
CONV_W = 4
N_MOD = 9
MACARON_W = 0.5
EPS = 1e-6
NEG = -1e30
VMEM_LIMIT_BYTES = 56 * 2 ** 20
ADAM_BLOCK_BYTES = 4 * 2 ** 20
FOX_BWD_HEADS_PER_STEP = 4
FOX_HEADS_PER_STEP = 8

ADAM_LR = 0.001
ADAM_B1 = 0.9
ADAM_B2 = 0.999
ADAM_EPS = 1e-08
ADAM_WD = 0.01
ADAM_STEP = 10

MESH_ID = pl.DeviceIdType.MESH
ANY = pl.BlockSpec(memory_space=pl.ANY)
VMEM = pl.BlockSpec(memory_space=pltpu.VMEM)


def _pcall(body, **kw):
    return pl.pallas_call(body, **kw)


def _params(*semantics):
    return pltpu.CompilerParams(dimension_semantics=semantics, vmem_limit_bytes=VMEM_LIMIT_BYTES)


def _tile(n, pref):
    if n % pref == 0 and pref % 8 == 0:
        return pref
    t = 1 << (max(1, min(n, pref)).bit_length() - 1)
    while n % t:
        t //= 2
    return t if t % 8 == 0 else n


def _sigmoid(x):
    return 1.0 / (1.0 + jnp.exp(-x))


def _dot(a, b, dims):
    return lax.dot_general(a.astype(BF16), b.astype(BF16), (dims, ((), ())), preferred_element_type=F32)


NN = ((1,), (0,))
NT = ((1,), (1,))
TN = ((0,), (0,))


def _split3(x):
    hi = x.astype(BF16)
    r1 = x - hi.astype(F32)
    mid = r1.astype(BF16)
    lo = (r1 - mid.astype(F32)).astype(BF16)
    return hi, mid, lo


def _dot_exact_lhs(m_bf16, x, dims=NN):
    hi, mid, lo = _split3(x)
    d = lambda p: lax.dot_general(m_bf16, p, (dims, ((), ())), preferred_element_type=F32)
    return d(hi) + (d(mid) + d(lo))


def _dot_hp(a, b, dims):
    ah = a.astype(BF16)
    al = (a - ah.astype(F32)).astype(BF16)
    bh = b.astype(BF16)
    bl = (b - bh.astype(F32)).astype(BF16)
    d = lambda p, q: lax.dot_general(p, q, (dims, ((), ())), preferred_element_type=F32)
    return d(ah, bh) + (d(ah, bl) + d(al, bh))


def _mesh_pos():
    return lax.axis_index("x"), lax.axis_index("y"), lax.axis_index("c")


def _peer(pos, mask):
    x, y, c = pos
    return (1 - x if mask & 4 else x, 1 - y if mask & 2 else y, 1 - c if mask & 1 else c)


def _linear(pos):
    return 4 * pos[0] + 2 * pos[1] + pos[2]


def _exchange_copies(ins, outs, sems, scatter, with_receives=True):
    send_sems, recv_sems, local_sems = sems
    pos = _mesh_pos()
    me = _linear(pos)
    local, sends, recvs = [], [], []
    for i in range(len(ins)):
        src = ins[i].at[me] if scatter else ins[i]
        local.append(pltpu.make_async_copy(src, outs[i].at[me], local_sems.at[i]))
    for mask in range(1, N_DEV):
        peer = _peer(pos, mask)
        for i in range(len(ins)):
            sem = dict(send_sem=send_sems.at[i, mask - 1], recv_sem=recv_sems.at[i, mask - 1],
                       device_id=peer, device_id_type=MESH_ID)
            sends.append(pltpu.make_async_remote_copy(
                src_ref=ins[i].at[_linear(peer)] if scatter else ins[i], dst_ref=outs[i].at[me], **sem))
            if with_receives:
                recvs.append(pltpu.make_async_remote_copy(
                    src_ref=ins[i].at[me] if scatter else ins[i], dst_ref=outs[i].at[_linear(peer)], **sem))
    return local, sends, recvs


def _exchange_start(ins, outs, sems, scatter):
    local, sends, _ = _exchange_copies(ins, outs, sems, scatter, with_receives=False)
    for cp in local + sends:
        cp.start()


def _exchange_wait(ins, outs, sems, scatter):
    local, sends, recvs = _exchange_copies(ins, outs, sems, scatter)
    for cp in recvs:
        cp.wait_recv()
    for cp in sends:
        cp.wait_send()
    for cp in local:
        cp.wait()


def _exchange_sems(n):
    return [pltpu.SemaphoreType.DMA((n, N_DEV - 1)), pltpu.SemaphoreType.DMA((n, N_DEV - 1)),
            pltpu.SemaphoreType.DMA((n,))]


def _exchange_shapes(arrays, scatter):
    return [jax.ShapeDtypeStruct(a.shape if scatter else (N_DEV,) + a.shape, a.dtype) for a in arrays]


def _exchange(arrays, *, scatter, in_vmem, name):
    n = len(arrays)

    def body(*refs):
        ins, outs, sems = refs[:n], refs[n:2 * n], refs[2 * n:]
        _exchange_start(ins, outs, sems, scatter)
        _exchange_wait(ins, outs, sems, scatter)

    spec = VMEM if in_vmem else ANY
    outs = _pcall(
        body, name=name, out_shape=_exchange_shapes(arrays, scatter),
        in_specs=[spec] * n, out_specs=[spec] * n, scratch_shapes=_exchange_sems(n),
    )(*arrays)
    return list(outs)


def _gather_via_sibling(arrays, name):
    n = len(arrays)

    def body(*refs):
        ins, outs = refs[:n], refs[n:2 * n]
        send_sems, recv_sems, local_sems = refs[2 * n:]
        x, y, c = _mesh_pos()
        me, sibling = (x, y, c), (x, y, 1 - c)
        chips = [(1 - x, y), (x, 1 - y), (1 - x, 1 - y)]

        def copy(i, k, block, to, from_input=False):
            return pltpu.make_async_remote_copy(
                src_ref=ins[i] if from_input else outs[i].at[_linear(block)], dst_ref=outs[i].at[_linear(block)],
                send_sem=send_sems.at[i, k], recv_sem=recv_sems.at[i, k], device_id=to, device_id_type=MESH_ID)

        mine = [pltpu.make_async_copy(ins[i], outs[i].at[_linear(me)], local_sems.at[i]) for i in range(n)]
        first = []
        for i in range(n):
            first.append(copy(i, 0, me, sibling, True))
            first += [copy(i, 1 + j, me, (*chip, c), True) for j, chip in enumerate(chips)]
        for cp in mine + first:
            cp.start()
        passed = []
        for j, chip in enumerate(chips):
            for i in range(n):
                copy(i, 1 + j, (*chip, c), me).wait_recv()
                cp = copy(i, 4 + j, (*chip, c), sibling)
                cp.start()
                passed.append(cp)
        for i in range(n):
            copy(i, 0, sibling, me).wait_recv()
            for j, chip in enumerate(chips):
                copy(i, 4 + j, (*chip, 1 - c), me).wait_recv()
        for cp in first + passed:
            cp.wait_send()
        for cp in mine:
            cp.wait()

    outs = _pcall(
        body, name=name, out_shape=_exchange_shapes(arrays, False), in_specs=[ANY] * n, out_specs=[ANY] * n,
        scratch_shapes=_exchange_sems(n),
    )(*arrays)
    return list(outs)


def _hosted(body, *, name, grid, in_specs, out_specs, out_shape, args, scratch_shapes=(), prefetch=(), carry=None):
    n_in, n_out, n_scr, n_pre = len(in_specs), len(out_shape), len(scratch_shapes), len(prefetch)
    arrays, scatter = carry if carry is not None else ([], False)
    n = len(arrays)

    def wrapped(*refs):
        pre, r = refs[:n_pre], refs[n_pre:]
        host_in, comm_in = r[:n_in], r[n_in:n_in + n]
        r = r[n_in + n:]
        host_out, comm_out = r[:n_out], r[n_out:n_out + n]
        r = r[n_out + n:]
        host_scr, sems = r[:n_scr], r[n_scr:]
        if n:
            first = pl.program_id(0) == 0
            last = pl.program_id(0) == grid[0] - 1
            for ax in range(1, len(grid)):
                first = first & (pl.program_id(ax) == 0)
                last = last & (pl.program_id(ax) == grid[ax] - 1)

            @pl.when(first)
            def _():
                _exchange_start(comm_in, comm_out, sems, scatter)

        body(*pre, *host_in, *host_out, *host_scr)
        if n:
            @pl.when(last)
            def _():
                _exchange_wait(comm_in, comm_out, sems, scatter)

    grid_spec = pltpu.PrefetchScalarGridSpec(
        num_scalar_prefetch=n_pre, grid=grid, in_specs=list(in_specs) + [ANY] * n,
        out_specs=list(out_specs) + [ANY] * n,
        scratch_shapes=list(scratch_shapes) + (_exchange_sems(n) if n else []))
    outs = _pcall(
        wrapped, name=name, grid_spec=grid_spec, out_shape=list(out_shape) + _exchange_shapes(arrays, scatter),
        compiler_params=_params(*(["arbitrary"] * len(grid))),
    )(*prefetch, *args, *arrays)
    return list(outs[:n_out]), list(outs[n_out:])


def _gather_row(v, name):
    return _exchange([v], scatter=False, in_vmem=True, name=name)[0].reshape(N_DEV, v.shape[1])


def _ada_fwd(c_all, w, b):
    d, n = w.shape
    tn = _tile(n, 256)

    def body(c_ref, w_ref, b_ref, o_ref):
        cv = c_ref[...]
        cond = cv * _sigmoid(cv)
        o_ref[...] = _dot_hp(cond, w_ref[...], NN) + b_ref[...]

    return _pcall(
        body, name="ada_fwd", grid=(n // tn,),
        in_specs=[pl.BlockSpec((N_DEV, d), lambda j: (0, 0)), pl.BlockSpec((d, tn), lambda j: (0, j)),
                  pl.BlockSpec((1, tn), lambda j: (0, j))],
        out_specs=pl.BlockSpec((N_DEV, tn), lambda j: (0, j)),
        out_shape=jax.ShapeDtypeStruct((N_DEV, n), F32), compiler_params=_params("parallel"),
    )(c_all, w, b)


def _ada_bwd(ct_pad, dmod_pad):
    d = ct_pad.shape[0]
    n = dmod_pad.shape[1]
    tn = _tile(n, 256)

    def body(c_ref, g_ref, o_ref):
        cv = c_ref[...]
        cond = cv * _sigmoid(cv)
        o_ref[...] = _dot_hp(cond, g_ref[...], NN)

    return _pcall(
        body, name="ada_bwd", grid=(n // tn,),
        in_specs=[pl.BlockSpec((d, LANES), lambda j: (0, 0)), pl.BlockSpec((LANES, tn), lambda j: (0, j))],
        out_specs=pl.BlockSpec((d, tn), lambda j: (0, j)),
        out_shape=jax.ShapeDtypeStruct((d, n), F32), compiler_params=_params("parallel"),
    )(ct_pad, dmod_pad)


def _norm_mod(x, g, sc, sh, name):
    s, d = x.shape
    ts = _tile(s, 512)

    def body(x_ref, g_ref, sc_ref, sh_ref, h_ref):
        xv = x_ref[...]
        r = lax.rsqrt(jnp.mean(xv * xv, axis=-1, keepdims=True) + EPS)
        h_ref[...] = (xv * r * g_ref[...] * (1.0 + sc_ref[...]) + sh_ref[...]).astype(BF16)

    row = pl.BlockSpec((1, d), lambda i: (0, 0))
    return _pcall(
        body, name=name, grid=(s // ts,),
        in_specs=[pl.BlockSpec((ts, d), lambda i: (i, 0)), row, row, row],
        out_specs=pl.BlockSpec((ts, d), lambda i: (i, 0)),
        out_shape=jax.ShapeDtypeStruct((s, d), BF16), compiler_params=_params("parallel"),
    )(x, g, sc, sh)


def _norm_mod_bwd(x, dh, dx_out, g, sc, name):
    s, d = x.shape
    ts = _tile(s, 512)

    def body(x_ref, dh_ref, dxo_ref, g_ref, sc_ref, dx_ref, dsh_ref, dsc_ref, dg_ref):
        @pl.when(pl.program_id(0) == 0)
        def _():
            dsh_ref[...] = jnp.zeros_like(dsh_ref)
            dsc_ref[...] = jnp.zeros_like(dsc_ref)
            dg_ref[...] = jnp.zeros_like(dg_ref)

        xv = x_ref[...]
        dh_v = dh_ref[...]
        gv = g_ref[...]
        one_sc = 1.0 + sc_ref[...]
        r = lax.rsqrt(jnp.mean(xv * xv, axis=-1, keepdims=True) + EPS)
        xn = xv * r
        dxn = dh_v * (gv * one_sc)
        dx_ref[...] = dxo_ref[...] + r * (dxn - xn * jnp.mean(dxn * xn, axis=-1, keepdims=True))
        t = dh_v * xn
        dsh_ref[...] += jnp.sum(dh_v, axis=0, keepdims=True)
        dsc_ref[...] += jnp.sum(t * gv, axis=0, keepdims=True)
        dg_ref[...] += jnp.sum(t * one_sc, axis=0, keepdims=True)

    blk = pl.BlockSpec((ts, d), lambda i: (i, 0))
    row = pl.BlockSpec((1, d), lambda i: (0, 0))
    return _pcall(
        body, name=name, grid=(s // ts,),
        in_specs=[blk, blk, blk, row, row], out_specs=[blk, row, row, row],
        out_shape=[jax.ShapeDtypeStruct((s, d), F32)] + [jax.ShapeDtypeStruct((1, d), F32)] * 3,
        compiler_params=_params("arbitrary"),
    )(x, dh, dx_out, g, sc)


def _gate_bwd(dx, f, gt, k, name):
    s, d = dx.shape
    ts = _tile(s, 512)

    def body(dx_ref, f_ref, gt_ref, df_ref, dgt_ref):
        @pl.when(pl.program_id(0) == 0)
        def _():
            dgt_ref[...] = jnp.zeros_like(dgt_ref)

        dxv = dx_ref[...]
        df_ref[...] = ((k * gt_ref[...]) * dxv).astype(BF16)
        dgt_ref[...] += k * jnp.sum(f_ref[...] * dxv, axis=0, keepdims=True)

    blk = pl.BlockSpec((ts, d), lambda i: (i, 0))
    row = pl.BlockSpec((1, d), lambda i: (0, 0))
    return _pcall(
        body, name=name, grid=(s // ts,),
        in_specs=[blk, blk, row], out_specs=[blk, row],
        out_shape=[jax.ShapeDtypeStruct((s, d), BF16), jax.ShapeDtypeStruct((1, d), F32)],
        compiler_params=_params("arbitrary"),
    )(dx, f, gt)


def _ffn_up(h, wg, wu, layer, name, carry=None):
    s, d = h.shape
    fs = wg.shape[-1]
    tm = _tile(s, 1024)

    def body(h_ref, wg_ref, wu_ref, a_ref, b_ref, s_ref):
        hv = h_ref[...]
        a = jnp.dot(hv, wg_ref[...], preferred_element_type=F32)
        b = jnp.dot(hv, wu_ref[...], preferred_element_type=F32)
        a_ref[...] = a.astype(BF16)
        b_ref[...] = b.astype(BF16)
        s_ref[...] = (a * _sigmoid(a) * b).astype(BF16)

    wspec = pl.BlockSpec((None, None, d, fs), lambda j, m: (j, layer, 0, 0))
    ospec = pl.BlockSpec((None, tm, fs), lambda j, m: (j, m, 0))
    return _hosted(
        body, name=name, grid=(N_DEV, s // tm),
        in_specs=[pl.BlockSpec((tm, d), lambda j, m: (m, 0)), wspec, wspec],
        out_specs=[ospec, ospec, ospec],
        out_shape=[jax.ShapeDtypeStruct((N_DEV, s, fs), BF16)] * 3,
        args=(h, wg, wu), carry=carry)


def _ffn_down(sv, wd, layer, x_in, gt, name, carry=None):
    _, s, fs = sv.shape
    d = wd.shape[-1]
    tm = _tile(s, 512)

    def body(s_ref, wd_ref, x_ref, gt_ref, f_ref, xo_ref, acc):
        j = pl.program_id(1)

        @pl.when(j == 0)
        def _():
            acc[...] = jnp.zeros_like(acc)

        acc[...] += jnp.dot(s_ref[...], wd_ref[...], preferred_element_type=F32)

        @pl.when(j == N_DEV - 1)
        def _():
            fv = acc[...]
            f_ref[...] = fv
            xo_ref[...] = x_ref[...] + (MACARON_W * gt_ref[...]) * fv

    blk = pl.BlockSpec((tm, d), lambda m, j: (m, 0))
    return _hosted(
        body, name=name, grid=(s // tm, N_DEV),
        in_specs=[pl.BlockSpec((None, tm, fs), lambda m, j: (j, m, 0)),
                  pl.BlockSpec((None, None, fs, d), lambda m, j: (j, layer, 0, 0)),
                  blk, pl.BlockSpec((1, d), lambda m, j: (0, 0))],
        out_specs=[blk, blk],
        out_shape=[jax.ShapeDtypeStruct((s, d), F32)] * 2,
        scratch_shapes=[pltpu.VMEM((tm, d), F32)],
        args=(sv, wd, x_in, gt), carry=carry)


def _ffn_bwd_act(df, wd, layer, a, b, name, carry=None):
    s, d = df.shape
    fs = a.shape[-1]
    tm = _tile(s, 1024)

    def body(df_ref, wd_ref, a_ref, b_ref, da_ref, db_ref):
        ds = lax.dot_general(df_ref[...], wd_ref[...], (NT, ((), ())), preferred_element_type=F32)
        av = a_ref[...].astype(F32)
        sg = _sigmoid(av)
        da_ref[...] = (ds * b_ref[...].astype(F32) * (sg * (1.0 + av * (1.0 - sg)))).astype(BF16)
        db_ref[...] = (ds * (av * sg)).astype(BF16)

    hid = pl.BlockSpec((None, tm, fs), lambda j, m: (j, m, 0))
    return _hosted(
        body, name=name, grid=(N_DEV, s // tm),
        in_specs=[pl.BlockSpec((tm, d), lambda j, m: (m, 0)),
                  pl.BlockSpec((None, None, fs, d), lambda j, m: (j, layer, 0, 0)), hid, hid],
        out_specs=[hid, hid],
        out_shape=[jax.ShapeDtypeStruct((N_DEV, s, fs), BF16)] * 2,
        args=(df, wd, a, b), carry=carry)


def _ffn_bwd_wd(sv, df, name, carry=None):
    _, s, fs = sv.shape
    d = df.shape[1]
    tk = _tile(s, 1024)
    nk = s // tk

    def body(s_ref, df_ref, o_ref, acc):
        @pl.when(pl.program_id(1) == 0)
        def _():
            acc[...] = jnp.zeros_like(acc)

        acc[...] += lax.dot_general(s_ref[...], df_ref[...], (TN, ((), ())), preferred_element_type=F32)

        @pl.when(pl.program_id(1) == nk - 1)
        def _():
            o_ref[...] = acc[...].astype(BF16)

    return _hosted(
        body, name=name, grid=(N_DEV, nk),
        in_specs=[pl.BlockSpec((None, tk, fs), lambda j, k: (j, k, 0)), pl.BlockSpec((tk, d), lambda j, k: (k, 0))],
        out_specs=[pl.BlockSpec((None, fs, d), lambda j, k: (j, 0, 0))],
        out_shape=[jax.ShapeDtypeStruct((N_DEV, fs, d), BF16)],
        scratch_shapes=[pltpu.VMEM((fs, d), F32)],
        args=(sv, df), carry=carry)


def _ffn_bwd_h(da, db, wg, wu, layer, name, carry=None):
    _, s, fs = da.shape
    d = wg.shape[-2]
    tm = _tile(s, 1024)

    def body(da_ref, db_ref, wg_ref, wu_ref, o_ref, acc):
        j = pl.program_id(1)

        @pl.when(j == 0)
        def _():
            acc[...] = jnp.zeros_like(acc)

        acc[...] += (lax.dot_general(da_ref[...], wg_ref[...], (NT, ((), ())), preferred_element_type=F32)
                     + lax.dot_general(db_ref[...], wu_ref[...], (NT, ((), ())), preferred_element_type=F32))

        @pl.when(j == N_DEV - 1)
        def _():
            o_ref[...] = acc[...]

    hid = pl.BlockSpec((None, tm, fs), lambda m, j: (j, m, 0))
    wspec = pl.BlockSpec((None, None, d, fs), lambda m, j: (j, layer, 0, 0))
    return _hosted(
        body, name=name, grid=(s // tm, N_DEV),
        in_specs=[hid, hid, wspec, wspec],
        out_specs=[pl.BlockSpec((tm, d), lambda m, j: (m, 0))],
        out_shape=[jax.ShapeDtypeStruct((s, d), F32)],
        scratch_shapes=[pltpu.VMEM((tm, d), F32)],
        args=(da, db, wg, wu), carry=carry)


def _ffn_bwd_wgu(h, da, db, name):
    s, d = h.shape
    fs = da.shape[-1]
    tk = _tile(s, 1024)
    nk = s // tk

    def body(h_ref, da_ref, db_ref, og_ref, ou_ref, accg, accu):
        @pl.when(pl.program_id(1) == 0)
        def _():
            accg[...] = jnp.zeros_like(accg)
            accu[...] = jnp.zeros_like(accu)

        hv = h_ref[...]
        accg[...] += lax.dot_general(hv, da_ref[...], (TN, ((), ())), preferred_element_type=F32)
        accu[...] += lax.dot_general(hv, db_ref[...], (TN, ((), ())), preferred_element_type=F32)

        @pl.when(pl.program_id(1) == nk - 1)
        def _():
            og_ref[...] = accg[...].astype(BF16)
            ou_ref[...] = accu[...].astype(BF16)

    hid = pl.BlockSpec((None, tk, fs), lambda j, k: (j, k, 0))
    ospec = pl.BlockSpec((None, d, fs), lambda j, k: (j, 0, 0))
    return _pcall(
        body, name=name, grid=(N_DEV, nk),
        in_specs=[pl.BlockSpec((tk, d), lambda j, k: (k, 0)), hid, hid],
        out_specs=[ospec, ospec],
        out_shape=[jax.ShapeDtypeStruct((N_DEV, d, fs), BF16)] * 2,
        scratch_shapes=[pltpu.VMEM((d, fs), F32), pltpu.VMEM((d, fs), F32)],
        compiler_params=_params("parallel", "arbitrary"),
    )(h, da, db)


def _mm(a, b, *, ta=False, tb=False, out_dtype=F32, name, tm=1024, tn=1024, tk=2048, residual=None):
    m, kdim = (a.shape[1], a.shape[0]) if ta else a.shape
    n = b.shape[0] if tb else b.shape[1]
    tm, tn, tk = _tile(m, tm), _tile(n, tn), _tile(kdim, tk)
    nk = kdim // tk
    dims = ((0,) if ta else (1,), (1,) if tb else (0,))

    def body(*refs):
        a_ref, b_ref = refs[:2]
        acc = refs[-1]
        kk = pl.program_id(2)

        @pl.when(kk == 0)
        def _():
            acc[...] = jnp.zeros_like(acc)

        acc[...] += lax.dot_general(a_ref[...].astype(BF16), b_ref[...].astype(BF16), (dims, ((), ())),
                                    preferred_element_type=F32)

        @pl.when(kk == nk - 1)
        def _():
            if residual is None:
                refs[2][...] = acc[...].astype(out_dtype)
            else:
                res_ref, gate_ref, y_ref, xo_ref = refs[2:6]
                yv = acc[...]
                y_ref[...] = yv
                xo_ref[...] = res_ref[...] + gate_ref[...] * yv

    a_spec = pl.BlockSpec((tk, tm), lambda i, j, k: (k, i)) if ta else pl.BlockSpec((tm, tk), lambda i, j, k: (i, k))
    b_spec = pl.BlockSpec((tn, tk), lambda i, j, k: (j, k)) if tb else pl.BlockSpec((tk, tn), lambda i, j, k: (k, j))
    o_spec = pl.BlockSpec((tm, tn), lambda i, j, k: (i, j))
    if residual is None:
        in_specs, out_specs = [a_spec, b_spec], o_spec
        out_shape = jax.ShapeDtypeStruct((m, n), out_dtype)
        args = (a, b)
    else:
        in_specs = [a_spec, b_spec, o_spec, pl.BlockSpec((1, tn), lambda i, j, k: (0, j))]
        out_specs = [o_spec, o_spec]
        out_shape = [jax.ShapeDtypeStruct((m, n), F32)] * 2
        args = (a, b) + tuple(residual)
    return _pcall(
        body, name=name, grid=(m // tm, n // tn, nk), in_specs=in_specs, out_specs=out_specs, out_shape=out_shape,
        scratch_shapes=[pltpu.VMEM((tm, tn), F32)],
        compiler_params=_params("parallel", "parallel", "arbitrary"),
    )(*args)


def _log_sigmoid(z):
    return jnp.minimum(z, 0.0) - jnp.log(1.0 + jnp.exp(-jnp.abs(z)))


def _fox_gate(proj, small_blk, bias_lane):
    s = proj.shape[0]
    ts = _tile(s, 1024)
    nsub = ts // LANES

    def body(z_ref, b_ref, cum_ref, carry):
        @pl.when(pl.program_id(0) == 0)
        def _():
            carry[...] = jnp.zeros_like(carry)

        ii = lax.broadcasted_iota(jnp.int32, (LANES, LANES), 0)
        jj = lax.broadcasted_iota(jnp.int32, (LANES, LANES), 1)
        tri = (ii >= jj).astype(BF16)
        logf = _log_sigmoid(z_ref[...] + b_ref[...])
        cv = carry[...]
        for sb in range(nsub):
            blk = logf[sb * LANES:(sb + 1) * LANES, :]
            cum_ref[sb * LANES:(sb + 1) * LANES, :] = _dot_exact_lhs(tri, blk) + cv
            cv = cv + jnp.sum(blk, axis=0, keepdims=True)
        carry[...] = cv

    return _pcall(
        body, name="fox_gate", grid=(s // ts,),
        in_specs=[pl.BlockSpec((ts, LANES), lambda i: (i, small_blk)), pl.BlockSpec((1, LANES), lambda i: (0, 0))],
        out_specs=pl.BlockSpec((ts, LANES), lambda i: (i, 0)),
        out_shape=jax.ShapeDtypeStruct((s, LANES), F32),
        scratch_shapes=[pltpu.VMEM((1, LANES), F32)],
        compiler_params=_params("arbitrary"),
    )(proj, bias_lane)


def _fox_gate_bwd(dcum_q, dcum_k, proj, small_blk, bias_lane):
    s = proj.shape[0]
    ts = _tile(s, 1024)
    nsub = ts // LANES
    nb = s // ts

    def body(dcq_ref, dc_ref, z_ref, b_ref, dz_ref, db_ref, carry):
        @pl.when(pl.program_id(0) == 0)
        def _():
            carry[...] = jnp.zeros_like(carry)
            db_ref[...] = jnp.zeros_like(db_ref)

        ii = lax.broadcasted_iota(jnp.int32, (LANES, LANES), 0)
        jj = lax.broadcasted_iota(jnp.int32, (LANES, LANES), 1)
        triu = (jj >= ii).astype(BF16)
        dc = dcq_ref[...] + dc_ref[...]
        zb = z_ref[...] + b_ref[...]
        cv = carry[...]
        dbv = jnp.zeros((1, LANES), F32)
        for sb in reversed(range(nsub)):
            rows = slice(sb * LANES, (sb + 1) * LANES)
            blk = dc[rows, :]
            dlogf = _dot_exact_lhs(triu, blk) + cv
            cv = cv + jnp.sum(blk, axis=0, keepdims=True)
            dz = dlogf * _sigmoid(-zb[rows, :])
            dz_ref[rows, :] = dz
            dbv = dbv + jnp.sum(dz, axis=0, keepdims=True)
        carry[...] = cv
        db_ref[...] += dbv

    row = pl.BlockSpec((1, LANES), lambda i: (0, 0))
    return _pcall(
        body, name="fox_gate_bwd", grid=(nb,),
        in_specs=[pl.BlockSpec((ts, LANES), lambda i: (nb - 1 - i, 0)),
                  pl.BlockSpec((ts, LANES), lambda i: (nb - 1 - i, 0)),
                  pl.BlockSpec((ts, LANES), lambda i: (nb - 1 - i, small_blk)), row],
        out_specs=[pl.BlockSpec((ts, LANES), lambda i: (nb - 1 - i, 0)), row],
        out_shape=[jax.ShapeDtypeStruct((s, LANES), F32), jax.ShapeDtypeStruct((1, LANES), F32)],
        scratch_shapes=[pltpu.VMEM((1, LANES), F32)],
        compiler_params=_params("arbitrary"),
    )(dcum_q, dcum_k, proj, bias_lane)


def _tri_tables(n, by_key):
    if by_key:
        pairs = [(i, j) for j in range(n) for i in range(j, n)]
    else:
        pairs = [(i, j) for i in range(n) for j in range(i + 1)]
    return (jnp.asarray(np.array([p[0] for p in pairs], np.int32)),
            jnp.asarray(np.array([p[1] for p in pairs], np.int32)))


def _fox_group(heads):
    return FOX_HEADS_PER_STEP if heads % FOX_HEADS_PER_STEP == 0 else 1


def _as_row(col):
    t = col.shape[0]
    eye = lax.broadcasted_iota(jnp.int32, (t, t), 0) == lax.broadcasted_iota(jnp.int32, (t, t), 1)
    return jnp.sum(jnp.where(eye, col, 0.0), axis=0, keepdims=True)


def _fox_scores(a, b, bias_col, bias_row, scale, diagonal, rows_are_keys=False):
    sc = lax.dot_general(a.astype(BF16), b.astype(BF16), (NT, ((), ())), preferred_element_type=F32) * scale
    sc = sc + (bias_col + bias_row)
    if not diagonal:
        return sc
    row = lax.broadcasted_iota(jnp.int32, sc.shape, 0)
    col = lax.broadcasted_iota(jnp.int32, sc.shape, 1)
    return jnp.where(row <= col if rows_are_keys else col <= row, sc, NEG)


def _fox_fwd(proj, cum_col, cum_row, w_norm, heads, carry=None):
    s = proj.shape[0]
    t = _tile(s, 512)
    grp = _fox_group(heads)
    qi, ki = _tri_tables(s // t, False)
    scale = 1.0 / math.sqrt(HEAD_DIM)

    def body(qi_ref, ki_ref, q_ref, k_ref, v_ref, cq_ref, ck_ref, w_ref, o_ref, lse_ref, lser_ref, on_ref, m_s, acc_s):
        iq, ik = qi_ref[pl.program_id(1)], ki_ref[pl.program_id(1)]

        @pl.when(ik == 0)
        def _():
            m_s[...] = jnp.full_like(m_s, NEG)
            acc_s[...] = jnp.zeros_like(acc_s)

        def step(diagonal):
            for g in range(grp):
                sl = slice(g * HEAD_DIM, (g + 1) * HEAD_DIM)
                sc = _fox_scores(q_ref[:, sl], k_ref[:, sl], cq_ref[g, :, 0:1], -ck_ref[g], scale, diagonal)
                m_prev = m_s[g]
                m_new = jnp.maximum(m_prev, jnp.max(sc, axis=1, keepdims=True))
                p = jnp.exp(sc - m_new).astype(BF16)
                v_ones = jnp.concatenate([v_ref[:, sl].astype(BF16), jnp.ones((t, LANES), BF16)], axis=1)
                acc_s[g] = jnp.exp(m_prev - m_new) * acc_s[g] + jnp.dot(p, v_ones, preferred_element_type=F32)
                m_s[g] = m_new

        @pl.when(ik < iq)
        def _():
            step(False)

        @pl.when(ik == iq)
        def _():
            step(True)
            for g in range(grp):
                sl = slice(g * HEAD_DIM, (g + 1) * HEAD_DIM)
                acc = acc_s[g]
                o = acc[:, :HEAD_DIM] / acc[:, HEAD_DIM:]
                lse = m_s[g] + jnp.log(acc[:, HEAD_DIM:])
                o_ref[:, sl] = o
                lse_ref[g] = lse
                lser_ref[g] = _as_row(lse[:, 0:1])
                r = lax.rsqrt(jnp.mean(o * o, axis=1, keepdims=True) + EPS)
                on_ref[:, sl] = (o * r * w_ref[...]).astype(BF16)

    ng = heads // grp
    qblk = pl.BlockSpec((t, grp * HEAD_DIM), lambda h, p, qi, ki: (qi[p], h))
    kblk = lambda off: pl.BlockSpec((t, grp * HEAD_DIM), lambda h, p, qi, ki: (ki[p], off + h))
    qcol = pl.BlockSpec((grp, t, LANES), lambda h, p, qi, ki: (h, qi[p], 0))
    return _hosted(
        body, name="fox_fwd", grid=(ng, int(qi.shape[0])), prefetch=(qi, ki),
        in_specs=[qblk, kblk(ng), kblk(2 * ng), qcol,
                  pl.BlockSpec((grp, 1, t), lambda h, p, qi, ki: (h, 0, ki[p])),
                  pl.BlockSpec((1, HEAD_DIM), lambda h, p, qi, ki: (0, 0))],
        out_specs=[qblk, qcol, pl.BlockSpec((grp, 1, t), lambda h, p, qi, ki: (h, 0, qi[p])), qblk],
        scratch_shapes=[pltpu.VMEM((grp, t, 1), F32), pltpu.VMEM((grp, t, 2 * HEAD_DIM), F32)],
        out_shape=[jax.ShapeDtypeStruct((s, heads * HEAD_DIM), F32), jax.ShapeDtypeStruct((heads, s, LANES), F32),
                   jax.ShapeDtypeStruct((heads, 1, s), F32), jax.ShapeDtypeStruct((s, heads * HEAD_DIM), BF16)],
        args=(proj, proj, proj, cum_col, cum_row, w_norm), carry=carry)


def _fox_prep_bwd(do_cat, o_raw, w_norm, heads):
    s = o_raw.shape[0]
    ts = _tile(s, 512)

    def body(g_ref, o_ref, w_ref, do_ref, delta_ref, deltar_ref, dw_ref):
        @pl.when((pl.program_id(0) == 0) & (pl.program_id(1) == 0))
        def _():
            dw_ref[...] = jnp.zeros_like(dw_ref)

        o = o_ref[...]
        g = g_ref[...]
        r = lax.rsqrt(jnp.mean(o * o, axis=1, keepdims=True) + EPS)
        wg = g * w_ref[...]
        do = r * wg - o * (r * r * r) * jnp.mean(wg * o, axis=1, keepdims=True)
        do_ref[...] = do.astype(BF16)
        delta = jnp.sum(do * o, axis=1, keepdims=True)
        delta_ref[...] = jnp.broadcast_to(delta, delta_ref.shape)
        deltar_ref[...] = _as_row(delta)
        dw_ref[...] += jnp.sum(g * o * r, axis=0, keepdims=True)

    blk = pl.BlockSpec((ts, HEAD_DIM), lambda h, i: (i, h))
    row = pl.BlockSpec((1, HEAD_DIM), lambda h, i: (0, 0))
    return _pcall(
        body, name="fox_prep_bwd", grid=(heads, s // ts),
        in_specs=[blk, blk, row],
        out_specs=[blk, pl.BlockSpec((None, ts, LANES), lambda h, i: (h, i, 0)),
                   pl.BlockSpec((None, 1, ts), lambda h, i: (h, 0, i)), row],
        out_shape=[jax.ShapeDtypeStruct((s, heads * HEAD_DIM), BF16), jax.ShapeDtypeStruct((heads, s, LANES), F32),
                   jax.ShapeDtypeStruct((heads, 1, s), F32), jax.ShapeDtypeStruct((1, HEAD_DIM), F32)],
        compiler_params=_params("arbitrary", "arbitrary"),
    )(do_cat, o_raw, w_norm)


def _fox_bwd(proj, do, cum_col, cum_row, lse_row, delta_row, heads, carry=None):
    s = proj.shape[0]
    t = _tile(s, 512)
    nk = s // t
    grp = FOX_BWD_HEADS_PER_STEP if heads % FOX_BWD_HEADS_PER_STEP == 0 else 1
    qi, ki = _tri_tables(nk, True)
    npairs = int(qi.shape[0])
    scale = 1.0 / math.sqrt(HEAD_DIM)

    def body(qi_ref, ki_ref, q_ref, k_ref, v_ref, do_ref, cqr_ref, ckc_ref, lse_ref, dl_ref,
             dk_ref, dv_ref, dck_ref, dq_hbm, dcq_ref, dk_acc, dv_acc, dck_acc, dq_acc, stage, sem):
        pair = pl.program_id(1)
        iq, ik = qi_ref[pair], ki_ref[pair]
        rows_q = pl.ds(pl.multiple_of(iq * t, t), t)

        @pl.when(pair == 0)
        def _():
            dq_acc[...] = jnp.zeros_like(dq_acc)
            dcq_ref[...] = jnp.zeros_like(dcq_ref)

        def step(diagonal):
            for g in range(grp):
                sl = slice(g * HEAD_DIM, (g + 1) * HEAD_DIM)
                qv = q_ref[:, sl]
                kv = k_ref[:, sl]
                dov = do_ref[:, sl]
                st = _fox_scores(kv, qv, -ckc_ref[g, :, 0:1], cqr_ref[g] - lse_ref[g], scale, diagonal, True)
                pt = jnp.exp(st)
                dv_acc[g] += _dot(pt, dov, NN)
                dpt = _dot(v_ref[:, sl], dov, NT)
                dst = pt * (dpt - dl_ref[g])
                dk_acc[g] += _dot(dst, qv, NN)
                dck_acc[g] += jnp.sum(dst, axis=1, keepdims=True)
                dq_acc[g, rows_q, :] += _dot(dst, kv, TN)
                dcq_ref[g, iq] += jnp.sum(dst, axis=0, keepdims=True)

        @pl.when(iq == ik)
        def _():
            dk_acc[...] = jnp.zeros_like(dk_acc)
            dv_acc[...] = jnp.zeros_like(dv_acc)
            dck_acc[...] = jnp.zeros_like(dck_acc)
            step(True)

        @pl.when(iq > ik)
        def _():
            step(False)

        @pl.when(iq == nk - 1)
        def _():
            for g in range(grp):
                sl = slice(g * HEAD_DIM, (g + 1) * HEAD_DIM)
                dk_ref[:, sl] = (dk_acc[g] * scale).astype(BF16)
                dv_ref[:, sl] = dv_acc[g].astype(BF16)
                dck_ref[g] = _as_row(-dck_acc[g])

        @pl.when(pair == npairs - 1)
        def _():
            for g in range(grp):
                head = pl.program_id(0) * grp + g

                def flush(i, c):
                    rows = pl.ds(pl.multiple_of(i * t, t), t)
                    stage[...] = (dq_acc[g, rows, :] * scale).astype(BF16)
                    cp = pltpu.make_async_copy(stage, dq_hbm.at[head, rows, :], sem)
                    cp.start()
                    cp.wait()
                    return c

                lax.fori_loop(0, nk, flush, 0)

    ng = heads // grp
    qblk = pl.BlockSpec((t, grp * HEAD_DIM), lambda h, p, qi, ki: (qi[p], h))
    qrow = pl.BlockSpec((grp, 1, t), lambda h, p, qi, ki: (h, 0, qi[p]))
    kblk = lambda off: pl.BlockSpec((t, grp * HEAD_DIM), lambda h, p, qi, ki: (ki[p], off + h))
    kout = pl.BlockSpec((t, grp * HEAD_DIM), lambda h, p, qi, ki: (ki[p], h))
    return _hosted(
        body, name="fox_bwd", grid=(ng, npairs), prefetch=(qi, ki),
        in_specs=[qblk, kblk(ng), kblk(2 * ng), qblk, qrow,
                  pl.BlockSpec((grp, t, LANES), lambda h, p, qi, ki: (h, ki[p], 0)), qrow, qrow],
        out_specs=[kout, kout, pl.BlockSpec((grp, 1, t), lambda h, p, qi, ki: (h, 0, ki[p])), ANY,
                   pl.BlockSpec((grp, nk, 1, t), lambda h, p, qi, ki: (h, 0, 0, 0))],
        scratch_shapes=[pltpu.VMEM((grp, t, HEAD_DIM), F32), pltpu.VMEM((grp, t, HEAD_DIM), F32),
                        pltpu.VMEM((grp, t, 1), F32), pltpu.VMEM((grp, s, HEAD_DIM), F32),
                        pltpu.VMEM((t, HEAD_DIM), BF16), pltpu.SemaphoreType.DMA],
        out_shape=[jax.ShapeDtypeStruct((s, heads * HEAD_DIM), BF16)] * 2 + [jax.ShapeDtypeStruct((heads, 1, s), F32)]
        + [jax.ShapeDtypeStruct((heads, s, HEAD_DIM), BF16), jax.ShapeDtypeStruct((heads, nk, 1, t), F32)],
        args=(proj, proj, proj, do, cum_row, cum_col, lse_row, delta_row), carry=carry)


def _shift_rows(xv, halo, j, forward):
    n = xv.shape[0]
    rid = lax.broadcasted_iota(jnp.int32, (8, xv.shape[1]), 0)
    if forward:
        xs = pltpu.roll(xv, n - j, 0)
        hs = pltpu.roll(halo, 8 - j, 0)
        edge = jnp.where(rid >= 8 - j, hs, xs[n - 8:, :])
        return jnp.concatenate([xs[:n - 8, :], edge], axis=0)
    xs = pltpu.roll(xv, j, 0)
    hs = pltpu.roll(halo, j, 0)
    edge = jnp.where(rid < j, hs, xs[:8, :])
    return jnp.concatenate([edge, xs[8:, :]], axis=0)


def _conv_silu(xv, halo, w):
    xc = w[CONV_W - 1:CONV_W, :] * xv
    for j in range(1, CONV_W):
        xc = xc + w[CONV_W - 1 - j:CONV_W - j, :] * _shift_rows(xv, halo, j, False)
    return xc, xc * _sigmoid(xc)


def _gdn_pre(proj, conv_w, heads):
    s = proj.shape[0]
    cw = 3 * heads * HEAD_DIM
    ts = _tile(s, 256)
    tb = ts // 8

    def body(x_ref, halo_ref, w_ref, q_ref, k_ref, v_ref):
        halo = jnp.where(pl.program_id(0) == 0, 0.0, halo_ref[...])
        _, y = _conv_silu(x_ref[...], halo, w_ref[...])
        for h in range(heads):
            for part, ref in enumerate((q_ref, k_ref, v_ref)):
                c0 = (part * heads + h) * HEAD_DIM
                blk = y[:, c0:c0 + HEAD_DIM]
                if part < 2:
                    blk = blk * lax.rsqrt(jnp.sum(blk * blk, axis=1, keepdims=True) + EPS)
                ref[h] = blk

    out = pl.BlockSpec((heads, ts, HEAD_DIM), lambda i: (0, i, 0))
    return _pcall(
        body, name="gdn_pre", grid=(s // ts,),
        in_specs=[pl.BlockSpec((ts, cw), lambda i: (i, 1)),
                  pl.BlockSpec((8, cw), lambda i: (jnp.maximum(i * tb - 1, 0), 1)),
                  pl.BlockSpec((CONV_W, cw), lambda i: (0, 0))],
        out_specs=[out, out, out],
        out_shape=[jax.ShapeDtypeStruct((heads, s, HEAD_DIM), F32)] * 3,
        compiler_params=_params("parallel"),
    )(proj, proj, conv_w)


def _gdn_pre_bwd_act(proj, conv_w, dq, dk, dv, heads):
    s = proj.shape[0]
    cw = 3 * heads * HEAD_DIM
    ts = _tile(s, 256)
    tb = ts // 8

    def body(x_ref, halo_ref, w_ref, dq_ref, dk_ref, dv_ref, dxc_ref, dw_ref):
        @pl.when(pl.program_id(0) == 0)
        def _():
            dw_ref[...] = jnp.zeros_like(dw_ref)

        xv = x_ref[...]
        halo = jnp.where(pl.program_id(0) == 0, 0.0, halo_ref[...])
        xc, y = _conv_silu(xv, halo, w_ref[...])
        sg = _sigmoid(xc)
        dsilu = sg * (1.0 + xc * (1.0 - sg))
        for h in range(heads):
            for part, ref in enumerate((dq_ref, dk_ref, dv_ref)):
                c0 = (part * heads + h) * HEAD_DIM
                g = ref[h]
                if part < 2:
                    blk = y[:, c0:c0 + HEAD_DIM]
                    r = lax.rsqrt(jnp.sum(blk * blk, axis=1, keepdims=True) + EPS)
                    g = r * g - blk * (r * r * r) * jnp.sum(g * blk, axis=1, keepdims=True)
                dxc_ref[:, c0:c0 + HEAD_DIM] = g * dsilu[:, c0:c0 + HEAD_DIM]
        dxc = dxc_ref[...]
        rows = [jnp.sum(dxc * (xv if j == 0 else _shift_rows(xv, halo, j, False)), axis=0, keepdims=True)
                for j in range(CONV_W)]
        dw_ref[...] += jnp.concatenate([rows[CONV_W - 1 - k] for k in range(CONV_W)]
                                       + [jnp.zeros((8 - CONV_W, cw), F32)], axis=0)

    hblk = pl.BlockSpec((heads, ts, HEAD_DIM), lambda i: (0, i, 0))
    return _pcall(
        body, name="gdn_pre_bwd_act", grid=(s // ts,),
        in_specs=[pl.BlockSpec((ts, cw), lambda i: (i, 1)),
                  pl.BlockSpec((8, cw), lambda i: (jnp.maximum(i * tb - 1, 0), 1)),
                  pl.BlockSpec((CONV_W, cw), lambda i: (0, 0)), hblk, hblk, hblk],
        out_specs=[pl.BlockSpec((ts, cw), lambda i: (i, 0)), pl.BlockSpec((8, cw), lambda i: (0, 0))],
        out_shape=[jax.ShapeDtypeStruct((s, cw), F32), jax.ShapeDtypeStruct((8, cw), F32)],
        compiler_params=_params("arbitrary"),
    )(proj, proj, conv_w, dq, dk, dv)


def _gdn_pre_bwd_conv(dxc, conv_w):
    s, cw = dxc.shape
    ts = _tile(s, 256)
    tb = ts // 8
    last = s // 8 - 1

    def body(g_ref, halo_ref, w_ref, dx_ref):
        gv = g_ref[...]
        w = w_ref[...]
        halo = jnp.where(pl.program_id(0) == s // ts - 1, 0.0, halo_ref[...])
        dx = w[CONV_W - 1:CONV_W, :] * gv
        for j in range(1, CONV_W):
            dx = dx + w[CONV_W - 1 - j:CONV_W - j, :] * _shift_rows(gv, halo, j, True)
        dx_ref[...] = dx.astype(BF16)

    return _pcall(
        body, name="gdn_pre_bwd_conv", grid=(s // ts,),
        in_specs=[pl.BlockSpec((ts, cw), lambda i: (i, 0)),
                  pl.BlockSpec((8, cw), lambda i: (jnp.minimum((i + 1) * tb, last), 0)),
                  pl.BlockSpec((CONV_W, cw), lambda i: (0, 0))],
        out_specs=pl.BlockSpec((ts, cw), lambda i: (i, 0)),
        out_shape=jax.ShapeDtypeStruct((s, cw), BF16),
        compiler_params=_params("parallel"),
    )(dxc, dxc, conv_w)


def _bdot(a, b, ca, cb):
    return lax.dot_general(a.astype(BF16), b.astype(BF16), (((ca,), (cb,)), ((0,), (0,))),
                           preferred_element_type=F32)


def _bdot_hp(a, b, ca, cb):
    ah = a.astype(BF16)
    al = (a - ah.astype(F32)).astype(BF16)
    bh = b.astype(BF16)
    bl = (b - bh.astype(F32)).astype(BF16)
    d = lambda p, q: lax.dot_general(p, q, (((ca,), (cb,)), ((0,), (0,))), preferred_element_type=F32)
    return d(ah, bh) + (d(ah, bl) + d(al, bh))


def _gdn_gates(small, a_lane, dt_lane, heads):
    lane = lax.broadcasted_iota(jnp.int32, small.shape, 1)
    za = small + dt_lane
    g_all = -jnp.exp(a_lane) * (jnp.maximum(za, 0.0) + jnp.log(1.0 + jnp.exp(-jnp.abs(za))))
    b_all = _sigmoid(small)
    pick = lambda v, l: jnp.sum(jnp.where(lane == l, v, 0.0), axis=1, keepdims=True)
    g = jnp.stack([pick(g_all, heads + h) for h in range(heads)], axis=0)
    beta = jnp.stack([pick(b_all, 2 * heads + h) for h in range(heads)], axis=0)
    return g, beta


def _chunk_masks(c):
    ii = lax.broadcasted_iota(jnp.int32, (1, c, c), 1)
    jj = lax.broadcasted_iota(jnp.int32, (1, c, c), 2)
    return ii >= jj, ii > jj, ii == jj


def _col_to_row(col, eye):
    return jnp.sum(jnp.where(eye, col, 0.0), axis=1, keepdims=True)


def _row_to_col(row, eye):
    return jnp.sum(jnp.where(eye, row, 0.0), axis=2, keepdims=True)


def _gdn_chunk(q, k, v, g, beta, state):
    c = q.shape[1]
    incl, strict, eye = _chunk_masks(c)
    g_row = _col_to_row(g, eye)
    gc_col = jnp.sum(jnp.where(incl, g_row, 0.0), axis=2, keepdims=True)
    gc_row = _col_to_row(gc_col, eye)
    gam = jnp.where(incl, jnp.exp(jnp.where(incl, gc_col - gc_row, NEG)), 0.0)
    egc = jnp.exp(gc_col)
    kb = k * beta
    vb = v * beta
    kbe = kb * egc
    low = jnp.where(strict, _bdot(kb, k, 2, 2), 0.0) * gam
    p = -low
    tinv = jnp.where(eye, 1.0, 0.0) + p
    width = 2
    while width < c:
        p = _bdot_hp(p, p, 2, 1)
        tinv = tinv + _bdot_hp(tinv, p, 2, 1)
        width *= 2
    u = _bdot(tinv, vb, 2, 1)
    w = _bdot(tinv, kbe, 2, 1)
    att = jnp.where(incl, _bdot(q, k, 2, 2), 0.0) * gam
    vn = u - _bdot(w, state, 2, 1)
    qe = q * egc
    o = _bdot(qe, state, 2, 1) + _bdot(att, vn, 2, 1)
    gl = jnp.sum(g, axis=1, keepdims=True)
    edec = jnp.exp(gl - gc_col)
    kdec = k * edec
    egl = jnp.exp(gl)
    new_state = state * egl + _bdot(kdec, vn, 1, 1)
    return dict(incl=incl, strict=strict, eye=eye, gam=gam, egc=egc, kb=kb, vb=vb, kbe=kbe, low=low, tinv=tinv, w=w,
                att=att, vn=vn, qe=qe, o=o, edec=edec, kdec=kdec, egl=egl, new_state=new_state)


def _gdn_load(q_ref, k_ref, v_ref, small_ref, a_ref, dt_ref, rows, heads):
    q = q_ref[:, rows, :] * (HEAD_DIM ** -0.5)
    g, beta = _gdn_gates(small_ref[rows, :], a_ref[...], dt_ref[...], heads)
    return q, k_ref[:, rows, :], v_ref[:, rows, :], g, beta


def _gdn_fwd(q, k, v, proj, z_blk, small_blk, a_lane, dt_lane, w_norm):
    heads, s, _ = q.shape
    c = min(GDN_CHUNK, s)
    r = _tile(s, 512)
    npb = r // c
    gw = heads * HEAD_DIM

    def body(q_ref, k_ref, v_ref, z_ref, small_ref, a_ref, dt_ref, w_ref, o_ref, st_ref, state):
        @pl.when(pl.program_id(0) == 0)
        def _():
            state[...] = jnp.zeros_like(state)

        def chunk(cb, carry):
            rows = pl.ds(pl.multiple_of(cb * c, c), c)
            qv, kv, vv, g, beta = _gdn_load(q_ref, k_ref, v_ref, small_ref, a_ref, dt_ref, rows, heads)
            st = state[...]
            st_ref[:, cb] = st
            res = _gdn_chunk(qv, kv, vv, g, beta, st)
            state[...] = res["new_state"]
            o = res["o"]
            rn = lax.rsqrt(jnp.mean(o * o, axis=2, keepdims=True) + EPS)
            zv = z_ref[rows, :]
            for h in range(heads):
                zh = zv[:, h * HEAD_DIM:(h + 1) * HEAD_DIM]
                o_ref[rows, h * HEAD_DIM:(h + 1) * HEAD_DIM] = (
                    o[h] * rn[h] * w_ref[...] * (zh * _sigmoid(zh))).astype(BF16)
            return carry

        lax.fori_loop(0, npb, chunk, 0)

    hblk = pl.BlockSpec((heads, r, HEAD_DIM), lambda i: (0, i, 0))
    row = pl.BlockSpec((1, LANES), lambda i: (0, 0))
    return _pcall(
        body, name="gdn_fwd", grid=(s // r,),
        in_specs=[hblk, hblk, hblk, pl.BlockSpec((r, gw), lambda i: (i, z_blk)),
                  pl.BlockSpec((r, LANES), lambda i: (i, small_blk)), row, row, row],
        out_specs=[pl.BlockSpec((r, gw), lambda i: (i, 0)),
                   pl.BlockSpec((heads, npb, HEAD_DIM, HEAD_DIM), lambda i: (0, i, 0, 0))],
        out_shape=[jax.ShapeDtypeStruct((s, gw), BF16),
                   jax.ShapeDtypeStruct((heads, s // c, HEAD_DIM, HEAD_DIM), F32)],
        scratch_shapes=[pltpu.VMEM((heads, HEAD_DIM, HEAD_DIM), F32)],
        compiler_params=_params("arbitrary"),
    )(q, k, v, proj, proj, a_lane, dt_lane, w_norm)


def _gdn_bwd(q, k, v, proj, z_blk, small_blk, a_lane, dt_lane, w_norm, states, do_cat, do_blk):
    heads, s, _ = q.shape
    c = min(GDN_CHUNK, s)
    r = _tile(s, 512)
    npb = r // c
    nb = s // r
    gw = heads * HEAD_DIM

    def body(q_ref, k_ref, v_ref, z_ref, small_ref, a_ref, dt_ref, w_ref, st_ref, do_ref,
             dq_ref, dk_ref, dv_ref, dz_ref, dsm_ref, da_ref, ddt_ref, dw_ref, dstate):
        @pl.when(pl.program_id(0) == 0)
        def _():
            dstate[...] = jnp.zeros_like(dstate)
            da_ref[...] = jnp.zeros_like(da_ref)
            ddt_ref[...] = jnp.zeros_like(ddt_ref)
            dw_ref[...] = jnp.zeros_like(dw_ref)

        def chunk(it, carry):
            cb = npb - 1 - it
            rows = pl.ds(pl.multiple_of(cb * c, c), c)
            qv, kv, vv, g, beta = _gdn_load(q_ref, k_ref, v_ref, small_ref, a_ref, dt_ref, rows, heads)
            st = st_ref[:, cb]
            f = _gdn_chunk(qv, kv, vv, g, beta, st)
            incl, strict, eye = f["incl"], f["strict"], f["eye"]
            o = f["o"]
            wv = w_ref[...]
            zv = z_ref[rows, :]
            dov = do_ref[rows, :]
            rn = lax.rsqrt(jnp.mean(o * o, axis=2, keepdims=True) + EPS)
            do_l, dw_acc = [], jnp.zeros((1, HEAD_DIM), F32)
            for h in range(heads):
                sl = slice(h * HEAD_DIM, (h + 1) * HEAD_DIM)
                zh, gh = zv[:, sl], dov[:, sl]
                sg = _sigmoid(zh)
                on = o[h] * rn[h]
                dz_ref[rows, sl] = (gh * (on * wv) * (sg * (1.0 + zh * (1.0 - sg)))).astype(BF16)
                gn = gh * (zh * sg)
                dw_acc = dw_acc + jnp.sum(gn * on, axis=0, keepdims=True)
                wg = gn * wv
                do_l.append(rn[h] * wg - o[h] * (rn[h] * rn[h] * rn[h]) * jnp.mean(wg * o[h], axis=1, keepdims=True))
            dw_ref[...] += dw_acc
            do = jnp.stack(do_l, axis=0)
            ds_out = dstate[...]
            dvn = _bdot(f["att"], do, 1, 1) + _bdot(f["kdec"], ds_out, 2, 1)
            datt = jnp.where(incl, _bdot(do, f["vn"], 2, 2), 0.0)
            dqe = _bdot(do, st, 2, 2)
            dstate[...] = _bdot(f["qe"], do, 1, 1) + f["egl"] * ds_out - _bdot(f["w"], dvn, 1, 1)
            dw = -_bdot(dvn, st, 2, 2)
            dkdec = _bdot(f["vn"], ds_out, 2, 2)
            t_kdec = jnp.sum(dkdec * f["kdec"], axis=2, keepdims=True)
            dgl = (jnp.sum(jnp.sum(st * ds_out, axis=2, keepdims=True), axis=1, keepdims=True) * f["egl"]
                   + jnp.sum(t_kdec, axis=1, keepdims=True))
            dgc = jnp.sum(dqe * f["qe"], axis=2, keepdims=True) - t_kdec
            dq = dqe * f["egc"]
            dk = dkdec * f["edec"]
            dtinv = _bdot(dvn, f["vb"], 2, 2) + _bdot(dw, f["kbe"], 2, 2)
            dvb = _bdot(f["tinv"], dvn, 1, 1)
            dkbe = _bdot(f["tinv"], dw, 1, 1)
            dkb = dkbe * f["egc"]
            dgc = dgc + jnp.sum(dkbe * f["kbe"], axis=2, keepdims=True)
            dlow = jnp.where(strict, -_bdot_hp(_bdot_hp(f["tinv"], dtinv, 1, 1), f["tinv"], 2, 2), 0.0)
            ml = dlow * f["gam"]
            dkb = dkb + _bdot(ml, kv, 2, 1)
            dk = dk + _bdot(ml, f["kb"], 1, 1)
            ma = datt * f["gam"]
            dq = dq + _bdot(ma, kv, 2, 1)
            dk = dk + _bdot(ma, qv, 1, 1)
            e = dlow * f["low"] + datt * f["att"]
            dgc = dgc + jnp.sum(e, axis=2, keepdims=True) - _row_to_col(jnp.sum(e, axis=1, keepdims=True), eye)
            dk = dk + beta * dkb
            dbeta = jnp.sum(dkb * kv, axis=2, keepdims=True) + jnp.sum(dvb * vv, axis=2, keepdims=True)
            dgc_row = _col_to_row(dgc, eye)
            dg = jnp.sum(jnp.where(incl, 0.0, dgc_row) + jnp.where(eye, dgc_row, 0.0), axis=2, keepdims=True) + dgl
            dq_ref[:, rows, :] = dq * (HEAD_DIM ** -0.5)
            dk_ref[:, rows, :] = dk
            dv_ref[:, rows, :] = beta * dvb
            small = small_ref[rows, :]
            lane = lax.broadcasted_iota(jnp.int32, small.shape, 1)
            dg_l = jnp.zeros(small.shape, F32)
            db_l = jnp.zeros(small.shape, F32)
            for h in range(heads):
                dg_l = dg_l + jnp.where(lane == heads + h, dg[h], 0.0)
                db_l = db_l + jnp.where(lane == 2 * heads + h, dbeta[h], 0.0)
            za = small + dt_ref[...]
            nexp = -jnp.exp(a_ref[...])
            softplus = jnp.maximum(za, 0.0) + jnp.log(1.0 + jnp.exp(-jnp.abs(za)))
            da_logit = dg_l * nexp * _sigmoid(za)
            sb = _sigmoid(small)
            dsm_ref[rows, :] = da_logit + db_l * sb * (1.0 - sb)
            ddt_ref[...] += jnp.sum(da_logit, axis=0, keepdims=True)
            da_ref[...] += jnp.sum(dg_l * nexp * softplus, axis=0, keepdims=True)
            return carry

        lax.fori_loop(0, npb, chunk, 0)

    rev = lambda i: nb - 1 - i
    hblk = pl.BlockSpec((heads, r, HEAD_DIM), lambda i: (0, rev(i), 0))
    row = pl.BlockSpec((1, LANES), lambda i: (0, 0))
    wide = lambda blk: pl.BlockSpec((r, gw), lambda i: (rev(i), blk))
    return _pcall(
        body, name="gdn_bwd", grid=(nb,),
        in_specs=[hblk, hblk, hblk, wide(z_blk), pl.BlockSpec((r, LANES), lambda i: (rev(i), small_blk)),
                  row, row, row, pl.BlockSpec((heads, npb, HEAD_DIM, HEAD_DIM), lambda i: (0, rev(i), 0, 0)),
                  wide(do_blk)],
        out_specs=[hblk, hblk, hblk, wide(0), pl.BlockSpec((r, LANES), lambda i: (rev(i), 0)), row, row, row],
        out_shape=[jax.ShapeDtypeStruct((heads, s, HEAD_DIM), F32)] * 3
        + [jax.ShapeDtypeStruct((s, gw), BF16), jax.ShapeDtypeStruct((s, LANES), F32)]
        + [jax.ShapeDtypeStruct((1, LANES), F32)] * 3,
        scratch_shapes=[pltpu.VMEM((heads, HEAD_DIM, HEAD_DIM), F32)],
        compiler_params=_params("arbitrary"),
    )(q, k, v, proj, proj, a_lane, dt_lane, w_norm, states, do_cat)


def _final(x, target, gf):
    s, d = x.shape
    ts = _tile(s, 512)

    def body(x_ref, t_ref, g_ref, loss_ref, dx_ref, dg_ref):
        @pl.when(pl.program_id(0) == 0)
        def _():
            loss_ref[...] = jnp.zeros_like(loss_ref)
            dg_ref[...] = jnp.zeros_like(dg_ref)

        xv = x_ref[...]
        gv = g_ref[...]
        r = lax.rsqrt(jnp.mean(xv * xv, axis=-1, keepdims=True) + EPS)
        xn = xv * r
        err = xn * gv - t_ref[...]
        per_tok = jnp.mean(err * err, axis=-1, keepdims=True)
        loss_ref[...] += 0.5 * jnp.sum(per_tok, axis=0, keepdims=True)
        dy = err * (1.0 / d)
        dg_ref[...] += jnp.sum(dy * xn, axis=0, keepdims=True)
        dxn = dy * gv
        dx_ref[...] = r * (dxn - xn * jnp.mean(dxn * xn, axis=-1, keepdims=True))

    blk = pl.BlockSpec((ts, d), lambda i: (i, 0))
    row = pl.BlockSpec((1, d), lambda i: (0, 0))
    return _pcall(
        body, name="final_loss", grid=(s // ts,),
        in_specs=[blk, blk, row], out_specs=[pl.BlockSpec((1, LANES), lambda i: (0, 0)), blk, row],
        out_shape=[jax.ShapeDtypeStruct((1, LANES), F32), jax.ShapeDtypeStruct((s, d), F32),
                   jax.ShapeDtypeStruct((1, d), F32)],
        compiler_params=_params("arbitrary"),
    )(x, target, gf)


def _adamw(parts, w, m, v, name):
    npart, rows, cols = parts.shape
    tr = _tile(rows, max(8, ADAM_BLOCK_BYTES // (4 * npart * cols)))
    c1 = 1.0 - ADAM_B1 ** ADAM_STEP
    c2 = 1.0 - ADAM_B2 ** ADAM_STEP

    def body(p_ref, w_ref, m_ref, v_ref, g_ref, d_ref, mo_ref, vo_ref):
        g = p_ref[0].astype(F32)
        for i in range(1, npart):
            g = g + p_ref[i].astype(F32)
        mn = ADAM_B1 * m_ref[...] + (1.0 - ADAM_B1) * g
        vn = ADAM_B2 * v_ref[...] + (1.0 - ADAM_B2) * (g * g)
        g_ref[...] = g
        mo_ref[...] = mn
        vo_ref[...] = vn
        d_ref[...] = -ADAM_LR * ((mn / c1) / (jnp.sqrt(vn / c2) + ADAM_EPS) + ADAM_WD * w_ref[...])

    blk = pl.BlockSpec((tr, cols), lambda i: (i, 0))
    return _pcall(
        body, name=name, grid=(rows // tr,),
        in_specs=[pl.BlockSpec((npart, tr, cols), lambda i: (0, i, 0)), blk, blk, blk],
        out_specs=[blk] * 4, out_shape=[jax.ShapeDtypeStruct((rows, cols), F32)] * 4,
        compiler_params=_params("parallel"),
    )(parts, w, m, v)


def _adamw_layers(parts0, parts1, w, m, v, name):
    npart, rows, cols = parts0.shape
    tr = _tile(rows, max(8, ADAM_BLOCK_BYTES // (4 * npart * cols)))
    nb = rows // tr
    c1 = 1.0 - ADAM_B1 ** ADAM_STEP
    c2 = 1.0 - ADAM_B2 ** ADAM_STEP

    def body(p0_ref, p1_ref, w_ref, m_ref, v_ref, g_ref, d_ref, mo_ref, vo_ref):
        def update(p_ref):
            g = p_ref[0].astype(F32)
            for i in range(1, npart):
                g = g + p_ref[i].astype(F32)
            mn = ADAM_B1 * m_ref[...] + (1.0 - ADAM_B1) * g
            vn = ADAM_B2 * v_ref[...] + (1.0 - ADAM_B2) * (g * g)
            g_ref[...] = g
            mo_ref[...] = mn
            vo_ref[...] = vn
            d_ref[...] = -ADAM_LR * ((mn / c1) / (jnp.sqrt(vn / c2) + ADAM_EPS) + ADAM_WD * w_ref[...])

        @pl.when(pl.program_id(0) == 0)
        def _():
            update(p0_ref)

        @pl.when(pl.program_id(0) == 1)
        def _():
            update(p1_ref)

    blk = pl.BlockSpec((None, None, tr, cols), lambda l, i: (0, l, i, 0))
    p0 = pl.BlockSpec((npart, tr, cols), lambda l, i: (0, jnp.where(l == 0, i, nb - 1), 0))
    p1 = pl.BlockSpec((npart, tr, cols), lambda l, i: (0, jnp.where(l == 0, 0, i), 0))
    return _pcall(
        body, name=name, grid=(2, nb), in_specs=[p0, p1, blk, blk, blk], out_specs=[blk] * 4,
        out_shape=[jax.ShapeDtypeStruct(w.shape, F32)] * 4, compiler_params=_params("arbitrary", "arbitrary"),
    )(parts0, parts1, w, m, v)


def _pad_lanes(v, n=LANES, at=0):
    return jnp.pad(v, ((0, 0), (at, n - at - v.shape[1])))


def _my_cols(a, me, width):
    return lax.dynamic_slice_in_dim(a, me * width, width, axis=a.ndim - 1)


def kernel(x, c, ada_w, ada_b, norm_g, ffn_w_gate, ffn_w_up, ffn_w_down, w_in, w_out, fox_f_bias, fox_out_norm, gdn_conv, gdn_A_log, gdn_dt_bias, gdn_out_norm, final_norm, loss_target, m_ada_w, m_ada_b, m_norm_g, m_ffn_w_gate, m_ffn_w_up, m_ffn_w_down, m_w_in, m_w_out, m_fox_f_bias, m_fox_out_norm, m_gdn_conv, m_gdn_A_log, m_gdn_dt_bias, m_gdn_out_norm, m_final_norm, v_ada_w, v_ada_b, v_norm_g, v_ffn_w_gate, v_ffn_w_up, v_ffn_w_down, v_w_in, v_w_out, v_fox_f_bias, v_fox_out_norm, v_gdn_conv, v_gdn_A_log, v_gdn_dt_bias, v_gdn_out_norm, v_final_norm):
    me = _linear(_mesh_pos())
    x0 = x[0]
    s, d = x0.shape
    heads = d // (2 * HEAD_DIM)
    fw = heads * HEAD_DIM
    ng = norm_g.shape[-1]
    ncv = gdn_conv.shape[-1]
    nada = ada_w.shape[-1]
    in_w = w_in.shape[-1] * N_DEV
    in_pad = -(-in_w // 512) * 512

    pack = jnp.concatenate([c, norm_g[0].reshape(1, 3 * ng), gdn_conv[0].reshape(1, CONV_W * ncv)], axis=1)
    pack_all = _gather_row(pack, "gather_small_params")
    c_all = pack_all[:, :d]
    g_all = pack_all[:, d:d + 3 * ng].reshape(N_DEV, 3, ng).transpose(1, 0, 2).reshape(3, d)
    conv_all = pack_all[:, d + 3 * ng:].reshape(N_DEV, CONV_W, ncv).transpose(1, 0, 2).reshape(CONV_W, 3 * fw)

    mod_blk = _ada_fwd(c_all, ada_w[0], _my_cols(ada_b, me, nada))
    mod_all = _exchange([mod_blk], scatter=False, in_vmem=True, name="gather_mod")[0]
    mod = lax.dynamic_slice_in_dim(mod_all, me, 1, axis=1).reshape(N_MOD, d)
    sh1, sc1, gt1, sh2, sc2, gt2, sh3, sc3, gt3 = [mod[i:i + 1] for i in range(N_MOD)]

    wg_sh, wu_sh, wd_sh = [w[0].astype(BF16) for w in (ffn_w_gate, ffn_w_up, ffn_w_down)]
    layer = lambda w, i: w[i:i + 1]
    wg0, wu0 = _gather_via_sibling([layer(wg_sh, 0), layer(wu_sh, 0)], "gather_ffn1_up_weights")
    small_blk = 7 * heads

    bias_lane = _pad_lanes(fox_f_bias)
    a_lane = _pad_lanes(gdn_A_log, at=heads)
    dt_lane = _pad_lanes(gdn_dt_bias, at=heads)

    h1 = _norm_mod(x0, g_all[0:1], sc1, sh1, "norm_mod_1")
    (a1, b1, s1), (wd0, wout_g) = _ffn_up(h1, wg0, wu0, 0, "ffn1_up",
                                          carry=([layer(wd_sh, 0), w_out[0].astype(BF16)], False))
    (f1, x1), (win_g,) = _ffn_down(s1, wd0, 0, x0, gt1, "ffn1_down", carry=([w_in[0].astype(BF16)], False))
    win_full = win_g.transpose(1, 0, 2).reshape(d, in_w)
    o_f, o_qkv, o_a, o_z = 3 * fw, 3 * fw + heads, 6 * fw + heads, 6 * fw + 3 * heads
    win_al = jnp.concatenate(
        [win_full[:, :o_f], win_full[:, o_qkv:o_a], win_full[:, o_z:], win_full[:, o_f:o_qkv],
         win_full[:, o_a:o_z], jnp.zeros((d, in_pad - in_w), BF16)], axis=1)
    wout_full = wout_g.reshape(d, d)

    h2 = _norm_mod(x1, g_all[1:2], sc2, sh2, "norm_mod_2")
    proj = _mm(h2, win_al, name="in_proj", tn=1536)
    cum = _fox_gate(proj, small_blk, bias_lane)
    cum_t = cum[:, :heads].T
    cum_row = cum_t[:, None, :]
    cum_col = jnp.broadcast_to(cum_t[:, :, None], (heads, s, LANES))
    (o_raw, lse, lse_row, o_fox), (wg1, wu1, wd1) = _fox_fwd(
        proj, cum_col, cum_row, fox_out_norm, heads,
        carry=([layer(wg_sh, 1), layer(wu_sh, 1), layer(wd_sh, 1)], False))
    qg, kg, vg = _gdn_pre(proj, conv_all, heads)
    o_gdn, states = _gdn_fwd(qg, kg, vg, proj, 6, small_blk, a_lane, dt_lane, gdn_out_norm)
    o_cat = jnp.concatenate([o_fox, o_gdn], axis=1)
    mix, x2 = _mm(o_cat, wout_full, name="out_proj", residual=(x1, gt2))

    h3 = _norm_mod(x2, g_all[2:3], sc3, sh3, "norm_mod_3")
    (a3, b3, s3), _ = _ffn_up(h3, wg1, wu1, 0, "ffn2_up")
    (f3, x3), _ = _ffn_down(s3, wd1, 0, x2, gt3, "ffn2_down")

    loss_row, dx3, d_final = _final(x3, loss_target[0], final_norm.reshape(1, d))
    loss = lax.psum(loss_row[0, 0], MESH_AXES)

    df3, dgt3 = _gate_bwd(dx3, f3, gt3, MACARON_W, "ffn2_gate_bwd")
    (da3, db3), _ = _ffn_bwd_act(df3, wd1, 0, a3, b3, "ffn2_bwd_act")
    (dwd2,), _ = _ffn_bwd_wd(s3, df3, "ffn2_bwd_wd")
    (dh3,), (r_wd2,) = _ffn_bwd_h(da3, db3, wg1, wu1, 0, "ffn2_bwd_h", carry=([dwd2], True))
    dwg2, dwu2 = _ffn_bwd_wgu(h3, da3, db3, "ffn2_bwd_wgu")
    dx2, dsh3, dsc3, dg3 = _norm_mod_bwd(x2, dh3, dx3, g_all[2:3], sc3, "norm_mod_3_bwd")

    dmix, dgt2 = _gate_bwd(dx2, mix, gt2, 1.0, "mix_gate_bwd")
    do_cat = _mm(dmix, wout_full, tb=True, name="out_proj_bwd_x")
    dwout = _mm(o_cat, dmix, ta=True, out_dtype=BF16, name="out_proj_bwd_w", tk=512)
    do_fox, delta, delta_row, d_foxw = _fox_prep_bwd(do_cat, o_raw, fox_out_norm, heads)
    (dk_f, dv_f, dcum_k, dq_heads, dcum_q), (r_wg2, r_wu2, r_wout) = _fox_bwd(
        proj, do_fox, cum_col, cum_row, lse_row, delta_row, heads,
        carry=([dwg2, dwu2, dwout.reshape(N_DEV, d // N_DEV, d)], True))
    dq_f = dq_heads.transpose(1, 0, 2).reshape(s, fw)
    head_lanes = lambda t: jnp.pad(t.reshape(heads, s).T, ((0, 0), (0, LANES - heads)))
    dsm_fox, d_fbias = _fox_gate_bwd(head_lanes(dcum_q), head_lanes(dcum_k), proj, small_blk, bias_lane)
    dqg, dkg, dvg, dz, dsm_gdn, d_alog, d_dt, d_gdnw = _gdn_bwd(
        qg, kg, vg, proj, 6, small_blk, a_lane, dt_lane, gdn_out_norm, states, do_cat, 1)
    dxc, d_conv = _gdn_pre_bwd_act(proj, conv_all, dqg, dkg, dvg, heads)
    dqkv = _gdn_pre_bwd_conv(dxc, conv_all)
    dsmall = (dsm_fox + dsm_gdn).astype(BF16)
    dproj = jnp.concatenate([dq_f, dk_f, dv_f, dqkv, dz, dsmall, jnp.zeros((s, in_pad - 7 * fw - LANES), BF16)], axis=1)
    dh2 = _mm(dproj, win_al, tb=True, name="in_proj_bwd_x", tk=1536)
    dwin_al = _mm(h2, dproj, ta=True, out_dtype=BF16, name="in_proj_bwd_w", tm=2048, tn=1536, tk=1024)
    dwin_full = jnp.concatenate(
        [dwin_al[:, :o_f], dwin_al[:, 7 * fw:7 * fw + heads], dwin_al[:, o_f:o_f + 3 * fw],
         dwin_al[:, 7 * fw + heads:7 * fw + 3 * heads], dwin_al[:, 6 * fw:7 * fw]], axis=1)
    dwin_parts = dwin_full.reshape(d, N_DEV, in_w // N_DEV).transpose(1, 0, 2)
    dx1, dsh2, dsc2, dg2 = _norm_mod_bwd(x1, dh2, dx2, g_all[1:2], sc2, "norm_mod_2_bwd")

    df1, dgt1 = _gate_bwd(dx1, f1, gt1, MACARON_W, "ffn1_gate_bwd")
    (da1, db1), (r_win,) = _ffn_bwd_act(df1, wd0, 0, a1, b1, "ffn1_bwd_act", carry=([dwin_parts], True))
    dwg1, dwu1 = _ffn_bwd_wgu(h1, da1, db1, "ffn1_bwd_wgu")
    (dwd1,), (r_wg1,) = _ffn_bwd_wd(s1, df1, "ffn1_bwd_wd", carry=([dwg1], True))
    (dh1,), (r_wu1, r_wd1) = _ffn_bwd_h(da1, db1, wg0, wu0, 0, "ffn1_bwd_h", carry=([dwu1, dwd1], True))
    grad_x, dsh1, dsc1, dg1 = _norm_mod_bwd(x0, dh1, dx1, g_all[0:1], sc1, "norm_mod_1_bwd")

    dmod = jnp.concatenate([dsh1, dsc1, dgt1, dsh2, dsc2, dgt2, dsh3, dsc3, dgt3], axis=1)
    dmod_all = _gather_row(dmod, "gather_dmod")
    ct_pad = jnp.pad(c_all.T, ((0, 0), (0, LANES - N_DEV)))
    dmod_mine = jnp.pad(_my_cols(dmod_all, me, nada), ((0, LANES - N_DEV), (0, 0)))
    g_ada_w = _ada_bwd(ct_pad, dmod_mine)

    g_small_cols = [d_fbias, d_foxw, d_alog[:, heads:], d_dt[:, heads:], d_gdnw]
    small_part = jnp.concatenate(
        [_pad_lanes(v[:, :LANES]) for v in g_small_cols]
        + [d_final, dg1, dg2, dg3] + [d_conv[k:k + 1] for k in range(CONV_W)], axis=1)
    small_all = _gather_row(small_part, "gather_small_grads")
    off = 5 * LANES
    w_small = jnp.concatenate(
        [_pad_lanes(fox_f_bias), fox_out_norm, _pad_lanes(gdn_A_log), _pad_lanes(gdn_dt_bias), gdn_out_norm,
         final_norm.reshape(1, d)], axis=1)
    m_small = jnp.concatenate(
        [_pad_lanes(m_fox_f_bias), m_fox_out_norm, _pad_lanes(m_gdn_A_log), _pad_lanes(m_gdn_dt_bias),
         m_gdn_out_norm, m_final_norm.reshape(1, d)], axis=1)
    v_small = jnp.concatenate(
        [_pad_lanes(v_fox_f_bias), v_fox_out_norm, _pad_lanes(v_gdn_A_log), _pad_lanes(v_gdn_dt_bias),
         v_gdn_out_norm, v_final_norm.reshape(1, d)], axis=1)
    rep = _adamw(small_all[:, None, :off + d], w_small, m_small, v_small, "adamw_replicated")
    ab = _adamw(dmod_all[:, None, :], ada_b, m_ada_b, v_ada_b, "adamw_ada_b")
    g_ng = small_all[:, off + d:off + 4 * d].reshape(N_DEV, 3, d)
    ngs = _adamw(_my_cols(g_ng, me, ng), norm_g[0], m_norm_g[0], v_norm_g[0], "adamw_norm_g")
    g_cv = small_all[:, off + 4 * d:].reshape(N_DEV, CONV_W, 3 * fw)
    cvs = _adamw(_my_cols(g_cv, me, ncv), gdn_conv[0], m_gdn_conv[0], v_gdn_conv[0], "adamw_gdn_conv")

    wgs = _adamw_layers(r_wg1, r_wg2, ffn_w_gate, m_ffn_w_gate, v_ffn_w_gate, "adamw_w_gate")
    wus = _adamw_layers(r_wu1, r_wu2, ffn_w_up, m_ffn_w_up, v_ffn_w_up, "adamw_w_up")
    wds = _adamw_layers(r_wd1, r_wd2, ffn_w_down, m_ffn_w_down, v_ffn_w_down, "adamw_w_down")
    wis = [o[None] for o in _adamw(r_win, w_in[0], m_w_in[0], v_w_in[0], "adamw_w_in")]
    wos = [o[None] for o in _adamw(r_wout, w_out[0], m_w_out[0], v_w_out[0], "adamw_w_out")]
    adas = [o[None] for o in _adamw(g_ada_w[None], ada_w[0], m_ada_w[0], v_ada_w[0], "adamw_ada_w")]
    ngs = [o[None] for o in ngs]
    cvs = [o[None] for o in cvs]

    def rep_piece(i, lo, width):
        return rep[i][:, lo:lo + width]

    nh = fox_f_bias.shape[1]
    outs = []
    for i in range(4):
        outs.append([adas[i], ab[i], ngs[i], wgs[i], wus[i], wds[i], wis[i], wos[i],
                     rep_piece(i, 0, nh), rep_piece(i, LANES, HEAD_DIM), cvs[i], rep_piece(i, 2 * LANES, nh),
                     rep_piece(i, 3 * LANES, nh), rep_piece(i, 4 * LANES, HEAD_DIM), rep_piece(i, off, d).reshape(d)])
    return (loss, grad_x[None], *outs[0], *outs[1], *outs[2], *outs[3])
```

```python
import math

import numpy as np
import jax
import jax.numpy as jnp
from jax import lax
from jax.experimental import pallas as pl
from jax.experimental.pallas import tpu as pltpu

F32 = jnp.float32
BF16 = jnp.bfloat16

N_DEV = 8
MESH_AXES = ("x", "y", "c")
LANES = 128
HEAD_DIM = 128
GDN_CHUNK = 64
CONV_W = 4
N_MOD = 9
MACARON_W = 0.5
EPS = 1e-6
NEG = -1e30
VMEM_LIMIT_BYTES = 56 * 2 ** 20
ADAM_BLOCK_BYTES = 4 * 2 ** 20
FOX_BWD_HEADS_PER_STEP = 4
FOX_HEADS_PER_STEP = 8

ADAM_LR = 0.001
ADAM_B1 = 0.9
ADAM_B2 = 0.999
ADAM_EPS = 1e-08
ADAM_WD = 0.01
ADAM_STEP = 10

MESH_ID = pl.DeviceIdType.MESH
ANY = pl.BlockSpec(memory_space=pl.ANY)
VMEM = pl.BlockSpec(memory_space=pltpu.VMEM)


def _pcall(body, **kw):
    return pl.pallas_call(body, **kw)


def _params(*semantics):
    return pltpu.CompilerParams(dimension_semantics=semantics, vmem_limit_bytes=VMEM_LIMIT_BYTES)


def _tile(n, pref):
    if n % pref == 0 and pref % 8 == 0:
        return pref
    t = 1 << (max(1, min(n, pref)).bit_length() - 1)
    while n % t:
        t //= 2
    return t if t % 8 == 0 else n


def _sigmoid(x):
    return 1.0 / (1.0 + jnp.exp(-x))


def _dot(a, b, dims):
    return lax.dot_general(a.astype(BF16), b.astype(BF16), (dims, ((), ())), preferred_element_type=F32)


NN = ((1,), (0,))
NT = ((1,), (1,))
TN = ((0,), (0,))


def _split3(x):
    hi = x.astype(BF16)
    r1 = x - hi.astype(F32)
    mid = r1.astype(BF16)
    lo = (r1 - mid.astype(F32)).astype(BF16)
    return hi, mid, lo


def _dot_exact_lhs(m_bf16, x, dims=NN):
    hi, mid, lo = _split3(x)
    d = lambda p: lax.dot_general(m_bf16, p, (dims, ((), ())), preferred_element_type=F32)
    return d(hi) + (d(mid) + d(lo))


def _dot_hp(a, b, dims):
    ah = a.astype(BF16)
    al = (a - ah.astype(F32)).astype(BF16)
    bh = b.astype(BF16)
    bl = (b - bh.astype(F32)).astype(BF16)
    d = lambda p, q: lax.dot_general(p, q, (dims, ((), ())), preferred_element_type=F32)
    return d(ah, bh) + (d(ah, bl) + d(al, bh))


def _mesh_pos():
    return lax.axis_index("x"), lax.axis_index("y"), lax.axis_index("c")


def _peer(pos, mask):
    x, y, c = pos
    return (1 - x if mask & 4 else x, 1 - y if mask & 2 else y, 1 - c if mask & 1 else c)


def _linear(pos):
    return 4 * pos[0] + 2 * pos[1] + pos[2]


def _exchange_copies(ins, outs, sems, scatter, with_receives=True):
    send_sems, recv_sems, local_sems = sems
    pos = _mesh_pos()
    me = _linear(pos)
    local, sends, recvs = [], [], []
    for i in range(len(ins)):
        src = ins[i].at[me] if scatter else ins[i]
        local.append(pltpu.make_async_copy(src, outs[i].at[me], local_sems.at[i]))
    for mask in range(1, N_DEV):
        peer = _peer(pos, mask)
        for i in range(len(ins)):
            sem = dict(send_sem=send_sems.at[i, mask - 1], recv_sem=recv_sems.at[i, mask - 1],
                       device_id=peer, device_id_type=MESH_ID)
            sends.append(pltpu.make_async_remote_copy(
                src_ref=ins[i].at[_linear(peer)] if scatter else ins[i], dst_ref=outs[i].at[me], **sem))
            if with_receives:
                recvs.append(pltpu.make_async_remote_copy(
                    src_ref=ins[i].at[me] if scatter else ins[i], dst_ref=outs[i].at[_linear(peer)], **sem))
    return local, sends, recvs


def _exchange_start(ins, outs, sems, scatter):
    local, sends, _ = _exchange_copies(ins, outs, sems, scatter, with_receives=False)
    for cp in local + sends:
        cp.start()


def _exchange_wait(ins, outs, sems, scatter):
    local, sends, recvs = _exchange_copies(ins, outs, sems, scatter)
    for cp in recvs:
        cp.wait_recv()
    for cp in sends:
        cp.wait_send()
    for cp in local:
        cp.wait()


def _exchange_sems(n):
    return [pltpu.SemaphoreType.DMA((n, N_DEV - 1)), pltpu.SemaphoreType.DMA((n, N_DEV - 1)),
            pltpu.SemaphoreType.DMA((n,))]


def _exchange_shapes(arrays, scatter):
    return [jax.ShapeDtypeStruct(a.shape if scatter else (N_DEV,) + a.shape, a.dtype) for a in arrays]


def _exchange(arrays, *, scatter, in_vmem, name):
    n = len(arrays)

    def body(*refs):
        ins, outs, sems = refs[:n], refs[n:2 * n], refs[2 * n:]
        _exchange_start(ins, outs, sems, scatter)
        _exchange_wait(ins, outs, sems, scatter)

    spec = VMEM if in_vmem else ANY
    outs = _pcall(
        body, name=name, out_shape=_exchange_shapes(arrays, scatter),
        in_specs=[spec] * n, out_specs=[spec] * n, scratch_shapes=_exchange_sems(n),
    )(*arrays)
    return list(outs)


def _gather_via_sibling(arrays, name):
    n = len(arrays)

    def body(*refs):
        ins, outs = refs[:n], refs[n:2 * n]
        send_sems, recv_sems, local_sems = refs[2 * n:]
        x, y, c = _mesh_pos()
        me, sibling = (x, y, c), (x, y, 1 - c)
        chips = [(1 - x, y), (x, 1 - y), (1 - x, 1 - y)]

        def copy(i, k, block, to, from_input=False):
            return pltpu.make_async_remote_copy(
                src_ref=ins[i] if from_input else outs[i].at[_linear(block)], dst_ref=outs[i].at[_linear(block)],
                send_sem=send_sems.at[i, k], recv_sem=recv_sems.at[i, k], device_id=to, device_id_type=MESH_ID)

        mine = [pltpu.make_async_copy(ins[i], outs[i].at[_linear(me)], local_sems.at[i]) for i in range(n)]
        first = []
        for i in range(n):
            first.append(copy(i, 0, me, sibling, True))
            first += [copy(i, 1 + j, me, (*chip, c), True) for j, chip in enumerate(chips)]
        for cp in mine + first:
            cp.start()
        passed = []
        for j, chip in enumerate(chips):
            for i in range(n):
                copy(i, 1 + j, (*chip, c), me).wait_recv()
                cp = copy(i, 4 + j, (*chip, c), sibling)
                cp.start()
                passed.append(cp)
        for i in range(n):
            copy(i, 0, sibling, me).wait_recv()
            for j, chip in enumerate(chips):
                copy(i, 4 + j, (*chip, 1 - c), me).wait_recv()
        for cp in first + passed:
            cp.wait_send()
        for cp in mine:
            cp.wait()

    outs = _pcall(
        body, name=name, out_shape=_exchange_shapes(arrays, False), in_specs=[ANY] * n, out_specs=[ANY] * n,
        scratch_shapes=_exchange_sems(n),
    )(*arrays)
    return list(outs)


def _hosted(body, *, name, grid, in_specs, out_specs, out_shape, args, scratch_shapes=(), prefetch=(), carry=None):
    n_in, n_out, n_scr, n_pre = len(in_specs), len(out_shape), len(scratch_shapes), len(prefetch)
    arrays, scatter = carry if carry is not None else ([], False)
    n = len(arrays)

    def wrapped(*refs):
        pre, r = refs[:n_pre], refs[n_pre:]
        host_in, comm_in = r[:n_in], r[n_in:n_in + n]
        r = r[n_in + n:]
        host_out, comm_out = r[:n_out], r[n_out:n_out + n]
        r = r[n_out + n:]
        host_scr, sems = r[:n_scr], r[n_scr:]
        if n:
            first = pl.program_id(0) == 0
            last = pl.program_id(0) == grid[0] - 1
            for ax in range(1, len(grid)):
                first = first & (pl.program_id(ax) == 0)
                last = last & (pl.program_id(ax) == grid[ax] - 1)

            @pl.when(first)
            def _():
                _exchange_start(comm_in, comm_out, sems, scatter)

        body(*pre, *host_in, *host_out, *host_scr)
        if n:
            @pl.when(last)
            def _():
                _exchange_wait(comm_in, comm_out, sems, scatter)

    grid_spec = pltpu.PrefetchScalarGridSpec(
        num_scalar_prefetch=n_pre, grid=grid, in_specs=list(in_specs) + [ANY] * n,
        out_specs=list(out_specs) + [ANY] * n,
        scratch_shapes=list(scratch_shapes) + (_exchange_sems(n) if n else []))
    outs = _pcall(
        wrapped, name=name, grid_spec=grid_spec, out_shape=list(out_shape) + _exchange_shapes(arrays, scatter),
        compiler_params=_params(*(["arbitrary"] * len(grid))),
    )(*prefetch, *args, *arrays)
    return list(outs[:n_out]), list(outs[n_out:])


def _gather_row(v, name):
    return _exchange([v], scatter=False, in_vmem=True, name=name)[0].reshape(N_DEV, v.shape[1])


def _ada_fwd(c_all, w, b):
    d, n = w.shape
    tn = _tile(n, 256)

    def body(c_ref, w_ref, b_ref, o_ref):
        cv = c_ref[...]
        cond = cv * _sigmoid(cv)
        o_ref[...] = _dot_hp(cond, w_ref[...], NN) + b_ref[...]

    return _pcall(
        body, name="ada_fwd", grid=(n // tn,),
        in_specs=[pl.BlockSpec((N_DEV, d), lambda j: (0, 0)), pl.BlockSpec((d, tn), lambda j: (0, j)),
                  pl.BlockSpec((1, tn), lambda j: (0, j))],
        out_specs=pl.BlockSpec((N_DEV, tn), lambda j: (0, j)),
        out_shape=jax.ShapeDtypeStruct((N_DEV, n), F32), compiler_params=_params("parallel"),
    )(c_all, w, b)


def _ada_bwd(ct_pad, dmod_pad):
    d = ct_pad.shape[0]
    n = dmod_pad.shape[1]
    tn = _tile(n, 256)

    def body(c_ref, g_ref, o_ref):
        cv = c_ref[...]
        cond = cv * _sigmoid(cv)
        o_ref[...] = _dot_hp(cond, g_ref[...], NN)

    return _pcall(
        body, name="ada_bwd", grid=(n // tn,),
        in_specs=[pl.BlockSpec((d, LANES), lambda j: (0, 0)), pl.BlockSpec((LANES, tn), lambda j: (0, j))],
        out_specs=pl.BlockSpec((d, tn), lambda j: (0, j)),
        out_shape=jax.ShapeDtypeStruct((d, n), F32), compiler_params=_params("parallel"),
    )(ct_pad, dmod_pad)


def _norm_mod(x, g, sc, sh, name):
    s, d = x.shape
    ts = _tile(s, 512)

    def body(x_ref, g_ref, sc_ref, sh_ref, h_ref):
        xv = x_ref[...]
        r = lax.rsqrt(jnp.mean(xv * xv, axis=-1, keepdims=True) + EPS)
        h_ref[...] = (xv * r * g_ref[...] * (1.0 + sc_ref[...]) + sh_ref[...]).astype(BF16)

    row = pl.BlockSpec((1, d), lambda i: (0, 0))
    return _pcall(
        body, name=name, grid=(s // ts,),
        in_specs=[pl.BlockSpec((ts, d), lambda i: (i, 0)), row, row, row],
        out_specs=pl.BlockSpec((ts, d), lambda i: (i, 0)),
        out_shape=jax.ShapeDtypeStruct((s, d), BF16), compiler_params=_params("parallel"),
    )(x, g, sc, sh)


def _norm_mod_bwd(x, dh, dx_out, g, sc, name):
    s, d = x.shape
    ts = _tile(s, 512)

    def body(x_ref, dh_ref, dxo_ref, g_ref, sc_ref, dx_ref, dsh_ref, dsc_ref, dg_ref):
        @pl.when(pl.program_id(0) == 0)
        def _():
            dsh_ref[...] = jnp.zeros_like(dsh_ref)
            dsc_ref[...] = jnp.zeros_like(dsc_ref)
            dg_ref[...] = jnp.zeros_like(dg_ref)

        xv = x_ref[...]
        dh_v = dh_ref[...]
        gv = g_ref[...]
        one_sc = 1.0 + sc_ref[...]
        r = lax.rsqrt(jnp.mean(xv * xv, axis=-1, keepdims=True) + EPS)
        xn = xv * r
        dxn = dh_v * (gv * one_sc)
        dx_ref[...] = dxo_ref[...] + r * (dxn - xn * jnp.mean(dxn * xn, axis=-1, keepdims=True))
        t = dh_v * xn
        dsh_ref[...] += jnp.sum(dh_v, axis=0, keepdims=True)
        dsc_ref[...] += jnp.sum(t * gv, axis=0, keepdims=True)
        dg_ref[...] += jnp.sum(t * one_sc, axis=0, keepdims=True)

    blk = pl.BlockSpec((ts, d), lambda i: (i, 0))
    row = pl.BlockSpec((1, d), lambda i: (0, 0))
    return _pcall(
        body, name=name, grid=(s // ts,),
        in_specs=[blk, blk, blk, row, row], out_specs=[blk, row, row, row],
        out_shape=[jax.ShapeDtypeStruct((s, d), F32)] + [jax.ShapeDtypeStruct((1, d), F32)] * 3,
        compiler_params=_params("arbitrary"),
    )(x, dh, dx_out, g, sc)


def _gate_bwd(dx, f, gt, k, name):
    s, d = dx.shape
    ts = _tile(s, 512)

    def body(dx_ref, f_ref, gt_ref, df_ref, dgt_ref):
        @pl.when(pl.program_id(0) == 0)
        def _():
            dgt_ref[...] = jnp.zeros_like(dgt_ref)

        dxv = dx_ref[...]
        df_ref[...] = ((k * gt_ref[...]) * dxv).astype(BF16)
        dgt_ref[...] += k * jnp.sum(f_ref[...] * dxv, axis=0, keepdims=True)

    blk = pl.BlockSpec((ts, d), lambda i: (i, 0))
    row = pl.BlockSpec((1, d), lambda i: (0, 0))
    return _pcall(
        body, name=name, grid=(s // ts,),
        in_specs=[blk, blk, row], out_specs=[blk, row],
        out_shape=[jax.ShapeDtypeStruct((s, d), BF16), jax.ShapeDtypeStruct((1, d), F32)],
        compiler_params=_params("arbitrary"),
    )(dx, f, gt)


def _ffn_up(h, wg, wu, layer, name, carry=None):
    s, d = h.shape
    fs = wg.shape[-1]
    tm = _tile(s, 1024)

    def body(h_ref, wg_ref, wu_ref, a_ref, b_ref, s_ref):
        hv = h_ref[...]
        a = jnp.dot(hv, wg_ref[...], preferred_element_type=F32)
        b = jnp.dot(hv, wu_ref[...], preferred_element_type=F32)
        a_ref[...] = a.astype(BF16)
        b_ref[...] = b.astype(BF16)
        s_ref[...] = (a * _sigmoid(a) * b).astype(BF16)

    wspec = pl.BlockSpec((None, None, d, fs), lambda j, m: (j, layer, 0, 0))
    ospec = pl.BlockSpec((None, tm, fs), lambda j, m: (j, m, 0))
    return _hosted(
        body, name=name, grid=(N_DEV, s // tm),
        in_specs=[pl.BlockSpec((tm, d), lambda j, m: (m, 0)), wspec, wspec],
        out_specs=[ospec, ospec, ospec],
        out_shape=[jax.ShapeDtypeStruct((N_DEV, s, fs), BF16)] * 3,
        args=(h, wg, wu), carry=carry)


def _ffn_down(sv, wd, layer, x_in, gt, name, carry=None):
    _, s, fs = sv.shape
    d = wd.shape[-1]
    tm = _tile(s, 512)

    def body(s_ref, wd_ref, x_ref, gt_ref, f_ref, xo_ref, acc):
        j = pl.program_id(1)

        @pl.when(j == 0)
        def _():
            acc[...] = jnp.zeros_like(acc)

        acc[...] += jnp.dot(s_ref[...], wd_ref[...], preferred_element_type=F32)

        @pl.when(j == N_DEV - 1)
        def _():
            fv = acc[...]
            f_ref[...] = fv
            xo_ref[...] = x_ref[...] + (MACARON_W * gt_ref[...]) * fv

    blk = pl.BlockSpec((tm, d), lambda m, j: (m, 0))
    return _hosted(
        body, name=name, grid=(s // tm, N_DEV),
        in_specs=[pl.BlockSpec((None, tm, fs), lambda m, j: (j, m, 0)),
                  pl.BlockSpec((None, None, fs, d), lambda m, j: (j, layer, 0, 0)),
                  blk, pl.BlockSpec((1, d), lambda m, j: (0, 0))],
        out_specs=[blk, blk],
        out_shape=[jax.ShapeDtypeStruct((s, d), F32)] * 2,
        scratch_shapes=[pltpu.VMEM((tm, d), F32)],
        args=(sv, wd, x_in, gt), carry=carry)


def _ffn_bwd_act(df, wd, layer, a, b, name, carry=None):
    s, d = df.shape
    fs = a.shape[-1]
    tm = _tile(s, 1024)

    def body(df_ref, wd_ref, a_ref, b_ref, da_ref, db_ref):
        ds = lax.dot_general(df_ref[...], wd_ref[...], (NT, ((), ())), preferred_element_type=F32)
        av = a_ref[...].astype(F32)
        sg = _sigmoid(av)
        da_ref[...] = (ds * b_ref[...].astype(F32) * (sg * (1.0 + av * (1.0 - sg)))).astype(BF16)
        db_ref[...] = (ds * (av * sg)).astype(BF16)

    hid = pl.BlockSpec((None, tm, fs), lambda j, m: (j, m, 0))
    return _hosted(
        body, name=name, grid=(N_DEV, s // tm),
        in_specs=[pl.BlockSpec((tm, d), lambda j, m: (m, 0)),
                  pl.BlockSpec((None, None, fs, d), lambda j, m: (j, layer, 0, 0)), hid, hid],
        out_specs=[hid, hid],
        out_shape=[jax.ShapeDtypeStruct((N_DEV, s, fs), BF16)] * 2,
        args=(df, wd, a, b), carry=carry)


def _ffn_bwd_wd(sv, df, name, carry=None):
    _, s, fs = sv.shape
    d = df.shape[1]
    tk = _tile(s, 1024)
    nk = s // tk

    def body(s_ref, df_ref, o_ref, acc):
        @pl.when(pl.program_id(1) == 0)
        def _():
            acc[...] = jnp.zeros_like(acc)

        acc[...] += lax.dot_general(s_ref[...], df_ref[...], (TN, ((), ())), preferred_element_type=F32)

        @pl.when(pl.program_id(1) == nk - 1)
        def _():
            o_ref[...] = acc[...].astype(BF16)

    return _hosted(
        body, name=name, grid=(N_DEV, nk),
        in_specs=[pl.BlockSpec((None, tk, fs), lambda j, k: (j, k, 0)), pl.BlockSpec((tk, d), lambda j, k: (k, 0))],
        out_specs=[pl.BlockSpec((None, fs, d), lambda j, k: (j, 0, 0))],
        out_shape=[jax.ShapeDtypeStruct((N_DEV, fs, d), BF16)],
        scratch_shapes=[pltpu.VMEM((fs, d), F32)],
        args=(sv, df), carry=carry)


def _ffn_bwd_h(da, db, wg, wu, layer, name, carry=None):
    _, s, fs = da.shape
    d = wg.shape[-2]
    tm = _tile(s, 1024)

    def body(da_ref, db_ref, wg_ref, wu_ref, o_ref, acc):
        j = pl.program_id(1)

        @pl.when(j == 0)
        def _():
            acc[...] = jnp.zeros_like(acc)

        acc[...] += (lax.dot_general(da_ref[...], wg_ref[...], (NT, ((), ())), preferred_element_type=F32)
                     + lax.dot_general(db_ref[...], wu_ref[...], (NT, ((), ())), preferred_element_type=F32))

        @pl.when(j == N_DEV - 1)
        def _():
            o_ref[...] = acc[...]

    hid = pl.BlockSpec((None, tm, fs), lambda m, j: (j, m, 0))
    wspec = pl.BlockSpec((None, None, d, fs), lambda m, j: (j, layer, 0, 0))
    return _hosted(
        body, name=name, grid=(s // tm, N_DEV),
        in_specs=[hid, hid, wspec, wspec],
        out_specs=[pl.BlockSpec((tm, d), lambda m, j: (m, 0))],
        out_shape=[jax.ShapeDtypeStruct((s, d), F32)],
        scratch_shapes=[pltpu.VMEM((tm, d), F32)],
        args=(da, db, wg, wu), carry=carry)


def _ffn_bwd_wgu(h, da, db, name):
    s, d = h.shape
    fs = da.shape[-1]
    tk = _tile(s, 1024)
    nk = s // tk

    def body(h_ref, da_ref, db_ref, og_ref, ou_ref, accg, accu):
        @pl.when(pl.program_id(1) == 0)
        def _():
            accg[...] = jnp.zeros_like(accg)
            accu[...] = jnp.zeros_like(accu)

        hv = h_ref[...]
        accg[...] += lax.dot_general(hv, da_ref[...], (TN, ((), ())), preferred_element_type=F32)
        accu[...] += lax.dot_general(hv, db_ref[...], (TN, ((), ())), preferred_element_type=F32)

        @pl.when(pl.program_id(1) == nk - 1)
        def _():
            og_ref[...] = accg[...].astype(BF16)
            ou_ref[...] = accu[...].astype(BF16)

    hid = pl.BlockSpec((None, tk, fs), lambda j, k: (j, k, 0))
    ospec = pl.BlockSpec((None, d, fs), lambda j, k: (j, 0, 0))
    return _pcall(
        body, name=name, grid=(N_DEV, nk),
        in_specs=[pl.BlockSpec((tk, d), lambda j, k: (k, 0)), hid, hid],
        out_specs=[ospec, ospec],
        out_shape=[jax.ShapeDtypeStruct((N_DEV, d, fs), BF16)] * 2,
        scratch_shapes=[pltpu.VMEM((d, fs), F32), pltpu.VMEM((d, fs), F32)],
        compiler_params=_params("parallel", "arbitrary"),
    )(h, da, db)


def _mm(a, b, *, ta=False, tb=False, out_dtype=F32, name, tm=1024, tn=1024, tk=2048, residual=None):
    m, kdim = (a.shape[1], a.shape[0]) if ta else a.shape
    n = b.shape[0] if tb else b.shape[1]
    tm, tn, tk = _tile(m, tm), _tile(n, tn), _tile(kdim, tk)
    nk = kdim // tk
    dims = ((0,) if ta else (1,), (1,) if tb else (0,))

    def body(*refs):
        a_ref, b_ref = refs[:2]
        acc = refs[-1]
        kk = pl.program_id(2)

        @pl.when(kk == 0)
        def _():
            acc[...] = jnp.zeros_like(acc)

        acc[...] += lax.dot_general(a_ref[...].astype(BF16), b_ref[...].astype(BF16), (dims, ((), ())),
                                    preferred_element_type=F32)

        @pl.when(kk == nk - 1)
        def _():
            if residual is None:
                refs[2][...] = acc[...].astype(out_dtype)
            else:
                res_ref, gate_ref, y_ref, xo_ref = refs[2:6]
                yv = acc[...]
                y_ref[...] = yv
                xo_ref[...] = res_ref[...] + gate_ref[...] * yv

    a_spec = pl.BlockSpec((tk, tm), lambda i, j, k: (k, i)) if ta else pl.BlockSpec((tm, tk), lambda i, j, k: (i, k))
    b_spec = pl.BlockSpec((tn, tk), lambda i, j, k: (j, k)) if tb else pl.BlockSpec((tk, tn), lambda i, j, k: (k, j))
    o_spec = pl.BlockSpec((tm, tn), lambda i, j, k: (i, j))
    if residual is None:
        in_specs, out_specs = [a_spec, b_spec], o_spec
        out_shape = jax.ShapeDtypeStruct((m, n), out_dtype)
        args = (a, b)
    else:
        in_specs = [a_spec, b_spec, o_spec, pl.BlockSpec((1, tn), lambda i, j, k: (0, j))]
        out_specs = [o_spec, o_spec]
        out_shape = [jax.ShapeDtypeStruct((m, n), F32)] * 2
        args = (a, b) + tuple(residual)
    return _pcall(
        body, name=name, grid=(m // tm, n // tn, nk), in_specs=in_specs, out_specs=out_specs, out_shape=out_shape,
        scratch_shapes=[pltpu.VMEM((tm, tn), F32)],
        compiler_params=_params("parallel", "parallel", "arbitrary"),
    )(*args)


def _log_sigmoid(z):
    return jnp.minimum(z, 0.0) - jnp.log(1.0 + jnp.exp(-jnp.abs(z)))


def _fox_gate(proj, small_blk, bias_lane):
    s = proj.shape[0]
    ts = _tile(s, 1024)
    nsub = ts // LANES

    def body(z_ref, b_ref, cum_ref, carry):
        @pl.when(pl.program_id(0) == 0)
        def _():
            carry[...] = jnp.zeros_like(carry)

        ii = lax.broadcasted_iota(jnp.int32, (LANES, LANES), 0)
        jj = lax.broadcasted_iota(jnp.int32, (LANES, LANES), 1)
        tri = (ii >= jj).astype(BF16)
        logf = _log_sigmoid(z_ref[...] + b_ref[...])
        cv = carry[...]
        for sb in range(nsub):
            blk = logf[sb * LANES:(sb + 1) * LANES, :]
            cum_ref[sb * LANES:(sb + 1) * LANES, :] = _dot_exact_lhs(tri, blk) + cv
            cv = cv + jnp.sum(blk, axis=0, keepdims=True)
        carry[...] = cv

    return _pcall(
        body, name="fox_gate", grid=(s // ts,),
        in_specs=[pl.BlockSpec((ts, LANES), lambda i: (i, small_blk)), pl.BlockSpec((1, LANES), lambda i: (0, 0))],
        out_specs=pl.BlockSpec((ts, LANES), lambda i: (i, 0)),
        out_shape=jax.ShapeDtypeStruct((s, LANES), F32),
        scratch_shapes=[pltpu.VMEM((1, LANES), F32)],
        compiler_params=_params("arbitrary"),
    )(proj, bias_lane)


def _fox_gate_bwd(dcum_q, dcum_k, proj, small_blk, bias_lane):
    s = proj.shape[0]
    ts = _tile(s, 1024)
    nsub = ts // LANES
    nb = s // ts

    def body(dcq_ref, dc_ref, z_ref, b_ref, dz_ref, db_ref, carry):
        @pl.when(pl.program_id(0) == 0)
        def _():
            carry[...] = jnp.zeros_like(carry)
            db_ref[...] = jnp.zeros_like(db_ref)

        ii = lax.broadcasted_iota(jnp.int32, (LANES, LANES), 0)
        jj = lax.broadcasted_iota(jnp.int32, (LANES, LANES), 1)
        triu = (jj >= ii).astype(BF16)
        dc = dcq_ref[...] + dc_ref[...]
        zb = z_ref[...] + b_ref[...]
        cv = carry[...]
        dbv = jnp.zeros((1, LANES), F32)
        for sb in reversed(range(nsub)):
            rows = slice(sb * LANES, (sb + 1) * LANES)
            blk = dc[rows, :]
            dlogf = _dot_exact_lhs(triu, blk) + cv
            cv = cv + jnp.sum(blk, axis=0, keepdims=True)
            dz = dlogf * _sigmoid(-zb[rows, :])
            dz_ref[rows, :] = dz
            dbv = dbv + jnp.sum(dz, axis=0, keepdims=True)
        carry[...] = cv
        db_ref[...] += dbv

    row = pl.BlockSpec((1, LANES), lambda i: (0, 0))
    return _pcall(
        body, name="fox_gate_bwd", grid=(nb,),
        in_specs=[pl.BlockSpec((ts, LANES), lambda i: (nb - 1 - i, 0)),
                  pl.BlockSpec((ts, LANES), lambda i: (nb - 1 - i, 0)),
                  pl.BlockSpec((ts, LANES), lambda i: (nb - 1 - i, small_blk)), row],
        out_specs=[pl.BlockSpec((ts, LANES), lambda i: (nb - 1 - i, 0)), row],
        out_shape=[jax.ShapeDtypeStruct((s, LANES), F32), jax.ShapeDtypeStruct((1, LANES), F32)],
        scratch_shapes=[pltpu.VMEM((1, LANES), F32)],
        compiler_params=_params("arbitrary"),
    )(dcum_q, dcum_k, proj, bias_lane)


def _tri_tables(n, by_key):
    if by_key:
        pairs = [(i, j) for j in range(n) for i in range(j, n)]
    else:
        pairs = [(i, j) for i in range(n) for j in range(i + 1)]
    return (jnp.asarray(np.array([p[0] for p in pairs], np.int32)),
            jnp.asarray(np.array([p[1] for p in pairs], np.int32)))


def _fox_group(heads):
    return FOX_HEADS_PER_STEP if heads % FOX_HEADS_PER_STEP == 0 else 1


def _as_row(col):
    t = col.shape[0]
    eye = lax.broadcasted_iota(jnp.int32, (t, t), 0) == lax.broadcasted_iota(jnp.int32, (t, t), 1)
    return jnp.sum(jnp.where(eye, col, 0.0), axis=0, keepdims=True)


LOG2E = 1.4426950408889634
FOX_Q_SCALE = LOG2E / math.sqrt(HEAD_DIM)


def _fox_scores(a, b, bias_col, bias_row, diagonal, rows_are_keys=False):
    sc = lax.dot_general(a.astype(BF16), b.astype(BF16), (NT, ((), ())), preferred_element_type=F32)
    sc = sc + (bias_col + bias_row)
    if not diagonal:
        return sc
    row = lax.broadcasted_iota(jnp.int32, sc.shape, 0)
    col = lax.broadcasted_iota(jnp.int32, sc.shape, 1)
    return jnp.where(row <= col if rows_are_keys else col <= row, sc, NEG)


def _fox_fwd(proj, cum_col, cum_row, w_norm, heads, carry=None):
    s = proj.shape[0]
    t = _tile(s, 512)
    grp = _fox_group(heads)
    qi, ki = _tri_tables(s // t, False)
    scale = 1.0 / math.sqrt(HEAD_DIM)

    def body(qi_ref, ki_ref, q_ref, k_ref, v_ref, cq_ref, ck_ref, w_ref, o_ref, lse_ref, lser_ref, on_ref, m_s, acc_s):
        iq, ik = qi_ref[pl.program_id(1)], ki_ref[pl.program_id(1)]

        @pl.when(ik == 0)
        def _():
            m_s[...] = jnp.full_like(m_s, NEG)
            acc_s[...] = jnp.zeros_like(acc_s)

        def step(diagonal):
            for g in range(grp):
                sl = slice(g * HEAD_DIM, (g + 1) * HEAD_DIM)
                qs = (q_ref[:, sl] * FOX_Q_SCALE).astype(BF16)
                sc = _fox_scores(qs, k_ref[:, sl], cq_ref[g, :, 0:1] * LOG2E, ck_ref[g] * (-LOG2E), diagonal)
                m_prev = m_s[g]
                m_new = jnp.maximum(m_prev, jnp.max(sc, axis=1, keepdims=True))
                p = jnp.exp2(sc - m_new).astype(BF16)
                v_ones = jnp.concatenate([v_ref[:, sl].astype(BF16), jnp.ones((t, LANES), BF16)], axis=1)
                acc_s[g] = jnp.exp2(m_prev - m_new) * acc_s[g] + jnp.dot(p, v_ones, preferred_element_type=F32)
                m_s[g] = m_new

        @pl.when(ik < iq)
        def _():
            step(False)

        @pl.when(ik == iq)
        def _():
            step(True)
            for g in range(grp):
                sl = slice(g * HEAD_DIM, (g + 1) * HEAD_DIM)
                acc = acc_s[g]
                o = acc[:, :HEAD_DIM] / acc[:, HEAD_DIM:]
                lse = m_s[g] + jnp.log(acc[:, HEAD_DIM:]) * LOG2E
                o_ref[:, sl] = o
                lse_ref[g] = lse
                lser_ref[g] = _as_row(lse[:, 0:1])
                r = lax.rsqrt(jnp.mean(o * o, axis=1, keepdims=True) + EPS)
                on_ref[:, sl] = (o * r * w_ref[...]).astype(BF16)

    ng = heads // grp
    qblk = pl.BlockSpec((t, grp * HEAD_DIM), lambda h, p, qi, ki: (qi[p], h))
    kblk = lambda off: pl.BlockSpec((t, grp * HEAD_DIM), lambda h, p, qi, ki: (ki[p], off + h))
    qcol = pl.BlockSpec((grp, t, LANES), lambda h, p, qi, ki: (h, qi[p], 0))
    return _hosted(
        body, name="fox_fwd", grid=(ng, int(qi.shape[0])), prefetch=(qi, ki),
        in_specs=[qblk, kblk(ng), kblk(2 * ng), qcol,
                  pl.BlockSpec((grp, 1, t), lambda h, p, qi, ki: (h, 0, ki[p])),
                  pl.BlockSpec((1, HEAD_DIM), lambda h, p, qi, ki: (0, 0))],
        out_specs=[qblk, qcol, pl.BlockSpec((grp, 1, t), lambda h, p, qi, ki: (h, 0, qi[p])), qblk],
        scratch_shapes=[pltpu.VMEM((grp, t, 1), F32), pltpu.VMEM((grp, t, 2 * HEAD_DIM), F32)],
        out_shape=[jax.ShapeDtypeStruct((s, heads * HEAD_DIM), F32), jax.ShapeDtypeStruct((heads, s, LANES), F32),
                   jax.ShapeDtypeStruct((heads, 1, s), F32), jax.ShapeDtypeStruct((s, heads * HEAD_DIM), BF16)],
        args=(proj, proj, proj, cum_col, cum_row, w_norm), carry=carry)


def _fox_prep_bwd(do_cat, o_raw, w_norm, heads):
    s = o_raw.shape[0]
    ts = _tile(s, 512)

    def body(g_ref, o_ref, w_ref, do_ref, delta_ref, deltar_ref, dw_ref):
        @pl.when((pl.program_id(0) == 0) & (pl.program_id(1) == 0))
        def _():
            dw_ref[...] = jnp.zeros_like(dw_ref)

        o = o_ref[...]
        g = g_ref[...]
        r = lax.rsqrt(jnp.mean(o * o, axis=1, keepdims=True) + EPS)
        wg = g * w_ref[...]
        do = r * wg - o * (r * r * r) * jnp.mean(wg * o, axis=1, keepdims=True)
        do_ref[...] = do.astype(BF16)
        delta = jnp.sum(do * o, axis=1, keepdims=True)
        delta_ref[...] = jnp.broadcast_to(delta, delta_ref.shape)
        deltar_ref[...] = _as_row(delta)
        dw_ref[...] += jnp.sum(g * o * r, axis=0, keepdims=True)

    blk = pl.BlockSpec((ts, HEAD_DIM), lambda h, i: (i, h))
    row = pl.BlockSpec((1, HEAD_DIM), lambda h, i: (0, 0))
    return _pcall(
        body, name="fox_prep_bwd", grid=(heads, s // ts),
        in_specs=[blk, blk, row],
        out_specs=[blk, pl.BlockSpec((None, ts, LANES), lambda h, i: (h, i, 0)),
                   pl.BlockSpec((None, 1, ts), lambda h, i: (h, 0, i)), row],
        out_shape=[jax.ShapeDtypeStruct((s, heads * HEAD_DIM), BF16), jax.ShapeDtypeStruct((heads, s, LANES), F32),
                   jax.ShapeDtypeStruct((heads, 1, s), F32), jax.ShapeDtypeStruct((1, HEAD_DIM), F32)],
        compiler_params=_params("arbitrary", "arbitrary"),
    )(do_cat, o_raw, w_norm)


def _fox_bwd(proj, do, cum_col, cum_row, lse_row, delta_row, heads, carry=None):
    s = proj.shape[0]
    t = _tile(s, 512)
    nk = s // t
    grp = FOX_BWD_HEADS_PER_STEP if heads % FOX_BWD_HEADS_PER_STEP == 0 else 1
    qi, ki = _tri_tables(nk, True)
    npairs = int(qi.shape[0])
    scale = 1.0 / math.sqrt(HEAD_DIM)

    def body(qi_ref, ki_ref, q_ref, k_ref, v_ref, do_ref, cqr_ref, ckc_ref, lse_ref, dl_ref,
             dk_ref, dv_ref, dck_ref, dq_hbm, dcq_ref, dk_acc, dv_acc, dck_acc, dq_acc, stage, sem):
        pair = pl.program_id(1)
        iq, ik = qi_ref[pair], ki_ref[pair]
        rows_q = pl.ds(pl.multiple_of(iq * t, t), t)

        @pl.when(pair == 0)
        def _():
            dq_acc[...] = jnp.zeros_like(dq_acc)
            dcq_ref[...] = jnp.zeros_like(dcq_ref)

        def step(diagonal):
            for g in range(grp):
                sl = slice(g * HEAD_DIM, (g + 1) * HEAD_DIM)
                qs = (q_ref[:, sl] * FOX_Q_SCALE).astype(BF16)
                kv = k_ref[:, sl]
                dov = do_ref[:, sl]
                st = _fox_scores(kv, qs, ckc_ref[g, :, 0:1] * (-LOG2E), cqr_ref[g] * LOG2E - lse_ref[g], diagonal, True)
                pt = jnp.exp2(st)
                dv_acc[g] += _dot(pt, dov, NN)
                dpt = _dot(v_ref[:, sl], dov, NT)
                dst = pt * (dpt - dl_ref[g])
                dk_acc[g] += _dot(dst, qs, NN)
                dck_acc[g] += jnp.sum(dst, axis=1, keepdims=True)
                dq_acc[g, rows_q, :] += _dot(dst, kv, TN)
                dcq_ref[g, iq] += jnp.sum(dst, axis=0, keepdims=True)

        @pl.when(iq == ik)
        def _():
            dk_acc[...] = jnp.zeros_like(dk_acc)
            dv_acc[...] = jnp.zeros_like(dv_acc)
            dck_acc[...] = jnp.zeros_like(dck_acc)
            step(True)

        @pl.when(iq > ik)
        def _():
            step(False)

        @pl.when(iq == nk - 1)
        def _():
            for g in range(grp):
                sl = slice(g * HEAD_DIM, (g + 1) * HEAD_DIM)
                dk_ref[:, sl] = (dk_acc[g] * (1.0 / LOG2E)).astype(BF16)
                dv_ref[:, sl] = dv_acc[g].astype(BF16)
                dck_ref[g] = _as_row(-dck_acc[g])

        @pl.when(pair == npairs - 1)
        def _():
            for g in range(grp):
                head = pl.program_id(0) * grp + g

                def flush(i, c):
                    rows = pl.ds(pl.multiple_of(i * t, t), t)
                    stage[...] = (dq_acc[g, rows, :] * scale).astype(BF16)
                    cp = pltpu.make_async_copy(stage, dq_hbm.at[head, rows, :], sem)
                    cp.start()
                    cp.wait()
                    return c

                lax.fori_loop(0, nk, flush, 0)

    ng = heads // grp
    qblk = pl.BlockSpec((t, grp * HEAD_DIM), lambda h, p, qi, ki: (qi[p], h))
    qrow = pl.BlockSpec((grp, 1, t), lambda h, p, qi, ki: (h, 0, qi[p]))
    kblk = lambda off: pl.BlockSpec((t, grp * HEAD_DIM), lambda h, p, qi, ki: (ki[p], off + h))
    kout = pl.BlockSpec((t, grp * HEAD_DIM), lambda h, p, qi, ki: (ki[p], h))
    return _hosted(
        body, name="fox_bwd", grid=(ng, npairs), prefetch=(qi, ki),
        in_specs=[qblk, kblk(ng), kblk(2 * ng), qblk, qrow,
                  pl.BlockSpec((grp, t, LANES), lambda h, p, qi, ki: (h, ki[p], 0)), qrow, qrow],
        out_specs=[kout, kout, pl.BlockSpec((grp, 1, t), lambda h, p, qi, ki: (h, 0, ki[p])), ANY,
                   pl.BlockSpec((grp, nk, 1, t), lambda h, p, qi, ki: (h, 0, 0, 0))],
        scratch_shapes=[pltpu.VMEM((grp, t, HEAD_DIM), F32), pltpu.VMEM((grp, t, HEAD_DIM), F32),
                        pltpu.VMEM((grp, t, 1), F32), pltpu.VMEM((grp, s, HEAD_DIM), F32),
                        pltpu.VMEM((t, HEAD_DIM), BF16), pltpu.SemaphoreType.DMA],
        out_shape=[jax.ShapeDtypeStruct((s, heads * HEAD_DIM), BF16)] * 2 + [jax.ShapeDtypeStruct((heads, 1, s), F32)]
        + [jax.ShapeDtypeStruct((heads, s, HEAD_DIM), BF16), jax.ShapeDtypeStruct((heads, nk, 1, t), F32)],
        args=(proj, proj, proj, do, cum_row, cum_col, lse_row, delta_row), carry=carry)


def _shift_rows(xv, halo, j, forward):
    n = xv.shape[0]
    rid = lax.broadcasted_iota(jnp.int32, (8, xv.shape[1]), 0)
    if forward:
        xs = pltpu.roll(xv, n - j, 0)
        hs = pltpu.roll(halo, 8 - j, 0)
        edge = jnp.where(rid >= 8 - j, hs, xs[n - 8:, :])
        return jnp.concatenate([xs[:n - 8, :], edge], axis=0)
    xs = pltpu.roll(xv, j, 0)
    hs = pltpu.roll(halo, j, 0)
    edge = jnp.where(rid < j, hs, xs[:8, :])
    return jnp.concatenate([edge, xs[8:, :]], axis=0)


def _conv_silu(xv, halo, w):
    xc = w[CONV_W - 1:CONV_W, :] * xv
    for j in range(1, CONV_W):
        xc = xc + w[CONV_W - 1 - j:CONV_W - j, :] * _shift_rows(xv, halo, j, False)
    return xc, xc * _sigmoid(xc)


def _gdn_pre(proj, conv_w, heads):
    s = proj.shape[0]
    cw = 3 * heads * HEAD_DIM
    ts = _tile(s, 256)
    tb = ts // 8

    def body(x_ref, halo_ref, w_ref, q_ref, k_ref, v_ref):
        halo = jnp.where(pl.program_id(0) == 0, 0.0, halo_ref[...])
        _, y = _conv_silu(x_ref[...], halo, w_ref[...])
        for h in range(heads):
            for part, ref in enumerate((q_ref, k_ref, v_ref)):
                c0 = (part * heads + h) * HEAD_DIM
                blk = y[:, c0:c0 + HEAD_DIM]
                if part < 2:
                    blk = blk * lax.rsqrt(jnp.sum(blk * blk, axis=1, keepdims=True) + EPS)
                ref[h] = blk

    out = pl.BlockSpec((heads, ts, HEAD_DIM), lambda i: (0, i, 0))
    return _pcall(
        body, name="gdn_pre", grid=(s // ts,),
        in_specs=[pl.BlockSpec((ts, cw), lambda i: (i, 1)),
                  pl.BlockSpec((8, cw), lambda i: (jnp.maximum(i * tb - 1, 0), 1)),
                  pl.BlockSpec((CONV_W, cw), lambda i: (0, 0))],
        out_specs=[out, out, out],
        out_shape=[jax.ShapeDtypeStruct((heads, s, HEAD_DIM), F32)] * 3,
        compiler_params=_params("parallel"),
    )(proj, proj, conv_w)


def _gdn_pre_bwd_act(proj, conv_w, dq, dk, dv, heads):
    s = proj.shape[0]
    cw = 3 * heads * HEAD_DIM
    ts = _tile(s, 256)
    tb = ts // 8

    def body(x_ref, halo_ref, w_ref, dq_ref, dk_ref, dv_ref, dxc_ref, dw_ref):
        @pl.when(pl.program_id(0) == 0)
        def _():
            dw_ref[...] = jnp.zeros_like(dw_ref)

        xv = x_ref[...]
        halo = jnp.where(pl.program_id(0) == 0, 0.0, halo_ref[...])
        xc, y = _conv_silu(xv, halo, w_ref[...])
        sg = _sigmoid(xc)
        dsilu = sg * (1.0 + xc * (1.0 - sg))
        for h in range(heads):
            for part, ref in enumerate((dq_ref, dk_ref, dv_ref)):
                c0 = (part * heads + h) * HEAD_DIM
                g = ref[h]
                if part < 2:
                    blk = y[:, c0:c0 + HEAD_DIM]
                    r = lax.rsqrt(jnp.sum(blk * blk, axis=1, keepdims=True) + EPS)
                    g = r * g - blk * (r * r * r) * jnp.sum(g * blk, axis=1, keepdims=True)
                dxc_ref[:, c0:c0 + HEAD_DIM] = g * dsilu[:, c0:c0 + HEAD_DIM]
        dxc = dxc_ref[...]
        rows = [jnp.sum(dxc * (xv if j == 0 else _shift_rows(xv, halo, j, False)), axis=0, keepdims=True)
                for j in range(CONV_W)]
        dw_ref[...] += jnp.concatenate([rows[CONV_W - 1 - k] for k in range(CONV_W)]
                                       + [jnp.zeros((8 - CONV_W, cw), F32)], axis=0)

    hblk = pl.BlockSpec((heads, ts, HEAD_DIM), lambda i: (0, i, 0))
    return _pcall(
        body, name="gdn_pre_bwd_act", grid=(s // ts,),
        in_specs=[pl.BlockSpec((ts, cw), lambda i: (i, 1)),
                  pl.BlockSpec((8, cw), lambda i: (jnp.maximum(i * tb - 1, 0), 1)),
                  pl.BlockSpec((CONV_W, cw), lambda i: (0, 0)), hblk, hblk, hblk],
        out_specs=[pl.BlockSpec((ts, cw), lambda i: (i, 0)), pl.BlockSpec((8, cw), lambda i: (0, 0))],
        out_shape=[jax.ShapeDtypeStruct((s, cw), F32), jax.ShapeDtypeStruct((8, cw), F32)],
        compiler_params=_params("arbitrary"),
    )(proj, proj, conv_w, dq, dk, dv)


def _gdn_pre_bwd_conv(dxc, conv_w):
    s, cw = dxc.shape
    ts = _tile(s, 256)
    tb = ts // 8
    last = s // 8 - 1

    def body(g_ref, halo_ref, w_ref, dx_ref):
        gv = g_ref[...]
        w = w_ref[...]
        halo = jnp.where(pl.program_id(0) == s // ts - 1, 0.0, halo_ref[...])
        dx = w[CONV_W - 1:CONV_W, :] * gv
        for j in range(1, CONV_W):
            dx = dx + w[CONV_W - 1 - j:CONV_W - j, :] * _shift_rows(gv, halo, j, True)
        dx_ref[...] = dx.astype(BF16)

    return _pcall(
        body, name="gdn_pre_bwd_conv", grid=(s // ts,),
        in_specs=[pl.BlockSpec((ts, cw), lambda i: (i, 0)),
                  pl.BlockSpec((8, cw), lambda i: (jnp.minimum((i + 1) * tb, last), 0)),
                  pl.BlockSpec((CONV_W, cw), lambda i: (0, 0))],
        out_specs=pl.BlockSpec((ts, cw), lambda i: (i, 0)),
        out_shape=jax.ShapeDtypeStruct((s, cw), BF16),
        compiler_params=_params("parallel"),
    )(dxc, dxc, conv_w)


def _bdot(a, b, ca, cb):
    return lax.dot_general(a.astype(BF16), b.astype(BF16), (((ca,), (cb,)), ((0,), (0,))),
                           preferred_element_type=F32)


def _bdot_hp(a, b, ca, cb):
    ah = a.astype(BF16)
    al = (a - ah.astype(F32)).astype(BF16)
    bh = b.astype(BF16)
    bl = (b - bh.astype(F32)).astype(BF16)
    d = lambda p, q: lax.dot_general(p, q, (((ca,), (cb,)), ((0,), (0,))), preferred_element_type=F32)
    return d(ah, bh) + (d(ah, bl) + d(al, bh))


def _gdn_gates(small, a_lane, dt_lane, heads):
    lane = lax.broadcasted_iota(jnp.int32, small.shape, 1)
    za = small + dt_lane
    g_all = -jnp.exp(a_lane) * (jnp.maximum(za, 0.0) + jnp.log(1.0 + jnp.exp(-jnp.abs(za))))
    b_all = _sigmoid(small)
    pick = lambda v, l: jnp.sum(jnp.where(lane == l, v, 0.0), axis=1, keepdims=True)
    g = jnp.stack([pick(g_all, heads + h) for h in range(heads)], axis=0)
    beta = jnp.stack([pick(b_all, 2 * heads + h) for h in range(heads)], axis=0)
    return g, beta


def _chunk_masks(c):
    ii = lax.broadcasted_iota(jnp.int32, (1, c, c), 1)
    jj = lax.broadcasted_iota(jnp.int32, (1, c, c), 2)
    return ii >= jj, ii > jj, ii == jj


def _col_to_row(col, eye):
    return jnp.sum(jnp.where(eye, col, 0.0), axis=1, keepdims=True)


def _row_to_col(row, eye):
    return jnp.sum(jnp.where(eye, row, 0.0), axis=2, keepdims=True)


def _gdn_chunk(q, k, v, g, beta, state, tinv=None):
    c = q.shape[1]
    incl, strict, eye = _chunk_masks(c)
    g_row = _col_to_row(g, eye)
    gc_col = jnp.sum(jnp.where(incl, g_row, 0.0), axis=2, keepdims=True)
    gc_row = _col_to_row(gc_col, eye)
    gam = jnp.where(incl, jnp.exp(jnp.where(incl, gc_col - gc_row, NEG)), 0.0)
    egc = jnp.exp(gc_col)
    kb = k * beta
    vb = v * beta
    kbe = kb * egc
    low = jnp.where(strict, _bdot(kb, k, 2, 2), 0.0) * gam
    if tinv is None:
        p = -low
        tinv = jnp.where(eye, 1.0, 0.0) + p
        width = 2
        while width < c:
            p = _bdot_hp(p, p, 2, 1)
            tinv = tinv + _bdot_hp(tinv, p, 2, 1)
            width *= 2
    u = _bdot(tinv, vb, 2, 1)
    w = _bdot(tinv, kbe, 2, 1)
    att = jnp.where(incl, _bdot(q, k, 2, 2), 0.0) * gam
    vn = u - _bdot(w, state, 2, 1)
    qe = q * egc
    o = _bdot(qe, state, 2, 1) + _bdot(att, vn, 2, 1)
    gl = jnp.sum(g, axis=1, keepdims=True)
    edec = jnp.exp(gl - gc_col)
    kdec = k * edec
    egl = jnp.exp(gl)
    new_state = state * egl + _bdot(kdec, vn, 1, 1)
    return dict(incl=incl, strict=strict, eye=eye, gam=gam, egc=egc, kb=kb, vb=vb, kbe=kbe, low=low, tinv=tinv, w=w,
                att=att, vn=vn, qe=qe, o=o, edec=edec, kdec=kdec, egl=egl, new_state=new_state)


def _gdn_load(q_ref, k_ref, v_ref, small_ref, a_ref, dt_ref, rows, heads):
    q = q_ref[:, rows, :] * (HEAD_DIM ** -0.5)
    g, beta = _gdn_gates(small_ref[rows, :], a_ref[...], dt_ref[...], heads)
    return q, k_ref[:, rows, :], v_ref[:, rows, :], g, beta


def _gdn_fwd(q, k, v, proj, z_blk, small_blk, a_lane, dt_lane, w_norm):
    heads, s, _ = q.shape
    c = min(GDN_CHUNK, s)
    r = _tile(s, 512)
    npb = r // c
    gw = heads * HEAD_DIM

    def body(q_ref, k_ref, v_ref, z_ref, small_ref, a_ref, dt_ref, w_ref, o_ref, st_ref, ti_ref, state):
        @pl.when(pl.program_id(0) == 0)
        def _():
            state[...] = jnp.zeros_like(state)

        def chunk(cb, carry):
            rows = pl.ds(pl.multiple_of(cb * c, c), c)
            qv, kv, vv, g, beta = _gdn_load(q_ref, k_ref, v_ref, small_ref, a_ref, dt_ref, rows, heads)
            st = state[...]
            st_ref[:, cb] = st
            res = _gdn_chunk(qv, kv, vv, g, beta, st)
            ti_ref[:, cb] = res["tinv"]
            state[...] = res["new_state"]
            o = res["o"]
            rn = lax.rsqrt(jnp.mean(o * o, axis=2, keepdims=True) + EPS)
            zv = z_ref[rows, :]
            for h in range(heads):
                zh = zv[:, h * HEAD_DIM:(h + 1) * HEAD_DIM]
                o_ref[rows, h * HEAD_DIM:(h + 1) * HEAD_DIM] = (
                    o[h] * rn[h] * w_ref[...] * (zh * _sigmoid(zh))).astype(BF16)
            return carry

        lax.fori_loop(0, npb, chunk, 0)

    hblk = pl.BlockSpec((heads, r, HEAD_DIM), lambda i: (0, i, 0))
    row = pl.BlockSpec((1, LANES), lambda i: (0, 0))
    return _pcall(
        body, name="gdn_fwd", grid=(s // r,),
        in_specs=[hblk, hblk, hblk, pl.BlockSpec((r, gw), lambda i: (i, z_blk)),
                  pl.BlockSpec((r, LANES), lambda i: (i, small_blk)), row, row, row],
        out_specs=[pl.BlockSpec((r, gw), lambda i: (i, 0)),
                   pl.BlockSpec((heads, npb, HEAD_DIM, HEAD_DIM), lambda i: (0, i, 0, 0)),
                   pl.BlockSpec((heads, npb, c, c), lambda i: (0, i, 0, 0))],
        out_shape=[jax.ShapeDtypeStruct((s, gw), BF16),
                   jax.ShapeDtypeStruct((heads, s // c, HEAD_DIM, HEAD_DIM), F32),
                   jax.ShapeDtypeStruct((heads, s // c, c, c), F32)],
        scratch_shapes=[pltpu.VMEM((heads, HEAD_DIM, HEAD_DIM), F32)],
        compiler_params=_params("arbitrary"),
    )(q, k, v, proj, proj, a_lane, dt_lane, w_norm)


def _gdn_bwd(q, k, v, proj, z_blk, small_blk, a_lane, dt_lane, w_norm, states, tinvs, do_cat, do_blk):
    heads, s, _ = q.shape
    c = min(GDN_CHUNK, s)
    r = _tile(s, 512)
    npb = r // c
    nb = s // r
    gw = heads * HEAD_DIM

    def body(q_ref, k_ref, v_ref, z_ref, small_ref, a_ref, dt_ref, w_ref, st_ref, ti_ref, do_ref,
             dq_ref, dk_ref, dv_ref, dz_ref, dsm_ref, da_ref, ddt_ref, dw_ref, dstate):
        @pl.when(pl.program_id(0) == 0)
        def _():
            dstate[...] = jnp.zeros_like(dstate)
            da_ref[...] = jnp.zeros_like(da_ref)
            ddt_ref[...] = jnp.zeros_like(ddt_ref)
            dw_ref[...] = jnp.zeros_like(dw_ref)

        def chunk(it, carry):
            cb = npb - 1 - it
            rows = pl.ds(pl.multiple_of(cb * c, c), c)
            qv, kv, vv, g, beta = _gdn_load(q_ref, k_ref, v_ref, small_ref, a_ref, dt_ref, rows, heads)
            st = st_ref[:, cb]
            f = _gdn_chunk(qv, kv, vv, g, beta, st, tinv=ti_ref[:, cb])
            incl, strict, eye = f["incl"], f["strict"], f["eye"]
            o = f["o"]
            wv = w_ref[...]
            zv = z_ref[rows, :]
            dov = do_ref[rows, :]
            rn = lax.rsqrt(jnp.mean(o * o, axis=2, keepdims=True) + EPS)
            do_l, dw_acc = [], jnp.zeros((1, HEAD_DIM), F32)
            for h in range(heads):
                sl = slice(h * HEAD_DIM, (h + 1) * HEAD_DIM)
                zh, gh = zv[:, sl], dov[:, sl]
                sg = _sigmoid(zh)
                on = o[h] * rn[h]
                dz_ref[rows, sl] = (gh * (on * wv) * (sg * (1.0 + zh * (1.0 - sg)))).astype(BF16)
                gn = gh * (zh * sg)
                dw_acc = dw_acc + jnp.sum(gn * on, axis=0, keepdims=True)
                wg = gn * wv
                do_l.append(rn[h] * wg - o[h] * (rn[h] * rn[h] * rn[h]) * jnp.mean(wg * o[h], axis=1, keepdims=True))
            dw_ref[...] += dw_acc
            do = jnp.stack(do_l, axis=0)
            ds_out = dstate[...]
            dvn = _bdot(f["att"], do, 1, 1) + _bdot(f["kdec"], ds_out, 2, 1)
            datt = jnp.where(incl, _bdot(do, f["vn"], 2, 2), 0.0)
            dqe = _bdot(do, st, 2, 2)
            dstate[...] = _bdot(f["qe"], do, 1, 1) + f["egl"] * ds_out - _bdot(f["w"], dvn, 1, 1)
            dw = -_bdot(dvn, st, 2, 2)
            dkdec = _bdot(f["vn"], ds_out, 2, 2)
            t_kdec = jnp.sum(dkdec * f["kdec"], axis=2, keepdims=True)
            dgl = (jnp.sum(jnp.sum(st * ds_out, axis=2, keepdims=True), axis=1, keepdims=True) * f["egl"]
                   + jnp.sum(t_kdec, axis=1, keepdims=True))
            dgc = jnp.sum(dqe * f["qe"], axis=2, keepdims=True) - t_kdec
            dq = dqe * f["egc"]
            dk = dkdec * f["edec"]
            dtinv = _bdot(dvn, f["vb"], 2, 2) + _bdot(dw, f["kbe"], 2, 2)
            dvb = _bdot(f["tinv"], dvn, 1, 1)
            dkbe = _bdot(f["tinv"], dw, 1, 1)
            dkb = dkbe * f["egc"]
            dgc = dgc + jnp.sum(dkbe * f["kbe"], axis=2, keepdims=True)
            dlow = jnp.where(strict, -_bdot_hp(_bdot_hp(f["tinv"], dtinv, 1, 1), f["tinv"], 2, 2), 0.0)
            ml = dlow * f["gam"]
            dkb = dkb + _bdot(ml, kv, 2, 1)
            dk = dk + _bdot(ml, f["kb"], 1, 1)
            ma = datt * f["gam"]
            dq = dq + _bdot(ma, kv, 2, 1)
            dk = dk + _bdot(ma, qv, 1, 1)
            e = dlow * f["low"] + datt * f["att"]
            dgc = dgc + jnp.sum(e, axis=2, keepdims=True) - _row_to_col(jnp.sum(e, axis=1, keepdims=True), eye)
            dk = dk + beta * dkb
            dbeta = jnp.sum(dkb * kv, axis=2, keepdims=True) + jnp.sum(dvb * vv, axis=2, keepdims=True)
            dgc_row = _col_to_row(dgc, eye)
            dg = jnp.sum(jnp.where(incl, 0.0, dgc_row) + jnp.where(eye, dgc_row, 0.0), axis=2, keepdims=True) + dgl
            dq_ref[:, rows, :] = dq * (HEAD_DIM ** -0.5)
            dk_ref[:, rows, :] = dk
            dv_ref[:, rows, :] = beta * dvb
            small = small_ref[rows, :]
            lane = lax.broadcasted_iota(jnp.int32, small.shape, 1)
            dg_l = jnp.zeros(small.shape, F32)
            db_l = jnp.zeros(small.shape, F32)
            for h in range(heads):
                dg_l = dg_l + jnp.where(lane == heads + h, dg[h], 0.0)
                db_l = db_l + jnp.where(lane == 2 * heads + h, dbeta[h], 0.0)
            za = small + dt_ref[...]
            nexp = -jnp.exp(a_ref[...])
            softplus = jnp.maximum(za, 0.0) + jnp.log(1.0 + jnp.exp(-jnp.abs(za)))
            da_logit = dg_l * nexp * _sigmoid(za)
            sb = _sigmoid(small)
            dsm_ref[rows, :] = da_logit + db_l * sb * (1.0 - sb)
            ddt_ref[...] += jnp.sum(da_logit, axis=0, keepdims=True)
            da_ref[...] += jnp.sum(dg_l * nexp * softplus, axis=0, keepdims=True)
            return carry

        lax.fori_loop(0, npb, chunk, 0)

    rev = lambda i: nb - 1 - i
    hblk = pl.BlockSpec((heads, r, HEAD_DIM), lambda i: (0, rev(i), 0))
    row = pl.BlockSpec((1, LANES), lambda i: (0, 0))
    wide = lambda blk: pl.BlockSpec((r, gw), lambda i: (rev(i), blk))
    return _pcall(
        body, name="gdn_bwd", grid=(nb,),
        in_specs=[hblk, hblk, hblk, wide(z_blk), pl.BlockSpec((r, LANES), lambda i: (rev(i), small_blk)),
                  row, row, row, pl.BlockSpec((heads, npb, HEAD_DIM, HEAD_DIM), lambda i: (0, rev(i), 0, 0)),
                  pl.BlockSpec((heads, npb, c, c), lambda i: (0, rev(i), 0, 0)), wide(do_blk)],
        out_specs=[hblk, hblk, hblk, wide(0), pl.BlockSpec((r, LANES), lambda i: (rev(i), 0)), row, row, row],
        out_shape=[jax.ShapeDtypeStruct((heads, s, HEAD_DIM), F32)] * 3
        + [jax.ShapeDtypeStruct((s, gw), BF16), jax.ShapeDtypeStruct((s, LANES), F32)]
        + [jax.ShapeDtypeStruct((1, LANES), F32)] * 3,
        scratch_shapes=[pltpu.VMEM((heads, HEAD_DIM, HEAD_DIM), F32)],
        compiler_params=_params("arbitrary"),
    )(q, k, v, proj, proj, a_lane, dt_lane, w_norm, states, tinvs, do_cat)


def _final(x, target, gf):
    s, d = x.shape
    ts = _tile(s, 512)

    def body(x_ref, t_ref, g_ref, loss_ref, dx_ref, dg_ref):
        @pl.when(pl.program_id(0) == 0)
        def _():
            loss_ref[...] = jnp.zeros_like(loss_ref)
            dg_ref[...] = jnp.zeros_like(dg_ref)

        xv = x_ref[...]
        gv = g_ref[...]
        r = lax.rsqrt(jnp.mean(xv * xv, axis=-1, keepdims=True) + EPS)
        xn = xv * r
        err = xn * gv - t_ref[...]
        per_tok = jnp.mean(err * err, axis=-1, keepdims=True)
        loss_ref[...] += 0.5 * jnp.sum(per_tok, axis=0, keepdims=True)
        dy = err * (1.0 / d)
        dg_ref[...] += jnp.sum(dy * xn, axis=0, keepdims=True)
        dxn = dy * gv
        dx_ref[...] = r * (dxn - xn * jnp.mean(dxn * xn, axis=-1, keepdims=True))

    blk = pl.BlockSpec((ts, d), lambda i: (i, 0))
    row = pl.BlockSpec((1, d), lambda i: (0, 0))
    return _pcall(
        body, name="final_loss", grid=(s // ts,),
        in_specs=[blk, blk, row], out_specs=[pl.BlockSpec((1, LANES), lambda i: (0, 0)), blk, row],
        out_shape=[jax.ShapeDtypeStruct((1, LANES), F32), jax.ShapeDtypeStruct((s, d), F32),
                   jax.ShapeDtypeStruct((1, d), F32)],
        compiler_params=_params("arbitrary"),
    )(x, target, gf)


def _adamw(parts, w, m, v, name):
    npart, rows, cols = parts.shape
    tr = _tile(rows, max(8, ADAM_BLOCK_BYTES // (4 * npart * cols)))
    c1 = 1.0 - ADAM_B1 ** ADAM_STEP
    c2 = 1.0 - ADAM_B2 ** ADAM_STEP

    def body(p_ref, w_ref, m_ref, v_ref, g_ref, d_ref, mo_ref, vo_ref):
        g = p_ref[0].astype(F32)
        for i in range(1, npart):
            g = g + p_ref[i].astype(F32)
        mn = ADAM_B1 * m_ref[...] + (1.0 - ADAM_B1) * g
        vn = ADAM_B2 * v_ref[...] + (1.0 - ADAM_B2) * (g * g)
        g_ref[...] = g
        mo_ref[...] = mn
        vo_ref[...] = vn
        d_ref[...] = -ADAM_LR * ((mn / c1) / (jnp.sqrt(vn / c2) + ADAM_EPS) + ADAM_WD * w_ref[...])

    blk = pl.BlockSpec((tr, cols), lambda i: (i, 0))
    return _pcall(
        body, name=name, grid=(rows // tr,),
        in_specs=[pl.BlockSpec((npart, tr, cols), lambda i: (0, i, 0)), blk, blk, blk],
        out_specs=[blk] * 4, out_shape=[jax.ShapeDtypeStruct((rows, cols), F32)] * 4,
        compiler_params=_params("parallel"),
    )(parts, w, m, v)


def _adamw_layers(parts0, parts1, w, m, v, name):
    npart, rows, cols = parts0.shape
    tr = _tile(rows, max(8, ADAM_BLOCK_BYTES // (4 * npart * cols)))
    nb = rows // tr
    c1 = 1.0 - ADAM_B1 ** ADAM_STEP
    c2 = 1.0 - ADAM_B2 ** ADAM_STEP

    def body(p0_ref, p1_ref, w_ref, m_ref, v_ref, g_ref, d_ref, mo_ref, vo_ref):
        def update(p_ref):
            g = p_ref[0].astype(F32)
            for i in range(1, npart):
                g = g + p_ref[i].astype(F32)
            mn = ADAM_B1 * m_ref[...] + (1.0 - ADAM_B1) * g
            vn = ADAM_B2 * v_ref[...] + (1.0 - ADAM_B2) * (g * g)
            g_ref[...] = g
            mo_ref[...] = mn
            vo_ref[...] = vn
            d_ref[...] = -ADAM_LR * ((mn / c1) / (jnp.sqrt(vn / c2) + ADAM_EPS) + ADAM_WD * w_ref[...])

        @pl.when(pl.program_id(0) == 0)
        def _():
            update(p0_ref)

        @pl.when(pl.program_id(0) == 1)
        def _():
            update(p1_ref)

    blk = pl.BlockSpec((None, None, tr, cols), lambda l, i: (0, l, i, 0))
    p0 = pl.BlockSpec((npart, tr, cols), lambda l, i: (0, jnp.where(l == 0, i, nb - 1), 0))
    p1 = pl.BlockSpec((npart, tr, cols), lambda l, i: (0, jnp.where(l == 0, 0, i), 0))
    return _pcall(
        body, name=name, grid=(2, nb), in_specs=[p0, p1, blk, blk, blk], out_specs=[blk] * 4,
        out_shape=[jax.ShapeDtypeStruct(w.shape, F32)] * 4, compiler_params=_params("arbitrary", "arbitrary"),
    )(parts0, parts1, w, m, v)


def _pad_lanes(v, n=LANES, at=0):
    return jnp.pad(v, ((0, 0), (at, n - at - v.shape[1])))


def _my_cols(a, me, width):
    return lax.dynamic_slice_in_dim(a, me * width, width, axis=a.ndim - 1)


def kernel(x, c, ada_w, ada_b, norm_g, ffn_w_gate, ffn_w_up, ffn_w_down, w_in, w_out, fox_f_bias, fox_out_norm, gdn_conv, gdn_A_log, gdn_dt_bias, gdn_out_norm, final_norm, loss_target, m_ada_w, m_ada_b, m_norm_g, m_ffn_w_gate, m_ffn_w_up, m_ffn_w_down, m_w_in, m_w_out, m_fox_f_bias, m_fox_out_norm, m_gdn_conv, m_gdn_A_log, m_gdn_dt_bias, m_gdn_out_norm, m_final_norm, v_ada_w, v_ada_b, v_norm_g, v_ffn_w_gate, v_ffn_w_up, v_ffn_w_down, v_w_in, v_w_out, v_fox_f_bias, v_fox_out_norm, v_gdn_conv, v_gdn_A_log, v_gdn_dt_bias, v_gdn_out_norm, v_final_norm):
    me = _linear(_mesh_pos())
    x0 = x[0]
    s, d = x0.shape
    heads = d // (2 * HEAD_DIM)
    fw = heads * HEAD_DIM
    ng = norm_g.shape[-1]
    ncv = gdn_conv.shape[-1]
    nada = ada_w.shape[-1]
    in_w = w_in.shape[-1] * N_DEV
    in_pad = -(-in_w // 512) * 512

    pack = jnp.concatenate([c, norm_g[0].reshape(1, 3 * ng), gdn_conv[0].reshape(1, CONV_W * ncv)], axis=1)
    pack_all = _gather_row(pack, "gather_small_params")
    c_all = pack_all[:, :d]
    g_all = pack_all[:, d:d + 3 * ng].reshape(N_DEV, 3, ng).transpose(1, 0, 2).reshape(3, d)
    conv_all = pack_all[:, d + 3 * ng:].reshape(N_DEV, CONV_W, ncv).transpose(1, 0, 2).reshape(CONV_W, 3 * fw)

    mod_blk = _ada_fwd(c_all, ada_w[0], _my_cols(ada_b, me, nada))
    mod_all = _exchange([mod_blk], scatter=False, in_vmem=True, name="gather_mod")[0]
    mod = lax.dynamic_slice_in_dim(mod_all, me, 1, axis=1).reshape(N_MOD, d)
    sh1, sc1, gt1, sh2, sc2, gt2, sh3, sc3, gt3 = [mod[i:i + 1] for i in range(N_MOD)]

    wg_sh, wu_sh, wd_sh = [w[0].astype(BF16) for w in (ffn_w_gate, ffn_w_up, ffn_w_down)]
    layer = lambda w, i: w[i:i + 1]
    wg0, wu0 = _gather_via_sibling([layer(wg_sh, 0), layer(wu_sh, 0)], "gather_ffn1_up_weights")
    small_blk = 7 * heads

    bias_lane = _pad_lanes(fox_f_bias)
    a_lane = _pad_lanes(gdn_A_log, at=heads)
    dt_lane = _pad_lanes(gdn_dt_bias, at=heads)

    h1 = _norm_mod(x0, g_all[0:1], sc1, sh1, "norm_mod_1")
    (a1, b1, s1), (wd0, wout_g) = _ffn_up(h1, wg0, wu0, 0, "ffn1_up",
                                          carry=([layer(wd_sh, 0), w_out[0].astype(BF16)], False))
    (f1, x1), (win_g,) = _ffn_down(s1, wd0, 0, x0, gt1, "ffn1_down", carry=([w_in[0].astype(BF16)], False))
    win_full = win_g.transpose(1, 0, 2).reshape(d, in_w)
    o_f, o_qkv, o_a, o_z = 3 * fw, 3 * fw + heads, 6 * fw + heads, 6 * fw + 3 * heads
    win_al = jnp.concatenate(
        [win_full[:, :o_f], win_full[:, o_qkv:o_a], win_full[:, o_z:], win_full[:, o_f:o_qkv],
         win_full[:, o_a:o_z], jnp.zeros((d, in_pad - in_w), BF16)], axis=1)
    wout_full = wout_g.reshape(d, d)

    h2 = _norm_mod(x1, g_all[1:2], sc2, sh2, "norm_mod_2")
    proj = _mm(h2, win_al, name="in_proj", tn=1536)
    cum = _fox_gate(proj, small_blk, bias_lane)
    cum_t = cum[:, :heads].T
    cum_row = cum_t[:, None, :]
    cum_col = jnp.broadcast_to(cum_t[:, :, None], (heads, s, LANES))
    (o_raw, lse, lse_row, o_fox), (wg1, wu1, wd1) = _fox_fwd(
        proj, cum_col, cum_row, fox_out_norm, heads,
        carry=([layer(wg_sh, 1), layer(wu_sh, 1), layer(wd_sh, 1)], False))
    qg, kg, vg = _gdn_pre(proj, conv_all, heads)
    o_gdn, states, tinvs = _gdn_fwd(qg, kg, vg, proj, 6, small_blk, a_lane, dt_lane, gdn_out_norm)
    o_cat = jnp.concatenate([o_fox, o_gdn], axis=1)
    mix, x2 = _mm(o_cat, wout_full, name="out_proj", residual=(x1, gt2))

    h3 = _norm_mod(x2, g_all[2:3], sc3, sh3, "norm_mod_3")
    (a3, b3, s3), _ = _ffn_up(h3, wg1, wu1, 0, "ffn2_up")
    (f3, x3), _ = _ffn_down(s3, wd1, 0, x2, gt3, "ffn2_down")

    loss_row, dx3, d_final = _final(x3, loss_target[0], final_norm.reshape(1, d))
    loss = lax.psum(loss_row[0, 0], MESH_AXES)

    df3, dgt3 = _gate_bwd(dx3, f3, gt3, MACARON_W, "ffn2_gate_bwd")
    (da3, db3), _ = _ffn_bwd_act(df3, wd1, 0, a3, b3, "ffn2_bwd_act")
    (dwd2,), _ = _ffn_bwd_wd(s3, df3, "ffn2_bwd_wd")
    (dh3,), (r_wd2,) = _ffn_bwd_h(da3, db3, wg1, wu1, 0, "ffn2_bwd_h", carry=([dwd2], True))
    dwg2, dwu2 = _ffn_bwd_wgu(h3, da3, db3, "ffn2_bwd_wgu")
    dx2, dsh3, dsc3, dg3 = _norm_mod_bwd(x2, dh3, dx3, g_all[2:3], sc3, "norm_mod_3_bwd")

    dmix, dgt2 = _gate_bwd(dx2, mix, gt2, 1.0, "mix_gate_bwd")
    do_cat = _mm(dmix, wout_full, tb=True, name="out_proj_bwd_x")
    dwout = _mm(o_cat, dmix, ta=True, out_dtype=BF16, name="out_proj_bwd_w", tk=512)
    do_fox, delta, delta_row, d_foxw = _fox_prep_bwd(do_cat, o_raw, fox_out_norm, heads)
    (dk_f, dv_f, dcum_k, dq_heads, dcum_q), (r_wg2, r_wu2, r_wout) = _fox_bwd(
        proj, do_fox, cum_col, cum_row, lse_row, delta_row, heads,
        carry=([dwg2, dwu2, dwout.reshape(N_DEV, d // N_DEV, d)], True))
    dq_f = dq_heads.transpose(1, 0, 2).reshape(s, fw)
    head_lanes = lambda t: jnp.pad(t.reshape(heads, s).T, ((0, 0), (0, LANES - heads)))
    dsm_fox, d_fbias = _fox_gate_bwd(head_lanes(dcum_q), head_lanes(dcum_k), proj, small_blk, bias_lane)
    dqg, dkg, dvg, dz, dsm_gdn, d_alog, d_dt, d_gdnw = _gdn_bwd(
        qg, kg, vg, proj, 6, small_blk, a_lane, dt_lane, gdn_out_norm, states, tinvs, do_cat, 1)
    dxc, d_conv = _gdn_pre_bwd_act(proj, conv_all, dqg, dkg, dvg, heads)
    dqkv = _gdn_pre_bwd_conv(dxc, conv_all)
    dsmall = (dsm_fox + dsm_gdn).astype(BF16)
    dproj = jnp.concatenate([dq_f, dk_f, dv_f, dqkv, dz, dsmall, jnp.zeros((s, in_pad - 7 * fw - LANES), BF16)], axis=1)
    dh2 = _mm(dproj, win_al, tb=True, name="in_proj_bwd_x", tk=1536)
    dwin_al = _mm(h2, dproj, ta=True, out_dtype=BF16, name="in_proj_bwd_w", tm=2048, tn=1536, tk=1024)
    dwin_full = jnp.concatenate(
        [dwin_al[:, :o_f], dwin_al[:, 7 * fw:7 * fw + heads], dwin_al[:, o_f:o_f + 3 * fw],
         dwin_al[:, 7 * fw + heads:7 * fw + 3 * heads], dwin_al[:, 6 * fw:7 * fw]], axis=1)
    dwin_parts = dwin_full.reshape(d, N_DEV, in_w // N_DEV).transpose(1, 0, 2)
    dx1, dsh2, dsc2, dg2 = _norm_mod_bwd(x1, dh2, dx2, g_all[1:2], sc2, "norm_mod_2_bwd")

    df1, dgt1 = _gate_bwd(dx1, f1, gt1, MACARON_W, "ffn1_gate_bwd")
    (da1, db1), (r_win,) = _ffn_bwd_act(df1, wd0, 0, a1, b1, "ffn1_bwd_act", carry=([dwin_parts], True))
    dwg1, dwu1 = _ffn_bwd_wgu(h1, da1, db1, "ffn1_bwd_wgu")
    (dwd1,), (r_wg1,) = _ffn_bwd_wd(s1, df1, "ffn1_bwd_wd", carry=([dwg1], True))
    (dh1,), (r_wu1, r_wd1) = _ffn_bwd_h(da1, db1, wg0, wu0, 0, "ffn1_bwd_h", carry=([dwu1, dwd1], True))
    grad_x, dsh1, dsc1, dg1 = _norm_mod_bwd(x0, dh1, dx1, g_all[0:1], sc1, "norm_mod_1_bwd")

    dmod = jnp.concatenate([dsh1, dsc1, dgt1, dsh2, dsc2, dgt2, dsh3, dsc3, dgt3], axis=1)
    dmod_all = _gather_row(dmod, "gather_dmod")
    ct_pad = jnp.pad(c_all.T, ((0, 0), (0, LANES - N_DEV)))
    dmod_mine = jnp.pad(_my_cols(dmod_all, me, nada), ((0, LANES - N_DEV), (0, 0)))
    g_ada_w = _ada_bwd(ct_pad, dmod_mine)

    g_small_cols = [d_fbias, d_foxw, d_alog[:, heads:], d_dt[:, heads:], d_gdnw]
    small_part = jnp.concatenate(
        [_pad_lanes(v[:, :LANES]) for v in g_small_cols]
        + [d_final, dg1, dg2, dg3] + [d_conv[k:k + 1] for k in range(CONV_W)], axis=1)
    small_all = _gather_row(small_part, "gather_small_grads")
    off = 5 * LANES
    w_small = jnp.concatenate(
        [_pad_lanes(fox_f_bias), fox_out_norm, _pad_lanes(gdn_A_log), _pad_lanes(gdn_dt_bias), gdn_out_norm,
         final_norm.reshape(1, d)], axis=1)
    m_small = jnp.concatenate(
        [_pad_lanes(m_fox_f_bias), m_fox_out_norm, _pad_lanes(m_gdn_A_log), _pad_lanes(m_gdn_dt_bias),
         m_gdn_out_norm, m_final_norm.reshape(1, d)], axis=1)
    v_small = jnp.concatenate(
        [_pad_lanes(v_fox_f_bias), v_fox_out_norm, _pad_lanes(v_gdn_A_log), _pad_lanes(v_gdn_dt_bias),
         v_gdn_out_norm, v_final_norm.reshape(1, d)], axis=1)
    rep = _adamw(small_all[:, None, :off + d], w_small, m_small, v_small, "adamw_replicated")
    ab = _adamw(dmod_all[:, None, :], ada_b, m_ada_b, v_ada_b, "adamw_ada_b")
    g_ng = small_all[:, off + d:off + 4 * d].reshape(N_DEV, 3, d)
    ngs = _adamw(_my_cols(g_ng, me, ng), norm_g[0], m_norm_g[0], v_norm_g[0], "adamw_norm_g")
    g_cv = small_all[:, off + 4 * d:].reshape(N_DEV, CONV_W, 3 * fw)
    cvs = _adamw(_my_cols(g_cv, me, ncv), gdn_conv[0], m_gdn_conv[0], v_gdn_conv[0], "adamw_gdn_conv")

    wgs = _adamw_layers(r_wg1, r_wg2, ffn_w_gate, m_ffn_w_gate, v_ffn_w_gate, "adamw_w_gate")
    wus = _adamw_layers(r_wu1, r_wu2, ffn_w_up, m_ffn_w_up, v_ffn_w_up, "adamw_w_up")
    wds = _adamw_layers(r_wd1, r_wd2, ffn_w_down, m_ffn_w_down, v_ffn_w_down, "adamw_w_down")
    wis = [o[None] for o in _adamw(r_win, w_in[0], m_w_in[0], v_w_in[0], "adamw_w_in")]
    wos = [o[None] for o in _adamw(r_wout, w_out[0], m_w_out[0], v_w_out[0], "adamw_w_out")]
    adas = [o[None] for o in _adamw(g_ada_w[None], ada_w[0], m_ada_w[0], v_ada_w[0], "adamw_ada_w")]
    ngs = [o[None] for o in ngs]
    cvs = [o[None] for o in cvs]

    def rep_piece(i, lo, width):
        return rep[i][:, lo:lo + width]

    nh = fox_f_bias.shape[1]
    outs = []
    for i in range(4):
        outs.append([adas[i], ab[i], ngs[i], wgs[i], wus[i], wds[i], wis[i], wos[i],
                     rep_piece(i, 0, nh), rep_piece(i, LANES, HEAD_DIM), cvs[i], rep_piece(i, 2 * LANES, nh),
                     rep_piece(i, 3 * LANES, nh), rep_piece(i, 4 * LANES, HEAD_DIM), rep_piece(i, off, d).reshape(d)])
    return (loss, grad_x[None], *outs[0], *outs[1], *outs[2], *outs[3])
```

```python
import math

import numpy as np
import jax
import jax.numpy as jnp
from jax import lax
from jax.experimental import pallas as pl
from jax.experimental.pallas import tpu as pltpu

F32 = jnp.float32
BF16 = jnp.bfloat16

N_DEV = 8
MESH_AXES = ("x", "y", "c")
LANES = 128
HEAD_DIM = 128
GDN_CHUNK = 64
CONV_W = 4
N_MOD = 9
MACARON_W = 0.5
EPS = 1e-6
NEG = -1e30
VMEM_LIMIT_BYTES = 56 * 2 ** 20
ADAM_BLOCK_BYTES = 4 * 2 ** 20
GDN_HP_WIDTH = 4
FOX_BWD_HEADS_PER_STEP = 4
FOX_HEADS_PER_STEP = 8

ADAM_LR = 0.001
ADAM_B1 = 0.9
ADAM_B2 = 0.999
ADAM_EPS = 1e-08
ADAM_WD = 0.01
ADAM_STEP = 10

MESH_ID = pl.DeviceIdType.MESH
ANY = pl.BlockSpec(memory_space=pl.ANY)
VMEM = pl.BlockSpec(memory_space=pltpu.VMEM)


def _pcall(body, **kw):
    return pl.pallas_call(body, **kw)


def _params(*semantics):
    return pltpu.CompilerParams(dimension_semantics=semantics, vmem_limit_bytes=VMEM_LIMIT_BYTES)


def _tile(n, pref):
    if n % pref == 0 and pref % 8 == 0:
        return pref
    t = 1 << (max(1, min(n, pref)).bit_length() - 1)
    while n % t:
        t //= 2
    return t if t % 8 == 0 else n


def _sigmoid(x):
    return 1.0 / (1.0 + jnp.exp(-x))


def _dot(a, b, dims):
    return lax.dot_general(a.astype(BF16), b.astype(BF16), (dims, ((), ())), preferred_element_type=F32)


NN = ((1,), (0,))
NT = ((1,), (1,))
TN = ((0,), (0,))


def _split3(x):
    hi = x.astype(BF16)
    r1 = x - hi.astype(F32)
    mid = r1.astype(BF16)
    lo = (r1 - mid.astype(F32)).astype(BF16)
    return hi, mid, lo


def _dot_exact_lhs(m_bf16, x, dims=NN):
    hi, mid, lo = _split3(x)
    d = lambda p: lax.dot_general(m_bf16, p, (dims, ((), ())), preferred_element_type=F32)
    return d(hi) + (d(mid) + d(lo))


def _dot_hp(a, b, dims):
    ah = a.astype(BF16)
    al = (a - ah.astype(F32)).astype(BF16)
    bh = b.astype(BF16)
    bl = (b - bh.astype(F32)).astype(BF16)
    d = lambda p, q: lax.dot_general(p, q, (dims, ((), ())), preferred_element_type=F32)
    return d(ah, bh) + (d(ah, bl) + d(al, bh))


def _mesh_pos():
    return lax.axis_index("x"), lax.axis_index("y"), lax.axis_index("c")


def _peer(pos, mask):
    x, y, c = pos
    return (1 - x if mask & 4 else x, 1 - y if mask & 2 else y, 1 - c if mask & 1 else c)


def _linear(pos):
    return 4 * pos[0] + 2 * pos[1] + pos[2]


def _exchange_copies(ins, outs, sems, scatter, with_receives=True):
    send_sems, recv_sems, local_sems = sems
    pos = _mesh_pos()
    me = _linear(pos)
    local, sends, recvs = [], [], []
    for i in range(len(ins)):
        src = ins[i].at[me] if scatter else ins[i]
        local.append(pltpu.make_async_copy(src, outs[i].at[me], local_sems.at[i]))
    for mask in range(1, N_DEV):
        peer = _peer(pos, mask)
        for i in range(len(ins)):
            sem = dict(send_sem=send_sems.at[i, mask - 1], recv_sem=recv_sems.at[i, mask - 1],
                       device_id=peer, device_id_type=MESH_ID)
            sends.append(pltpu.make_async_remote_copy(
                src_ref=ins[i].at[_linear(peer)] if scatter else ins[i], dst_ref=outs[i].at[me], **sem))
            if with_receives:
                recvs.append(pltpu.make_async_remote_copy(
                    src_ref=ins[i].at[me] if scatter else ins[i], dst_ref=outs[i].at[_linear(peer)], **sem))
    return local, sends, recvs


def _exchange_start(ins, outs, sems, scatter):
    local, sends, _ = _exchange_copies(ins, outs, sems, scatter, with_receives=False)
    for cp in local + sends:
        cp.start()


def _exchange_wait(ins, outs, sems, scatter):
    local, sends, recvs = _exchange_copies(ins, outs, sems, scatter)
    for cp in recvs:
        cp.wait_recv()
    for cp in sends:
        cp.wait_send()
    for cp in local:
        cp.wait()


def _exchange_sems(n):
    return [pltpu.SemaphoreType.DMA((n, N_DEV - 1)), pltpu.SemaphoreType.DMA((n, N_DEV - 1)),
            pltpu.SemaphoreType.DMA((n,))]


def _exchange_shapes(arrays, scatter):
    return [jax.ShapeDtypeStruct(a.shape if scatter else (N_DEV,) + a.shape, a.dtype) for a in arrays]


def _exchange(arrays, *, scatter, in_vmem, name):
    n = len(arrays)

    def body(*refs):
        ins, outs, sems = refs[:n], refs[n:2 * n], refs[2 * n:]
        _exchange_start(ins, outs, sems, scatter)
        _exchange_wait(ins, outs, sems, scatter)

    spec = VMEM if in_vmem else ANY
    outs = _pcall(
        body, name=name, out_shape=_exchange_shapes(arrays, scatter),
        in_specs=[spec] * n, out_specs=[spec] * n, scratch_shapes=_exchange_sems(n),
    )(*arrays)
    return list(outs)


def _gather_via_sibling(arrays, name):
    n = len(arrays)

    def body(*refs):
        ins, outs = refs[:n], refs[n:2 * n]
        send_sems, recv_sems, local_sems = refs[2 * n:]
        x, y, c = _mesh_pos()
        me, sibling = (x, y, c), (x, y, 1 - c)
        chips = [(1 - x, y), (x, 1 - y), (1 - x, 1 - y)]

        def copy(i, k, block, to, from_input=False):
            return pltpu.make_async_remote_copy(
                src_ref=ins[i] if from_input else outs[i].at[_linear(block)], dst_ref=outs[i].at[_linear(block)],
                send_sem=send_sems.at[i, k], recv_sem=recv_sems.at[i, k], device_id=to, device_id_type=MESH_ID)

        mine = [pltpu.make_async_copy(ins[i], outs[i].at[_linear(me)], local_sems.at[i]) for i in range(n)]
        first = []
        for i in range(n):
            first.append(copy(i, 0, me, sibling, True))
            first += [copy(i, 1 + j, me, (*chip, c), True) for j, chip in enumerate(chips)]
        for cp in mine + first:
            cp.start()
        passed = []
        for j, chip in enumerate(chips):
            for i in range(n):
                copy(i, 1 + j, (*chip, c), me).wait_recv()
                cp = copy(i, 4 + j, (*chip, c), sibling)
                cp.start()
                passed.append(cp)
        for i in range(n):
            copy(i, 0, sibling, me).wait_recv()
            for j, chip in enumerate(chips):
                copy(i, 4 + j, (*chip, 1 - c), me).wait_recv()
        for cp in first + passed:
            cp.wait_send()
        for cp in mine:
            cp.wait()

    outs = _pcall(
        body, name=name, out_shape=_exchange_shapes(arrays, False), in_specs=[ANY] * n, out_specs=[ANY] * n,
        scratch_shapes=_exchange_sems(n),
    )(*arrays)
    return list(outs)


def _hosted(body, *, name, grid, in_specs, out_specs, out_shape, args, scratch_shapes=(), prefetch=(), carry=None):
    n_in, n_out, n_scr, n_pre = len(in_specs), len(out_shape), len(scratch_shapes), len(prefetch)
    arrays, scatter = carry if carry is not None else ([], False)
    n = len(arrays)

    def wrapped(*refs):
        pre, r = refs[:n_pre], refs[n_pre:]
        host_in, comm_in = r[:n_in], r[n_in:n_in + n]
        r = r[n_in + n:]
        host_out, comm_out = r[:n_out], r[n_out:n_out + n]
        r = r[n_out + n:]
        host_scr, sems = r[:n_scr], r[n_scr:]
        if n:
            first = pl.program_id(0) == 0
            last = pl.program_id(0) == grid[0] - 1
            for ax in range(1, len(grid)):
                first = first & (pl.program_id(ax) == 0)
                last = last & (pl.program_id(ax) == grid[ax] - 1)

            @pl.when(first)
            def _():
                _exchange_start(comm_in, comm_out, sems, scatter)

        body(*pre, *host_in, *host_out, *host_scr)
        if n:
            @pl.when(last)
            def _():
                _exchange_wait(comm_in, comm_out, sems, scatter)

    grid_spec = pltpu.PrefetchScalarGridSpec(
        num_scalar_prefetch=n_pre, grid=grid, in_specs=list(in_specs) + [ANY] * n,
        out_specs=list(out_specs) + [ANY] * n,
        scratch_shapes=list(scratch_shapes) + (_exchange_sems(n) if n else []))
    outs = _pcall(
        wrapped, name=name, grid_spec=grid_spec, out_shape=list(out_shape) + _exchange_shapes(arrays, scatter),
        compiler_params=_params(*(["arbitrary"] * len(grid))),
    )(*prefetch, *args, *arrays)
    return list(outs[:n_out]), list(outs[n_out:])


def _gather_row(v, name):
    return _exchange([v], scatter=False, in_vmem=True, name=name)[0].reshape(N_DEV, v.shape[1])


def _ada_fwd(c_all, w, b):
    d, n = w.shape
    tn = _tile(n, 256)

    def body(c_ref, w_ref, b_ref, o_ref):
        cv = c_ref[...]
        cond = cv * _sigmoid(cv)
        o_ref[...] = _dot_hp(cond, w_ref[...], NN) + b_ref[...]

    return _pcall(
        body, name="ada_fwd", grid=(n // tn,),
        in_specs=[pl.BlockSpec((N_DEV, d), lambda j: (0, 0)), pl.BlockSpec((d, tn), lambda j: (0, j)),
                  pl.BlockSpec((1, tn), lambda j: (0, j))],
        out_specs=pl.BlockSpec((N_DEV, tn), lambda j: (0, j)),
        out_shape=jax.ShapeDtypeStruct((N_DEV, n), F32), compiler_params=_params("parallel"),
    )(c_all, w, b)


def _ada_bwd(ct_pad, dmod_pad):
    d = ct_pad.shape[0]
    n = dmod_pad.shape[1]
    tn = _tile(n, 256)

    def body(c_ref, g_ref, o_ref):
        cv = c_ref[...]
        cond = cv * _sigmoid(cv)
        o_ref[...] = _dot_hp(cond, g_ref[...], NN)

    return _pcall(
        body, name="ada_bwd", grid=(n // tn,),
        in_specs=[pl.BlockSpec((d, LANES), lambda j: (0, 0)), pl.BlockSpec((LANES, tn), lambda j: (0, j))],
        out_specs=pl.BlockSpec((d, tn), lambda j: (0, j)),
        out_shape=jax.ShapeDtypeStruct((d, n), F32), compiler_params=_params("parallel"),
    )(ct_pad, dmod_pad)


def _norm_mod(x, g, sc, sh, name):
    s, d = x.shape
    ts = _tile(s, 512)

    def body(x_ref, g_ref, sc_ref, sh_ref, h_ref):
        xv = x_ref[...]
        r = lax.rsqrt(jnp.mean(xv * xv, axis=-1, keepdims=True) + EPS)
        h_ref[...] = (xv * r * g_ref[...] * (1.0 + sc_ref[...]) + sh_ref[...]).astype(BF16)

    row = pl.BlockSpec((1, d), lambda i: (0, 0))
    return _pcall(
        body, name=name, grid=(s // ts,),
        in_specs=[pl.BlockSpec((ts, d), lambda i: (i, 0)), row, row, row],
        out_specs=pl.BlockSpec((ts, d), lambda i: (i, 0)),
        out_shape=jax.ShapeDtypeStruct((s, d), BF16), compiler_params=_params("parallel"),
    )(x, g, sc, sh)


def _norm_mod_bwd(x, dh, dx_out, g, sc, name):
    s, d = x.shape
    ts = _tile(s, 512)

    def body(x_ref, dh_ref, dxo_ref, g_ref, sc_ref, dx_ref, dsh_ref, dsc_ref, dg_ref):
        @pl.when(pl.program_id(0) == 0)
        def _():
            dsh_ref[...] = jnp.zeros_like(dsh_ref)
            dsc_ref[...] = jnp.zeros_like(dsc_ref)
            dg_ref[...] = jnp.zeros_like(dg_ref)

        xv = x_ref[...]
        dh_v = dh_ref[...]
        gv = g_ref[...]
        one_sc = 1.0 + sc_ref[...]
        r = lax.rsqrt(jnp.mean(xv * xv, axis=-1, keepdims=True) + EPS)
        xn = xv * r
        dxn = dh_v * (gv * one_sc)
        dx_ref[...] = dxo_ref[...] + r * (dxn - xn * jnp.mean(dxn * xn, axis=-1, keepdims=True))
        t = dh_v * xn
        dsh_ref[...] += jnp.sum(dh_v, axis=0, keepdims=True)
        dsc_ref[...] += jnp.sum(t * gv, axis=0, keepdims=True)
        dg_ref[...] += jnp.sum(t * one_sc, axis=0, keepdims=True)

    blk = pl.BlockSpec((ts, d), lambda i: (i, 0))
    row = pl.BlockSpec((1, d), lambda i: (0, 0))
    return _pcall(
        body, name=name, grid=(s // ts,),
        in_specs=[blk, blk, blk, row, row], out_specs=[blk, row, row, row],
        out_shape=[jax.ShapeDtypeStruct((s, d), F32)] + [jax.ShapeDtypeStruct((1, d), F32)] * 3,
        compiler_params=_params("arbitrary"),
    )(x, dh, dx_out, g, sc)


def _gate_bwd(dx, f, gt, k, name):
    s, d = dx.shape
    ts = _tile(s, 512)

    def body(dx_ref, f_ref, gt_ref, df_ref, dgt_ref):
        @pl.when(pl.program_id(0) == 0)
        def _():
            dgt_ref[...] = jnp.zeros_like(dgt_ref)

        dxv = dx_ref[...]
        df_ref[...] = ((k * gt_ref[...]) * dxv).astype(BF16)
        dgt_ref[...] += k * jnp.sum(f_ref[...] * dxv, axis=0, keepdims=True)

    blk = pl.BlockSpec((ts, d), lambda i: (i, 0))
    row = pl.BlockSpec((1, d), lambda i: (0, 0))
    return _pcall(
        body, name=name, grid=(s // ts,),
        in_specs=[blk, blk, row], out_specs=[blk, row],
        out_shape=[jax.ShapeDtypeStruct((s, d), BF16), jax.ShapeDtypeStruct((1, d), F32)],
        compiler_params=_params("arbitrary"),
    )(dx, f, gt)


def _ffn_up(h, wg, wu, layer, name, carry=None):
    s, d = h.shape
    fs = wg.shape[-1]
    tm = _tile(s, 1024)

    def body(h_ref, wg_ref, wu_ref, a_ref, b_ref, s_ref):
        hv = h_ref[...]
        a = jnp.dot(hv, wg_ref[...], preferred_element_type=F32)
        b = jnp.dot(hv, wu_ref[...], preferred_element_type=F32)
        a_ref[...] = a.astype(BF16)
        b_ref[...] = b.astype(BF16)
        s_ref[...] = (a * _sigmoid(a) * b).astype(BF16)

    wspec = pl.BlockSpec((None, None, d, fs), lambda j, m: (j, layer, 0, 0))
    ospec = pl.BlockSpec((None, tm, fs), lambda j, m: (j, m, 0))
    return _hosted(
        body, name=name, grid=(N_DEV, s // tm),
        in_specs=[pl.BlockSpec((tm, d), lambda j, m: (m, 0)), wspec, wspec],
        out_specs=[ospec, ospec, ospec],
        out_shape=[jax.ShapeDtypeStruct((N_DEV, s, fs), BF16)] * 3,
        args=(h, wg, wu), carry=carry)


def _ffn_down(sv, wd, layer, x_in, gt, name, carry=None):
    _, s, fs = sv.shape
    d = wd.shape[-1]
    tm = _tile(s, 512)

    def body(s_ref, wd_ref, x_ref, gt_ref, f_ref, xo_ref, acc):
        j = pl.program_id(1)

        @pl.when(j == 0)
        def _():
            acc[...] = jnp.zeros_like(acc)

        acc[...] += jnp.dot(s_ref[...], wd_ref[...], preferred_element_type=F32)

        @pl.when(j == N_DEV - 1)
        def _():
            fv = acc[...]
            f_ref[...] = fv
            xo_ref[...] = x_ref[...] + (MACARON_W * gt_ref[...]) * fv

    blk = pl.BlockSpec((tm, d), lambda m, j: (m, 0))
    return _hosted(
        body, name=name, grid=(s // tm, N_DEV),
        in_specs=[pl.BlockSpec((None, tm, fs), lambda m, j: (j, m, 0)),
                  pl.BlockSpec((None, None, fs, d), lambda m, j: (j, layer, 0, 0)),
                  blk, pl.BlockSpec((1, d), lambda m, j: (0, 0))],
        out_specs=[blk, blk],
        out_shape=[jax.ShapeDtypeStruct((s, d), F32)] * 2,
        scratch_shapes=[pltpu.VMEM((tm, d), F32)],
        args=(sv, wd, x_in, gt), carry=carry)


def _ffn_bwd_act(df, wd, layer, a, b, name, carry=None):
    s, d = df.shape
    fs = a.shape[-1]
    tm = _tile(s, 1024)

    def body(df_ref, wd_ref, a_ref, b_ref, da_ref, db_ref):
        ds = lax.dot_general(df_ref[...], wd_ref[...], (NT, ((), ())), preferred_element_type=F32)
        av = a_ref[...].astype(F32)
        sg = _sigmoid(av)
        da_ref[...] = (ds * b_ref[...].astype(F32) * (sg * (1.0 + av * (1.0 - sg)))).astype(BF16)
        db_ref[...] = (ds * (av * sg)).astype(BF16)

    hid = pl.BlockSpec((None, tm, fs), lambda j, m: (j, m, 0))
    return _hosted(
        body, name=name, grid=(N_DEV, s // tm),
        in_specs=[pl.BlockSpec((tm, d), lambda j, m: (m, 0)),
                  pl.BlockSpec((None, None, fs, d), lambda j, m: (j, layer, 0, 0)), hid, hid],
        out_specs=[hid, hid],
        out_shape=[jax.ShapeDtypeStruct((N_DEV, s, fs), BF16)] * 2,
        args=(df, wd, a, b), carry=carry)


def _ffn_bwd_wd(sv, df, name, carry=None):
    _, s, fs = sv.shape
    d = df.shape[1]
    tk = _tile(s, 1024)
    nk = s // tk

    def body(s_ref, df_ref, o_ref, acc):
        @pl.when(pl.program_id(1) == 0)
        def _():
            acc[...] = jnp.zeros_like(acc)

        acc[...] += lax.dot_general(s_ref[...], df_ref[...], (TN, ((), ())), preferred_element_type=F32)

        @pl.when(pl.program_id(1) == nk - 1)
        def _():
            o_ref[...] = acc[...].astype(BF16)

    return _hosted(
        body, name=name, grid=(N_DEV, nk),
        in_specs=[pl.BlockSpec((None, tk, fs), lambda j, k: (j, k, 0)), pl.BlockSpec((tk, d), lambda j, k: (k, 0))],
        out_specs=[pl.BlockSpec((None, fs, d), lambda j, k: (j, 0, 0))],
        out_shape=[jax.ShapeDtypeStruct((N_DEV, fs, d), BF16)],
        scratch_shapes=[pltpu.VMEM((fs, d), F32)],
        args=(sv, df), carry=carry)


def _ffn_bwd_h(da, db, wg, wu, layer, name, carry=None):
    _, s, fs = da.shape
    d = wg.shape[-2]
    tm = _tile(s, 1024)

    def body(da_ref, db_ref, wg_ref, wu_ref, o_ref, acc):
        j = pl.program_id(1)

        @pl.when(j == 0)
        def _():
            acc[...] = jnp.zeros_like(acc)

        acc[...] += (lax.dot_general(da_ref[...], wg_ref[...], (NT, ((), ())), preferred_element_type=F32)
                     + lax.dot_general(db_ref[...], wu_ref[...], (NT, ((), ())), preferred_element_type=F32))

        @pl.when(j == N_DEV - 1)
        def _():
            o_ref[...] = acc[...]

    hid = pl.BlockSpec((None, tm, fs), lambda m, j: (j, m, 0))
    wspec = pl.BlockSpec((None, None, d, fs), lambda m, j: (j, layer, 0, 0))
    return _hosted(
        body, name=name, grid=(s // tm, N_DEV),
        in_specs=[hid, hid, wspec, wspec],
        out_specs=[pl.BlockSpec((tm, d), lambda m, j: (m, 0))],
        out_shape=[jax.ShapeDtypeStruct((s, d), F32)],
        scratch_shapes=[pltpu.VMEM((tm, d), F32)],
        args=(da, db, wg, wu), carry=carry)


def _ffn_bwd_wgu(h, da, db, name, carry=None):
    s, d = h.shape
    fs = da.shape[-1]
    tk = _tile(s, 1024)
    nk = s // tk

    def body(h_ref, da_ref, db_ref, og_ref, ou_ref, accg, accu):
        @pl.when(pl.program_id(1) == 0)
        def _():
            accg[...] = jnp.zeros_like(accg)
            accu[...] = jnp.zeros_like(accu)

        hv = h_ref[...]
        accg[...] += lax.dot_general(hv, da_ref[...], (TN, ((), ())), preferred_element_type=F32)
        accu[...] += lax.dot_general(hv, db_ref[...], (TN, ((), ())), preferred_element_type=F32)

        @pl.when(pl.program_id(1) == nk - 1)
        def _():
            og_ref[...] = accg[...].astype(BF16)
            ou_ref[...] = accu[...].astype(BF16)

    hid = pl.BlockSpec((None, tk, fs), lambda j, k: (j, k, 0))
    ospec = pl.BlockSpec((None, d, fs), lambda j, k: (j, 0, 0))
    return _hosted(
        body, name=name, grid=(N_DEV, nk),
        in_specs=[pl.BlockSpec((tk, d), lambda j, k: (k, 0)), hid, hid],
        out_specs=[ospec, ospec],
        out_shape=[jax.ShapeDtypeStruct((N_DEV, d, fs), BF16)] * 2,
        scratch_shapes=[pltpu.VMEM((d, fs), F32), pltpu.VMEM((d, fs), F32)],
        args=(h, da, db), carry=carry)


def _mm(a, b, *, ta=False, tb=False, out_dtype=F32, name, tm=1024, tn=1024, tk=2048, residual=None):
    m, kdim = (a.shape[1], a.shape[0]) if ta else a.shape
    n = b.shape[0] if tb else b.shape[1]
    tm, tn, tk = _tile(m, tm), _tile(n, tn), _tile(kdim, tk)
    nk = kdim // tk
    dims = ((0,) if ta else (1,), (1,) if tb else (0,))

    def body(*refs):
        a_ref, b_ref = refs[:2]
        acc = refs[-1]
        kk = pl.program_id(2)

        @pl.when(kk == 0)
        def _():
            acc[...] = jnp.zeros_like(acc)

        acc[...] += lax.dot_general(a_ref[...].astype(BF16), b_ref[...].astype(BF16), (dims, ((), ())),
                                    preferred_element_type=F32)

        @pl.when(kk == nk - 1)
        def _():
            if residual is None:
                refs[2][...] = acc[...].astype(out_dtype)
            else:
                res_ref, gate_ref, y_ref, xo_ref = refs[2:6]
                yv = acc[...]
                y_ref[...] = yv
                xo_ref[...] = res_ref[...] + gate_ref[...] * yv

    a_spec = pl.BlockSpec((tk, tm), lambda i, j, k: (k, i)) if ta else pl.BlockSpec((tm, tk), lambda i, j, k: (i, k))
    b_spec = pl.BlockSpec((tn, tk), lambda i, j, k: (j, k)) if tb else pl.BlockSpec((tk, tn), lambda i, j, k: (k, j))
    o_spec = pl.BlockSpec((tm, tn), lambda i, j, k: (i, j))
    if residual is None:
        in_specs, out_specs = [a_spec, b_spec], o_spec
        out_shape = jax.ShapeDtypeStruct((m, n), out_dtype)
        args = (a, b)
    else:
        in_specs = [a_spec, b_spec, o_spec, pl.BlockSpec((1, tn), lambda i, j, k: (0, j))]
        out_specs = [o_spec, o_spec]
        out_shape = [jax.ShapeDtypeStruct((m, n), F32)] * 2
        args = (a, b) + tuple(residual)
    return _pcall(
        body, name=name, grid=(m // tm, n // tn, nk), in_specs=in_specs, out_specs=out_specs, out_shape=out_shape,
        scratch_shapes=[pltpu.VMEM((tm, tn), F32)],
        compiler_params=_params("parallel", "parallel", "arbitrary"),
    )(*args)


def _log_sigmoid(z):
    return jnp.minimum(z, 0.0) - jnp.log(1.0 + jnp.exp(-jnp.abs(z)))


def _fox_gate(proj, small_blk, bias_lane):
    s = proj.shape[0]
    ts = _tile(s, 1024)
    nsub = ts // LANES

    def body(z_ref, b_ref, cum_ref, carry):
        @pl.when(pl.program_id(0) == 0)
        def _():
            carry[...] = jnp.zeros_like(carry)

        ii = lax.broadcasted_iota(jnp.int32, (LANES, LANES), 0)
        jj = lax.broadcasted_iota(jnp.int32, (LANES, LANES), 1)
        tri = (ii >= jj).astype(BF16)
        logf = _log_sigmoid(z_ref[...] + b_ref[...])
        cv = carry[...]
        for sb in range(nsub):
            blk = logf[sb * LANES:(sb + 1) * LANES, :]
            cum_ref[sb * LANES:(sb + 1) * LANES, :] = _dot_exact_lhs(tri, blk) + cv
            cv = cv + jnp.sum(blk, axis=0, keepdims=True)
        carry[...] = cv

    return _pcall(
        body, name="fox_gate", grid=(s // ts,),
        in_specs=[pl.BlockSpec((ts, LANES), lambda i: (i, small_blk)), pl.BlockSpec((1, LANES), lambda i: (0, 0))],
        out_specs=pl.BlockSpec((ts, LANES), lambda i: (i, 0)),
        out_shape=jax.ShapeDtypeStruct((s, LANES), F32),
        scratch_shapes=[pltpu.VMEM((1, LANES), F32)],
        compiler_params=_params("arbitrary"),
    )(proj, bias_lane)


def _fox_gate_bwd(dcum_q, dcum_k, proj, small_blk, bias_lane):
    s = proj.shape[0]
    ts = _tile(s, 1024)
    nsub = ts // LANES
    nb = s // ts

    def body(dcq_ref, dc_ref, z_ref, b_ref, dz_ref, db_ref, carry):
        @pl.when(pl.program_id(0) == 0)
        def _():
            carry[...] = jnp.zeros_like(carry)
            db_ref[...] = jnp.zeros_like(db_ref)

        ii = lax.broadcasted_iota(jnp.int32, (LANES, LANES), 0)
        jj = lax.broadcasted_iota(jnp.int32, (LANES, LANES), 1)
        triu = (jj >= ii).astype(BF16)
        dc = dcq_ref[...] + dc_ref[...]
        zb = z_ref[...] + b_ref[...]
        cv = carry[...]
        dbv = jnp.zeros((1, LANES), F32)
        for sb in reversed(range(nsub)):
            rows = slice(sb * LANES, (sb + 1) * LANES)
            blk = dc[rows, :]
            dlogf = _dot_exact_lhs(triu, blk) + cv
            cv = cv + jnp.sum(blk, axis=0, keepdims=True)
            dz = dlogf * _sigmoid(-zb[rows, :])
            dz_ref[rows, :] = dz
            dbv = dbv + jnp.sum(dz, axis=0, keepdims=True)
        carry[...] = cv
        db_ref[...] += dbv

    row = pl.BlockSpec((1, LANES), lambda i: (0, 0))
    return _pcall(
        body, name="fox_gate_bwd", grid=(nb,),
        in_specs=[pl.BlockSpec((ts, LANES), lambda i: (nb - 1 - i, 0)),
                  pl.BlockSpec((ts, LANES), lambda i: (nb - 1 - i, 0)),
                  pl.BlockSpec((ts, LANES), lambda i: (nb - 1 - i, small_blk)), row],
        out_specs=[pl.BlockSpec((ts, LANES), lambda i: (nb - 1 - i, 0)), row],
        out_shape=[jax.ShapeDtypeStruct((s, LANES), F32), jax.ShapeDtypeStruct((1, LANES), F32)],
        scratch_shapes=[pltpu.VMEM((1, LANES), F32)],
        compiler_params=_params("arbitrary"),
    )(dcum_q, dcum_k, proj, bias_lane)


def _tri_tables(n, by_key):
    if by_key:
        pairs = [(i, j) for j in range(n) for i in range(j, n)]
    else:
        pairs = [(i, j) for i in range(n) for j in range(i + 1)]
    return (jnp.asarray(np.array([p[0] for p in pairs], np.int32)),
            jnp.asarray(np.array([p[1] for p in pairs], np.int32)))


def _fox_group(heads):
    return FOX_HEADS_PER_STEP if heads % FOX_HEADS_PER_STEP == 0 else 1


def _as_row(col):
    t = col.shape[0]
    eye = lax.broadcasted_iota(jnp.int32, (t, t), 0) == lax.broadcasted_iota(jnp.int32, (t, t), 1)
    return jnp.sum(jnp.where(eye, col, 0.0), axis=0, keepdims=True)


LOG2E = 1.4426950408889634
FOX_Q_SCALE = LOG2E / math.sqrt(HEAD_DIM)


def _fox_scores(a, b, bias_col, bias_row, diagonal, rows_are_keys=False):
    sc = lax.dot_general(a.astype(BF16), b.astype(BF16), (NT, ((), ())), preferred_element_type=F32)
    sc = sc + (bias_col + bias_row)
    if not diagonal:
        return sc
    row = lax.broadcasted_iota(jnp.int32, sc.shape, 0)
    col = lax.broadcasted_iota(jnp.int32, sc.shape, 1)
    return jnp.where(row <= col if rows_are_keys else col <= row, sc, NEG)


def _fox_fwd(proj, cum_col, cum_row, w_norm, heads, carry=None):
    s = proj.shape[0]
    t = _tile(s, 512)
    grp = _fox_group(heads)
    qi, ki = _tri_tables(s // t, False)
    scale = 1.0 / math.sqrt(HEAD_DIM)

    def body(qi_ref, ki_ref, q_ref, k_ref, v_ref, cq_ref, ck_ref, w_ref, o_ref, lse_ref, lser_ref, on_ref, m_s, acc_s):
        iq, ik = qi_ref[pl.program_id(1)], ki_ref[pl.program_id(1)]

        @pl.when(ik == 0)
        def _():
            m_s[...] = jnp.full_like(m_s, NEG)
            acc_s[...] = jnp.zeros_like(acc_s)

        def step(diagonal):
            for g in range(grp):
                sl = slice(g * HEAD_DIM, (g + 1) * HEAD_DIM)
                qs = (q_ref[:, sl] * FOX_Q_SCALE).astype(BF16)
                sc = _fox_scores(qs, k_ref[:, sl], cq_ref[g, :, 0:1] * LOG2E, ck_ref[g] * (-LOG2E), diagonal)
                m_prev = m_s[g]
                m_new = jnp.maximum(m_prev, jnp.max(sc, axis=1, keepdims=True))
                p = jnp.exp2(sc - m_new).astype(BF16)
                v_ones = jnp.concatenate([v_ref[:, sl].astype(BF16), jnp.ones((t, LANES), BF16)], axis=1)
                acc_s[g] = jnp.exp2(m_prev - m_new) * acc_s[g] + jnp.dot(p, v_ones, preferred_element_type=F32)
                m_s[g] = m_new

        @pl.when(ik < iq)
        def _():
            step(False)

        @pl.when(ik == iq)
        def _():
            step(True)
            for g in range(grp):
                sl = slice(g * HEAD_DIM, (g + 1) * HEAD_DIM)
                acc = acc_s[g]
                o = acc[:, :HEAD_DIM] / acc[:, HEAD_DIM:]
                lse = m_s[g] + jnp.log(acc[:, HEAD_DIM:]) * LOG2E
                o_ref[:, sl] = o
                lse_ref[g] = lse
                lser_ref[g] = _as_row(lse[:, 0:1])
                r = lax.rsqrt(jnp.mean(o * o, axis=1, keepdims=True) + EPS)
                on_ref[:, sl] = (o * r * w_ref[...]).astype(BF16)

    ng = heads // grp
    qblk = pl.BlockSpec((t, grp * HEAD_DIM), lambda h, p, qi, ki: (qi[p], h))
    kblk = lambda off: pl.BlockSpec((t, grp * HEAD_DIM), lambda h, p, qi, ki: (ki[p], off + h))
    qcol = pl.BlockSpec((grp, t, LANES), lambda h, p, qi, ki: (h, qi[p], 0))
    return _hosted(
        body, name="fox_fwd", grid=(ng, int(qi.shape[0])), prefetch=(qi, ki),
        in_specs=[qblk, kblk(ng), kblk(2 * ng), qcol,
                  pl.BlockSpec((grp, 1, t), lambda h, p, qi, ki: (h, 0, ki[p])),
                  pl.BlockSpec((1, HEAD_DIM), lambda h, p, qi, ki: (0, 0))],
        out_specs=[qblk, qcol, pl.BlockSpec((grp, 1, t), lambda h, p, qi, ki: (h, 0, qi[p])), qblk],
        scratch_shapes=[pltpu.VMEM((grp, t, 1), F32), pltpu.VMEM((grp, t, 2 * HEAD_DIM), F32)],
        out_shape=[jax.ShapeDtypeStruct((s, heads * HEAD_DIM), F32), jax.ShapeDtypeStruct((heads, s, LANES), F32),
                   jax.ShapeDtypeStruct((heads, 1, s), F32), jax.ShapeDtypeStruct((s, heads * HEAD_DIM), BF16)],
        args=(proj, proj, proj, cum_col, cum_row, w_norm), carry=carry)


def _fox_prep_bwd(do_cat, o_raw, w_norm, heads):
    s = o_raw.shape[0]
    ts = _tile(s, 512)

    def body(g_ref, o_ref, w_ref, do_ref, delta_ref, deltar_ref, dw_ref):
        @pl.when((pl.program_id(0) == 0) & (pl.program_id(1) == 0))
        def _():
            dw_ref[...] = jnp.zeros_like(dw_ref)

        o = o_ref[...]
        g = g_ref[...]
        r = lax.rsqrt(jnp.mean(o * o, axis=1, keepdims=True) + EPS)
        wg = g * w_ref[...]
        do = r * wg - o * (r * r * r) * jnp.mean(wg * o, axis=1, keepdims=True)
        do_ref[...] = do.astype(BF16)
        delta = jnp.sum(do * o, axis=1, keepdims=True)
        delta_ref[...] = jnp.broadcast_to(delta, delta_ref.shape)
        deltar_ref[...] = _as_row(delta)
        dw_ref[...] += jnp.sum(g * o * r, axis=0, keepdims=True)

    blk = pl.BlockSpec((ts, HEAD_DIM), lambda h, i: (i, h))
    row = pl.BlockSpec((1, HEAD_DIM), lambda h, i: (0, 0))
    return _pcall(
        body, name="fox_prep_bwd", grid=(heads, s // ts),
        in_specs=[blk, blk, row],
        out_specs=[blk, pl.BlockSpec((None, ts, LANES), lambda h, i: (h, i, 0)),
                   pl.BlockSpec((None, 1, ts), lambda h, i: (h, 0, i)), row],
        out_shape=[jax.ShapeDtypeStruct((s, heads * HEAD_DIM), BF16), jax.ShapeDtypeStruct((heads, s, LANES), F32),
                   jax.ShapeDtypeStruct((heads, 1, s), F32), jax.ShapeDtypeStruct((1, HEAD_DIM), F32)],
        compiler_params=_params("arbitrary", "arbitrary"),
    )(do_cat, o_raw, w_norm)


def _fox_bwd(proj, do, cum_col, cum_row, lse_row, delta_row, heads, carry=None):
    s = proj.shape[0]
    t = _tile(s, 512)
    nk = s // t
    grp = FOX_BWD_HEADS_PER_STEP if heads % FOX_BWD_HEADS_PER_STEP == 0 else 1
    qi, ki = _tri_tables(nk, True)
    npairs = int(qi.shape[0])
    scale = 1.0 / math.sqrt(HEAD_DIM)

    def body(qi_ref, ki_ref, q_ref, k_ref, v_ref, do_ref, cqr_ref, ckc_ref, lse_ref, dl_ref,
             dk_ref, dv_ref, dck_ref, dq_hbm, dcq_ref, dk_acc, dv_acc, dck_acc, dq_acc, stage, sem):
        pair = pl.program_id(1)
        iq, ik = qi_ref[pair], ki_ref[pair]
        rows_q = pl.ds(pl.multiple_of(iq * t, t), t)

        @pl.when(pair == 0)
        def _():
            dq_acc[...] = jnp.zeros_like(dq_acc)
            dcq_ref[...] = jnp.zeros_like(dcq_ref)

        def step(diagonal):
            for g in range(grp):
                sl = slice(g * HEAD_DIM, (g + 1) * HEAD_DIM)
                qs = (q_ref[:, sl] * FOX_Q_SCALE).astype(BF16)
                kv = k_ref[:, sl]
                dov = do_ref[:, sl]
                st = _fox_scores(kv, qs, ckc_ref[g, :, 0:1] * (-LOG2E), cqr_ref[g] * LOG2E - lse_ref[g], diagonal, True)
                pt = jnp.exp2(st)
                dv_acc[g] += _dot(pt, dov, NN)
                dpt = _dot(v_ref[:, sl], dov, NT)
                dst = pt * (dpt - dl_ref[g])
                dk_acc[g] += _dot(dst, qs, NN)
                dck_acc[g] += jnp.sum(dst, axis=1, keepdims=True)
                dq_acc[g, rows_q, :] += _dot(dst, kv, TN)
                dcq_ref[g, iq] += jnp.sum(dst, axis=0, keepdims=True)

        @pl.when(iq == ik)
        def _():
            dk_acc[...] = jnp.zeros_like(dk_acc)
            dv_acc[...] = jnp.zeros_like(dv_acc)
            dck_acc[...] = jnp.zeros_like(dck_acc)
            step(True)

        @pl.when(iq > ik)
        def _():
            step(False)

        @pl.when(iq == nk - 1)
        def _():
            for g in range(grp):
                sl = slice(g * HEAD_DIM, (g + 1) * HEAD_DIM)
                dk_ref[:, sl] = (dk_acc[g] * (1.0 / LOG2E)).astype(BF16)
                dv_ref[:, sl] = dv_acc[g].astype(BF16)
                dck_ref[g] = _as_row(-dck_acc[g])

        @pl.when(pair == npairs - 1)
        def _():
            for g in range(grp):
                head = pl.program_id(0) * grp + g

                def flush(i, c):
                    rows = pl.ds(pl.multiple_of(i * t, t), t)
                    stage[...] = (dq_acc[g, rows, :] * scale).astype(BF16)
                    cp = pltpu.make_async_copy(stage, dq_hbm.at[head, rows, :], sem)
                    cp.start()
                    cp.wait()
                    return c

                lax.fori_loop(0, nk, flush, 0)

    ng = heads // grp
    qblk = pl.BlockSpec((t, grp * HEAD_DIM), lambda h, p, qi, ki: (qi[p], h))
    qrow = pl.BlockSpec((grp, 1, t), lambda h, p, qi, ki: (h, 0, qi[p]))
    kblk = lambda off: pl.BlockSpec((t, grp * HEAD_DIM), lambda h, p, qi, ki: (ki[p], off + h))
    kout = pl.BlockSpec((t, grp * HEAD_DIM), lambda h, p, qi, ki: (ki[p], h))
    return _hosted(
        body, name="fox_bwd", grid=(ng, npairs), prefetch=(qi, ki),
        in_specs=[qblk, kblk(ng), kblk(2 * ng), qblk, qrow,
                  pl.BlockSpec((grp, t, LANES), lambda h, p, qi, ki: (h, ki[p], 0)), qrow, qrow],
        out_specs=[kout, kout, pl.BlockSpec((grp, 1, t), lambda h, p, qi, ki: (h, 0, ki[p])), ANY,
                   pl.BlockSpec((grp, nk, 1, t), lambda h, p, qi, ki: (h, 0, 0, 0))],
        scratch_shapes=[pltpu.VMEM((grp, t, HEAD_DIM), F32), pltpu.VMEM((grp, t, HEAD_DIM), F32),
                        pltpu.VMEM((grp, t, 1), F32), pltpu.VMEM((grp, s, HEAD_DIM), F32),
                        pltpu.VMEM((t, HEAD_DIM), BF16), pltpu.SemaphoreType.DMA],
        out_shape=[jax.ShapeDtypeStruct((s, heads * HEAD_DIM), BF16)] * 2 + [jax.ShapeDtypeStruct((heads, 1, s), F32)]
        + [jax.ShapeDtypeStruct((heads, s, HEAD_DIM), BF16), jax.ShapeDtypeStruct((heads, nk, 1, t), F32)],
        args=(proj, proj, proj, do, cum_row, cum_col, lse_row, delta_row), carry=carry)


def _shift_rows(xv, halo, j, forward):
    n = xv.shape[0]
    rid = lax.broadcasted_iota(jnp.int32, (8, xv.shape[1]), 0)
    if forward:
        xs = pltpu.roll(xv, n - j, 0)
        hs = pltpu.roll(halo, 8 - j, 0)
        edge = jnp.where(rid >= 8 - j, hs, xs[n - 8:, :])
        return jnp.concatenate([xs[:n - 8, :], edge], axis=0)
    xs = pltpu.roll(xv, j, 0)
    hs = pltpu.roll(halo, j, 0)
    edge = jnp.where(rid < j, hs, xs[:8, :])
    return jnp.concatenate([edge, xs[8:, :]], axis=0)


def _conv_silu(xv, halo, w):
    xc = w[CONV_W - 1:CONV_W, :] * xv
    for j in range(1, CONV_W):
        xc = xc + w[CONV_W - 1 - j:CONV_W - j, :] * _shift_rows(xv, halo, j, False)
    return xc, xc * _sigmoid(xc)


def _gdn_pre(proj, conv_w, heads):
    s = proj.shape[0]
    cw = 3 * heads * HEAD_DIM
    ts = _tile(s, 256)
    tb = ts // 8

    def body(x_ref, halo_ref, w_ref, q_ref, k_ref, v_ref):
        halo = jnp.where(pl.program_id(0) == 0, 0.0, halo_ref[...])
        _, y = _conv_silu(x_ref[...], halo, w_ref[...])
        for h in range(heads):
            for part, ref in enumerate((q_ref, k_ref, v_ref)):
                c0 = (part * heads + h) * HEAD_DIM
                blk = y[:, c0:c0 + HEAD_DIM]
                if part < 2:
                    blk = blk * lax.rsqrt(jnp.sum(blk * blk, axis=1, keepdims=True) + EPS)
                ref[h] = blk

    out = pl.BlockSpec((heads, ts, HEAD_DIM), lambda i: (0, i, 0))
    return _pcall(
        body, name="gdn_pre", grid=(s // ts,),
        in_specs=[pl.BlockSpec((ts, cw), lambda i: (i, 1)),
                  pl.BlockSpec((8, cw), lambda i: (jnp.maximum(i * tb - 1, 0), 1)),
                  pl.BlockSpec((CONV_W, cw), lambda i: (0, 0))],
        out_specs=[out, out, out],
        out_shape=[jax.ShapeDtypeStruct((heads, s, HEAD_DIM), F32)] * 3,
        compiler_params=_params("parallel"),
    )(proj, proj, conv_w)


def _gdn_pre_bwd_act(proj, conv_w, dq, dk, dv, heads):
    s = proj.shape[0]
    cw = 3 * heads * HEAD_DIM
    ts = _tile(s, 256)
    tb = ts // 8

    def body(x_ref, halo_ref, w_ref, dq_ref, dk_ref, dv_ref, dxc_ref, dw_ref):
        @pl.when(pl.program_id(0) == 0)
        def _():
            dw_ref[...] = jnp.zeros_like(dw_ref)

        xv = x_ref[...]
        halo = jnp.where(pl.program_id(0) == 0, 0.0, halo_ref[...])
        xc, y = _conv_silu(xv, halo, w_ref[...])
        sg = _sigmoid(xc)
        dsilu = sg * (1.0 + xc * (1.0 - sg))
        for h in range(heads):
            for part, ref in enumerate((dq_ref, dk_ref, dv_ref)):
                c0 = (part * heads + h) * HEAD_DIM
                g = ref[h]
                if part < 2:
                    blk = y[:, c0:c0 + HEAD_DIM]
                    r = lax.rsqrt(jnp.sum(blk * blk, axis=1, keepdims=True) + EPS)
                    g = r * g - blk * (r * r * r) * jnp.sum(g * blk, axis=1, keepdims=True)
                dxc_ref[:, c0:c0 + HEAD_DIM] = g * dsilu[:, c0:c0 + HEAD_DIM]
        dxc = dxc_ref[...]
        rows = [jnp.sum(dxc * (xv if j == 0 else _shift_rows(xv, halo, j, False)), axis=0, keepdims=True)
                for j in range(CONV_W)]
        dw_ref[...] += jnp.concatenate([rows[CONV_W - 1 - k] for k in range(CONV_W)]
                                       + [jnp.zeros((8 - CONV_W, cw), F32)], axis=0)

    hblk = pl.BlockSpec((heads, ts, HEAD_DIM), lambda i: (0, i, 0))
    return _pcall(
        body, name="gdn_pre_bwd_act", grid=(s // ts,),
        in_specs=[pl.BlockSpec((ts, cw), lambda i: (i, 1)),
                  pl.BlockSpec((8, cw), lambda i: (jnp.maximum(i * tb - 1, 0), 1)),
                  pl.BlockSpec((CONV_W, cw), lambda i: (0, 0)), hblk, hblk, hblk],
        out_specs=[pl.BlockSpec((ts, cw), lambda i: (i, 0)), pl.BlockSpec((8, cw), lambda i: (0, 0))],
        out_shape=[jax.ShapeDtypeStruct((s, cw), F32), jax.ShapeDtypeStruct((8, cw), F32)],
        compiler_params=_params("arbitrary"),
    )(proj, proj, conv_w, dq, dk, dv)


def _gdn_pre_bwd_conv(dxc, conv_w):
    s, cw = dxc.shape
    ts = _tile(s, 256)
    tb = ts // 8
    last = s // 8 - 1

    def body(g_ref, halo_ref, w_ref, dx_ref):
        gv = g_ref[...]
        w = w_ref[...]
        halo = jnp.where(pl.program_id(0) == s // ts - 1, 0.0, halo_ref[...])
        dx = w[CONV_W - 1:CONV_W, :] * gv
        for j in range(1, CONV_W):
            dx = dx + w[CONV_W - 1 - j:CONV_W - j, :] * _shift_rows(gv, halo, j, True)
        dx_ref[...] = dx.astype(BF16)

    return _pcall(
        body, name="gdn_pre_bwd_conv", grid=(s // ts,),
        in_specs=[pl.BlockSpec((ts, cw), lambda i: (i, 0)),
                  pl.BlockSpec((8, cw), lambda i: (jnp.minimum((i + 1) * tb, last), 0)),
                  pl.BlockSpec((CONV_W, cw), lambda i: (0, 0))],
        out_specs=pl.BlockSpec((ts, cw), lambda i: (i, 0)),
        out_shape=jax.ShapeDtypeStruct((s, cw), BF16),
        compiler_params=_params("parallel"),
    )(dxc, dxc, conv_w)


def _bdot(a, b, ca, cb):
    return lax.dot_general(a.astype(BF16), b.astype(BF16), (((ca,), (cb,)), ((0,), (0,))),
                           preferred_element_type=F32)


def _bdot_hp(a, b, ca, cb):
    ah = a.astype(BF16)
    al = (a - ah.astype(F32)).astype(BF16)
    bh = b.astype(BF16)
    bl = (b - bh.astype(F32)).astype(BF16)
    d = lambda p, q: lax.dot_general(p, q, (((ca,), (cb,)), ((0,), (0,))), preferred_element_type=F32)
    return d(ah, bh) + (d(ah, bl) + d(al, bh))


def _gdn_gates(small, a_lane, dt_lane, heads):
    lane = lax.broadcasted_iota(jnp.int32, small.shape, 1)
    za = small + dt_lane
    g_all = -jnp.exp(a_lane) * (jnp.maximum(za, 0.0) + jnp.log(1.0 + jnp.exp(-jnp.abs(za))))
    b_all = _sigmoid(small)
    pick = lambda v, l: jnp.sum(jnp.where(lane == l, v, 0.0), axis=1, keepdims=True)
    g = jnp.stack([pick(g_all, heads + h) for h in range(heads)], axis=0)
    beta = jnp.stack([pick(b_all, 2 * heads + h) for h in range(heads)], axis=0)
    return g, beta


def _chunk_masks(c):
    ii = lax.broadcasted_iota(jnp.int32, (1, c, c), 1)
    jj = lax.broadcasted_iota(jnp.int32, (1, c, c), 2)
    return ii >= jj, ii > jj, ii == jj


def _col_to_row(col, eye):
    return jnp.sum(jnp.where(eye, col, 0.0), axis=1, keepdims=True)


def _row_to_col(row, eye):
    return jnp.sum(jnp.where(eye, row, 0.0), axis=2, keepdims=True)


def _gdn_chunk(q, k, v, g, beta, state, tinv=None):
    c = q.shape[1]
    incl, strict, eye = _chunk_masks(c)
    g_row = _col_to_row(g, eye)
    gc_col = jnp.sum(jnp.where(incl, g_row, 0.0), axis=2, keepdims=True)
    gc_row = _col_to_row(gc_col, eye)
    gam = jnp.where(incl, jnp.exp(jnp.where(incl, gc_col - gc_row, NEG)), 0.0)
    egc = jnp.exp(gc_col)
    kb = k * beta
    vb = v * beta
    kbe = kb * egc
    low = jnp.where(strict, _bdot(kb, k, 2, 2), 0.0) * gam
    if tinv is None:
        p = -low
        tinv = jnp.where(eye, 1.0, 0.0) + p
        width = 2
        while width < c:
            dot = _bdot_hp if width <= GDN_HP_WIDTH else _bdot
            p = dot(p, p, 2, 1)
            tinv = tinv + dot(tinv, p, 2, 1)
            width *= 2
    u = _bdot(tinv, vb, 2, 1)
    w = _bdot(tinv, kbe, 2, 1)
    att = jnp.where(incl, _bdot(q, k, 2, 2), 0.0) * gam
    vn = u - _bdot(w, state, 2, 1)
    qe = q * egc
    o = _bdot(qe, state, 2, 1) + _bdot(att, vn, 2, 1)
    gl = jnp.sum(g, axis=1, keepdims=True)
    edec = jnp.exp(gl - gc_col)
    kdec = k * edec
    egl = jnp.exp(gl)
    new_state = state * egl + _bdot(kdec, vn, 1, 1)
    return dict(incl=incl, strict=strict, eye=eye, gam=gam, egc=egc, kb=kb, vb=vb, kbe=kbe, low=low, tinv=tinv, w=w,
                att=att, vn=vn, qe=qe, o=o, edec=edec, kdec=kdec, egl=egl, new_state=new_state)


def _gdn_load(q_ref, k_ref, v_ref, small_ref, a_ref, dt_ref, rows, heads):
    q = q_ref[:, rows, :] * (HEAD_DIM ** -0.5)
    g, beta = _gdn_gates(small_ref[rows, :], a_ref[...], dt_ref[...], heads)
    return q, k_ref[:, rows, :], v_ref[:, rows, :], g, beta


def _gdn_fwd(q, k, v, proj, z_blk, small_blk, a_lane, dt_lane, w_norm):
    heads, s, _ = q.shape
    c = min(GDN_CHUNK, s)
    r = _tile(s, 512)
    npb = r // c
    gw = heads * HEAD_DIM

    def body(q_ref, k_ref, v_ref, z_ref, small_ref, a_ref, dt_ref, w_ref, o_ref, st_ref, ti_ref, state):
        @pl.when(pl.program_id(0) == 0)
        def _():
            state[...] = jnp.zeros_like(state)

        def chunk(cb, carry):
            rows = pl.ds(pl.multiple_of(cb * c, c), c)
            qv, kv, vv, g, beta = _gdn_load(q_ref, k_ref, v_ref, small_ref, a_ref, dt_ref, rows, heads)
            st = state[...]
            st_ref[:, cb] = st
            res = _gdn_chunk(qv, kv, vv, g, beta, st)
            ti_ref[:, cb] = res["tinv"]
            state[...] = res["new_state"]
            o = res["o"]
            rn = lax.rsqrt(jnp.mean(o * o, axis=2, keepdims=True) + EPS)
            zv = z_ref[rows, :]
            for h in range(heads):
                zh = zv[:, h * HEAD_DIM:(h + 1) * HEAD_DIM]
                o_ref[rows, h * HEAD_DIM:(h + 1) * HEAD_DIM] = (
                    o[h] * rn[h] * w_ref[...] * (zh * _sigmoid(zh))).astype(BF16)
            return carry

        lax.fori_loop(0, npb, chunk, 0)

    hblk = pl.BlockSpec((heads, r, HEAD_DIM), lambda i: (0, i, 0))
    row = pl.BlockSpec((1, LANES), lambda i: (0, 0))
    return _pcall(
        body, name="gdn_fwd", grid=(s // r,),
        in_specs=[hblk, hblk, hblk, pl.BlockSpec((r, gw), lambda i: (i, z_blk)),
                  pl.BlockSpec((r, LANES), lambda i: (i, small_blk)), row, row, row],
        out_specs=[pl.BlockSpec((r, gw), lambda i: (i, 0)),
                   pl.BlockSpec((heads, npb, HEAD_DIM, HEAD_DIM), lambda i: (0, i, 0, 0)),
                   pl.BlockSpec((heads, npb, c, c), lambda i: (0, i, 0, 0))],
        out_shape=[jax.ShapeDtypeStruct((s, gw), BF16),
                   jax.ShapeDtypeStruct((heads, s // c, HEAD_DIM, HEAD_DIM), F32),
                   jax.ShapeDtypeStruct((heads, s // c, c, c), F32)],
        scratch_shapes=[pltpu.VMEM((heads, HEAD_DIM, HEAD_DIM), F32)],
        compiler_params=_params("arbitrary"),
    )(q, k, v, proj, proj, a_lane, dt_lane, w_norm)


def _gdn_bwd(q, k, v, proj, z_blk, small_blk, a_lane, dt_lane, w_norm, states, tinvs, do_cat, do_blk):
    heads, s, _ = q.shape
    c = min(GDN_CHUNK, s)
    r = _tile(s, 512)
    npb = r // c
    nb = s // r
    gw = heads * HEAD_DIM

    def body(q_ref, k_ref, v_ref, z_ref, small_ref, a_ref, dt_ref, w_ref, st_ref, ti_ref, do_ref,
             dq_ref, dk_ref, dv_ref, dz_ref, dsm_ref, da_ref, ddt_ref, dw_ref, dstate):
        @pl.when(pl.program_id(0) == 0)
        def _():
            dstate[...] = jnp.zeros_like(dstate)
            da_ref[...] = jnp.zeros_like(da_ref)
            ddt_ref[...] = jnp.zeros_like(ddt_ref)
            dw_ref[...] = jnp.zeros_like(dw_ref)

        def chunk(it, carry):
            cb = npb - 1 - it
            rows = pl.ds(pl.multiple_of(cb * c, c), c)
            qv, kv, vv, g, beta = _gdn_load(q_ref, k_ref, v_ref, small_ref, a_ref, dt_ref, rows, heads)
            st = st_ref[:, cb]
            f = _gdn_chunk(qv, kv, vv, g, beta, st, tinv=ti_ref[:, cb])
            incl, strict, eye = f["incl"], f["strict"], f["eye"]
            o = f["o"]
            wv = w_ref[...]
            zv = z_ref[rows, :]
            dov = do_ref[rows, :]
            rn = lax.rsqrt(jnp.mean(o * o, axis=2, keepdims=True) + EPS)
            do_l, dw_acc = [], jnp.zeros((1, HEAD_DIM), F32)
            for h in range(heads):
                sl = slice(h * HEAD_DIM, (h + 1) * HEAD_DIM)
                zh, gh = zv[:, sl], dov[:, sl]
                sg = _sigmoid(zh)
                on = o[h] * rn[h]
                dz_ref[rows, sl] = (gh * (on * wv) * (sg * (1.0 + zh * (1.0 - sg)))).astype(BF16)
                gn = gh * (zh * sg)
                dw_acc = dw_acc + jnp.sum(gn * on, axis=0, keepdims=True)
                wg = gn * wv
                do_l.append(rn[h] * wg - o[h] * (rn[h] * rn[h] * rn[h]) * jnp.mean(wg * o[h], axis=1, keepdims=True))
            dw_ref[...] += dw_acc
            do = jnp.stack(do_l, axis=0)
            ds_out = dstate[...]
            dvn = _bdot(f["att"], do, 1, 1) + _bdot(f["kdec"], ds_out, 2, 1)
            datt = jnp.where(incl, _bdot(do, f["vn"], 2, 2), 0.0)
            dqe = _bdot(do, st, 2, 2)
            dstate[...] = _bdot(f["qe"], do, 1, 1) + f["egl"] * ds_out - _bdot(f["w"], dvn, 1, 1)
            dw = -_bdot(dvn, st, 2, 2)
            dkdec = _bdot(f["vn"], ds_out, 2, 2)
            t_kdec = jnp.sum(dkdec * f["kdec"], axis=2, keepdims=True)
            dgl = (jnp.sum(jnp.sum(st * ds_out, axis=2, keepdims=True), axis=1, keepdims=True) * f["egl"]
                   + jnp.sum(t_kdec, axis=1, keepdims=True))
            dgc = jnp.sum(dqe * f["qe"], axis=2, keepdims=True) - t_kdec
            dq = dqe * f["egc"]
            dk = dkdec * f["edec"]
            dtinv = _bdot(dvn, f["vb"], 2, 2) + _bdot(dw, f["kbe"], 2, 2)
            dvb = _bdot(f["tinv"], dvn, 1, 1)
            dkbe = _bdot(f["tinv"], dw, 1, 1)
            dkb = dkbe * f["egc"]
            dgc = dgc + jnp.sum(dkbe * f["kbe"], axis=2, keepdims=True)
            dlow = jnp.where(strict, -_bdot_hp(_bdot_hp(f["tinv"], dtinv, 1, 1), f["tinv"], 2, 2), 0.0)
            ml = dlow * f["gam"]
            dkb = dkb + _bdot(ml, kv, 2, 1)
            dk = dk + _bdot(ml, f["kb"], 1, 1)
            ma = datt * f["gam"]
            dq = dq + _bdot(ma, kv, 2, 1)
            dk = dk + _bdot(ma, qv, 1, 1)
            e = dlow * f["low"] + datt * f["att"]
            dgc = dgc + jnp.sum(e, axis=2, keepdims=True) - _row_to_col(jnp.sum(e, axis=1, keepdims=True), eye)
            dk = dk + beta * dkb
            dbeta = jnp.sum(dkb * kv, axis=2, keepdims=True) + jnp.sum(dvb * vv, axis=2, keepdims=True)
            dgc_row = _col_to_row(dgc, eye)
            dg = jnp.sum(jnp.where(incl, 0.0, dgc_row) + jnp.where(eye, dgc_row, 0.0), axis=2, keepdims=True) + dgl
            dq_ref[:, rows, :] = dq * (HEAD_DIM ** -0.5)
            dk_ref[:, rows, :] = dk
            dv_ref[:, rows, :] = beta * dvb
            small = small_ref[rows, :]
            lane = lax.broadcasted_iota(jnp.int32, small.shape, 1)
            dg_l = jnp.zeros(small.shape, F32)
            db_l = jnp.zeros(small.shape, F32)
            for h in range(heads):
                dg_l = dg_l + jnp.where(lane == heads + h, dg[h], 0.0)
                db_l = db_l + jnp.where(lane == 2 * heads + h, dbeta[h], 0.0)
            za = small + dt_ref[...]
            nexp = -jnp.exp(a_ref[...])
            softplus = jnp.maximum(za, 0.0) + jnp.log(1.0 + jnp.exp(-jnp.abs(za)))
            da_logit = dg_l * nexp * _sigmoid(za)
            sb = _sigmoid(small)
            dsm_ref[rows, :] = da_logit + db_l * sb * (1.0 - sb)
            ddt_ref[...] += jnp.sum(da_logit, axis=0, keepdims=True)
            da_ref[...] += jnp.sum(dg_l * nexp * softplus, axis=0, keepdims=True)
            return carry

        lax.fori_loop(0, npb, chunk, 0)

    rev = lambda i: nb - 1 - i
    hblk = pl.BlockSpec((heads, r, HEAD_DIM), lambda i: (0, rev(i), 0))
    row = pl.BlockSpec((1, LANES), lambda i: (0, 0))
    wide = lambda blk: pl.BlockSpec((r, gw), lambda i: (rev(i), blk))
    return _pcall(
        body, name="gdn_bwd", grid=(nb,),
        in_specs=[hblk, hblk, hblk, wide(z_blk), pl.BlockSpec((r, LANES), lambda i: (rev(i), small_blk)),
                  row, row, row, pl.BlockSpec((heads, npb, HEAD_DIM, HEAD_DIM), lambda i: (0, rev(i), 0, 0)),
                  pl.BlockSpec((heads, npb, c, c), lambda i: (0, rev(i), 0, 0)), wide(do_blk)],
        out_specs=[hblk, hblk, hblk, wide(0), pl.BlockSpec((r, LANES), lambda i: (rev(i), 0)), row, row, row],
        out_shape=[jax.ShapeDtypeStruct((heads, s, HEAD_DIM), F32)] * 3
        + [jax.ShapeDtypeStruct((s, gw), BF16), jax.ShapeDtypeStruct((s, LANES), F32)]
        + [jax.ShapeDtypeStruct((1, LANES), F32)] * 3,
        scratch_shapes=[pltpu.VMEM((heads, HEAD_DIM, HEAD_DIM), F32)],
        compiler_params=_params("arbitrary"),
    )(q, k, v, proj, proj, a_lane, dt_lane, w_norm, states, tinvs, do_cat)


def _final(x, target, gf):
    s, d = x.shape
    ts = _tile(s, 512)

    def body(x_ref, t_ref, g_ref, loss_ref, dx_ref, dg_ref):
        @pl.when(pl.program_id(0) == 0)
        def _():
            loss_ref[...] = jnp.zeros_like(loss_ref)
            dg_ref[...] = jnp.zeros_like(dg_ref)

        xv = x_ref[...]
        gv = g_ref[...]
        r = lax.rsqrt(jnp.mean(xv * xv, axis=-1, keepdims=True) + EPS)
        xn = xv * r
        err = xn * gv - t_ref[...]
        per_tok = jnp.mean(err * err, axis=-1, keepdims=True)
        loss_ref[...] += 0.5 * jnp.sum(per_tok, axis=0, keepdims=True)
        dy = err * (1.0 / d)
        dg_ref[...] += jnp.sum(dy * xn, axis=0, keepdims=True)
        dxn = dy * gv
        dx_ref[...] = r * (dxn - xn * jnp.mean(dxn * xn, axis=-1, keepdims=True))

    blk = pl.BlockSpec((ts, d), lambda i: (i, 0))
    row = pl.BlockSpec((1, d), lambda i: (0, 0))
    return _pcall(
        body, name="final_loss", grid=(s // ts,),
        in_specs=[blk, blk, row], out_specs=[pl.BlockSpec((1, LANES), lambda i: (0, 0)), blk, row],
        out_shape=[jax.ShapeDtypeStruct((1, LANES), F32), jax.ShapeDtypeStruct((s, d), F32),
                   jax.ShapeDtypeStruct((1, d), F32)],
        compiler_params=_params("arbitrary"),
    )(x, target, gf)


def _adamw(parts, w, m, v, name):
    npart, rows, cols = parts.shape
    tr = _tile(rows, max(8, ADAM_BLOCK_BYTES // (4 * npart * cols)))
    c1 = 1.0 - ADAM_B1 ** ADAM_STEP
    c2 = 1.0 - ADAM_B2 ** ADAM_STEP

    def body(p_ref, w_ref, m_ref, v_ref, g_ref, d_ref, mo_ref, vo_ref):
        g = p_ref[0].astype(F32)
        for i in range(1, npart):
            g = g + p_ref[i].astype(F32)
        mn = ADAM_B1 * m_ref[...] + (1.0 - ADAM_B1) * g
        vn = ADAM_B2 * v_ref[...] + (1.0 - ADAM_B2) * (g * g)
        g_ref[...] = g
        mo_ref[...] = mn
        vo_ref[...] = vn
        d_ref[...] = -ADAM_LR * ((mn / c1) / (jnp.sqrt(vn / c2) + ADAM_EPS) + ADAM_WD * w_ref[...])

    blk = pl.BlockSpec((tr, cols), lambda i: (i, 0))
    return _pcall(
        body, name=name, grid=(rows // tr,),
        in_specs=[pl.BlockSpec((npart, tr, cols), lambda i: (0, i, 0)), blk, blk, blk],
        out_specs=[blk] * 4, out_shape=[jax.ShapeDtypeStruct((rows, cols), F32)] * 4,
        compiler_params=_params("parallel"),
    )(parts, w, m, v)


def _adamw_layers(parts0, parts1, w, m, v, name):
    npart, rows, cols = parts0.shape
    tr = _tile(rows, max(8, ADAM_BLOCK_BYTES // (4 * npart * cols)))
    nb = rows // tr
    c1 = 1.0 - ADAM_B1 ** ADAM_STEP
    c2 = 1.0 - ADAM_B2 ** ADAM_STEP

    def body(p0_ref, p1_ref, w_ref, m_ref, v_ref, g_ref, d_ref, mo_ref, vo_ref):
        def update(p_ref):
            g = p_ref[0].astype(F32)
            for i in range(1, npart):
                g = g + p_ref[i].astype(F32)
            mn = ADAM_B1 * m_ref[...] + (1.0 - ADAM_B1) * g
            vn = ADAM_B2 * v_ref[...] + (1.0 - ADAM_B2) * (g * g)
            g_ref[...] = g
            mo_ref[...] = mn
            vo_ref[...] = vn
            d_ref[...] = -ADAM_LR * ((mn / c1) / (jnp.sqrt(vn / c2) + ADAM_EPS) + ADAM_WD * w_ref[...])

        @pl.when(pl.program_id(0) == 0)
        def _():
            update(p0_ref)

        @pl.when(pl.program_id(0) == 1)
        def _():
            update(p1_ref)

    blk = pl.BlockSpec((None, None, tr, cols), lambda l, i: (0, l, i, 0))
    p0 = pl.BlockSpec((npart, tr, cols), lambda l, i: (0, jnp.where(l == 0, i, nb - 1), 0))
    p1 = pl.BlockSpec((npart, tr, cols), lambda l, i: (0, jnp.where(l == 0, 0, i), 0))
    return _pcall(
        body, name=name, grid=(2, nb), in_specs=[p0, p1, blk, blk, blk], out_specs=[blk] * 4,
        out_shape=[jax.ShapeDtypeStruct(w.shape, F32)] * 4, compiler_params=_params("arbitrary", "arbitrary"),
    )(parts0, parts1, w, m, v)


def _pad_lanes(v, n=LANES, at=0):
    return jnp.pad(v, ((0, 0), (at, n - at - v.shape[1])))


def _my_cols(a, me, width):
    return lax.dynamic_slice_in_dim(a, me * width, width, axis=a.ndim - 1)


def kernel(x, c, ada_w, ada_b, norm_g, ffn_w_gate, ffn_w_up, ffn_w_down, w_in, w_out, fox_f_bias, fox_out_norm, gdn_conv, gdn_A_log, gdn_dt_bias, gdn_out_norm, final_norm, loss_target, m_ada_w, m_ada_b, m_norm_g, m_ffn_w_gate, m_ffn_w_up, m_ffn_w_down, m_w_in, m_w_out, m_fox_f_bias, m_fox_out_norm, m_gdn_conv, m_gdn_A_log, m_gdn_dt_bias, m_gdn_out_norm, m_final_norm, v_ada_w, v_ada_b, v_norm_g, v_ffn_w_gate, v_ffn_w_up, v_ffn_w_down, v_w_in, v_w_out, v_fox_f_bias, v_fox_out_norm, v_gdn_conv, v_gdn_A_log, v_gdn_dt_bias, v_gdn_out_norm, v_final_norm):
    me = _linear(_mesh_pos())
    x0 = x[0]
    s, d = x0.shape
    heads = d // (2 * HEAD_DIM)
    fw = heads * HEAD_DIM
    ng = norm_g.shape[-1]
    ncv = gdn_conv.shape[-1]
    nada = ada_w.shape[-1]
    in_w = w_in.shape[-1] * N_DEV
    in_pad = -(-in_w // 512) * 512

    pack = jnp.concatenate([c, norm_g[0].reshape(1, 3 * ng), gdn_conv[0].reshape(1, CONV_W * ncv)], axis=1)
    pack_all = _gather_row(pack, "gather_small_params")
    c_all = pack_all[:, :d]
    g_all = pack_all[:, d:d + 3 * ng].reshape(N_DEV, 3, ng).transpose(1, 0, 2).reshape(3, d)
    conv_all = pack_all[:, d + 3 * ng:].reshape(N_DEV, CONV_W, ncv).transpose(1, 0, 2).reshape(CONV_W, 3 * fw)

    mod_blk = _ada_fwd(c_all, ada_w[0], _my_cols(ada_b, me, nada))
    mod_all = _exchange([mod_blk], scatter=False, in_vmem=True, name="gather_mod")[0]
    mod = lax.dynamic_slice_in_dim(mod_all, me, 1, axis=1).reshape(N_MOD, d)
    sh1, sc1, gt1, sh2, sc2, gt2, sh3, sc3, gt3 = [mod[i:i + 1] for i in range(N_MOD)]

    wg_sh, wu_sh, wd_sh = [w[0].astype(BF16) for w in (ffn_w_gate, ffn_w_up, ffn_w_down)]
    layer = lambda w, i: w[i:i + 1]
    wg0, wu0 = _gather_via_sibling([layer(wg_sh, 0), layer(wu_sh, 0)], "gather_ffn1_up_weights")
    small_blk = 7 * heads

    bias_lane = _pad_lanes(fox_f_bias)
    a_lane = _pad_lanes(gdn_A_log, at=heads)
    dt_lane = _pad_lanes(gdn_dt_bias, at=heads)

    h1 = _norm_mod(x0, g_all[0:1], sc1, sh1, "norm_mod_1")
    (a1, b1, s1), (wd0, wout_g) = _ffn_up(h1, wg0, wu0, 0, "ffn1_up",
                                          carry=([layer(wd_sh, 0), w_out[0].astype(BF16)], False))
    (f1, x1), (win_g,) = _ffn_down(s1, wd0, 0, x0, gt1, "ffn1_down", carry=([w_in[0].astype(BF16)], False))
    win_full = win_g.transpose(1, 0, 2).reshape(d, in_w)
    o_f, o_qkv, o_a, o_z = 3 * fw, 3 * fw + heads, 6 * fw + heads, 6 * fw + 3 * heads
    win_al = jnp.concatenate(
        [win_full[:, :o_f], win_full[:, o_qkv:o_a], win_full[:, o_z:], win_full[:, o_f:o_qkv],
         win_full[:, o_a:o_z], jnp.zeros((d, in_pad - in_w), BF16)], axis=1)
    wout_full = wout_g.reshape(d, d)

    h2 = _norm_mod(x1, g_all[1:2], sc2, sh2, "norm_mod_2")
    proj = _mm(h2, win_al, name="in_proj", tn=1536)
    cum = _fox_gate(proj, small_blk, bias_lane)
    cum_t = cum[:, :heads].T
    cum_row = cum_t[:, None, :]
    cum_col = jnp.broadcast_to(cum_t[:, :, None], (heads, s, LANES))
    (o_raw, lse, lse_row, o_fox), (wg1, wu1, wd1) = _fox_fwd(
        proj, cum_col, cum_row, fox_out_norm, heads,
        carry=([layer(wg_sh, 1), layer(wu_sh, 1), layer(wd_sh, 1)], False))
    qg, kg, vg = _gdn_pre(proj, conv_all, heads)
    o_gdn, states, tinvs = _gdn_fwd(qg, kg, vg, proj, 6, small_blk, a_lane, dt_lane, gdn_out_norm)
    o_cat = jnp.concatenate([o_fox, o_gdn], axis=1)
    mix, x2 = _mm(o_cat, wout_full, name="out_proj", residual=(x1, gt2))

    h3 = _norm_mod(x2, g_all[2:3], sc3, sh3, "norm_mod_3")
    (a3, b3, s3), _ = _ffn_up(h3, wg1, wu1, 0, "ffn2_up")
    (f3, x3), _ = _ffn_down(s3, wd1, 0, x2, gt3, "ffn2_down")

    loss_row, dx3, d_final = _final(x3, loss_target[0], final_norm.reshape(1, d))
    loss = lax.psum(loss_row[0, 0], MESH_AXES)

    df3, dgt3 = _gate_bwd(dx3, f3, gt3, MACARON_W, "ffn2_gate_bwd")
    (da3, db3), _ = _ffn_bwd_act(df3, wd1, 0, a3, b3, "ffn2_bwd_act")
    (dwd2,), _ = _ffn_bwd_wd(s3, df3, "ffn2_bwd_wd")
    (dh3,), (r_wd2,) = _ffn_bwd_h(da3, db3, wg1, wu1, 0, "ffn2_bwd_h", carry=([dwd2], True))
    (dwg2, dwu2), _ = _ffn_bwd_wgu(h3, da3, db3, "ffn2_bwd_wgu")
    dx2, dsh3, dsc3, dg3 = _norm_mod_bwd(x2, dh3, dx3, g_all[2:3], sc3, "norm_mod_3_bwd")

    dmix, dgt2 = _gate_bwd(dx2, mix, gt2, 1.0, "mix_gate_bwd")
    do_cat = _mm(dmix, wout_full, tb=True, name="out_proj_bwd_x")
    dwout = _mm(o_cat, dmix, ta=True, out_dtype=BF16, name="out_proj_bwd_w", tk=512)
    do_fox, delta, delta_row, d_foxw = _fox_prep_bwd(do_cat, o_raw, fox_out_norm, heads)
    (dk_f, dv_f, dcum_k, dq_heads, dcum_q), (r_wg2, r_wu2, r_wout) = _fox_bwd(
        proj, do_fox, cum_col, cum_row, lse_row, delta_row, heads,
        carry=([dwg2, dwu2, dwout.reshape(N_DEV, d // N_DEV, d)], True))
    dq_f = dq_heads.transpose(1, 0, 2).reshape(s, fw)
    head_lanes = lambda t: jnp.pad(t.reshape(heads, s).T, ((0, 0), (0, LANES - heads)))
    dsm_fox, d_fbias = _fox_gate_bwd(head_lanes(dcum_q), head_lanes(dcum_k), proj, small_blk, bias_lane)
    dqg, dkg, dvg, dz, dsm_gdn, d_alog, d_dt, d_gdnw = _gdn_bwd(
        qg, kg, vg, proj, 6, small_blk, a_lane, dt_lane, gdn_out_norm, states, tinvs, do_cat, 1)
    dxc, d_conv = _gdn_pre_bwd_act(proj, conv_all, dqg, dkg, dvg, heads)
    dqkv = _gdn_pre_bwd_conv(dxc, conv_all)
    dsmall = (dsm_fox + dsm_gdn).astype(BF16)
    dproj = jnp.concatenate([dq_f, dk_f, dv_f, dqkv, dz, dsmall, jnp.zeros((s, in_pad - 7 * fw - LANES), BF16)], axis=1)
    dh2 = _mm(dproj, win_al, tb=True, name="in_proj_bwd_x", tk=1536)
    dwin_al = _mm(h2, dproj, ta=True, out_dtype=BF16, name="in_proj_bwd_w", tm=2048, tn=1536, tk=1024)
    dwin_full = jnp.concatenate(
        [dwin_al[:, :o_f], dwin_al[:, 7 * fw:7 * fw + heads], dwin_al[:, o_f:o_f + 3 * fw],
         dwin_al[:, 7 * fw + heads:7 * fw + 3 * heads], dwin_al[:, 6 * fw:7 * fw]], axis=1)
    dwin_parts = dwin_full.reshape(d, N_DEV, in_w // N_DEV).transpose(1, 0, 2)
    dx1, dsh2, dsc2, dg2 = _norm_mod_bwd(x1, dh2, dx2, g_all[1:2], sc2, "norm_mod_2_bwd")

    df1, dgt1 = _gate_bwd(dx1, f1, gt1, MACARON_W, "ffn1_gate_bwd")
    (da1, db1), (r_win,) = _ffn_bwd_act(df1, wd0, 0, a1, b1, "ffn1_bwd_act", carry=([dwin_parts], True))
    (dwd1,), _ = _ffn_bwd_wd(s1, df1, "ffn1_bwd_wd")
    (dwg1, dwu1), (r_wd1,) = _ffn_bwd_wgu(h1, da1, db1, "ffn1_bwd_wgu", carry=([dwd1], True))
    (dh1,), (r_wg1, r_wu1) = _ffn_bwd_h(da1, db1, wg0, wu0, 0, "ffn1_bwd_h", carry=([dwg1, dwu1], True))
    grad_x, dsh1, dsc1, dg1 = _norm_mod_bwd(x0, dh1, dx1, g_all[0:1], sc1, "norm_mod_1_bwd")

    dmod = jnp.concatenate([dsh1, dsc1, dgt1, dsh2, dsc2, dgt2, dsh3, dsc3, dgt3], axis=1)
    dmod_all = _gather_row(dmod, "gather_dmod")
    ct_pad = jnp.pad(c_all.T, ((0, 0), (0, LANES - N_DEV)))
    dmod_mine = jnp.pad(_my_cols(dmod_all, me, nada), ((0, LANES - N_DEV), (0, 0)))
    g_ada_w = _ada_bwd(ct_pad, dmod_mine)

    g_small_cols = [d_fbias, d_foxw, d_alog[:, heads:], d_dt[:, heads:], d_gdnw]
    small_part = jnp.concatenate(
        [_pad_lanes(v[:, :LANES]) for v in g_small_cols]
        + [d_final, dg1, dg2, dg3] + [d_conv[k:k + 1] for k in range(CONV_W)], axis=1)
    small_all = _gather_row(small_part, "gather_small_grads")
    off = 5 * LANES
    w_small = jnp.concatenate(
        [_pad_lanes(fox_f_bias), fox_out_norm, _pad_lanes(gdn_A_log), _pad_lanes(gdn_dt_bias), gdn_out_norm,
         final_norm.reshape(1, d)], axis=1)
    m_small = jnp.concatenate(
        [_pad_lanes(m_fox_f_bias), m_fox_out_norm, _pad_lanes(m_gdn_A_log), _pad_lanes(m_gdn_dt_bias),
         m_gdn_out_norm, m_final_norm.reshape(1, d)], axis=1)
    v_small = jnp.concatenate(
        [_pad_lanes(v_fox_f_bias), v_fox_out_norm, _pad_lanes(v_gdn_A_log), _pad_lanes(v_gdn_dt_bias),
         v_gdn_out_norm, v_final_norm.reshape(1, d)], axis=1)
    rep = _adamw(small_all[:, None, :off + d], w_small, m_small, v_small, "adamw_replicated")
    ab = _adamw(dmod_all[:, None, :], ada_b, m_ada_b, v_ada_b, "adamw_ada_b")
    g_ng = small_all[:, off + d:off + 4 * d].reshape(N_DEV, 3, d)
    ngs = _adamw(_my_cols(g_ng, me, ng), norm_g[0], m_norm_g[0], v_norm_g[0], "adamw_norm_g")
    g_cv = small_all[:, off + 4 * d:].reshape(N_DEV, CONV_W, 3 * fw)
    cvs = _adamw(_my_cols(g_cv, me, ncv), gdn_conv[0], m_gdn_conv[0], v_gdn_conv[0], "adamw_gdn_conv")

    wgs = _adamw_layers(r_wg1, r_wg2, ffn_w_gate, m_ffn_w_gate, v_ffn_w_gate, "adamw_w_gate")
    wus = _adamw_layers(r_wu1, r_wu2, ffn_w_up, m_ffn_w_up, v_ffn_w_up, "adamw_w_up")
    wds = _adamw_layers(r_wd1, r_wd2, ffn_w_down, m_ffn_w_down, v_ffn_w_down, "adamw_w_down")
    wis = [o[None] for o in _adamw(r_win, w_in[0], m_w_in[0], v_w_in[0], "adamw_w_in")]
    wos = [o[None] for o in _adamw(r_wout, w_out[0], m_w_out[0], v_w_out[0], "adamw_w_out")]
    adas = [o[None] for o in _adamw(g_ada_w[None], ada_w[0], m_ada_w[0], v_ada_w[0], "adamw_ada_w")]
    ngs = [o[None] for o in ngs]
    cvs = [o[None] for o in cvs]

    def rep_piece(i, lo, width):
        return rep[i][:, lo:lo + width]

    nh = fox_f_bias.shape[1]
    outs = []
    for i in range(4):
        outs.append([adas[i], ab[i], ngs[i], wgs[i], wus[i], wds[i], wis[i], wos[i],
                     rep_piece(i, 0, nh), rep_piece(i, LANES, HEAD_DIM), cvs[i], rep_piece(i, 2 * LANES, nh),
                     rep_piece(i, 3 * LANES, nh), rep_piece(i, 4 * LANES, HEAD_DIM), rep_piece(i, off, d).reshape(d)])
    return (loss, grad_x[None], *outs[0], *outs[1], *outs[2], *outs[3])
```

```python
import math

import numpy as np
import jax
import jax.numpy as jnp
from jax import lax
from jax.experimental import pallas as pl
from jax.experimental.pallas import tpu as pltpu

F32 = jnp.float32
BF16 = jnp.bfloat16

N_DEV = 8
MESH_AXES = ("x", "y", "c")
LANES = 128
HEAD_DIM = 128
GDN_CHUNK = 64
CONV_W = 4
N_MOD = 9
MACARON_W = 0.5
EPS = 1e-6
NEG = -1e30
VMEM_LIMIT_BYTES = 56 * 2 ** 20
ADAM_BLOCK_BYTES = 4 * 2 ** 20
GDN_HP_WIDTH = 4
FOX_BWD_HEADS_PER_STEP = 4
FOX_HEADS_PER_STEP = 8

ADAM_LR = 0.001
ADAM_B1 = 0.9
ADAM_B2 = 0.999
ADAM_EPS = 1e-08
ADAM_WD = 0.01
ADAM_STEP = 10

MESH_ID = pl.DeviceIdType.MESH
ANY = pl.BlockSpec(memory_space=pl.ANY)
VMEM = pl.BlockSpec(memory_space=pltpu.VMEM)


def _pcall(body, **kw):
    return pl.pallas_call(body, **kw)


def _params(*semantics):
    return pltpu.CompilerParams(dimension_semantics=semantics, vmem_limit_bytes=VMEM_LIMIT_BYTES)


def _tile(n, pref):
    if n % pref == 0 and pref % 8 == 0:
        return pref
    t = 1 << (max(1, min(n, pref)).bit_length() - 1)
    while n % t:
        t //= 2
    return t if t % 8 == 0 else n


def _sigmoid(x):
    return 1.0 / (1.0 + jnp.exp(-x))


def _dot(a, b, dims):
    return lax.dot_general(a.astype(BF16), b.astype(BF16), (dims, ((), ())), preferred_element_type=F32)


NN = ((1,), (0,))
NT = ((1,), (1,))
TN = ((0,), (0,))


def _split3(x):
    hi = x.astype(BF16)
    r1 = x - hi.astype(F32)
    mid = r1.astype(BF16)
    lo = (r1 - mid.astype(F32)).astype(BF16)
    return hi, mid, lo


def _dot_exact_lhs(m_bf16, x, dims=NN):
    hi, mid, lo = _split3(x)
    d = lambda p: lax.dot_general(m_bf16, p, (dims, ((), ())), preferred_element_type=F32)
    return d(hi) + (d(mid) + d(lo))


def _dot_hp(a, b, dims):
    ah = a.astype(BF16)
    al = (a - ah.astype(F32)).astype(BF16)
    bh = b.astype(BF16)
    bl = (b - bh.astype(F32)).astype(BF16)
    d = lambda p, q: lax.dot_general(p, q, (dims, ((), ())), preferred_element_type=F32)
    return d(ah, bh) + (d(ah, bl) + d(al, bh))


def _mesh_pos():
    return lax.axis_index("x"), lax.axis_index("y"), lax.axis_index("c")


def _peer(pos, mask):
    x, y, c = pos
    return (1 - x if mask & 4 else x, 1 - y if mask & 2 else y, 1 - c if mask & 1 else c)


def _linear(pos):
    return 4 * pos[0] + 2 * pos[1] + pos[2]


def _exchange_copies(ins, outs, sems, scatter, with_receives=True):
    send_sems, recv_sems, local_sems = sems
    pos = _mesh_pos()
    me = _linear(pos)
    local, sends, recvs = [], [], []
    for i in range(len(ins)):
        src = ins[i].at[me] if scatter else ins[i]
        local.append(pltpu.make_async_copy(src, outs[i].at[me], local_sems.at[i]))
    for mask in range(1, N_DEV):
        peer = _peer(pos, mask)
        for i in range(len(ins)):
            sem = dict(send_sem=send_sems.at[i, mask - 1], recv_sem=recv_sems.at[i, mask - 1],
                       device_id=peer, device_id_type=MESH_ID)
            sends.append(pltpu.make_async_remote_copy(
                src_ref=ins[i].at[_linear(peer)] if scatter else ins[i], dst_ref=outs[i].at[me], **sem))
            if with_receives:
                recvs.append(pltpu.make_async_remote_copy(
                    src_ref=ins[i].at[me] if scatter else ins[i], dst_ref=outs[i].at[_linear(peer)], **sem))
    return local, sends, recvs


def _exchange_start(ins, outs, sems, scatter):
    local, sends, _ = _exchange_copies(ins, outs, sems, scatter, with_receives=False)
    for cp in local + sends:
        cp.start()


def _exchange_wait(ins, outs, sems, scatter):
    local, sends, recvs = _exchange_copies(ins, outs, sems, scatter)
    for cp in recvs:
        cp.wait_recv()
    for cp in sends:
        cp.wait_send()
    for cp in local:
        cp.wait()


def _exchange_sems(n):
    return [pltpu.SemaphoreType.DMA((n, N_DEV - 1)), pltpu.SemaphoreType.DMA((n, N_DEV - 1)),
            pltpu.SemaphoreType.DMA((n,))]


def _exchange_shapes(arrays, scatter):
    return [jax.ShapeDtypeStruct(a.shape if scatter else (N_DEV,) + a.shape, a.dtype) for a in arrays]


def _exchange(arrays, *, scatter, in_vmem, name):
    n = len(arrays)

    def body(*refs):
        ins, outs, sems = refs[:n], refs[n:2 * n], refs[2 * n:]
        _exchange_start(ins, outs, sems, scatter)
        _exchange_wait(ins, outs, sems, scatter)

    spec = VMEM if in_vmem else ANY
    outs = _pcall(
        body, name=name, out_shape=_exchange_shapes(arrays, scatter),
        in_specs=[spec] * n, out_specs=[spec] * n, scratch_shapes=_exchange_sems(n),
    )(*arrays)
    return list(outs)


def _gather_via_sibling(arrays, name):
    n = len(arrays)

    def body(*refs):
        ins, outs = refs[:n], refs[n:2 * n]
        send_sems, recv_sems, local_sems = refs[2 * n:]
        x, y, c = _mesh_pos()
        me, sibling = (x, y, c), (x, y, 1 - c)
        chips = [(1 - x, y), (x, 1 - y), (1 - x, 1 - y)]

        def copy(i, k, block, to, from_input=False):
            return pltpu.make_async_remote_copy(
                src_ref=ins[i] if from_input else outs[i].at[_linear(block)], dst_ref=outs[i].at[_linear(block)],
                send_sem=send_sems.at[i, k], recv_sem=recv_sems.at[i, k], device_id=to, device_id_type=MESH_ID)

        mine = [pltpu.make_async_copy(ins[i], outs[i].at[_linear(me)], local_sems.at[i]) for i in range(n)]
        first = []
        for i in range(n):
            first.append(copy(i, 0, me, sibling, True))
            first += [copy(i, 1 + j, me, (*chip, c), True) for j, chip in enumerate(chips)]
        for cp in mine + first:
            cp.start()
        passed = []
        for j, chip in enumerate(chips):
            for i in range(n):
                copy(i, 1 + j, (*chip, c), me).wait_recv()
                cp = copy(i, 4 + j, (*chip, c), sibling)
                cp.start()
                passed.append(cp)
        for i in range(n):
            copy(i, 0, sibling, me).wait_recv()
            for j, chip in enumerate(chips):
                copy(i, 4 + j, (*chip, 1 - c), me).wait_recv()
        for cp in first + passed:
            cp.wait_send()
        for cp in mine:
            cp.wait()

    outs = _pcall(
        body, name=name, out_shape=_exchange_shapes(arrays, False), in_specs=[ANY] * n, out_specs=[ANY] * n,
        scratch_shapes=_exchange_sems(n),
    )(*arrays)
    return list(outs)


def _hosted(body, *, name, grid, in_specs, out_specs, out_shape, args, scratch_shapes=(), prefetch=(), carry=None):
    n_in, n_out, n_scr, n_pre = len(in_specs), len(out_shape), len(scratch_shapes), len(prefetch)
    arrays, scatter = carry if carry is not None else ([], False)
    n = len(arrays)

    def wrapped(*refs):
        pre, r = refs[:n_pre], refs[n_pre:]
        host_in, comm_in = r[:n_in], r[n_in:n_in + n]
        r = r[n_in + n:]
        host_out, comm_out = r[:n_out], r[n_out:n_out + n]
        r = r[n_out + n:]
        host_scr, sems = r[:n_scr], r[n_scr:]
        if n:
            first = pl.program_id(0) == 0
            last = pl.program_id(0) == grid[0] - 1
            for ax in range(1, len(grid)):
                first = first & (pl.program_id(ax) == 0)
                last = last & (pl.program_id(ax) == grid[ax] - 1)

            @pl.when(first)
            def _():
                _exchange_start(comm_in, comm_out, sems, scatter)

        body(*pre, *host_in, *host_out, *host_scr)
        if n:
            @pl.when(last)
            def _():
                _exchange_wait(comm_in, comm_out, sems, scatter)

    grid_spec = pltpu.PrefetchScalarGridSpec(
        num_scalar_prefetch=n_pre, grid=grid, in_specs=list(in_specs) + [ANY] * n,
        out_specs=list(out_specs) + [ANY] * n,
        scratch_shapes=list(scratch_shapes) + (_exchange_sems(n) if n else []))
    outs = _pcall(
        wrapped, name=name, grid_spec=grid_spec, out_shape=list(out_shape) + _exchange_shapes(arrays, scatter),
        compiler_params=_params(*(["arbitrary"] * len(grid))),
    )(*prefetch, *args, *arrays)
    return list(outs[:n_out]), list(outs[n_out:])


def _gather_row(v, name):
    return _exchange([v], scatter=False, in_vmem=True, name=name)[0].reshape(N_DEV, v.shape[1])


def _ada_fwd(c_all, w, b):
    d, n = w.shape
    tn = _tile(n, 256)

    def body(c_ref, w_ref, b_ref, o_ref):
        cv = c_ref[...]
        cond = cv * _sigmoid(cv)
        o_ref[...] = _dot_hp(cond, w_ref[...], NN) + b_ref[...]

    return _pcall(
        body, name="ada_fwd", grid=(n // tn,),
        in_specs=[pl.BlockSpec((N_DEV, d), lambda j: (0, 0)), pl.BlockSpec((d, tn), lambda j: (0, j)),
                  pl.BlockSpec((1, tn), lambda j: (0, j))],
        out_specs=pl.BlockSpec((N_DEV, tn), lambda j: (0, j)),
        out_shape=jax.ShapeDtypeStruct((N_DEV, n), F32), compiler_params=_params("parallel"),
    )(c_all, w, b)


def _ada_bwd(ct_pad, dmod_pad):
    d = ct_pad.shape[0]
    n = dmod_pad.shape[1]
    tn = _tile(n, 256)

    def body(c_ref, g_ref, o_ref):
        cv = c_ref[...]
        cond = cv * _sigmoid(cv)
        o_ref[...] = _dot_hp(cond, g_ref[...], NN)

    return _pcall(
        body, name="ada_bwd", grid=(n // tn,),
        in_specs=[pl.BlockSpec((d, LANES), lambda j: (0, 0)), pl.BlockSpec((LANES, tn), lambda j: (0, j))],
        out_specs=pl.BlockSpec((d, tn), lambda j: (0, j)),
        out_shape=jax.ShapeDtypeStruct((d, n), F32), compiler_params=_params("parallel"),
    )(ct_pad, dmod_pad)


def _norm_mod(x, g, sc, sh, name):
    s, d = x.shape
    ts = _tile(s, 512)

    def body(x_ref, g_ref, sc_ref, sh_ref, h_ref):
        xv = x_ref[...]
        r = lax.rsqrt(jnp.mean(xv * xv, axis=-1, keepdims=True) + EPS)
        h_ref[...] = (xv * r * g_ref[...] * (1.0 + sc_ref[...]) + sh_ref[...]).astype(BF16)

    row = pl.BlockSpec((1, d), lambda i: (0, 0))
    return _pcall(
        body, name=name, grid=(s // ts,),
        in_specs=[pl.BlockSpec((ts, d), lambda i: (i, 0)), row, row, row],
        out_specs=pl.BlockSpec((ts, d), lambda i: (i, 0)),
        out_shape=jax.ShapeDtypeStruct((s, d), BF16), compiler_params=_params("parallel"),
    )(x, g, sc, sh)


def _norm_mod_bwd(x, dh, dx_out, g, sc, name):
    s, d = x.shape
    ts = _tile(s, 512)

    def body(x_ref, dh_ref, dxo_ref, g_ref, sc_ref, dx_ref, dsh_ref, dsc_ref, dg_ref):
        @pl.when(pl.program_id(0) == 0)
        def _():
            dsh_ref[...] = jnp.zeros_like(dsh_ref)
            dsc_ref[...] = jnp.zeros_like(dsc_ref)
            dg_ref[...] = jnp.zeros_like(dg_ref)

        xv = x_ref[...]
        dh_v = dh_ref[...]
        gv = g_ref[...]
        one_sc = 1.0 + sc_ref[...]
        r = lax.rsqrt(jnp.mean(xv * xv, axis=-1, keepdims=True) + EPS)
        xn = xv * r
        dxn = dh_v * (gv * one_sc)
        dx_ref[...] = dxo_ref[...] + r * (dxn - xn * jnp.mean(dxn * xn, axis=-1, keepdims=True))
        t = dh_v * xn
        dsh_ref[...] += jnp.sum(dh_v, axis=0, keepdims=True)
        dsc_ref[...] += jnp.sum(t * gv, axis=0, keepdims=True)
        dg_ref[...] += jnp.sum(t * one_sc, axis=0, keepdims=True)

    blk = pl.BlockSpec((ts, d), lambda i: (i, 0))
    row = pl.BlockSpec((1, d), lambda i: (0, 0))
    return _pcall(
        body, name=name, grid=(s // ts,),
        in_specs=[blk, blk, blk, row, row], out_specs=[blk, row, row, row],
        out_shape=[jax.ShapeDtypeStruct((s, d), F32)] + [jax.ShapeDtypeStruct((1, d), F32)] * 3,
        compiler_params=_params("arbitrary"),
    )(x, dh, dx_out, g, sc)


def _gate_bwd(dx, f, gt, k, name):
    s, d = dx.shape
    ts = _tile(s, 512)

    def body(dx_ref, f_ref, gt_ref, df_ref, dgt_ref):
        @pl.when(pl.program_id(0) == 0)
        def _():
            dgt_ref[...] = jnp.zeros_like(dgt_ref)

        dxv = dx_ref[...]
        df_ref[...] = ((k * gt_ref[...]) * dxv).astype(BF16)
        dgt_ref[...] += k * jnp.sum(f_ref[...] * dxv, axis=0, keepdims=True)

    blk = pl.BlockSpec((ts, d), lambda i: (i, 0))
    row = pl.BlockSpec((1, d), lambda i: (0, 0))
    return _pcall(
        body, name=name, grid=(s // ts,),
        in_specs=[blk, blk, row], out_specs=[blk, row],
        out_shape=[jax.ShapeDtypeStruct((s, d), BF16), jax.ShapeDtypeStruct((1, d), F32)],
        compiler_params=_params("arbitrary"),
    )(dx, f, gt)


def _ffn_up(h, wg, wu, layer, name, carry=None):
    s, d = h.shape
    fs = wg.shape[-1]
    tm = _tile(s, 1024)

    def body(h_ref, wg_ref, wu_ref, a_ref, b_ref, s_ref):
        hv = h_ref[...]
        a = jnp.dot(hv, wg_ref[...], preferred_element_type=F32)
        b = jnp.dot(hv, wu_ref[...], preferred_element_type=F32)
        a_ref[...] = a.astype(BF16)
        b_ref[...] = b.astype(BF16)
        s_ref[...] = (a * _sigmoid(a) * b).astype(BF16)

    wspec = pl.BlockSpec((None, None, d, fs), lambda j, m: (j, layer, 0, 0))
    ospec = pl.BlockSpec((None, tm, fs), lambda j, m: (j, m, 0))
    return _hosted(
        body, name=name, grid=(N_DEV, s // tm),
        in_specs=[pl.BlockSpec((tm, d), lambda j, m: (m, 0)), wspec, wspec],
        out_specs=[ospec, ospec, ospec],
        out_shape=[jax.ShapeDtypeStruct((N_DEV, s, fs), BF16)] * 3,
        args=(h, wg, wu), carry=carry)


def _ffn_down(sv, wd, layer, x_in, gt, name, carry=None):
    _, s, fs = sv.shape
    d = wd.shape[-1]
    tm = _tile(s, 512)

    def body(s_ref, wd_ref, x_ref, gt_ref, f_ref, xo_ref, acc):
        j = pl.program_id(1)

        @pl.when(j == 0)
        def _():
            acc[...] = jnp.zeros_like(acc)

        acc[...] += jnp.dot(s_ref[...], wd_ref[...], preferred_element_type=F32)

        @pl.when(j == N_DEV - 1)
        def _():
            fv = acc[...]
            f_ref[...] = fv
            xo_ref[...] = x_ref[...] + (MACARON_W * gt_ref[...]) * fv

    blk = pl.BlockSpec((tm, d), lambda m, j: (m, 0))
    return _hosted(
        body, name=name, grid=(s // tm, N_DEV),
        in_specs=[pl.BlockSpec((None, tm, fs), lambda m, j: (j, m, 0)),
                  pl.BlockSpec((None, None, fs, d), lambda m, j: (j, layer, 0, 0)),
                  blk, pl.BlockSpec((1, d), lambda m, j: (0, 0))],
        out_specs=[blk, blk],
        out_shape=[jax.ShapeDtypeStruct((s, d), F32)] * 2,
        scratch_shapes=[pltpu.VMEM((tm, d), F32)],
        args=(sv, wd, x_in, gt), carry=carry)


def _ffn_bwd_act(df, wd, layer, a, b, name, carry=None):
    s, d = df.shape
    fs = a.shape[-1]
    tm = _tile(s, 1024)

    def body(df_ref, wd_ref, a_ref, b_ref, da_ref, db_ref):
        ds = lax.dot_general(df_ref[...], wd_ref[...], (NT, ((), ())), preferred_element_type=F32)
        av = a_ref[...].astype(F32)
        sg = _sigmoid(av)
        da_ref[...] = (ds * b_ref[...].astype(F32) * (sg * (1.0 + av * (1.0 - sg)))).astype(BF16)
        db_ref[...] = (ds * (av * sg)).astype(BF16)

    hid = pl.BlockSpec((None, tm, fs), lambda j, m: (j, m, 0))
    return _hosted(
        body, name=name, grid=(N_DEV, s // tm),
        in_specs=[pl.BlockSpec((tm, d), lambda j, m: (m, 0)),
                  pl.BlockSpec((None, None, fs, d), lambda j, m: (j, layer, 0, 0)), hid, hid],
        out_specs=[hid, hid],
        out_shape=[jax.ShapeDtypeStruct((N_DEV, s, fs), BF16)] * 2,
        args=(df, wd, a, b), carry=carry)


def _ffn_bwd_wd(sv, df, name, carry=None):
    _, s, fs = sv.shape
    d = df.shape[1]
    tk = _tile(s, 1024)
    nk = s // tk

    def body(s_ref, df_ref, o_ref, acc):
        @pl.when(pl.program_id(1) == 0)
        def _():
            acc[...] = jnp.zeros_like(acc)

        acc[...] += lax.dot_general(s_ref[...], df_ref[...], (TN, ((), ())), preferred_element_type=F32)

        @pl.when(pl.program_id(1) == nk - 1)
        def _():
            o_ref[...] = acc[...].astype(BF16)

    return _hosted(
        body, name=name, grid=(N_DEV, nk),
        in_specs=[pl.BlockSpec((None, tk, fs), lambda j, k: (j, k, 0)), pl.BlockSpec((tk, d), lambda j, k: (k, 0))],
        out_specs=[pl.BlockSpec((None, fs, d), lambda j, k: (j, 0, 0))],
        out_shape=[jax.ShapeDtypeStruct((N_DEV, fs, d), BF16)],
        scratch_shapes=[pltpu.VMEM((fs, d), F32)],
        args=(sv, df), carry=carry)


def _ffn_bwd_h(da, db, wg, wu, layer, name, carry=None):
    _, s, fs = da.shape
    d = wg.shape[-2]
    tm = _tile(s, 1024)

    def body(da_ref, db_ref, wg_ref, wu_ref, o_ref, acc):
        j = pl.program_id(1)

        @pl.when(j == 0)
        def _():
            acc[...] = jnp.zeros_like(acc)

        acc[...] += (lax.dot_general(da_ref[...], wg_ref[...], (NT, ((), ())), preferred_element_type=F32)
                     + lax.dot_general(db_ref[...], wu_ref[...], (NT, ((), ())), preferred_element_type=F32))

        @pl.when(j == N_DEV - 1)
        def _():
            o_ref[...] = acc[...]

    hid = pl.BlockSpec((None, tm, fs), lambda m, j: (j, m, 0))
    wspec = pl.BlockSpec((None, None, d, fs), lambda m, j: (j, layer, 0, 0))
    return _hosted(
        body, name=name, grid=(s // tm, N_DEV),
        in_specs=[hid, hid, wspec, wspec],
        out_specs=[pl.BlockSpec((tm, d), lambda m, j: (m, 0))],
        out_shape=[jax.ShapeDtypeStruct((s, d), F32)],
        scratch_shapes=[pltpu.VMEM((tm, d), F32)],
        args=(da, db, wg, wu), carry=carry)


def _ffn_bwd_wgu(h, da, db, name, carry=None):
    s, d = h.shape
    fs = da.shape[-1]
    tk = _tile(s, 1024)
    nk = s // tk

    def body(h_ref, da_ref, db_ref, og_ref, ou_ref, accg, accu):
        @pl.when(pl.program_id(1) == 0)
        def _():
            accg[...] = jnp.zeros_like(accg)
            accu[...] = jnp.zeros_like(accu)

        hv = h_ref[...]
        accg[...] += lax.dot_general(hv, da_ref[...], (TN, ((), ())), preferred_element_type=F32)
        accu[...] += lax.dot_general(hv, db_ref[...], (TN, ((), ())), preferred_element_type=F32)

        @pl.when(pl.program_id(1) == nk - 1)
        def _():
            og_ref[...] = accg[...].astype(BF16)
            ou_ref[...] = accu[...].astype(BF16)

    hid = pl.BlockSpec((None, tk, fs), lambda j, k: (j, k, 0))
    ospec = pl.BlockSpec((None, d, fs), lambda j, k: (j, 0, 0))
    return _hosted(
        body, name=name, grid=(N_DEV, nk),
        in_specs=[pl.BlockSpec((tk, d), lambda j, k: (k, 0)), hid, hid],
        out_specs=[ospec, ospec],
        out_shape=[jax.ShapeDtypeStruct((N_DEV, d, fs), BF16)] * 2,
        scratch_shapes=[pltpu.VMEM((d, fs), F32), pltpu.VMEM((d, fs), F32)],
        args=(h, da, db), carry=carry)


def _mm(a, b, *, ta=False, tb=False, out_dtype=F32, name, tm=1024, tn=1024, tk=2048, residual=None):
    m, kdim = (a.shape[1], a.shape[0]) if ta else a.shape
    n = b.shape[0] if tb else b.shape[1]
    tm, tn, tk = _tile(m, tm), _tile(n, tn), _tile(kdim, tk)
    nk = kdim // tk
    dims = ((0,) if ta else (1,), (1,) if tb else (0,))

    def body(*refs):
        a_ref, b_ref = refs[:2]
        acc = refs[-1]
        kk = pl.program_id(2)

        @pl.when(kk == 0)
        def _():
            acc[...] = jnp.zeros_like(acc)

        acc[...] += lax.dot_general(a_ref[...].astype(BF16), b_ref[...].astype(BF16), (dims, ((), ())),
                                    preferred_element_type=F32)

        @pl.when(kk == nk - 1)
        def _():
            if residual is None:
                refs[2][...] = acc[...].astype(out_dtype)
            else:
                res_ref, gate_ref, y_ref, xo_ref = refs[2:6]
                yv = acc[...]
                y_ref[...] = yv
                xo_ref[...] = res_ref[...] + gate_ref[...] * yv

    a_spec = pl.BlockSpec((tk, tm), lambda i, j, k: (k, i)) if ta else pl.BlockSpec((tm, tk), lambda i, j, k: (i, k))
    b_spec = pl.BlockSpec((tn, tk), lambda i, j, k: (j, k)) if tb else pl.BlockSpec((tk, tn), lambda i, j, k: (k, j))
    o_spec = pl.BlockSpec((tm, tn), lambda i, j, k: (i, j))
    if residual is None:
        in_specs, out_specs = [a_spec, b_spec], o_spec
        out_shape = jax.ShapeDtypeStruct((m, n), out_dtype)
        args = (a, b)
    else:
        in_specs = [a_spec, b_spec, o_spec, pl.BlockSpec((1, tn), lambda i, j, k: (0, j))]
        out_specs = [o_spec, o_spec]
        out_shape = [jax.ShapeDtypeStruct((m, n), F32)] * 2
        args = (a, b) + tuple(residual)
    return _pcall(
        body, name=name, grid=(m // tm, n // tn, nk), in_specs=in_specs, out_specs=out_specs, out_shape=out_shape,
        scratch_shapes=[pltpu.VMEM((tm, tn), F32)],
        compiler_params=_params("parallel", "parallel", "arbitrary"),
    )(*args)


def _log_sigmoid(z):
    return jnp.minimum(z, 0.0) - jnp.log(1.0 + jnp.exp(-jnp.abs(z)))


def _fox_gate(proj, small_blk, bias_lane):
    s = proj.shape[0]
    ts = _tile(s, 1024)
    nsub = ts // LANES

    def body(z_ref, b_ref, cum_ref, carry):
        @pl.when(pl.program_id(0) == 0)
        def _():
            carry[...] = jnp.zeros_like(carry)

        ii = lax.broadcasted_iota(jnp.int32, (LANES, LANES), 0)
        jj = lax.broadcasted_iota(jnp.int32, (LANES, LANES), 1)
        tri = (ii >= jj).astype(BF16)
        logf = _log_sigmoid(z_ref[...] + b_ref[...])
        cv = carry[...]
        for sb in range(nsub):
            blk = logf[sb * LANES:(sb + 1) * LANES, :]
            cum_ref[sb * LANES:(sb + 1) * LANES, :] = _dot_exact_lhs(tri, blk) + cv
            cv = cv + jnp.sum(blk, axis=0, keepdims=True)
        carry[...] = cv

    return _pcall(
        body, name="fox_gate", grid=(s // ts,),
        in_specs=[pl.BlockSpec((ts, LANES), lambda i: (i, small_blk)), pl.BlockSpec((1, LANES), lambda i: (0, 0))],
        out_specs=pl.BlockSpec((ts, LANES), lambda i: (i, 0)),
        out_shape=jax.ShapeDtypeStruct((s, LANES), F32),
        scratch_shapes=[pltpu.VMEM((1, LANES), F32)],
        compiler_params=_params("arbitrary"),
    )(proj, bias_lane)


def _fox_gate_bwd(dcum_q, dcum_k, proj, small_blk, bias_lane):
    s = proj.shape[0]
    ts = _tile(s, 1024)
    nsub = ts // LANES
    nb = s // ts

    def body(dcq_ref, dc_ref, z_ref, b_ref, dz_ref, db_ref, carry):
        @pl.when(pl.program_id(0) == 0)
        def _():
            carry[...] = jnp.zeros_like(carry)
            db_ref[...] = jnp.zeros_like(db_ref)

        ii = lax.broadcasted_iota(jnp.int32, (LANES, LANES), 0)
        jj = lax.broadcasted_iota(jnp.int32, (LANES, LANES), 1)
        triu = (jj >= ii).astype(BF16)
        dc = dcq_ref[...] + dc_ref[...]
        zb = z_ref[...] + b_ref[...]
        cv = carry[...]
        dbv = jnp.zeros((1, LANES), F32)
        for sb in reversed(range(nsub)):
            rows = slice(sb * LANES, (sb + 1) * LANES)
            blk = dc[rows, :]
            dlogf = _dot_exact_lhs(triu, blk) + cv
            cv = cv + jnp.sum(blk, axis=0, keepdims=True)
            dz = dlogf * _sigmoid(-zb[rows, :])
            dz_ref[rows, :] = dz
            dbv = dbv + jnp.sum(dz, axis=0, keepdims=True)
        carry[...] = cv
        db_ref[...] += dbv

    row = pl.BlockSpec((1, LANES), lambda i: (0, 0))
    return _pcall(
        body, name="fox_gate_bwd", grid=(nb,),
        in_specs=[pl.BlockSpec((ts, LANES), lambda i: (nb - 1 - i, 0)),
                  pl.BlockSpec((ts, LANES), lambda i: (nb - 1 - i, 0)),
                  pl.BlockSpec((ts, LANES), lambda i: (nb - 1 - i, small_blk)), row],
        out_specs=[pl.BlockSpec((ts, LANES), lambda i: (nb - 1 - i, 0)), row],
        out_shape=[jax.ShapeDtypeStruct((s, LANES), F32), jax.ShapeDtypeStruct((1, LANES), F32)],
        scratch_shapes=[pltpu.VMEM((1, LANES), F32)],
        compiler_params=_params("arbitrary"),
    )(dcum_q, dcum_k, proj, bias_lane)


def _tri_tables(n, by_key):
    if by_key:
        pairs = [(i, j) for j in range(n) for i in range(j, n)]
    else:
        pairs = [(i, j) for i in range(n) for j in range(i + 1)]
    return (jnp.asarray(np.array([p[0] for p in pairs], np.int32)),
            jnp.asarray(np.array([p[1] for p in pairs], np.int32)))


def _fox_group(heads):
    return FOX_HEADS_PER_STEP if heads % FOX_HEADS_PER_STEP == 0 else 1


def _as_row(col):
    t = col.shape[0]
    eye = lax.broadcasted_iota(jnp.int32, (t, t), 0) == lax.broadcasted_iota(jnp.int32, (t, t), 1)
    return jnp.sum(jnp.where(eye, col, 0.0), axis=0, keepdims=True)


LOG2E = 1.4426950408889634
FOX_Q_SCALE = LOG2E / math.sqrt(HEAD_DIM)


def _fox_scores(a, b, bias_col, bias_row, diagonal, rows_are_keys=False):
    sc = lax.dot_general(a.astype(BF16), b.astype(BF16), (NT, ((), ())), preferred_element_type=F32)
    sc = sc + (bias_col + bias_row)
    if not diagonal:
        return sc
    row = lax.broadcasted_iota(jnp.int32, sc.shape, 0)
    col = lax.broadcasted_iota(jnp.int32, sc.shape, 1)
    return jnp.where(row <= col if rows_are_keys else col <= row, sc, NEG)


def _fox_fwd(proj, cum_col, cum_row, w_norm, heads, carry=None):
    s = proj.shape[0]
    t = _tile(s, 512)
    grp = _fox_group(heads)
    qi, ki = _tri_tables(s // t, False)
    scale = 1.0 / math.sqrt(HEAD_DIM)

    def body(qi_ref, ki_ref, q_ref, k_ref, v_ref, cq_ref, ck_ref, w_ref, o_ref, lse_ref, lser_ref, on_ref, m_s, acc_s):
        iq, ik = qi_ref[pl.program_id(1)], ki_ref[pl.program_id(1)]

        @pl.when(ik == 0)
        def _():
            m_s[...] = jnp.full_like(m_s, NEG)
            acc_s[...] = jnp.zeros_like(acc_s)

        def step(diagonal):
            for g in range(grp):
                sl = slice(g * HEAD_DIM, (g + 1) * HEAD_DIM)
                qs = (q_ref[:, sl] * FOX_Q_SCALE).astype(BF16)
                sc = _fox_scores(qs, k_ref[:, sl], cq_ref[g, :, 0:1] * LOG2E, ck_ref[g] * (-LOG2E), diagonal)
                m_prev = m_s[g]
                m_new = jnp.maximum(m_prev, jnp.max(sc, axis=1, keepdims=True))
                p = jnp.exp2(sc - m_new).astype(BF16)
                v_ones = jnp.concatenate([v_ref[:, sl].astype(BF16), jnp.ones((t, LANES), BF16)], axis=1)
                acc_s[g] = jnp.exp2(m_prev - m_new) * acc_s[g] + jnp.dot(p, v_ones, preferred_element_type=F32)
                m_s[g] = m_new

        @pl.when(ik < iq)
        def _():
            step(False)

        @pl.when(ik == iq)
        def _():
            step(True)
            for g in range(grp):
                sl = slice(g * HEAD_DIM, (g + 1) * HEAD_DIM)
                acc = acc_s[g]
                o = acc[:, :HEAD_DIM] / acc[:, HEAD_DIM:]
                lse = m_s[g] + jnp.log(acc[:, HEAD_DIM:]) * LOG2E
                o_ref[:, sl] = o
                lse_ref[g] = lse
                lser_ref[g] = _as_row(lse[:, 0:1])
                r = lax.rsqrt(jnp.mean(o * o, axis=1, keepdims=True) + EPS)
                on_ref[:, sl] = (o * r * w_ref[...]).astype(BF16)

    ng = heads // grp
    qblk = pl.BlockSpec((t, grp * HEAD_DIM), lambda h, p, qi, ki: (qi[p], h))
    kblk = lambda off: pl.BlockSpec((t, grp * HEAD_DIM), lambda h, p, qi, ki: (ki[p], off + h))
    qcol = pl.BlockSpec((grp, t, LANES), lambda h, p, qi, ki: (h, qi[p], 0))
    return _hosted(
        body, name="fox_fwd", grid=(ng, int(qi.shape[0])), prefetch=(qi, ki),
        in_specs=[qblk, kblk(ng), kblk(2 * ng), qcol,
                  pl.BlockSpec((grp, 1, t), lambda h, p, qi, ki: (h, 0, ki[p])),
                  pl.BlockSpec((1, HEAD_DIM), lambda h, p, qi, ki: (0, 0))],
        out_specs=[qblk, qcol, pl.BlockSpec((grp, 1, t), lambda h, p, qi, ki: (h, 0, qi[p])), qblk],
        scratch_shapes=[pltpu.VMEM((grp, t, 1), F32), pltpu.VMEM((grp, t, 2 * HEAD_DIM), F32)],
        out_shape=[jax.ShapeDtypeStruct((s, heads * HEAD_DIM), F32), jax.ShapeDtypeStruct((heads, s, LANES), F32),
                   jax.ShapeDtypeStruct((heads, 1, s), F32), jax.ShapeDtypeStruct((s, heads * HEAD_DIM), BF16)],
        args=(proj, proj, proj, cum_col, cum_row, w_norm), carry=carry)


def _fox_prep_bwd(do_cat, o_raw, w_norm, heads):
    s = o_raw.shape[0]
    ts = _tile(s, 512)

    def body(g_ref, o_ref, w_ref, do_ref, delta_ref, deltar_ref, dw_ref):
        @pl.when((pl.program_id(0) == 0) & (pl.program_id(1) == 0))
        def _():
            dw_ref[...] = jnp.zeros_like(dw_ref)

        o = o_ref[...]
        g = g_ref[...]
        r = lax.rsqrt(jnp.mean(o * o, axis=1, keepdims=True) + EPS)
        wg = g * w_ref[...]
        do = r * wg - o * (r * r * r) * jnp.mean(wg * o, axis=1, keepdims=True)
        do_ref[...] = do.astype(BF16)
        delta = jnp.sum(do * o, axis=1, keepdims=True)
        delta_ref[...] = jnp.broadcast_to(delta, delta_ref.shape)
        deltar_ref[...] = _as_row(delta)
        dw_ref[...] += jnp.sum(g * o * r, axis=0, keepdims=True)

    blk = pl.BlockSpec((ts, HEAD_DIM), lambda h, i: (i, h))
    row = pl.BlockSpec((1, HEAD_DIM), lambda h, i: (0, 0))
    return _pcall(
        body, name="fox_prep_bwd", grid=(heads, s // ts),
        in_specs=[blk, blk, row],
        out_specs=[blk, pl.BlockSpec((None, ts, LANES), lambda h, i: (h, i, 0)),
                   pl.BlockSpec((None, 1, ts), lambda h, i: (h, 0, i)), row],
        out_shape=[jax.ShapeDtypeStruct((s, heads * HEAD_DIM), BF16), jax.ShapeDtypeStruct((heads, s, LANES), F32),
                   jax.ShapeDtypeStruct((heads, 1, s), F32), jax.ShapeDtypeStruct((1, HEAD_DIM), F32)],
        compiler_params=_params("arbitrary", "arbitrary"),
    )(do_cat, o_raw, w_norm)


def _fox_bwd(proj, do, cum_col, cum_row, lse_row, delta_row, heads, carry=None):
    s = proj.shape[0]
    t = _tile(s, 512)
    nk = s // t
    grp = FOX_BWD_HEADS_PER_STEP if heads % FOX_BWD_HEADS_PER_STEP == 0 else 1
    qi, ki = _tri_tables(nk, True)
    npairs = int(qi.shape[0])
    scale = 1.0 / math.sqrt(HEAD_DIM)

    def body(qi_ref, ki_ref, q_ref, k_ref, v_ref, do_ref, cqr_ref, ckc_ref, lse_ref, dl_ref,
             dk_ref, dv_ref, dck_ref, dq_hbm, dcq_ref, dk_acc, dv_acc, dck_acc, dq_acc, stage, sem):
        pair = pl.program_id(1)
        iq, ik = qi_ref[pair], ki_ref[pair]
        rows_q = pl.ds(pl.multiple_of(iq * t, t), t)

        @pl.when(pair == 0)
        def _():
            dq_acc[...] = jnp.zeros_like(dq_acc)
            dcq_ref[...] = jnp.zeros_like(dcq_ref)

        def step(diagonal):
            for g in range(grp):
                sl = slice(g * HEAD_DIM, (g + 1) * HEAD_DIM)
                qs = (q_ref[:, sl] * FOX_Q_SCALE).astype(BF16)
                kv = k_ref[:, sl]
                dov = do_ref[:, sl]
                st = _fox_scores(kv, qs, ckc_ref[g, :, 0:1] * (-LOG2E), cqr_ref[g] * LOG2E - lse_ref[g], diagonal, True)
                pt = jnp.exp2(st)
                dv_acc[g] += _dot(pt, dov, NN)
                dpt = _dot(v_ref[:, sl], dov, NT)
                dst = pt * (dpt - dl_ref[g])
                dk_acc[g] += _dot(dst, qs, NN)
                dck_acc[g] += jnp.sum(dst, axis=1, keepdims=True)
                dq_acc[g, rows_q, :] += _dot(dst, kv, TN)
                dcq_ref[g, iq] += jnp.sum(dst, axis=0, keepdims=True)

        @pl.when(iq == ik)
        def _():
            dk_acc[...] = jnp.zeros_like(dk_acc)
            dv_acc[...] = jnp.zeros_like(dv_acc)
            dck_acc[...] = jnp.zeros_like(dck_acc)
            step(True)

        @pl.when(iq > ik)
        def _():
            step(False)

        @pl.when(iq == nk - 1)
        def _():
            for g in range(grp):
                sl = slice(g * HEAD_DIM, (g + 1) * HEAD_DIM)
                dk_ref[:, sl] = (dk_acc[g] * (1.0 / LOG2E)).astype(BF16)
                dv_ref[:, sl] = dv_acc[g].astype(BF16)
                dck_ref[g] = _as_row(-dck_acc[g])

        @pl.when(pair == npairs - 1)
        def _():
            for g in range(grp):
                head = pl.program_id(0) * grp + g

                def flush(i, c):
                    rows = pl.ds(pl.multiple_of(i * t, t), t)
                    stage[...] = (dq_acc[g, rows, :] * scale).astype(BF16)
                    cp = pltpu.make_async_copy(stage, dq_hbm.at[head, rows, :], sem)
                    cp.start()
                    cp.wait()
                    return c

                lax.fori_loop(0, nk, flush, 0)

    ng = heads // grp
    qblk = pl.BlockSpec((t, grp * HEAD_DIM), lambda h, p, qi, ki: (qi[p], h))
    qrow = pl.BlockSpec((grp, 1, t), lambda h, p, qi, ki: (h, 0, qi[p]))
    kblk = lambda off: pl.BlockSpec((t, grp * HEAD_DIM), lambda h, p, qi, ki: (ki[p], off + h))
    kout = pl.BlockSpec((t, grp * HEAD_DIM), lambda h, p, qi, ki: (ki[p], h))
    return _hosted(
        body, name="fox_bwd", grid=(ng, npairs), prefetch=(qi, ki),
        in_specs=[qblk, kblk(ng), kblk(2 * ng), qblk, qrow,
                  pl.BlockSpec((grp, t, LANES), lambda h, p, qi, ki: (h, ki[p], 0)), qrow, qrow],
        out_specs=[kout, kout, pl.BlockSpec((grp, 1, t), lambda h, p, qi, ki: (h, 0, ki[p])), ANY,
                   pl.BlockSpec((grp, nk, 1, t), lambda h, p, qi, ki: (h, 0, 0, 0))],
        scratch_shapes=[pltpu.VMEM((grp, t, HEAD_DIM), F32), pltpu.VMEM((grp, t, HEAD_DIM), F32),
                        pltpu.VMEM((grp, t, 1), F32), pltpu.VMEM((grp, s, HEAD_DIM), F32),
                        pltpu.VMEM((t, HEAD_DIM), BF16), pltpu.SemaphoreType.DMA],
        out_shape=[jax.ShapeDtypeStruct((s, heads * HEAD_DIM), BF16)] * 2 + [jax.ShapeDtypeStruct((heads, 1, s), F32)]
        + [jax.ShapeDtypeStruct((heads, s, HEAD_DIM), BF16), jax.ShapeDtypeStruct((heads, nk, 1, t), F32)],
        args=(proj, proj, proj, do, cum_row, cum_col, lse_row, delta_row), carry=carry)


def _shift_rows(xv, halo, j, forward):
    n = xv.shape[0]
    rid = lax.broadcasted_iota(jnp.int32, (8, xv.shape[1]), 0)
    if forward:
        xs = pltpu.roll(xv, n - j, 0)
        hs = pltpu.roll(halo, 8 - j, 0)
        edge = jnp.where(rid >= 8 - j, hs, xs[n - 8:, :])
        return jnp.concatenate([xs[:n - 8, :], edge], axis=0)
    xs = pltpu.roll(xv, j, 0)
    hs = pltpu.roll(halo, j, 0)
    edge = jnp.where(rid < j, hs, xs[:8, :])
    return jnp.concatenate([edge, xs[8:, :]], axis=0)


def _conv_silu(xv, halo, w):
    xc = w[CONV_W - 1:CONV_W, :] * xv
    for j in range(1, CONV_W):
        xc = xc + w[CONV_W - 1 - j:CONV_W - j, :] * _shift_rows(xv, halo, j, False)
    return xc, xc * _sigmoid(xc)


def _gdn_pre(proj, conv_w, heads):
    s = proj.shape[0]
    cw = 3 * heads * HEAD_DIM
    ts = _tile(s, 256)
    tb = ts // 8

    def body(x_ref, halo_ref, w_ref, q_ref, k_ref, v_ref):
        halo = jnp.where(pl.program_id(0) == 0, 0.0, halo_ref[...])
        _, y = _conv_silu(x_ref[...], halo, w_ref[...])
        for h in range(heads):
            for part, ref in enumerate((q_ref, k_ref, v_ref)):
                c0 = (part * heads + h) * HEAD_DIM
                blk = y[:, c0:c0 + HEAD_DIM]
                if part < 2:
                    blk = blk * lax.rsqrt(jnp.sum(blk * blk, axis=1, keepdims=True) + EPS)
                ref[h] = blk

    out = pl.BlockSpec((heads, ts, HEAD_DIM), lambda i: (0, i, 0))
    return _pcall(
        body, name="gdn_pre", grid=(s // ts,),
        in_specs=[pl.BlockSpec((ts, cw), lambda i: (i, 1)),
                  pl.BlockSpec((8, cw), lambda i: (jnp.maximum(i * tb - 1, 0), 1)),
                  pl.BlockSpec((CONV_W, cw), lambda i: (0, 0))],
        out_specs=[out, out, out],
        out_shape=[jax.ShapeDtypeStruct((heads, s, HEAD_DIM), F32)] * 3,
        compiler_params=_params("parallel"),
    )(proj, proj, conv_w)


def _gdn_pre_bwd_act(proj, conv_w, dq, dk, dv, heads):
    s = proj.shape[0]
    cw = 3 * heads * HEAD_DIM
    ts = _tile(s, 256)
    tb = ts // 8

    def body(x_ref, halo_ref, w_ref, dq_ref, dk_ref, dv_ref, dxc_ref, dw_ref):
        @pl.when(pl.program_id(0) == 0)
        def _():
            dw_ref[...] = jnp.zeros_like(dw_ref)

        xv = x_ref[...]
        halo = jnp.where(pl.program_id(0) == 0, 0.0, halo_ref[...])
        xc, y = _conv_silu(xv, halo, w_ref[...])
        sg = _sigmoid(xc)
        dsilu = sg * (1.0 + xc * (1.0 - sg))
        for h in range(heads):
            for part, ref in enumerate((dq_ref, dk_ref, dv_ref)):
                c0 = (part * heads + h) * HEAD_DIM
                g = ref[h]
                if part < 2:
                    blk = y[:, c0:c0 + HEAD_DIM]
                    r = lax.rsqrt(jnp.sum(blk * blk, axis=1, keepdims=True) + EPS)
                    g = r * g - blk * (r * r * r) * jnp.sum(g * blk, axis=1, keepdims=True)
                dxc_ref[:, c0:c0 + HEAD_DIM] = g * dsilu[:, c0:c0 + HEAD_DIM]
        dxc = dxc_ref[...]
        rows = [jnp.sum(dxc * (xv if j == 0 else _shift_rows(xv, halo, j, False)), axis=0, keepdims=True)
                for j in range(CONV_W)]
        dw_ref[...] += jnp.concatenate([rows[CONV_W - 1 - k] for k in range(CONV_W)]
                                       + [jnp.zeros((8 - CONV_W, cw), F32)], axis=0)

    hblk = pl.BlockSpec((heads, ts, HEAD_DIM), lambda i: (0, i, 0))
    return _pcall(
        body, name="gdn_pre_bwd_act", grid=(s // ts,),
        in_specs=[pl.BlockSpec((ts, cw), lambda i: (i, 1)),
                  pl.BlockSpec((8, cw), lambda i: (jnp.maximum(i * tb - 1, 0), 1)),
                  pl.BlockSpec((CONV_W, cw), lambda i: (0, 0)), hblk, hblk, hblk],
        out_specs=[pl.BlockSpec((ts, cw), lambda i: (i, 0)), pl.BlockSpec((8, cw), lambda i: (0, 0))],
        out_shape=[jax.ShapeDtypeStruct((s, cw), F32), jax.ShapeDtypeStruct((8, cw), F32)],
        compiler_params=_params("arbitrary"),
    )(proj, proj, conv_w, dq, dk, dv)


def _gdn_pre_bwd_conv(dxc, conv_w):
    s, cw = dxc.shape
    ts = _tile(s, 256)
    tb = ts // 8
    last = s // 8 - 1

    def body(g_ref, halo_ref, w_ref, dx_ref):
        gv = g_ref[...]
        w = w_ref[...]
        halo = jnp.where(pl.program_id(0) == s // ts - 1, 0.0, halo_ref[...])
        dx = w[CONV_W - 1:CONV_W, :] * gv
        for j in range(1, CONV_W):
            dx = dx + w[CONV_W - 1 - j:CONV_W - j, :] * _shift_rows(gv, halo, j, True)
        dx_ref[...] = dx.astype(BF16)

    return _pcall(
        body, name="gdn_pre_bwd_conv", grid=(s // ts,),
        in_specs=[pl.BlockSpec((ts, cw), lambda i: (i, 0)),
                  pl.BlockSpec((8, cw), lambda i: (jnp.minimum((i + 1) * tb, last), 0)),
                  pl.BlockSpec((CONV_W, cw), lambda i: (0, 0))],
        out_specs=pl.BlockSpec((ts, cw), lambda i: (i, 0)),
        out_shape=jax.ShapeDtypeStruct((s, cw), BF16),
        compiler_params=_params("parallel"),
    )(dxc, dxc, conv_w)


def _bdot(a, b, ca, cb):
    return lax.dot_general(a.astype(BF16), b.astype(BF16), (((ca,), (cb,)), ((0,), (0,))),
                           preferred_element_type=F32)


def _bdot_hp(a, b, ca, cb):
    ah = a.astype(BF16)
    al = (a - ah.astype(F32)).astype(BF16)
    bh = b.astype(BF16)
    bl = (b - bh.astype(F32)).astype(BF16)
    d = lambda p, q: lax.dot_general(p, q, (((ca,), (cb,)), ((0,), (0,))), preferred_element_type=F32)
    return d(ah, bh) + (d(ah, bl) + d(al, bh))


def _gdn_gates(small, a_lane, dt_lane, heads):
    lane = lax.broadcasted_iota(jnp.int32, small.shape, 1)
    za = small + dt_lane
    g_all = -jnp.exp(a_lane) * (jnp.maximum(za, 0.0) + jnp.log(1.0 + jnp.exp(-jnp.abs(za))))
    b_all = _sigmoid(small)
    pick = lambda v, l: jnp.sum(jnp.where(lane == l, v, 0.0), axis=1, keepdims=True)
    g = jnp.stack([pick(g_all, heads + h) for h in range(heads)], axis=0)
    beta = jnp.stack([pick(b_all, 2 * heads + h) for h in range(heads)], axis=0)
    return g, beta


def _chunk_masks(c):
    ii = lax.broadcasted_iota(jnp.int32, (1, c, c), 1)
    jj = lax.broadcasted_iota(jnp.int32, (1, c, c), 2)
    return ii >= jj, ii > jj, ii == jj


def _col_to_row(col, eye):
    return jnp.sum(jnp.where(eye, col, 0.0), axis=1, keepdims=True)


def _row_to_col(row, eye):
    return jnp.sum(jnp.where(eye, row, 0.0), axis=2, keepdims=True)


def _gdn_chunk(q, k, v, g, beta, state, tinv=None):
    c = q.shape[1]
    incl, strict, eye = _chunk_masks(c)
    g_row = _col_to_row(g, eye)
    gc_col = jnp.sum(jnp.where(incl, g_row, 0.0), axis=2, keepdims=True)
    gc_row = _col_to_row(gc_col, eye)
    gam = jnp.where(incl, jnp.exp(jnp.where(incl, gc_col - gc_row, NEG)), 0.0)
    egc = jnp.exp(gc_col)
    kb = k * beta
    vb = v * beta
    kbe = kb * egc
    low = jnp.where(strict, _bdot(kb, k, 2, 2), 0.0) * gam
    if tinv is None:
        p = -low
        tinv = jnp.where(eye, 1.0, 0.0) + p
        width = 2
        while width < c:
            dot = _bdot_hp if width <= GDN_HP_WIDTH else _bdot
            p = dot(p, p, 2, 1)
            tinv = tinv + dot(tinv, p, 2, 1)
            width *= 2
    uw = _bdot(tinv, jnp.concatenate([vb, kbe], axis=2), 2, 1)
    u, w = uw[:, :, :HEAD_DIM], uw[:, :, HEAD_DIM:]
    att = jnp.where(incl, _bdot(q, k, 2, 2), 0.0) * gam
    qe = q * egc
    on_state = _bdot(jnp.concatenate([w, qe], axis=1), state, 2, 1)
    vn = u - on_state[:, :c]
    o = on_state[:, c:] + _bdot(att, vn, 2, 1)
    gl = jnp.sum(g, axis=1, keepdims=True)
    edec = jnp.exp(gl - gc_col)
    kdec = k * edec
    egl = jnp.exp(gl)
    new_state = state * egl + _bdot(kdec, vn, 1, 1)
    return dict(incl=incl, strict=strict, eye=eye, gam=gam, egc=egc, kb=kb, vb=vb, kbe=kbe, low=low, tinv=tinv, w=w,
                att=att, vn=vn, qe=qe, o=o, edec=edec, kdec=kdec, egl=egl, new_state=new_state)


def _gdn_load(q_ref, k_ref, v_ref, small_ref, a_ref, dt_ref, rows, heads):
    q = q_ref[:, rows, :] * (HEAD_DIM ** -0.5)
    g, beta = _gdn_gates(small_ref[rows, :], a_ref[...], dt_ref[...], heads)
    return q, k_ref[:, rows, :], v_ref[:, rows, :], g, beta


def _gdn_fwd(q, k, v, proj, z_blk, small_blk, a_lane, dt_lane, w_norm):
    heads, s, _ = q.shape
    c = min(GDN_CHUNK, s)
    r = _tile(s, 512)
    npb = r // c
    gw = heads * HEAD_DIM

    def body(q_ref, k_ref, v_ref, z_ref, small_ref, a_ref, dt_ref, w_ref, o_ref, st_ref, ti_ref, state):
        @pl.when(pl.program_id(0) == 0)
        def _():
            state[...] = jnp.zeros_like(state)

        def chunk(cb, carry):
            rows = pl.ds(pl.multiple_of(cb * c, c), c)
            qv, kv, vv, g, beta = _gdn_load(q_ref, k_ref, v_ref, small_ref, a_ref, dt_ref, rows, heads)
            st = state[...]
            st_ref[:, cb] = st
            res = _gdn_chunk(qv, kv, vv, g, beta, st)
            ti_ref[:, cb] = res["tinv"]
            state[...] = res["new_state"]
            o = res["o"]
            rn = lax.rsqrt(jnp.mean(o * o, axis=2, keepdims=True) + EPS)
            zv = z_ref[rows, :]
            for h in range(heads):
                zh = zv[:, h * HEAD_DIM:(h + 1) * HEAD_DIM]
                o_ref[rows, h * HEAD_DIM:(h + 1) * HEAD_DIM] = (
                    o[h] * rn[h] * w_ref[...] * (zh * _sigmoid(zh))).astype(BF16)
            return carry

        lax.fori_loop(0, npb, chunk, 0)

    hblk = pl.BlockSpec((heads, r, HEAD_DIM), lambda i: (0, i, 0))
    row = pl.BlockSpec((1, LANES), lambda i: (0, 0))
    return _pcall(
        body, name="gdn_fwd", grid=(s // r,),
        in_specs=[hblk, hblk, hblk, pl.BlockSpec((r, gw), lambda i: (i, z_blk)),
                  pl.BlockSpec((r, LANES), lambda i: (i, small_blk)), row, row, row],
        out_specs=[pl.BlockSpec((r, gw), lambda i: (i, 0)),
                   pl.BlockSpec((heads, npb, HEAD_DIM, HEAD_DIM), lambda i: (0, i, 0, 0)),
                   pl.BlockSpec((heads, npb, c, c), lambda i: (0, i, 0, 0))],
        out_shape=[jax.ShapeDtypeStruct((s, gw), BF16),
                   jax.ShapeDtypeStruct((heads, s // c, HEAD_DIM, HEAD_DIM), F32),
                   jax.ShapeDtypeStruct((heads, s // c, c, c), F32)],
        scratch_shapes=[pltpu.VMEM((heads, HEAD_DIM, HEAD_DIM), F32)],
        compiler_params=_params("arbitrary"),
    )(q, k, v, proj, proj, a_lane, dt_lane, w_norm)


def _gdn_bwd(q, k, v, proj, z_blk, small_blk, a_lane, dt_lane, w_norm, states, tinvs, do_cat, do_blk):
    heads, s, _ = q.shape
    c = min(GDN_CHUNK, s)
    r = _tile(s, 512)
    npb = r // c
    nb = s // r
    gw = heads * HEAD_DIM

    def body(q_ref, k_ref, v_ref, z_ref, small_ref, a_ref, dt_ref, w_ref, st_ref, ti_ref, do_ref,
             dq_ref, dk_ref, dv_ref, dz_ref, dsm_ref, da_ref, ddt_ref, dw_ref, dstate):
        @pl.when(pl.program_id(0) == 0)
        def _():
            dstate[...] = jnp.zeros_like(dstate)
            da_ref[...] = jnp.zeros_like(da_ref)
            ddt_ref[...] = jnp.zeros_like(ddt_ref)
            dw_ref[...] = jnp.zeros_like(dw_ref)

        def chunk(it, carry):
            cb = npb - 1 - it
            rows = pl.ds(pl.multiple_of(cb * c, c), c)
            qv, kv, vv, g, beta = _gdn_load(q_ref, k_ref, v_ref, small_ref, a_ref, dt_ref, rows, heads)
            st = st_ref[:, cb]
            f = _gdn_chunk(qv, kv, vv, g, beta, st, tinv=ti_ref[:, cb])
            incl, strict, eye = f["incl"], f["strict"], f["eye"]
            o = f["o"]
            wv = w_ref[...]
            zv = z_ref[rows, :]
            dov = do_ref[rows, :]
            rn = lax.rsqrt(jnp.mean(o * o, axis=2, keepdims=True) + EPS)
            do_l, dw_acc = [], jnp.zeros((1, HEAD_DIM), F32)
            for h in range(heads):
                sl = slice(h * HEAD_DIM, (h + 1) * HEAD_DIM)
                zh, gh = zv[:, sl], dov[:, sl]
                sg = _sigmoid(zh)
                on = o[h] * rn[h]
                dz_ref[rows, sl] = (gh * (on * wv) * (sg * (1.0 + zh * (1.0 - sg)))).astype(BF16)
                gn = gh * (zh * sg)
                dw_acc = dw_acc + jnp.sum(gn * on, axis=0, keepdims=True)
                wg = gn * wv
                do_l.append(rn[h] * wg - o[h] * (rn[h] * rn[h] * rn[h]) * jnp.mean(wg * o[h], axis=1, keepdims=True))
            dw_ref[...] += dw_acc
            do = jnp.stack(do_l, axis=0)
            ds_out = dstate[...]
            dvn = _bdot(f["att"], do, 1, 1) + _bdot(f["kdec"], ds_out, 2, 1)
            datt = jnp.where(incl, _bdot(do, f["vn"], 2, 2), 0.0)
            do_dvn = jnp.concatenate([do, dvn], axis=1)
            on_st = _bdot(do_dvn, st, 2, 2)
            dqe, dw = on_st[:, :c], -on_st[:, c:]
            dstate[...] = _bdot(jnp.concatenate([f["qe"], -f["w"]], axis=1), do_dvn, 1, 1) + f["egl"] * ds_out
            dkdec = _bdot(f["vn"], ds_out, 2, 2)
            t_kdec = jnp.sum(dkdec * f["kdec"], axis=2, keepdims=True)
            dgl = (jnp.sum(jnp.sum(st * ds_out, axis=2, keepdims=True), axis=1, keepdims=True) * f["egl"]
                   + jnp.sum(t_kdec, axis=1, keepdims=True))
            dgc = jnp.sum(dqe * f["qe"], axis=2, keepdims=True) - t_kdec
            dq = dqe * f["egc"]
            dk = dkdec * f["edec"]
            dvn_dw = jnp.concatenate([dvn, dw], axis=2)
            dtinv = _bdot(dvn_dw, jnp.concatenate([f["vb"], f["kbe"]], axis=2), 2, 2)
            through_t = _bdot(f["tinv"], dvn_dw, 1, 1)
            dvb, dkbe = through_t[:, :, :HEAD_DIM], through_t[:, :, HEAD_DIM:]
            dkb = dkbe * f["egc"]
            dgc = dgc + jnp.sum(dkbe * f["kbe"], axis=2, keepdims=True)
            dlow = jnp.where(strict, -_bdot_hp(_bdot_hp(f["tinv"], dtinv, 1, 1), f["tinv"], 2, 2), 0.0)
            ml = dlow * f["gam"]
            ma = datt * f["gam"]
            ml_ma = jnp.concatenate([ml, ma], axis=1)
            on_k = _bdot(ml_ma, kv, 2, 1)
            dkb = dkb + on_k[:, :c]
            dq = dq + on_k[:, c:]
            dk = dk + _bdot(ml_ma, jnp.concatenate([f["kb"], qv], axis=1), 1, 1)
            e = dlow * f["low"] + datt * f["att"]
            dgc = dgc + jnp.sum(e, axis=2, keepdims=True) - _row_to_col(jnp.sum(e, axis=1, keepdims=True), eye)
            dk = dk + beta * dkb
            dbeta = jnp.sum(dkb * kv, axis=2, keepdims=True) + jnp.sum(dvb * vv, axis=2, keepdims=True)
            dgc_row = _col_to_row(dgc, eye)
            dg = jnp.sum(jnp.where(incl, 0.0, dgc_row) + jnp.where(eye, dgc_row, 0.0), axis=2, keepdims=True) + dgl
            dq_ref[:, rows, :] = dq * (HEAD_DIM ** -0.5)
            dk_ref[:, rows, :] = dk
            dv_ref[:, rows, :] = beta * dvb
            small = small_ref[rows, :]
            lane = lax.broadcasted_iota(jnp.int32, small.shape, 1)
            dg_l = jnp.zeros(small.shape, F32)
            db_l = jnp.zeros(small.shape, F32)
            for h in range(heads):
                dg_l = dg_l + jnp.where(lane == heads + h, dg[h], 0.0)
                db_l = db_l + jnp.where(lane == 2 * heads + h, dbeta[h], 0.0)
            za = small + dt_ref[...]
            nexp = -jnp.exp(a_ref[...])
            softplus = jnp.maximum(za, 0.0) + jnp.log(1.0 + jnp.exp(-jnp.abs(za)))
            da_logit = dg_l * nexp * _sigmoid(za)
            sb = _sigmoid(small)
            dsm_ref[rows, :] = da_logit + db_l * sb * (1.0 - sb)
            ddt_ref[...] += jnp.sum(da_logit, axis=0, keepdims=True)
            da_ref[...] += jnp.sum(dg_l * nexp * softplus, axis=0, keepdims=True)
            return carry

        lax.fori_loop(0, npb, chunk, 0)

    rev = lambda i: nb - 1 - i
    hblk = pl.BlockSpec((heads, r, HEAD_DIM), lambda i: (0, rev(i), 0))
    row = pl.BlockSpec((1, LANES), lambda i: (0, 0))
    wide = lambda blk: pl.BlockSpec((r, gw), lambda i: (rev(i), blk))
    return _pcall(
        body, name="gdn_bwd", grid=(nb,),
        in_specs=[hblk, hblk, hblk, wide(z_blk), pl.BlockSpec((r, LANES), lambda i: (rev(i), small_blk)),
                  row, row, row, pl.BlockSpec((heads, npb, HEAD_DIM, HEAD_DIM), lambda i: (0, rev(i), 0, 0)),
                  pl.BlockSpec((heads, npb, c, c), lambda i: (0, rev(i), 0, 0)), wide(do_blk)],
        out_specs=[hblk, hblk, hblk, wide(0), pl.BlockSpec((r, LANES), lambda i: (rev(i), 0)), row, row, row],
        out_shape=[jax.ShapeDtypeStruct((heads, s, HEAD_DIM), F32)] * 3
        + [jax.ShapeDtypeStruct((s, gw), BF16), jax.ShapeDtypeStruct((s, LANES), F32)]
        + [jax.ShapeDtypeStruct((1, LANES), F32)] * 3,
        scratch_shapes=[pltpu.VMEM((heads, HEAD_DIM, HEAD_DIM), F32)],
        compiler_params=_params("arbitrary"),
    )(q, k, v, proj, proj, a_lane, dt_lane, w_norm, states, tinvs, do_cat)


def _final(x, target, gf):
    s, d = x.shape
    ts = _tile(s, 512)

    def body(x_ref, t_ref, g_ref, loss_ref, dx_ref, dg_ref):
        @pl.when(pl.program_id(0) == 0)
        def _():
            loss_ref[...] = jnp.zeros_like(loss_ref)
            dg_ref[...] = jnp.zeros_like(dg_ref)

        xv = x_ref[...]
        gv = g_ref[...]
        r = lax.rsqrt(jnp.mean(xv * xv, axis=-1, keepdims=True) + EPS)
        xn = xv * r
        err = xn * gv - t_ref[...]
        per_tok = jnp.mean(err * err, axis=-1, keepdims=True)
        loss_ref[...] += 0.5 * jnp.sum(per_tok, axis=0, keepdims=True)
        dy = err * (1.0 / d)
        dg_ref[...] += jnp.sum(dy * xn, axis=0, keepdims=True)
        dxn = dy * gv
        dx_ref[...] = r * (dxn - xn * jnp.mean(dxn * xn, axis=-1, keepdims=True))

    blk = pl.BlockSpec((ts, d), lambda i: (i, 0))
    row = pl.BlockSpec((1, d), lambda i: (0, 0))
    return _pcall(
        body, name="final_loss", grid=(s // ts,),
        in_specs=[blk, blk, row], out_specs=[pl.BlockSpec((1, LANES), lambda i: (0, 0)), blk, row],
        out_shape=[jax.ShapeDtypeStruct((1, LANES), F32), jax.ShapeDtypeStruct((s, d), F32),
                   jax.ShapeDtypeStruct((1, d), F32)],
        compiler_params=_params("arbitrary"),
    )(x, target, gf)


def _adamw(parts, w, m, v, name):
    npart, rows, cols = parts.shape
    tr = _tile(rows, max(8, ADAM_BLOCK_BYTES // (4 * npart * cols)))
    c1 = 1.0 - ADAM_B1 ** ADAM_STEP
    c2 = 1.0 - ADAM_B2 ** ADAM_STEP

    def body(p_ref, w_ref, m_ref, v_ref, g_ref, d_ref, mo_ref, vo_ref):
        g = p_ref[0].astype(F32)
        for i in range(1, npart):
            g = g + p_ref[i].astype(F32)
        mn = ADAM_B1 * m_ref[...] + (1.0 - ADAM_B1) * g
        vn = ADAM_B2 * v_ref[...] + (1.0 - ADAM_B2) * (g * g)
        g_ref[...] = g
        mo_ref[...] = mn
        vo_ref[...] = vn
        d_ref[...] = -ADAM_LR * ((mn / c1) / (jnp.sqrt(vn / c2) + ADAM_EPS) + ADAM_WD * w_ref[...])

    blk = pl.BlockSpec((tr, cols), lambda i: (i, 0))
    return _pcall(
        body, name=name, grid=(rows // tr,),
        in_specs=[pl.BlockSpec((npart, tr, cols), lambda i: (0, i, 0)), blk, blk, blk],
        out_specs=[blk] * 4, out_shape=[jax.ShapeDtypeStruct((rows, cols), F32)] * 4,
        compiler_params=_params("parallel"),
    )(parts, w, m, v)


def _adamw_layers(parts0, parts1, w, m, v, name):
    npart, rows, cols = parts0.shape
    tr = _tile(rows, max(8, ADAM_BLOCK_BYTES // (4 * npart * cols)))
    nb = rows // tr
    c1 = 1.0 - ADAM_B1 ** ADAM_STEP
    c2 = 1.0 - ADAM_B2 ** ADAM_STEP

    def body(p0_ref, p1_ref, w_ref, m_ref, v_ref, g_ref, d_ref, mo_ref, vo_ref):
        def update(p_ref):
            g = p_ref[0].astype(F32)
            for i in range(1, npart):
                g = g + p_ref[i].astype(F32)
            mn = ADAM_B1 * m_ref[...] + (1.0 - ADAM_B1) * g
            vn = ADAM_B2 * v_ref[...] + (1.0 - ADAM_B2) * (g * g)
            g_ref[...] = g
            mo_ref[...] = mn
            vo_ref[...] = vn
            d_ref[...] = -ADAM_LR * ((mn / c1) / (jnp.sqrt(vn / c2) + ADAM_EPS) + ADAM_WD * w_ref[...])

        @pl.when(pl.program_id(0) == 0)
        def _():
            update(p0_ref)

        @pl.when(pl.program_id(0) == 1)
        def _():
            update(p1_ref)

    blk = pl.BlockSpec((None, None, tr, cols), lambda l, i: (0, l, i, 0))
    p0 = pl.BlockSpec((npart, tr, cols), lambda l, i: (0, jnp.where(l == 0, i, nb - 1), 0))
    p1 = pl.BlockSpec((npart, tr, cols), lambda l, i: (0, jnp.where(l == 0, 0, i), 0))
    return _pcall(
        body, name=name, grid=(2, nb), in_specs=[p0, p1, blk, blk, blk], out_specs=[blk] * 4,
        out_shape=[jax.ShapeDtypeStruct(w.shape, F32)] * 4, compiler_params=_params("arbitrary", "arbitrary"),
    )(parts0, parts1, w, m, v)


def _pad_lanes(v, n=LANES, at=0):
    return jnp.pad(v, ((0, 0), (at, n - at - v.shape[1])))


def _my_cols(a, me, width):
    return lax.dynamic_slice_in_dim(a, me * width, width, axis=a.ndim - 1)


def kernel(x, c, ada_w, ada_b, norm_g, ffn_w_gate, ffn_w_up, ffn_w_down, w_in, w_out, fox_f_bias, fox_out_norm, gdn_conv, gdn_A_log, gdn_dt_bias, gdn_out_norm, final_norm, loss_target, m_ada_w, m_ada_b, m_norm_g, m_ffn_w_gate, m_ffn_w_up, m_ffn_w_down, m_w_in, m_w_out, m_fox_f_bias, m_fox_out_norm, m_gdn_conv, m_gdn_A_log, m_gdn_dt_bias, m_gdn_out_norm, m_final_norm, v_ada_w, v_ada_b, v_norm_g, v_ffn_w_gate, v_ffn_w_up, v_ffn_w_down, v_w_in, v_w_out, v_fox_f_bias, v_fox_out_norm, v_gdn_conv, v_gdn_A_log, v_gdn_dt_bias, v_gdn_out_norm, v_final_norm):
    me = _linear(_mesh_pos())
    x0 = x[0]
    s, d = x0.shape
    heads = d // (2 * HEAD_DIM)
    fw = heads * HEAD_DIM
    ng = norm_g.shape[-1]
    ncv = gdn_conv.shape[-1]
    nada = ada_w.shape[-1]
    in_w = w_in.shape[-1] * N_DEV
    in_pad = -(-in_w // 512) * 512

    pack = jnp.concatenate([c, norm_g[0].reshape(1, 3 * ng), gdn_conv[0].reshape(1, CONV_W * ncv)], axis=1)
    pack_all = _gather_row(pack, "gather_small_params")
    c_all = pack_all[:, :d]
    g_all = pack_all[:, d:d + 3 * ng].reshape(N_DEV, 3, ng).transpose(1, 0, 2).reshape(3, d)
    conv_all = pack_all[:, d + 3 * ng:].reshape(N_DEV, CONV_W, ncv).transpose(1, 0, 2).reshape(CONV_W, 3 * fw)

    mod_blk = _ada_fwd(c_all, ada_w[0], _my_cols(ada_b, me, nada))
    mod_all = _exchange([mod_blk], scatter=False, in_vmem=True, name="gather_mod")[0]
    mod = lax.dynamic_slice_in_dim(mod_all, me, 1, axis=1).reshape(N_MOD, d)
    sh1, sc1, gt1, sh2, sc2, gt2, sh3, sc3, gt3 = [mod[i:i + 1] for i in range(N_MOD)]

    wg_sh, wu_sh, wd_sh = [w[0].astype(BF16) for w in (ffn_w_gate, ffn_w_up, ffn_w_down)]
    layer = lambda w, i: w[i:i + 1]
    wg0, wu0 = _gather_via_sibling([layer(wg_sh, 0), layer(wu_sh, 0)], "gather_ffn1_up_weights")
    small_blk = 7 * heads

    bias_lane = _pad_lanes(fox_f_bias)
    a_lane = _pad_lanes(gdn_A_log, at=heads)
    dt_lane = _pad_lanes(gdn_dt_bias, at=heads)

    h1 = _norm_mod(x0, g_all[0:1], sc1, sh1, "norm_mod_1")
    (a1, b1, s1), (wd0, wout_g) = _ffn_up(h1, wg0, wu0, 0, "ffn1_up",
                                          carry=([layer(wd_sh, 0), w_out[0].astype(BF16)], False))
    (f1, x1), (win_g,) = _ffn_down(s1, wd0, 0, x0, gt1, "ffn1_down", carry=([w_in[0].astype(BF16)], False))
    win_full = win_g.transpose(1, 0, 2).reshape(d, in_w)
    o_f, o_qkv, o_a, o_z = 3 * fw, 3 * fw + heads, 6 * fw + heads, 6 * fw + 3 * heads
    win_al = jnp.concatenate(
        [win_full[:, :o_f], win_full[:, o_qkv:o_a], win_full[:, o_z:], win_full[:, o_f:o_qkv],
         win_full[:, o_a:o_z], jnp.zeros((d, in_pad - in_w), BF16)], axis=1)
    wout_full = wout_g.reshape(d, d)

    h2 = _norm_mod(x1, g_all[1:2], sc2, sh2, "norm_mod_2")
    proj = _mm(h2, win_al, name="in_proj", tn=1536)
    cum = _fox_gate(proj, small_blk, bias_lane)
    cum_t = cum[:, :heads].T
    cum_row = cum_t[:, None, :]
    cum_col = jnp.broadcast_to(cum_t[:, :, None], (heads, s, LANES))
    (o_raw, lse, lse_row, o_fox), (wg1, wu1, wd1) = _fox_fwd(
        proj, cum_col, cum_row, fox_out_norm, heads,
        carry=([layer(wg_sh, 1), layer(wu_sh, 1), layer(wd_sh, 1)], False))
    qg, kg, vg = _gdn_pre(proj, conv_all, heads)
    o_gdn, states, tinvs = _gdn_fwd(qg, kg, vg, proj, 6, small_blk, a_lane, dt_lane, gdn_out_norm)
    o_cat = jnp.concatenate([o_fox, o_gdn], axis=1)
    mix, x2 = _mm(o_cat, wout_full, name="out_proj", residual=(x1, gt2))

    h3 = _norm_mod(x2, g_all[2:3], sc3, sh3, "norm_mod_3")
    (a3, b3, s3), _ = _ffn_up(h3, wg1, wu1, 0, "ffn2_up")
    (f3, x3), _ = _ffn_down(s3, wd1, 0, x2, gt3, "ffn2_down")

    loss_row, dx3, d_final = _final(x3, loss_target[0], final_norm.reshape(1, d))
    loss = lax.psum(loss_row[0, 0], MESH_AXES)

    df3, dgt3 = _gate_bwd(dx3, f3, gt3, MACARON_W, "ffn2_gate_bwd")
    (da3, db3), _ = _ffn_bwd_act(df3, wd1, 0, a3, b3, "ffn2_bwd_act")
    (dwd2,), _ = _ffn_bwd_wd(s3, df3, "ffn2_bwd_wd")
    (dh3,), (r_wd2,) = _ffn_bwd_h(da3, db3, wg1, wu1, 0, "ffn2_bwd_h", carry=([dwd2], True))
    (dwg2, dwu2), _ = _ffn_bwd_wgu(h3, da3, db3, "ffn2_bwd_wgu")
    dx2, dsh3, dsc3, dg3 = _norm_mod_bwd(x2, dh3, dx3, g_all[2:3], sc3, "norm_mod_3_bwd")

    dmix, dgt2 = _gate_bwd(dx2, mix, gt2, 1.0, "mix_gate_bwd")
    do_cat = _mm(dmix, wout_full, tb=True, name="out_proj_bwd_x")
    dwout = _mm(o_cat, dmix, ta=True, out_dtype=BF16, name="out_proj_bwd_w", tk=512)
    do_fox, delta, delta_row, d_foxw = _fox_prep_bwd(do_cat, o_raw, fox_out_norm, heads)
    (dk_f, dv_f, dcum_k, dq_heads, dcum_q), (r_wg2, r_wu2, r_wout) = _fox_bwd(
        proj, do_fox, cum_col, cum_row, lse_row, delta_row, heads,
        carry=([dwg2, dwu2, dwout.reshape(N_DEV, d // N_DEV, d)], True))
    dq_f = dq_heads.transpose(1, 0, 2).reshape(s, fw)
    head_lanes = lambda t: jnp.pad(t.reshape(heads, s).T, ((0, 0), (0, LANES - heads)))
    dsm_fox, d_fbias = _fox_gate_bwd(head_lanes(dcum_q), head_lanes(dcum_k), proj, small_blk, bias_lane)
    dqg, dkg, dvg, dz, dsm_gdn, d_alog, d_dt, d_gdnw = _gdn_bwd(
        qg, kg, vg, proj, 6, small_blk, a_lane, dt_lane, gdn_out_norm, states, tinvs, do_cat, 1)
    dxc, d_conv = _gdn_pre_bwd_act(proj, conv_all, dqg, dkg, dvg, heads)
    dqkv = _gdn_pre_bwd_conv(dxc, conv_all)
    dsmall = (dsm_fox + dsm_gdn).astype(BF16)
    dproj = jnp.concatenate([dq_f, dk_f, dv_f, dqkv, dz, dsmall, jnp.zeros((s, in_pad - 7 * fw - LANES), BF16)], axis=1)
    dh2 = _mm(dproj, win_al, tb=True, name="in_proj_bwd_x", tk=1536)
    dwin_al = _mm(h2, dproj, ta=True, out_dtype=BF16, name="in_proj_bwd_w", tm=2048, tn=1536, tk=1024)
    dwin_full = jnp.concatenate(
        [dwin_al[:, :o_f], dwin_al[:, 7 * fw:7 * fw + heads], dwin_al[:, o_f:o_f + 3 * fw],
         dwin_al[:, 7 * fw + heads:7 * fw + 3 * heads], dwin_al[:, 6 * fw:7 * fw]], axis=1)
    dwin_parts = dwin_full.reshape(d, N_DEV, in_w // N_DEV).transpose(1, 0, 2)
    dx1, dsh2, dsc2, dg2 = _norm_mod_bwd(x1, dh2, dx2, g_all[1:2], sc2, "norm_mod_2_bwd")

    df1, dgt1 = _gate_bwd(dx1, f1, gt1, MACARON_W, "ffn1_gate_bwd")
    (da1, db1), (r_win,) = _ffn_bwd_act(df1, wd0, 0, a1, b1, "ffn1_bwd_act", carry=([dwin_parts], True))
    (dwd1,), _ = _ffn_bwd_wd(s1, df1, "ffn1_bwd_wd")
    (dwg1, dwu1), (r_wd1,) = _ffn_bwd_wgu(h1, da1, db1, "ffn1_bwd_wgu", carry=([dwd1], True))
    (dh1,), (r_wg1, r_wu1) = _ffn_bwd_h(da1, db1, wg0, wu0, 0, "ffn1_bwd_h", carry=([dwg1, dwu1], True))
    grad_x, dsh1, dsc1, dg1 = _norm_mod_bwd(x0, dh1, dx1, g_all[0:1], sc1, "norm_mod_1_bwd")

    dmod = jnp.concatenate([dsh1, dsc1, dgt1, dsh2, dsc2, dgt2, dsh3, dsc3, dgt3], axis=1)
    dmod_all = _gather_row(dmod, "gather_dmod")
    ct_pad = jnp.pad(c_all.T, ((0, 0), (0, LANES - N_DEV)))
    dmod_mine = jnp.pad(_my_cols(dmod_all, me, nada), ((0, LANES - N_DEV), (0, 0)))
    g_ada_w = _ada_bwd(ct_pad, dmod_mine)

    g_small_cols = [d_fbias, d_foxw, d_alog[:, heads:], d_dt[:, heads:], d_gdnw]
    small_part = jnp.concatenate(
        [_pad_lanes(v[:, :LANES]) for v in g_small_cols]
        + [d_final, dg1, dg2, dg3] + [d_conv[k:k + 1] for k in range(CONV_W)], axis=1)
    small_all = _gather_row(small_part, "gather_small_grads")
    off = 5 * LANES
    w_small = jnp.concatenate(
        [_pad_lanes(fox_f_bias), fox_out_norm, _pad_lanes(gdn_A_log), _pad_lanes(gdn_dt_bias), gdn_out_norm,
         final_norm.reshape(1, d)], axis=1)
    m_small = jnp.concatenate(
        [_pad_lanes(m_fox_f_bias), m_fox_out_norm, _pad_lanes(m_gdn_A_log), _pad_lanes(m_gdn_dt_bias),
         m_gdn_out_norm, m_final_norm.reshape(1, d)], axis=1)
    v_small = jnp.concatenate(
        [_pad_lanes(v_fox_f_bias), v_fox_out_norm, _pad_lanes(v_gdn_A_log), _pad_lanes(v_gdn_dt_bias),
         v_gdn_out_norm, v_final_norm.reshape(1, d)], axis=1)
    rep = _adamw(small_all[:, None, :off + d], w_small, m_small, v_small, "adamw_replicated")
    ab = _adamw(dmod_all[:, None, :], ada_b, m_ada_b, v_ada_b, "adamw_ada_b")
    g_ng = small_all[:, off + d:off + 4 * d].reshape(N_DEV, 3, d)
    ngs = _adamw(_my_cols(g_ng, me, ng), norm_g[0], m_norm_g[0], v_norm_g[0], "adamw_norm_g")
    g_cv = small_all[:, off + 4 * d:].reshape(N_DEV, CONV_W, 3 * fw)
    cvs = _adamw(_my_cols(g_cv, me, ncv), gdn_conv[0], m_gdn_conv[0], v_gdn_conv[0], "adamw_gdn_conv")

    wgs = _adamw_layers(r_wg1, r_wg2, ffn_w_gate, m_ffn_w_gate, v_ffn_w_gate, "adamw_w_gate")
    wus = _adamw_layers(r_wu1, r_wu2, ffn_w_up, m_ffn_w_up, v_ffn_w_up, "adamw_w_up")
    wds = _adamw_layers(r_wd1, r_wd2, ffn_w_down, m_ffn_w_down, v_ffn_w_down, "adamw_w_down")
    wis = [o[None] for o in _adamw(r_win, w_in[0], m_w_in[0], v_w_in[0], "adamw_w_in")]
    wos = [o[None] for o in _adamw(r_wout, w_out[0], m_w_out[0], v_w_out[0], "adamw_w_out")]
    adas = [o[None] for o in _adamw(g_ada_w[None], ada_w[0], m_ada_w[0], v_ada_w[0], "adamw_ada_w")]
    ngs = [o[None] for o in ngs]
    cvs = [o[None] for o in cvs]

    def rep_piece(i, lo, width):
        return rep[i][:, lo:lo + width]

    nh = fox_f_bias.shape[1]
    outs = []
    for i in range(4):
        outs.append([adas[i], ab[i], ngs[i], wgs[i], wus[i], wds[i], wis[i], wos[i],
                     rep_piece(i, 0, nh), rep_piece(i, LANES, HEAD_DIM), cvs[i], rep_piece(i, 2 * LANES, nh),
                     rep_piece(i, 3 * LANES, nh), rep_piece(i, 4 * LANES, HEAD_DIM), rep_piece(i, off, d).reshape(d)])
    return (loss, grad_x[None], *outs[0], *outs[1], *outs[2], *outs[3])
```

```python
import math

import numpy as np
import jax
import jax.numpy as jnp
from jax import lax
from jax.experimental import pallas as pl
from jax.experimental.pallas import tpu as pltpu

F32 = jnp.float32
BF16 = jnp.bfloat16

N_DEV = 8
MESH_AXES = ("x", "y", "c")
LANES = 128
HEAD_DIM = 128
GDN_CHUNK = 64
CONV_W = 4
N_MOD = 9
MACARON_W = 0.5
EPS = 1e-6
NEG = -1e30
VMEM_LIMIT_BYTES = 56 * 2 ** 20
ADAM_BLOCK_BYTES = 4 * 2 ** 20
GDN_HP_WIDTH = 4
FOX_BWD_HEADS_PER_STEP = 4
FOX_HEADS_PER_STEP = 8

ADAM_LR = 0.001
ADAM_B1 = 0.9
ADAM_B2 = 0.999
ADAM_EPS = 1e-08
ADAM_WD = 0.01
ADAM_STEP = 10

MESH_ID = pl.DeviceIdType.MESH
ANY = pl.BlockSpec(memory_space=pl.ANY)
VMEM = pl.BlockSpec(memory_space=pltpu.VMEM)


def _pcall(body, **kw):
    return pl.pallas_call(body, **kw)


def _params(*semantics):
    return pltpu.CompilerParams(dimension_semantics=semantics, vmem_limit_bytes=VMEM_LIMIT_BYTES)


def _tile(n, pref):
    if n % pref == 0 and pref % 8 == 0:
        return pref
    t = 1 << (max(1, min(n, pref)).bit_length() - 1)
    while n % t:
        t //= 2
    return t if t % 8 == 0 else n


def _sigmoid(x):
    return 1.0 / (1.0 + jnp.exp(-x))


def _dot(a, b, dims):
    return lax.dot_general(a.astype(BF16), b.astype(BF16), (dims, ((), ())), preferred_element_type=F32)


NN = ((1,), (0,))
NT = ((1,), (1,))
TN = ((0,), (0,))


def _split3(x):
    hi = x.astype(BF16)
    r1 = x - hi.astype(F32)
    mid = r1.astype(BF16)
    lo = (r1 - mid.astype(F32)).astype(BF16)
    return hi, mid, lo


def _dot_exact_lhs(m_bf16, x, dims=NN):
    hi, mid, lo = _split3(x)
    d = lambda p: lax.dot_general(m_bf16, p, (dims, ((), ())), preferred_element_type=F32)
    return d(hi) + (d(mid) + d(lo))


def _dot_hp(a, b, dims):
    ah = a.astype(BF16)
    al = (a - ah.astype(F32)).astype(BF16)
    bh = b.astype(BF16)
    bl = (b - bh.astype(F32)).astype(BF16)
    d = lambda p, q: lax.dot_general(p, q, (dims, ((), ())), preferred_element_type=F32)
    return d(ah, bh) + (d(ah, bl) + d(al, bh))


def _mesh_pos():
    return lax.axis_index("x"), lax.axis_index("y"), lax.axis_index("c")


def _peer(pos, mask):
    x, y, c = pos
    return (1 - x if mask & 4 else x, 1 - y if mask & 2 else y, 1 - c if mask & 1 else c)


def _linear(pos):
    return 4 * pos[0] + 2 * pos[1] + pos[2]


def _exchange_copies(ins, outs, sems, scatter, with_receives=True):
    send_sems, recv_sems, local_sems = sems
    pos = _mesh_pos()
    me = _linear(pos)
    local, sends, recvs = [], [], []
    for i in range(len(ins)):
        src = ins[i].at[me] if scatter else ins[i]
        local.append(pltpu.make_async_copy(src, outs[i].at[me], local_sems.at[i]))
    for mask in range(1, N_DEV):
        peer = _peer(pos, mask)
        for i in range(len(ins)):
            sem = dict(send_sem=send_sems.at[i, mask - 1], recv_sem=recv_sems.at[i, mask - 1],
                       device_id=peer, device_id_type=MESH_ID)
            sends.append(pltpu.make_async_remote_copy(
                src_ref=ins[i].at[_linear(peer)] if scatter else ins[i], dst_ref=outs[i].at[me], **sem))
            if with_receives:
                recvs.append(pltpu.make_async_remote_copy(
                    src_ref=ins[i].at[me] if scatter else ins[i], dst_ref=outs[i].at[_linear(peer)], **sem))
    return local, sends, recvs


def _exchange_start(ins, outs, sems, scatter):
    local, sends, _ = _exchange_copies(ins, outs, sems, scatter, with_receives=False)
    for cp in local + sends:
        cp.start()


def _exchange_wait(ins, outs, sems, scatter):
    local, sends, recvs = _exchange_copies(ins, outs, sems, scatter)
    for cp in recvs:
        cp.wait_recv()
    for cp in sends:
        cp.wait_send()
    for cp in local:
        cp.wait()


def _exchange_sems(n):
    return [pltpu.SemaphoreType.DMA((n, N_DEV - 1)), pltpu.SemaphoreType.DMA((n, N_DEV - 1)),
            pltpu.SemaphoreType.DMA((n,))]


def _exchange_shapes(arrays, scatter):
    return [jax.ShapeDtypeStruct(a.shape if scatter else (N_DEV,) + a.shape, a.dtype) for a in arrays]


def _exchange(arrays, *, scatter, in_vmem, name):
    n = len(arrays)

    def body(*refs):
        ins, outs, sems = refs[:n], refs[n:2 * n], refs[2 * n:]
        _exchange_start(ins, outs, sems, scatter)
        _exchange_wait(ins, outs, sems, scatter)

    spec = VMEM if in_vmem else ANY
    outs = _pcall(
        body, name=name, out_shape=_exchange_shapes(arrays, scatter),
        in_specs=[spec] * n, out_specs=[spec] * n, scratch_shapes=_exchange_sems(n),
    )(*arrays)
    return list(outs)


def _gather_via_sibling(arrays, name):
    n = len(arrays)

    def body(*refs):
        ins, outs = refs[:n], refs[n:2 * n]
        send_sems, recv_sems, local_sems = refs[2 * n:]
        x, y, c = _mesh_pos()
        me, sibling = (x, y, c), (x, y, 1 - c)
        chips = [(1 - x, y), (x, 1 - y), (1 - x, 1 - y)]

        def copy(i, k, block, to, from_input=False):
            return pltpu.make_async_remote_copy(
                src_ref=ins[i] if from_input else outs[i].at[_linear(block)], dst_ref=outs[i].at[_linear(block)],
                send_sem=send_sems.at[i, k], recv_sem=recv_sems.at[i, k], device_id=to, device_id_type=MESH_ID)

        mine = [pltpu.make_async_copy(ins[i], outs[i].at[_linear(me)], local_sems.at[i]) for i in range(n)]
        first = []
        for i in range(n):
            first.append(copy(i, 0, me, sibling, True))
            first += [copy(i, 1 + j, me, (*chip, c), True) for j, chip in enumerate(chips)]
        for cp in mine + first:
            cp.start()
        passed = []
        for j, chip in enumerate(chips):
            for i in range(n):
                copy(i, 1 + j, (*chip, c), me).wait_recv()
                cp = copy(i, 4 + j, (*chip, c), sibling)
                cp.start()
                passed.append(cp)
        for i in range(n):
            copy(i, 0, sibling, me).wait_recv()
            for j, chip in enumerate(chips):
                copy(i, 4 + j, (*chip, 1 - c), me).wait_recv()
        for cp in first + passed:
            cp.wait_send()
        for cp in mine:
            cp.wait()

    outs = _pcall(
        body, name=name, out_shape=_exchange_shapes(arrays, False), in_specs=[ANY] * n, out_specs=[ANY] * n,
        scratch_shapes=_exchange_sems(n),
    )(*arrays)
    return list(outs)


def _hosted(body, *, name, grid, in_specs, out_specs, out_shape, args, scratch_shapes=(), prefetch=(), carry=None):
    n_in, n_out, n_scr, n_pre = len(in_specs), len(out_shape), len(scratch_shapes), len(prefetch)
    arrays, scatter = carry if carry is not None else ([], False)
    n = len(arrays)

    def wrapped(*refs):
        pre, r = refs[:n_pre], refs[n_pre:]
        host_in, comm_in = r[:n_in], r[n_in:n_in + n]
        r = r[n_in + n:]
        host_out, comm_out = r[:n_out], r[n_out:n_out + n]
        r = r[n_out + n:]
        host_scr, sems = r[:n_scr], r[n_scr:]
        if n:
            first = pl.program_id(0) == 0
            last = pl.program_id(0) == grid[0] - 1
            for ax in range(1, len(grid)):
                first = first & (pl.program_id(ax) == 0)
                last = last & (pl.program_id(ax) == grid[ax] - 1)

            @pl.when(first)
            def _():
                _exchange_start(comm_in, comm_out, sems, scatter)

        body(*pre, *host_in, *host_out, *host_scr)
        if n:
            @pl.when(last)
            def _():
                _exchange_wait(comm_in, comm_out, sems, scatter)

    grid_spec = pltpu.PrefetchScalarGridSpec(
        num_scalar_prefetch=n_pre, grid=grid, in_specs=list(in_specs) + [ANY] * n,
        out_specs=list(out_specs) + [ANY] * n,
        scratch_shapes=list(scratch_shapes) + (_exchange_sems(n) if n else []))
    outs = _pcall(
        wrapped, name=name, grid_spec=grid_spec, out_shape=list(out_shape) + _exchange_shapes(arrays, scatter),
        compiler_params=_params(*(["arbitrary"] * len(grid))),
    )(*prefetch, *args, *arrays)
    return list(outs[:n_out]), list(outs[n_out:])


def _gather_row(v, name):
    return _exchange([v], scatter=False, in_vmem=True, name=name)[0].reshape(N_DEV, v.shape[1])


def _ada_fwd(c_all, w, b):
    d, n = w.shape
    tn = _tile(n, 256)

    def body(c_ref, w_ref, b_ref, o_ref):
        cv = c_ref[...]
        cond = cv * _sigmoid(cv)
        o_ref[...] = _dot_hp(cond, w_ref[...], NN) + b_ref[...]

    return _pcall(
        body, name="ada_fwd", grid=(n // tn,),
        in_specs=[pl.BlockSpec((N_DEV, d), lambda j: (0, 0)), pl.BlockSpec((d, tn), lambda j: (0, j)),
                  pl.BlockSpec((1, tn), lambda j: (0, j))],
        out_specs=pl.BlockSpec((N_DEV, tn), lambda j: (0, j)),
        out_shape=jax.ShapeDtypeStruct((N_DEV, n), F32), compiler_params=_params("parallel"),
    )(c_all, w, b)


def _ada_bwd(ct_pad, dmod_pad):
    d = ct_pad.shape[0]
    n = dmod_pad.shape[1]
    tn = _tile(n, 256)

    def body(c_ref, g_ref, o_ref):
        cv = c_ref[...]
        cond = cv * _sigmoid(cv)
        o_ref[...] = _dot_hp(cond, g_ref[...], NN)

    return _pcall(
        body, name="ada_bwd", grid=(n // tn,),
        in_specs=[pl.BlockSpec((d, LANES), lambda j: (0, 0)), pl.BlockSpec((LANES, tn), lambda j: (0, j))],
        out_specs=pl.BlockSpec((d, tn), lambda j: (0, j)),
        out_shape=jax.ShapeDtypeStruct((d, n), F32), compiler_params=_params("parallel"),
    )(ct_pad, dmod_pad)


def _norm_mod(x, g, sc, sh, name):
    s, d = x.shape
    ts = _tile(s, 512)

    def body(x_ref, g_ref, sc_ref, sh_ref, h_ref):
        xv = x_ref[...]
        r = lax.rsqrt(jnp.mean(xv * xv, axis=-1, keepdims=True) + EPS)
        h_ref[...] = (xv * r * g_ref[...] * (1.0 + sc_ref[...]) + sh_ref[...]).astype(BF16)

    row = pl.BlockSpec((1, d), lambda i: (0, 0))
    return _pcall(
        body, name=name, grid=(s // ts,),
        in_specs=[pl.BlockSpec((ts, d), lambda i: (i, 0)), row, row, row],
        out_specs=pl.BlockSpec((ts, d), lambda i: (i, 0)),
        out_shape=jax.ShapeDtypeStruct((s, d), BF16), compiler_params=_params("parallel"),
    )(x, g, sc, sh)


def _norm_mod_bwd(x, dh, dx_out, g, sc, name):
    s, d = x.shape
    ts = _tile(s, 512)

    def body(x_ref, dh_ref, dxo_ref, g_ref, sc_ref, dx_ref, dsh_ref, dsc_ref, dg_ref):
        @pl.when(pl.program_id(0) == 0)
        def _():
            dsh_ref[...] = jnp.zeros_like(dsh_ref)
            dsc_ref[...] = jnp.zeros_like(dsc_ref)
            dg_ref[...] = jnp.zeros_like(dg_ref)

        xv = x_ref[...]
        dh_v = dh_ref[...]
        gv = g_ref[...]
        one_sc = 1.0 + sc_ref[...]
        r = lax.rsqrt(jnp.mean(xv * xv, axis=-1, keepdims=True) + EPS)
        xn = xv * r
        dxn = dh_v * (gv * one_sc)
        dx_ref[...] = dxo_ref[...] + r * (dxn - xn * jnp.mean(dxn * xn, axis=-1, keepdims=True))
        t = dh_v * xn
        dsh_ref[...] += jnp.sum(dh_v, axis=0, keepdims=True)
        dsc_ref[...] += jnp.sum(t * gv, axis=0, keepdims=True)
        dg_ref[...] += jnp.sum(t * one_sc, axis=0, keepdims=True)

    blk = pl.BlockSpec((ts, d), lambda i: (i, 0))
    row = pl.BlockSpec((1, d), lambda i: (0, 0))
    return _pcall(
        body, name=name, grid=(s // ts,),
        in_specs=[blk, blk, blk, row, row], out_specs=[blk, row, row, row],
        out_shape=[jax.ShapeDtypeStruct((s, d), F32)] + [jax.ShapeDtypeStruct((1, d), F32)] * 3,
        compiler_params=_params("arbitrary"),
    )(x, dh, dx_out, g, sc)


def _gate_bwd(dx, f, gt, k, name):
    s, d = dx.shape
    ts = _tile(s, 512)

    def body(dx_ref, f_ref, gt_ref, df_ref, dgt_ref):
        @pl.when(pl.program_id(0) == 0)
        def _():
            dgt_ref[...] = jnp.zeros_like(dgt_ref)

        dxv = dx_ref[...]
        df_ref[...] = ((k * gt_ref[...]) * dxv).astype(BF16)
        dgt_ref[...] += k * jnp.sum(f_ref[...] * dxv, axis=0, keepdims=True)

    blk = pl.BlockSpec((ts, d), lambda i: (i, 0))
    row = pl.BlockSpec((1, d), lambda i: (0, 0))
    return _pcall(
        body, name=name, grid=(s // ts,),
        in_specs=[blk, blk, row], out_specs=[blk, row],
        out_shape=[jax.ShapeDtypeStruct((s, d), BF16), jax.ShapeDtypeStruct((1, d), F32)],
        compiler_params=_params("arbitrary"),
    )(dx, f, gt)


def _ffn_up(h, wg, wu, layer, name, carry=None):
    s, d = h.shape
    fs = wg.shape[-1]
    tm = _tile(s, 1024)

    def body(h_ref, wg_ref, wu_ref, a_ref, b_ref, s_ref):
        hv = h_ref[...]
        a = jnp.dot(hv, wg_ref[...], preferred_element_type=F32)
        b = jnp.dot(hv, wu_ref[...], preferred_element_type=F32)
        a_ref[...] = a.astype(BF16)
        b_ref[...] = b.astype(BF16)
        s_ref[...] = (a * _sigmoid(a) * b).astype(BF16)

    wspec = pl.BlockSpec((None, None, d, fs), lambda j, m: (j, layer, 0, 0))
    ospec = pl.BlockSpec((None, tm, fs), lambda j, m: (j, m, 0))
    return _hosted(
        body, name=name, grid=(N_DEV, s // tm),
        in_specs=[pl.BlockSpec((tm, d), lambda j, m: (m, 0)), wspec, wspec],
        out_specs=[ospec, ospec, ospec],
        out_shape=[jax.ShapeDtypeStruct((N_DEV, s, fs), BF16)] * 3,
        args=(h, wg, wu), carry=carry)


def _ffn_down(sv, wd, layer, x_in, gt, name, carry=None):
    _, s, fs = sv.shape
    d = wd.shape[-1]
    tm = _tile(s, 512)

    def body(s_ref, wd_ref, x_ref, gt_ref, f_ref, xo_ref, acc):
        j = pl.program_id(1)

        @pl.when(j == 0)
        def _():
            acc[...] = jnp.zeros_like(acc)

        acc[...] += jnp.dot(s_ref[...], wd_ref[...], preferred_element_type=F32)

        @pl.when(j == N_DEV - 1)
        def _():
            fv = acc[...]
            f_ref[...] = fv
            xo_ref[...] = x_ref[...] + (MACARON_W * gt_ref[...]) * fv

    blk = pl.BlockSpec((tm, d), lambda m, j: (m, 0))
    return _hosted(
        body, name=name, grid=(s // tm, N_DEV),
        in_specs=[pl.BlockSpec((None, tm, fs), lambda m, j: (j, m, 0)),
                  pl.BlockSpec((None, None, fs, d), lambda m, j: (j, layer, 0, 0)),
                  blk, pl.BlockSpec((1, d), lambda m, j: (0, 0))],
        out_specs=[blk, blk],
        out_shape=[jax.ShapeDtypeStruct((s, d), F32)] * 2,
        scratch_shapes=[pltpu.VMEM((tm, d), F32)],
        args=(sv, wd, x_in, gt), carry=carry)


def _ffn_bwd_act(df, wd, layer, a, b, sv, name, carry=None):
    s, d = df.shape
    fs = a.shape[-1]
    tm = _tile(s, 1024)
    nm = s // tm

    def body(df_ref, wd_ref, a_ref, b_ref, s_ref, da_ref, db_ref, dwd_ref, acc):
        @pl.when(pl.program_id(1) == 0)
        def _():
            acc[...] = jnp.zeros_like(acc)

        dfv = df_ref[...]
        ds = lax.dot_general(dfv, wd_ref[...], (NT, ((), ())), preferred_element_type=F32)
        av = a_ref[...].astype(F32)
        sg = _sigmoid(av)
        da_ref[...] = (ds * b_ref[...].astype(F32) * (sg * (1.0 + av * (1.0 - sg)))).astype(BF16)
        db_ref[...] = (ds * (av * sg)).astype(BF16)
        acc[...] += lax.dot_general(s_ref[...], dfv, (TN, ((), ())), preferred_element_type=F32)

        @pl.when(pl.program_id(1) == nm - 1)
        def _():
            dwd_ref[...] = acc[...].astype(BF16)

    hid = pl.BlockSpec((None, tm, fs), lambda j, m: (j, m, 0))
    return _hosted(
        body, name=name, grid=(N_DEV, nm),
        in_specs=[pl.BlockSpec((tm, d), lambda j, m: (m, 0)),
                  pl.BlockSpec((None, None, fs, d), lambda j, m: (j, layer, 0, 0)), hid, hid, hid],
        out_specs=[hid, hid, pl.BlockSpec((None, fs, d), lambda j, m: (j, 0, 0))],
        out_shape=[jax.ShapeDtypeStruct((N_DEV, s, fs), BF16)] * 2 + [jax.ShapeDtypeStruct((N_DEV, fs, d), BF16)],
        scratch_shapes=[pltpu.VMEM((fs, d), F32)],
        args=(df, wd, a, b, sv), carry=carry)


def _ffn_bwd_h(da, db, wg, wu, layer, name, carry=None):
    _, s, fs = da.shape
    d = wg.shape[-2]
    tm = _tile(s, 1024)

    def body(da_ref, db_ref, wg_ref, wu_ref, o_ref, acc):
        j = pl.program_id(1)

        @pl.when(j == 0)
        def _():
            acc[...] = jnp.zeros_like(acc)

        acc[...] += (lax.dot_general(da_ref[...], wg_ref[...], (NT, ((), ())), preferred_element_type=F32)
                     + lax.dot_general(db_ref[...], wu_ref[...], (NT, ((), ())), preferred_element_type=F32))

        @pl.when(j == N_DEV - 1)
        def _():
            o_ref[...] = acc[...]

    hid = pl.BlockSpec((None, tm, fs), lambda m, j: (j, m, 0))
    wspec = pl.BlockSpec((None, None, d, fs), lambda m, j: (j, layer, 0, 0))
    return _hosted(
        body, name=name, grid=(s // tm, N_DEV),
        in_specs=[hid, hid, wspec, wspec],
        out_specs=[pl.BlockSpec((tm, d), lambda m, j: (m, 0))],
        out_shape=[jax.ShapeDtypeStruct((s, d), F32)],
        scratch_shapes=[pltpu.VMEM((tm, d), F32)],
        args=(da, db, wg, wu), carry=carry)


def _ffn_bwd_wgu(h, da, db, name, carry=None):
    s, d = h.shape
    fs = da.shape[-1]
    tk = _tile(s, 1024)
    nk = s // tk

    def body(h_ref, da_ref, db_ref, og_ref, ou_ref, accg, accu):
        @pl.when(pl.program_id(1) == 0)
        def _():
            accg[...] = jnp.zeros_like(accg)
            accu[...] = jnp.zeros_like(accu)

        hv = h_ref[...]
        accg[...] += lax.dot_general(hv, da_ref[...], (TN, ((), ())), preferred_element_type=F32)
        accu[...] += lax.dot_general(hv, db_ref[...], (TN, ((), ())), preferred_element_type=F32)

        @pl.when(pl.program_id(1) == nk - 1)
        def _():
            og_ref[...] = accg[...].astype(BF16)
            ou_ref[...] = accu[...].astype(BF16)

    hid = pl.BlockSpec((None, tk, fs), lambda j, k: (j, k, 0))
    ospec = pl.BlockSpec((None, d, fs), lambda j, k: (j, 0, 0))
    return _hosted(
        body, name=name, grid=(N_DEV, nk),
        in_specs=[pl.BlockSpec((tk, d), lambda j, k: (k, 0)), hid, hid],
        out_specs=[ospec, ospec],
        out_shape=[jax.ShapeDtypeStruct((N_DEV, d, fs), BF16)] * 2,
        scratch_shapes=[pltpu.VMEM((d, fs), F32), pltpu.VMEM((d, fs), F32)],
        args=(h, da, db), carry=carry)


def _mm(a, b, *, ta=False, tb=False, out_dtype=F32, name, tm=1024, tn=1024, tk=2048, residual=None):
    m, kdim = (a.shape[1], a.shape[0]) if ta else a.shape
    n = b.shape[0] if tb else b.shape[1]
    tm, tn, tk = _tile(m, tm), _tile(n, tn), _tile(kdim, tk)
    nk = kdim // tk
    dims = ((0,) if ta else (1,), (1,) if tb else (0,))

    def body(*refs):
        a_ref, b_ref = refs[:2]
        acc = refs[-1]
        kk = pl.program_id(2)

        @pl.when(kk == 0)
        def _():
            acc[...] = jnp.zeros_like(acc)

        acc[...] += lax.dot_general(a_ref[...].astype(BF16), b_ref[...].astype(BF16), (dims, ((), ())),
                                    preferred_element_type=F32)

        @pl.when(kk == nk - 1)
        def _():
            if residual is None:
                refs[2][...] = acc[...].astype(out_dtype)
            else:
                res_ref, gate_ref, y_ref, xo_ref = refs[2:6]
                yv = acc[...]
                y_ref[...] = yv
                xo_ref[...] = res_ref[...] + gate_ref[...] * yv

    a_spec = pl.BlockSpec((tk, tm), lambda i, j, k: (k, i)) if ta else pl.BlockSpec((tm, tk), lambda i, j, k: (i, k))
    b_spec = pl.BlockSpec((tn, tk), lambda i, j, k: (j, k)) if tb else pl.BlockSpec((tk, tn), lambda i, j, k: (k, j))
    o_spec = pl.BlockSpec((tm, tn), lambda i, j, k: (i, j))
    if residual is None:
        in_specs, out_specs = [a_spec, b_spec], o_spec
        out_shape = jax.ShapeDtypeStruct((m, n), out_dtype)
        args = (a, b)
    else:
        in_specs = [a_spec, b_spec, o_spec, pl.BlockSpec((1, tn), lambda i, j, k: (0, j))]
        out_specs = [o_spec, o_spec]
        out_shape = [jax.ShapeDtypeStruct((m, n), F32)] * 2
        args = (a, b) + tuple(residual)
    return _pcall(
        body, name=name, grid=(m // tm, n // tn, nk), in_specs=in_specs, out_specs=out_specs, out_shape=out_shape,
        scratch_shapes=[pltpu.VMEM((tm, tn), F32)],
        compiler_params=_params("parallel", "parallel", "arbitrary"),
    )(*args)


def _log_sigmoid(z):
    return jnp.minimum(z, 0.0) - jnp.log(1.0 + jnp.exp(-jnp.abs(z)))


def _fox_gate(proj, small_blk, bias_lane):
    s = proj.shape[0]
    ts = _tile(s, 1024)
    nsub = ts // LANES

    def body(z_ref, b_ref, cum_ref, carry):
        @pl.when(pl.program_id(0) == 0)
        def _():
            carry[...] = jnp.zeros_like(carry)

        ii = lax.broadcasted_iota(jnp.int32, (LANES, LANES), 0)
        jj = lax.broadcasted_iota(jnp.int32, (LANES, LANES), 1)
        tri = (ii >= jj).astype(BF16)
        logf = _log_sigmoid(z_ref[...] + b_ref[...])
        cv = carry[...]
        for sb in range(nsub):
            blk = logf[sb * LANES:(sb + 1) * LANES, :]
            cum_ref[sb * LANES:(sb + 1) * LANES, :] = _dot_exact_lhs(tri, blk) + cv
            cv = cv + jnp.sum(blk, axis=0, keepdims=True)
        carry[...] = cv

    return _pcall(
        body, name="fox_gate", grid=(s // ts,),
        in_specs=[pl.BlockSpec((ts, LANES), lambda i: (i, small_blk)), pl.BlockSpec((1, LANES), lambda i: (0, 0))],
        out_specs=pl.BlockSpec((ts, LANES), lambda i: (i, 0)),
        out_shape=jax.ShapeDtypeStruct((s, LANES), F32),
        scratch_shapes=[pltpu.VMEM((1, LANES), F32)],
        compiler_params=_params("arbitrary"),
    )(proj, bias_lane)


def _fox_gate_bwd(dcum_q, dcum_k, proj, small_blk, bias_lane):
    s = proj.shape[0]
    ts = _tile(s, 1024)
    nsub = ts // LANES
    nb = s // ts

    def body(dcq_ref, dc_ref, z_ref, b_ref, dz_ref, db_ref, carry):
        @pl.when(pl.program_id(0) == 0)
        def _():
            carry[...] = jnp.zeros_like(carry)
            db_ref[...] = jnp.zeros_like(db_ref)

        ii = lax.broadcasted_iota(jnp.int32, (LANES, LANES), 0)
        jj = lax.broadcasted_iota(jnp.int32, (LANES, LANES), 1)
        triu = (jj >= ii).astype(BF16)
        dc = dcq_ref[...] + dc_ref[...]
        zb = z_ref[...] + b_ref[...]
        cv = carry[...]
        dbv = jnp.zeros((1, LANES), F32)
        for sb in reversed(range(nsub)):
            rows = slice(sb * LANES, (sb + 1) * LANES)
            blk = dc[rows, :]
            dlogf = _dot_exact_lhs(triu, blk) + cv
            cv = cv + jnp.sum(blk, axis=0, keepdims=True)
            dz = dlogf * _sigmoid(-zb[rows, :])
            dz_ref[rows, :] = dz
            dbv = dbv + jnp.sum(dz, axis=0, keepdims=True)
        carry[...] = cv
        db_ref[...] += dbv

    row = pl.BlockSpec((1, LANES), lambda i: (0, 0))
    return _pcall(
        body, name="fox_gate_bwd", grid=(nb,),
        in_specs=[pl.BlockSpec((ts, LANES), lambda i: (nb - 1 - i, 0)),
                  pl.BlockSpec((ts, LANES), lambda i: (nb - 1 - i, 0)),
                  pl.BlockSpec((ts, LANES), lambda i: (nb - 1 - i, small_blk)), row],
        out_specs=[pl.BlockSpec((ts, LANES), lambda i: (nb - 1 - i, 0)), row],
        out_shape=[jax.ShapeDtypeStruct((s, LANES), F32), jax.ShapeDtypeStruct((1, LANES), F32)],
        scratch_shapes=[pltpu.VMEM((1, LANES), F32)],
        compiler_params=_params("arbitrary"),
    )(dcum_q, dcum_k, proj, bias_lane)


def _tri_tables(n, by_key):
    if by_key:
        pairs = [(i, j) for j in range(n) for i in range(j, n)]
    else:
        pairs = [(i, j) for i in range(n) for j in range(i + 1)]
    return (jnp.asarray(np.array([p[0] for p in pairs], np.int32)),
            jnp.asarray(np.array([p[1] for p in pairs], np.int32)))


def _fox_group(heads):
    return FOX_HEADS_PER_STEP if heads % FOX_HEADS_PER_STEP == 0 else 1


def _as_row(col):
    t = col.shape[0]
    eye = lax.broadcasted_iota(jnp.int32, (t, t), 0) == lax.broadcasted_iota(jnp.int32, (t, t), 1)
    return jnp.sum(jnp.where(eye, col, 0.0), axis=0, keepdims=True)


LOG2E = 1.4426950408889634
FOX_Q_SCALE = LOG2E / math.sqrt(HEAD_DIM)


def _fox_scores(a, b, bias_col, bias_row, diagonal, rows_are_keys=False):
    sc = lax.dot_general(a.astype(BF16), b.astype(BF16), (NT, ((), ())), preferred_element_type=F32)
    sc = sc + (bias_col + bias_row)
    if not diagonal:
        return sc
    row = lax.broadcasted_iota(jnp.int32, sc.shape, 0)
    col = lax.broadcasted_iota(jnp.int32, sc.shape, 1)
    return jnp.where(row <= col if rows_are_keys else col <= row, sc, NEG)


def _fox_fwd(proj, cum_col, cum_row, w_norm, heads, carry=None):
    s = proj.shape[0]
    t = _tile(s, 512)
    grp = _fox_group(heads)
    qi, ki = _tri_tables(s // t, False)
    scale = 1.0 / math.sqrt(HEAD_DIM)

    def body(qi_ref, ki_ref, q_ref, k_ref, v_ref, cq_ref, ck_ref, w_ref, o_ref, lse_ref, lser_ref, on_ref, m_s, acc_s):
        iq, ik = qi_ref[pl.program_id(1)], ki_ref[pl.program_id(1)]

        @pl.when(ik == 0)
        def _():
            m_s[...] = jnp.full_like(m_s, NEG)
            acc_s[...] = jnp.zeros_like(acc_s)

        def step(diagonal):
            for g in range(grp):
                sl = slice(g * HEAD_DIM, (g + 1) * HEAD_DIM)
                qs = (q_ref[:, sl] * FOX_Q_SCALE).astype(BF16)
                sc = _fox_scores(qs, k_ref[:, sl], cq_ref[g, :, 0:1] * LOG2E, ck_ref[g] * (-LOG2E), diagonal)
                m_prev = m_s[g]
                m_new = jnp.maximum(m_prev, jnp.max(sc, axis=1, keepdims=True))
                p = jnp.exp2(sc - m_new).astype(BF16)
                v_ones = jnp.concatenate([v_ref[:, sl].astype(BF16), jnp.ones((t, LANES), BF16)], axis=1)
                acc_s[g] = jnp.exp2(m_prev - m_new) * acc_s[g] + jnp.dot(p, v_ones, preferred_element_type=F32)
                m_s[g] = m_new

        @pl.when(ik < iq)
        def _():
            step(False)

        @pl.when(ik == iq)
        def _():
            step(True)
            for g in range(grp):
                sl = slice(g * HEAD_DIM, (g + 1) * HEAD_DIM)
                acc = acc_s[g]
                o = acc[:, :HEAD_DIM] / acc[:, HEAD_DIM:]
                lse = m_s[g] + jnp.log(acc[:, HEAD_DIM:]) * LOG2E
                o_ref[:, sl] = o
                lse_ref[g] = lse
                lser_ref[g] = _as_row(lse[:, 0:1])
                r = lax.rsqrt(jnp.mean(o * o, axis=1, keepdims=True) + EPS)
                on_ref[:, sl] = (o * r * w_ref[...]).astype(BF16)

    ng = heads // grp
    qblk = pl.BlockSpec((t, grp * HEAD_DIM), lambda h, p, qi, ki: (qi[p], h))
    kblk = lambda off: pl.BlockSpec((t, grp * HEAD_DIM), lambda h, p, qi, ki: (ki[p], off + h))
    qcol = pl.BlockSpec((grp, t, LANES), lambda h, p, qi, ki: (h, qi[p], 0))
    return _hosted(
        body, name="fox_fwd", grid=(ng, int(qi.shape[0])), prefetch=(qi, ki),
        in_specs=[qblk, kblk(ng), kblk(2 * ng), qcol,
                  pl.BlockSpec((grp, 1, t), lambda h, p, qi, ki: (h, 0, ki[p])),
                  pl.BlockSpec((1, HEAD_DIM), lambda h, p, qi, ki: (0, 0))],
        out_specs=[qblk, qcol, pl.BlockSpec((grp, 1, t), lambda h, p, qi, ki: (h, 0, qi[p])), qblk],
        scratch_shapes=[pltpu.VMEM((grp, t, 1), F32), pltpu.VMEM((grp, t, 2 * HEAD_DIM), F32)],
        out_shape=[jax.ShapeDtypeStruct((s, heads * HEAD_DIM), F32), jax.ShapeDtypeStruct((heads, s, LANES), F32),
                   jax.ShapeDtypeStruct((heads, 1, s), F32), jax.ShapeDtypeStruct((s, heads * HEAD_DIM), BF16)],
        args=(proj, proj, proj, cum_col, cum_row, w_norm), carry=carry)


def _fox_prep_bwd(do_cat, o_raw, w_norm, heads):
    s = o_raw.shape[0]
    ts = _tile(s, 512)

    def body(g_ref, o_ref, w_ref, do_ref, delta_ref, deltar_ref, dw_ref):
        @pl.when((pl.program_id(0) == 0) & (pl.program_id(1) == 0))
        def _():
            dw_ref[...] = jnp.zeros_like(dw_ref)

        o = o_ref[...]
        g = g_ref[...]
        r = lax.rsqrt(jnp.mean(o * o, axis=1, keepdims=True) + EPS)
        wg = g * w_ref[...]
        do = r * wg - o * (r * r * r) * jnp.mean(wg * o, axis=1, keepdims=True)
        do_ref[...] = do.astype(BF16)
        delta = jnp.sum(do * o, axis=1, keepdims=True)
        delta_ref[...] = jnp.broadcast_to(delta, delta_ref.shape)
        deltar_ref[...] = _as_row(delta)
        dw_ref[...] += jnp.sum(g * o * r, axis=0, keepdims=True)

    blk = pl.BlockSpec((ts, HEAD_DIM), lambda h, i: (i, h))
    row = pl.BlockSpec((1, HEAD_DIM), lambda h, i: (0, 0))
    return _pcall(
        body, name="fox_prep_bwd", grid=(heads, s // ts),
        in_specs=[blk, blk, row],
        out_specs=[blk, pl.BlockSpec((None, ts, LANES), lambda h, i: (h, i, 0)),
                   pl.BlockSpec((None, 1, ts), lambda h, i: (h, 0, i)), row],
        out_shape=[jax.ShapeDtypeStruct((s, heads * HEAD_DIM), BF16), jax.ShapeDtypeStruct((heads, s, LANES), F32),
                   jax.ShapeDtypeStruct((heads, 1, s), F32), jax.ShapeDtypeStruct((1, HEAD_DIM), F32)],
        compiler_params=_params("arbitrary", "arbitrary"),
    )(do_cat, o_raw, w_norm)


def _fox_bwd(proj, do, cum_col, cum_row, lse_row, delta_row, heads, carry=None):
    s = proj.shape[0]
    t = _tile(s, 512)
    nk = s // t
    grp = FOX_BWD_HEADS_PER_STEP if heads % FOX_BWD_HEADS_PER_STEP == 0 else 1
    qi, ki = _tri_tables(nk, True)
    npairs = int(qi.shape[0])
    scale = 1.0 / math.sqrt(HEAD_DIM)

    def body(qi_ref, ki_ref, q_ref, k_ref, v_ref, do_ref, cqr_ref, ckc_ref, lse_ref, dl_ref,
             dk_ref, dv_ref, dck_ref, dq_hbm, dcq_ref, dk_acc, dv_acc, dck_acc, dq_acc, stage, sem):
        pair = pl.program_id(1)
        iq, ik = qi_ref[pair], ki_ref[pair]
        rows_q = pl.ds(pl.multiple_of(iq * t, t), t)

        @pl.when(pair == 0)
        def _():
            dq_acc[...] = jnp.zeros_like(dq_acc)
            dcq_ref[...] = jnp.zeros_like(dcq_ref)

        def step(diagonal):
            for g in range(grp):
                sl = slice(g * HEAD_DIM, (g + 1) * HEAD_DIM)
                qs = (q_ref[:, sl] * FOX_Q_SCALE).astype(BF16)
                kv = k_ref[:, sl]
                dov = do_ref[:, sl]
                st = _fox_scores(kv, qs, ckc_ref[g, :, 0:1] * (-LOG2E), cqr_ref[g] * LOG2E - lse_ref[g], diagonal, True)
                pt = jnp.exp2(st)
                dv_acc[g] += _dot(pt, dov, NN)
                dpt = _dot(v_ref[:, sl], dov, NT)
                dst = pt * (dpt - dl_ref[g])
                dk_acc[g] += _dot(dst, qs, NN)
                dck_acc[g] += jnp.sum(dst, axis=1, keepdims=True)
                dq_acc[g, rows_q, :] += _dot(dst, kv, TN)
                dcq_ref[g, iq] += jnp.sum(dst, axis=0, keepdims=True)

        @pl.when(iq == ik)
        def _():
            dk_acc[...] = jnp.zeros_like(dk_acc)
            dv_acc[...] = jnp.zeros_like(dv_acc)
            dck_acc[...] = jnp.zeros_like(dck_acc)
            step(True)

        @pl.when(iq > ik)
        def _():
            step(False)

        @pl.when(iq == nk - 1)
        def _():
            for g in range(grp):
                sl = slice(g * HEAD_DIM, (g + 1) * HEAD_DIM)
                dk_ref[:, sl] = (dk_acc[g] * (1.0 / LOG2E)).astype(BF16)
                dv_ref[:, sl] = dv_acc[g].astype(BF16)
                dck_ref[g] = _as_row(-dck_acc[g])

        @pl.when(pair == npairs - 1)
        def _():
            for g in range(grp):
                head = pl.program_id(0) * grp + g

                def flush(i, c):
                    rows = pl.ds(pl.multiple_of(i * t, t), t)
                    stage[...] = (dq_acc[g, rows, :] * scale).astype(BF16)
                    cp = pltpu.make_async_copy(stage, dq_hbm.at[head, rows, :], sem)
                    cp.start()
                    cp.wait()
                    return c

                lax.fori_loop(0, nk, flush, 0)

    ng = heads // grp
    qblk = pl.BlockSpec((t, grp * HEAD_DIM), lambda h, p, qi, ki: (qi[p], h))
    qrow = pl.BlockSpec((grp, 1, t), lambda h, p, qi, ki: (h, 0, qi[p]))
    kblk = lambda off: pl.BlockSpec((t, grp * HEAD_DIM), lambda h, p, qi, ki: (ki[p], off + h))
    kout = pl.BlockSpec((t, grp * HEAD_DIM), lambda h, p, qi, ki: (ki[p], h))
    return _hosted(
        body, name="fox_bwd", grid=(ng, npairs), prefetch=(qi, ki),
        in_specs=[qblk, kblk(ng), kblk(2 * ng), qblk, qrow,
                  pl.BlockSpec((grp, t, LANES), lambda h, p, qi, ki: (h, ki[p], 0)), qrow, qrow],
        out_specs=[kout, kout, pl.BlockSpec((grp, 1, t), lambda h, p, qi, ki: (h, 0, ki[p])), ANY,
                   pl.BlockSpec((grp, nk, 1, t), lambda h, p, qi, ki: (h, 0, 0, 0))],
        scratch_shapes=[pltpu.VMEM((grp, t, HEAD_DIM), F32), pltpu.VMEM((grp, t, HEAD_DIM), F32),
                        pltpu.VMEM((grp, t, 1), F32), pltpu.VMEM((grp, s, HEAD_DIM), F32),
                        pltpu.VMEM((t, HEAD_DIM), BF16), pltpu.SemaphoreType.DMA],
        out_shape=[jax.ShapeDtypeStruct((s, heads * HEAD_DIM), BF16)] * 2 + [jax.ShapeDtypeStruct((heads, 1, s), F32)]
        + [jax.ShapeDtypeStruct((heads, s, HEAD_DIM), BF16), jax.ShapeDtypeStruct((heads, nk, 1, t), F32)],
        args=(proj, proj, proj, do, cum_row, cum_col, lse_row, delta_row), carry=carry)


def _shift_rows(xv, halo, j, forward):
    n = xv.shape[0]
    rid = lax.broadcasted_iota(jnp.int32, (8, xv.shape[1]), 0)
    if forward:
        xs = pltpu.roll(xv, n - j, 0)
        hs = pltpu.roll(halo, 8 - j, 0)
        edge = jnp.where(rid >= 8 - j, hs, xs[n - 8:, :])
        return jnp.concatenate([xs[:n - 8, :], edge], axis=0)
    xs = pltpu.roll(xv, j, 0)
    hs = pltpu.roll(halo, j, 0)
    edge = jnp.where(rid < j, hs, xs[:8, :])
    return jnp.concatenate([edge, xs[8:, :]], axis=0)


def _conv_silu(xv, halo, w):
    xc = w[CONV_W - 1:CONV_W, :] * xv
    for j in range(1, CONV_W):
        xc = xc + w[CONV_W - 1 - j:CONV_W - j, :] * _shift_rows(xv, halo, j, False)
    return xc, xc * _sigmoid(xc)


def _gdn_pre(proj, conv_w, heads):
    s = proj.shape[0]
    cw = 3 * heads * HEAD_DIM
    ts = _tile(s, 256)
    tb = ts // 8

    def body(x_ref, halo_ref, w_ref, q_ref, k_ref, v_ref):
        halo = jnp.where(pl.program_id(0) == 0, 0.0, halo_ref[...])
        _, y = _conv_silu(x_ref[...], halo, w_ref[...])
        for h in range(heads):
            for part, ref in enumerate((q_ref, k_ref, v_ref)):
                c0 = (part * heads + h) * HEAD_DIM
                blk = y[:, c0:c0 + HEAD_DIM]
                if part < 2:
                    blk = blk * lax.rsqrt(jnp.sum(blk * blk, axis=1, keepdims=True) + EPS)
                ref[h] = blk

    out = pl.BlockSpec((heads, ts, HEAD_DIM), lambda i: (0, i, 0))
    return _pcall(
        body, name="gdn_pre", grid=(s // ts,),
        in_specs=[pl.BlockSpec((ts, cw), lambda i: (i, 1)),
                  pl.BlockSpec((8, cw), lambda i: (jnp.maximum(i * tb - 1, 0), 1)),
                  pl.BlockSpec((CONV_W, cw), lambda i: (0, 0))],
        out_specs=[out, out, out],
        out_shape=[jax.ShapeDtypeStruct((heads, s, HEAD_DIM), F32)] * 3,
        compiler_params=_params("parallel"),
    )(proj, proj, conv_w)


def _gdn_pre_bwd_act(proj, conv_w, dq, dk, dv, heads):
    s = proj.shape[0]
    cw = 3 * heads * HEAD_DIM
    ts = _tile(s, 256)
    tb = ts // 8

    def body(x_ref, halo_ref, w_ref, dq_ref, dk_ref, dv_ref, dxc_ref, dw_ref):
        @pl.when(pl.program_id(0) == 0)
        def _():
            dw_ref[...] = jnp.zeros_like(dw_ref)

        xv = x_ref[...]
        halo = jnp.where(pl.program_id(0) == 0, 0.0, halo_ref[...])
        xc, y = _conv_silu(xv, halo, w_ref[...])
        sg = _sigmoid(xc)
        dsilu = sg * (1.0 + xc * (1.0 - sg))
        for h in range(heads):
            for part, ref in enumerate((dq_ref, dk_ref, dv_ref)):
                c0 = (part * heads + h) * HEAD_DIM
                g = ref[h]
                if part < 2:
                    blk = y[:, c0:c0 + HEAD_DIM]
                    r = lax.rsqrt(jnp.sum(blk * blk, axis=1, keepdims=True) + EPS)
                    g = r * g - blk * (r * r * r) * jnp.sum(g * blk, axis=1, keepdims=True)
                dxc_ref[:, c0:c0 + HEAD_DIM] = g * dsilu[:, c0:c0 + HEAD_DIM]
        dxc = dxc_ref[...]
        rows = [jnp.sum(dxc * (xv if j == 0 else _shift_rows(xv, halo, j, False)), axis=0, keepdims=True)
                for j in range(CONV_W)]
        dw_ref[...] += jnp.concatenate([rows[CONV_W - 1 - k] for k in range(CONV_W)]
                                       + [jnp.zeros((8 - CONV_W, cw), F32)], axis=0)

    hblk = pl.BlockSpec((heads, ts, HEAD_DIM), lambda i: (0, i, 0))
    return _pcall(
        body, name="gdn_pre_bwd_act", grid=(s // ts,),
        in_specs=[pl.BlockSpec((ts, cw), lambda i: (i, 1)),
                  pl.BlockSpec((8, cw), lambda i: (jnp.maximum(i * tb - 1, 0), 1)),
                  pl.BlockSpec((CONV_W, cw), lambda i: (0, 0)), hblk, hblk, hblk],
        out_specs=[pl.BlockSpec((ts, cw), lambda i: (i, 0)), pl.BlockSpec((8, cw), lambda i: (0, 0))],
        out_shape=[jax.ShapeDtypeStruct((s, cw), F32), jax.ShapeDtypeStruct((8, cw), F32)],
        compiler_params=_params("arbitrary"),
    )(proj, proj, conv_w, dq, dk, dv)


def _gdn_pre_bwd_conv(dxc, conv_w):
    s, cw = dxc.shape
    ts = _tile(s, 256)
    tb = ts // 8
    last = s // 8 - 1

    def body(g_ref, halo_ref, w_ref, dx_ref):
        gv = g_ref[...]
        w = w_ref[...]
        halo = jnp.where(pl.program_id(0) == s // ts - 1, 0.0, halo_ref[...])
        dx = w[CONV_W - 1:CONV_W, :] * gv
        for j in range(1, CONV_W):
            dx = dx + w[CONV_W - 1 - j:CONV_W - j, :] * _shift_rows(gv, halo, j, True)
        dx_ref[...] = dx.astype(BF16)

    return _pcall(
        body, name="gdn_pre_bwd_conv", grid=(s // ts,),
        in_specs=[pl.BlockSpec((ts, cw), lambda i: (i, 0)),
                  pl.BlockSpec((8, cw), lambda i: (jnp.minimum((i + 1) * tb, last), 0)),
                  pl.BlockSpec((CONV_W, cw), lambda i: (0, 0))],
        out_specs=pl.BlockSpec((ts, cw), lambda i: (i, 0)),
        out_shape=jax.ShapeDtypeStruct((s, cw), BF16),
        compiler_params=_params("parallel"),
    )(dxc, dxc, conv_w)


def _bdot(a, b, ca, cb):
    return lax.dot_general(a.astype(BF16), b.astype(BF16), (((ca,), (cb,)), ((0,), (0,))),
                           preferred_element_type=F32)


def _bdot_hp(a, b, ca, cb):
    ah = a.astype(BF16)
    al = (a - ah.astype(F32)).astype(BF16)
    bh = b.astype(BF16)
    bl = (b - bh.astype(F32)).astype(BF16)
    d = lambda p, q: lax.dot_general(p, q, (((ca,), (cb,)), ((0,), (0,))), preferred_element_type=F32)
    return d(ah, bh) + (d(ah, bl) + d(al, bh))


def _gdn_gates(small, a_lane, dt_lane, heads):
    lane = lax.broadcasted_iota(jnp.int32, small.shape, 1)
    za = small + dt_lane
    g_all = -jnp.exp(a_lane) * (jnp.maximum(za, 0.0) + jnp.log(1.0 + jnp.exp(-jnp.abs(za))))
    b_all = _sigmoid(small)
    pick = lambda v, l: jnp.sum(jnp.where(lane == l, v, 0.0), axis=1, keepdims=True)
    g = jnp.stack([pick(g_all, heads + h) for h in range(heads)], axis=0)
    beta = jnp.stack([pick(b_all, 2 * heads + h) for h in range(heads)], axis=0)
    return g, beta


def _chunk_masks(c):
    ii = lax.broadcasted_iota(jnp.int32, (1, c, c), 1)
    jj = lax.broadcasted_iota(jnp.int32, (1, c, c), 2)
    return ii >= jj, ii > jj, ii == jj


def _col_to_row(col, eye):
    return jnp.sum(jnp.where(eye, col, 0.0), axis=1, keepdims=True)


def _row_to_col(row, eye):
    return jnp.sum(jnp.where(eye, row, 0.0), axis=2, keepdims=True)


def _gdn_chunk(q, k, v, g, beta, state, tinv=None):
    c = q.shape[1]
    incl, strict, eye = _chunk_masks(c)
    g_row = _col_to_row(g, eye)
    gc_col = jnp.sum(jnp.where(incl, g_row, 0.0), axis=2, keepdims=True)
    gc_row = _col_to_row(gc_col, eye)
    gam = jnp.where(incl, jnp.exp(jnp.where(incl, gc_col - gc_row, NEG)), 0.0)
    egc = jnp.exp(gc_col)
    kb = k * beta
    vb = v * beta
    kbe = kb * egc
    low = jnp.where(strict, _bdot(kb, k, 2, 2), 0.0) * gam
    if tinv is None:
        p = -low
        tinv = jnp.where(eye, 1.0, 0.0) + p
        width = 2
        while width < c:
            dot = _bdot_hp if width <= GDN_HP_WIDTH else _bdot
            p = dot(p, p, 2, 1)
            tinv = tinv + dot(tinv, p, 2, 1)
            width *= 2
    u = _bdot(tinv, vb, 2, 1)
    w = _bdot(tinv, kbe, 2, 1)
    att = jnp.where(incl, _bdot(q, k, 2, 2), 0.0) * gam
    vn = u - _bdot(w, state, 2, 1)
    qe = q * egc
    o = _bdot(qe, state, 2, 1) + _bdot(att, vn, 2, 1)
    gl = jnp.sum(g, axis=1, keepdims=True)
    edec = jnp.exp(gl - gc_col)
    kdec = k * edec
    egl = jnp.exp(gl)
    new_state = state * egl + _bdot(kdec, vn, 1, 1)
    return dict(incl=incl, strict=strict, eye=eye, gam=gam, egc=egc, kb=kb, vb=vb, kbe=kbe, low=low, tinv=tinv, w=w,
                att=att, vn=vn, qe=qe, o=o, edec=edec, kdec=kdec, egl=egl, new_state=new_state)


def _gdn_load(q_ref, k_ref, v_ref, small_ref, a_ref, dt_ref, rows, heads):
    q = q_ref[:, rows, :] * (HEAD_DIM ** -0.5)
    g, beta = _gdn_gates(small_ref[rows, :], a_ref[...], dt_ref[...], heads)
    return q, k_ref[:, rows, :], v_ref[:, rows, :], g, beta


def _gdn_fwd(q, k, v, proj, z_blk, small_blk, a_lane, dt_lane, w_norm):
    heads, s, _ = q.shape
    c = min(GDN_CHUNK, s)
    r = _tile(s, 512)
    npb = r // c
    gw = heads * HEAD_DIM

    def body(q_ref, k_ref, v_ref, z_ref, small_ref, a_ref, dt_ref, w_ref, o_ref, st_ref, ti_ref, state):
        @pl.when(pl.program_id(0) == 0)
        def _():
            state[...] = jnp.zeros_like(state)

        def chunk(cb, carry):
            rows = pl.ds(pl.multiple_of(cb * c, c), c)
            qv, kv, vv, g, beta = _gdn_load(q_ref, k_ref, v_ref, small_ref, a_ref, dt_ref, rows, heads)
            st = state[...]
            st_ref[:, cb] = st
            res = _gdn_chunk(qv, kv, vv, g, beta, st)
            ti_ref[:, cb] = res["tinv"]
            state[...] = res["new_state"]
            o = res["o"]
            rn = lax.rsqrt(jnp.mean(o * o, axis=2, keepdims=True) + EPS)
            zv = z_ref[rows, :]
            for h in range(heads):
                zh = zv[:, h * HEAD_DIM:(h + 1) * HEAD_DIM]
                o_ref[rows, h * HEAD_DIM:(h + 1) * HEAD_DIM] = (
                    o[h] * rn[h] * w_ref[...] * (zh * _sigmoid(zh))).astype(BF16)
            return carry

        lax.fori_loop(0, npb, chunk, 0)

    hblk = pl.BlockSpec((heads, r, HEAD_DIM), lambda i: (0, i, 0))
    row = pl.BlockSpec((1, LANES), lambda i: (0, 0))
    return _pcall(
        body, name="gdn_fwd", grid=(s // r,),
        in_specs=[hblk, hblk, hblk, pl.BlockSpec((r, gw), lambda i: (i, z_blk)),
                  pl.BlockSpec((r, LANES), lambda i: (i, small_blk)), row, row, row],
        out_specs=[pl.BlockSpec((r, gw), lambda i: (i, 0)),
                   pl.BlockSpec((heads, npb, HEAD_DIM, HEAD_DIM), lambda i: (0, i, 0, 0)),
                   pl.BlockSpec((heads, npb, c, c), lambda i: (0, i, 0, 0))],
        out_shape=[jax.ShapeDtypeStruct((s, gw), BF16),
                   jax.ShapeDtypeStruct((heads, s // c, HEAD_DIM, HEAD_DIM), F32),
                   jax.ShapeDtypeStruct((heads, s // c, c, c), F32)],
        scratch_shapes=[pltpu.VMEM((heads, HEAD_DIM, HEAD_DIM), F32)],
        compiler_params=_params("arbitrary"),
    )(q, k, v, proj, proj, a_lane, dt_lane, w_norm)


def _gdn_bwd(q, k, v, proj, z_blk, small_blk, a_lane, dt_lane, w_norm, states, tinvs, do_cat, do_blk):
    heads, s, _ = q.shape
    c = min(GDN_CHUNK, s)
    r = _tile(s, 512)
    npb = r // c
    nb = s // r
    gw = heads * HEAD_DIM

    def body(q_ref, k_ref, v_ref, z_ref, small_ref, a_ref, dt_ref, w_ref, st_ref, ti_ref, do_ref,
             dq_ref, dk_ref, dv_ref, dz_ref, dsm_ref, da_ref, ddt_ref, dw_ref, dstate):
        @pl.when(pl.program_id(0) == 0)
        def _():
            dstate[...] = jnp.zeros_like(dstate)
            da_ref[...] = jnp.zeros_like(da_ref)
            ddt_ref[...] = jnp.zeros_like(ddt_ref)
            dw_ref[...] = jnp.zeros_like(dw_ref)

        def chunk(it, carry):
            cb = npb - 1 - it
            rows = pl.ds(pl.multiple_of(cb * c, c), c)
            qv, kv, vv, g, beta = _gdn_load(q_ref, k_ref, v_ref, small_ref, a_ref, dt_ref, rows, heads)
            st = st_ref[:, cb]
            f = _gdn_chunk(qv, kv, vv, g, beta, st, tinv=ti_ref[:, cb])
            incl, strict, eye = f["incl"], f["strict"], f["eye"]
            o = f["o"]
            wv = w_ref[...]
            zv = z_ref[rows, :]
            dov = do_ref[rows, :]
            rn = lax.rsqrt(jnp.mean(o * o, axis=2, keepdims=True) + EPS)
            do_l, dw_acc = [], jnp.zeros((1, HEAD_DIM), F32)
            for h in range(heads):
                sl = slice(h * HEAD_DIM, (h + 1) * HEAD_DIM)
                zh, gh = zv[:, sl], dov[:, sl]
                sg = _sigmoid(zh)
                on = o[h] * rn[h]
                dz_ref[rows, sl] = (gh * (on * wv) * (sg * (1.0 + zh * (1.0 - sg)))).astype(BF16)
                gn = gh * (zh * sg)
                dw_acc = dw_acc + jnp.sum(gn * on, axis=0, keepdims=True)
                wg = gn * wv
                do_l.append(rn[h] * wg - o[h] * (rn[h] * rn[h] * rn[h]) * jnp.mean(wg * o[h], axis=1, keepdims=True))
            dw_ref[...] += dw_acc
            do = jnp.stack(do_l, axis=0)
            ds_out = dstate[...]
            dvn = _bdot(f["att"], do, 1, 1) + _bdot(f["kdec"], ds_out, 2, 1)
            datt = jnp.where(incl, _bdot(do, f["vn"], 2, 2), 0.0)
            dqe = _bdot(do, st, 2, 2)
            dstate[...] = _bdot(f["qe"], do, 1, 1) + f["egl"] * ds_out - _bdot(f["w"], dvn, 1, 1)
            dw = -_bdot(dvn, st, 2, 2)
            dkdec = _bdot(f["vn"], ds_out, 2, 2)
            t_kdec = jnp.sum(dkdec * f["kdec"], axis=2, keepdims=True)
            dgl = (jnp.sum(jnp.sum(st * ds_out, axis=2, keepdims=True), axis=1, keepdims=True) * f["egl"]
                   + jnp.sum(t_kdec, axis=1, keepdims=True))
            dgc = jnp.sum(dqe * f["qe"], axis=2, keepdims=True) - t_kdec
            dq = dqe * f["egc"]
            dk = dkdec * f["edec"]
            dtinv = _bdot(dvn, f["vb"], 2, 2) + _bdot(dw, f["kbe"], 2, 2)
            dvb = _bdot(f["tinv"], dvn, 1, 1)
            dkbe = _bdot(f["tinv"], dw, 1, 1)
            dkb = dkbe * f["egc"]
            dgc = dgc + jnp.sum(dkbe * f["kbe"], axis=2, keepdims=True)
            dlow = jnp.where(strict, -_bdot_hp(_bdot_hp(f["tinv"], dtinv, 1, 1), f["tinv"], 2, 2), 0.0)
            ml = dlow * f["gam"]
            dkb = dkb + _bdot(ml, kv, 2, 1)
            dk = dk + _bdot(ml, f["kb"], 1, 1)
            ma = datt * f["gam"]
            dq = dq + _bdot(ma, kv, 2, 1)
            dk = dk + _bdot(ma, qv, 1, 1)
            e = dlow * f["low"] + datt * f["att"]
            dgc = dgc + jnp.sum(e, axis=2, keepdims=True) - _row_to_col(jnp.sum(e, axis=1, keepdims=True), eye)
            dk = dk + beta * dkb
            dbeta = jnp.sum(dkb * kv, axis=2, keepdims=True) + jnp.sum(dvb * vv, axis=2, keepdims=True)
            dgc_row = _col_to_row(dgc, eye)
            dg = jnp.sum(jnp.where(incl, 0.0, dgc_row) + jnp.where(eye, dgc_row, 0.0), axis=2, keepdims=True) + dgl
            dq_ref[:, rows, :] = dq * (HEAD_DIM ** -0.5)
            dk_ref[:, rows, :] = dk
            dv_ref[:, rows, :] = beta * dvb
            small = small_ref[rows, :]
            lane = lax.broadcasted_iota(jnp.int32, small.shape, 1)
            dg_l = jnp.zeros(small.shape, F32)
            db_l = jnp.zeros(small.shape, F32)
            for h in range(heads):
                dg_l = dg_l + jnp.where(lane == heads + h, dg[h], 0.0)
                db_l = db_l + jnp.where(lane == 2 * heads + h, dbeta[h], 0.0)
            za = small + dt_ref[...]
            nexp = -jnp.exp(a_ref[...])
            softplus = jnp.maximum(za, 0.0) + jnp.log(1.0 + jnp.exp(-jnp.abs(za)))
            da_logit = dg_l * nexp * _sigmoid(za)
            sb = _sigmoid(small)
            dsm_ref[rows, :] = da_logit + db_l * sb * (1.0 - sb)
            ddt_ref[...] += jnp.sum(da_logit, axis=0, keepdims=True)
            da_ref[...] += jnp.sum(dg_l * nexp * softplus, axis=0, keepdims=True)
            return carry

        lax.fori_loop(0, npb, chunk, 0)

    rev = lambda i: nb - 1 - i
    hblk = pl.BlockSpec((heads, r, HEAD_DIM), lambda i: (0, rev(i), 0))
    row = pl.BlockSpec((1, LANES), lambda i: (0, 0))
    wide = lambda blk: pl.BlockSpec((r, gw), lambda i: (rev(i), blk))
    return _pcall(
        body, name="gdn_bwd", grid=(nb,),
        in_specs=[hblk, hblk, hblk, wide(z_blk), pl.BlockSpec((r, LANES), lambda i: (rev(i), small_blk)),
                  row, row, row, pl.BlockSpec((heads, npb, HEAD_DIM, HEAD_DIM), lambda i: (0, rev(i), 0, 0)),
                  pl.BlockSpec((heads, npb, c, c), lambda i: (0, rev(i), 0, 0)), wide(do_blk)],
        out_specs=[hblk, hblk, hblk, wide(0), pl.BlockSpec((r, LANES), lambda i: (rev(i), 0)), row, row, row],
        out_shape=[jax.ShapeDtypeStruct((heads, s, HEAD_DIM), F32)] * 3
        + [jax.ShapeDtypeStruct((s, gw), BF16), jax.ShapeDtypeStruct((s, LANES), F32)]
        + [jax.ShapeDtypeStruct((1, LANES), F32)] * 3,
        scratch_shapes=[pltpu.VMEM((heads, HEAD_DIM, HEAD_DIM), F32)],
        compiler_params=_params("arbitrary"),
    )(q, k, v, proj, proj, a_lane, dt_lane, w_norm, states, tinvs, do_cat)


def _final(x, target, gf):
    s, d = x.shape
    ts = _tile(s, 512)

    def body(x_ref, t_ref, g_ref, loss_ref, dx_ref, dg_ref):
        @pl.when(pl.program_id(0) == 0)
        def _():
            loss_ref[...] = jnp.zeros_like(loss_ref)
            dg_ref[...] = jnp.zeros_like(dg_ref)

        xv = x_ref[...]
        gv = g_ref[...]
        r = lax.rsqrt(jnp.mean(xv * xv, axis=-1, keepdims=True) + EPS)
        xn = xv * r
        err = xn * gv - t_ref[...]
        per_tok = jnp.mean(err * err, axis=-1, keepdims=True)
        loss_ref[...] += 0.5 * jnp.sum(per_tok, axis=0, keepdims=True)
        dy = err * (1.0 / d)
        dg_ref[...] += jnp.sum(dy * xn, axis=0, keepdims=True)
        dxn = dy * gv
        dx_ref[...] = r * (dxn - xn * jnp.mean(dxn * xn, axis=-1, keepdims=True))

    blk = pl.BlockSpec((ts, d), lambda i: (i, 0))
    row = pl.BlockSpec((1, d), lambda i: (0, 0))
    return _pcall(
        body, name="final_loss", grid=(s // ts,),
        in_specs=[blk, blk, row], out_specs=[pl.BlockSpec((1, LANES), lambda i: (0, 0)), blk, row],
        out_shape=[jax.ShapeDtypeStruct((1, LANES), F32), jax.ShapeDtypeStruct((s, d), F32),
                   jax.ShapeDtypeStruct((1, d), F32)],
        compiler_params=_params("arbitrary"),
    )(x, target, gf)


def _adamw(parts, w, m, v, name):
    npart, rows, cols = parts.shape
    tr = _tile(rows, max(8, ADAM_BLOCK_BYTES // (4 * npart * cols)))
    c1 = 1.0 - ADAM_B1 ** ADAM_STEP
    c2 = 1.0 - ADAM_B2 ** ADAM_STEP

    def body(p_ref, w_ref, m_ref, v_ref, g_ref, d_ref, mo_ref, vo_ref):
        g = p_ref[0].astype(F32)
        for i in range(1, npart):
            g = g + p_ref[i].astype(F32)
        mn = ADAM_B1 * m_ref[...] + (1.0 - ADAM_B1) * g
        vn = ADAM_B2 * v_ref[...] + (1.0 - ADAM_B2) * (g * g)
        g_ref[...] = g
        mo_ref[...] = mn
        vo_ref[...] = vn
        d_ref[...] = -ADAM_LR * ((mn / c1) / (jnp.sqrt(vn / c2) + ADAM_EPS) + ADAM_WD * w_ref[...])

    blk = pl.BlockSpec((tr, cols), lambda i: (i, 0))
    return _pcall(
        body, name=name, grid=(rows // tr,),
        in_specs=[pl.BlockSpec((npart, tr, cols), lambda i: (0, i, 0)), blk, blk, blk],
        out_specs=[blk] * 4, out_shape=[jax.ShapeDtypeStruct((rows, cols), F32)] * 4,
        compiler_params=_params("parallel"),
    )(parts, w, m, v)


def _adamw_layers(parts0, parts1, w, m, v, name):
    npart, rows, cols = parts0.shape
    tr = _tile(rows, max(8, ADAM_BLOCK_BYTES // (4 * npart * cols)))
    nb = rows // tr
    c1 = 1.0 - ADAM_B1 ** ADAM_STEP
    c2 = 1.0 - ADAM_B2 ** ADAM_STEP

    def body(p0_ref, p1_ref, w_ref, m_ref, v_ref, g_ref, d_ref, mo_ref, vo_ref):
        def update(p_ref):
            g = p_ref[0].astype(F32)
            for i in range(1, npart):
                g = g + p_ref[i].astype(F32)
            mn = ADAM_B1 * m_ref[...] + (1.0 - ADAM_B1) * g
            vn = ADAM_B2 * v_ref[...] + (1.0 - ADAM_B2) * (g * g)
            g_ref[...] = g
            mo_ref[...] = mn
            vo_ref[...] = vn
            d_ref[...] = -ADAM_LR * ((mn / c1) / (jnp.sqrt(vn / c2) + ADAM_EPS) + ADAM_WD * w_ref[...])

        @pl.when(pl.program_id(0) == 0)
        def _():
            update(p0_ref)

        @pl.when(pl.program_id(0) == 1)
        def _():
            update(p1_ref)

    blk = pl.BlockSpec((None, None, tr, cols), lambda l, i: (0, l, i, 0))
    p0 = pl.BlockSpec((npart, tr, cols), lambda l, i: (0, jnp.where(l == 0, i, nb - 1), 0))
    p1 = pl.BlockSpec((npart, tr, cols), lambda l, i: (0, jnp.where(l == 0, 0, i), 0))
    return _pcall(
        body, name=name, grid=(2, nb), in_specs=[p0, p1, blk, blk, blk], out_specs=[blk] * 4,
        out_shape=[jax.ShapeDtypeStruct(w.shape, F32)] * 4, compiler_params=_params("arbitrary", "arbitrary"),
    )(parts0, parts1, w, m, v)


def _pad_lanes(v, n=LANES, at=0):
    return jnp.pad(v, ((0, 0), (at, n - at - v.shape[1])))


def _my_cols(a, me, width):
    return lax.dynamic_slice_in_dim(a, me * width, width, axis=a.ndim - 1)


def kernel(x, c, ada_w, ada_b, norm_g, ffn_w_gate, ffn_w_up, ffn_w_down, w_in, w_out, fox_f_bias, fox_out_norm, gdn_conv, gdn_A_log, gdn_dt_bias, gdn_out_norm, final_norm, loss_target, m_ada_w, m_ada_b, m_norm_g, m_ffn_w_gate, m_ffn_w_up, m_ffn_w_down, m_w_in, m_w_out, m_fox_f_bias, m_fox_out_norm, m_gdn_conv, m_gdn_A_log, m_gdn_dt_bias, m_gdn_out_norm, m_final_norm, v_ada_w, v_ada_b, v_norm_g, v_ffn_w_gate, v_ffn_w_up, v_ffn_w_down, v_w_in, v_w_out, v_fox_f_bias, v_fox_out_norm, v_gdn_conv, v_gdn_A_log, v_gdn_dt_bias, v_gdn_out_norm, v_final_norm):
    me = _linear(_mesh_pos())
    x0 = x[0]
    s, d = x0.shape
    heads = d // (2 * HEAD_DIM)
    fw = heads * HEAD_DIM
    ng = norm_g.shape[-1]
    ncv = gdn_conv.shape[-1]
    nada = ada_w.shape[-1]
    in_w = w_in.shape[-1] * N_DEV
    in_pad = -(-in_w // 512) * 512

    pack = jnp.concatenate([c, norm_g[0].reshape(1, 3 * ng), gdn_conv[0].reshape(1, CONV_W * ncv)], axis=1)
    pack_all = _gather_row(pack, "gather_small_params")
    c_all = pack_all[:, :d]
    g_all = pack_all[:, d:d + 3 * ng].reshape(N_DEV, 3, ng).transpose(1, 0, 2).reshape(3, d)
    conv_all = pack_all[:, d + 3 * ng:].reshape(N_DEV, CONV_W, ncv).transpose(1, 0, 2).reshape(CONV_W, 3 * fw)

    mod_blk = _ada_fwd(c_all, ada_w[0], _my_cols(ada_b, me, nada))
    mod_all = _exchange([mod_blk], scatter=False, in_vmem=True, name="gather_mod")[0]
    mod = lax.dynamic_slice_in_dim(mod_all, me, 1, axis=1).reshape(N_MOD, d)
    sh1, sc1, gt1, sh2, sc2, gt2, sh3, sc3, gt3 = [mod[i:i + 1] for i in range(N_MOD)]

    wg_sh, wu_sh, wd_sh = [w[0].astype(BF16) for w in (ffn_w_gate, ffn_w_up, ffn_w_down)]
    layer = lambda w, i: w[i:i + 1]
    wg0, wu0 = _gather_via_sibling([layer(wg_sh, 0), layer(wu_sh, 0)], "gather_ffn1_up_weights")
    small_blk = 7 * heads

    bias_lane = _pad_lanes(fox_f_bias)
    a_lane = _pad_lanes(gdn_A_log, at=heads)
    dt_lane = _pad_lanes(gdn_dt_bias, at=heads)

    h1 = _norm_mod(x0, g_all[0:1], sc1, sh1, "norm_mod_1")
    (a1, b1, s1), (wd0, wout_g) = _ffn_up(h1, wg0, wu0, 0, "ffn1_up",
                                          carry=([layer(wd_sh, 0), w_out[0].astype(BF16)], False))
    (f1, x1), (win_g,) = _ffn_down(s1, wd0, 0, x0, gt1, "ffn1_down", carry=([w_in[0].astype(BF16)], False))
    win_full = win_g.transpose(1, 0, 2).reshape(d, in_w)
    o_f, o_qkv, o_a, o_z = 3 * fw, 3 * fw + heads, 6 * fw + heads, 6 * fw + 3 * heads
    win_al = jnp.concatenate(
        [win_full[:, :o_f], win_full[:, o_qkv:o_a], win_full[:, o_z:], win_full[:, o_f:o_qkv],
         win_full[:, o_a:o_z], jnp.zeros((d, in_pad - in_w), BF16)], axis=1)
    wout_full = wout_g.reshape(d, d)

    h2 = _norm_mod(x1, g_all[1:2], sc2, sh2, "norm_mod_2")
    proj = _mm(h2, win_al, name="in_proj", tn=1536)
    cum = _fox_gate(proj, small_blk, bias_lane)
    cum_t = cum[:, :heads].T
    cum_row = cum_t[:, None, :]
    cum_col = jnp.broadcast_to(cum_t[:, :, None], (heads, s, LANES))
    (o_raw, lse, lse_row, o_fox), (wg1, wu1, wd1) = _fox_fwd(
        proj, cum_col, cum_row, fox_out_norm, heads,
        carry=([layer(wg_sh, 1), layer(wu_sh, 1), layer(wd_sh, 1)], False))
    qg, kg, vg = _gdn_pre(proj, conv_all, heads)
    o_gdn, states, tinvs = _gdn_fwd(qg, kg, vg, proj, 6, small_blk, a_lane, dt_lane, gdn_out_norm)
    o_cat = jnp.concatenate([o_fox, o_gdn], axis=1)
    mix, x2 = _mm(o_cat, wout_full, name="out_proj", residual=(x1, gt2))

    h3 = _norm_mod(x2, g_all[2:3], sc3, sh3, "norm_mod_3")
    (a3, b3, s3), _ = _ffn_up(h3, wg1, wu1, 0, "ffn2_up")
    (f3, x3), _ = _ffn_down(s3, wd1, 0, x2, gt3, "ffn2_down")

    loss_row, dx3, d_final = _final(x3, loss_target[0], final_norm.reshape(1, d))
    loss = lax.psum(loss_row[0, 0], MESH_AXES)

    df3, dgt3 = _gate_bwd(dx3, f3, gt3, MACARON_W, "ffn2_gate_bwd")
    (da3, db3, dwd2), _ = _ffn_bwd_act(df3, wd1, 0, a3, b3, s3, "ffn2_bwd_act")
    (dh3,), (r_wd2,) = _ffn_bwd_h(da3, db3, wg1, wu1, 0, "ffn2_bwd_h", carry=([dwd2], True))
    (dwg2, dwu2), _ = _ffn_bwd_wgu(h3, da3, db3, "ffn2_bwd_wgu")
    dx2, dsh3, dsc3, dg3 = _norm_mod_bwd(x2, dh3, dx3, g_all[2:3], sc3, "norm_mod_3_bwd")

    dmix, dgt2 = _gate_bwd(dx2, mix, gt2, 1.0, "mix_gate_bwd")
    do_cat = _mm(dmix, wout_full, tb=True, name="out_proj_bwd_x")
    dwout = _mm(o_cat, dmix, ta=True, out_dtype=BF16, name="out_proj_bwd_w", tk=512)
    do_fox, delta, delta_row, d_foxw = _fox_prep_bwd(do_cat, o_raw, fox_out_norm, heads)
    (dk_f, dv_f, dcum_k, dq_heads, dcum_q), (r_wg2, r_wu2, r_wout) = _fox_bwd(
        proj, do_fox, cum_col, cum_row, lse_row, delta_row, heads,
        carry=([dwg2, dwu2, dwout.reshape(N_DEV, d // N_DEV, d)], True))
    dq_f = dq_heads.transpose(1, 0, 2).reshape(s, fw)
    head_lanes = lambda t: jnp.pad(t.reshape(heads, s).T, ((0, 0), (0, LANES - heads)))
    dsm_fox, d_fbias = _fox_gate_bwd(head_lanes(dcum_q), head_lanes(dcum_k), proj, small_blk, bias_lane)
    dqg, dkg, dvg, dz, dsm_gdn, d_alog, d_dt, d_gdnw = _gdn_bwd(
        qg, kg, vg, proj, 6, small_blk, a_lane, dt_lane, gdn_out_norm, states, tinvs, do_cat, 1)
    dxc, d_conv = _gdn_pre_bwd_act(proj, conv_all, dqg, dkg, dvg, heads)
    dqkv = _gdn_pre_bwd_conv(dxc, conv_all)
    dsmall = (dsm_fox + dsm_gdn).astype(BF16)
    dproj = jnp.concatenate([dq_f, dk_f, dv_f, dqkv, dz, dsmall, jnp.zeros((s, in_pad - 7 * fw - LANES), BF16)], axis=1)
    dh2 = _mm(dproj, win_al, tb=True, name="in_proj_bwd_x", tk=1536)
    dwin_al = _mm(h2, dproj, ta=True, out_dtype=BF16, name="in_proj_bwd_w", tm=2048, tn=1536, tk=1024)
    dwin_full = jnp.concatenate(
        [dwin_al[:, :o_f], dwin_al[:, 7 * fw:7 * fw + heads], dwin_al[:, o_f:o_f + 3 * fw],
         dwin_al[:, 7 * fw + heads:7 * fw + 3 * heads], dwin_al[:, 6 * fw:7 * fw]], axis=1)
    dwin_parts = dwin_full.reshape(d, N_DEV, in_w // N_DEV).transpose(1, 0, 2)
    dx1, dsh2, dsc2, dg2 = _norm_mod_bwd(x1, dh2, dx2, g_all[1:2], sc2, "norm_mod_2_bwd")

    df1, dgt1 = _gate_bwd(dx1, f1, gt1, MACARON_W, "ffn1_gate_bwd")
    (da1, db1, dwd1), (r_win,) = _ffn_bwd_act(df1, wd0, 0, a1, b1, s1, "ffn1_bwd_act", carry=([dwin_parts], True))
    (dwg1, dwu1), (r_wd1,) = _ffn_bwd_wgu(h1, da1, db1, "ffn1_bwd_wgu", carry=([dwd1], True))
    (dh1,), (r_wg1, r_wu1) = _ffn_bwd_h(da1, db1, wg0, wu0, 0, "ffn1_bwd_h", carry=([dwg1, dwu1], True))
    grad_x, dsh1, dsc1, dg1 = _norm_mod_bwd(x0, dh1, dx1, g_all[0:1], sc1, "norm_mod_1_bwd")

    dmod = jnp.concatenate([dsh1, dsc1, dgt1, dsh2, dsc2, dgt2, dsh3, dsc3, dgt3], axis=1)
    dmod_all = _gather_row(dmod, "gather_dmod")
    ct_pad = jnp.pad(c_all.T, ((0, 0), (0, LANES - N_DEV)))
    dmod_mine = jnp.pad(_my_cols(dmod_all, me, nada), ((0, LANES - N_DEV), (0, 0)))
    g_ada_w = _ada_bwd(ct_pad, dmod_mine)

    g_small_cols = [d_fbias, d_foxw, d_alog[:, heads:], d_dt[:, heads:], d_gdnw]
    small_part = jnp.concatenate(
        [_pad_lanes(v[:, :LANES]) for v in g_small_cols]
        + [d_final, dg1, dg2, dg3] + [d_conv[k:k + 1] for k in range(CONV_W)], axis=1)
    small_all = _gather_row(small_part, "gather_small_grads")
    off = 5 * LANES
    w_small = jnp.concatenate(
        [_pad_lanes(fox_f_bias), fox_out_norm, _pad_lanes(gdn_A_log), _pad_lanes(gdn_dt_bias), gdn_out_norm,
         final_norm.reshape(1, d)], axis=1)
    m_small = jnp.concatenate(
        [_pad_lanes(m_fox_f_bias), m_fox_out_norm, _pad_lanes(m_gdn_A_log), _pad_lanes(m_gdn_dt_bias),
         m_gdn_out_norm, m_final_norm.reshape(1, d)], axis=1)
    v_small = jnp.concatenate(
        [_pad_lanes(v_fox_f_bias), v_fox_out_norm, _pad_lanes(v_gdn_A_log), _pad_lanes(v_gdn_dt_bias),
         v_gdn_out_norm, v_final_norm.reshape(1, d)], axis=1)
    rep = _adamw(small_all[:, None, :off + d], w_small, m_small, v_small, "adamw_replicated")
    ab = _adamw(dmod_all[:, None, :], ada_b, m_ada_b, v_ada_b, "adamw_ada_b")
    g_ng = small_all[:, off + d:off + 4 * d].reshape(N_DEV, 3, d)
    ngs = _adamw(_my_cols(g_ng, me, ng), norm_g[0], m_norm_g[0], v_norm_g[0], "adamw_norm_g")
    g_cv = small_all[:, off + 4 * d:].reshape(N_DEV, CONV_W, 3 * fw)
    cvs = _adamw(_my_cols(g_cv, me, ncv), gdn_conv[0], m_gdn_conv[0], v_gdn_conv[0], "adamw_gdn_conv")

    wgs = _adamw_layers(r_wg1, r_wg2, ffn_w_gate, m_ffn_w_gate, v_ffn_w_gate, "adamw_w_gate")
    wus = _adamw_layers(r_wu1, r_wu2, ffn_w_up, m_ffn_w_up, v_ffn_w_up, "adamw_w_up")
    wds = _adamw_layers(r_wd1, r_wd2, ffn_w_down, m_ffn_w_down, v_ffn_w_down, "adamw_w_down")
    wis = [o[None] for o in _adamw(r_win, w_in[0], m_w_in[0], v_w_in[0], "adamw_w_in")]
    wos = [o[None] for o in _adamw(r_wout, w_out[0], m_w_out[0], v_w_out[0], "adamw_w_out")]
    adas = [o[None] for o in _adamw(g_ada_w[None], ada_w[0], m_ada_w[0], v_ada_w[0], "adamw_ada_w")]
    ngs = [o[None] for o in ngs]
    cvs = [o[None] for o in cvs]

    def rep_piece(i, lo, width):
        return rep[i][:, lo:lo + width]

    nh = fox_f_bias.shape[1]
    outs = []
    for i in range(4):
        outs.append([adas[i], ab[i], ngs[i], wgs[i], wus[i], wds[i], wis[i], wos[i],
                     rep_piece(i, 0, nh), rep_piece(i, LANES, HEAD_DIM), cvs[i], rep_piece(i, 2 * LANES, nh),
                     rep_piece(i, 3 * LANES, nh), rep_piece(i, 4 * LANES, HEAD_DIM), rep_piece(i, off, d).reshape(d)])
    return (loss, grad_x[None], *outs[0], *outs[1], *outs[2], *outs[3])
```

```python
import math

import numpy as np
import jax
import jax.numpy as jnp
from jax import lax
from jax.experimental import pallas as pl
from jax.experimental.pallas import tpu as pltpu

F32 = jnp.float32
BF16 = jnp.bfloat16

N_DEV = 8
MESH_AXES = ("x", "y", "c")
LANES = 128
HEAD_DIM = 128
GDN_CHUNK = 64
CONV_W = 4
N_MOD = 9
MACARON_W = 0.5
EPS = 1e-6
NEG = -1e30
VMEM_LIMIT_BYTES = 56 * 2 ** 20
FFN_DOWN_VMEM_BYTES = 62 * 2 ** 20
ADAM_BLOCK_BYTES = 4 * 2 ** 20
GDN_HP_WIDTH = 4
FOX_BWD_HEADS_PER_STEP = 4
FOX_HEADS_PER_STEP = 8

ADAM_LR = 0.001
ADAM_B1 = 0.9
ADAM_B2 = 0.999
ADAM_EPS = 1e-08
ADAM_WD = 0.01
ADAM_STEP = 10

MESH_ID = pl.DeviceIdType.MESH
ANY = pl.BlockSpec(memory_space=pl.ANY)
VMEM = pl.BlockSpec(memory_space=pltpu.VMEM)


def _pcall(body, **kw):
    return pl.pallas_call(body, **kw)


def _params(*semantics, vmem=VMEM_LIMIT_BYTES):
    return pltpu.CompilerParams(dimension_semantics=semantics, vmem_limit_bytes=vmem)


def _tile(n, pref):
    if n % pref == 0 and pref % 8 == 0:
        return pref
    t = 1 << (max(1, min(n, pref)).bit_length() - 1)
    while n % t:
        t //= 2
    return t if t % 8 == 0 else n


def _sigmoid(x):
    return 1.0 / (1.0 + jnp.exp(-x))


def _dot(a, b, dims):
    return lax.dot_general(a.astype(BF16), b.astype(BF16), (dims, ((), ())), preferred_element_type=F32)


NN = ((1,), (0,))
NT = ((1,), (1,))
TN = ((0,), (0,))


def _split3(x):
    hi = x.astype(BF16)
    r1 = x - hi.astype(F32)
    mid = r1.astype(BF16)
    lo = (r1 - mid.astype(F32)).astype(BF16)
    return hi, mid, lo


def _dot_exact_lhs(m_bf16, x, dims=NN):
    hi, mid, lo = _split3(x)
    d = lambda p: lax.dot_general(m_bf16, p, (dims, ((), ())), preferred_element_type=F32)
    return d(hi) + (d(mid) + d(lo))


def _dot_hp(a, b, dims):
    ah = a.astype(BF16)
    al = (a - ah.astype(F32)).astype(BF16)
    bh = b.astype(BF16)
    bl = (b - bh.astype(F32)).astype(BF16)
    d = lambda p, q: lax.dot_general(p, q, (dims, ((), ())), preferred_element_type=F32)
    return d(ah, bh) + (d(ah, bl) + d(al, bh))


def _mesh_pos():
    return lax.axis_index("x"), lax.axis_index("y"), lax.axis_index("c")


def _peer(pos, mask):
    x, y, c = pos
    return (1 - x if mask & 4 else x, 1 - y if mask & 2 else y, 1 - c if mask & 1 else c)


def _linear(pos):
    return 4 * pos[0] + 2 * pos[1] + pos[2]


def _exchange_copies(ins, outs, sems, scatter, with_receives=True):
    send_sems, recv_sems, local_sems = sems
    pos = _mesh_pos()
    me = _linear(pos)
    local, sends, recvs = [], [], []
    for i in range(len(ins)):
        src = ins[i].at[me] if scatter else ins[i]
        local.append(pltpu.make_async_copy(src, outs[i].at[me], local_sems.at[i]))
    for mask in range(1, N_DEV):
        peer = _peer(pos, mask)
        for i in range(len(ins)):
            sem = dict(send_sem=send_sems.at[i, mask - 1], recv_sem=recv_sems.at[i, mask - 1],
                       device_id=peer, device_id_type=MESH_ID)
            sends.append(pltpu.make_async_remote_copy(
                src_ref=ins[i].at[_linear(peer)] if scatter else ins[i], dst_ref=outs[i].at[me], **sem))
            if with_receives:
                recvs.append(pltpu.make_async_remote_copy(
                    src_ref=ins[i].at[me] if scatter else ins[i], dst_ref=outs[i].at[_linear(peer)], **sem))
    return local, sends, recvs


def _exchange_start(ins, outs, sems, scatter):
    local, sends, _ = _exchange_copies(ins, outs, sems, scatter, with_receives=False)
    for cp in local + sends:
        cp.start()


def _exchange_wait(ins, outs, sems, scatter):
    local, sends, recvs = _exchange_copies(ins, outs, sems, scatter)
    for cp in recvs:
        cp.wait_recv()
    for cp in sends:
        cp.wait_send()
    for cp in local:
        cp.wait()


def _exchange_sems(n):
    return [pltpu.SemaphoreType.DMA((n, N_DEV - 1)), pltpu.SemaphoreType.DMA((n, N_DEV - 1)),
            pltpu.SemaphoreType.DMA((n,))]


def _exchange_shapes(arrays, scatter):
    return [jax.ShapeDtypeStruct(a.shape if scatter else (N_DEV,) + a.shape, a.dtype) for a in arrays]


def _exchange(arrays, *, scatter, in_vmem, name):
    n = len(arrays)

    def body(*refs):
        ins, outs, sems = refs[:n], refs[n:2 * n], refs[2 * n:]
        _exchange_start(ins, outs, sems, scatter)
        _exchange_wait(ins, outs, sems, scatter)

    spec = VMEM if in_vmem else ANY
    outs = _pcall(
        body, name=name, out_shape=_exchange_shapes(arrays, scatter),
        in_specs=[spec] * n, out_specs=[spec] * n, scratch_shapes=_exchange_sems(n),
    )(*arrays)
    return list(outs)


def _gather_via_sibling(arrays, name):
    n = len(arrays)

    def body(*refs):
        ins, outs = refs[:n], refs[n:2 * n]
        send_sems, recv_sems, local_sems = refs[2 * n:]
        x, y, c = _mesh_pos()
        me, sibling = (x, y, c), (x, y, 1 - c)
        chips = [(1 - x, y), (x, 1 - y), (1 - x, 1 - y)]

        def copy(i, k, block, to, from_input=False):
            return pltpu.make_async_remote_copy(
                src_ref=ins[i] if from_input else outs[i].at[_linear(block)], dst_ref=outs[i].at[_linear(block)],
                send_sem=send_sems.at[i, k], recv_sem=recv_sems.at[i, k], device_id=to, device_id_type=MESH_ID)

        mine = [pltpu.make_async_copy(ins[i], outs[i].at[_linear(me)], local_sems.at[i]) for i in range(n)]
        first = []
        for i in range(n):
            first.append(copy(i, 0, me, sibling, True))
            first += [copy(i, 1 + j, me, (*chip, c), True) for j, chip in enumerate(chips)]
        for cp in mine + first:
            cp.start()
        passed = []
        for j, chip in enumerate(chips):
            for i in range(n):
                copy(i, 1 + j, (*chip, c), me).wait_recv()
                cp = copy(i, 4 + j, (*chip, c), sibling)
                cp.start()
                passed.append(cp)
        for i in range(n):
            copy(i, 0, sibling, me).wait_recv()
            for j, chip in enumerate(chips):
                copy(i, 4 + j, (*chip, 1 - c), me).wait_recv()
        for cp in first + passed:
            cp.wait_send()
        for cp in mine:
            cp.wait()

    outs = _pcall(
        body, name=name, out_shape=_exchange_shapes(arrays, False), in_specs=[ANY] * n, out_specs=[ANY] * n,
        scratch_shapes=_exchange_sems(n),
    )(*arrays)
    return list(outs)


def _hosted(body, *, name, grid, in_specs, out_specs, out_shape, args, scratch_shapes=(), prefetch=(), carry=None,
            vmem=VMEM_LIMIT_BYTES):
    n_in, n_out, n_scr, n_pre = len(in_specs), len(out_shape), len(scratch_shapes), len(prefetch)
    arrays, scatter = carry if carry is not None else ([], False)
    n = len(arrays)

    def wrapped(*refs):
        pre, r = refs[:n_pre], refs[n_pre:]
        host_in, comm_in = r[:n_in], r[n_in:n_in + n]
        r = r[n_in + n:]
        host_out, comm_out = r[:n_out], r[n_out:n_out + n]
        r = r[n_out + n:]
        host_scr, sems = r[:n_scr], r[n_scr:]
        if n:
            first = pl.program_id(0) == 0
            last = pl.program_id(0) == grid[0] - 1
            for ax in range(1, len(grid)):
                first = first & (pl.program_id(ax) == 0)
                last = last & (pl.program_id(ax) == grid[ax] - 1)

            @pl.when(first)
            def _():
                _exchange_start(comm_in, comm_out, sems, scatter)

        body(*pre, *host_in, *host_out, *host_scr)
        if n:
            @pl.when(last)
            def _():
                _exchange_wait(comm_in, comm_out, sems, scatter)

    grid_spec = pltpu.PrefetchScalarGridSpec(
        num_scalar_prefetch=n_pre, grid=grid, in_specs=list(in_specs) + [ANY] * n,
        out_specs=list(out_specs) + [ANY] * n,
        scratch_shapes=list(scratch_shapes) + (_exchange_sems(n) if n else []))
    outs = _pcall(
        wrapped, name=name, grid_spec=grid_spec, out_shape=list(out_shape) + _exchange_shapes(arrays, scatter),
        compiler_params=_params(*(["arbitrary"] * len(grid)), vmem=vmem),
    )(*prefetch, *args, *arrays)
    return list(outs[:n_out]), list(outs[n_out:])


def _gather_row(v, name):
    return _exchange([v], scatter=False, in_vmem=True, name=name)[0].reshape(N_DEV, v.shape[1])


def _ada_fwd(c_all, w, b):
    d, n = w.shape
    tn = _tile(n, 256)

    def body(c_ref, w_ref, b_ref, o_ref):
        cv = c_ref[...]
        cond = cv * _sigmoid(cv)
        o_ref[...] = _dot_hp(cond, w_ref[...], NN) + b_ref[...]

    return _pcall(
        body, name="ada_fwd", grid=(n // tn,),
        in_specs=[pl.BlockSpec((N_DEV, d), lambda j: (0, 0)), pl.BlockSpec((d, tn), lambda j: (0, j)),
                  pl.BlockSpec((1, tn), lambda j: (0, j))],
        out_specs=pl.BlockSpec((N_DEV, tn), lambda j: (0, j)),
        out_shape=jax.ShapeDtypeStruct((N_DEV, n), F32), compiler_params=_params("parallel"),
    )(c_all, w, b)


def _ada_bwd(ct_pad, dmod_pad):
    d = ct_pad.shape[0]
    n = dmod_pad.shape[1]
    tn = _tile(n, 256)

    def body(c_ref, g_ref, o_ref):
        cv = c_ref[...]
        cond = cv * _sigmoid(cv)
        o_ref[...] = _dot_hp(cond, g_ref[...], NN)

    return _pcall(
        body, name="ada_bwd", grid=(n // tn,),
        in_specs=[pl.BlockSpec((d, LANES), lambda j: (0, 0)), pl.BlockSpec((LANES, tn), lambda j: (0, j))],
        out_specs=pl.BlockSpec((d, tn), lambda j: (0, j)),
        out_shape=jax.ShapeDtypeStruct((d, n), F32), compiler_params=_params("parallel"),
    )(ct_pad, dmod_pad)


def _norm_mod(x, g, sc, sh, name):
    s, d = x.shape
    ts = _tile(s, 512)

    def body(x_ref, g_ref, sc_ref, sh_ref, h_ref):
        xv = x_ref[...]
        r = lax.rsqrt(jnp.mean(xv * xv, axis=-1, keepdims=True) + EPS)
        h_ref[...] = (xv * r * g_ref[...] * (1.0 + sc_ref[...]) + sh_ref[...]).astype(BF16)

    row = pl.BlockSpec((1, d), lambda i: (0, 0))
    return _pcall(
        body, name=name, grid=(s // ts,),
        in_specs=[pl.BlockSpec((ts, d), lambda i: (i, 0)), row, row, row],
        out_specs=pl.BlockSpec((ts, d), lambda i: (i, 0)),
        out_shape=jax.ShapeDtypeStruct((s, d), BF16), compiler_params=_params("parallel"),
    )(x, g, sc, sh)


def _norm_mod_bwd(x, dh, dx_out, g, sc, name):
    s, d = x.shape
    ts = _tile(s, 512)

    def body(x_ref, dh_ref, dxo_ref, g_ref, sc_ref, dx_ref, dsh_ref, dsc_ref, dg_ref):
        @pl.when(pl.program_id(0) == 0)
        def _():
            dsh_ref[...] = jnp.zeros_like(dsh_ref)
            dsc_ref[...] = jnp.zeros_like(dsc_ref)
            dg_ref[...] = jnp.zeros_like(dg_ref)

        xv = x_ref[...]
        dh_v = dh_ref[...]
        gv = g_ref[...]
        one_sc = 1.0 + sc_ref[...]
        r = lax.rsqrt(jnp.mean(xv * xv, axis=-1, keepdims=True) + EPS)
        xn = xv * r
        dxn = dh_v * (gv * one_sc)
        dx_ref[...] = dxo_ref[...] + r * (dxn - xn * jnp.mean(dxn * xn, axis=-1, keepdims=True))
        t = dh_v * xn
        dsh_ref[...] += jnp.sum(dh_v, axis=0, keepdims=True)
        dsc_ref[...] += jnp.sum(t * gv, axis=0, keepdims=True)
        dg_ref[...] += jnp.sum(t * one_sc, axis=0, keepdims=True)

    blk = pl.BlockSpec((ts, d), lambda i: (i, 0))
    row = pl.BlockSpec((1, d), lambda i: (0, 0))
    return _pcall(
        body, name=name, grid=(s // ts,),
        in_specs=[blk, blk, blk, row, row], out_specs=[blk, row, row, row],
        out_shape=[jax.ShapeDtypeStruct((s, d), F32)] + [jax.ShapeDtypeStruct((1, d), F32)] * 3,
        compiler_params=_params("arbitrary"),
    )(x, dh, dx_out, g, sc)


def _gate_bwd(dx, f, gt, k, name):
    s, d = dx.shape
    ts = _tile(s, 512)

    def body(dx_ref, f_ref, gt_ref, df_ref, dgt_ref):
        @pl.when(pl.program_id(0) == 0)
        def _():
            dgt_ref[...] = jnp.zeros_like(dgt_ref)

        dxv = dx_ref[...]
        df_ref[...] = ((k * gt_ref[...]) * dxv).astype(BF16)
        dgt_ref[...] += k * jnp.sum(f_ref[...].astype(F32) * dxv, axis=0, keepdims=True)

    blk = pl.BlockSpec((ts, d), lambda i: (i, 0))
    row = pl.BlockSpec((1, d), lambda i: (0, 0))
    return _pcall(
        body, name=name, grid=(s // ts,),
        in_specs=[blk, blk, row], out_specs=[blk, row],
        out_shape=[jax.ShapeDtypeStruct((s, d), BF16), jax.ShapeDtypeStruct((1, d), F32)],
        compiler_params=_params("arbitrary"),
    )(dx, f, gt)


def _ffn_up(h, wg, wu, layer, name, carry=None):
    s, d = h.shape
    fs = wg.shape[-1]
    tm = _tile(s, 1024)

    def body(h_ref, wg_ref, wu_ref, a_ref, b_ref, s_ref):
        hv = h_ref[...]
        a = jnp.dot(hv, wg_ref[...], preferred_element_type=F32)
        b = jnp.dot(hv, wu_ref[...], preferred_element_type=F32)
        a_ref[...] = a.astype(BF16)
        b_ref[...] = b.astype(BF16)
        s_ref[...] = (a * _sigmoid(a) * b).astype(BF16)

    wspec = pl.BlockSpec((None, None, d, fs), lambda j, m: (j, layer, 0, 0))
    ospec = pl.BlockSpec((None, tm, fs), lambda j, m: (j, m, 0))
    return _hosted(
        body, name=name, grid=(N_DEV, s // tm),
        in_specs=[pl.BlockSpec((tm, d), lambda j, m: (m, 0)), wspec, wspec],
        out_specs=[ospec, ospec, ospec],
        out_shape=[jax.ShapeDtypeStruct((N_DEV, s, fs), BF16)] * 3,
        args=(h, wg, wu), carry=carry)


def _ffn_down(sv, wd, layer, x_in, gt, name, carry=None):
    _, s, fs = sv.shape
    d = wd.shape[-1]
    tm = _tile(s, 1024)

    def body(s_ref, wd_ref, x_ref, gt_ref, f_ref, xo_ref, acc):
        j = pl.program_id(1)

        @pl.when(j == 0)
        def _():
            acc[...] = jnp.zeros_like(acc)

        acc[...] += jnp.dot(s_ref[...], wd_ref[...], preferred_element_type=F32)

        @pl.when(j == N_DEV - 1)
        def _():
            fv = acc[...]
            f_ref[...] = fv.astype(BF16)
            xo_ref[...] = x_ref[...] + (MACARON_W * gt_ref[...]) * fv

    blk = pl.BlockSpec((tm, d), lambda m, j: (m, 0))
    return _hosted(
        body, name=name, grid=(s // tm, N_DEV),
        in_specs=[pl.BlockSpec((None, tm, fs), lambda m, j: (j, m, 0)),
                  pl.BlockSpec((None, None, fs, d), lambda m, j: (j, layer, 0, 0)),
                  blk, pl.BlockSpec((1, d), lambda m, j: (0, 0))],
        out_specs=[blk, blk],
        out_shape=[jax.ShapeDtypeStruct((s, d), BF16), jax.ShapeDtypeStruct((s, d), F32)],
        scratch_shapes=[pltpu.VMEM((tm, d), F32)],
        args=(sv, wd, x_in, gt), carry=carry, vmem=FFN_DOWN_VMEM_BYTES)


def _ffn_bwd_act(df, wd, layer, a, b, sv, name, carry=None):
    s, d = df.shape
    fs = a.shape[-1]
    tm = _tile(s, 1024)
    nm = s // tm

    def body(df_ref, wd_ref, a_ref, b_ref, s_ref, da_ref, db_ref, dwd_ref, acc):
        @pl.when(pl.program_id(1) == 0)
        def _():
            acc[...] = jnp.zeros_like(acc)

        dfv = df_ref[...]
        ds = lax.dot_general(dfv, wd_ref[...], (NT, ((), ())), preferred_element_type=F32)
        av = a_ref[...].astype(F32)
        sg = _sigmoid(av)
        da_ref[...] = (ds * b_ref[...].astype(F32) * (sg * (1.0 + av * (1.0 - sg)))).astype(BF16)
        db_ref[...] = (ds * (av * sg)).astype(BF16)
        acc[...] += lax.dot_general(s_ref[...], dfv, (TN, ((), ())), preferred_element_type=F32)

        @pl.when(pl.program_id(1) == nm - 1)
        def _():
            dwd_ref[...] = acc[...].astype(BF16)

    hid = pl.BlockSpec((None, tm, fs), lambda j, m: (j, m, 0))
    return _hosted(
        body, name=name, grid=(N_DEV, nm),
        in_specs=[pl.BlockSpec((tm, d), lambda j, m: (m, 0)),
                  pl.BlockSpec((None, None, fs, d), lambda j, m: (j, layer, 0, 0)), hid, hid, hid],
        out_specs=[hid, hid, pl.BlockSpec((None, fs, d), lambda j, m: (j, 0, 0))],
        out_shape=[jax.ShapeDtypeStruct((N_DEV, s, fs), BF16)] * 2 + [jax.ShapeDtypeStruct((N_DEV, fs, d), BF16)],
        scratch_shapes=[pltpu.VMEM((fs, d), F32)],
        args=(df, wd, a, b, sv), carry=carry)


def _ffn_bwd_h(da, db, wg, wu, layer, name, carry=None):
    _, s, fs = da.shape
    d = wg.shape[-2]
    tm = _tile(s, 1024)

    def body(da_ref, db_ref, wg_ref, wu_ref, o_ref, acc):
        j = pl.program_id(1)

        @pl.when(j == 0)
        def _():
            acc[...] = jnp.zeros_like(acc)

        acc[...] += (lax.dot_general(da_ref[...], wg_ref[...], (NT, ((), ())), preferred_element_type=F32)
                     + lax.dot_general(db_ref[...], wu_ref[...], (NT, ((), ())), preferred_element_type=F32))

        @pl.when(j == N_DEV - 1)
        def _():
            o_ref[...] = acc[...]

    hid = pl.BlockSpec((None, tm, fs), lambda m, j: (j, m, 0))
    wspec = pl.BlockSpec((None, None, d, fs), lambda m, j: (j, layer, 0, 0))
    return _hosted(
        body, name=name, grid=(s // tm, N_DEV),
        in_specs=[hid, hid, wspec, wspec],
        out_specs=[pl.BlockSpec((tm, d), lambda m, j: (m, 0))],
        out_shape=[jax.ShapeDtypeStruct((s, d), F32)],
        scratch_shapes=[pltpu.VMEM((tm, d), F32)],
        args=(da, db, wg, wu), carry=carry)


def _ffn_bwd_wgu(h, da, db, name, carry=None):
    s, d = h.shape
    fs = da.shape[-1]
    tk = _tile(s, 1024)
    nk = s // tk

    def body(h_ref, da_ref, db_ref, og_ref, ou_ref, accg, accu):
        @pl.when(pl.program_id(1) == 0)
        def _():
            accg[...] = jnp.zeros_like(accg)
            accu[...] = jnp.zeros_like(accu)

        hv = h_ref[...]
        accg[...] += lax.dot_general(hv, da_ref[...], (TN, ((), ())), preferred_element_type=F32)
        accu[...] += lax.dot_general(hv, db_ref[...], (TN, ((), ())), preferred_element_type=F32)

        @pl.when(pl.program_id(1) == nk - 1)
        def _():
            og_ref[...] = accg[...].astype(BF16)
            ou_ref[...] = accu[...].astype(BF16)

    hid = pl.BlockSpec((None, tk, fs), lambda j, k: (j, k, 0))
    ospec = pl.BlockSpec((None, d, fs), lambda j, k: (j, 0, 0))
    return _hosted(
        body, name=name, grid=(N_DEV, nk),
        in_specs=[pl.BlockSpec((tk, d), lambda j, k: (k, 0)), hid, hid],
        out_specs=[ospec, ospec],
        out_shape=[jax.ShapeDtypeStruct((N_DEV, d, fs), BF16)] * 2,
        scratch_shapes=[pltpu.VMEM((d, fs), F32), pltpu.VMEM((d, fs), F32)],
        args=(h, da, db), carry=carry)


def _mm(a, b, *, ta=False, tb=False, out_dtype=F32, name, tm=1024, tn=1024, tk=2048, residual=None):
    m, kdim = (a.shape[1], a.shape[0]) if ta else a.shape
    n = b.shape[0] if tb else b.shape[1]
    tm, tn, tk = _tile(m, tm), _tile(n, tn), _tile(kdim, tk)
    nk = kdim // tk
    dims = ((0,) if ta else (1,), (1,) if tb else (0,))

    def body(*refs):
        a_ref, b_ref = refs[:2]
        acc = refs[-1]
        kk = pl.program_id(2)

        @pl.when(kk == 0)
        def _():
            acc[...] = jnp.zeros_like(acc)

        acc[...] += lax.dot_general(a_ref[...].astype(BF16), b_ref[...].astype(BF16), (dims, ((), ())),
                                    preferred_element_type=F32)

        @pl.when(kk == nk - 1)
        def _():
            if residual is None:
                refs[2][...] = acc[...].astype(out_dtype)
            else:
                res_ref, gate_ref, y_ref, xo_ref = refs[2:6]
                yv = acc[...]
                y_ref[...] = yv
                xo_ref[...] = res_ref[...] + gate_ref[...] * yv

    a_spec = pl.BlockSpec((tk, tm), lambda i, j, k: (k, i)) if ta else pl.BlockSpec((tm, tk), lambda i, j, k: (i, k))
    b_spec = pl.BlockSpec((tn, tk), lambda i, j, k: (j, k)) if tb else pl.BlockSpec((tk, tn), lambda i, j, k: (k, j))
    o_spec = pl.BlockSpec((tm, tn), lambda i, j, k: (i, j))
    if residual is None:
        in_specs, out_specs = [a_spec, b_spec], o_spec
        out_shape = jax.ShapeDtypeStruct((m, n), out_dtype)
        args = (a, b)
    else:
        in_specs = [a_spec, b_spec, o_spec, pl.BlockSpec((1, tn), lambda i, j, k: (0, j))]
        out_specs = [o_spec, o_spec]
        out_shape = [jax.ShapeDtypeStruct((m, n), F32)] * 2
        args = (a, b) + tuple(residual)
    return _pcall(
        body, name=name, grid=(m // tm, n // tn, nk), in_specs=in_specs, out_specs=out_specs, out_shape=out_shape,
        scratch_shapes=[pltpu.VMEM((tm, tn), F32)],
        compiler_params=_params("parallel", "parallel", "arbitrary"),
    )(*args)


def _log_sigmoid(z):
    return jnp.minimum(z, 0.0) - jnp.log(1.0 + jnp.exp(-jnp.abs(z)))


def _fox_gate(proj, small_blk, bias_lane):
    s = proj.shape[0]
    ts = _tile(s, 1024)
    nsub = ts // LANES

    def body(z_ref, b_ref, cum_ref, carry):
        @pl.when(pl.program_id(0) == 0)
        def _():
            carry[...] = jnp.zeros_like(carry)

        ii = lax.broadcasted_iota(jnp.int32, (LANES, LANES), 0)
        jj = lax.broadcasted_iota(jnp.int32, (LANES, LANES), 1)
        tri = (ii >= jj).astype(BF16)
        logf = _log_sigmoid(z_ref[...] + b_ref[...])
        cv = carry[...]
        for sb in range(nsub):
            blk = logf[sb * LANES:(sb + 1) * LANES, :]
            cum_ref[sb * LANES:(sb + 1) * LANES, :] = _dot_exact_lhs(tri, blk) + cv
            cv = cv + jnp.sum(blk, axis=0, keepdims=True)
        carry[...] = cv

    return _pcall(
        body, name="fox_gate", grid=(s // ts,),
        in_specs=[pl.BlockSpec((ts, LANES), lambda i: (i, small_blk)), pl.BlockSpec((1, LANES), lambda i: (0, 0))],
        out_specs=pl.BlockSpec((ts, LANES), lambda i: (i, 0)),
        out_shape=jax.ShapeDtypeStruct((s, LANES), F32),
        scratch_shapes=[pltpu.VMEM((1, LANES), F32)],
        compiler_params=_params("arbitrary"),
    )(proj, bias_lane)


def _fox_gate_bwd(dcum_q, dcum_k, proj, small_blk, bias_lane):
    s = proj.shape[0]
    ts = _tile(s, 1024)
    nsub = ts // LANES
    nb = s // ts

    def body(dcq_ref, dc_ref, z_ref, b_ref, dz_ref, db_ref, carry):
        @pl.when(pl.program_id(0) == 0)
        def _():
            carry[...] = jnp.zeros_like(carry)
            db_ref[...] = jnp.zeros_like(db_ref)

        ii = lax.broadcasted_iota(jnp.int32, (LANES, LANES), 0)
        jj = lax.broadcasted_iota(jnp.int32, (LANES, LANES), 1)
        triu = (jj >= ii).astype(BF16)
        dc = dcq_ref[...] + dc_ref[...]
        zb = z_ref[...] + b_ref[...]
        cv = carry[...]
        dbv = jnp.zeros((1, LANES), F32)
        for sb in reversed(range(nsub)):
            rows = slice(sb * LANES, (sb + 1) * LANES)
            blk = dc[rows, :]
            dlogf = _dot_exact_lhs(triu, blk) + cv
            cv = cv + jnp.sum(blk, axis=0, keepdims=True)
            dz = dlogf * _sigmoid(-zb[rows, :])
            dz_ref[rows, :] = dz
            dbv = dbv + jnp.sum(dz, axis=0, keepdims=True)
        carry[...] = cv
        db_ref[...] += dbv

    row = pl.BlockSpec((1, LANES), lambda i: (0, 0))
    return _pcall(
        body, name="fox_gate_bwd", grid=(nb,),
        in_specs=[pl.BlockSpec((ts, LANES), lambda i: (nb - 1 - i, 0)),
                  pl.BlockSpec((ts, LANES), lambda i: (nb - 1 - i, 0)),
                  pl.BlockSpec((ts, LANES), lambda i: (nb - 1 - i, small_blk)), row],
        out_specs=[pl.BlockSpec((ts, LANES), lambda i: (nb - 1 - i, 0)), row],
        out_shape=[jax.ShapeDtypeStruct((s, LANES), F32), jax.ShapeDtypeStruct((1, LANES), F32)],
        scratch_shapes=[pltpu.VMEM((1, LANES), F32)],
        compiler_params=_params("arbitrary"),
    )(dcum_q, dcum_k, proj, bias_lane)


def _tri_tables(n, by_key):
    if by_key:
        pairs = [(i, j) for j in range(n) for i in range(j, n)]
    else:
        pairs = [(i, j) for i in range(n) for j in range(i + 1)]
    return (jnp.asarray(np.array([p[0] for p in pairs], np.int32)),
            jnp.asarray(np.array([p[1] for p in pairs], np.int32)))


def _fox_group(heads):
    return FOX_HEADS_PER_STEP if heads % FOX_HEADS_PER_STEP == 0 else 1


def _as_row(col):
    t = col.shape[0]
    eye = lax.broadcasted_iota(jnp.int32, (t, t), 0) == lax.broadcasted_iota(jnp.int32, (t, t), 1)
    return jnp.sum(jnp.where(eye, col, 0.0), axis=0, keepdims=True)


LOG2E = 1.4426950408889634
FOX_Q_SCALE = LOG2E / math.sqrt(HEAD_DIM)


def _fox_scores(a, b, bias_col, bias_row, diagonal, rows_are_keys=False):
    sc = lax.dot_general(a.astype(BF16), b.astype(BF16), (NT, ((), ())), preferred_element_type=F32)
    sc = sc + (bias_col + bias_row)
    if not diagonal:
        return sc
    row = lax.broadcasted_iota(jnp.int32, sc.shape, 0)
    col = lax.broadcasted_iota(jnp.int32, sc.shape, 1)
    return jnp.where(row <= col if rows_are_keys else col <= row, sc, NEG)


def _fox_fwd(proj, cum_col, cum_row, w_norm, heads, carry=None):
    s = proj.shape[0]
    t = _tile(s, 512)
    grp = _fox_group(heads)
    qi, ki = _tri_tables(s // t, False)
    scale = 1.0 / math.sqrt(HEAD_DIM)

    def body(qi_ref, ki_ref, q_ref, k_ref, v_ref, cq_ref, ck_ref, w_ref, o_ref, lse_ref, lser_ref, on_ref, m_s, acc_s):
        iq, ik = qi_ref[pl.program_id(1)], ki_ref[pl.program_id(1)]

        @pl.when(ik == 0)
        def _():
            m_s[...] = jnp.full_like(m_s, NEG)
            acc_s[...] = jnp.zeros_like(acc_s)

        def step(diagonal):
            for g in range(grp):
                sl = slice(g * HEAD_DIM, (g + 1) * HEAD_DIM)
                qs = (q_ref[:, sl] * FOX_Q_SCALE).astype(BF16)
                sc = _fox_scores(qs, k_ref[:, sl], cq_ref[g, :, 0:1] * LOG2E, ck_ref[g] * (-LOG2E), diagonal)
                m_prev = m_s[g]
                m_new = jnp.maximum(m_prev, jnp.max(sc, axis=1, keepdims=True))
                p = jnp.exp2(sc - m_new).astype(BF16)
                v_ones = jnp.concatenate([v_ref[:, sl].astype(BF16), jnp.ones((t, LANES), BF16)], axis=1)
                acc_s[g] = jnp.exp2(m_prev - m_new) * acc_s[g] + jnp.dot(p, v_ones, preferred_element_type=F32)
                m_s[g] = m_new

        @pl.when(ik < iq)
        def _():
            step(False)

        @pl.when(ik == iq)
        def _():
            step(True)
            for g in range(grp):
                sl = slice(g * HEAD_DIM, (g + 1) * HEAD_DIM)
                acc = acc_s[g]
                o = acc[:, :HEAD_DIM] / acc[:, HEAD_DIM:]
                lse = m_s[g] + jnp.log(acc[:, HEAD_DIM:]) * LOG2E
                o_ref[:, sl] = o
                lse_ref[g] = lse
                lser_ref[g] = _as_row(lse[:, 0:1])
                r = lax.rsqrt(jnp.mean(o * o, axis=1, keepdims=True) + EPS)
                on_ref[:, sl] = (o * r * w_ref[...]).astype(BF16)

    ng = heads // grp
    qblk = pl.BlockSpec((t, grp * HEAD_DIM), lambda h, p, qi, ki: (qi[p], h))
    kblk = lambda off: pl.BlockSpec((t, grp * HEAD_DIM), lambda h, p, qi, ki: (ki[p], off + h))
    qcol = pl.BlockSpec((grp, t, LANES), lambda h, p, qi, ki: (h, qi[p], 0))
    return _hosted(
        body, name="fox_fwd", grid=(ng, int(qi.shape[0])), prefetch=(qi, ki),
        in_specs=[qblk, kblk(ng), kblk(2 * ng), qcol,
                  pl.BlockSpec((grp, 1, t), lambda h, p, qi, ki: (h, 0, ki[p])),
                  pl.BlockSpec((1, HEAD_DIM), lambda h, p, qi, ki: (0, 0))],
        out_specs=[qblk, qcol, pl.BlockSpec((grp, 1, t), lambda h, p, qi, ki: (h, 0, qi[p])), qblk],
        scratch_shapes=[pltpu.VMEM((grp, t, 1), F32), pltpu.VMEM((grp, t, 2 * HEAD_DIM), F32)],
        out_shape=[jax.ShapeDtypeStruct((s, heads * HEAD_DIM), F32), jax.ShapeDtypeStruct((heads, s, LANES), F32),
                   jax.ShapeDtypeStruct((heads, 1, s), F32), jax.ShapeDtypeStruct((s, heads * HEAD_DIM), BF16)],
        args=(proj, proj, proj, cum_col, cum_row, w_norm), carry=carry)


def _fox_prep_bwd(do_cat, o_raw, w_norm, heads):
    s = o_raw.shape[0]
    ts = _tile(s, 512)

    def body(g_ref, o_ref, w_ref, do_ref, delta_ref, deltar_ref, dw_ref):
        @pl.when((pl.program_id(0) == 0) & (pl.program_id(1) == 0))
        def _():
            dw_ref[...] = jnp.zeros_like(dw_ref)

        o = o_ref[...]
        g = g_ref[...]
        r = lax.rsqrt(jnp.mean(o * o, axis=1, keepdims=True) + EPS)
        wg = g * w_ref[...]
        do = r * wg - o * (r * r * r) * jnp.mean(wg * o, axis=1, keepdims=True)
        do_ref[...] = do.astype(BF16)
        delta = jnp.sum(do * o, axis=1, keepdims=True)
        delta_ref[...] = jnp.broadcast_to(delta, delta_ref.shape)
        deltar_ref[...] = _as_row(delta)
        dw_ref[...] += jnp.sum(g * o * r, axis=0, keepdims=True)

    blk = pl.BlockSpec((ts, HEAD_DIM), lambda h, i: (i, h))
    row = pl.BlockSpec((1, HEAD_DIM), lambda h, i: (0, 0))
    return _pcall(
        body, name="fox_prep_bwd", grid=(heads, s // ts),
        in_specs=[blk, blk, row],
        out_specs=[blk, pl.BlockSpec((None, ts, LANES), lambda h, i: (h, i, 0)),
                   pl.BlockSpec((None, 1, ts), lambda h, i: (h, 0, i)), row],
        out_shape=[jax.ShapeDtypeStruct((s, heads * HEAD_DIM), BF16), jax.ShapeDtypeStruct((heads, s, LANES), F32),
                   jax.ShapeDtypeStruct((heads, 1, s), F32), jax.ShapeDtypeStruct((1, HEAD_DIM), F32)],
        compiler_params=_params("arbitrary", "arbitrary"),
    )(do_cat, o_raw, w_norm)


def _fox_bwd(proj, do, cum_col, cum_row, lse_row, delta_row, heads, carry=None):
    s = proj.shape[0]
    t = _tile(s, 512)
    nk = s // t
    grp = FOX_BWD_HEADS_PER_STEP if heads % FOX_BWD_HEADS_PER_STEP == 0 else 1
    qi, ki = _tri_tables(nk, True)
    npairs = int(qi.shape[0])
    scale = 1.0 / math.sqrt(HEAD_DIM)

    def body(qi_ref, ki_ref, q_ref, k_ref, v_ref, do_ref, cqr_ref, ckc_ref, lse_ref, dl_ref,
             dk_ref, dv_ref, dck_ref, dq_hbm, dcq_ref, dk_acc, dv_acc, dck_acc, dq_acc, stage, sem):
        pair = pl.program_id(1)
        iq, ik = qi_ref[pair], ki_ref[pair]
        rows_q = pl.ds(pl.multiple_of(iq * t, t), t)

        @pl.when(pair == 0)
        def _():
            dq_acc[...] = jnp.zeros_like(dq_acc)
            dcq_ref[...] = jnp.zeros_like(dcq_ref)

        def step(diagonal):
            for g in range(grp):
                sl = slice(g * HEAD_DIM, (g + 1) * HEAD_DIM)
                qs = (q_ref[:, sl] * FOX_Q_SCALE).astype(BF16)
                kv = k_ref[:, sl]
                dov = do_ref[:, sl]
                st = _fox_scores(kv, qs, ckc_ref[g, :, 0:1] * (-LOG2E), cqr_ref[g] * LOG2E - lse_ref[g], diagonal, True)
                pt = jnp.exp2(st)
                dv_acc[g] += _dot(pt, dov, NN)
                dpt = _dot(v_ref[:, sl], dov, NT)
                dst = pt * (dpt - dl_ref[g])
                dk_acc[g] += _dot(dst, qs, NN)
                dck_acc[g] += jnp.sum(dst, axis=1, keepdims=True)
                dq_acc[g, rows_q, :] += _dot(dst, kv, TN)
                dcq_ref[g, iq] += jnp.sum(dst, axis=0, keepdims=True)

        @pl.when(iq == ik)
        def _():
            dk_acc[...] = jnp.zeros_like(dk_acc)
            dv_acc[...] = jnp.zeros_like(dv_acc)
            dck_acc[...] = jnp.zeros_like(dck_acc)
            step(True)

        @pl.when(iq > ik)
        def _():
            step(False)

        @pl.when(iq == nk - 1)
        def _():
            for g in range(grp):
                sl = slice(g * HEAD_DIM, (g + 1) * HEAD_DIM)
                dk_ref[:, sl] = (dk_acc[g] * (1.0 / LOG2E)).astype(BF16)
                dv_ref[:, sl] = dv_acc[g].astype(BF16)
                dck_ref[g] = _as_row(-dck_acc[g])

        @pl.when(pair == npairs - 1)
        def _():
            for g in range(grp):
                head = pl.program_id(0) * grp + g

                def flush(i, c):
                    rows = pl.ds(pl.multiple_of(i * t, t), t)
                    stage[...] = (dq_acc[g, rows, :] * scale).astype(BF16)
                    cp = pltpu.make_async_copy(stage, dq_hbm.at[head, rows, :], sem)
                    cp.start()
                    cp.wait()
                    return c

                lax.fori_loop(0, nk, flush, 0)

    ng = heads // grp
    qblk = pl.BlockSpec((t, grp * HEAD_DIM), lambda h, p, qi, ki: (qi[p], h))
    qrow = pl.BlockSpec((grp, 1, t), lambda h, p, qi, ki: (h, 0, qi[p]))
    kblk = lambda off: pl.BlockSpec((t, grp * HEAD_DIM), lambda h, p, qi, ki: (ki[p], off + h))
    kout = pl.BlockSpec((t, grp * HEAD_DIM), lambda h, p, qi, ki: (ki[p], h))
    return _hosted(
        body, name="fox_bwd", grid=(ng, npairs), prefetch=(qi, ki),
        in_specs=[qblk, kblk(ng), kblk(2 * ng), qblk, qrow,
                  pl.BlockSpec((grp, t, LANES), lambda h, p, qi, ki: (h, ki[p], 0)), qrow, qrow],
        out_specs=[kout, kout, pl.BlockSpec((grp, 1, t), lambda h, p, qi, ki: (h, 0, ki[p])), ANY,
                   pl.BlockSpec((grp, nk, 1, t), lambda h, p, qi, ki: (h, 0, 0, 0))],
        scratch_shapes=[pltpu.VMEM((grp, t, HEAD_DIM), F32), pltpu.VMEM((grp, t, HEAD_DIM), F32),
                        pltpu.VMEM((grp, t, 1), F32), pltpu.VMEM((grp, s, HEAD_DIM), F32),
                        pltpu.VMEM((t, HEAD_DIM), BF16), pltpu.SemaphoreType.DMA],
        out_shape=[jax.ShapeDtypeStruct((s, heads * HEAD_DIM), BF16)] * 2 + [jax.ShapeDtypeStruct((heads, 1, s), F32)]
        + [jax.ShapeDtypeStruct((heads, s, HEAD_DIM), BF16), jax.ShapeDtypeStruct((heads, nk, 1, t), F32)],
        args=(proj, proj, proj, do, cum_row, cum_col, lse_row, delta_row), carry=carry)


def _shift_rows(xv, halo, j, forward):
    n = xv.shape[0]
    rid = lax.broadcasted_iota(jnp.int32, (8, xv.shape[1]), 0)
    if forward:
        xs = pltpu.roll(xv, n - j, 0)
        hs = pltpu.roll(halo, 8 - j, 0)
        edge = jnp.where(rid >= 8 - j, hs, xs[n - 8:, :])
        return jnp.concatenate([xs[:n - 8, :], edge], axis=0)
    xs = pltpu.roll(xv, j, 0)
    hs = pltpu.roll(halo, j, 0)
    edge = jnp.where(rid < j, hs, xs[:8, :])
    return jnp.concatenate([edge, xs[8:, :]], axis=0)


def _conv_silu(xv, halo, w):
    xc = w[CONV_W - 1:CONV_W, :] * xv
    for j in range(1, CONV_W):
        xc = xc + w[CONV_W - 1 - j:CONV_W - j, :] * _shift_rows(xv, halo, j, False)
    return xc, xc * _sigmoid(xc)


def _gdn_pre(proj, conv_w, heads):
    s = proj.shape[0]
    cw = 3 * heads * HEAD_DIM
    ts = _tile(s, 256)
    tb = ts // 8

    def body(x_ref, halo_ref, w_ref, q_ref, k_ref, v_ref):
        halo = jnp.where(pl.program_id(0) == 0, 0.0, halo_ref[...])
        _, y = _conv_silu(x_ref[...], halo, w_ref[...])
        for h in range(heads):
            for part, ref in enumerate((q_ref, k_ref, v_ref)):
                c0 = (part * heads + h) * HEAD_DIM
                blk = y[:, c0:c0 + HEAD_DIM]
                if part < 2:
                    blk = blk * lax.rsqrt(jnp.sum(blk * blk, axis=1, keepdims=True) + EPS)
                ref[h] = blk

    out = pl.BlockSpec((heads, ts, HEAD_DIM), lambda i: (0, i, 0))
    return _pcall(
        body, name="gdn_pre", grid=(s // ts,),
        in_specs=[pl.BlockSpec((ts, cw), lambda i: (i, 1)),
                  pl.BlockSpec((8, cw), lambda i: (jnp.maximum(i * tb - 1, 0), 1)),
                  pl.BlockSpec((CONV_W, cw), lambda i: (0, 0))],
        out_specs=[out, out, out],
        out_shape=[jax.ShapeDtypeStruct((heads, s, HEAD_DIM), F32)] * 3,
        compiler_params=_params("parallel"),
    )(proj, proj, conv_w)


def _gdn_pre_bwd_act(proj, conv_w, dq, dk, dv, heads):
    s = proj.shape[0]
    cw = 3 * heads * HEAD_DIM
    ts = _tile(s, 256)
    tb = ts // 8

    def body(x_ref, halo_ref, w_ref, dq_ref, dk_ref, dv_ref, dxc_ref, dw_ref):
        @pl.when(pl.program_id(0) == 0)
        def _():
            dw_ref[...] = jnp.zeros_like(dw_ref)

        xv = x_ref[...]
        halo = jnp.where(pl.program_id(0) == 0, 0.0, halo_ref[...])
        xc, y = _conv_silu(xv, halo, w_ref[...])
        sg = _sigmoid(xc)
        dsilu = sg * (1.0 + xc * (1.0 - sg))
        for h in range(heads):
            for part, ref in enumerate((dq_ref, dk_ref, dv_ref)):
                c0 = (part * heads + h) * HEAD_DIM
                g = ref[h]
                if part < 2:
                    blk = y[:, c0:c0 + HEAD_DIM]
                    r = lax.rsqrt(jnp.sum(blk * blk, axis=1, keepdims=True) + EPS)
                    g = r * g - blk * (r * r * r) * jnp.sum(g * blk, axis=1, keepdims=True)
                dxc_ref[:, c0:c0 + HEAD_DIM] = g * dsilu[:, c0:c0 + HEAD_DIM]
        dxc = dxc_ref[...]
        rows = [jnp.sum(dxc * (xv if j == 0 else _shift_rows(xv, halo, j, False)), axis=0, keepdims=True)
                for j in range(CONV_W)]
        dw_ref[...] += jnp.concatenate([rows[CONV_W - 1 - k] for k in range(CONV_W)]
                                       + [jnp.zeros((8 - CONV_W, cw), F32)], axis=0)

    hblk = pl.BlockSpec((heads, ts, HEAD_DIM), lambda i: (0, i, 0))
    return _pcall(
        body, name="gdn_pre_bwd_act", grid=(s // ts,),
        in_specs=[pl.BlockSpec((ts, cw), lambda i: (i, 1)),
                  pl.BlockSpec((8, cw), lambda i: (jnp.maximum(i * tb - 1, 0), 1)),
                  pl.BlockSpec((CONV_W, cw), lambda i: (0, 0)), hblk, hblk, hblk],
        out_specs=[pl.BlockSpec((ts, cw), lambda i: (i, 0)), pl.BlockSpec((8, cw), lambda i: (0, 0))],
        out_shape=[jax.ShapeDtypeStruct((s, cw), F32), jax.ShapeDtypeStruct((8, cw), F32)],
        compiler_params=_params("arbitrary"),
    )(proj, proj, conv_w, dq, dk, dv)


def _gdn_pre_bwd_conv(dxc, conv_w):
    s, cw = dxc.shape
    ts = _tile(s, 256)
    tb = ts // 8
    last = s // 8 - 1

    def body(g_ref, halo_ref, w_ref, dx_ref):
        gv = g_ref[...]
        w = w_ref[...]
        halo = jnp.where(pl.program_id(0) == s // ts - 1, 0.0, halo_ref[...])
        dx = w[CONV_W - 1:CONV_W, :] * gv
        for j in range(1, CONV_W):
            dx = dx + w[CONV_W - 1 - j:CONV_W - j, :] * _shift_rows(gv, halo, j, True)
        dx_ref[...] = dx.astype(BF16)

    return _pcall(
        body, name="gdn_pre_bwd_conv", grid=(s // ts,),
        in_specs=[pl.BlockSpec((ts, cw), lambda i: (i, 0)),
                  pl.BlockSpec((8, cw), lambda i: (jnp.minimum((i + 1) * tb, last), 0)),
                  pl.BlockSpec((CONV_W, cw), lambda i: (0, 0))],
        out_specs=pl.BlockSpec((ts, cw), lambda i: (i, 0)),
        out_shape=jax.ShapeDtypeStruct((s, cw), BF16),
        compiler_params=_params("parallel"),
    )(dxc, dxc, conv_w)


def _bdot(a, b, ca, cb):
    return lax.dot_general(a.astype(BF16), b.astype(BF16), (((ca,), (cb,)), ((0,), (0,))),
                           preferred_element_type=F32)


def _bdot_hp(a, b, ca, cb):
    ah = a.astype(BF16)
    al = (a - ah.astype(F32)).astype(BF16)
    bh = b.astype(BF16)
    bl = (b - bh.astype(F32)).astype(BF16)
    d = lambda p, q: lax.dot_general(p, q, (((ca,), (cb,)), ((0,), (0,))), preferred_element_type=F32)
    return d(ah, bh) + (d(ah, bl) + d(al, bh))


def _gdn_gates(small, a_lane, dt_lane, heads):
    lane = lax.broadcasted_iota(jnp.int32, small.shape, 1)
    za = small + dt_lane
    g_all = -jnp.exp(a_lane) * (jnp.maximum(za, 0.0) + jnp.log(1.0 + jnp.exp(-jnp.abs(za))))
    b_all = _sigmoid(small)
    pick = lambda v, l: jnp.sum(jnp.where(lane == l, v, 0.0), axis=1, keepdims=True)
    g = jnp.stack([pick(g_all, heads + h) for h in range(heads)], axis=0)
    beta = jnp.stack([pick(b_all, 2 * heads + h) for h in range(heads)], axis=0)
    return g, beta


def _chunk_masks(c):
    ii = lax.broadcasted_iota(jnp.int32, (1, c, c), 1)
    jj = lax.broadcasted_iota(jnp.int32, (1, c, c), 2)
    return ii >= jj, ii > jj, ii == jj


def _col_to_row(col, eye):
    return jnp.sum(jnp.where(eye, col, 0.0), axis=1, keepdims=True)


def _row_to_col(row, eye):
    return jnp.sum(jnp.where(eye, row, 0.0), axis=2, keepdims=True)


def _gdn_chunk(q, k, v, g, beta, state, tinv=None):
    c = q.shape[1]
    incl, strict, eye = _chunk_masks(c)
    g_row = _col_to_row(g, eye)
    gc_col = jnp.sum(jnp.where(incl, g_row, 0.0), axis=2, keepdims=True)
    gc_row = _col_to_row(gc_col, eye)
    gam = jnp.where(incl, jnp.exp(jnp.where(incl, gc_col - gc_row, NEG)), 0.0)
    egc = jnp.exp(gc_col)
    kb = k * beta
    vb = v * beta
    kbe = kb * egc
    low = jnp.where(strict, _bdot(kb, k, 2, 2), 0.0) * gam
    if tinv is None:
        p = -low
        tinv = jnp.where(eye, 1.0, 0.0) + p
        width = 2
        while width < c:
            dot = _bdot_hp if width <= GDN_HP_WIDTH else _bdot
            p = dot(p, p, 2, 1)
            tinv = tinv + dot(tinv, p, 2, 1)
            width *= 2
    u = _bdot(tinv, vb, 2, 1)
    w = _bdot(tinv, kbe, 2, 1)
    att = jnp.where(incl, _bdot(q, k, 2, 2), 0.0) * gam
    vn = u - _bdot(w, state, 2, 1)
    qe = q * egc
    o = _bdot(qe, state, 2, 1) + _bdot(att, vn, 2, 1)
    gl = jnp.sum(g, axis=1, keepdims=True)
    edec = jnp.exp(gl - gc_col)
    kdec = k * edec
    egl = jnp.exp(gl)
    new_state = state * egl + _bdot(kdec, vn, 1, 1)
    return dict(incl=incl, strict=strict, eye=eye, gam=gam, egc=egc, kb=kb, vb=vb, kbe=kbe, low=low, tinv=tinv, w=w,
                att=att, vn=vn, qe=qe, o=o, edec=edec, kdec=kdec, egl=egl, new_state=new_state)


def _gdn_load(q_ref, k_ref, v_ref, small_ref, a_ref, dt_ref, rows, heads):
    q = q_ref[:, rows, :] * (HEAD_DIM ** -0.5)
    g, beta = _gdn_gates(small_ref[rows, :], a_ref[...], dt_ref[...], heads)
    return q, k_ref[:, rows, :], v_ref[:, rows, :], g, beta


def _gdn_fwd(q, k, v, proj, z_blk, small_blk, a_lane, dt_lane, w_norm):
    heads, s, _ = q.shape
    c = min(GDN_CHUNK, s)
    r = _tile(s, 512)
    npb = r // c
    gw = heads * HEAD_DIM

    def body(q_ref, k_ref, v_ref, z_ref, small_ref, a_ref, dt_ref, w_ref, o_ref, st_ref, ti_ref, state):
        @pl.when(pl.program_id(0) == 0)
        def _():
            state[...] = jnp.zeros_like(state)

        def chunk(cb, carry):
            rows = pl.ds(pl.multiple_of(cb * c, c), c)
            qv, kv, vv, g, beta = _gdn_load(q_ref, k_ref, v_ref, small_ref, a_ref, dt_ref, rows, heads)
            st = state[...]
            st_ref[:, cb] = st
            res = _gdn_chunk(qv, kv, vv, g, beta, st)
            ti_ref[:, cb] = res["tinv"]
            state[...] = res["new_state"]
            o = res["o"]
            rn = lax.rsqrt(jnp.mean(o * o, axis=2, keepdims=True) + EPS)
            zv = z_ref[rows, :]
            for h in range(heads):
                zh = zv[:, h * HEAD_DIM:(h + 1) * HEAD_DIM]
                o_ref[rows, h * HEAD_DIM:(h + 1) * HEAD_DIM] = (
                    o[h] * rn[h] * w_ref[...] * (zh * _sigmoid(zh))).astype(BF16)
            return carry

        lax.fori_loop(0, npb, chunk, 0)

    hblk = pl.BlockSpec((heads, r, HEAD_DIM), lambda i: (0, i, 0))
    row = pl.BlockSpec((1, LANES), lambda i: (0, 0))
    return _pcall(
        body, name="gdn_fwd", grid=(s // r,),
        in_specs=[hblk, hblk, hblk, pl.BlockSpec((r, gw), lambda i: (i, z_blk)),
                  pl.BlockSpec((r, LANES), lambda i: (i, small_blk)), row, row, row],
        out_specs=[pl.BlockSpec((r, gw), lambda i: (i, 0)),
                   pl.BlockSpec((heads, npb, HEAD_DIM, HEAD_DIM), lambda i: (0, i, 0, 0)),
                   pl.BlockSpec((heads, npb, c, c), lambda i: (0, i, 0, 0))],
        out_shape=[jax.ShapeDtypeStruct((s, gw), BF16),
                   jax.ShapeDtypeStruct((heads, s // c, HEAD_DIM, HEAD_DIM), F32),
                   jax.ShapeDtypeStruct((heads, s // c, c, c), F32)],
        scratch_shapes=[pltpu.VMEM((heads, HEAD_DIM, HEAD_DIM), F32)],
        compiler_params=_params("arbitrary"),
    )(q, k, v, proj, proj, a_lane, dt_lane, w_norm)


def _gdn_bwd(q, k, v, proj, z_blk, small_blk, a_lane, dt_lane, w_norm, states, tinvs, do_cat, do_blk):
    heads, s, _ = q.shape
    c = min(GDN_CHUNK, s)
    r = _tile(s, 512)
    npb = r // c
    nb = s // r
    gw = heads * HEAD_DIM

    def body(q_ref, k_ref, v_ref, z_ref, small_ref, a_ref, dt_ref, w_ref, st_ref, ti_ref, do_ref,
             dq_ref, dk_ref, dv_ref, dz_ref, dsm_ref, da_ref, ddt_ref, dw_ref, dstate):
        @pl.when(pl.program_id(0) == 0)
        def _():
            dstate[...] = jnp.zeros_like(dstate)
            da_ref[...] = jnp.zeros_like(da_ref)
            ddt_ref[...] = jnp.zeros_like(ddt_ref)
            dw_ref[...] = jnp.zeros_like(dw_ref)

        def chunk(it, carry):
            cb = npb - 1 - it
            rows = pl.ds(pl.multiple_of(cb * c, c), c)
            qv, kv, vv, g, beta = _gdn_load(q_ref, k_ref, v_ref, small_ref, a_ref, dt_ref, rows, heads)
            st = st_ref[:, cb]
            f = _gdn_chunk(qv, kv, vv, g, beta, st, tinv=ti_ref[:, cb])
            incl, strict, eye = f["incl"], f["strict"], f["eye"]
            o = f["o"]
            wv = w_ref[...]
            zv = z_ref[rows, :]
            dov = do_ref[rows, :]
            rn = lax.rsqrt(jnp.mean(o * o, axis=2, keepdims=True) + EPS)
            do_l, dw_acc = [], jnp.zeros((1, HEAD_DIM), F32)
            for h in range(heads):
                sl = slice(h * HEAD_DIM, (h + 1) * HEAD_DIM)
                zh, gh = zv[:, sl], dov[:, sl]
                sg = _sigmoid(zh)
                on = o[h] * rn[h]
                dz_ref[rows, sl] = (gh * (on * wv) * (sg * (1.0 + zh * (1.0 - sg)))).astype(BF16)
                gn = gh * (zh * sg)
                dw_acc = dw_acc + jnp.sum(gn * on, axis=0, keepdims=True)
                wg = gn * wv
                do_l.append(rn[h] * wg - o[h] * (rn[h] * rn[h] * rn[h]) * jnp.mean(wg * o[h], axis=1, keepdims=True))
            dw_ref[...] += dw_acc
            do = jnp.stack(do_l, axis=0)
            ds_out = dstate[...]
            dvn = _bdot(f["att"], do, 1, 1) + _bdot(f["kdec"], ds_out, 2, 1)
            datt = jnp.where(incl, _bdot(do, f["vn"], 2, 2), 0.0)
            dqe = _bdot(do, st, 2, 2)
            dstate[...] = _bdot(f["qe"], do, 1, 1) + f["egl"] * ds_out - _bdot(f["w"], dvn, 1, 1)
            dw = -_bdot(dvn, st, 2, 2)
            dkdec = _bdot(f["vn"], ds_out, 2, 2)
            t_kdec = jnp.sum(dkdec * f["kdec"], axis=2, keepdims=True)
            dgl = (jnp.sum(jnp.sum(st * ds_out, axis=2, keepdims=True), axis=1, keepdims=True) * f["egl"]
                   + jnp.sum(t_kdec, axis=1, keepdims=True))
            dgc = jnp.sum(dqe * f["qe"], axis=2, keepdims=True) - t_kdec
            dq = dqe * f["egc"]
            dk = dkdec * f["edec"]
            dtinv = _bdot(dvn, f["vb"], 2, 2) + _bdot(dw, f["kbe"], 2, 2)
            dvb = _bdot(f["tinv"], dvn, 1, 1)
            dkbe = _bdot(f["tinv"], dw, 1, 1)
            dkb = dkbe * f["egc"]
            dgc = dgc + jnp.sum(dkbe * f["kbe"], axis=2, keepdims=True)
            dlow = jnp.where(strict, -_bdot_hp(_bdot_hp(f["tinv"], dtinv, 1, 1), f["tinv"], 2, 2), 0.0)
            ml = dlow * f["gam"]
            dkb = dkb + _bdot(ml, kv, 2, 1)
            dk = dk + _bdot(ml, f["kb"], 1, 1)
            ma = datt * f["gam"]
            dq = dq + _bdot(ma, kv, 2, 1)
            dk = dk + _bdot(ma, qv, 1, 1)
            e = dlow * f["low"] + datt * f["att"]
            dgc = dgc + jnp.sum(e, axis=2, keepdims=True) - _row_to_col(jnp.sum(e, axis=1, keepdims=True), eye)
            dk = dk + beta * dkb
            dbeta = jnp.sum(dkb * kv, axis=2, keepdims=True) + jnp.sum(dvb * vv, axis=2, keepdims=True)
            dgc_row = _col_to_row(dgc, eye)
            dg = jnp.sum(jnp.where(incl, 0.0, dgc_row) + jnp.where(eye, dgc_row, 0.0), axis=2, keepdims=True) + dgl
            dq_ref[:, rows, :] = dq * (HEAD_DIM ** -0.5)
            dk_ref[:, rows, :] = dk
            dv_ref[:, rows, :] = beta * dvb
            small = small_ref[rows, :]
            lane = lax.broadcasted_iota(jnp.int32, small.shape, 1)
            dg_l = jnp.zeros(small.shape, F32)
            db_l = jnp.zeros(small.shape, F32)
            for h in range(heads):
                dg_l = dg_l + jnp.where(lane == heads + h, dg[h], 0.0)
                db_l = db_l + jnp.where(lane == 2 * heads + h, dbeta[h], 0.0)
            za = small + dt_ref[...]
            nexp = -jnp.exp(a_ref[...])
            softplus = jnp.maximum(za, 0.0) + jnp.log(1.0 + jnp.exp(-jnp.abs(za)))
            da_logit = dg_l * nexp * _sigmoid(za)
            sb = _sigmoid(small)
            dsm_ref[rows, :] = da_logit + db_l * sb * (1.0 - sb)
            ddt_ref[...] += jnp.sum(da_logit, axis=0, keepdims=True)
            da_ref[...] += jnp.sum(dg_l * nexp * softplus, axis=0, keepdims=True)
            return carry

        lax.fori_loop(0, npb, chunk, 0)

    rev = lambda i: nb - 1 - i
    hblk = pl.BlockSpec((heads, r, HEAD_DIM), lambda i: (0, rev(i), 0))
    row = pl.BlockSpec((1, LANES), lambda i: (0, 0))
    wide = lambda blk: pl.BlockSpec((r, gw), lambda i: (rev(i), blk))
    return _pcall(
        body, name="gdn_bwd", grid=(nb,),
        in_specs=[hblk, hblk, hblk, wide(z_blk), pl.BlockSpec((r, LANES), lambda i: (rev(i), small_blk)),
                  row, row, row, pl.BlockSpec((heads, npb, HEAD_DIM, HEAD_DIM), lambda i: (0, rev(i), 0, 0)),
                  pl.BlockSpec((heads, npb, c, c), lambda i: (0, rev(i), 0, 0)), wide(do_blk)],
        out_specs=[hblk, hblk, hblk, wide(0), pl.BlockSpec((r, LANES), lambda i: (rev(i), 0)), row, row, row],
        out_shape=[jax.ShapeDtypeStruct((heads, s, HEAD_DIM), F32)] * 3
        + [jax.ShapeDtypeStruct((s, gw), BF16), jax.ShapeDtypeStruct((s, LANES), F32)]
        + [jax.ShapeDtypeStruct((1, LANES), F32)] * 3,
        scratch_shapes=[pltpu.VMEM((heads, HEAD_DIM, HEAD_DIM), F32)],
        compiler_params=_params("arbitrary"),
    )(q, k, v, proj, proj, a_lane, dt_lane, w_norm, states, tinvs, do_cat)


def _final(x, target, gf):
    s, d = x.shape
    ts = _tile(s, 512)

    def body(x_ref, t_ref, g_ref, loss_ref, dx_ref, dg_ref):
        @pl.when(pl.program_id(0) == 0)
        def _():
            loss_ref[...] = jnp.zeros_like(loss_ref)
            dg_ref[...] = jnp.zeros_like(dg_ref)

        xv = x_ref[...]
        gv = g_ref[...]
        r = lax.rsqrt(jnp.mean(xv * xv, axis=-1, keepdims=True) + EPS)
        xn = xv * r
        err = xn * gv - t_ref[...]
        per_tok = jnp.mean(err * err, axis=-1, keepdims=True)
        loss_ref[...] += 0.5 * jnp.sum(per_tok, axis=0, keepdims=True)
        dy = err * (1.0 / d)
        dg_ref[...] += jnp.sum(dy * xn, axis=0, keepdims=True)
        dxn = dy * gv
        dx_ref[...] = r * (dxn - xn * jnp.mean(dxn * xn, axis=-1, keepdims=True))

    blk = pl.BlockSpec((ts, d), lambda i: (i, 0))
    row = pl.BlockSpec((1, d), lambda i: (0, 0))
    return _pcall(
        body, name="final_loss", grid=(s // ts,),
        in_specs=[blk, blk, row], out_specs=[pl.BlockSpec((1, LANES), lambda i: (0, 0)), blk, row],
        out_shape=[jax.ShapeDtypeStruct((1, LANES), F32), jax.ShapeDtypeStruct((s, d), F32),
                   jax.ShapeDtypeStruct((1, d), F32)],
        compiler_params=_params("arbitrary"),
    )(x, target, gf)


def _adamw(parts, w, m, v, name):
    npart, rows, cols = parts.shape
    tr = _tile(rows, max(8, ADAM_BLOCK_BYTES // (4 * npart * cols)))
    c1 = 1.0 - ADAM_B1 ** ADAM_STEP
    c2 = 1.0 - ADAM_B2 ** ADAM_STEP

    def body(p_ref, w_ref, m_ref, v_ref, g_ref, d_ref, mo_ref, vo_ref):
        g = p_ref[0].astype(F32)
        for i in range(1, npart):
            g = g + p_ref[i].astype(F32)
        mn = ADAM_B1 * m_ref[...] + (1.0 - ADAM_B1) * g
        vn = ADAM_B2 * v_ref[...] + (1.0 - ADAM_B2) * (g * g)
        g_ref[...] = g
        mo_ref[...] = mn
        vo_ref[...] = vn
        d_ref[...] = -ADAM_LR * ((mn / c1) / (jnp.sqrt(vn / c2) + ADAM_EPS) + ADAM_WD * w_ref[...])

    blk = pl.BlockSpec((tr, cols), lambda i: (i, 0))
    return _pcall(
        body, name=name, grid=(rows // tr,),
        in_specs=[pl.BlockSpec((npart, tr, cols), lambda i: (0, i, 0)), blk, blk, blk],
        out_specs=[blk] * 4, out_shape=[jax.ShapeDtypeStruct((rows, cols), F32)] * 4,
        compiler_params=_params("parallel"),
    )(parts, w, m, v)


def _adamw_layers(parts0, parts1, w, m, v, name):
    npart, rows, cols = parts0.shape
    tr = _tile(rows, max(8, ADAM_BLOCK_BYTES // (4 * npart * cols)))
    nb = rows // tr
    c1 = 1.0 - ADAM_B1 ** ADAM_STEP
    c2 = 1.0 - ADAM_B2 ** ADAM_STEP

    def body(p0_ref, p1_ref, w_ref, m_ref, v_ref, g_ref, d_ref, mo_ref, vo_ref):
        def update(p_ref):
            g = p_ref[0].astype(F32)
            for i in range(1, npart):
                g = g + p_ref[i].astype(F32)
            mn = ADAM_B1 * m_ref[...] + (1.0 - ADAM_B1) * g
            vn = ADAM_B2 * v_ref[...] + (1.0 - ADAM_B2) * (g * g)
            g_ref[...] = g
            mo_ref[...] = mn
            vo_ref[...] = vn
            d_ref[...] = -ADAM_LR * ((mn / c1) / (jnp.sqrt(vn / c2) + ADAM_EPS) + ADAM_WD * w_ref[...])

        @pl.when(pl.program_id(0) == 0)
        def _():
            update(p0_ref)

        @pl.when(pl.program_id(0) == 1)
        def _():
            update(p1_ref)

    blk = pl.BlockSpec((None, None, tr, cols), lambda l, i: (0, l, i, 0))
    p0 = pl.BlockSpec((npart, tr, cols), lambda l, i: (0, jnp.where(l == 0, i, nb - 1), 0))
    p1 = pl.BlockSpec((npart, tr, cols), lambda l, i: (0, jnp.where(l == 0, 0, i), 0))
    return _pcall(
        body, name=name, grid=(2, nb), in_specs=[p0, p1, blk, blk, blk], out_specs=[blk] * 4,
        out_shape=[jax.ShapeDtypeStruct(w.shape, F32)] * 4, compiler_params=_params("arbitrary", "arbitrary"),
    )(parts0, parts1, w, m, v)


def _pad_lanes(v, n=LANES, at=0):
    return jnp.pad(v, ((0, 0), (at, n - at - v.shape[1])))


def _my_cols(a, me, width):
    return lax.dynamic_slice_in_dim(a, me * width, width, axis=a.ndim - 1)


def kernel(x, c, ada_w, ada_b, norm_g, ffn_w_gate, ffn_w_up, ffn_w_down, w_in, w_out, fox_f_bias, fox_out_norm, gdn_conv, gdn_A_log, gdn_dt_bias, gdn_out_norm, final_norm, loss_target, m_ada_w, m_ada_b, m_norm_g, m_ffn_w_gate, m_ffn_w_up, m_ffn_w_down, m_w_in, m_w_out, m_fox_f_bias, m_fox_out_norm, m_gdn_conv, m_gdn_A_log, m_gdn_dt_bias, m_gdn_out_norm, m_final_norm, v_ada_w, v_ada_b, v_norm_g, v_ffn_w_gate, v_ffn_w_up, v_ffn_w_down, v_w_in, v_w_out, v_fox_f_bias, v_fox_out_norm, v_gdn_conv, v_gdn_A_log, v_gdn_dt_bias, v_gdn_out_norm, v_final_norm):
    me = _linear(_mesh_pos())
    x0 = x[0]
    s, d = x0.shape
    heads = d // (2 * HEAD_DIM)
    fw = heads * HEAD_DIM
    ng = norm_g.shape[-1]
    ncv = gdn_conv.shape[-1]
    nada = ada_w.shape[-1]
    in_w = w_in.shape[-1] * N_DEV
    in_pad = -(-in_w // 512) * 512

    pack = jnp.concatenate([c, norm_g[0].reshape(1, 3 * ng), gdn_conv[0].reshape(1, CONV_W * ncv)], axis=1)
    pack_all = _gather_row(pack, "gather_small_params")
    c_all = pack_all[:, :d]
    g_all = pack_all[:, d:d + 3 * ng].reshape(N_DEV, 3, ng).transpose(1, 0, 2).reshape(3, d)
    conv_all = pack_all[:, d + 3 * ng:].reshape(N_DEV, CONV_W, ncv).transpose(1, 0, 2).reshape(CONV_W, 3 * fw)

    mod_blk = _ada_fwd(c_all, ada_w[0], _my_cols(ada_b, me, nada))
    mod_all = _exchange([mod_blk], scatter=False, in_vmem=True, name="gather_mod")[0]
    mod = lax.dynamic_slice_in_dim(mod_all, me, 1, axis=1).reshape(N_MOD, d)
    sh1, sc1, gt1, sh2, sc2, gt2, sh3, sc3, gt3 = [mod[i:i + 1] for i in range(N_MOD)]

    wg_sh, wu_sh, wd_sh = [w[0].astype(BF16) for w in (ffn_w_gate, ffn_w_up, ffn_w_down)]
    layer = lambda w, i: w[i:i + 1]
    wg0, wu0 = _gather_via_sibling([layer(wg_sh, 0), layer(wu_sh, 0)], "gather_ffn1_up_weights")
    small_blk = 7 * heads

    bias_lane = _pad_lanes(fox_f_bias)
    a_lane = _pad_lanes(gdn_A_log, at=heads)
    dt_lane = _pad_lanes(gdn_dt_bias, at=heads)

    h1 = _norm_mod(x0, g_all[0:1], sc1, sh1, "norm_mod_1")
    (a1, b1, s1), (wd0, wout_g) = _ffn_up(h1, wg0, wu0, 0, "ffn1_up",
                                          carry=([layer(wd_sh, 0), w_out[0].astype(BF16)], False))
    (f1, x1), (win_g,) = _ffn_down(s1, wd0, 0, x0, gt1, "ffn1_down", carry=([w_in[0].astype(BF16)], False))
    win_full = win_g.transpose(1, 0, 2).reshape(d, in_w)
    o_f, o_qkv, o_a, o_z = 3 * fw, 3 * fw + heads, 6 * fw + heads, 6 * fw + 3 * heads
    win_al = jnp.concatenate(
        [win_full[:, :o_f], win_full[:, o_qkv:o_a], win_full[:, o_z:], win_full[:, o_f:o_qkv],
         win_full[:, o_a:o_z], jnp.zeros((d, in_pad - in_w), BF16)], axis=1)
    wout_full = wout_g.reshape(d, d)

    h2 = _norm_mod(x1, g_all[1:2], sc2, sh2, "norm_mod_2")
    proj = _mm(h2, win_al, name="in_proj", tn=1536)
    cum = _fox_gate(proj, small_blk, bias_lane)
    cum_t = cum[:, :heads].T
    cum_row = cum_t[:, None, :]
    cum_col = jnp.broadcast_to(cum_t[:, :, None], (heads, s, LANES))
    (o_raw, lse, lse_row, o_fox), (wg1, wu1, wd1) = _fox_fwd(
        proj, cum_col, cum_row, fox_out_norm, heads,
        carry=([layer(wg_sh, 1), layer(wu_sh, 1), layer(wd_sh, 1)], False))
    qg, kg, vg = _gdn_pre(proj, conv_all, heads)
    o_gdn, states, tinvs = _gdn_fwd(qg, kg, vg, proj, 6, small_blk, a_lane, dt_lane, gdn_out_norm)
    o_cat = jnp.concatenate([o_fox, o_gdn], axis=1)
    mix, x2 = _mm(o_cat, wout_full, name="out_proj", residual=(x1, gt2))

    h3 = _norm_mod(x2, g_all[2:3], sc3, sh3, "norm_mod_3")
    (a3, b3, s3), _ = _ffn_up(h3, wg1, wu1, 0, "ffn2_up")
    (f3, x3), _ = _ffn_down(s3, wd1, 0, x2, gt3, "ffn2_down")

    loss_row, dx3, d_final = _final(x3, loss_target[0], final_norm.reshape(1, d))
    loss = lax.psum(loss_row[0, 0], MESH_AXES)

    df3, dgt3 = _gate_bwd(dx3, f3, gt3, MACARON_W, "ffn2_gate_bwd")
    (da3, db3, dwd2), _ = _ffn_bwd_act(df3, wd1, 0, a3, b3, s3, "ffn2_bwd_act")
    (dh3,), (r_wd2,) = _ffn_bwd_h(da3, db3, wg1, wu1, 0, "ffn2_bwd_h", carry=([dwd2], True))
    (dwg2, dwu2), _ = _ffn_bwd_wgu(h3, da3, db3, "ffn2_bwd_wgu")
    dx2, dsh3, dsc3, dg3 = _norm_mod_bwd(x2, dh3, dx3, g_all[2:3], sc3, "norm_mod_3_bwd")

    dmix, dgt2 = _gate_bwd(dx2, mix, gt2, 1.0, "mix_gate_bwd")
    do_cat = _mm(dmix, wout_full, tb=True, name="out_proj_bwd_x")
    dwout = _mm(o_cat, dmix, ta=True, out_dtype=BF16, name="out_proj_bwd_w", tk=512)
    do_fox, delta, delta_row, d_foxw = _fox_prep_bwd(do_cat, o_raw, fox_out_norm, heads)
    (dk_f, dv_f, dcum_k, dq_heads, dcum_q), (r_wg2, r_wu2, r_wout) = _fox_bwd(
        proj, do_fox, cum_col, cum_row, lse_row, delta_row, heads,
        carry=([dwg2, dwu2, dwout.reshape(N_DEV, d // N_DEV, d)], True))
    dq_f = dq_heads.transpose(1, 0, 2).reshape(s, fw)
    head_lanes = lambda t: jnp.pad(t.reshape(heads, s).T, ((0, 0), (0, LANES - heads)))
    dsm_fox, d_fbias = _fox_gate_bwd(head_lanes(dcum_q), head_lanes(dcum_k), proj, small_blk, bias_lane)
    dqg, dkg, dvg, dz, dsm_gdn, d_alog, d_dt, d_gdnw = _gdn_bwd(
        qg, kg, vg, proj, 6, small_blk, a_lane, dt_lane, gdn_out_norm, states, tinvs, do_cat, 1)
    dxc, d_conv = _gdn_pre_bwd_act(proj, conv_all, dqg, dkg, dvg, heads)
    dqkv = _gdn_pre_bwd_conv(dxc, conv_all)
    dsmall = (dsm_fox + dsm_gdn).astype(BF16)
    dproj = jnp.concatenate([dq_f, dk_f, dv_f, dqkv, dz, dsmall, jnp.zeros((s, in_pad - 7 * fw - LANES), BF16)], axis=1)
    dh2 = _mm(dproj, win_al, tb=True, name="in_proj_bwd_x", tk=1536)
    dwin_al = _mm(h2, dproj, ta=True, out_dtype=BF16, name="in_proj_bwd_w", tm=2048, tn=1536, tk=1024)
    dwin_full = jnp.concatenate(
        [dwin_al[:, :o_f], dwin_al[:, 7 * fw:7 * fw + heads], dwin_al[:, o_f:o_f + 3 * fw],
         dwin_al[:, 7 * fw + heads:7 * fw + 3 * heads], dwin_al[:, 6 * fw:7 * fw]], axis=1)
    dwin_parts = dwin_full.reshape(d, N_DEV, in_w // N_DEV).transpose(1, 0, 2)
    dx1, dsh2, dsc2, dg2 = _norm_mod_bwd(x1, dh2, dx2, g_all[1:2], sc2, "norm_mod_2_bwd")

    df1, dgt1 = _gate_bwd(dx1, f1, gt1, MACARON_W, "ffn1_gate_bwd")
    (da1, db1, dwd1), (r_win,) = _ffn_bwd_act(df1, wd0, 0, a1, b1, s1, "ffn1_bwd_act", carry=([dwin_parts], True))
    (dwg1, dwu1), (r_wd1,) = _ffn_bwd_wgu(h1, da1, db1, "ffn1_bwd_wgu", carry=([dwd1], True))
    (dh1,), (r_wg1, r_wu1) = _ffn_bwd_h(da1, db1, wg0, wu0, 0, "ffn1_bwd_h", carry=([dwg1, dwu1], True))
    grad_x, dsh1, dsc1, dg1 = _norm_mod_bwd(x0, dh1, dx1, g_all[0:1], sc1, "norm_mod_1_bwd")

    dmod = jnp.concatenate([dsh1, dsc1, dgt1, dsh2, dsc2, dgt2, dsh3, dsc3, dgt3], axis=1)
    dmod_all = _gather_row(dmod, "gather_dmod")
    ct_pad = jnp.pad(c_all.T, ((0, 0), (0, LANES - N_DEV)))
    dmod_mine = jnp.pad(_my_cols(dmod_all, me, nada), ((0, LANES - N_DEV), (0, 0)))
    g_ada_w = _ada_bwd(ct_pad, dmod_mine)

    g_small_cols = [d_fbias, d_foxw, d_alog[:, heads:], d_dt[:, heads:], d_gdnw]
    small_part = jnp.concatenate(
        [_pad_lanes(v[:, :LANES]) for v in g_small_cols]
        + [d_final, dg1, dg2, dg3] + [d_conv[k:k + 1] for k in range(CONV_W)], axis=1)
    small_all = _gather_row(small_part, "gather_small_grads")
    off = 5 * LANES
    w_small = jnp.concatenate(
        [_pad_lanes(fox_f_bias), fox_out_norm, _pad_lanes(gdn_A_log), _pad_lanes(gdn_dt_bias), gdn_out_norm,
         final_norm.reshape(1, d)], axis=1)
    m_small = jnp.concatenate(
        [_pad_lanes(m_fox_f_bias), m_fox_out_norm, _pad_lanes(m_gdn_A_log), _pad_lanes(m_gdn_dt_bias),
         m_gdn_out_norm, m_final_norm.reshape(1, d)], axis=1)
    v_small = jnp.concatenate(
        [_pad_lanes(v_fox_f_bias), v_fox_out_norm, _pad_lanes(v_gdn_A_log), _pad_lanes(v_gdn_dt_bias),
         v_gdn_out_norm, v_final_norm.reshape(1, d)], axis=1)
    rep = _adamw(small_all[:, None, :off + d], w_small, m_small, v_small, "adamw_replicated")
    ab = _adamw(dmod_all[:, None, :], ada_b, m_ada_b, v_ada_b, "adamw_ada_b")
    g_ng = small_all[:, off + d:off + 4 * d].reshape(N_DEV, 3, d)
    ngs = _adamw(_my_cols(g_ng, me, ng), norm_g[0], m_norm_g[0], v_norm_g[0], "adamw_norm_g")
    g_cv = small_all[:, off + 4 * d:].reshape(N_DEV, CONV_W, 3 * fw)
    cvs = _adamw(_my_cols(g_cv, me, ncv), gdn_conv[0], m_gdn_conv[0], v_gdn_conv[0], "adamw_gdn_conv")

    wgs = _adamw_layers(r_wg1, r_wg2, ffn_w_gate, m_ffn_w_gate, v_ffn_w_gate, "adamw_w_gate")
    wus = _adamw_layers(r_wu1, r_wu2, ffn_w_up, m_ffn_w_up, v_ffn_w_up, "adamw_w_up")
    wds = _adamw_layers(r_wd1, r_wd2, ffn_w_down, m_ffn_w_down, v_ffn_w_down, "adamw_w_down")
    wis = [o[None] for o in _adamw(r_win, w_in[0], m_w_in[0], v_w_in[0], "adamw_w_in")]
    wos = [o[None] for o in _adamw(r_wout, w_out[0], m_w_out[0], v_w_out[0], "adamw_w_out")]
    adas = [o[None] for o in _adamw(g_ada_w[None], ada_w[0], m_ada_w[0], v_ada_w[0], "adamw_ada_w")]
    ngs = [o[None] for o in ngs]
    cvs = [o[None] for o in cvs]

    def rep_piece(i, lo, width):
        return rep[i][:, lo:lo + width]

    nh = fox_f_bias.shape[1]
    outs = []
    for i in range(4):
        outs.append([adas[i], ab[i], ngs[i], wgs[i], wus[i], wds[i], wis[i], wos[i],
                     rep_piece(i, 0, nh), rep_piece(i, LANES, HEAD_DIM), cvs[i], rep_piece(i, 2 * LANES, nh),
                     rep_piece(i, 3 * LANES, nh), rep_piece(i, 4 * LANES, HEAD_DIM), rep_piece(i, off, d).reshape(d)])
    return (loss, grad_x[None], *outs[0], *outs[1], *outs[2], *outs[3])
```
